```python
import jax, jax.numpy as jnp
from jax import lax
import numpy as np

D_MODEL = 1024
BATCH = 8
SEQ = 16384
DEPTH = 1

PLE_DIM = 256
N_HEADS = 8
QK_NOPE = 64
QK_ROPE = 32
V_HEAD = 64
Q_LORA = 384
KV_LORA = 256
POOL_WINDOWS = (2, 4, 8, 16)
POOL_GROUP = 128
POOL_WIDTH = POOL_GROUP * len(POOL_WINDOWS)
N_BRANCH = 2
D_FF = 4 * D_MODEL
ROPE_THETA = 10000.0
EPS = 1e-6
Q_BLOCK = 128
IN_SPLITS = (Q_LORA, KV_LORA, QK_ROPE, POOL_WIDTH, D_MODEL, D_MODEL)
IN_WIDTH = sum(IN_SPLITS)

kernel_name = "hybrid_mla_multiscale_pool_gated_block"


def rmsnorm(x, g):
    xf = x.astype(jnp.float32)
    y = xf * lax.rsqrt(jnp.mean(xf * xf, axis=-1, keepdims=True) + EPS)
    return (y * g.astype(jnp.float32)).astype(x.dtype)


def rope_cos_sin(positions, dim, dtype):
    inv_freq = ROPE_THETA ** (-jnp.arange(0, dim, 2, dtype=jnp.float32) / dim)
    ang = positions.astype(jnp.float32)[..., None] * inv_freq
    return jnp.cos(ang).astype(dtype), jnp.sin(ang).astype(dtype)


def apply_rope(x, cos, sin):
    half = x.shape[-1] // 2
    x1, x2 = x[..., :half], x[..., half:]
    return jnp.concatenate([x1 * cos - x2 * sin, x2 * cos + x1 * sin], axis=-1)


def mla_attention(q_nope, q_rope, k_nope, k_rope, v):
    B, S, H, _ = q_nope.shape
    nb = S // Q_BLOCK
    scale = (QK_NOPE + QK_ROPE) ** -0.5
    kpos = jnp.arange(S)

    def block(args):
        qn, qr, i = args
        s = jnp.einsum('bqhd,bkhd->bhqk', qn, k_nope, preferred_element_type=jnp.float32)
        s = s + jnp.einsum('bqhr,bkr->bhqk', qr, k_rope, preferred_element_type=jnp.float32)
        qpos = i * Q_BLOCK + jnp.arange(Q_BLOCK)
        mask = kpos[None, :] <= qpos[:, None]
        s = jnp.where(mask, s * scale, -jnp.inf)
        pr = jax.nn.softmax(s, axis=-1).astype(v.dtype)
        return jnp.einsum('bhqk,bkhd->bqhd', pr, v)

    qn_b = q_nope.reshape(B, nb, Q_BLOCK, H, QK_NOPE).transpose(1, 0, 2, 3, 4)
    qr_b = q_rope.reshape(B, nb, Q_BLOCK, H, QK_ROPE).transpose(1, 0, 2, 3, 4)
    out = lax.map(block, (qn_b, qr_b, jnp.arange(nb)))
    return out.transpose(1, 0, 2, 3, 4).reshape(B, S, H * V_HEAD)


def multiscale_pool(u, w_pool, pool_scale):
    B, S, _ = u.shape
    uf = u.reshape(B, S, len(POOL_WINDOWS), POOL_GROUP).astype(jnp.float32)
    cs = jnp.cumsum(uf, axis=1)
    t = jnp.arange(S)
    outs = []
    for g, w in enumerate(POOL_WINDOWS):
        c = cs[:, :, g]
        prev = jnp.pad(c, ((0, 0), (w, 0), (0, 0)))[:, :S]
        cnt = jnp.minimum(t + 1, w).astype(jnp.float32)[None, :, None]
        outs.append((c - prev) / cnt - uf[:, :, g])
    d = jnp.stack(outs, axis=2).astype(u.dtype)
    y = jnp.einsum('bsgc,gcd->bsgd', d, w_pool).reshape(B, S, POOL_WIDTH)
    return y * pool_scale


def _fwd_setup_inputs(seed: int = 0) -> dict:
    key = jax.random.key(seed)
    ks = jax.random.split(key, 24)

    def w(k, shape, fan_in):
        return jax.random.normal(k, shape, jnp.float32) * (fan_in ** -0.5)

    def gain(k, shape):
        return 1.0 + 0.05 * jax.random.normal(k, shape, jnp.float32)

    L = DEPTH
    x = jax.random.normal(ks[0], (BATCH, SEQ, D_MODEL), jnp.float32)
    p = jax.random.normal(ks[1], (DEPTH, BATCH, SEQ, PLE_DIM), jnp.float32)
    offset = jax.random.randint(ks[2], (BATCH, 1), 0, 4096, dtype=jnp.int32)
    positions = offset + jnp.arange(SEQ, dtype=jnp.int32)[None, :]
    return {
        "x": x,
        "p": p,
        "positions": positions,
        "g_pre_mix": gain(ks[3], (L, D_MODEL)),
        "w_in": w(ks[4], (L, D_MODEL, IN_WIDTH), D_MODEL),
        "b_gate": 0.01 * jax.random.normal(ks[5], (L, N_BRANCH * D_MODEL), jnp.float32),
        "g_q": gain(ks[6], (L, Q_LORA)),
        "w_uq": w(ks[7], (L, Q_LORA, N_HEADS * (QK_NOPE + QK_ROPE)), Q_LORA),
        "g_kv": gain(ks[8], (L, KV_LORA)),
        "w_ukv": w(ks[9], (L, KV_LORA, N_HEADS * (QK_NOPE + V_HEAD)), KV_LORA),
        "w_pool": w(ks[10], (L, len(POOL_WINDOWS), POOL_GROUP, POOL_GROUP), POOL_GROUP),
        "pool_scale": gain(ks[11], (L, POOL_WIDTH)),
        "w_branch_attn": w(ks[12], (L, N_HEADS * V_HEAD, D_MODEL), N_HEADS * V_HEAD),
        "w_branch_pool": w(ks[13], (L, POOL_WIDTH, D_MODEL), POOL_WIDTH),
        "w_out": w(ks[14], (L, D_MODEL, D_MODEL), D_MODEL),
        "g_post_mix": gain(ks[15], (L, D_MODEL)),
        "g_pre_mlp": gain(ks[16], (L, D_MODEL)),
        "w_ff1": w(ks[17], (L, D_MODEL, D_FF), D_MODEL),
        "w_ff2": w(ks[18], (L, D_FF, D_MODEL), D_FF),
        "g_post_mlp": gain(ks[19], (L, D_MODEL)),
        "w_ple_proj": w(ks[20], (L, PLE_DIM, D_MODEL), PLE_DIM),
        "w_ple_gate": w(ks[21], (L, D_MODEL, D_MODEL), D_MODEL),
        "g_ple": gain(ks[22], (L, D_MODEL)),
    }


def _fwd_reference(x, p, positions, g_pre_mix, w_in, b_gate, g_q, w_uq, g_kv, w_ukv,
              w_pool, pool_scale, w_branch_attn, w_branch_pool, w_out, g_post_mix,
              g_pre_mlp, w_ff1, w_ff2, g_post_mlp, w_ple_proj, w_ple_gate, g_ple):
    B, S, _ = x.shape
    cos, sin = rope_cos_sin(positions, QK_ROPE, x.dtype)
    split_idx = list(np.cumsum(IN_SPLITS)[:-1])
    h = x
    for i in range(DEPTH):
        a = rmsnorm(h, g_pre_mix[i])
        proj = a @ w_in[i]
        q_down, kv_down, k_rope, pool_in, gates = (
            *jnp.split(proj, split_idx, axis=-1)[:4],
            proj[..., sum(IN_SPLITS[:4]):])
        gates = jax.nn.sigmoid(gates + b_gate[i])
        gate_attn, gate_pool = gates[..., :D_MODEL], gates[..., D_MODEL:]

        q = (rmsnorm(q_down, g_q[i]) @ w_uq[i]).reshape(B, S, N_HEADS, QK_NOPE + QK_ROPE)
        q_nope = q[..., :QK_NOPE]
        q_rope = apply_rope(q[..., QK_NOPE:], cos[:, :, None, :], sin[:, :, None, :])
        kv = (rmsnorm(kv_down, g_kv[i]) @ w_ukv[i]).reshape(B, S, N_HEADS, QK_NOPE + V_HEAD)
        k_nope, v = kv[..., :QK_NOPE], kv[..., QK_NOPE:]
        k_rope = apply_rope(k_rope, cos, sin)
        attn = mla_attention(q_nope, q_rope, k_nope, k_rope, v)

        pooled = multiscale_pool(pool_in, w_pool[i], pool_scale[i])

        merged = gate_attn * (attn @ w_branch_attn[i]) + gate_pool * (pooled @ w_branch_pool[i])
        h = h + rmsnorm(merged @ w_out[i], g_post_mix[i])

        m = rmsnorm(h, g_pre_mlp[i])
        f = jnp.square(jax.nn.relu(m @ w_ff1[i])) @ w_ff2[i]
        h = h + rmsnorm(f, g_post_mlp[i])

        e = p[i] @ w_ple_proj[i]
        pg = jax.nn.sigmoid(h @ w_ple_gate[i])
        h = h + rmsnorm(pg * e, g_ple[i])
    return h


import jax as _jax
import jax.numpy as _jnp

TWIN_FORMAT = 'train_step'
FWD_PARAMS = ['x', 'p', 'positions', 'g_pre_mix', 'w_in', 'b_gate', 'g_q', 'w_uq', 'g_kv', 'w_ukv', 'w_pool', 'pool_scale', 'w_branch_attn', 'w_branch_pool', 'w_out', 'g_post_mix', 'g_pre_mlp', 'w_ff1', 'w_ff2', 'g_post_mlp', 'w_ple_proj', 'w_ple_gate', 'g_ple']
TWIN_WEIGHTS = ['g_pre_mix', 'w_in', 'b_gate', 'g_q', 'w_uq', 'g_kv', 'w_ukv', 'w_pool', 'pool_scale', 'w_branch_attn', 'w_branch_pool', 'w_out', 'g_post_mix', 'g_pre_mlp', 'w_ff1', 'w_ff2', 'g_post_mlp', 'w_ple_proj', 'w_ple_gate', 'g_ple']
TWIN_DIFF_INPUT = 'x'
TWIN_INPUTS = ['x', 'p', 'positions', 'g_pre_mix', 'w_in', 'b_gate', 'g_q', 'w_uq', 'g_kv', 'w_ukv', 'w_pool', 'pool_scale', 'w_branch_attn', 'w_branch_pool', 'w_out', 'g_post_mix', 'g_pre_mlp', 'w_ff1', 'w_ff2', 'g_post_mlp', 'w_ple_proj', 'w_ple_gate', 'g_ple', 'loss_target', 'm_g_pre_mix', 'm_w_in', 'm_b_gate', 'm_g_q', 'm_w_uq', 'm_g_kv', 'm_w_ukv', 'm_w_pool', 'm_pool_scale', 'm_w_branch_attn', 'm_w_branch_pool', 'm_w_out', 'm_g_post_mix', 'm_g_pre_mlp', 'm_w_ff1', 'm_w_ff2', 'm_g_post_mlp', 'm_w_ple_proj', 'm_w_ple_gate', 'm_g_ple', 'v_g_pre_mix', 'v_w_in', 'v_b_gate', 'v_g_q', 'v_w_uq', 'v_g_kv', 'v_w_ukv', 'v_w_pool', 'v_pool_scale', 'v_w_branch_attn', 'v_w_branch_pool', 'v_w_out', 'v_g_post_mix', 'v_g_pre_mlp', 'v_w_ff1', 'v_w_ff2', 'v_g_post_mlp', 'v_w_ple_proj', 'v_w_ple_gate', 'v_g_ple']
TWIN_OUTPUTS = ['loss', 'grad_x', 'grad_g_pre_mix', 'grad_w_in', 'grad_b_gate', 'grad_g_q', 'grad_w_uq', 'grad_g_kv', 'grad_w_ukv', 'grad_w_pool', 'grad_pool_scale', 'grad_w_branch_attn', 'grad_w_branch_pool', 'grad_w_out', 'grad_g_post_mix', 'grad_g_pre_mlp', 'grad_w_ff1', 'grad_w_ff2', 'grad_g_post_mlp', 'grad_w_ple_proj', 'grad_w_ple_gate', 'grad_g_ple', 'delta_g_pre_mix', 'delta_w_in', 'delta_b_gate', 'delta_g_q', 'delta_w_uq', 'delta_g_kv', 'delta_w_ukv', 'delta_w_pool', 'delta_pool_scale', 'delta_w_branch_attn', 'delta_w_branch_pool', 'delta_w_out', 'delta_g_post_mix', 'delta_g_pre_mlp', 'delta_w_ff1', 'delta_w_ff2', 'delta_g_post_mlp', 'delta_w_ple_proj', 'delta_w_ple_gate', 'delta_g_ple', 'new_m_g_pre_mix', 'new_m_w_in', 'new_m_b_gate', 'new_m_g_q', 'new_m_w_uq', 'new_m_g_kv', 'new_m_w_ukv', 'new_m_w_pool', 'new_m_pool_scale', 'new_m_w_branch_attn', 'new_m_w_branch_pool', 'new_m_w_out', 'new_m_g_post_mix', 'new_m_g_pre_mlp', 'new_m_w_ff1', 'new_m_w_ff2', 'new_m_g_post_mlp', 'new_m_w_ple_proj', 'new_m_w_ple_gate', 'new_m_g_ple', 'new_v_g_pre_mix', 'new_v_w_in', 'new_v_b_gate', 'new_v_g_q', 'new_v_w_uq', 'new_v_g_kv', 'new_v_w_ukv', 'new_v_w_pool', 'new_v_pool_scale', 'new_v_w_branch_attn', 'new_v_w_branch_pool', 'new_v_w_out', 'new_v_g_post_mix', 'new_v_g_pre_mlp', 'new_v_w_ff1', 'new_v_w_ff2', 'new_v_g_post_mlp', 'new_v_w_ple_proj', 'new_v_w_ple_gate', 'new_v_g_ple']
TWIN_LEAF_KINDS = {'loss': 'loss', 'grad_x': 'grad_x', 'grad_g_pre_mix': 'grad_w', 'grad_w_in': 'grad_w', 'grad_b_gate': 'grad_w', 'grad_g_q': 'grad_w', 'grad_w_uq': 'grad_w', 'grad_g_kv': 'grad_w', 'grad_w_ukv': 'grad_w', 'grad_w_pool': 'grad_w', 'grad_pool_scale': 'grad_w', 'grad_w_branch_attn': 'grad_w', 'grad_w_branch_pool': 'grad_w', 'grad_w_out': 'grad_w', 'grad_g_post_mix': 'grad_w', 'grad_g_pre_mlp': 'grad_w', 'grad_w_ff1': 'grad_w', 'grad_w_ff2': 'grad_w', 'grad_g_post_mlp': 'grad_w', 'grad_w_ple_proj': 'grad_w', 'grad_w_ple_gate': 'grad_w', 'grad_g_ple': 'grad_w', 'delta_g_pre_mix': 'delta_w', 'delta_w_in': 'delta_w', 'delta_b_gate': 'delta_w', 'delta_g_q': 'delta_w', 'delta_w_uq': 'delta_w', 'delta_g_kv': 'delta_w', 'delta_w_ukv': 'delta_w', 'delta_w_pool': 'delta_w', 'delta_pool_scale': 'delta_w', 'delta_w_branch_attn': 'delta_w', 'delta_w_branch_pool': 'delta_w', 'delta_w_out': 'delta_w', 'delta_g_post_mix': 'delta_w', 'delta_g_pre_mlp': 'delta_w', 'delta_w_ff1': 'delta_w', 'delta_w_ff2': 'delta_w', 'delta_g_post_mlp': 'delta_w', 'delta_w_ple_proj': 'delta_w', 'delta_w_ple_gate': 'delta_w', 'delta_g_ple': 'delta_w', 'new_m_g_pre_mix': 'new_m', 'new_m_w_in': 'new_m', 'new_m_b_gate': 'new_m', 'new_m_g_q': 'new_m', 'new_m_w_uq': 'new_m', 'new_m_g_kv': 'new_m', 'new_m_w_ukv': 'new_m', 'new_m_w_pool': 'new_m', 'new_m_pool_scale': 'new_m', 'new_m_w_branch_attn': 'new_m', 'new_m_w_branch_pool': 'new_m', 'new_m_w_out': 'new_m', 'new_m_g_post_mix': 'new_m', 'new_m_g_pre_mlp': 'new_m', 'new_m_w_ff1': 'new_m', 'new_m_w_ff2': 'new_m', 'new_m_g_post_mlp': 'new_m', 'new_m_w_ple_proj': 'new_m', 'new_m_w_ple_gate': 'new_m', 'new_m_g_ple': 'new_m', 'new_v_g_pre_mix': 'new_v', 'new_v_w_in': 'new_v', 'new_v_b_gate': 'new_v', 'new_v_g_q': 'new_v', 'new_v_w_uq': 'new_v', 'new_v_g_kv': 'new_v', 'new_v_w_ukv': 'new_v', 'new_v_w_pool': 'new_v', 'new_v_pool_scale': 'new_v', 'new_v_w_branch_attn': 'new_v', 'new_v_w_branch_pool': 'new_v', 'new_v_w_out': 'new_v', 'new_v_g_post_mix': 'new_v', 'new_v_g_pre_mlp': 'new_v', 'new_v_w_ff1': 'new_v', 'new_v_w_ff2': 'new_v', 'new_v_g_post_mlp': 'new_v', 'new_v_w_ple_proj': 'new_v', 'new_v_w_ple_gate': 'new_v', 'new_v_g_ple': 'new_v'}


def _forward(args):
    return _fwd_reference(*[args[k] for k in FWD_PARAMS])


def _output_shape():
    def fwd():
        inp = _fwd_setup_inputs(0)
        return _fwd_reference(*[inp[k] for k in FWD_PARAMS])
    out = _jax.eval_shape(fwd)
    return out.shape, out.dtype

N_MICROBATCH = 1
ADAM_LR = 0.001
ADAM_B1 = 0.9
ADAM_B2 = 0.999
ADAM_EPS = 1e-08
ADAM_WD = 0.01
ADAM_STEP = 10
PER_EXAMPLE_BATCH_AXIS = {'x': 0, 'p': 1, 'positions': 0, 'loss_target': 0}
SHARED_INPUTS = []
_WEIGHT_DTYPES = {'g_pre_mix': _jnp.float32, 'w_in': _jnp.float32, 'b_gate': _jnp.float32, 'g_q': _jnp.float32, 'w_uq': _jnp.float32, 'g_kv': _jnp.float32, 'w_ukv': _jnp.float32, 'w_pool': _jnp.float32, 'pool_scale': _jnp.float32, 'w_branch_attn': _jnp.float32, 'w_branch_pool': _jnp.float32, 'w_out': _jnp.float32, 'g_post_mix': _jnp.float32, 'g_pre_mlp': _jnp.float32, 'w_ff1': _jnp.float32, 'w_ff2': _jnp.float32, 'g_post_mlp': _jnp.float32, 'w_ple_proj': _jnp.float32, 'w_ple_gate': _jnp.float32, 'g_ple': _jnp.float32}
MOMENT_SCALE = {'g_pre_mix': 2.967037e+00, 'w_in': 1.404468e+00, 'b_gate': 1.505169e+00, 'g_q': 3.258500e-01, 'w_uq': 2.272325e-01, 'g_kv': 7.309705e-01, 'w_ukv': 3.680539e-01, 'w_pool': 4.897936e+00, 'pool_scale': 7.066243e+00, 'w_branch_attn': 3.183333e-01, 'w_branch_pool': 4.239753e+00, 'w_out': 4.581110e+00, 'g_post_mix': 1.281461e+02, 'g_pre_mlp': 4.154768e+00, 'w_ff1': 1.907401e+00, 'w_ff2': 4.429317e+00, 'g_post_mlp': 1.318631e+02, 'w_ple_proj': 7.855090e-01, 'w_ple_gate': 7.287729e-01, 'g_ple': 1.309590e+02}


def _to_microbatches(a, axis):
    t = _jnp.moveaxis(a, axis, 0)
    t = t.reshape((N_MICROBATCH, t.shape[0] // N_MICROBATCH) + t.shape[1:])
    return _jnp.moveaxis(t, 1, axis + 1)


def setup_inputs(seed: int = 0) -> dict:
    inp = _fwd_setup_inputs(seed)
    key = _jax.random.fold_in(_jax.random.key(seed), 7919)
    shape, _ = _output_shape()
    out = dict(inp)
    out["loss_target"] = _jax.random.normal(_jax.random.fold_in(key, 0), shape, _jnp.float32)
    for i, name in enumerate(TWIN_WEIGHTS):
        w = inp[name].astype(_jnp.float32)
        if MOMENT_SCALE is None:
            s = _jnp.sqrt(_jnp.mean(_jnp.square(w)) + 1e-30)
        else:
            s = MOMENT_SCALE[name]
        km, kv = _jax.random.split(_jax.random.fold_in(key, i + 1))
        out[name] = w
        out["m_" + name] = s * _jax.random.normal(km, w.shape, _jnp.float32)
        out["v_" + name] = (s * s) * _jax.random.uniform(kv, w.shape, _jnp.float32, 0.5, 1.5)
    if N_MICROBATCH > 1:
        for name, axis in PER_EXAMPLE_BATCH_AXIS.items():
            out[name] = _to_microbatches(out[name], axis)
    return {'x': out['x'], 'p': out['p'], 'positions': out['positions'], 'g_pre_mix': out['g_pre_mix'], 'w_in': out['w_in'], 'b_gate': out['b_gate'], 'g_q': out['g_q'], 'w_uq': out['w_uq'], 'g_kv': out['g_kv'], 'w_ukv': out['w_ukv'], 'w_pool': out['w_pool'], 'pool_scale': out['pool_scale'], 'w_branch_attn': out['w_branch_attn'], 'w_branch_pool': out['w_branch_pool'], 'w_out': out['w_out'], 'g_post_mix': out['g_post_mix'], 'g_pre_mlp': out['g_pre_mlp'], 'w_ff1': out['w_ff1'], 'w_ff2': out['w_ff2'], 'g_post_mlp': out['g_post_mlp'], 'w_ple_proj': out['w_ple_proj'], 'w_ple_gate': out['w_ple_gate'], 'g_ple': out['g_ple'], 'loss_target': out['loss_target'], 'm_g_pre_mix': out['m_g_pre_mix'], 'm_w_in': out['m_w_in'], 'm_b_gate': out['m_b_gate'], 'm_g_q': out['m_g_q'], 'm_w_uq': out['m_w_uq'], 'm_g_kv': out['m_g_kv'], 'm_w_ukv': out['m_w_ukv'], 'm_w_pool': out['m_w_pool'], 'm_pool_scale': out['m_pool_scale'], 'm_w_branch_attn': out['m_w_branch_attn'], 'm_w_branch_pool': out['m_w_branch_pool'], 'm_w_out': out['m_w_out'], 'm_g_post_mix': out['m_g_post_mix'], 'm_g_pre_mlp': out['m_g_pre_mlp'], 'm_w_ff1': out['m_w_ff1'], 'm_w_ff2': out['m_w_ff2'], 'm_g_post_mlp': out['m_g_post_mlp'], 'm_w_ple_proj': out['m_w_ple_proj'], 'm_w_ple_gate': out['m_w_ple_gate'], 'm_g_ple': out['m_g_ple'], 'v_g_pre_mix': out['v_g_pre_mix'], 'v_w_in': out['v_w_in'], 'v_b_gate': out['v_b_gate'], 'v_g_q': out['v_g_q'], 'v_w_uq': out['v_w_uq'], 'v_g_kv': out['v_g_kv'], 'v_w_ukv': out['v_w_ukv'], 'v_w_pool': out['v_w_pool'], 'v_pool_scale': out['v_pool_scale'], 'v_w_branch_attn': out['v_w_branch_attn'], 'v_w_branch_pool': out['v_w_branch_pool'], 'v_w_out': out['v_w_out'], 'v_g_post_mix': out['v_g_post_mix'], 'v_g_pre_mlp': out['v_g_pre_mlp'], 'v_w_ff1': out['v_w_ff1'], 'v_w_ff2': out['v_w_ff2'], 'v_g_post_mlp': out['v_g_post_mlp'], 'v_w_ple_proj': out['v_w_ple_proj'], 'v_w_ple_gate': out['v_w_ple_gate'], 'v_g_ple': out['v_g_ple']}


def _loss(weights, diff, rest, loss_target):
    with _jax.named_scope("forward"):
        args = {**rest, TWIN_DIFF_INPUT: diff, **{k: w.astype(_WEIGHT_DTYPES[k]) for k, w in weights.items()}}
        y = _forward(args)
    with _jax.named_scope("loss_head"):
        err = _jnp.square(y.astype(_jnp.float32) - loss_target)
        return 0.5 * _jnp.sum(_jnp.mean(err, axis=-1)) if err.ndim else 0.5 * err


def _adamw(w, g, m, v):
    m = ADAM_B1 * m + (1.0 - ADAM_B1) * g
    v = ADAM_B2 * v + (1.0 - ADAM_B2) * _jnp.square(g)
    m_hat = m / (1.0 - ADAM_B1 ** ADAM_STEP)
    v_hat = v / (1.0 - ADAM_B2 ** ADAM_STEP)
    delta = -ADAM_LR * (m_hat / (_jnp.sqrt(v_hat) + ADAM_EPS) + ADAM_WD * w)
    return delta, m, v


def reference(x, p, positions, g_pre_mix, w_in, b_gate, g_q, w_uq, g_kv, w_ukv, w_pool, pool_scale, w_branch_attn, w_branch_pool, w_out, g_post_mix, g_pre_mlp, w_ff1, w_ff2, g_post_mlp, w_ple_proj, w_ple_gate, g_ple, loss_target, m_g_pre_mix, m_w_in, m_b_gate, m_g_q, m_w_uq, m_g_kv, m_w_ukv, m_w_pool, m_pool_scale, m_w_branch_attn, m_w_branch_pool, m_w_out, m_g_post_mix, m_g_pre_mlp, m_w_ff1, m_w_ff2, m_g_post_mlp, m_w_ple_proj, m_w_ple_gate, m_g_ple, v_g_pre_mix, v_w_in, v_b_gate, v_g_q, v_w_uq, v_g_kv, v_w_ukv, v_w_pool, v_pool_scale, v_w_branch_attn, v_w_branch_pool, v_w_out, v_g_post_mix, v_g_pre_mlp, v_w_ff1, v_w_ff2, v_g_post_mlp, v_w_ple_proj, v_w_ple_gate, v_g_ple):
    given = dict(x=x, p=p, positions=positions, g_pre_mix=g_pre_mix, w_in=w_in, b_gate=b_gate, g_q=g_q, w_uq=w_uq, g_kv=g_kv, w_ukv=w_ukv, w_pool=w_pool, pool_scale=pool_scale, w_branch_attn=w_branch_attn, w_branch_pool=w_branch_pool, w_out=w_out, g_post_mix=g_post_mix, g_pre_mlp=g_pre_mlp, w_ff1=w_ff1, w_ff2=w_ff2, g_post_mlp=g_post_mlp, w_ple_proj=w_ple_proj, w_ple_gate=w_ple_gate, g_ple=g_ple, loss_target=loss_target, m_g_pre_mix=m_g_pre_mix, m_w_in=m_w_in, m_b_gate=m_b_gate, m_g_q=m_g_q, m_w_uq=m_w_uq, m_g_kv=m_g_kv, m_w_ukv=m_w_ukv, m_w_pool=m_w_pool, m_pool_scale=m_pool_scale, m_w_branch_attn=m_w_branch_attn, m_w_branch_pool=m_w_branch_pool, m_w_out=m_w_out, m_g_post_mix=m_g_post_mix, m_g_pre_mlp=m_g_pre_mlp, m_w_ff1=m_w_ff1, m_w_ff2=m_w_ff2, m_g_post_mlp=m_g_post_mlp, m_w_ple_proj=m_w_ple_proj, m_w_ple_gate=m_w_ple_gate, m_g_ple=m_g_ple, v_g_pre_mix=v_g_pre_mix, v_w_in=v_w_in, v_b_gate=v_b_gate, v_g_q=v_g_q, v_w_uq=v_w_uq, v_g_kv=v_g_kv, v_w_ukv=v_w_ukv, v_w_pool=v_w_pool, v_pool_scale=v_pool_scale, v_w_branch_attn=v_w_branch_attn, v_w_branch_pool=v_w_branch_pool, v_w_out=v_w_out, v_g_post_mix=v_g_post_mix, v_g_pre_mlp=v_g_pre_mlp, v_w_ff1=v_w_ff1, v_w_ff2=v_w_ff2, v_g_post_mlp=v_g_post_mlp, v_w_ple_proj=v_w_ple_proj, v_w_ple_gate=v_w_ple_gate, v_g_ple=v_g_ple)
    weights = {n: given[n] for n in TWIN_WEIGHTS}
    shared = {n: given[n] for n in SHARED_INPUTS}
    per_example = {n: given[n] for n in ['x', 'p', 'positions']}
    grad_fn = _jax.value_and_grad(_loss, argnums=(0, 1))

    def one_microbatch(ex, loss_target):
        ex = dict(ex)
        diff = ex.pop(TWIN_DIFF_INPUT)
        return grad_fn(weights, diff, {**shared, **ex}, loss_target)

    if N_MICROBATCH == 1:
        loss, (grad_w, grad_x) = one_microbatch(per_example, given["loss_target"])
    else:
        def body(carry, xs):
            loss_sum, grad_sum = carry
            l_k, (gw_k, gx_k) = one_microbatch(xs[0], xs[1])
            with _jax.named_scope("update"):
                return (loss_sum + l_k, _jax.tree.map(_jnp.add, grad_sum, gw_k)), gx_k

        init = (_jnp.zeros((), _jnp.float32), _jax.tree.map(_jnp.zeros_like, weights))
        (loss, grad_w), grad_x = _jax.lax.scan(body, init, (per_example, given["loss_target"]))
    with _jax.named_scope("update"):
        delta_w, new_m, new_v = {}, {}, {}
        for n in TWIN_WEIGHTS:
            delta_w[n], new_m[n], new_v[n] = _adamw(weights[n], grad_w[n], given["m_" + n], given["v_" + n])
    return (loss, grad_x, *[grad_w[n] for n in TWIN_WEIGHTS], *[delta_w[n] for n in TWIN_WEIGHTS],
            *[new_m[n] for n in TWIN_WEIGHTS], *[new_v[n] for n in TWIN_WEIGHTS])
```

```python
import functools

import numpy as np
import jax
import jax.numpy as jnp
from jax import lax
from jax.experimental import pallas as pl
from jax.experimental.pallas import tpu as pltpu

F32 = jnp.float32
MX = jnp.bfloat16

D_MODEL = 1024
N_HEADS = 8
QK_NOPE = 64
QK_ROPE = 32
V_HEAD = 64
Q_LORA = 384
KV_LORA = 256
POOL_WINDOWS = (2, 4, 8, 16)
POOL_GROUP = 128
POOL_WIDTH = 512
D_FF = 4096
PLE_DIM = 256
ROPE_THETA = 10000.0
EPS = 1e-6
HEAD_SLOT = 128
QK_WIDTH = N_HEADS * HEAD_SLOT
ROPE_LANE = 64
SMALL_COLS = Q_LORA + KV_LORA + HEAD_SLOT
IN_PAD = SMALL_COLS + POOL_WIDTH + 2 * D_MODEL
SCALE = (QK_NOPE + QK_ROPE) ** -0.5
NEG = -1e30
HALO = 16

ADAM_LR = 0.001
ADAM_B1 = 0.9
ADAM_B2 = 0.999
ADAM_EPS = 1e-08
ADAM_WD = 0.01
ADAM_STEP = 10

VMEM_LIMIT = 56 * 2**20
MESH = pl.DeviceIdType.MESH

SHARDED = (
    ("w_in", (1024, 3232), 1),
    ("w_uq", (384, 768), 1),
    ("w_ukv", (256, 1024), 1),
    ("w_branch_attn", (512, 1024), 1),
    ("w_branch_pool", (512, 1024), 1),
    ("w_out", (1024, 1024), 0),
    ("w_ff1", (1024, 4096), 1),
    ("w_ff2", (4096, 1024), 0),
    ("w_ple_proj", (256, 1024), 1),
    ("w_ple_gate", (1024, 1024), 0),
)
SMALL = (
    ("g_pre_mix", (1, 1024)),
    ("b_gate", (1, 2048)),
    ("g_q", (1, 384)),
    ("g_kv", (1, 256)),
    ("w_pool", (1, 4, 128, 128)),
    ("pool_scale", (1, 512)),
    ("g_post_mix", (1, 1024)),
    ("g_pre_mlp", (1, 1024)),
    ("g_post_mlp", (1, 1024)),
    ("g_ple", (1, 1024)),
)
WEIGHT_ORDER = ("g_pre_mix", "w_in", "b_gate", "g_q", "w_uq", "g_kv", "w_ukv", "w_pool", "pool_scale", "w_branch_attn",
                "w_branch_pool", "w_out", "g_post_mix", "g_pre_mlp", "w_ff1", "w_ff2", "g_post_mlp", "w_ple_proj",
                "w_ple_gate", "g_ple")
N_CHIPS = 4
PACK_COLS = 1024
BIG_ROWS = 3840
SMALL_ROWS = 80


def _dot(a, b):
    return jnp.dot(a.astype(MX), b.astype(MX), preferred_element_type=F32)


def _dot_nt(a, b):
    return lax.dot_general(a.astype(MX), b.astype(MX), (((1,), (1,)), ((), ())), preferred_element_type=F32)


def _dot_tn(a, b):
    return lax.dot_general(a.astype(MX), b.astype(MX), (((0,), (0,)), ((), ())), preferred_element_type=F32)


def _sig(x):
    return 1.0 / (1.0 + jnp.exp(-x))


def _rms(x, g):
    r = lax.rsqrt(jnp.mean(x * x, axis=1, keepdims=True) + EPS)
    xh = x * r
    return xh * g, xh, r


def _rms_bwd(xh, r, g, dy):
    dxn = dy * g
    dx = r * (dxn - xh * jnp.mean(dxn * xh, axis=1, keepdims=True))
    return dx, jnp.sum(dy * xh, axis=0, keepdims=True)


def _rot_half(v):
    lane = lax.broadcasted_iota(jnp.int32, v.shape, 1)
    return jnp.where(lane < ROPE_LANE + QK_ROPE // 2, pltpu.roll(v, HEAD_SLOT - QK_ROPE // 2, 1), pltpu.roll(v, QK_ROPE // 2, 1))


def _rope(v, cc, sa):
    return v * cc + _rot_half(v) * sa


def _unrope(v, cc, sa):
    return v * cc - _rot_half(v) * sa


def _params(sem):
    return pltpu.CompilerParams(dimension_semantics=sem, vmem_limit_bytes=VMEM_LIMIT)


def _tok_call(name, body, n_tok, tm, tiled, resident, outs, accs=(), scratch=()):
    def as_pair(t):
        if isinstance(t, tuple):
            return t
        return t, pl.BlockSpec((tm, t.shape[1]), lambda i: (i, 0))
    tiled = [as_pair(t) for t in tiled]
    res_specs = [pl.BlockSpec(r.shape, lambda i, nd=r.ndim: (0,) * nd, pipeline_mode=pl.Buffered(1)) for r in resident]
    out_specs = [pl.BlockSpec((tm, o.shape[1]), lambda i: (i, 0)) for o in outs]
    out_specs += [pl.BlockSpec(a.shape, lambda i: (0, 0)) for a in accs]
    n_t, n_r, n_o, n_a = len(tiled), len(resident), len(outs), len(accs)

    def kern(*refs):
        tin, res = refs[:n_t], refs[n_t:n_t + n_r]
        tout = refs[n_t + n_r:n_t + n_r + n_o]
        acc = refs[n_t + n_r + n_o:n_t + n_r + n_o + n_a]
        scr = refs[n_t + n_r + n_o + n_a:]
        i = pl.program_id(0)

        @pl.when(i == 0)
        def _():
            for a in acc:
                a[...] = jnp.zeros(a.shape, a.dtype)
        body(i, tin, res, tout, acc, scr)

    return pl.pallas_call(
        kern, name=name, grid=(n_tok // tm,), in_specs=[s for _, s in tiled] + res_specs, out_specs=out_specs,
        out_shape=list(outs) + list(accs), scratch_shapes=list(scratch), compiler_params=_params(("arbitrary",)),
    )(*[a for a, _ in tiled], *resident)


def _sds(rows, cols, dtype):
    return jax.ShapeDtypeStruct((rows, cols), dtype)


def _fwd_inproj(x, cc, sa, w, tm):
    n_tok = x.shape[0]

    def body(i, tin, res, tout, acc, scr):
        x_ref, c_ref, s_ref = tin
        g_pre, w_in, g_q, w_uq, g_kv, w_k, w_v, e_mat = res
        a_o, ps_o, u_o, gl_o, qn_o, kvn_o, q_o, k_o, v_o = tout
        a = _rms(x_ref[...], g_pre[...])[0].astype(MX)
        a_o[...] = a
        ps = _dot(a, w_in[:, :SMALL_COLS])
        ps_o[...] = ps.astype(ps_o.dtype)
        u_o[...] = _dot(a, w_in[:, SMALL_COLS:SMALL_COLS + POOL_WIDTH]).astype(u_o.dtype)
        gl_o[...] = _dot(a, w_in[:, SMALL_COLS + POOL_WIDTH:]).astype(gl_o.dtype)
        cc_, sa_ = c_ref[...], s_ref[...]
        qn = _rms(ps[:, :Q_LORA], g_q[...])[0].astype(MX)
        qn_o[...] = qn
        q = _dot(qn, w_uq[...])
        for h in range(N_HEADS):
            hs = slice(h * HEAD_SLOT, (h + 1) * HEAD_SLOT)
            q_o[:, hs] = (_rope(q[:, hs], cc_, sa_) * SCALE).astype(q_o.dtype)
        kvn = _rms(ps[:, Q_LORA:Q_LORA + KV_LORA], g_kv[...])[0].astype(MX)
        kvn_o[...] = kvn
        kr = _rope(ps[:, Q_LORA + KV_LORA:], cc_, sa_)
        k_o[...] = (_dot(kvn, w_k[...]) + _dot(kr, e_mat[...])).astype(k_o.dtype)
        v_o[...] = _dot(kvn, w_v[...]).astype(v_o.dtype)

    outs = [_sds(n_tok, D_MODEL, MX), _sds(n_tok, SMALL_COLS, MX), _sds(n_tok, POOL_WIDTH, MX), _sds(n_tok, 2 * D_MODEL, MX),
            _sds(n_tok, Q_LORA, MX), _sds(n_tok, KV_LORA, MX), _sds(n_tok, QK_WIDTH, MX), _sds(n_tok, QK_WIDTH, MX),
            _sds(n_tok, N_HEADS * V_HEAD, MX)]
    res = [w["g_pre_mix"], w["w_in"], w["g_q"], w["w_uq"], w["g_kv"], w["w_k"], w["w_v"], w["e_mat"]]
    return _tok_call("fwd_inproj", body, n_tok, tm, [x, cc, sa], res, outs)


def _causal_pairs(n, by_kv):
    if by_kv:
        pairs = [(i, j) for j in range(n) for i in range(j, n)]
    else:
        pairs = [(i, j) for i in range(n) for j in range(i + 1)]
    return (jnp.asarray(np.array([p[0] for p in pairs], np.int32)), jnp.asarray(np.array([p[1] for p in pairs], np.int32)))


def _attn_fwd(q, k, v, tq):
    n_tok = q.shape[0]
    n = n_tok // tq
    qi, kj = _causal_pairs(n, by_kv=False)

    def kern(qi_ref, kj_ref, q_ref, k_ref, v_ref, o_ref, lse_ref, m_s, l_s, acc_s):
        s_id = pl.program_id(0)
        i, j = qi_ref[s_id], kj_ref[s_id]

        @pl.when(j == 0)
        def _():
            m_s[...] = jnp.full(m_s.shape, NEG, F32)
            l_s[...] = jnp.zeros(l_s.shape, F32)
            acc_s[...] = jnp.zeros(acc_s.shape, F32)

        row = lax.broadcasted_iota(jnp.int32, (tq, tq), 0)
        col = lax.broadcasted_iota(jnp.int32, (tq, tq), 1)
        keep = jnp.logical_or(col <= row, j < i)
        for h in range(N_HEADS):
            hs = slice(h * HEAD_SLOT, (h + 1) * HEAD_SLOT)
            s = jnp.where(keep, _dot_nt(q_ref[:, hs], k_ref[:, hs]), NEG)
            m_old = m_s[h]
            m_new = jnp.maximum(m_old, jnp.max(s, axis=1, keepdims=True))
            alpha = jnp.exp(m_old - m_new)
            p = jnp.exp(s - m_new)
            l_s[h] = alpha * l_s[h] + jnp.sum(p, axis=1, keepdims=True)
            acc_s[h] = alpha * acc_s[h] + _dot(p, v_ref[:, h * V_HEAD:(h + 1) * V_HEAD])
            m_s[h] = m_new

        @pl.when(j == i)
        def _():
            lane = lax.broadcasted_iota(jnp.int32, (tq, HEAD_SLOT), 1)
            lse = jnp.zeros((tq, HEAD_SLOT), F32)
            for h in range(N_HEADS):
                o_ref[:, h * V_HEAD:(h + 1) * V_HEAD] = (acc_s[h] / l_s[h]).astype(o_ref.dtype)
                lse = jnp.where(lane == h, m_s[h] + jnp.log(l_s[h]), lse)
            lse_ref[...] = lse

    gs = pltpu.PrefetchScalarGridSpec(
        num_scalar_prefetch=2, grid=(qi.shape[0],),
        in_specs=[pl.BlockSpec((tq, QK_WIDTH), lambda s, qi, kj: (qi[s], 0)),
                  pl.BlockSpec((tq, QK_WIDTH), lambda s, qi, kj: (kj[s], 0)),
                  pl.BlockSpec((tq, N_HEADS * V_HEAD), lambda s, qi, kj: (kj[s], 0))],
        out_specs=[pl.BlockSpec((tq, N_HEADS * V_HEAD), lambda s, qi, kj: (qi[s], 0)),
                   pl.BlockSpec((tq, HEAD_SLOT), lambda s, qi, kj: (qi[s], 0))],
        scratch_shapes=[pltpu.VMEM((N_HEADS, tq, 1), F32), pltpu.VMEM((N_HEADS, tq, 1), F32),
                        pltpu.VMEM((N_HEADS, tq, V_HEAD), F32)])
    return pl.pallas_call(kern, name="attn_fwd", grid_spec=gs,
                          out_shape=[_sds(n_tok, N_HEADS * V_HEAD, MX), _sds(n_tok, HEAD_SLOT, F32)],
                          compiler_params=_params(("arbitrary",)))(qi, kj, q, k, v)


def _pool_windows(ext, i, tm, first_row):
    row = i * tm + lax.broadcasted_iota(jnp.int32, (tm, 1), 0)
    out = []
    for g, w in enumerate(POOL_WINDOWS):
        cs = slice(g * POOL_GROUP, (g + 1) * POOL_GROUP)
        s = ext[pl.ds(first_row, tm), cs]
        for k in range(1, w):
            s = s + ext[pl.ds(first_row - k, tm), cs]
        cnt = jnp.minimum(row + 1, w).astype(F32)
        out.append(s / cnt)
    return out


def _fwd_mix(x, u, gl, attn, w, tm):
    n_tok = x.shape[0]
    halo_spec = pl.BlockSpec((HALO, POOL_WIDTH), lambda i: (jnp.maximum(i * (tm // HALO) - 1, 0), 0))

    def body(i, tin, res, tout, acc, scr):
        x_ref, u_ref, uh_ref, gl_ref, at_ref = tin
        w_pool, pool_scale, w_ba, w_bp, b_gate, w_out, g_post = res
        d_o, pooled_o, a_o, pp_o, merged_o, y_o, h1_o = tout
        ext, = scr
        ext[pl.ds(0, HALO), :] = jnp.where(i > 0, uh_ref[...].astype(F32), 0.0)
        ext[pl.ds(HALO, tm), :] = u_ref[...].astype(F32)
        means = _pool_windows(ext, i, tm, HALO)
        for g in range(len(POOL_WINDOWS)):
            cs = slice(g * POOL_GROUP, (g + 1) * POOL_GROUP)
            d = (means[g] - ext[pl.ds(HALO, tm), cs]).astype(MX)
            d_o[:, cs] = d
            pooled_o[:, cs] = (_dot(d, w_pool[g]) * pool_scale[:, cs]).astype(pooled_o.dtype)
        a_br = _dot(at_ref[...], w_ba[...])
        p_br = _dot(pooled_o[...], w_bp[...])
        a_o[...] = a_br.astype(a_o.dtype)
        pp_o[...] = p_br.astype(pp_o.dtype)
        gates = _sig(gl_ref[...].astype(F32) + b_gate[...])
        merged = (gates[:, :D_MODEL] * a_br + gates[:, D_MODEL:] * p_br).astype(MX)
        merged_o[...] = merged
        y = _dot(merged, w_out[...])
        y_o[...] = y.astype(y_o.dtype)
        h1_o[...] = x_ref[...] + _rms(y, g_post[...])[0]

    outs = [_sds(n_tok, POOL_WIDTH, MX), _sds(n_tok, POOL_WIDTH, MX), _sds(n_tok, D_MODEL, MX), _sds(n_tok, D_MODEL, MX),
            _sds(n_tok, D_MODEL, MX), _sds(n_tok, D_MODEL, MX), _sds(n_tok, D_MODEL, F32)]
    res = [w["w_pool"], w["pool_scale"], w["w_branch_attn"], w["w_branch_pool"], w["b_gate"], w["w_out"], w["g_post_mix"]]
    return _tok_call("fwd_mix", body, n_tok, tm, [x, u, (u, halo_spec), gl, attn], res, outs,
                     scratch=[pltpu.VMEM((tm + HALO, POOL_WIDTH), F32)])


def _fwd_mlp(h1, w, tm):
    n_tok = h1.shape[0]

    def body(i, tin, res, tout, acc, scr):
        h1_ref, = tin
        g_pre, w1, w2, g_post = res
        m_o, zr_o, a2_o, f_o, h2_o = tout
        h1_ = h1_ref[...]
        m = _rms(h1_, g_pre[...])[0].astype(MX)
        m_o[...] = m
        zr = jnp.maximum(_dot(m, w1[...]), 0.0)
        zr_o[...] = zr.astype(zr_o.dtype)
        a2 = (zr * zr).astype(MX)
        a2_o[...] = a2
        f = _dot(a2, w2[...])
        f_o[...] = f.astype(f_o.dtype)
        h2_o[...] = h1_ + _rms(f, g_post[...])[0]

    outs = [_sds(n_tok, D_MODEL, MX), _sds(n_tok, D_FF, MX), _sds(n_tok, D_FF, MX), _sds(n_tok, D_MODEL, MX),
            _sds(n_tok, D_MODEL, F32)]
    res = [w["g_pre_mlp"], w["w_ff1"], w["w_ff2"], w["g_post_mlp"]]
    return _tok_call("fwd_mlp", body, n_tok, tm, [h1], res, outs)


def _ple_fwd_bwd(h2, p, target, w, tm):
    n_tok = h2.shape[0]

    def body(i, tin, res, tout, acc, scr):
        h2_ref, p_ref, t_ref = tin
        w_pe, w_pg, g_ple = res
        dh2_o, de_o, dzg_o = tout
        loss_a, dg_a = acc
        h2_ = h2_ref[...]
        e = _dot(p_ref[...], w_pe[...])
        pg = _sig(_dot(h2_, w_pg[...]))
        t = pg * e
        g = g_ple[...]
        tn, th, r = _rms(t, g)
        diff = h2_ + tn - t_ref[...]
        loss_a[...] += jnp.sum(diff * diff, axis=0, keepdims=True)
        dh3 = diff * (1.0 / D_MODEL)
        dt, dg = _rms_bwd(th, r, g, dh3)
        dg_a[...] += dg
        de_o[...] = (dt * pg).astype(de_o.dtype)
        dzg = (dt * e * pg * (1.0 - pg)).astype(MX)
        dzg_o[...] = dzg
        dh2_o[...] = dh3 + _dot_nt(dzg, w_pg[...])

    outs = [_sds(n_tok, D_MODEL, F32), _sds(n_tok, D_MODEL, MX), _sds(n_tok, D_MODEL, MX)]
    accs = [_sds(1, D_MODEL, F32), _sds(1, D_MODEL, F32)]
    return _tok_call("ple_fwd_bwd", body, n_tok, tm, [h2, p, target], [w["w_ple_proj"], w["w_ple_gate"], w["g_ple"]], outs, accs)


def _bwd_mlp(dh2, f, h1, zr, w, tm):
    n_tok = dh2.shape[0]

    def body(i, tin, res, tout, acc, scr):
        dh2_ref, f_ref, h1_ref, zr_ref = tin
        g_pre, w1, w2, g_post = res
        df_o, dz_o, dh1_o = tout
        dg_post_a, dg_pre_a = acc
        dh2_ = dh2_ref[...]
        gp = g_post[...]
        _, fh, rf = _rms(f_ref[...].astype(F32), gp)
        df, dg = _rms_bwd(fh, rf, gp, dh2_)
        dg_post_a[...] += dg
        df = df.astype(MX)
        df_o[...] = df
        dz = (_dot_nt(df, w2[...]) * (2.0 * zr_ref[...].astype(F32))).astype(MX)
        dz_o[...] = dz
        dm = _dot_nt(dz, w1[...])
        gq = g_pre[...]
        _, hh, rh = _rms(h1_ref[...], gq)
        dh1, dg = _rms_bwd(hh, rh, gq, dm)
        dg_pre_a[...] += dg
        dh1_o[...] = dh2_ + dh1

    outs = [_sds(n_tok, D_MODEL, MX), _sds(n_tok, D_FF, MX), _sds(n_tok, D_MODEL, F32)]
    accs = [_sds(1, D_MODEL, F32), _sds(1, D_MODEL, F32)]
    res = [w["g_pre_mlp"], w["w_ff1"], w["w_ff2"], w["g_post_mlp"]]
    return _tok_call("bwd_mlp", body, n_tok, tm, [dh2, f, h1, zr], res, outs, accs)


def _bwd_mix(dh1, y, a_br, p_br, gl, attn, d, w, tm):
    n_tok = dh1.shape[0]

    def body(i, tin, res, tout, acc, scr):
        dh1_ref, y_ref, a_ref, pp_ref, gl_ref, at_ref, d_ref = tin
        g_post, w_out, b_gate, w_ba, w_bp, w_pool, pool_scale, sel = res
        dy_o, da_o, dpp_o, dgl_o, do_o, delta_o, dyp_o, dd_o = tout
        dg_post_a, db_a, dps_a = acc
        g = g_post[...]
        _, yh, r = _rms(y_ref[...].astype(F32), g)
        dy, dg = _rms_bwd(yh, r, g, dh1_ref[...])
        dg_post_a[...] += dg
        dy = dy.astype(MX)
        dy_o[...] = dy
        dmerged = _dot_nt(dy, w_out[...])
        gates = _sig(gl_ref[...].astype(F32) + b_gate[...])
        ga, gp = gates[:, :D_MODEL], gates[:, D_MODEL:]
        da = (dmerged * ga).astype(MX)
        dpp = (dmerged * gp).astype(MX)
        da_o[...] = da
        dpp_o[...] = dpp
        dgl_a = dmerged * a_ref[...].astype(F32) * ga * (1.0 - ga)
        dgl_p = dmerged * pp_ref[...].astype(F32) * gp * (1.0 - gp)
        dgl_o[:, :D_MODEL] = dgl_a.astype(dgl_o.dtype)
        dgl_o[:, D_MODEL:] = dgl_p.astype(dgl_o.dtype)
        db_a[:, :D_MODEL] += jnp.sum(dgl_a, axis=0, keepdims=True)
        db_a[:, D_MODEL:] += jnp.sum(dgl_p, axis=0, keepdims=True)
        do = _dot_nt(da, w_ba[...]).astype(MX)
        do_o[...] = do
        prod = do.astype(F32) * at_ref[...].astype(F32)
        hi = prod.astype(MX)
        lo = (prod - hi.astype(F32)).astype(MX)
        delta_o[...] = _dot(hi, sel[...]) + _dot(lo, sel[...])
        dpooled = _dot_nt(dpp, w_bp[...])
        for gi in range(len(POOL_WINDOWS)):
            cs = slice(gi * POOL_GROUP, (gi + 1) * POOL_GROUP)
            ypre = _dot(d_ref[:, cs], w_pool[gi])
            dps_a[:, cs] += jnp.sum(dpooled[:, cs] * ypre, axis=0, keepdims=True)
            dyp = (dpooled[:, cs] * pool_scale[:, cs]).astype(MX)
            dyp_o[:, cs] = dyp
            dd_o[:, cs] = _dot_nt(dyp, w_pool[gi])

    outs = [_sds(n_tok, D_MODEL, MX), _sds(n_tok, D_MODEL, MX), _sds(n_tok, D_MODEL, MX), _sds(n_tok, 2 * D_MODEL, MX),
            _sds(n_tok, N_HEADS * V_HEAD, MX), _sds(n_tok, HEAD_SLOT, F32), _sds(n_tok, POOL_WIDTH, MX),
            _sds(n_tok, POOL_WIDTH, F32)]
    accs = [_sds(1, D_MODEL, F32), _sds(1, 2 * D_MODEL, F32), _sds(1, POOL_WIDTH, F32)]
    res = [w["g_post_mix"], w["w_out"], w["b_gate"], w["w_branch_attn"], w["w_branch_pool"], w["w_pool"], w["pool_scale"],
           w["head_sel"]]
    return _tok_call("bwd_mix", body, n_tok, tm, [dh1, y, a_br, p_br, gl, attn, d], res, outs, accs)


def _attn_bwd_dq(q, k, v, do, lse, delta, tq):
    n_tok = q.shape[0]
    n = n_tok // tq
    qi, kj = _causal_pairs(n, by_kv=False)

    def kern(qi_ref, kj_ref, q_ref, k_ref, v_ref, do_ref, lse_ref, dl_ref, dq_ref, dq_s):
        s_id = pl.program_id(0)
        i, j = qi_ref[s_id], kj_ref[s_id]

        @pl.when(j == 0)
        def _():
            dq_s[...] = jnp.zeros(dq_s.shape, F32)

        row = lax.broadcasted_iota(jnp.int32, (tq, tq), 0)
        col = lax.broadcasted_iota(jnp.int32, (tq, tq), 1)
        keep = jnp.logical_or(col <= row, j < i)
        for h in range(N_HEADS):
            hs = slice(h * HEAD_SLOT, (h + 1) * HEAD_SLOT)
            vs = slice(h * V_HEAD, (h + 1) * V_HEAD)
            s = jnp.where(keep, _dot_nt(q_ref[:, hs], k_ref[:, hs]), NEG)
            p = jnp.exp(s - lse_ref[:, h:h + 1])
            dp = _dot_nt(do_ref[:, vs], v_ref[:, vs])
            ds = p * (dp - dl_ref[:, h:h + 1])
            dq_s[:, hs] += _dot(ds, k_ref[:, hs])

        @pl.when(j == i)
        def _():
            dq_ref[...] = dq_s[...].astype(dq_ref.dtype)

    at_q = lambda s, qi, kj: (qi[s], 0)
    at_k = lambda s, qi, kj: (kj[s], 0)
    gs = pltpu.PrefetchScalarGridSpec(
        num_scalar_prefetch=2, grid=(qi.shape[0],),
        in_specs=[pl.BlockSpec((tq, QK_WIDTH), at_q), pl.BlockSpec((tq, QK_WIDTH), at_k),
                  pl.BlockSpec((tq, N_HEADS * V_HEAD), at_k), pl.BlockSpec((tq, N_HEADS * V_HEAD), at_q),
                  pl.BlockSpec((tq, HEAD_SLOT), at_q), pl.BlockSpec((tq, HEAD_SLOT), at_q)],
        out_specs=[pl.BlockSpec((tq, QK_WIDTH), at_q)],
        scratch_shapes=[pltpu.VMEM((tq, QK_WIDTH), F32)])
    return pl.pallas_call(kern, name="attn_bwd_dq", grid_spec=gs, out_shape=[_sds(n_tok, QK_WIDTH, MX)],
                          compiler_params=_params(("arbitrary",)))(qi, kj, q, k, v, do, lse, delta)[0]


def _attn_bwd_dkv(q, k, v, do, lse_t, delta_t, tq):
    n_tok = q.shape[0]
    n = n_tok // tq
    qi, kj = _causal_pairs(n, by_kv=True)

    def kern(qi_ref, kj_ref, q_ref, k_ref, v_ref, do_ref, lse_ref, dl_ref, dk_ref, dv_ref, dk_s, dv_s):
        s_id = pl.program_id(0)
        i, j = qi_ref[s_id], kj_ref[s_id]

        @pl.when(i == j)
        def _():
            dk_s[...] = jnp.zeros(dk_s.shape, F32)
            dv_s[...] = jnp.zeros(dv_s.shape, F32)

        krow = lax.broadcasted_iota(jnp.int32, (tq, tq), 0)
        qcol = lax.broadcasted_iota(jnp.int32, (tq, tq), 1)
        keep = jnp.logical_or(krow <= qcol, j < i)
        for h in range(N_HEADS):
            hs = slice(h * HEAD_SLOT, (h + 1) * HEAD_SLOT)
            vs = slice(h * V_HEAD, (h + 1) * V_HEAD)
            st = jnp.where(keep, _dot_nt(k_ref[:, hs], q_ref[:, hs]), NEG)
            pt = jnp.exp(st - lse_ref[h:h + 1, :])
            dv_s[h] += _dot(pt, do_ref[:, vs])
            dpt = _dot_nt(v_ref[:, vs], do_ref[:, vs])
            dst = pt * (dpt - dl_ref[h:h + 1, :])
            dk_s[:, hs] += _dot(dst, q_ref[:, hs])

        @pl.when(i == n - 1)
        def _():
            dk_ref[...] = dk_s[...].astype(dk_ref.dtype)
            for h in range(N_HEADS):
                dv_ref[:, h * V_HEAD:(h + 1) * V_HEAD] = dv_s[h].astype(dv_ref.dtype)

    at_q = lambda s, qi, kj: (qi[s], 0)
    at_k = lambda s, qi, kj: (kj[s], 0)
    at_qt = lambda s, qi, kj: (0, qi[s])
    gs = pltpu.PrefetchScalarGridSpec(
        num_scalar_prefetch=2, grid=(qi.shape[0],),
        in_specs=[pl.BlockSpec((tq, QK_WIDTH), at_q), pl.BlockSpec((tq, QK_WIDTH), at_k),
                  pl.BlockSpec((tq, N_HEADS * V_HEAD), at_k), pl.BlockSpec((tq, N_HEADS * V_HEAD), at_q),
                  pl.BlockSpec((N_HEADS, tq), at_qt), pl.BlockSpec((N_HEADS, tq), at_qt)],
        out_specs=[pl.BlockSpec((tq, QK_WIDTH), at_k), pl.BlockSpec((tq, N_HEADS * V_HEAD), at_k)],
        scratch_shapes=[pltpu.VMEM((tq, QK_WIDTH), F32), pltpu.VMEM((N_HEADS, tq, V_HEAD), F32)])
    return pl.pallas_call(kern, name="attn_bwd_dkv", grid_spec=gs,
                          out_shape=[_sds(n_tok, QK_WIDTH, MX), _sds(n_tok, N_HEADS * V_HEAD, MX)],
                          compiler_params=_params(("arbitrary",)))(qi, kj, q, k, v, do, lse_t, delta_t)


def _bwd_inproj(dq, dk, dv, dd, dgl, ps, x, dh1, cc, sa, w, tm):
    n_tok = x.shape[0]
    n_tiles = n_tok // tm
    last_halo = n_tok // HALO - 1
    halo_spec = pl.BlockSpec((HALO, POOL_WIDTH), lambda i: (jnp.minimum((i + 1) * (tm // HALO), last_halo), 0))

    def body(i, tin, res, tout, acc, scr):
        dq_ref, dk_ref, dv_ref, dd_ref, ddh_ref, dgl_ref, ps_ref, x_ref, dh1_ref, c_ref, s_ref = tin
        w_uq, g_q, w_k, w_v, e_mat, g_kv, w_in, g_pre = res
        dqu_o, dproj_o, dx_o = tout
        dgq_a, dgkv_a, dgpre_a = acc
        ext, = scr
        cc_, sa_ = c_ref[...], s_ref[...]
        for h in range(N_HEADS):
            hs = slice(h * HEAD_SLOT, (h + 1) * HEAD_SLOT)
            dqu_o[:, hs] = (_unrope(dq_ref[:, hs].astype(F32), cc_, sa_) * SCALE).astype(dqu_o.dtype)
        gq = g_q[...]
        _, qh, rq = _rms(ps_ref[:, :Q_LORA].astype(F32), gq)
        dqd, dg = _rms_bwd(qh, rq, gq, _dot_nt(dqu_o[...], w_uq[...]))
        dgq_a[...] += dg
        dproj_o[:, :Q_LORA] = dqd.astype(dproj_o.dtype)
        gkv = g_kv[...]
        _, kh, rk = _rms(ps_ref[:, Q_LORA:Q_LORA + KV_LORA].astype(F32), gkv)
        dkvd, dg = _rms_bwd(kh, rk, gkv, _dot_nt(dk_ref[...], w_k[...]) + _dot_nt(dv_ref[...], w_v[...]))
        dgkv_a[...] += dg
        dproj_o[:, Q_LORA:Q_LORA + KV_LORA] = dkvd.astype(dproj_o.dtype)
        dproj_o[:, Q_LORA + KV_LORA:SMALL_COLS] = _unrope(_dot_nt(dk_ref[...], e_mat[...]), cc_, sa_).astype(dproj_o.dtype)
        row = i * tm + lax.broadcasted_iota(jnp.int32, (tm + HALO, 1), 0)
        for gi, wdw in enumerate(POOL_WINDOWS):
            cs = slice(gi * POOL_GROUP, (gi + 1) * POOL_GROUP)
            inv = 1.0 / jnp.minimum(row + 1, wdw).astype(F32)
            ext[pl.ds(0, tm), cs] = dd_ref[:, cs] * inv[:tm]
            ext[pl.ds(tm, HALO), cs] = jnp.where(i < n_tiles - 1, ddh_ref[:, cs] * inv[tm:], 0.0)
            s = ext[pl.ds(0, tm), cs]
            for k_ in range(1, wdw):
                s = s + ext[pl.ds(k_, tm), cs]
            dproj_o[:, SMALL_COLS + gi * POOL_GROUP:SMALL_COLS + (gi + 1) * POOL_GROUP] = (s - dd_ref[:, cs]).astype(dproj_o.dtype)
        dproj_o[:, SMALL_COLS + POOL_WIDTH:] = dgl_ref[...]
        da = _dot_nt(dproj_o[...], w_in[...])
        gp = g_pre[...]
        _, xh, rx = _rms(x_ref[...], gp)
        dx, dg = _rms_bwd(xh, rx, gp, da)
        dgpre_a[...] += dg
        dx_o[...] = dh1_ref[...] + dx

    outs = [_sds(n_tok, QK_WIDTH, MX), _sds(n_tok, IN_PAD, MX), _sds(n_tok, D_MODEL, F32)]
    accs = [_sds(1, Q_LORA, F32), _sds(1, KV_LORA, F32), _sds(1, D_MODEL, F32)]
    res = [w["w_uq"], w["g_q"], w["w_k"], w["w_v"], w["e_mat"], w["g_kv"], w["w_in"], w["g_pre_mix"]]
    return _tok_call("bwd_inproj", body, n_tok, tm, [dq, dk, dv, dd, (dd, halo_spec), dgl, ps, x, dh1, cc, sa], res, outs, accs,
                     scratch=[pltpu.VMEM((tm + HALO, POOL_WIDTH), F32)])


def _xtdy(name, x, dy, bk, bt):
    n_tok, kk = x.shape
    nn = dy.shape[1]

    def kern(x_ref, dy_ref, o_ref):
        @pl.when(pl.program_id(1) == 0)
        def _():
            o_ref[...] = jnp.zeros(o_ref.shape, F32)
        o_ref[...] += _dot_tn(x_ref[...], dy_ref[...])

    return pl.pallas_call(
        kern, name=name, grid=(kk // bk, n_tok // bt),
        in_specs=[pl.BlockSpec((bt, bk), lambda a, t: (t, a)), pl.BlockSpec((bt, nn), lambda a, t: (t, 0))],
        out_specs=pl.BlockSpec((bk, nn), lambda a, t: (a, 0)), out_shape=_sds(kk, nn, F32),
        compiler_params=_params(("arbitrary", "arbitrary")))(x, dy)


def _rope_tables(positions):
    inv_freq = ROPE_THETA ** (-jnp.arange(0, QK_ROPE, 2, dtype=F32) / QK_ROPE)
    ang = positions.astype(F32)[:, None] * inv_freq
    cos, sin = jnp.cos(ang), jnp.sin(ang)
    n_tok = positions.shape[0]
    ones, z64 = jnp.ones((n_tok, ROPE_LANE), F32), jnp.zeros((n_tok, ROPE_LANE), F32)
    z32 = jnp.zeros((n_tok, HEAD_SLOT - ROPE_LANE - QK_ROPE), F32)
    return jnp.concatenate([ones, cos, cos, z32], 1), jnp.concatenate([z64, -sin, sin, z32], 1)


def _kernel_weights(full):
    w_in, w_uq, w_ukv = full["w_in"], full["w_uq"], full["w_ukv"]
    c0 = Q_LORA + KV_LORA
    z = lambda n: jnp.zeros((D_MODEL, n), w_in.dtype)
    w = dict(full)
    w["w_in"] = jnp.concatenate([w_in[:, :c0], z(ROPE_LANE), w_in[:, c0:c0 + QK_ROPE], z(HEAD_SLOT - ROPE_LANE - QK_ROPE),
                                 w_in[:, c0 + QK_ROPE:]], 1)
    w["w_uq"] = jnp.pad(w_uq.reshape(Q_LORA, N_HEADS, QK_NOPE + QK_ROPE),
                        ((0, 0), (0, 0), (0, HEAD_SLOT - QK_NOPE - QK_ROPE))).reshape(Q_LORA, QK_WIDTH)
    kv = w_ukv.reshape(KV_LORA, N_HEADS, QK_NOPE + V_HEAD)
    w["w_k"] = jnp.pad(kv[:, :, :QK_NOPE], ((0, 0), (0, 0), (0, HEAD_SLOT - QK_NOPE))).reshape(KV_LORA, QK_WIDTH)
    w["w_v"] = kv[:, :, QK_NOPE:].reshape(KV_LORA, N_HEADS * V_HEAD)
    e = np.zeros((HEAD_SLOT, QK_WIDTH), np.float32)
    sel = np.zeros((N_HEADS * V_HEAD, HEAD_SLOT), np.float32)
    for h in range(N_HEADS):
        for r in range(QK_ROPE):
            e[ROPE_LANE + r, h * HEAD_SLOT + ROPE_LANE + r] = 1.0
        sel[h * V_HEAD:(h + 1) * V_HEAD, h] = 1.0
    w["e_mat"] = jnp.asarray(e, MX)
    w["head_sel"] = jnp.asarray(sel, MX)
    w["w_pool"] = full["w_pool"].astype(MX)
    return w


def _local_step(x, p, positions, target, full):
    n_tok = x.shape[0]
    tm = min(512, n_tok)
    tm_mlp = min(256, n_tok)
    tq = min(512, n_tok)
    w = _kernel_weights(full)
    cc, sa = _rope_tables(positions)

    a, ps, u, gl, qn, kvn, q, k, v = _fwd_inproj(x, cc, sa, w, tm)
    attn, lse = _attn_fwd(q, k, v, tq)
    d, pooled, a_br, p_br, merged, y, h1 = _fwd_mix(x, u, gl, attn, w, tm)
    m, zr, a2, f, h2 = _fwd_mlp(h1, w, tm_mlp)
    dh2, de, dzg, loss_cols, dg_ple = _ple_fwd_bwd(h2, p, target, w, tm)
    df, dz, dh1, dg_post_mlp, dg_pre_mlp = _bwd_mlp(dh2, f, h1, zr, w, tm_mlp)
    dy, da_br, dp_br, dgl, do, delta, dyp, dd, dg_post_mix, db_gate, dpool_scale = _bwd_mix(dh1, y, a_br, p_br, gl, attn, d, w, tm)
    dq = _attn_bwd_dq(q, k, v, do, lse, delta, tq)
    dk, dv = _attn_bwd_dkv(q, k, v, do, lse[:, :N_HEADS].T, delta[:, :N_HEADS].T, tq)
    dqu, dproj, dx, dg_q, dg_kv, dg_pre_mix = _bwd_inproj(dq, dk, dv, dd, dgl, ps, x, dh1, cc, sa, w, tm)

    bt = min(512, n_tok)
    g_in = _xtdy("dw_in", a, dproj, 256, bt)
    g_uq = _xtdy("dw_uq", qn, dqu, Q_LORA, bt)
    g_k = _xtdy("dw_k", kvn, dk, KV_LORA, bt)
    g_v = _xtdy("dw_v", kvn, dv, KV_LORA, bt)
    g_pool = _xtdy("dw_pool", d, dyp, POOL_WIDTH, bt)
    g_ba = _xtdy("dw_ba", attn, da_br, 512, bt)
    g_bp = _xtdy("dw_bp", pooled, dp_br, 512, bt)
    g_out = _xtdy("dw_out", merged, dy, 512, bt)
    g_ff1 = _xtdy("dw_ff1", m, dz, 256, bt)
    g_ff2 = _xtdy("dw_ff2", a2, df, 1024, bt)
    g_pe = _xtdy("dw_pe", p, de, PLE_DIM, bt)
    g_pg = _xtdy("dw_pg", h2, dzg, 512, bt)

    c0 = Q_LORA + KV_LORA
    grads = {
        "g_pre_mix": dg_pre_mix,
        "w_in": jnp.concatenate([g_in[:, :c0], g_in[:, c0 + ROPE_LANE:c0 + ROPE_LANE + QK_ROPE], g_in[:, SMALL_COLS:]], 1),
        "b_gate": db_gate,
        "g_q": dg_q,
        "w_uq": g_uq.reshape(Q_LORA, N_HEADS, HEAD_SLOT)[:, :, :QK_NOPE + QK_ROPE].reshape(Q_LORA, N_HEADS * (QK_NOPE + QK_ROPE)),
        "g_kv": dg_kv,
        "w_ukv": jnp.concatenate([g_k.reshape(KV_LORA, N_HEADS, HEAD_SLOT)[:, :, :QK_NOPE],
                                  g_v.reshape(KV_LORA, N_HEADS, V_HEAD)], 2).reshape(KV_LORA, N_HEADS * (QK_NOPE + V_HEAD)),
        "w_pool": jnp.stack([g_pool[g * POOL_GROUP:(g + 1) * POOL_GROUP, g * POOL_GROUP:(g + 1) * POOL_GROUP]
                             for g in range(len(POOL_WINDOWS))]),
        "pool_scale": dpool_scale,
        "w_branch_attn": g_ba,
        "w_branch_pool": g_bp,
        "w_out": g_out,
        "g_post_mix": dg_post_mix,
        "g_pre_mlp": dg_pre_mlp,
        "w_ff1": g_ff1,
        "w_ff2": g_ff2,
        "g_post_mlp": dg_post_mlp,
        "w_ple_proj": g_pe,
        "w_ple_gate": g_pg,
        "g_ple": dg_ple,
    }
    return loss_cols, dx, grads


def _place():
    return lax.axis_index("x"), lax.axis_index("y"), lax.axis_index("c")


CHIP_FLIPS = ((1, 0), (0, 1), (1, 1))


def _flip(x, y, fx, fy):
    return (1 - x if fx else x), (1 - y if fy else y)


_HBM = pl.BlockSpec(memory_space=pl.ANY)


def _allgather_shards(wp):
    rows = wp.shape[0]
    half = rows // 2

    def body(w_ref, out_ref, send_sems, recv_sems, local_sem):
        x, y, c = _place()
        my_chip = 2 * x + y
        sibling = (x, y, 1 - c)

        def half_of(chip, hc):
            return out_ref.at[chip, pl.ds(pl.multiple_of(hc * half, 16), half), :]

        src = w_ref.at[pl.ds(pl.multiple_of(c * half, 16), half), :]
        mine = pltpu.make_async_copy(w_ref, out_ref.at[my_chip], local_sem)
        mine.start()
        chips = [_flip(x, y, fx, fy) for fx, fy in CHIP_FLIPS]
        first = []
        for j, (px, py) in enumerate(chips):
            cp = pltpu.make_async_remote_copy(src, half_of(my_chip, c), send_sems.at[j], recv_sems.at[j],
                                              device_id=(px, py, c), device_id_type=MESH)
            cp.start()
            first.append(cp)
        passed = []
        for j, (px, py) in enumerate(chips):
            landed = half_of(2 * px + py, c)
            pltpu.make_async_remote_copy(src, landed, send_sems.at[j], recv_sems.at[j],
                                         device_id=(px, py, c), device_id_type=MESH).wait_recv()
            cp = pltpu.make_async_remote_copy(landed, landed, send_sems.at[3 + j], recv_sems.at[3 + j],
                                              device_id=sibling, device_id_type=MESH)
            cp.start()
            passed.append(cp)
        for j, (px, py) in enumerate(chips):
            theirs = half_of(2 * px + py, 1 - c)
            pltpu.make_async_remote_copy(theirs, theirs, send_sems.at[3 + j], recv_sems.at[3 + j],
                                         device_id=sibling, device_id_type=MESH).wait_recv()
        for cp in first + passed:
            cp.wait_send()
        mine.wait()

    return pl.pallas_call(
        body, name="allgather_shards", out_shape=jax.ShapeDtypeStruct((N_CHIPS, rows, PACK_COLS), wp.dtype),
        in_specs=[_HBM], out_specs=_HBM,
        scratch_shapes=[pltpu.SemaphoreType.DMA((6,)), pltpu.SemaphoreType.DMA((6,)), pltpu.SemaphoreType.DMA],
    )(wp)


def _exchange_halves(g):
    rows = g.shape[1]
    half = rows // 2

    def body(g_ref, r_ref, send_sem, recv_sem):
        x, y, c = _place()
        src = g_ref.at[:, pl.ds(pl.multiple_of((1 - c) * half, 8), half), :]
        cp = pltpu.make_async_remote_copy(src, r_ref, send_sem, recv_sem, device_id=(x, y, 1 - c), device_id_type=MESH)
        cp.start()
        cp.wait()

    return pl.pallas_call(
        body, name="exchange_halves", out_shape=jax.ShapeDtypeStruct((N_CHIPS, half, PACK_COLS), g.dtype),
        in_specs=[_HBM], out_specs=_HBM, scratch_shapes=[pltpu.SemaphoreType.DMA, pltpu.SemaphoreType.DMA],
    )(g)


def _add_halves(g, r, c):
    rows = g.shape[1]
    half = rows // 2
    br = half // 8

    def kern(c_ref, g_ref, r_ref, o_ref):
        o_ref[...] = g_ref[...] + r_ref[...]

    gs = pltpu.PrefetchScalarGridSpec(
        num_scalar_prefetch=1, grid=(N_CHIPS, 8),
        in_specs=[pl.BlockSpec((1, br, PACK_COLS), lambda k, t, c: (k, c[0] * 8 + t, 0)),
                  pl.BlockSpec((1, br, PACK_COLS), lambda k, t, c: (k, t, 0))],
        out_specs=pl.BlockSpec((1, br, PACK_COLS), lambda k, t, c: (k, t, 0)))
    return pl.pallas_call(kern, name="add_halves", grid_spec=gs,
                          out_shape=jax.ShapeDtypeStruct((N_CHIPS, half, PACK_COLS), F32),
                          compiler_params=_params(("arbitrary", "arbitrary")))(c.reshape(1), g, r)


def _scatter_pieces(s):
    def body(s_ref, r_ref, send_sems, recv_sems, local_sem):
        x, y, c = _place()
        my_chip = 2 * x + y
        mine = pltpu.make_async_copy(s_ref.at[my_chip], r_ref.at[my_chip], local_sem)
        mine.start()
        chips = [_flip(x, y, fx, fy) for fx, fy in CHIP_FLIPS]
        sent = []
        for j, (px, py) in enumerate(chips):
            cp = pltpu.make_async_remote_copy(s_ref.at[2 * px + py], r_ref.at[my_chip], send_sems.at[j], recv_sems.at[j],
                                              device_id=(px, py, c), device_id_type=MESH)
            cp.start()
            sent.append(cp)
        for j, (px, py) in enumerate(chips):
            slot = r_ref.at[2 * px + py]
            pltpu.make_async_remote_copy(slot, slot, send_sems.at[j], recv_sems.at[j],
                                         device_id=(px, py, c), device_id_type=MESH).wait_recv()
        for cp in sent:
            cp.wait_send()
        mine.wait()

    return pl.pallas_call(
        body, name="scatter_pieces", out_shape=jax.ShapeDtypeStruct(s.shape, s.dtype), in_specs=[_HBM], out_specs=_HBM,
        scratch_shapes=[pltpu.SemaphoreType.DMA((3,)), pltpu.SemaphoreType.DMA((3,)), pltpu.SemaphoreType.DMA],
    )(s)


def _sum_pieces(r):
    half = r.shape[1]
    br = half // 8

    def kern(r_ref, o_ref):
        o_ref[...] = ((r_ref[0] + r_ref[1]) + r_ref[2]) + r_ref[3]

    return pl.pallas_call(
        kern, name="sum_pieces", grid=(8,), in_specs=[pl.BlockSpec((N_CHIPS, br, PACK_COLS), lambda t: (0, t, 0))],
        out_specs=pl.BlockSpec((br, PACK_COLS), lambda t: (t, 0)), out_shape=_sds(half, PACK_COLS, F32),
        compiler_params=_params(("arbitrary",)))(r)


def _join_halves(f):
    half = f.shape[0]

    def body(f_ref, o_ref, send_sem, recv_sem, local_sem):
        x, y, c = _place()
        here = o_ref.at[pl.ds(pl.multiple_of(c * half, 8), half), :]
        mine = pltpu.make_async_copy(f_ref, here, local_sem)
        mine.start()
        cp = pltpu.make_async_remote_copy(f_ref, here, send_sem, recv_sem, device_id=(x, y, 1 - c), device_id_type=MESH)
        cp.start()
        cp.wait()
        mine.wait()

    return pl.pallas_call(
        body, name="join_halves", out_shape=_sds(2 * half, PACK_COLS, f.dtype), in_specs=[_HBM], out_specs=_HBM,
        scratch_shapes=[pltpu.SemaphoreType.DMA, pltpu.SemaphoreType.DMA, pltpu.SemaphoreType.DMA],
    )(f)


def _allreduce_small(g):
    n_dev = 8

    def body(g_ref, o_ref, buf, send_sems, recv_sems):
        x, y, c = _place()
        me = 4 * x + 2 * y + c
        buf[me] = g_ref[...]
        peers = []
        for f in range(1, n_dev):
            px, py = _flip(x, y, f & 4, f & 2)
            pc = 1 - c if f & 1 else c
            peers.append((px, py, pc))
        sent = []
        for f, peer in enumerate(peers):
            cp = pltpu.make_async_remote_copy(g_ref, buf.at[me], send_sems.at[f], recv_sems.at[f], device_id=peer,
                                              device_id_type=MESH)
            cp.start()
            sent.append(cp)
        for f, (px, py, pc) in enumerate(peers):
            slot = buf.at[4 * px + 2 * py + pc]
            pltpu.make_async_remote_copy(slot, slot, send_sems.at[f], recv_sems.at[f], device_id=(px, py, pc),
                                         device_id_type=MESH).wait_recv()
        for cp in sent:
            cp.wait_send()
        total = buf[0]
        for k in range(1, n_dev):
            total = total + buf[k]
        o_ref[...] = total

    vmem = pl.BlockSpec(memory_space=pltpu.VMEM)
    return pl.pallas_call(
        body, name="allreduce_small", out_shape=jax.ShapeDtypeStruct(g.shape, g.dtype), in_specs=[vmem], out_specs=vmem,
        scratch_shapes=[pltpu.VMEM((n_dev,) + g.shape, g.dtype), pltpu.SemaphoreType.DMA((n_dev - 1,)),
                        pltpu.SemaphoreType.DMA((n_dev - 1,))],
    )(g)


def _adamw(g, w, m, v, br):
    rows = g.shape[0]
    c1 = 1.0 - ADAM_B1 ** ADAM_STEP
    c2 = 1.0 - ADAM_B2 ** ADAM_STEP

    def kern(g_ref, w_ref, m_ref, v_ref, d_o, m_o, v_o):
        g_ = g_ref[...]
        m_new = ADAM_B1 * m_ref[...] + (1.0 - ADAM_B1) * g_
        v_new = ADAM_B2 * v_ref[...] + (1.0 - ADAM_B2) * (g_ * g_)
        m_o[...] = m_new
        v_o[...] = v_new
        d_o[...] = -ADAM_LR * ((m_new / c1) / (jnp.sqrt(v_new / c2) + ADAM_EPS) + ADAM_WD * w_ref[...])

    spec = pl.BlockSpec((br, PACK_COLS), lambda t: (t, 0))
    out = _sds(rows, PACK_COLS, F32)
    return pl.pallas_call(kern, name="adamw", grid=(rows // br,), in_specs=[spec] * 4, out_specs=[spec] * 3,
                          out_shape=[out, out, out], compiler_params=_params(("arbitrary",)))(g, w, m, v)


def _shard_rows(shape, axis):
    k, n = shape
    return (k * n // N_CHIPS) // PACK_COLS


def _pack_shards(shards, dtype):
    parts = [shards[name].astype(dtype).reshape(-1, PACK_COLS) for name, _, _ in SHARDED]
    used = sum(p.shape[0] for p in parts)
    parts.append(jnp.zeros((BIG_ROWS - used, PACK_COLS), dtype))
    return jnp.concatenate(parts, 0)


def _unpack_shards(packed):
    out, r0 = {}, 0
    for name, (k, n), axis in SHARDED:
        nr = _shard_rows((k, n), axis)
        shape = (k // N_CHIPS, n) if axis == 0 else (k, n // N_CHIPS)
        out[name] = packed[r0:r0 + nr].reshape(shape)
        r0 += nr
    return out


def _unpack_full(gathered):
    out, r0 = {}, 0
    for name, (k, n), axis in SHARDED:
        nr = _shard_rows((k, n), axis)
        part = gathered[:, r0:r0 + nr]
        if axis == 0:
            out[name] = part.reshape(k, n)
        else:
            out[name] = part.reshape(N_CHIPS, k, n // N_CHIPS).transpose(1, 0, 2).reshape(k, n)
        r0 += nr
    return out


def _pack_pieces(grads):
    parts = []
    for name, (k, n), axis in SHARDED:
        g = grads[name]
        if axis == 0:
            parts.append(g.reshape(N_CHIPS, -1, PACK_COLS))
        else:
            parts.append(g.reshape(k, N_CHIPS, n // N_CHIPS).transpose(1, 0, 2).reshape(N_CHIPS, -1, PACK_COLS))
    used = sum(p.shape[1] for p in parts)
    parts.append(jnp.zeros((N_CHIPS, BIG_ROWS - used, PACK_COLS), F32))
    return jnp.concatenate(parts, 1)


def _pack_small(vals):
    flat = jnp.concatenate([vals[name].astype(F32).reshape(-1) for name, _ in SMALL])
    flat = jnp.concatenate([flat, jnp.zeros((SMALL_ROWS * PACK_COLS - flat.shape[0],), F32)])
    return flat.reshape(SMALL_ROWS, PACK_COLS)


def _unpack_small(packed):
    flat, out, o = packed.reshape(-1), {}, 0
    for name, shape in SMALL:
        n = int(np.prod(shape))
        out[name] = flat[o:o + n].reshape(shape)
        o += n
    return out


def kernel(x, p, positions, g_pre_mix, w_in, b_gate, g_q, w_uq, g_kv, w_ukv, w_pool, pool_scale, w_branch_attn, w_branch_pool, w_out, g_post_mix, g_pre_mlp, w_ff1, w_ff2, g_post_mlp, w_ple_proj, w_ple_gate, g_ple, loss_target, m_g_pre_mix, m_w_in, m_b_gate, m_g_q, m_w_uq, m_g_kv, m_w_ukv, m_w_pool, m_pool_scale, m_w_branch_attn, m_w_branch_pool, m_w_out, m_g_post_mix, m_g_pre_mlp, m_w_ff1, m_w_ff2, m_g_post_mlp, m_w_ple_proj, m_w_ple_gate, m_g_ple, v_g_pre_mix, v_w_in, v_b_gate, v_g_q, v_w_uq, v_g_kv, v_w_ukv, v_w_pool, v_pool_scale, v_w_branch_attn, v_w_branch_pool, v_w_out, v_g_post_mix, v_g_pre_mlp, v_w_ff1, v_w_ff2, v_g_post_mlp, v_w_ple_proj, v_w_ple_gate, v_g_ple):
    given = dict(locals())
    weights = {n: given[n] for n in WEIGHT_ORDER}
    moments_m = {n: given["m_" + n] for n in WEIGHT_ORDER}
    moments_v = {n: given["v_" + n] for n in WEIGHT_ORDER}
    c = lax.axis_index("c")

    big_w = {name: weights[name][0] for name, _, _ in SHARDED}
    gathered = _allgather_shards(_pack_shards(big_w, MX))
    full = _unpack_full(gathered)
    for name, _ in SMALL:
        full[name] = weights[name][0] if name == "w_pool" else weights[name]

    loss_cols, dx, grads = _local_step(x[0], p[0, 0], positions[0], loss_target[0], full)
    loss = lax.psum(0.5 * jnp.sum(loss_cols) / D_MODEL, ("x", "y", "c"))

    pieces = _pack_pieces(grads)
    summed = _add_halves(pieces, _exchange_halves(pieces), c)
    g_big = _join_halves(_sum_pieces(_scatter_pieces(summed)))
    g_small = _allreduce_small(_pack_small(grads))

    d_big, m_big, v_big = _adamw(g_big, _pack_shards(big_w, F32), _pack_shards({n: moments_m[n][0] for n, _, _ in SHARDED}, F32),
                                 _pack_shards({n: moments_v[n][0] for n, _, _ in SHARDED}, F32), BIG_ROWS // 10)
    d_small, m_small, v_small = _adamw(g_small, _pack_small(weights), _pack_small(moments_m), _pack_small(moments_v), SMALL_ROWS)

    def unpack(big, small):
        out = {n: a[None] for n, a in _unpack_shards(big).items()}
        out.update(_unpack_small(small))
        return [out[n] for n in WEIGHT_ORDER]

    return (loss, dx[None], *unpack(g_big, g_small), *unpack(d_big, d_small), *unpack(m_big, m_small), *unpack(v_big, v_small))
```

```python
import functools

import numpy as np
import jax
import jax.numpy as jnp
from jax import lax
from jax.experimental import pallas as pl
from jax.experimental.pallas import tpu as pltpu

F32 = jnp.float32
MX = jnp.bfloat16

D_MODEL = 1024
N_HEADS = 8
QK_NOPE = 64
QK_ROPE = 32
V_HEAD = 64
Q_LORA = 384
KV_LORA = 256
POOL_WINDOWS = (2, 4, 8, 16)
POOL_GROUP = 128
POOL_WIDTH = 512
D_FF = 4096
PLE_DIM = 256
ROPE_THETA = 10000.0
EPS = 1e-6
HEAD_SLOT = 128
QK_WIDTH = N_HEADS * HEAD_SLOT
ROPE_LANE = 64
SMALL_COLS = Q_LORA + KV_LORA + HEAD_SLOT
IN_PAD = SMALL_COLS + POOL_WIDTH + 2 * D_MODEL
SCALE = (QK_NOPE + QK_ROPE) ** -0.5
LOG2E = 1.4426950408889634
NEG = -1e30
HALO = 16

ADAM_LR = 0.001
ADAM_B1 = 0.9
ADAM_B2 = 0.999
ADAM_EPS = 1e-08
ADAM_WD = 0.01
ADAM_STEP = 10

VMEM_LIMIT = 56 * 2**20
MESH = pl.DeviceIdType.MESH

SHARDED = (
    ("w_in", (1024, 3232), 1),
    ("w_uq", (384, 768), 1),
    ("w_ukv", (256, 1024), 1),
    ("w_branch_attn", (512, 1024), 1),
    ("w_branch_pool", (512, 1024), 1),
    ("w_out", (1024, 1024), 0),
    ("w_ff1", (1024, 4096), 1),
    ("w_ff2", (4096, 1024), 0),
    ("w_ple_proj", (256, 1024), 1),
    ("w_ple_gate", (1024, 1024), 0),
)
SMALL = (
    ("g_pre_mix", (1, 1024)),
    ("b_gate", (1, 2048)),
    ("g_q", (1, 384)),
    ("g_kv", (1, 256)),
    ("w_pool", (1, 4, 128, 128)),
    ("pool_scale", (1, 512)),
    ("g_post_mix", (1, 1024)),
    ("g_pre_mlp", (1, 1024)),
    ("g_post_mlp", (1, 1024)),
    ("g_ple", (1, 1024)),
)
WEIGHT_ORDER = ("g_pre_mix", "w_in", "b_gate", "g_q", "w_uq", "g_kv", "w_ukv", "w_pool", "pool_scale", "w_branch_attn",
                "w_branch_pool", "w_out", "g_post_mix", "g_pre_mlp", "w_ff1", "w_ff2", "g_post_mlp", "w_ple_proj",
                "w_ple_gate", "g_ple")
N_CHIPS = 4
PACK_COLS = 1024
BIG_ROWS = 3840
SMALL_ROWS = 80


def _dot(a, b):
    return jnp.dot(a.astype(MX), b.astype(MX), preferred_element_type=F32)


def _dot_nt(a, b):
    return lax.dot_general(a.astype(MX), b.astype(MX), (((1,), (1,)), ((), ())), preferred_element_type=F32)


def _dot_tn(a, b):
    return lax.dot_general(a.astype(MX), b.astype(MX), (((0,), (0,)), ((), ())), preferred_element_type=F32)


def _sig(x):
    return 1.0 / (1.0 + jnp.exp(-x))


def _rms(x, g):
    r = lax.rsqrt(jnp.mean(x * x, axis=1, keepdims=True) + EPS)
    xh = x * r
    return xh * g, xh, r


def _rms_bwd(xh, r, g, dy):
    dxn = dy * g
    dx = r * (dxn - xh * jnp.mean(dxn * xh, axis=1, keepdims=True))
    return dx, jnp.sum(dy * xh, axis=0, keepdims=True)


def _rot_half(v):
    lane = lax.broadcasted_iota(jnp.int32, v.shape, 1)
    return jnp.where(lane < ROPE_LANE + QK_ROPE // 2, pltpu.roll(v, HEAD_SLOT - QK_ROPE // 2, 1), pltpu.roll(v, QK_ROPE // 2, 1))


def _rope(v, cc, sa):
    return v * cc + _rot_half(v) * sa


def _unrope(v, cc, sa):
    return v * cc - _rot_half(v) * sa


def _params(sem):
    return pltpu.CompilerParams(dimension_semantics=sem, vmem_limit_bytes=VMEM_LIMIT)


def _tok_call(name, body, n_tok, tm, tiled, resident, outs, accs=(), scratch=()):
    def as_pair(t):
        if isinstance(t, tuple):
            return t
        return t, pl.BlockSpec((tm, t.shape[1]), lambda i: (i, 0))
    tiled = [as_pair(t) for t in tiled]
    res_specs = [pl.BlockSpec(r.shape, lambda i, nd=r.ndim: (0,) * nd, pipeline_mode=pl.Buffered(1)) for r in resident]
    out_specs = [pl.BlockSpec((tm, o.shape[1]), lambda i: (i, 0)) for o in outs]
    out_specs += [pl.BlockSpec(a.shape, lambda i: (0, 0)) for a in accs]
    n_t, n_r, n_o, n_a = len(tiled), len(resident), len(outs), len(accs)

    def kern(*refs):
        tin, res = refs[:n_t], refs[n_t:n_t + n_r]
        tout = refs[n_t + n_r:n_t + n_r + n_o]
        acc = refs[n_t + n_r + n_o:n_t + n_r + n_o + n_a]
        scr = refs[n_t + n_r + n_o + n_a:]
        i = pl.program_id(0)

        @pl.when(i == 0)
        def _():
            for a in acc:
                a[...] = jnp.zeros(a.shape, a.dtype)
        body(i, tin, res, tout, acc, scr)

    return pl.pallas_call(
        kern, name=name, grid=(n_tok // tm,), in_specs=[s for _, s in tiled] + res_specs, out_specs=out_specs,
        out_shape=list(outs) + list(accs), scratch_shapes=list(scratch), compiler_params=_params(("arbitrary",)),
    )(*[a for a, _ in tiled], *resident)


def _sds(rows, cols, dtype):
    return jax.ShapeDtypeStruct((rows, cols), dtype)


def _fwd_inproj(x, cc, sa, w, tm):
    n_tok = x.shape[0]

    def body(i, tin, res, tout, acc, scr):
        x_ref, c_ref, s_ref = tin
        g_pre, w_in, g_q, w_uq, g_kv, w_k, w_v, e_mat = res
        a_o, ps_o, u_o, gl_o, qn_o, kvn_o, q_o, k_o, v_o = tout
        a = _rms(x_ref[...], g_pre[...])[0].astype(MX)
        a_o[...] = a
        ps = _dot(a, w_in[:, :SMALL_COLS])
        ps_o[...] = ps.astype(ps_o.dtype)
        u_o[...] = _dot(a, w_in[:, SMALL_COLS:SMALL_COLS + POOL_WIDTH]).astype(u_o.dtype)
        gl_o[...] = _dot(a, w_in[:, SMALL_COLS + POOL_WIDTH:]).astype(gl_o.dtype)
        cc_, sa_ = c_ref[...], s_ref[...]
        qn = _rms(ps[:, :Q_LORA], g_q[...])[0].astype(MX)
        qn_o[...] = qn
        q = _dot(qn, w_uq[...])
        for h in range(N_HEADS):
            hs = slice(h * HEAD_SLOT, (h + 1) * HEAD_SLOT)
            q_o[:, hs] = (_rope(q[:, hs], cc_, sa_) * (SCALE * LOG2E)).astype(q_o.dtype)
        kvn = _rms(ps[:, Q_LORA:Q_LORA + KV_LORA], g_kv[...])[0].astype(MX)
        kvn_o[...] = kvn
        kr = _rope(ps[:, Q_LORA + KV_LORA:], cc_, sa_)
        k_o[...] = (_dot(kvn, w_k[...]) + _dot(kr, e_mat[...])).astype(k_o.dtype)
        v_o[...] = _dot(kvn, w_v[...]).astype(v_o.dtype)

    outs = [_sds(n_tok, D_MODEL, MX), _sds(n_tok, SMALL_COLS, MX), _sds(n_tok, POOL_WIDTH, MX), _sds(n_tok, 2 * D_MODEL, MX),
            _sds(n_tok, Q_LORA, MX), _sds(n_tok, KV_LORA, MX), _sds(n_tok, QK_WIDTH, MX), _sds(n_tok, QK_WIDTH, MX),
            _sds(n_tok, N_HEADS * V_HEAD, MX)]
    res = [w["g_pre_mix"], w["w_in"], w["g_q"], w["w_uq"], w["g_kv"], w["w_k"], w["w_v"], w["e_mat"]]
    return _tok_call("fwd_inproj", body, n_tok, tm, [x, cc, sa], res, outs)


def _causal_pairs(n, by_kv):
    if by_kv:
        pairs = [(i, j) for j in range(n) for i in range(j, n)]
    else:
        pairs = [(i, j) for i in range(n) for j in range(i + 1)]
    return (jnp.asarray(np.array([p[0] for p in pairs], np.int32)), jnp.asarray(np.array([p[1] for p in pairs], np.int32)))


def _keep_t(tq):
    return lax.broadcasted_iota(jnp.int32, (tq, tq), 0) <= lax.broadcasted_iota(jnp.int32, (tq, tq), 1)


def _attn_fwd(q, k, vt, tq):
    n_tok = q.shape[0]
    n = n_tok // tq
    qi, kj = _causal_pairs(n, by_kv=False)

    def kern(qi_ref, kj_ref, q_ref, k_ref, vt_ref, ot_ref, lse_ref, m_s, l_s, acc_s, st_s):
        s_id = pl.program_id(0)
        i, j = qi_ref[s_id], kj_ref[s_id]

        @pl.when(j == 0)
        def _():
            m_s[...] = jnp.full(m_s.shape, NEG, F32)
            l_s[...] = jnp.zeros(l_s.shape, F32)
            acc_s[...] = jnp.zeros(acc_s.shape, F32)

        def scores(h):
            hs = slice(h * HEAD_SLOT, (h + 1) * HEAD_SLOT)
            return _dot_nt(k_ref[:, hs], q_ref[:, hs])

        def heads(masked):
            st_s[0] = scores(0)
            for h in range(N_HEADS):
                if h + 1 < N_HEADS:
                    st_s[(h + 1) % 2] = scores(h + 1)
                st = st_s[h % 2]
                if masked:
                    st = jnp.where(_keep_t(tq), st, NEG)
                m_old = m_s[h]
                m_new = jnp.maximum(m_old, jnp.max(st, axis=0, keepdims=True))
                alpha = jnp.exp2(m_old - m_new)
                pt = jnp.exp2(st - m_new)
                l_s[h] = alpha * l_s[h] + jnp.sum(pt, axis=0, keepdims=True)
                acc_s[h] = alpha * acc_s[h] + _dot(vt_ref[h * V_HEAD:(h + 1) * V_HEAD, :], pt)
                m_s[h] = m_new

        @pl.when(j < i)
        def _():
            heads(False)

        @pl.when(j == i)
        def _():
            heads(True)
            for h in range(N_HEADS):
                ot_ref[h * V_HEAD:(h + 1) * V_HEAD, :] = (acc_s[h] / l_s[h]).astype(ot_ref.dtype)
                lse_ref[h:h + 1, :] = m_s[h] + jnp.log2(l_s[h])

    gs = pltpu.PrefetchScalarGridSpec(
        num_scalar_prefetch=2, grid=(qi.shape[0],),
        in_specs=[pl.BlockSpec((tq, QK_WIDTH), lambda s, qi, kj: (qi[s], 0)),
                  pl.BlockSpec((tq, QK_WIDTH), lambda s, qi, kj: (kj[s], 0)),
                  pl.BlockSpec((N_HEADS * V_HEAD, tq), lambda s, qi, kj: (0, kj[s]))],
        out_specs=[pl.BlockSpec((N_HEADS * V_HEAD, tq), lambda s, qi, kj: (0, qi[s])),
                   pl.BlockSpec((N_HEADS, tq), lambda s, qi, kj: (0, qi[s]))],
        scratch_shapes=[pltpu.VMEM((N_HEADS, 1, tq), F32), pltpu.VMEM((N_HEADS, 1, tq), F32),
                        pltpu.VMEM((N_HEADS, V_HEAD, tq), F32), pltpu.VMEM((2, tq, tq), F32)])
    return pl.pallas_call(kern, name="attn_fwd", grid_spec=gs,
                          out_shape=[_sds(N_HEADS * V_HEAD, n_tok, MX), _sds(N_HEADS, n_tok, F32)],
                          compiler_params=_params(("arbitrary",)))(qi, kj, q, k, vt)


def _pool_windows(ext, i, tm, first_row):
    row = i * tm + lax.broadcasted_iota(jnp.int32, (tm, 1), 0)
    out = []
    for g, w in enumerate(POOL_WINDOWS):
        cs = slice(g * POOL_GROUP, (g + 1) * POOL_GROUP)
        s = ext[pl.ds(first_row, tm), cs]
        for k in range(1, w):
            s = s + ext[pl.ds(first_row - k, tm), cs]
        cnt = jnp.minimum(row + 1, w).astype(F32)
        out.append(s / cnt)
    return out


def _fwd_mix(x, u, gl, attn, w, tm):
    n_tok = x.shape[0]
    halo_spec = pl.BlockSpec((HALO, POOL_WIDTH), lambda i: (jnp.maximum(i * (tm // HALO) - 1, 0), 0))

    def body(i, tin, res, tout, acc, scr):
        x_ref, u_ref, uh_ref, gl_ref, at_ref = tin
        w_pool, pool_scale, w_ba, w_bp, b_gate, w_out, g_post = res
        d_o, pooled_o, a_o, pp_o, merged_o, y_o, h1_o = tout
        ext, = scr
        ext[pl.ds(0, HALO), :] = jnp.where(i > 0, uh_ref[...].astype(F32), 0.0)
        ext[pl.ds(HALO, tm), :] = u_ref[...].astype(F32)
        means = _pool_windows(ext, i, tm, HALO)
        for g in range(len(POOL_WINDOWS)):
            cs = slice(g * POOL_GROUP, (g + 1) * POOL_GROUP)
            d = (means[g] - ext[pl.ds(HALO, tm), cs]).astype(MX)
            d_o[:, cs] = d
            pooled_o[:, cs] = (_dot(d, w_pool[g]) * pool_scale[:, cs]).astype(pooled_o.dtype)
        a_br = _dot(at_ref[...], w_ba[...])
        p_br = _dot(pooled_o[...], w_bp[...])
        a_o[...] = a_br.astype(a_o.dtype)
        pp_o[...] = p_br.astype(pp_o.dtype)
        gates = _sig(gl_ref[...].astype(F32) + b_gate[...])
        merged = (gates[:, :D_MODEL] * a_br + gates[:, D_MODEL:] * p_br).astype(MX)
        merged_o[...] = merged
        y = _dot(merged, w_out[...])
        y_o[...] = y.astype(y_o.dtype)
        h1_o[...] = x_ref[...] + _rms(y, g_post[...])[0]

    outs = [_sds(n_tok, POOL_WIDTH, MX), _sds(n_tok, POOL_WIDTH, MX), _sds(n_tok, D_MODEL, MX), _sds(n_tok, D_MODEL, MX),
            _sds(n_tok, D_MODEL, MX), _sds(n_tok, D_MODEL, MX), _sds(n_tok, D_MODEL, F32)]
    res = [w["w_pool"], w["pool_scale"], w["w_branch_attn"], w["w_branch_pool"], w["b_gate"], w["w_out"], w["g_post_mix"]]
    return _tok_call("fwd_mix", body, n_tok, tm, [x, u, (u, halo_spec), gl, attn], res, outs,
                     scratch=[pltpu.VMEM((tm + HALO, POOL_WIDTH), F32)])


def _fwd_mlp(h1, w, tm):
    n_tok = h1.shape[0]

    def body(i, tin, res, tout, acc, scr):
        h1_ref, = tin
        g_pre, w1, w2, g_post = res
        m_o, zr_o, a2_o, f_o, h2_o = tout
        h1_ = h1_ref[...]
        m = _rms(h1_, g_pre[...])[0].astype(MX)
        m_o[...] = m
        zr = jnp.maximum(_dot(m, w1[...]), 0.0)
        zr_o[...] = zr.astype(zr_o.dtype)
        a2 = (zr * zr).astype(MX)
        a2_o[...] = a2
        f = _dot(a2, w2[...])
        f_o[...] = f.astype(f_o.dtype)
        h2_o[...] = h1_ + _rms(f, g_post[...])[0]

    outs = [_sds(n_tok, D_MODEL, MX), _sds(n_tok, D_FF, MX), _sds(n_tok, D_FF, MX), _sds(n_tok, D_MODEL, MX),
            _sds(n_tok, D_MODEL, F32)]
    res = [w["g_pre_mlp"], w["w_ff1"], w["w_ff2"], w["g_post_mlp"]]
    return _tok_call("fwd_mlp", body, n_tok, tm, [h1], res, outs)


def _ple_fwd_bwd(h2, p, target, w, tm):
    n_tok = h2.shape[0]

    def body(i, tin, res, tout, acc, scr):
        h2_ref, p_ref, t_ref = tin
        w_pe, w_pg, g_ple = res
        dh2_o, de_o, dzg_o = tout
        loss_a, dg_a = acc
        h2_ = h2_ref[...]
        e = _dot(p_ref[...], w_pe[...])
        pg = _sig(_dot(h2_, w_pg[...]))
        t = pg * e
        g = g_ple[...]
        tn, th, r = _rms(t, g)
        diff = h2_ + tn - t_ref[...]
        loss_a[...] += jnp.sum(diff * diff, axis=0, keepdims=True)
        dh3 = diff * (1.0 / D_MODEL)
        dt, dg = _rms_bwd(th, r, g, dh3)
        dg_a[...] += dg
        de_o[...] = (dt * pg).astype(de_o.dtype)
        dzg = (dt * e * pg * (1.0 - pg)).astype(MX)
        dzg_o[...] = dzg
        dh2_o[...] = dh3 + _dot_nt(dzg, w_pg[...])

    outs = [_sds(n_tok, D_MODEL, F32), _sds(n_tok, D_MODEL, MX), _sds(n_tok, D_MODEL, MX)]
    accs = [_sds(1, D_MODEL, F32), _sds(1, D_MODEL, F32)]
    return _tok_call("ple_fwd_bwd", body, n_tok, tm, [h2, p, target], [w["w_ple_proj"], w["w_ple_gate"], w["g_ple"]], outs, accs)


def _bwd_mlp(dh2, f, h1, zr, w, tm):
    n_tok = dh2.shape[0]

    def body(i, tin, res, tout, acc, scr):
        dh2_ref, f_ref, h1_ref, zr_ref = tin
        g_pre, w1, w2, g_post = res
        df_o, dz_o, dh1_o = tout
        dg_post_a, dg_pre_a = acc
        dh2_ = dh2_ref[...]
        gp = g_post[...]
        _, fh, rf = _rms(f_ref[...].astype(F32), gp)
        df, dg = _rms_bwd(fh, rf, gp, dh2_)
        dg_post_a[...] += dg
        df = df.astype(MX)
        df_o[...] = df
        dz = (_dot_nt(df, w2[...]) * (2.0 * zr_ref[...].astype(F32))).astype(MX)
        dz_o[...] = dz
        dm = _dot_nt(dz, w1[...])
        gq = g_pre[...]
        _, hh, rh = _rms(h1_ref[...], gq)
        dh1, dg = _rms_bwd(hh, rh, gq, dm)
        dg_pre_a[...] += dg
        dh1_o[...] = dh2_ + dh1

    outs = [_sds(n_tok, D_MODEL, MX), _sds(n_tok, D_FF, MX), _sds(n_tok, D_MODEL, F32)]
    accs = [_sds(1, D_MODEL, F32), _sds(1, D_MODEL, F32)]
    res = [w["g_pre_mlp"], w["w_ff1"], w["w_ff2"], w["g_post_mlp"]]
    return _tok_call("bwd_mlp", body, n_tok, tm, [dh2, f, h1, zr], res, outs, accs)


def _bwd_mix(dh1, y, a_br, p_br, gl, attn, d, w, tm):
    n_tok = dh1.shape[0]

    def body(i, tin, res, tout, acc, scr):
        dh1_ref, y_ref, a_ref, pp_ref, gl_ref, at_ref, d_ref = tin
        g_post, w_out, b_gate, w_ba, w_bp, w_pool, pool_scale, sel = res
        dy_o, da_o, dpp_o, dgl_o, do_o, delta_o, dyp_o, dd_o = tout
        dg_post_a, db_a, dps_a = acc
        g = g_post[...]
        _, yh, r = _rms(y_ref[...].astype(F32), g)
        dy, dg = _rms_bwd(yh, r, g, dh1_ref[...])
        dg_post_a[...] += dg
        dy = dy.astype(MX)
        dy_o[...] = dy
        dmerged = _dot_nt(dy, w_out[...])
        gates = _sig(gl_ref[...].astype(F32) + b_gate[...])
        ga, gp = gates[:, :D_MODEL], gates[:, D_MODEL:]
        da = (dmerged * ga).astype(MX)
        dpp = (dmerged * gp).astype(MX)
        da_o[...] = da
        dpp_o[...] = dpp
        dgl_a = dmerged * a_ref[...].astype(F32) * ga * (1.0 - ga)
        dgl_p = dmerged * pp_ref[...].astype(F32) * gp * (1.0 - gp)
        dgl_o[:, :D_MODEL] = dgl_a.astype(dgl_o.dtype)
        dgl_o[:, D_MODEL:] = dgl_p.astype(dgl_o.dtype)
        db_a[:, :D_MODEL] += jnp.sum(dgl_a, axis=0, keepdims=True)
        db_a[:, D_MODEL:] += jnp.sum(dgl_p, axis=0, keepdims=True)
        do = _dot_nt(da, w_ba[...]).astype(MX)
        do_o[...] = do
        prod = do.astype(F32) * at_ref[...].astype(F32)
        hi = prod.astype(MX)
        lo = (prod - hi.astype(F32)).astype(MX)
        delta_o[...] = _dot(hi, sel[...]) + _dot(lo, sel[...])
        dpooled = _dot_nt(dpp, w_bp[...])
        for gi in range(len(POOL_WINDOWS)):
            cs = slice(gi * POOL_GROUP, (gi + 1) * POOL_GROUP)
            ypre = _dot(d_ref[:, cs], w_pool[gi])
            dps_a[:, cs] += jnp.sum(dpooled[:, cs] * ypre, axis=0, keepdims=True)
            dyp = (dpooled[:, cs] * pool_scale[:, cs]).astype(MX)
            dyp_o[:, cs] = dyp
            dd_o[:, cs] = _dot_nt(dyp, w_pool[gi])

    outs = [_sds(n_tok, D_MODEL, MX), _sds(n_tok, D_MODEL, MX), _sds(n_tok, D_MODEL, MX), _sds(n_tok, 2 * D_MODEL, MX),
            _sds(n_tok, N_HEADS * V_HEAD, MX), _sds(n_tok, HEAD_SLOT, F32), _sds(n_tok, POOL_WIDTH, MX),
            _sds(n_tok, POOL_WIDTH, F32)]
    accs = [_sds(1, D_MODEL, F32), _sds(1, 2 * D_MODEL, F32), _sds(1, POOL_WIDTH, F32)]
    res = [w["g_post_mix"], w["w_out"], w["b_gate"], w["w_branch_attn"], w["w_branch_pool"], w["w_pool"], w["pool_scale"],
           w["head_sel"]]
    return _tok_call("bwd_mix", body, n_tok, tm, [dh1, y, a_br, p_br, gl, attn, d], res, outs, accs)


def _bwd_heads(q_ref, k_ref, v_ref, do_ref, lse_ref, dl_ref, st_s, dpt_s, masked, tq, use):
    def products(h):
        hs = slice(h * HEAD_SLOT, (h + 1) * HEAD_SLOT)
        vs = slice(h * V_HEAD, (h + 1) * V_HEAD)
        st_s[h % 2] = _dot_nt(k_ref[:, hs], q_ref[:, hs])
        dpt_s[h % 2] = _dot_nt(v_ref[:, vs], do_ref[:, vs])

    products(0)
    for h in range(N_HEADS):
        if h + 1 < N_HEADS:
            products(h + 1)
        st = st_s[h % 2]
        if masked:
            st = jnp.where(_keep_t(tq), st, NEG)
        pt = jnp.exp2(st - lse_ref[h:h + 1, :])
        use(h, pt, pt * (dpt_s[h % 2] - dl_ref[h:h + 1, :]))


def _attn_bwd_dq(q, k, kt, v, do, lse, delta, tq):
    n_tok = q.shape[0]
    n = n_tok // tq
    qi, kj = _causal_pairs(n, by_kv=False)

    def kern(qi_ref, kj_ref, q_ref, k_ref, kt_ref, v_ref, do_ref, lse_ref, dl_ref, dq_ref, dq_s, st_s, dpt_s):
        s_id = pl.program_id(0)
        i, j = qi_ref[s_id], kj_ref[s_id]

        @pl.when(j == 0)
        def _():
            dq_s[...] = jnp.zeros(dq_s.shape, F32)

        def use(h, pt, dst):
            hs = slice(h * HEAD_SLOT, (h + 1) * HEAD_SLOT)
            dq_s[hs, :] += _dot(kt_ref[hs, :], dst)

        def heads(masked):
            _bwd_heads(q_ref, k_ref, v_ref, do_ref, lse_ref, dl_ref, st_s, dpt_s, masked, tq, use)

        @pl.when(j < i)
        def _():
            heads(False)

        @pl.when(j == i)
        def _():
            heads(True)
            dq_ref[...] = dq_s[...].astype(dq_ref.dtype)

    at_q = lambda s, qi, kj: (qi[s], 0)
    at_k = lambda s, qi, kj: (kj[s], 0)
    at_qt = lambda s, qi, kj: (0, qi[s])
    at_kt = lambda s, qi, kj: (0, kj[s])
    gs = pltpu.PrefetchScalarGridSpec(
        num_scalar_prefetch=2, grid=(qi.shape[0],),
        in_specs=[pl.BlockSpec((tq, QK_WIDTH), at_q), pl.BlockSpec((tq, QK_WIDTH), at_k), pl.BlockSpec((QK_WIDTH, tq), at_kt),
                  pl.BlockSpec((tq, N_HEADS * V_HEAD), at_k), pl.BlockSpec((tq, N_HEADS * V_HEAD), at_q),
                  pl.BlockSpec((N_HEADS, tq), at_qt), pl.BlockSpec((N_HEADS, tq), at_qt)],
        out_specs=[pl.BlockSpec((QK_WIDTH, tq), at_qt)],
        scratch_shapes=[pltpu.VMEM((QK_WIDTH, tq), F32), pltpu.VMEM((2, tq, tq), F32), pltpu.VMEM((2, tq, tq), F32)])
    return pl.pallas_call(kern, name="attn_bwd_dq", grid_spec=gs, out_shape=[_sds(QK_WIDTH, n_tok, MX)],
                          compiler_params=_params(("arbitrary",)))(qi, kj, q, k, kt, v, do, lse, delta)[0]


def _attn_bwd_dkv(q, k, v, do, lse_t, delta_t, tq):
    n_tok = q.shape[0]
    n = n_tok // tq
    qi, kj = _causal_pairs(n, by_kv=True)

    def kern(qi_ref, kj_ref, q_ref, k_ref, v_ref, do_ref, lse_ref, dl_ref, dk_ref, dv_ref, dk_s, dv_s, st_s, dpt_s):
        s_id = pl.program_id(0)
        i, j = qi_ref[s_id], kj_ref[s_id]

        def use(h, pt, dst):
            hs = slice(h * HEAD_SLOT, (h + 1) * HEAD_SLOT)
            dv_s[h] += _dot(pt, do_ref[:, h * V_HEAD:(h + 1) * V_HEAD])
            dk_s[:, hs] += _dot(dst, q_ref[:, hs])

        def heads(masked):
            _bwd_heads(q_ref, k_ref, v_ref, do_ref, lse_ref, dl_ref, st_s, dpt_s, masked, tq, use)

        @pl.when(i == j)
        def _():
            dk_s[...] = jnp.zeros(dk_s.shape, F32)
            dv_s[...] = jnp.zeros(dv_s.shape, F32)
            heads(True)

        @pl.when(i > j)
        def _():
            heads(False)

        @pl.when(i == n - 1)
        def _():
            dk_ref[...] = (dk_s[...] * (1.0 / LOG2E)).astype(dk_ref.dtype)
            for h in range(N_HEADS):
                dv_ref[:, h * V_HEAD:(h + 1) * V_HEAD] = dv_s[h].astype(dv_ref.dtype)

    at_q = lambda s, qi, kj: (qi[s], 0)
    at_k = lambda s, qi, kj: (kj[s], 0)
    at_qt = lambda s, qi, kj: (0, qi[s])
    gs = pltpu.PrefetchScalarGridSpec(
        num_scalar_prefetch=2, grid=(qi.shape[0],),
        in_specs=[pl.BlockSpec((tq, QK_WIDTH), at_q), pl.BlockSpec((tq, QK_WIDTH), at_k),
                  pl.BlockSpec((tq, N_HEADS * V_HEAD), at_k), pl.BlockSpec((tq, N_HEADS * V_HEAD), at_q),
                  pl.BlockSpec((N_HEADS, tq), at_qt), pl.BlockSpec((N_HEADS, tq), at_qt)],
        out_specs=[pl.BlockSpec((tq, QK_WIDTH), at_k), pl.BlockSpec((tq, N_HEADS * V_HEAD), at_k)],
        scratch_shapes=[pltpu.VMEM((tq, QK_WIDTH), F32), pltpu.VMEM((N_HEADS, tq, V_HEAD), F32),
                        pltpu.VMEM((2, tq, tq), F32), pltpu.VMEM((2, tq, tq), F32)])
    return pl.pallas_call(kern, name="attn_bwd_dkv", grid_spec=gs,
                          out_shape=[_sds(n_tok, QK_WIDTH, MX), _sds(n_tok, N_HEADS * V_HEAD, MX)],
                          compiler_params=_params(("arbitrary",)))(qi, kj, q, k, v, do, lse_t, delta_t)


def _bwd_inproj(dq, dk, dv, dd, dgl, ps, x, dh1, cc, sa, w, tm):
    n_tok = x.shape[0]
    n_tiles = n_tok // tm
    last_halo = n_tok // HALO - 1
    halo_spec = pl.BlockSpec((HALO, POOL_WIDTH), lambda i: (jnp.minimum((i + 1) * (tm // HALO), last_halo), 0))

    def body(i, tin, res, tout, acc, scr):
        dq_ref, dk_ref, dv_ref, dd_ref, ddh_ref, dgl_ref, ps_ref, x_ref, dh1_ref, c_ref, s_ref = tin
        w_uq, g_q, w_k, w_v, e_mat, g_kv, w_in, g_pre = res
        dqu_o, dproj_o, dx_o = tout
        dgq_a, dgkv_a, dgpre_a = acc
        ext, = scr
        cc_, sa_ = c_ref[...], s_ref[...]
        for h in range(N_HEADS):
            hs = slice(h * HEAD_SLOT, (h + 1) * HEAD_SLOT)
            dqu_o[:, hs] = (_unrope(dq_ref[:, hs].astype(F32), cc_, sa_) * SCALE).astype(dqu_o.dtype)
        gq = g_q[...]
        _, qh, rq = _rms(ps_ref[:, :Q_LORA].astype(F32), gq)
        dqd, dg = _rms_bwd(qh, rq, gq, _dot_nt(dqu_o[...], w_uq[...]))
        dgq_a[...] += dg
        dproj_o[:, :Q_LORA] = dqd.astype(dproj_o.dtype)
        gkv = g_kv[...]
        _, kh, rk = _rms(ps_ref[:, Q_LORA:Q_LORA + KV_LORA].astype(F32), gkv)
        dkvd, dg = _rms_bwd(kh, rk, gkv, _dot_nt(dk_ref[...], w_k[...]) + _dot_nt(dv_ref[...], w_v[...]))
        dgkv_a[...] += dg
        dproj_o[:, Q_LORA:Q_LORA + KV_LORA] = dkvd.astype(dproj_o.dtype)
        dproj_o[:, Q_LORA + KV_LORA:SMALL_COLS] = _unrope(_dot_nt(dk_ref[...], e_mat[...]), cc_, sa_).astype(dproj_o.dtype)
        row = i * tm + lax.broadcasted_iota(jnp.int32, (tm + HALO, 1), 0)
        for gi, wdw in enumerate(POOL_WINDOWS):
            cs = slice(gi * POOL_GROUP, (gi + 1) * POOL_GROUP)
            inv = 1.0 / jnp.minimum(row + 1, wdw).astype(F32)
            ext[pl.ds(0, tm), cs] = dd_ref[:, cs] * inv[:tm]
            ext[pl.ds(tm, HALO), cs] = jnp.where(i < n_tiles - 1, ddh_ref[:, cs] * inv[tm:], 0.0)
            s = ext[pl.ds(0, tm), cs]
            for k_ in range(1, wdw):
                s = s + ext[pl.ds(k_, tm), cs]
            dproj_o[:, SMALL_COLS + gi * POOL_GROUP:SMALL_COLS + (gi + 1) * POOL_GROUP] = (s - dd_ref[:, cs]).astype(dproj_o.dtype)
        dproj_o[:, SMALL_COLS + POOL_WIDTH:] = dgl_ref[...]
        da = _dot_nt(dproj_o[...], w_in[...])
        gp = g_pre[...]
        _, xh, rx = _rms(x_ref[...], gp)
        dx, dg = _rms_bwd(xh, rx, gp, da)
        dgpre_a[...] += dg
        dx_o[...] = dh1_ref[...] + dx

    outs = [_sds(n_tok, QK_WIDTH, MX), _sds(n_tok, IN_PAD, MX), _sds(n_tok, D_MODEL, F32)]
    accs = [_sds(1, Q_LORA, F32), _sds(1, KV_LORA, F32), _sds(1, D_MODEL, F32)]
    res = [w["w_uq"], w["g_q"], w["w_k"], w["w_v"], w["e_mat"], w["g_kv"], w["w_in"], w["g_pre_mix"]]
    return _tok_call("bwd_inproj", body, n_tok, tm, [dq, dk, dv, dd, (dd, halo_spec), dgl, ps, x, dh1, cc, sa], res, outs, accs,
                     scratch=[pltpu.VMEM((tm + HALO, POOL_WIDTH), F32)])


def _xtdy(name, x, dy, bk, bt):
    n_tok, kk = x.shape
    nn = dy.shape[1]

    def kern(x_ref, dy_ref, o_ref):
        @pl.when(pl.program_id(1) == 0)
        def _():
            o_ref[...] = jnp.zeros(o_ref.shape, F32)
        o_ref[...] += _dot_tn(x_ref[...], dy_ref[...])

    return pl.pallas_call(
        kern, name=name, grid=(kk // bk, n_tok // bt),
        in_specs=[pl.BlockSpec((bt, bk), lambda a, t: (t, a)), pl.BlockSpec((bt, nn), lambda a, t: (t, 0))],
        out_specs=pl.BlockSpec((bk, nn), lambda a, t: (a, 0)), out_shape=_sds(kk, nn, F32),
        compiler_params=_params(("arbitrary", "arbitrary")))(x, dy)


def _rope_tables(positions):
    inv_freq = ROPE_THETA ** (-jnp.arange(0, QK_ROPE, 2, dtype=F32) / QK_ROPE)
    ang = positions.astype(F32)[:, None] * inv_freq
    cos, sin = jnp.cos(ang), jnp.sin(ang)
    n_tok = positions.shape[0]
    ones, z64 = jnp.ones((n_tok, ROPE_LANE), F32), jnp.zeros((n_tok, ROPE_LANE), F32)
    z32 = jnp.zeros((n_tok, HEAD_SLOT - ROPE_LANE - QK_ROPE), F32)
    return jnp.concatenate([ones, cos, cos, z32], 1), jnp.concatenate([z64, -sin, sin, z32], 1)


def _kernel_weights(full):
    w_in, w_uq, w_ukv = full["w_in"], full["w_uq"], full["w_ukv"]
    c0 = Q_LORA + KV_LORA
    z = lambda n: jnp.zeros((D_MODEL, n), w_in.dtype)
    w = dict(full)
    w["w_in"] = jnp.concatenate([w_in[:, :c0], z(ROPE_LANE), w_in[:, c0:c0 + QK_ROPE], z(HEAD_SLOT - ROPE_LANE - QK_ROPE),
                                 w_in[:, c0 + QK_ROPE:]], 1)
    w["w_uq"] = jnp.pad(w_uq.reshape(Q_LORA, N_HEADS, QK_NOPE + QK_ROPE),
                        ((0, 0), (0, 0), (0, HEAD_SLOT - QK_NOPE - QK_ROPE))).reshape(Q_LORA, QK_WIDTH)
    kv = w_ukv.reshape(KV_LORA, N_HEADS, QK_NOPE + V_HEAD)
    w["w_k"] = jnp.pad(kv[:, :, :QK_NOPE], ((0, 0), (0, 0), (0, HEAD_SLOT - QK_NOPE))).reshape(KV_LORA, QK_WIDTH)
    w["w_v"] = kv[:, :, QK_NOPE:].reshape(KV_LORA, N_HEADS * V_HEAD)
    e = np.zeros((HEAD_SLOT, QK_WIDTH), np.float32)
    sel = np.zeros((N_HEADS * V_HEAD, HEAD_SLOT), np.float32)
    for h in range(N_HEADS):
        for r in range(QK_ROPE):
            e[ROPE_LANE + r, h * HEAD_SLOT + ROPE_LANE + r] = 1.0
        sel[h * V_HEAD:(h + 1) * V_HEAD, h] = 1.0
    w["e_mat"] = jnp.asarray(e, MX)
    w["head_sel"] = jnp.asarray(sel, MX)
    w["w_pool"] = full["w_pool"].astype(MX)
    return w


def _local_step(x, p, positions, target, full):
    n_tok = x.shape[0]
    tm = min(512, n_tok)
    tm_mlp = min(256, n_tok)
    tq = min(512, n_tok)
    w = _kernel_weights(full)
    cc, sa = _rope_tables(positions)

    a, ps, u, gl, qn, kvn, q, k, v = _fwd_inproj(x, cc, sa, w, tm)
    attn_t, lse = _attn_fwd(q, k, v.T, tq)
    attn = attn_t.T
    d, pooled, a_br, p_br, merged, y, h1 = _fwd_mix(x, u, gl, attn, w, tm)
    m, zr, a2, f, h2 = _fwd_mlp(h1, w, tm_mlp)
    dh2, de, dzg, loss_cols, dg_ple = _ple_fwd_bwd(h2, p, target, w, tm)
    df, dz, dh1, dg_post_mlp, dg_pre_mlp = _bwd_mlp(dh2, f, h1, zr, w, tm_mlp)
    dy, da_br, dp_br, dgl, do, delta, dyp, dd, dg_post_mix, db_gate, dpool_scale = _bwd_mix(dh1, y, a_br, p_br, gl, attn, d, w, tm)
    delta_t = delta[:, :N_HEADS].T
    dq = _attn_bwd_dq(q, k, k.T, v, do, lse, delta_t, tq).T
    dk, dv = _attn_bwd_dkv(q, k, v, do, lse, delta_t, tq)
    dqu, dproj, dx, dg_q, dg_kv, dg_pre_mix = _bwd_inproj(dq, dk, dv, dd, dgl, ps, x, dh1, cc, sa, w, tm)

    bt = min(512, n_tok)
    g_in = _xtdy("dw_in", a, dproj, 256, bt)
    g_uq = _xtdy("dw_uq", qn, dqu, Q_LORA, bt)
    g_k = _xtdy("dw_k", kvn, dk, KV_LORA, bt)
    g_v = _xtdy("dw_v", kvn, dv, KV_LORA, bt)
    g_pool = _xtdy("dw_pool", d, dyp, POOL_WIDTH, bt)
    g_ba = _xtdy("dw_ba", attn, da_br, 512, bt)
    g_bp = _xtdy("dw_bp", pooled, dp_br, 512, bt)
    g_out = _xtdy("dw_out", merged, dy, 512, bt)
    g_ff1 = _xtdy("dw_ff1", m, dz, 256, bt)
    g_ff2 = _xtdy("dw_ff2", a2, df, 1024, bt)
    g_pe = _xtdy("dw_pe", p, de, PLE_DIM, bt)
    g_pg = _xtdy("dw_pg", h2, dzg, 512, bt)

    c0 = Q_LORA + KV_LORA
    grads = {
        "g_pre_mix": dg_pre_mix,
        "w_in": jnp.concatenate([g_in[:, :c0], g_in[:, c0 + ROPE_LANE:c0 + ROPE_LANE + QK_ROPE], g_in[:, SMALL_COLS:]], 1),
        "b_gate": db_gate,
        "g_q": dg_q,
        "w_uq": g_uq.reshape(Q_LORA, N_HEADS, HEAD_SLOT)[:, :, :QK_NOPE + QK_ROPE].reshape(Q_LORA, N_HEADS * (QK_NOPE + QK_ROPE)),
        "g_kv": dg_kv,
        "w_ukv": jnp.concatenate([g_k.reshape(KV_LORA, N_HEADS, HEAD_SLOT)[:, :, :QK_NOPE],
                                  g_v.reshape(KV_LORA, N_HEADS, V_HEAD)], 2).reshape(KV_LORA, N_HEADS * (QK_NOPE + V_HEAD)),
        "w_pool": jnp.stack([g_pool[g * POOL_GROUP:(g + 1) * POOL_GROUP, g * POOL_GROUP:(g + 1) * POOL_GROUP]
                             for g in range(len(POOL_WINDOWS))]),
        "pool_scale": dpool_scale,
        "w_branch_attn": g_ba,
        "w_branch_pool": g_bp,
        "w_out": g_out,
        "g_post_mix": dg_post_mix,
        "g_pre_mlp": dg_pre_mlp,
        "w_ff1": g_ff1,
        "w_ff2": g_ff2,
        "g_post_mlp": dg_post_mlp,
        "w_ple_proj": g_pe,
        "w_ple_gate": g_pg,
        "g_ple": dg_ple,
    }
    return loss_cols, dx, grads


def _place():
    return lax.axis_index("x"), lax.axis_index("y"), lax.axis_index("c")


CHIP_FLIPS = ((1, 0), (0, 1), (1, 1))


def _flip(x, y, fx, fy):
    return (1 - x if fx else x), (1 - y if fy else y)


_HBM = pl.BlockSpec(memory_space=pl.ANY)


def _allgather_shards(wp):
    rows = wp.shape[0]
    half = rows // 2

    def body(w_ref, out_ref, send_sems, recv_sems, local_sem):
        x, y, c = _place()
        my_chip = 2 * x + y
        sibling = (x, y, 1 - c)

        def half_of(chip, hc):
            return out_ref.at[chip, pl.ds(pl.multiple_of(hc * half, 16), half), :]

        src = w_ref.at[pl.ds(pl.multiple_of(c * half, 16), half), :]
        mine = pltpu.make_async_copy(w_ref, out_ref.at[my_chip], local_sem)
        mine.start()
        chips = [_flip(x, y, fx, fy) for fx, fy in CHIP_FLIPS]
        first = []
        for j, (px, py) in enumerate(chips):
            cp = pltpu.make_async_remote_copy(src, half_of(my_chip, c), send_sems.at[j], recv_sems.at[j],
                                              device_id=(px, py, c), device_id_type=MESH)
            cp.start()
            first.append(cp)
        passed = []
        for j, (px, py) in enumerate(chips):
            landed = half_of(2 * px + py, c)
            pltpu.make_async_remote_copy(src, landed, send_sems.at[j], recv_sems.at[j],
                                         device_id=(px, py, c), device_id_type=MESH).wait_recv()
            cp = pltpu.make_async_remote_copy(landed, landed, send_sems.at[3 + j], recv_sems.at[3 + j],
                                              device_id=sibling, device_id_type=MESH)
            cp.start()
            passed.append(cp)
        for j, (px, py) in enumerate(chips):
            theirs = half_of(2 * px + py, 1 - c)
            pltpu.make_async_remote_copy(theirs, theirs, send_sems.at[3 + j], recv_sems.at[3 + j],
                                         device_id=sibling, device_id_type=MESH).wait_recv()
        for cp in first + passed:
            cp.wait_send()
        mine.wait()

    return pl.pallas_call(
        body, name="allgather_shards", out_shape=jax.ShapeDtypeStruct((N_CHIPS, rows, PACK_COLS), wp.dtype),
        in_specs=[_HBM], out_specs=_HBM,
        scratch_shapes=[pltpu.SemaphoreType.DMA((6,)), pltpu.SemaphoreType.DMA((6,)), pltpu.SemaphoreType.DMA],
    )(wp)


def _exchange_halves(g):
    rows = g.shape[1]
    half = rows // 2

    def body(g_ref, r_ref, send_sem, recv_sem):
        x, y, c = _place()
        src = g_ref.at[:, pl.ds(pl.multiple_of((1 - c) * half, 8), half), :]
        cp = pltpu.make_async_remote_copy(src, r_ref, send_sem, recv_sem, device_id=(x, y, 1 - c), device_id_type=MESH)
        cp.start()
        cp.wait()

    return pl.pallas_call(
        body, name="exchange_halves", out_shape=jax.ShapeDtypeStruct((N_CHIPS, half, PACK_COLS), g.dtype),
        in_specs=[_HBM], out_specs=_HBM, scratch_shapes=[pltpu.SemaphoreType.DMA, pltpu.SemaphoreType.DMA],
    )(g)


def _add_halves(g, r, c):
    rows = g.shape[1]
    half = rows // 2
    br = half // 8

    def kern(c_ref, g_ref, r_ref, o_ref):
        o_ref[...] = g_ref[...] + r_ref[...]

    gs = pltpu.PrefetchScalarGridSpec(
        num_scalar_prefetch=1, grid=(N_CHIPS, 8),
        in_specs=[pl.BlockSpec((1, br, PACK_COLS), lambda k, t, c: (k, c[0] * 8 + t, 0)),
                  pl.BlockSpec((1, br, PACK_COLS), lambda k, t, c: (k, t, 0))],
        out_specs=pl.BlockSpec((1, br, PACK_COLS), lambda k, t, c: (k, t, 0)))
    return pl.pallas_call(kern, name="add_halves", grid_spec=gs,
                          out_shape=jax.ShapeDtypeStruct((N_CHIPS, half, PACK_COLS), F32),
                          compiler_params=_params(("arbitrary", "arbitrary")))(c.reshape(1), g, r)


def _scatter_pieces(s):
    def body(s_ref, r_ref, send_sems, recv_sems, local_sem):
        x, y, c = _place()
        my_chip = 2 * x + y
        mine = pltpu.make_async_copy(s_ref.at[my_chip], r_ref.at[my_chip], local_sem)
        mine.start()
        chips = [_flip(x, y, fx, fy) for fx, fy in CHIP_FLIPS]
        sent = []
        for j, (px, py) in enumerate(chips):
            cp = pltpu.make_async_remote_copy(s_ref.at[2 * px + py], r_ref.at[my_chip], send_sems.at[j], recv_sems.at[j],
                                              device_id=(px, py, c), device_id_type=MESH)
            cp.start()
            sent.append(cp)
        for j, (px, py) in enumerate(chips):
            slot = r_ref.at[2 * px + py]
            pltpu.make_async_remote_copy(slot, slot, send_sems.at[j], recv_sems.at[j],
                                         device_id=(px, py, c), device_id_type=MESH).wait_recv()
        for cp in sent:
            cp.wait_send()
        mine.wait()

    return pl.pallas_call(
        body, name="scatter_pieces", out_shape=jax.ShapeDtypeStruct(s.shape, s.dtype), in_specs=[_HBM], out_specs=_HBM,
        scratch_shapes=[pltpu.SemaphoreType.DMA((3,)), pltpu.SemaphoreType.DMA((3,)), pltpu.SemaphoreType.DMA],
    )(s)


def _sum_pieces(r):
    half = r.shape[1]
    br = half // 8

    def kern(r_ref, o_ref):
        o_ref[...] = ((r_ref[0] + r_ref[1]) + r_ref[2]) + r_ref[3]

    return pl.pallas_call(
        kern, name="sum_pieces", grid=(8,), in_specs=[pl.BlockSpec((N_CHIPS, br, PACK_COLS), lambda t: (0, t, 0))],
        out_specs=pl.BlockSpec((br, PACK_COLS), lambda t: (t, 0)), out_shape=_sds(half, PACK_COLS, F32),
        compiler_params=_params(("arbitrary",)))(r)


def _join_halves(f):
    half = f.shape[0]

    def body(f_ref, o_ref, send_sem, recv_sem, local_sem):
        x, y, c = _place()
        here = o_ref.at[pl.ds(pl.multiple_of(c * half, 8), half), :]
        mine = pltpu.make_async_copy(f_ref, here, local_sem)
        mine.start()
        cp = pltpu.make_async_remote_copy(f_ref, here, send_sem, recv_sem, device_id=(x, y, 1 - c), device_id_type=MESH)
        cp.start()
        cp.wait()
        mine.wait()

    return pl.pallas_call(
        body, name="join_halves", out_shape=_sds(2 * half, PACK_COLS, f.dtype), in_specs=[_HBM], out_specs=_HBM,
        scratch_shapes=[pltpu.SemaphoreType.DMA, pltpu.SemaphoreType.DMA, pltpu.SemaphoreType.DMA],
    )(f)


def _allreduce_small(g):
    n_dev = 8

    def body(g_ref, o_ref, buf, send_sems, recv_sems):
        x, y, c = _place()
        me = 4 * x + 2 * y + c
        buf[me] = g_ref[...]
        peers = []
        for f in range(1, n_dev):
            px, py = _flip(x, y, f & 4, f & 2)
            pc = 1 - c if f & 1 else c
            peers.append((px, py, pc))
        sent = []
        for f, peer in enumerate(peers):
            cp = pltpu.make_async_remote_copy(g_ref, buf.at[me], send_sems.at[f], recv_sems.at[f], device_id=peer,
                                              device_id_type=MESH)
            cp.start()
            sent.append(cp)
        for f, (px, py, pc) in enumerate(peers):
            slot = buf.at[4 * px + 2 * py + pc]
            pltpu.make_async_remote_copy(slot, slot, send_sems.at[f], recv_sems.at[f], device_id=(px, py, pc),
                                         device_id_type=MESH).wait_recv()
        for cp in sent:
            cp.wait_send()
        total = buf[0]
        for k in range(1, n_dev):
            total = total + buf[k]
        o_ref[...] = total

    vmem = pl.BlockSpec(memory_space=pltpu.VMEM)
    return pl.pallas_call(
        body, name="allreduce_small", out_shape=jax.ShapeDtypeStruct(g.shape, g.dtype), in_specs=[vmem], out_specs=vmem,
        scratch_shapes=[pltpu.VMEM((n_dev,) + g.shape, g.dtype), pltpu.SemaphoreType.DMA((n_dev - 1,)),
                        pltpu.SemaphoreType.DMA((n_dev - 1,))],
    )(g)


def _adamw(g, w, m, v, br):
    rows = g.shape[0]
    c1 = 1.0 - ADAM_B1 ** ADAM_STEP
    c2 = 1.0 - ADAM_B2 ** ADAM_STEP

    def kern(g_ref, w_ref, m_ref, v_ref, d_o, m_o, v_o):
        g_ = g_ref[...]
        m_new = ADAM_B1 * m_ref[...] + (1.0 - ADAM_B1) * g_
        v_new = ADAM_B2 * v_ref[...] + (1.0 - ADAM_B2) * (g_ * g_)
        m_o[...] = m_new
        v_o[...] = v_new
        d_o[...] = -ADAM_LR * ((m_new / c1) / (jnp.sqrt(v_new / c2) + ADAM_EPS) + ADAM_WD * w_ref[...])

    spec = pl.BlockSpec((br, PACK_COLS), lambda t: (t, 0))
    out = _sds(rows, PACK_COLS, F32)
    return pl.pallas_call(kern, name="adamw", grid=(rows // br,), in_specs=[spec] * 4, out_specs=[spec] * 3,
                          out_shape=[out, out, out], compiler_params=_params(("arbitrary",)))(g, w, m, v)


def _shard_rows(shape, axis):
    k, n = shape
    return (k * n // N_CHIPS) // PACK_COLS


def _pack_shards(shards, dtype):
    parts = [shards[name].astype(dtype).reshape(-1, PACK_COLS) for name, _, _ in SHARDED]
    used = sum(p.shape[0] for p in parts)
    parts.append(jnp.zeros((BIG_ROWS - used, PACK_COLS), dtype))
    return jnp.concatenate(parts, 0)


def _unpack_shards(packed):
    out, r0 = {}, 0
    for name, (k, n), axis in SHARDED:
        nr = _shard_rows((k, n), axis)
        shape = (k // N_CHIPS, n) if axis == 0 else (k, n // N_CHIPS)
        out[name] = packed[r0:r0 + nr].reshape(shape)
        r0 += nr
    return out


def _unpack_full(gathered):
    out, r0 = {}, 0
    for name, (k, n), axis in SHARDED:
        nr = _shard_rows((k, n), axis)
        part = gathered[:, r0:r0 + nr]
        if axis == 0:
            out[name] = part.reshape(k, n)
        else:
            out[name] = part.reshape(N_CHIPS, k, n // N_CHIPS).transpose(1, 0, 2).reshape(k, n)
        r0 += nr
    return out


def _pack_pieces(grads):
    parts = []
    for name, (k, n), axis in SHARDED:
        g = grads[name]
        if axis == 0:
            parts.append(g.reshape(N_CHIPS, -1, PACK_COLS))
        else:
            parts.append(g.reshape(k, N_CHIPS, n // N_CHIPS).transpose(1, 0, 2).reshape(N_CHIPS, -1, PACK_COLS))
    used = sum(p.shape[1] for p in parts)
    parts.append(jnp.zeros((N_CHIPS, BIG_ROWS - used, PACK_COLS), F32))
    return jnp.concatenate(parts, 1)


def _pack_small(vals):
    flat = jnp.concatenate([vals[name].astype(F32).reshape(-1) for name, _ in SMALL])
    flat = jnp.concatenate([flat, jnp.zeros((SMALL_ROWS * PACK_COLS - flat.shape[0],), F32)])
    return flat.reshape(SMALL_ROWS, PACK_COLS)


def _unpack_small(packed):
    flat, out, o = packed.reshape(-1), {}, 0
    for name, shape in SMALL:
        n = int(np.prod(shape))
        out[name] = flat[o:o + n].reshape(shape)
        o += n
    return out


def kernel(x, p, positions, g_pre_mix, w_in, b_gate, g_q, w_uq, g_kv, w_ukv, w_pool, pool_scale, w_branch_attn, w_branch_pool, w_out, g_post_mix, g_pre_mlp, w_ff1, w_ff2, g_post_mlp, w_ple_proj, w_ple_gate, g_ple, loss_target, m_g_pre_mix, m_w_in, m_b_gate, m_g_q, m_w_uq, m_g_kv, m_w_ukv, m_w_pool, m_pool_scale, m_w_branch_attn, m_w_branch_pool, m_w_out, m_g_post_mix, m_g_pre_mlp, m_w_ff1, m_w_ff2, m_g_post_mlp, m_w_ple_proj, m_w_ple_gate, m_g_ple, v_g_pre_mix, v_w_in, v_b_gate, v_g_q, v_w_uq, v_g_kv, v_w_ukv, v_w_pool, v_pool_scale, v_w_branch_attn, v_w_branch_pool, v_w_out, v_g_post_mix, v_g_pre_mlp, v_w_ff1, v_w_ff2, v_g_post_mlp, v_w_ple_proj, v_w_ple_gate, v_g_ple):
    given = dict(locals())
    weights = {n: given[n] for n in WEIGHT_ORDER}
    moments_m = {n: given["m_" + n] for n in WEIGHT_ORDER}
    moments_v = {n: given["v_" + n] for n in WEIGHT_ORDER}
    c = lax.axis_index("c")

    big_w = {name: weights[name][0] for name, _, _ in SHARDED}
    gathered = _allgather_shards(_pack_shards(big_w, MX))
    full = _unpack_full(gathered)
    for name, _ in SMALL:
        full[name] = weights[name][0] if name == "w_pool" else weights[name]

    loss_cols, dx, grads = _local_step(x[0], p[0, 0], positions[0], loss_target[0], full)
    loss = lax.psum(0.5 * jnp.sum(loss_cols) / D_MODEL, ("x", "y", "c"))

    pieces = _pack_pieces(grads)
    summed = _add_halves(pieces, _exchange_halves(pieces), c)
    g_big = _join_halves(_sum_pieces(_scatter_pieces(summed)))
    g_small = _allreduce_small(_pack_small(grads))

    d_big, m_big, v_big = _adamw(g_big, _pack_shards(big_w, F32), _pack_shards({n: moments_m[n][0] for n, _, _ in SHARDED}, F32),
                                 _pack_shards({n: moments_v[n][0] for n, _, _ in SHARDED}, F32), BIG_ROWS // 10)
    d_small, m_small, v_small = _adamw(g_small, _pack_small(weights), _pack_small(moments_m), _pack_small(moments_v), SMALL_ROWS)

    def unpack(big, small):
        out = {n: a[None] for n, a in _unpack_shards(big).items()}
        out.update(_unpack_small(small))
        return [out[n] for n in WEIGHT_ORDER]

    return (loss, dx[None], *unpack(g_big, g_small), *unpack(d_big, d_small), *unpack(m_big, m_small), *unpack(v_big, v_small))
```

```python
import functools

import numpy as np
import jax
import jax.numpy as jnp
from jax import lax
from jax.experimental import pallas as pl
from jax.experimental.pallas import tpu as pltpu

F32 = jnp.float32
MX = jnp.bfloat16

D_MODEL = 1024
N_HEADS = 8
QK_NOPE = 64
QK_ROPE = 32
V_HEAD = 64
Q_LORA = 384
KV_LORA = 256
POOL_WINDOWS = (2, 4, 8, 16)
POOL_GROUP = 128
POOL_WIDTH = 512
D_FF = 4096
PLE_DIM = 256
ROPE_THETA = 10000.0
EPS = 1e-6
HEAD_SLOT = 128
QK_WIDTH = N_HEADS * HEAD_SLOT
ROPE_LANE = 64
SMALL_COLS = Q_LORA + KV_LORA + HEAD_SLOT
IN_PAD = SMALL_COLS + POOL_WIDTH + 2 * D_MODEL
SCALE = (QK_NOPE + QK_ROPE) ** -0.5
LOG2E = 1.4426950408889634
NEG = -1e30
HALO = 16

ADAM_LR = 0.001
ADAM_B1 = 0.9
ADAM_B2 = 0.999
ADAM_EPS = 1e-08
ADAM_WD = 0.01
ADAM_STEP = 10

VMEM_LIMIT = 56 * 2**20
MESH = pl.DeviceIdType.MESH

SHARDED = (
    ("w_in", (1024, 3232), 1),
    ("w_uq", (384, 768), 1),
    ("w_ukv", (256, 1024), 1),
    ("w_branch_attn", (512, 1024), 1),
    ("w_branch_pool", (512, 1024), 1),
    ("w_out", (1024, 1024), 0),
    ("w_ff1", (1024, 4096), 1),
    ("w_ff2", (4096, 1024), 0),
    ("w_ple_proj", (256, 1024), 1),
    ("w_ple_gate", (1024, 1024), 0),
)
SMALL = (
    ("g_pre_mix", (1, 1024)),
    ("b_gate", (1, 2048)),
    ("g_q", (1, 384)),
    ("g_kv", (1, 256)),
    ("w_pool", (1, 4, 128, 128)),
    ("pool_scale", (1, 512)),
    ("g_post_mix", (1, 1024)),
    ("g_pre_mlp", (1, 1024)),
    ("g_post_mlp", (1, 1024)),
    ("g_ple", (1, 1024)),
)
WEIGHT_ORDER = ("g_pre_mix", "w_in", "b_gate", "g_q", "w_uq", "g_kv", "w_ukv", "w_pool", "pool_scale", "w_branch_attn",
                "w_branch_pool", "w_out", "g_post_mix", "g_pre_mlp", "w_ff1", "w_ff2", "g_post_mlp", "w_ple_proj",
                "w_ple_gate", "g_ple")
N_CHIPS = 4
PACK_COLS = 1024
BIG_ROWS = 3840
SMALL_ROWS = 80


def _dot(a, b):
    return jnp.dot(a.astype(MX), b.astype(MX), preferred_element_type=F32)


def _dot_nt(a, b):
    return lax.dot_general(a.astype(MX), b.astype(MX), (((1,), (1,)), ((), ())), preferred_element_type=F32)


def _dot_tn(a, b):
    return lax.dot_general(a.astype(MX), b.astype(MX), (((0,), (0,)), ((), ())), preferred_element_type=F32)


def _sig(x):
    return 1.0 / (1.0 + jnp.exp(-x))


def _rms(x, g):
    r = lax.rsqrt(jnp.mean(x * x, axis=1, keepdims=True) + EPS)
    xh = x * r
    return xh * g, xh, r


def _rms_bwd(xh, r, g, dy):
    dxn = dy * g
    dx = r * (dxn - xh * jnp.mean(dxn * xh, axis=1, keepdims=True))
    return dx, jnp.sum(dy * xh, axis=0, keepdims=True)


def _rot_half(v):
    lane = lax.broadcasted_iota(jnp.int32, v.shape, 1)
    return jnp.where(lane < ROPE_LANE + QK_ROPE // 2, pltpu.roll(v, HEAD_SLOT - QK_ROPE // 2, 1), pltpu.roll(v, QK_ROPE // 2, 1))


def _rope(v, cc, sa):
    return v * cc + _rot_half(v) * sa


def _unrope(v, cc, sa):
    return v * cc - _rot_half(v) * sa


def _params(sem):
    return pltpu.CompilerParams(dimension_semantics=sem, vmem_limit_bytes=VMEM_LIMIT)


def _tok_call(name, body, n_tok, tm, tiled, resident, outs, accs=(), scratch=()):
    def as_pair(t):
        if isinstance(t, tuple):
            return t
        return t, pl.BlockSpec((tm, t.shape[1]), lambda i: (i, 0))
    tiled = [as_pair(t) for t in tiled]
    res_specs = [pl.BlockSpec(r.shape, lambda i, nd=r.ndim: (0,) * nd, pipeline_mode=pl.Buffered(1)) for r in resident]
    out_specs = [pl.BlockSpec((tm, o.shape[1]), lambda i: (i, 0)) for o in outs]
    out_specs += [pl.BlockSpec(a.shape, lambda i: (0, 0)) for a in accs]
    n_t, n_r, n_o, n_a = len(tiled), len(resident), len(outs), len(accs)

    def kern(*refs):
        tin, res = refs[:n_t], refs[n_t:n_t + n_r]
        tout = refs[n_t + n_r:n_t + n_r + n_o]
        acc = refs[n_t + n_r + n_o:n_t + n_r + n_o + n_a]
        scr = refs[n_t + n_r + n_o + n_a:]
        i = pl.program_id(0)

        @pl.when(i == 0)
        def _():
            for a in acc:
                a[...] = jnp.zeros(a.shape, a.dtype)
        body(i, tin, res, tout, acc, scr)

    return pl.pallas_call(
        kern, name=name, grid=(n_tok // tm,), in_specs=[s for _, s in tiled] + res_specs, out_specs=out_specs,
        out_shape=list(outs) + list(accs), scratch_shapes=list(scratch), compiler_params=_params(("arbitrary",)),
    )(*[a for a, _ in tiled], *resident)


def _sds(rows, cols, dtype):
    return jax.ShapeDtypeStruct((rows, cols), dtype)


def _fwd_inproj(x, cc, sa, w, tm):
    n_tok = x.shape[0]

    def body(i, tin, res, tout, acc, scr):
        x_ref, c_ref, s_ref = tin
        g_pre, w_in, g_q, w_uq, g_kv, w_k, w_v, e_mat = res
        a_o, ps_o, u_o, gl_o, qn_o, kvn_o, q_o, k_o, v_o = tout
        a = _rms(x_ref[...], g_pre[...])[0].astype(MX)
        a_o[...] = a
        ps = _dot(a, w_in[:, :SMALL_COLS])
        ps_o[...] = ps.astype(ps_o.dtype)
        u_o[...] = _dot(a, w_in[:, SMALL_COLS:SMALL_COLS + POOL_WIDTH]).astype(u_o.dtype)
        gl_o[...] = _dot(a, w_in[:, SMALL_COLS + POOL_WIDTH:]).astype(gl_o.dtype)
        cc_, sa_ = c_ref[...], s_ref[...]
        qn = _rms(ps[:, :Q_LORA], g_q[...])[0].astype(MX)
        qn_o[...] = qn
        q = _dot(qn, w_uq[...])
        for h in range(N_HEADS):
            hs = slice(h * HEAD_SLOT, (h + 1) * HEAD_SLOT)
            q_o[:, hs] = (_rope(q[:, hs], cc_, sa_) * (SCALE * LOG2E)).astype(q_o.dtype)
        kvn = _rms(ps[:, Q_LORA:Q_LORA + KV_LORA], g_kv[...])[0].astype(MX)
        kvn_o[...] = kvn
        kr = _rope(ps[:, Q_LORA + KV_LORA:], cc_, sa_)
        k_o[...] = (_dot(kvn, w_k[...]) + _dot(kr, e_mat[...])).astype(k_o.dtype)
        v_o[...] = _dot(kvn, w_v[...]).astype(v_o.dtype)

    outs = [_sds(n_tok, D_MODEL, MX), _sds(n_tok, SMALL_COLS, MX), _sds(n_tok, POOL_WIDTH, MX), _sds(n_tok, 2 * D_MODEL, MX),
            _sds(n_tok, Q_LORA, MX), _sds(n_tok, KV_LORA, MX), _sds(n_tok, QK_WIDTH, MX), _sds(n_tok, QK_WIDTH, MX),
            _sds(n_tok, N_HEADS * V_HEAD, MX)]
    res = [w["g_pre_mix"], w["w_in"], w["g_q"], w["w_uq"], w["g_kv"], w["w_k"], w["w_v"], w["e_mat"]]
    return _tok_call("fwd_inproj", body, n_tok, tm, [x, cc, sa], res, outs)


def _causal_pairs(n, by_kv):
    if by_kv:
        pairs = [(i, j) for j in range(n) for i in range(j, n)]
    else:
        pairs = [(i, j) for i in range(n) for j in range(i + 1)]
    return (jnp.asarray(np.array([p[0] for p in pairs], np.int32)), jnp.asarray(np.array([p[1] for p in pairs], np.int32)))


def _keep_t(tq):
    return lax.broadcasted_iota(jnp.int32, (tq, tq), 0) <= lax.broadcasted_iota(jnp.int32, (tq, tq), 1)


def _attn_fwd(q, k, vt, tq):
    n_tok = q.shape[0]
    n = n_tok // tq
    qi, kj = _causal_pairs(n, by_kv=False)

    def kern(qi_ref, kj_ref, q_ref, k_ref, vt_ref, ot_ref, lse_ref, m_s, l_s, acc_s, st_s):
        s_id = pl.program_id(0)
        i, j = qi_ref[s_id], kj_ref[s_id]

        @pl.when(j == 0)
        def _():
            m_s[...] = jnp.full(m_s.shape, NEG, F32)
            l_s[...] = jnp.zeros(l_s.shape, F32)
            acc_s[...] = jnp.zeros(acc_s.shape, F32)

        def scores(h):
            hs = slice(h * HEAD_SLOT, (h + 1) * HEAD_SLOT)
            return _dot_nt(k_ref[:, hs], q_ref[:, hs])

        def heads(masked):
            st_s[0] = scores(0)
            for h in range(N_HEADS):
                if h + 1 < N_HEADS:
                    st_s[(h + 1) % 2] = scores(h + 1)
                st = st_s[h % 2]
                if masked:
                    st = jnp.where(_keep_t(tq), st, NEG)
                m_old = m_s[h]
                m_new = jnp.maximum(m_old, jnp.max(st, axis=0, keepdims=True))
                alpha = jnp.exp2(m_old - m_new)
                pt = jnp.exp2(st - m_new)
                l_s[h] = alpha * l_s[h] + jnp.sum(pt, axis=0, keepdims=True)
                acc_s[h] = alpha * acc_s[h] + _dot(vt_ref[h * V_HEAD:(h + 1) * V_HEAD, :], pt)
                m_s[h] = m_new

        @pl.when(j < i)
        def _():
            heads(False)

        @pl.when(j == i)
        def _():
            heads(True)
            for h in range(N_HEADS):
                ot_ref[h * V_HEAD:(h + 1) * V_HEAD, :] = (acc_s[h] / l_s[h]).astype(ot_ref.dtype)
                lse_ref[h:h + 1, :] = m_s[h] + jnp.log2(l_s[h])

    gs = pltpu.PrefetchScalarGridSpec(
        num_scalar_prefetch=2, grid=(qi.shape[0],),
        in_specs=[pl.BlockSpec((tq, QK_WIDTH), lambda s, qi, kj: (qi[s], 0)),
                  pl.BlockSpec((tq, QK_WIDTH), lambda s, qi, kj: (kj[s], 0)),
                  pl.BlockSpec((N_HEADS * V_HEAD, tq), lambda s, qi, kj: (0, kj[s]))],
        out_specs=[pl.BlockSpec((N_HEADS * V_HEAD, tq), lambda s, qi, kj: (0, qi[s])),
                   pl.BlockSpec((N_HEADS, tq), lambda s, qi, kj: (0, qi[s]))],
        scratch_shapes=[pltpu.VMEM((N_HEADS, 1, tq), F32), pltpu.VMEM((N_HEADS, 1, tq), F32),
                        pltpu.VMEM((N_HEADS, V_HEAD, tq), F32), pltpu.VMEM((2, tq, tq), F32)])
    return pl.pallas_call(kern, name="attn_fwd", grid_spec=gs,
                          out_shape=[_sds(N_HEADS * V_HEAD, n_tok, MX), _sds(N_HEADS, n_tok, F32)],
                          compiler_params=_params(("arbitrary",)))(qi, kj, q, k, vt)


def _pool_windows(ext, i, tm, first_row):
    row = i * tm + lax.broadcasted_iota(jnp.int32, (tm, 1), 0)
    out = []
    for g, w in enumerate(POOL_WINDOWS):
        cs = slice(g * POOL_GROUP, (g + 1) * POOL_GROUP)
        s = ext[pl.ds(first_row, tm), cs]
        for k in range(1, w):
            s = s + ext[pl.ds(first_row - k, tm), cs]
        cnt = jnp.minimum(row + 1, w).astype(F32)
        out.append(s / cnt)
    return out


def _fwd_mix(x, u, gl, attn, w, tm):
    n_tok = x.shape[0]
    halo_spec = pl.BlockSpec((HALO, POOL_WIDTH), lambda i: (jnp.maximum(i * (tm // HALO) - 1, 0), 0))

    def body(i, tin, res, tout, acc, scr):
        x_ref, u_ref, uh_ref, gl_ref, at_ref = tin
        w_pool, pool_scale, w_ba, w_bp, b_gate, w_out, g_post = res
        d_o, pooled_o, a_o, pp_o, merged_o, y_o, h1_o = tout
        ext, = scr
        ext[pl.ds(0, HALO), :] = jnp.where(i > 0, uh_ref[...].astype(F32), 0.0)
        ext[pl.ds(HALO, tm), :] = u_ref[...].astype(F32)
        means = _pool_windows(ext, i, tm, HALO)
        for g in range(len(POOL_WINDOWS)):
            cs = slice(g * POOL_GROUP, (g + 1) * POOL_GROUP)
            d = (means[g] - ext[pl.ds(HALO, tm), cs]).astype(MX)
            d_o[:, cs] = d
            pooled_o[:, cs] = (_dot(d, w_pool[g]) * pool_scale[:, cs]).astype(pooled_o.dtype)
        a_br = _dot(at_ref[...], w_ba[...])
        p_br = _dot(pooled_o[...], w_bp[...])
        a_o[...] = a_br.astype(a_o.dtype)
        pp_o[...] = p_br.astype(pp_o.dtype)
        gates = _sig(gl_ref[...].astype(F32) + b_gate[...])
        merged = (gates[:, :D_MODEL] * a_br + gates[:, D_MODEL:] * p_br).astype(MX)
        merged_o[...] = merged
        y = _dot(merged, w_out[...])
        y_o[...] = y.astype(y_o.dtype)
        h1_o[...] = x_ref[...] + _rms(y, g_post[...])[0]

    outs = [_sds(n_tok, POOL_WIDTH, MX), _sds(n_tok, POOL_WIDTH, MX), _sds(n_tok, D_MODEL, MX), _sds(n_tok, D_MODEL, MX),
            _sds(n_tok, D_MODEL, MX), _sds(n_tok, D_MODEL, MX), _sds(n_tok, D_MODEL, F32)]
    res = [w["w_pool"], w["pool_scale"], w["w_branch_attn"], w["w_branch_pool"], w["b_gate"], w["w_out"], w["g_post_mix"]]
    return _tok_call("fwd_mix", body, n_tok, tm, [x, u, (u, halo_spec), gl, attn], res, outs,
                     scratch=[pltpu.VMEM((tm + HALO, POOL_WIDTH), F32)])


def _fwd_mlp(h1, w, tm):
    n_tok = h1.shape[0]

    def body(i, tin, res, tout, acc, scr):
        h1_ref, = tin
        g_pre, w1, w2, g_post = res
        m_o, zr_o, a2_o, f_o, h2_o = tout
        h1_ = h1_ref[...]
        m = _rms(h1_, g_pre[...])[0].astype(MX)
        m_o[...] = m
        zr = jnp.maximum(_dot(m, w1[...]), 0.0)
        zr_o[...] = zr.astype(zr_o.dtype)
        a2 = (zr * zr).astype(MX)
        a2_o[...] = a2
        f = _dot(a2, w2[...])
        f_o[...] = f.astype(f_o.dtype)
        h2_o[...] = h1_ + _rms(f, g_post[...])[0]

    outs = [_sds(n_tok, D_MODEL, MX), _sds(n_tok, D_FF, MX), _sds(n_tok, D_FF, MX), _sds(n_tok, D_MODEL, MX),
            _sds(n_tok, D_MODEL, F32)]
    res = [w["g_pre_mlp"], w["w_ff1"], w["w_ff2"], w["g_post_mlp"]]
    return _tok_call("fwd_mlp", body, n_tok, tm, [h1], res, outs)


def _ple_fwd_bwd(h2, p, target, w, tm):
    n_tok = h2.shape[0]

    def body(i, tin, res, tout, acc, scr):
        h2_ref, p_ref, t_ref = tin
        w_pe, w_pg, g_ple = res
        dh2_o, de_o, dzg_o = tout
        loss_a, dg_a = acc
        h2_ = h2_ref[...]
        e = _dot(p_ref[...], w_pe[...])
        pg = _sig(_dot(h2_, w_pg[...]))
        t = pg * e
        g = g_ple[...]
        tn, th, r = _rms(t, g)
        diff = h2_ + tn - t_ref[...]
        loss_a[...] += jnp.sum(diff * diff, axis=0, keepdims=True)
        dh3 = diff * (1.0 / D_MODEL)
        dt, dg = _rms_bwd(th, r, g, dh3)
        dg_a[...] += dg
        de_o[...] = (dt * pg).astype(de_o.dtype)
        dzg = (dt * e * pg * (1.0 - pg)).astype(MX)
        dzg_o[...] = dzg
        dh2_o[...] = dh3 + _dot_nt(dzg, w_pg[...])

    outs = [_sds(n_tok, D_MODEL, F32), _sds(n_tok, D_MODEL, MX), _sds(n_tok, D_MODEL, MX)]
    accs = [_sds(1, D_MODEL, F32), _sds(1, D_MODEL, F32)]
    return _tok_call("ple_fwd_bwd", body, n_tok, tm, [h2, p, target], [w["w_ple_proj"], w["w_ple_gate"], w["g_ple"]], outs, accs)


def _bwd_mlp(dh2, f, h1, zr, w, tm):
    n_tok = dh2.shape[0]

    def body(i, tin, res, tout, acc, scr):
        dh2_ref, f_ref, h1_ref, zr_ref = tin
        g_pre, w1, w2, g_post = res
        df_o, dz_o, dh1_o = tout
        dg_post_a, dg_pre_a = acc
        dh2_ = dh2_ref[...]
        gp = g_post[...]
        _, fh, rf = _rms(f_ref[...].astype(F32), gp)
        df, dg = _rms_bwd(fh, rf, gp, dh2_)
        dg_post_a[...] += dg
        df = df.astype(MX)
        df_o[...] = df
        dz = (_dot_nt(df, w2[...]) * (2.0 * zr_ref[...].astype(F32))).astype(MX)
        dz_o[...] = dz
        dm = _dot_nt(dz, w1[...])
        gq = g_pre[...]
        _, hh, rh = _rms(h1_ref[...], gq)
        dh1, dg = _rms_bwd(hh, rh, gq, dm)
        dg_pre_a[...] += dg
        dh1_o[...] = dh2_ + dh1

    outs = [_sds(n_tok, D_MODEL, MX), _sds(n_tok, D_FF, MX), _sds(n_tok, D_MODEL, F32)]
    accs = [_sds(1, D_MODEL, F32), _sds(1, D_MODEL, F32)]
    res = [w["g_pre_mlp"], w["w_ff1"], w["w_ff2"], w["g_post_mlp"]]
    return _tok_call("bwd_mlp", body, n_tok, tm, [dh2, f, h1, zr], res, outs, accs)


def _bwd_mix(dh1, y, a_br, p_br, gl, attn, d, w, tm):
    n_tok = dh1.shape[0]

    def body(i, tin, res, tout, acc, scr):
        dh1_ref, y_ref, a_ref, pp_ref, gl_ref, at_ref, d_ref = tin
        g_post, w_out, b_gate, w_ba, w_bp, w_pool, pool_scale, sel = res
        dy_o, da_o, dpp_o, dgl_o, do_o, delta_o, dyp_o, dd_o = tout
        dg_post_a, db_a, dps_a = acc
        g = g_post[...]
        _, yh, r = _rms(y_ref[...].astype(F32), g)
        dy, dg = _rms_bwd(yh, r, g, dh1_ref[...])
        dg_post_a[...] += dg
        dy = dy.astype(MX)
        dy_o[...] = dy
        dmerged = _dot_nt(dy, w_out[...])
        gates = _sig(gl_ref[...].astype(F32) + b_gate[...])
        ga, gp = gates[:, :D_MODEL], gates[:, D_MODEL:]
        da = (dmerged * ga).astype(MX)
        dpp = (dmerged * gp).astype(MX)
        da_o[...] = da
        dpp_o[...] = dpp
        dgl_a = dmerged * a_ref[...].astype(F32) * ga * (1.0 - ga)
        dgl_p = dmerged * pp_ref[...].astype(F32) * gp * (1.0 - gp)
        dgl_o[:, :D_MODEL] = dgl_a.astype(dgl_o.dtype)
        dgl_o[:, D_MODEL:] = dgl_p.astype(dgl_o.dtype)
        db_a[:, :D_MODEL] += jnp.sum(dgl_a, axis=0, keepdims=True)
        db_a[:, D_MODEL:] += jnp.sum(dgl_p, axis=0, keepdims=True)
        do = _dot_nt(da, w_ba[...]).astype(MX)
        do_o[...] = do
        prod = do.astype(F32) * at_ref[...].astype(F32)
        hi = prod.astype(MX)
        lo = (prod - hi.astype(F32)).astype(MX)
        delta_o[...] = _dot(hi, sel[...]) + _dot(lo, sel[...])
        dpooled = _dot_nt(dpp, w_bp[...])
        for gi in range(len(POOL_WINDOWS)):
            cs = slice(gi * POOL_GROUP, (gi + 1) * POOL_GROUP)
            ypre = _dot(d_ref[:, cs], w_pool[gi])
            dps_a[:, cs] += jnp.sum(dpooled[:, cs] * ypre, axis=0, keepdims=True)
            dyp = (dpooled[:, cs] * pool_scale[:, cs]).astype(MX)
            dyp_o[:, cs] = dyp
            dd_o[:, cs] = _dot_nt(dyp, w_pool[gi])

    outs = [_sds(n_tok, D_MODEL, MX), _sds(n_tok, D_MODEL, MX), _sds(n_tok, D_MODEL, MX), _sds(n_tok, 2 * D_MODEL, MX),
            _sds(n_tok, N_HEADS * V_HEAD, MX), _sds(n_tok, HEAD_SLOT, F32), _sds(n_tok, POOL_WIDTH, MX),
            _sds(n_tok, POOL_WIDTH, F32)]
    accs = [_sds(1, D_MODEL, F32), _sds(1, 2 * D_MODEL, F32), _sds(1, POOL_WIDTH, F32)]
    res = [w["g_post_mix"], w["w_out"], w["b_gate"], w["w_branch_attn"], w["w_branch_pool"], w["w_pool"], w["pool_scale"],
           w["head_sel"]]
    return _tok_call("bwd_mix", body, n_tok, tm, [dh1, y, a_br, p_br, gl, attn, d], res, outs, accs)


def _bwd_heads(q_ref, k_ref, v_ref, do_ref, lse_ref, dl_ref, st_s, dpt_s, masked, tq, use, n_heads):
    def products(h):
        hs = slice(h * HEAD_SLOT, (h + 1) * HEAD_SLOT)
        vs = slice(h * V_HEAD, (h + 1) * V_HEAD)
        st_s[h % 2] = _dot_nt(k_ref[:, hs], q_ref[:, hs])
        dpt_s[h % 2] = _dot_nt(v_ref[:, vs], do_ref[:, vs])

    products(0)
    for h in range(n_heads):
        if h + 1 < n_heads:
            products(h + 1)
        st = st_s[h % 2]
        if masked:
            st = jnp.where(_keep_t(tq), st, NEG)
        pt = jnp.exp2(st - lse_ref[h:h + 1, :])
        use(h, pt, pt * (dpt_s[h % 2] - dl_ref[h:h + 1, :]))


HEAD_GROUP = 4


def _attn_bwd(q, k, kt, v, do, lse, delta, tq):
    n_tok = q.shape[0]
    n = n_tok // tq
    n_groups = N_HEADS // HEAD_GROUP
    gq, gv = HEAD_GROUP * HEAD_SLOT, HEAD_GROUP * V_HEAD
    qi, kj = _causal_pairs(n, by_kv=True)

    def kern(qi_ref, kj_ref, q_ref, k_ref, kt_ref, v_ref, do_ref, lse_ref, dl_ref, dq_ref, dk_ref, dv_ref,
             dk_s, dv_s, st_s, dpt_s):
        s_id = pl.program_id(1)
        i, j = qi_ref[s_id], kj_ref[s_id]
        cols = pl.ds(pl.multiple_of(i * tq, tq), tq)

        @pl.when(s_id == 0)
        def _():
            dq_ref[...] = jnp.zeros(dq_ref.shape, F32)

        def use(h, pt, dst):
            hs = slice(h * HEAD_SLOT, (h + 1) * HEAD_SLOT)
            dv_s[h] += _dot(pt, do_ref[:, h * V_HEAD:(h + 1) * V_HEAD])
            dk_s[:, hs] += _dot(dst, q_ref[:, hs])
            dq_ref[hs, cols] += _dot(kt_ref[hs, :], dst)

        def heads(masked):
            _bwd_heads(q_ref, k_ref, v_ref, do_ref, lse_ref.at[0], dl_ref.at[0], st_s, dpt_s, masked, tq, use, HEAD_GROUP)

        @pl.when(i == j)
        def _():
            dk_s[...] = jnp.zeros(dk_s.shape, F32)
            dv_s[...] = jnp.zeros(dv_s.shape, F32)
            heads(True)

        @pl.when(i > j)
        def _():
            heads(False)

        @pl.when(i == n - 1)
        def _():
            dk_ref[...] = (dk_s[...] * (1.0 / LOG2E)).astype(dk_ref.dtype)
            for h in range(HEAD_GROUP):
                dv_ref[:, h * V_HEAD:(h + 1) * V_HEAD] = dv_s[h].astype(dv_ref.dtype)

    at_q = lambda g, s, qi, kj: (qi[s], g)
    at_k = lambda g, s, qi, kj: (kj[s], g)
    at_kt = lambda g, s, qi, kj: (g, kj[s])
    at_stat = lambda g, s, qi, kj: (g, 0, qi[s])
    gs = pltpu.PrefetchScalarGridSpec(
        num_scalar_prefetch=2, grid=(n_groups, qi.shape[0]),
        in_specs=[pl.BlockSpec((tq, gq), at_q), pl.BlockSpec((tq, gq), at_k), pl.BlockSpec((gq, tq), at_kt),
                  pl.BlockSpec((tq, gv), at_k), pl.BlockSpec((tq, gv), at_q),
                  pl.BlockSpec((1, HEAD_GROUP, tq), at_stat), pl.BlockSpec((1, HEAD_GROUP, tq), at_stat)],
        out_specs=[pl.BlockSpec((gq, n_tok), lambda g, s, qi, kj: (g, 0), pipeline_mode=pl.Buffered(1)),
                   pl.BlockSpec((tq, gq), at_k), pl.BlockSpec((tq, gv), at_k)],
        scratch_shapes=[pltpu.VMEM((tq, gq), F32), pltpu.VMEM((HEAD_GROUP, tq, V_HEAD), F32),
                        pltpu.VMEM((2, tq, tq), F32), pltpu.VMEM((2, tq, tq), F32)])
    stat3 = lambda a: a.reshape(n_groups, HEAD_GROUP, n_tok)
    return pl.pallas_call(kern, name="attn_bwd", grid_spec=gs,
                          out_shape=[_sds(QK_WIDTH, n_tok, F32), _sds(n_tok, QK_WIDTH, MX), _sds(n_tok, N_HEADS * V_HEAD, MX)],
                          compiler_params=_params(("arbitrary", "arbitrary")))(qi, kj, q, k, kt, v, do, stat3(lse), stat3(delta))


def _bwd_inproj(dq, dk, dv, dd, dgl, ps, x, dh1, cc, sa, w, tm):
    n_tok = x.shape[0]
    n_tiles = n_tok // tm
    last_halo = n_tok // HALO - 1
    halo_spec = pl.BlockSpec((HALO, POOL_WIDTH), lambda i: (jnp.minimum((i + 1) * (tm // HALO), last_halo), 0))

    def body(i, tin, res, tout, acc, scr):
        dq_ref, dk_ref, dv_ref, dd_ref, ddh_ref, dgl_ref, ps_ref, x_ref, dh1_ref, c_ref, s_ref = tin
        w_uq, g_q, w_k, w_v, e_mat, g_kv, w_in, g_pre = res
        dqu_o, dproj_o, dx_o = tout
        dgq_a, dgkv_a, dgpre_a = acc
        ext, = scr
        cc_, sa_ = c_ref[...], s_ref[...]
        for h in range(N_HEADS):
            hs = slice(h * HEAD_SLOT, (h + 1) * HEAD_SLOT)
            dqu_o[:, hs] = (_unrope(dq_ref[:, hs].astype(F32), cc_, sa_) * SCALE).astype(dqu_o.dtype)
        gq = g_q[...]
        _, qh, rq = _rms(ps_ref[:, :Q_LORA].astype(F32), gq)
        dqd, dg = _rms_bwd(qh, rq, gq, _dot_nt(dqu_o[...], w_uq[...]))
        dgq_a[...] += dg
        dproj_o[:, :Q_LORA] = dqd.astype(dproj_o.dtype)
        gkv = g_kv[...]
        _, kh, rk = _rms(ps_ref[:, Q_LORA:Q_LORA + KV_LORA].astype(F32), gkv)
        dkvd, dg = _rms_bwd(kh, rk, gkv, _dot_nt(dk_ref[...], w_k[...]) + _dot_nt(dv_ref[...], w_v[...]))
        dgkv_a[...] += dg
        dproj_o[:, Q_LORA:Q_LORA + KV_LORA] = dkvd.astype(dproj_o.dtype)
        dproj_o[:, Q_LORA + KV_LORA:SMALL_COLS] = _unrope(_dot_nt(dk_ref[...], e_mat[...]), cc_, sa_).astype(dproj_o.dtype)
        row = i * tm + lax.broadcasted_iota(jnp.int32, (tm + HALO, 1), 0)
        for gi, wdw in enumerate(POOL_WINDOWS):
            cs = slice(gi * POOL_GROUP, (gi + 1) * POOL_GROUP)
            inv = 1.0 / jnp.minimum(row + 1, wdw).astype(F32)
            ext[pl.ds(0, tm), cs] = dd_ref[:, cs] * inv[:tm]
            ext[pl.ds(tm, HALO), cs] = jnp.where(i < n_tiles - 1, ddh_ref[:, cs] * inv[tm:], 0.0)
            s = ext[pl.ds(0, tm), cs]
            for k_ in range(1, wdw):
                s = s + ext[pl.ds(k_, tm), cs]
            dproj_o[:, SMALL_COLS + gi * POOL_GROUP:SMALL_COLS + (gi + 1) * POOL_GROUP] = (s - dd_ref[:, cs]).astype(dproj_o.dtype)
        dproj_o[:, SMALL_COLS + POOL_WIDTH:] = dgl_ref[...]
        da = _dot_nt(dproj_o[...], w_in[...])
        gp = g_pre[...]
        _, xh, rx = _rms(x_ref[...], gp)
        dx, dg = _rms_bwd(xh, rx, gp, da)
        dgpre_a[...] += dg
        dx_o[...] = dh1_ref[...] + dx

    outs = [_sds(n_tok, QK_WIDTH, MX), _sds(n_tok, IN_PAD, MX), _sds(n_tok, D_MODEL, F32)]
    accs = [_sds(1, Q_LORA, F32), _sds(1, KV_LORA, F32), _sds(1, D_MODEL, F32)]
    res = [w["w_uq"], w["g_q"], w["w_k"], w["w_v"], w["e_mat"], w["g_kv"], w["w_in"], w["g_pre_mix"]]
    return _tok_call("bwd_inproj", body, n_tok, tm, [dq, dk, dv, dd, (dd, halo_spec), dgl, ps, x, dh1, cc, sa], res, outs, accs,
                     scratch=[pltpu.VMEM((tm + HALO, POOL_WIDTH), F32)])


def _xtdy(name, x, dy, bk, bt):
    n_tok, kk = x.shape
    nn = dy.shape[1]

    def kern(x_ref, dy_ref, o_ref):
        @pl.when(pl.program_id(1) == 0)
        def _():
            o_ref[...] = jnp.zeros(o_ref.shape, F32)
        o_ref[...] += _dot_tn(x_ref[...], dy_ref[...])

    return pl.pallas_call(
        kern, name=name, grid=(kk // bk, n_tok // bt),
        in_specs=[pl.BlockSpec((bt, bk), lambda a, t: (t, a)), pl.BlockSpec((bt, nn), lambda a, t: (t, 0))],
        out_specs=pl.BlockSpec((bk, nn), lambda a, t: (a, 0)), out_shape=_sds(kk, nn, F32),
        compiler_params=_params(("arbitrary", "arbitrary")))(x, dy)


def _rope_tables(positions):
    inv_freq = ROPE_THETA ** (-jnp.arange(0, QK_ROPE, 2, dtype=F32) / QK_ROPE)
    ang = positions.astype(F32)[:, None] * inv_freq
    cos, sin = jnp.cos(ang), jnp.sin(ang)
    n_tok = positions.shape[0]
    ones, z64 = jnp.ones((n_tok, ROPE_LANE), F32), jnp.zeros((n_tok, ROPE_LANE), F32)
    z32 = jnp.zeros((n_tok, HEAD_SLOT - ROPE_LANE - QK_ROPE), F32)
    return jnp.concatenate([ones, cos, cos, z32], 1), jnp.concatenate([z64, -sin, sin, z32], 1)


def _kernel_weights(full):
    w_in, w_uq, w_ukv = full["w_in"], full["w_uq"], full["w_ukv"]
    c0 = Q_LORA + KV_LORA
    z = lambda n: jnp.zeros((D_MODEL, n), w_in.dtype)
    w = dict(full)
    w["w_in"] = jnp.concatenate([w_in[:, :c0], z(ROPE_LANE), w_in[:, c0:c0 + QK_ROPE], z(HEAD_SLOT - ROPE_LANE - QK_ROPE),
                                 w_in[:, c0 + QK_ROPE:]], 1)
    w["w_uq"] = jnp.pad(w_uq.reshape(Q_LORA, N_HEADS, QK_NOPE + QK_ROPE),
                        ((0, 0), (0, 0), (0, HEAD_SLOT - QK_NOPE - QK_ROPE))).reshape(Q_LORA, QK_WIDTH)
    kv = w_ukv.reshape(KV_LORA, N_HEADS, QK_NOPE + V_HEAD)
    w["w_k"] = jnp.pad(kv[:, :, :QK_NOPE], ((0, 0), (0, 0), (0, HEAD_SLOT - QK_NOPE))).reshape(KV_LORA, QK_WIDTH)
    w["w_v"] = kv[:, :, QK_NOPE:].reshape(KV_LORA, N_HEADS * V_HEAD)
    e = np.zeros((HEAD_SLOT, QK_WIDTH), np.float32)
    sel = np.zeros((N_HEADS * V_HEAD, HEAD_SLOT), np.float32)
    for h in range(N_HEADS):
        for r in range(QK_ROPE):
            e[ROPE_LANE + r, h * HEAD_SLOT + ROPE_LANE + r] = 1.0
        sel[h * V_HEAD:(h + 1) * V_HEAD, h] = 1.0
    w["e_mat"] = jnp.asarray(e, MX)
    w["head_sel"] = jnp.asarray(sel, MX)
    w["w_pool"] = full["w_pool"].astype(MX)
    return w


def _local_step(x, p, positions, target, full):
    n_tok = x.shape[0]
    tm = min(512, n_tok)
    tm_mlp = min(256, n_tok)
    tq = min(512, n_tok)
    w = _kernel_weights(full)
    cc, sa = _rope_tables(positions)

    a, ps, u, gl, qn, kvn, q, k, v = _fwd_inproj(x, cc, sa, w, tm)
    attn_t, lse = _attn_fwd(q, k, v.T, tq)
    attn = attn_t.T
    d, pooled, a_br, p_br, merged, y, h1 = _fwd_mix(x, u, gl, attn, w, tm)
    m, zr, a2, f, h2 = _fwd_mlp(h1, w, tm_mlp)
    dh2, de, dzg, loss_cols, dg_ple = _ple_fwd_bwd(h2, p, target, w, tm)
    df, dz, dh1, dg_post_mlp, dg_pre_mlp = _bwd_mlp(dh2, f, h1, zr, w, tm_mlp)
    dy, da_br, dp_br, dgl, do, delta, dyp, dd, dg_post_mix, db_gate, dpool_scale = _bwd_mix(dh1, y, a_br, p_br, gl, attn, d, w, tm)
    dq_t, dk, dv = _attn_bwd(q, k, k.T, v, do, lse, delta[:, :N_HEADS].T, tq)
    dq = dq_t.T.astype(MX)
    dqu, dproj, dx, dg_q, dg_kv, dg_pre_mix = _bwd_inproj(dq, dk, dv, dd, dgl, ps, x, dh1, cc, sa, w, tm)

    bt = min(512, n_tok)
    g_in = _xtdy("dw_in", a, dproj, 256, bt)
    g_uq = _xtdy("dw_uq", qn, dqu, Q_LORA, bt)
    g_k = _xtdy("dw_k", kvn, dk, KV_LORA, bt)
    g_v = _xtdy("dw_v", kvn, dv, KV_LORA, bt)
    g_pool = _xtdy("dw_pool", d, dyp, POOL_WIDTH, bt)
    g_ba = _xtdy("dw_ba", attn, da_br, 512, bt)
    g_bp = _xtdy("dw_bp", pooled, dp_br, 512, bt)
    g_out = _xtdy("dw_out", merged, dy, 512, bt)
    g_ff1 = _xtdy("dw_ff1", m, dz, 256, bt)
    g_ff2 = _xtdy("dw_ff2", a2, df, 1024, bt)
    g_pe = _xtdy("dw_pe", p, de, PLE_DIM, bt)
    g_pg = _xtdy("dw_pg", h2, dzg, 512, bt)

    c0 = Q_LORA + KV_LORA
    grads = {
        "g_pre_mix": dg_pre_mix,
        "w_in": jnp.concatenate([g_in[:, :c0], g_in[:, c0 + ROPE_LANE:c0 + ROPE_LANE + QK_ROPE], g_in[:, SMALL_COLS:]], 1),
        "b_gate": db_gate,
        "g_q": dg_q,
        "w_uq": g_uq.reshape(Q_LORA, N_HEADS, HEAD_SLOT)[:, :, :QK_NOPE + QK_ROPE].reshape(Q_LORA, N_HEADS * (QK_NOPE + QK_ROPE)),
        "g_kv": dg_kv,
        "w_ukv": jnp.concatenate([g_k.reshape(KV_LORA, N_HEADS, HEAD_SLOT)[:, :, :QK_NOPE],
                                  g_v.reshape(KV_LORA, N_HEADS, V_HEAD)], 2).reshape(KV_LORA, N_HEADS * (QK_NOPE + V_HEAD)),
        "w_pool": jnp.stack([g_pool[g * POOL_GROUP:(g + 1) * POOL_GROUP, g * POOL_GROUP:(g + 1) * POOL_GROUP]
                             for g in range(len(POOL_WINDOWS))]),
        "pool_scale": dpool_scale,
        "w_branch_attn": g_ba,
        "w_branch_pool": g_bp,
        "w_out": g_out,
        "g_post_mix": dg_post_mix,
        "g_pre_mlp": dg_pre_mlp,
        "w_ff1": g_ff1,
        "w_ff2": g_ff2,
        "g_post_mlp": dg_post_mlp,
        "w_ple_proj": g_pe,
        "w_ple_gate": g_pg,
        "g_ple": dg_ple,
    }
    return loss_cols, dx, grads


def _place():
    return lax.axis_index("x"), lax.axis_index("y"), lax.axis_index("c")


CHIP_FLIPS = ((1, 0), (0, 1), (1, 1))


def _flip(x, y, fx, fy):
    return (1 - x if fx else x), (1 - y if fy else y)


_HBM = pl.BlockSpec(memory_space=pl.ANY)


def _allgather_shards(wp):
    rows = wp.shape[0]
    half = rows // 2

    def body(w_ref, out_ref, send_sems, recv_sems, local_sem):
        x, y, c = _place()
        my_chip = 2 * x + y
        sibling = (x, y, 1 - c)

        def half_of(chip, hc):
            return out_ref.at[chip, pl.ds(pl.multiple_of(hc * half, 16), half), :]

        src = w_ref.at[pl.ds(pl.multiple_of(c * half, 16), half), :]
        mine = pltpu.make_async_copy(w_ref, out_ref.at[my_chip], local_sem)
        mine.start()
        chips = [_flip(x, y, fx, fy) for fx, fy in CHIP_FLIPS]
        first = []
        for j, (px, py) in enumerate(chips):
            cp = pltpu.make_async_remote_copy(src, half_of(my_chip, c), send_sems.at[j], recv_sems.at[j],
                                              device_id=(px, py, c), device_id_type=MESH)
            cp.start()
            first.append(cp)
        passed = []
        for j, (px, py) in enumerate(chips):
            landed = half_of(2 * px + py, c)
            pltpu.make_async_remote_copy(src, landed, send_sems.at[j], recv_sems.at[j],
                                         device_id=(px, py, c), device_id_type=MESH).wait_recv()
            cp = pltpu.make_async_remote_copy(landed, landed, send_sems.at[3 + j], recv_sems.at[3 + j],
                                              device_id=sibling, device_id_type=MESH)
            cp.start()
            passed.append(cp)
        for j, (px, py) in enumerate(chips):
            theirs = half_of(2 * px + py, 1 - c)
            pltpu.make_async_remote_copy(theirs, theirs, send_sems.at[3 + j], recv_sems.at[3 + j],
                                         device_id=sibling, device_id_type=MESH).wait_recv()
        for cp in first + passed:
            cp.wait_send()
        mine.wait()

    return pl.pallas_call(
        body, name="allgather_shards", out_shape=jax.ShapeDtypeStruct((N_CHIPS, rows, PACK_COLS), wp.dtype),
        in_specs=[_HBM], out_specs=_HBM,
        scratch_shapes=[pltpu.SemaphoreType.DMA((6,)), pltpu.SemaphoreType.DMA((6,)), pltpu.SemaphoreType.DMA],
    )(wp)


def _exchange_halves(g):
    rows = g.shape[1]
    half = rows // 2

    def body(g_ref, r_ref, send_sem, recv_sem):
        x, y, c = _place()
        src = g_ref.at[:, pl.ds(pl.multiple_of((1 - c) * half, 8), half), :]
        cp = pltpu.make_async_remote_copy(src, r_ref, send_sem, recv_sem, device_id=(x, y, 1 - c), device_id_type=MESH)
        cp.start()
        cp.wait()

    return pl.pallas_call(
        body, name="exchange_halves", out_shape=jax.ShapeDtypeStruct((N_CHIPS, half, PACK_COLS), g.dtype),
        in_specs=[_HBM], out_specs=_HBM, scratch_shapes=[pltpu.SemaphoreType.DMA, pltpu.SemaphoreType.DMA],
    )(g)


def _add_halves(g, r, c):
    rows = g.shape[1]
    half = rows // 2
    br = half // 8

    def kern(c_ref, g_ref, r_ref, o_ref):
        o_ref[...] = g_ref[...] + r_ref[...]

    gs = pltpu.PrefetchScalarGridSpec(
        num_scalar_prefetch=1, grid=(N_CHIPS, 8),
        in_specs=[pl.BlockSpec((1, br, PACK_COLS), lambda k, t, c: (k, c[0] * 8 + t, 0)),
                  pl.BlockSpec((1, br, PACK_COLS), lambda k, t, c: (k, t, 0))],
        out_specs=pl.BlockSpec((1, br, PACK_COLS), lambda k, t, c: (k, t, 0)))
    return pl.pallas_call(kern, name="add_halves", grid_spec=gs,
                          out_shape=jax.ShapeDtypeStruct((N_CHIPS, half, PACK_COLS), F32),
                          compiler_params=_params(("arbitrary", "arbitrary")))(c.reshape(1), g, r)


def _scatter_pieces(s):
    def body(s_ref, r_ref, send_sems, recv_sems, local_sem):
        x, y, c = _place()
        my_chip = 2 * x + y
        mine = pltpu.make_async_copy(s_ref.at[my_chip], r_ref.at[my_chip], local_sem)
        mine.start()
        chips = [_flip(x, y, fx, fy) for fx, fy in CHIP_FLIPS]
        sent = []
        for j, (px, py) in enumerate(chips):
            cp = pltpu.make_async_remote_copy(s_ref.at[2 * px + py], r_ref.at[my_chip], send_sems.at[j], recv_sems.at[j],
                                              device_id=(px, py, c), device_id_type=MESH)
            cp.start()
            sent.append(cp)
        for j, (px, py) in enumerate(chips):
            slot = r_ref.at[2 * px + py]
            pltpu.make_async_remote_copy(slot, slot, send_sems.at[j], recv_sems.at[j],
                                         device_id=(px, py, c), device_id_type=MESH).wait_recv()
        for cp in sent:
            cp.wait_send()
        mine.wait()

    return pl.pallas_call(
        body, name="scatter_pieces", out_shape=jax.ShapeDtypeStruct(s.shape, s.dtype), in_specs=[_HBM], out_specs=_HBM,
        scratch_shapes=[pltpu.SemaphoreType.DMA((3,)), pltpu.SemaphoreType.DMA((3,)), pltpu.SemaphoreType.DMA],
    )(s)


def _sum_pieces(r):
    half = r.shape[1]
    br = half // 8

    def kern(r_ref, o_ref):
        o_ref[...] = ((r_ref[0] + r_ref[1]) + r_ref[2]) + r_ref[3]

    return pl.pallas_call(
        kern, name="sum_pieces", grid=(8,), in_specs=[pl.BlockSpec((N_CHIPS, br, PACK_COLS), lambda t: (0, t, 0))],
        out_specs=pl.BlockSpec((br, PACK_COLS), lambda t: (t, 0)), out_shape=_sds(half, PACK_COLS, F32),
        compiler_params=_params(("arbitrary",)))(r)


def _join_halves(f):
    half = f.shape[0]

    def body(f_ref, o_ref, send_sem, recv_sem, local_sem):
        x, y, c = _place()
        here = o_ref.at[pl.ds(pl.multiple_of(c * half, 8), half), :]
        mine = pltpu.make_async_copy(f_ref, here, local_sem)
        mine.start()
        cp = pltpu.make_async_remote_copy(f_ref, here, send_sem, recv_sem, device_id=(x, y, 1 - c), device_id_type=MESH)
        cp.start()
        cp.wait()
        mine.wait()

    return pl.pallas_call(
        body, name="join_halves", out_shape=_sds(2 * half, PACK_COLS, f.dtype), in_specs=[_HBM], out_specs=_HBM,
        scratch_shapes=[pltpu.SemaphoreType.DMA, pltpu.SemaphoreType.DMA, pltpu.SemaphoreType.DMA],
    )(f)


def _allreduce_small(g):
    n_dev = 8

    def body(g_ref, o_ref, buf, send_sems, recv_sems):
        x, y, c = _place()
        me = 4 * x + 2 * y + c
        buf[me] = g_ref[...]
        peers = []
        for f in range(1, n_dev):
            px, py = _flip(x, y, f & 4, f & 2)
            pc = 1 - c if f & 1 else c
            peers.append((px, py, pc))
        sent = []
        for f, peer in enumerate(peers):
            cp = pltpu.make_async_remote_copy(g_ref, buf.at[me], send_sems.at[f], recv_sems.at[f], device_id=peer,
                                              device_id_type=MESH)
            cp.start()
            sent.append(cp)
        for f, (px, py, pc) in enumerate(peers):
            slot = buf.at[4 * px + 2 * py + pc]
            pltpu.make_async_remote_copy(slot, slot, send_sems.at[f], recv_sems.at[f], device_id=(px, py, pc),
                                         device_id_type=MESH).wait_recv()
        for cp in sent:
            cp.wait_send()
        total = buf[0]
        for k in range(1, n_dev):
            total = total + buf[k]
        o_ref[...] = total

    vmem = pl.BlockSpec(memory_space=pltpu.VMEM)
    return pl.pallas_call(
        body, name="allreduce_small", out_shape=jax.ShapeDtypeStruct(g.shape, g.dtype), in_specs=[vmem], out_specs=vmem,
        scratch_shapes=[pltpu.VMEM((n_dev,) + g.shape, g.dtype), pltpu.SemaphoreType.DMA((n_dev - 1,)),
                        pltpu.SemaphoreType.DMA((n_dev - 1,))],
    )(g)


def _adamw(g, w, m, v, br):
    rows = g.shape[0]
    c1 = 1.0 - ADAM_B1 ** ADAM_STEP
    c2 = 1.0 - ADAM_B2 ** ADAM_STEP

    def kern(g_ref, w_ref, m_ref, v_ref, d_o, m_o, v_o):
        g_ = g_ref[...]
        m_new = ADAM_B1 * m_ref[...] + (1.0 - ADAM_B1) * g_
        v_new = ADAM_B2 * v_ref[...] + (1.0 - ADAM_B2) * (g_ * g_)
        m_o[...] = m_new
        v_o[...] = v_new
        d_o[...] = -ADAM_LR * ((m_new / c1) / (jnp.sqrt(v_new / c2) + ADAM_EPS) + ADAM_WD * w_ref[...])

    spec = pl.BlockSpec((br, PACK_COLS), lambda t: (t, 0))
    out = _sds(rows, PACK_COLS, F32)
    return pl.pallas_call(kern, name="adamw", grid=(rows // br,), in_specs=[spec] * 4, out_specs=[spec] * 3,
                          out_shape=[out, out, out], compiler_params=_params(("arbitrary",)))(g, w, m, v)


def _shard_rows(shape, axis):
    k, n = shape
    return (k * n // N_CHIPS) // PACK_COLS


def _pack_shards(shards, dtype):
    parts = [shards[name].astype(dtype).reshape(-1, PACK_COLS) for name, _, _ in SHARDED]
    used = sum(p.shape[0] for p in parts)
    parts.append(jnp.zeros((BIG_ROWS - used, PACK_COLS), dtype))
    return jnp.concatenate(parts, 0)


def _unpack_shards(packed):
    out, r0 = {}, 0
    for name, (k, n), axis in SHARDED:
        nr = _shard_rows((k, n), axis)
        shape = (k // N_CHIPS, n) if axis == 0 else (k, n // N_CHIPS)
        out[name] = packed[r0:r0 + nr].reshape(shape)
        r0 += nr
    return out


def _unpack_full(gathered):
    out, r0 = {}, 0
    for name, (k, n), axis in SHARDED:
        nr = _shard_rows((k, n), axis)
        part = gathered[:, r0:r0 + nr]
        if axis == 0:
            out[name] = part.reshape(k, n)
        else:
            out[name] = part.reshape(N_CHIPS, k, n // N_CHIPS).transpose(1, 0, 2).reshape(k, n)
        r0 += nr
    return out


def _pack_pieces(grads):
    parts = []
    for name, (k, n), axis in SHARDED:
        g = grads[name]
        if axis == 0:
            parts.append(g.reshape(N_CHIPS, -1, PACK_COLS))
        else:
            parts.append(g.reshape(k, N_CHIPS, n // N_CHIPS).transpose(1, 0, 2).reshape(N_CHIPS, -1, PACK_COLS))
    used = sum(p.shape[1] for p in parts)
    parts.append(jnp.zeros((N_CHIPS, BIG_ROWS - used, PACK_COLS), F32))
    return jnp.concatenate(parts, 1)


def _pack_small(vals):
    flat = jnp.concatenate([vals[name].astype(F32).reshape(-1) for name, _ in SMALL])
    flat = jnp.concatenate([flat, jnp.zeros((SMALL_ROWS * PACK_COLS - flat.shape[0],), F32)])
    return flat.reshape(SMALL_ROWS, PACK_COLS)


def _unpack_small(packed):
    flat, out, o = packed.reshape(-1), {}, 0
    for name, shape in SMALL:
        n = int(np.prod(shape))
        out[name] = flat[o:o + n].reshape(shape)
        o += n
    return out


def kernel(x, p, positions, g_pre_mix, w_in, b_gate, g_q, w_uq, g_kv, w_ukv, w_pool, pool_scale, w_branch_attn, w_branch_pool, w_out, g_post_mix, g_pre_mlp, w_ff1, w_ff2, g_post_mlp, w_ple_proj, w_ple_gate, g_ple, loss_target, m_g_pre_mix, m_w_in, m_b_gate, m_g_q, m_w_uq, m_g_kv, m_w_ukv, m_w_pool, m_pool_scale, m_w_branch_attn, m_w_branch_pool, m_w_out, m_g_post_mix, m_g_pre_mlp, m_w_ff1, m_w_ff2, m_g_post_mlp, m_w_ple_proj, m_w_ple_gate, m_g_ple, v_g_pre_mix, v_w_in, v_b_gate, v_g_q, v_w_uq, v_g_kv, v_w_ukv, v_w_pool, v_pool_scale, v_w_branch_attn, v_w_branch_pool, v_w_out, v_g_post_mix, v_g_pre_mlp, v_w_ff1, v_w_ff2, v_g_post_mlp, v_w_ple_proj, v_w_ple_gate, v_g_ple):
    given = dict(locals())
    weights = {n: given[n] for n in WEIGHT_ORDER}
    moments_m = {n: given["m_" + n] for n in WEIGHT_ORDER}
    moments_v = {n: given["v_" + n] for n in WEIGHT_ORDER}
    c = lax.axis_index("c")

    big_w = {name: weights[name][0] for name, _, _ in SHARDED}
    gathered = _allgather_shards(_pack_shards(big_w, MX))
    full = _unpack_full(gathered)
    for name, _ in SMALL:
        full[name] = weights[name][0] if name == "w_pool" else weights[name]

    loss_cols, dx, grads = _local_step(x[0], p[0, 0], positions[0], loss_target[0], full)
    loss = lax.psum(0.5 * jnp.sum(loss_cols) / D_MODEL, ("x", "y", "c"))

    pieces = _pack_pieces(grads)
    summed = _add_halves(pieces, _exchange_halves(pieces), c)
    g_big = _join_halves(_sum_pieces(_scatter_pieces(summed)))
    g_small = _allreduce_small(_pack_small(grads))

    d_big, m_big, v_big = _adamw(g_big, _pack_shards(big_w, F32), _pack_shards({n: moments_m[n][0] for n, _, _ in SHARDED}, F32),
                                 _pack_shards({n: moments_v[n][0] for n, _, _ in SHARDED}, F32), BIG_ROWS // 10)
    d_small, m_small, v_small = _adamw(g_small, _pack_small(weights), _pack_small(moments_m), _pack_small(moments_v), SMALL_ROWS)

    def unpack(big, small):
        out = {n: a[None] for n, a in _unpack_shards(big).items()}
        out.update(_unpack_small(small))
        return [out[n] for n in WEIGHT_ORDER]

    return (loss, dx[None], *unpack(g_big, g_small), *unpack(d_big, d_small), *unpack(m_big, m_small), *unpack(v_big, v_small))
```

```python
import functools

import numpy as np
import jax
import jax.numpy as jnp
from jax import lax
from jax.experimental import pallas as pl
from jax.experimental.pallas import tpu as pltpu

F32 = jnp.float32
MX = jnp.bfloat16
WIRE = jnp.bfloat16

D_MODEL = 1024
N_HEADS = 8
QK_NOPE = 64
QK_ROPE = 32
V_HEAD = 64
Q_LORA = 384
KV_LORA = 256
POOL_WINDOWS = (2, 4, 8, 16)
POOL_GROUP = 128
POOL_WIDTH = 512
D_FF = 4096
PLE_DIM = 256
ROPE_THETA = 10000.0
EPS = 1e-6
HEAD_SLOT = 128
QK_WIDTH = N_HEADS * HEAD_SLOT
ROPE_LANE = 64
SMALL_COLS = Q_LORA + KV_LORA + HEAD_SLOT
IN_PAD = SMALL_COLS + POOL_WIDTH + 2 * D_MODEL
SCALE = (QK_NOPE + QK_ROPE) ** -0.5
LOG2E = 1.4426950408889634
NEG = -1e30
HALO = 16

ADAM_LR = 0.001
ADAM_B1 = 0.9
ADAM_B2 = 0.999
ADAM_EPS = 1e-08
ADAM_WD = 0.01
ADAM_STEP = 10

VMEM_LIMIT = 56 * 2**20
MESH = pl.DeviceIdType.MESH

SHARDED = (
    ("w_in", (1024, 3232), 1),
    ("w_uq", (384, 768), 1),
    ("w_ukv", (256, 1024), 1),
    ("w_branch_attn", (512, 1024), 1),
    ("w_branch_pool", (512, 1024), 1),
    ("w_out", (1024, 1024), 0),
    ("w_ff1", (1024, 4096), 1),
    ("w_ff2", (4096, 1024), 0),
    ("w_ple_proj", (256, 1024), 1),
    ("w_ple_gate", (1024, 1024), 0),
)
SMALL = (
    ("g_pre_mix", (1, 1024)),
    ("b_gate", (1, 2048)),
    ("g_q", (1, 384)),
    ("g_kv", (1, 256)),
    ("w_pool", (1, 4, 128, 128)),
    ("pool_scale", (1, 512)),
    ("g_post_mix", (1, 1024)),
    ("g_pre_mlp", (1, 1024)),
    ("g_post_mlp", (1, 1024)),
    ("g_ple", (1, 1024)),
)
WEIGHT_ORDER = ("g_pre_mix", "w_in", "b_gate", "g_q", "w_uq", "g_kv", "w_ukv", "w_pool", "pool_scale", "w_branch_attn",
                "w_branch_pool", "w_out", "g_post_mix", "g_pre_mlp", "w_ff1", "w_ff2", "g_post_mlp", "w_ple_proj",
                "w_ple_gate", "g_ple")
N_CHIPS = 4
PACK_COLS = 1024
BIG_ROWS = 3840
SMALL_ROWS = 80


def _dot(a, b):
    return jnp.dot(a.astype(MX), b.astype(MX), preferred_element_type=F32)


def _dot_nt(a, b):
    return lax.dot_general(a.astype(MX), b.astype(MX), (((1,), (1,)), ((), ())), preferred_element_type=F32)


def _dot_tn(a, b):
    return lax.dot_general(a.astype(MX), b.astype(MX), (((0,), (0,)), ((), ())), preferred_element_type=F32)


def _sig(x):
    return 1.0 / (1.0 + jnp.exp(-x))


def _rms(x, g):
    r = lax.rsqrt(jnp.mean(x * x, axis=1, keepdims=True) + EPS)
    xh = x * r
    return xh * g, xh, r


def _rms_bwd(xh, r, g, dy):
    dxn = dy * g
    dx = r * (dxn - xh * jnp.mean(dxn * xh, axis=1, keepdims=True))
    return dx, jnp.sum(dy * xh, axis=0, keepdims=True)


def _rot_half(v):
    lane = lax.broadcasted_iota(jnp.int32, v.shape, 1)
    return jnp.where(lane < ROPE_LANE + QK_ROPE // 2, pltpu.roll(v, HEAD_SLOT - QK_ROPE // 2, 1), pltpu.roll(v, QK_ROPE // 2, 1))


def _rope(v, cc, sa):
    return v * cc + _rot_half(v) * sa


def _unrope(v, cc, sa):
    return v * cc - _rot_half(v) * sa


def _params(sem):
    return pltpu.CompilerParams(dimension_semantics=sem, vmem_limit_bytes=VMEM_LIMIT)


def _tok_call(name, body, n_tok, tm, tiled, resident, outs, accs=(), scratch=()):
    def as_pair(t):
        if isinstance(t, tuple):
            return t
        return t, pl.BlockSpec((tm, t.shape[1]), lambda i: (i, 0))
    tiled = [as_pair(t) for t in tiled]
    res_specs = [pl.BlockSpec(r.shape, lambda i, nd=r.ndim: (0,) * nd, pipeline_mode=pl.Buffered(1)) for r in resident]
    out_specs = [pl.BlockSpec((tm, o.shape[1]), lambda i: (i, 0)) for o in outs]
    out_specs += [pl.BlockSpec(a.shape, lambda i: (0, 0)) for a in accs]
    n_t, n_r, n_o, n_a = len(tiled), len(resident), len(outs), len(accs)

    def kern(*refs):
        tin, res = refs[:n_t], refs[n_t:n_t + n_r]
        tout = refs[n_t + n_r:n_t + n_r + n_o]
        acc = refs[n_t + n_r + n_o:n_t + n_r + n_o + n_a]
        scr = refs[n_t + n_r + n_o + n_a:]
        i = pl.program_id(0)

        @pl.when(i == 0)
        def _():
            for a in acc:
                a[...] = jnp.zeros(a.shape, a.dtype)
        body(i, tin, res, tout, acc, scr)

    return pl.pallas_call(
        kern, name=name, grid=(n_tok // tm,), in_specs=[s for _, s in tiled] + res_specs, out_specs=out_specs,
        out_shape=list(outs) + list(accs), scratch_shapes=list(scratch), compiler_params=_params(("arbitrary",)),
    )(*[a for a, _ in tiled], *resident)


def _sds(rows, cols, dtype):
    return jax.ShapeDtypeStruct((rows, cols), dtype)


def _fwd_inproj(x, cc, sa, w, tm):
    n_tok = x.shape[0]

    def body(i, tin, res, tout, acc, scr):
        x_ref, c_ref, s_ref = tin
        g_pre, w_in, g_q, w_uq, g_kv, w_k, w_v, e_mat = res
        a_o, ps_o, u_o, gl_o, qn_o, kvn_o, q_o, k_o, v_o = tout
        a = _rms(x_ref[...], g_pre[...])[0].astype(MX)
        a_o[...] = a
        ps = _dot(a, w_in[:, :SMALL_COLS])
        ps_o[...] = ps.astype(ps_o.dtype)
        u_o[...] = _dot(a, w_in[:, SMALL_COLS:SMALL_COLS + POOL_WIDTH]).astype(u_o.dtype)
        gl_o[...] = _dot(a, w_in[:, SMALL_COLS + POOL_WIDTH:]).astype(gl_o.dtype)
        cc_, sa_ = c_ref[...], s_ref[...]
        qn = _rms(ps[:, :Q_LORA], g_q[...])[0].astype(MX)
        qn_o[...] = qn
        q = _dot(qn, w_uq[...])
        for h in range(N_HEADS):
            hs = slice(h * HEAD_SLOT, (h + 1) * HEAD_SLOT)
            q_o[:, hs] = (_rope(q[:, hs], cc_, sa_) * (SCALE * LOG2E)).astype(q_o.dtype)
        kvn = _rms(ps[:, Q_LORA:Q_LORA + KV_LORA], g_kv[...])[0].astype(MX)
        kvn_o[...] = kvn
        kr = _rope(ps[:, Q_LORA + KV_LORA:], cc_, sa_)
        k_o[...] = (_dot(kvn, w_k[...]) + _dot(kr, e_mat[...])).astype(k_o.dtype)
        v_o[...] = _dot(kvn, w_v[...]).astype(v_o.dtype)

    outs = [_sds(n_tok, D_MODEL, MX), _sds(n_tok, SMALL_COLS, MX), _sds(n_tok, POOL_WIDTH, MX), _sds(n_tok, 2 * D_MODEL, MX),
            _sds(n_tok, Q_LORA, MX), _sds(n_tok, KV_LORA, MX), _sds(n_tok, QK_WIDTH, MX), _sds(n_tok, QK_WIDTH, MX),
            _sds(n_tok, N_HEADS * V_HEAD, MX)]
    res = [w["g_pre_mix"], w["w_in"], w["g_q"], w["w_uq"], w["g_kv"], w["w_k"], w["w_v"], w["e_mat"]]
    return _tok_call("fwd_inproj", body, n_tok, tm, [x, cc, sa], res, outs)


def _causal_pairs(n, by_kv):
    if by_kv:
        pairs = [(i, j) for j in range(n) for i in range(j, n)]
    else:
        pairs = [(i, j) for i in range(n) for j in range(i + 1)]
    return (jnp.asarray(np.array([p[0] for p in pairs], np.int32)), jnp.asarray(np.array([p[1] for p in pairs], np.int32)))


def _keep_t(tq):
    return lax.broadcasted_iota(jnp.int32, (tq, tq), 0) <= lax.broadcasted_iota(jnp.int32, (tq, tq), 1)


def _attn_fwd(q, k, vt, tq):
    n_tok = q.shape[0]
    n = n_tok // tq
    qi, kj = _causal_pairs(n, by_kv=False)

    def kern(qi_ref, kj_ref, q_ref, k_ref, vt_ref, ot_ref, lse_ref, m_s, l_s, acc_s, st_s):
        s_id = pl.program_id(0)
        i, j = qi_ref[s_id], kj_ref[s_id]

        @pl.when(j == 0)
        def _():
            m_s[...] = jnp.full(m_s.shape, NEG, F32)
            l_s[...] = jnp.zeros(l_s.shape, F32)
            acc_s[...] = jnp.zeros(acc_s.shape, F32)

        def scores(h):
            hs = slice(h * HEAD_SLOT, (h + 1) * HEAD_SLOT)
            return _dot_nt(k_ref[:, hs], q_ref[:, hs])

        def heads(masked):
            st_s[0] = scores(0)
            for h in range(N_HEADS):
                if h + 1 < N_HEADS:
                    st_s[(h + 1) % 2] = scores(h + 1)
                st = st_s[h % 2]
                if masked:
                    st = jnp.where(_keep_t(tq), st, NEG)
                m_old = m_s[h]
                m_new = jnp.maximum(m_old, jnp.max(st, axis=0, keepdims=True))
                alpha = jnp.exp2(m_old - m_new)
                pt = jnp.exp2(st - m_new)
                l_s[h] = alpha * l_s[h] + jnp.sum(pt, axis=0, keepdims=True)
                acc_s[h] = alpha * acc_s[h] + _dot(vt_ref[h * V_HEAD:(h + 1) * V_HEAD, :], pt)
                m_s[h] = m_new

        @pl.when(j < i)
        def _():
            heads(False)

        @pl.when(j == i)
        def _():
            heads(True)
            for h in range(N_HEADS):
                ot_ref[h * V_HEAD:(h + 1) * V_HEAD, :] = (acc_s[h] / l_s[h]).astype(ot_ref.dtype)
                lse_ref[h:h + 1, :] = m_s[h] + jnp.log2(l_s[h])

    gs = pltpu.PrefetchScalarGridSpec(
        num_scalar_prefetch=2, grid=(qi.shape[0],),
        in_specs=[pl.BlockSpec((tq, QK_WIDTH), lambda s, qi, kj: (qi[s], 0)),
                  pl.BlockSpec((tq, QK_WIDTH), lambda s, qi, kj: (kj[s], 0)),
                  pl.BlockSpec((N_HEADS * V_HEAD, tq), lambda s, qi, kj: (0, kj[s]))],
        out_specs=[pl.BlockSpec((N_HEADS * V_HEAD, tq), lambda s, qi, kj: (0, qi[s])),
                   pl.BlockSpec((N_HEADS, tq), lambda s, qi, kj: (0, qi[s]))],
        scratch_shapes=[pltpu.VMEM((N_HEADS, 1, tq), F32), pltpu.VMEM((N_HEADS, 1, tq), F32),
                        pltpu.VMEM((N_HEADS, V_HEAD, tq), F32), pltpu.VMEM((2, tq, tq), F32)])
    return pl.pallas_call(kern, name="attn_fwd", grid_spec=gs,
                          out_shape=[_sds(N_HEADS * V_HEAD, n_tok, MX), _sds(N_HEADS, n_tok, F32)],
                          compiler_params=_params(("arbitrary",)))(qi, kj, q, k, vt)


def _pool_windows(ext, i, tm, first_row):
    row = i * tm + lax.broadcasted_iota(jnp.int32, (tm, 1), 0)
    out = []
    for g, w in enumerate(POOL_WINDOWS):
        cs = slice(g * POOL_GROUP, (g + 1) * POOL_GROUP)
        s = ext[pl.ds(first_row, tm), cs]
        for k in range(1, w):
            s = s + ext[pl.ds(first_row - k, tm), cs]
        cnt = jnp.minimum(row + 1, w).astype(F32)
        out.append(s / cnt)
    return out


def _fwd_mix(x, u, gl, attn, w, tm):
    n_tok = x.shape[0]
    halo_spec = pl.BlockSpec((HALO, POOL_WIDTH), lambda i: (jnp.maximum(i * (tm // HALO) - 1, 0), 0))

    def body(i, tin, res, tout, acc, scr):
        x_ref, u_ref, uh_ref, gl_ref, at_ref = tin
        w_pool, pool_scale, w_ba, w_bp, b_gate, w_out, g_post = res
        d_o, pooled_o, a_o, pp_o, merged_o, y_o, h1_o = tout
        ext, = scr
        ext[pl.ds(0, HALO), :] = jnp.where(i > 0, uh_ref[...].astype(F32), 0.0)
        ext[pl.ds(HALO, tm), :] = u_ref[...].astype(F32)
        means = _pool_windows(ext, i, tm, HALO)
        for g in range(len(POOL_WINDOWS)):
            cs = slice(g * POOL_GROUP, (g + 1) * POOL_GROUP)
            d = (means[g] - ext[pl.ds(HALO, tm), cs]).astype(MX)
            d_o[:, cs] = d
            pooled_o[:, cs] = (_dot(d, w_pool[g]) * pool_scale[:, cs]).astype(pooled_o.dtype)
        a_br = _dot(at_ref[...], w_ba[...])
        p_br = _dot(pooled_o[...], w_bp[...])
        a_o[...] = a_br.astype(a_o.dtype)
        pp_o[...] = p_br.astype(pp_o.dtype)
        gates = _sig(gl_ref[...].astype(F32) + b_gate[...])
        merged = (gates[:, :D_MODEL] * a_br + gates[:, D_MODEL:] * p_br).astype(MX)
        merged_o[...] = merged
        y = _dot(merged, w_out[...])
        y_o[...] = y.astype(y_o.dtype)
        h1_o[...] = x_ref[...] + _rms(y, g_post[...])[0]

    outs = [_sds(n_tok, POOL_WIDTH, MX), _sds(n_tok, POOL_WIDTH, MX), _sds(n_tok, D_MODEL, MX), _sds(n_tok, D_MODEL, MX),
            _sds(n_tok, D_MODEL, MX), _sds(n_tok, D_MODEL, MX), _sds(n_tok, D_MODEL, F32)]
    res = [w["w_pool"], w["pool_scale"], w["w_branch_attn"], w["w_branch_pool"], w["b_gate"], w["w_out"], w["g_post_mix"]]
    return _tok_call("fwd_mix", body, n_tok, tm, [x, u, (u, halo_spec), gl, attn], res, outs,
                     scratch=[pltpu.VMEM((tm + HALO, POOL_WIDTH), F32)])


def _fwd_mlp(h1, w, tm):
    n_tok = h1.shape[0]

    def body(i, tin, res, tout, acc, scr):
        h1_ref, = tin
        g_pre, w1, w2, g_post = res
        m_o, zr_o, a2_o, f_o, h2_o = tout
        h1_ = h1_ref[...]
        m = _rms(h1_, g_pre[...])[0].astype(MX)
        m_o[...] = m
        zr = jnp.maximum(_dot(m, w1[...]), 0.0)
        zr_o[...] = zr.astype(zr_o.dtype)
        a2 = (zr * zr).astype(MX)
        a2_o[...] = a2
        f = _dot(a2, w2[...])
        f_o[...] = f.astype(f_o.dtype)
        h2_o[...] = h1_ + _rms(f, g_post[...])[0]

    outs = [_sds(n_tok, D_MODEL, MX), _sds(n_tok, D_FF, MX), _sds(n_tok, D_FF, MX), _sds(n_tok, D_MODEL, MX),
            _sds(n_tok, D_MODEL, F32)]
    res = [w["g_pre_mlp"], w["w_ff1"], w["w_ff2"], w["g_post_mlp"]]
    return _tok_call("fwd_mlp", body, n_tok, tm, [h1], res, outs)


def _ple_fwd_bwd(h2, p, target, w, tm):
    n_tok = h2.shape[0]

    def body(i, tin, res, tout, acc, scr):
        h2_ref, p_ref, t_ref = tin
        w_pe, w_pg, g_ple = res
        dh2_o, de_o, dzg_o = tout
        loss_a, dg_a = acc
        h2_ = h2_ref[...]
        e = _dot(p_ref[...], w_pe[...])
        pg = _sig(_dot(h2_, w_pg[...]))
        t = pg * e
        g = g_ple[...]
        tn, th, r = _rms(t, g)
        diff = h2_ + tn - t_ref[...]
        loss_a[...] += jnp.sum(diff * diff, axis=0, keepdims=True)
        dh3 = diff * (1.0 / D_MODEL)
        dt, dg = _rms_bwd(th, r, g, dh3)
        dg_a[...] += dg
        de_o[...] = (dt * pg).astype(de_o.dtype)
        dzg = (dt * e * pg * (1.0 - pg)).astype(MX)
        dzg_o[...] = dzg
        dh2_o[...] = dh3 + _dot_nt(dzg, w_pg[...])

    outs = [_sds(n_tok, D_MODEL, F32), _sds(n_tok, D_MODEL, MX), _sds(n_tok, D_MODEL, MX)]
    accs = [_sds(1, D_MODEL, F32), _sds(1, D_MODEL, F32)]
    return _tok_call("ple_fwd_bwd", body, n_tok, tm, [h2, p, target], [w["w_ple_proj"], w["w_ple_gate"], w["g_ple"]], outs, accs)


def _bwd_mlp(dh2, f, h1, zr, w, tm):
    n_tok = dh2.shape[0]

    def body(i, tin, res, tout, acc, scr):
        dh2_ref, f_ref, h1_ref, zr_ref = tin
        g_pre, w1, w2, g_post = res
        df_o, dz_o, dh1_o = tout
        dg_post_a, dg_pre_a = acc
        dh2_ = dh2_ref[...]
        gp = g_post[...]
        _, fh, rf = _rms(f_ref[...].astype(F32), gp)
        df, dg = _rms_bwd(fh, rf, gp, dh2_)
        dg_post_a[...] += dg
        df = df.astype(MX)
        df_o[...] = df
        dz = (_dot_nt(df, w2[...]) * (2.0 * zr_ref[...].astype(F32))).astype(MX)
        dz_o[...] = dz
        dm = _dot_nt(dz, w1[...])
        gq = g_pre[...]
        _, hh, rh = _rms(h1_ref[...], gq)
        dh1, dg = _rms_bwd(hh, rh, gq, dm)
        dg_pre_a[...] += dg
        dh1_o[...] = dh2_ + dh1

    outs = [_sds(n_tok, D_MODEL, MX), _sds(n_tok, D_FF, MX), _sds(n_tok, D_MODEL, F32)]
    accs = [_sds(1, D_MODEL, F32), _sds(1, D_MODEL, F32)]
    res = [w["g_pre_mlp"], w["w_ff1"], w["w_ff2"], w["g_post_mlp"]]
    return _tok_call("bwd_mlp", body, n_tok, tm, [dh2, f, h1, zr], res, outs, accs)


def _bwd_mix(dh1, y, a_br, p_br, gl, attn, d, w, tm):
    n_tok = dh1.shape[0]

    def body(i, tin, res, tout, acc, scr):
        dh1_ref, y_ref, a_ref, pp_ref, gl_ref, at_ref, d_ref = tin
        g_post, w_out, b_gate, w_ba, w_bp, w_pool, pool_scale, sel = res
        dy_o, da_o, dpp_o, dgl_o, do_o, delta_o, dyp_o, dd_o = tout
        dg_post_a, db_a, dps_a = acc
        g = g_post[...]
        _, yh, r = _rms(y_ref[...].astype(F32), g)
        dy, dg = _rms_bwd(yh, r, g, dh1_ref[...])
        dg_post_a[...] += dg
        dy = dy.astype(MX)
        dy_o[...] = dy
        dmerged = _dot_nt(dy, w_out[...])
        gates = _sig(gl_ref[...].astype(F32) + b_gate[...])
        ga, gp = gates[:, :D_MODEL], gates[:, D_MODEL:]
        da = (dmerged * ga).astype(MX)
        dpp = (dmerged * gp).astype(MX)
        da_o[...] = da
        dpp_o[...] = dpp
        dgl_a = dmerged * a_ref[...].astype(F32) * ga * (1.0 - ga)
        dgl_p = dmerged * pp_ref[...].astype(F32) * gp * (1.0 - gp)
        dgl_o[:, :D_MODEL] = dgl_a.astype(dgl_o.dtype)
        dgl_o[:, D_MODEL:] = dgl_p.astype(dgl_o.dtype)
        db_a[:, :D_MODEL] += jnp.sum(dgl_a, axis=0, keepdims=True)
        db_a[:, D_MODEL:] += jnp.sum(dgl_p, axis=0, keepdims=True)
        do = _dot_nt(da, w_ba[...]).astype(MX)
        do_o[...] = do
        prod = do.astype(F32) * at_ref[...].astype(F32)
        hi = prod.astype(MX)
        lo = (prod - hi.astype(F32)).astype(MX)
        delta_o[...] = _dot(hi, sel[...]) + _dot(lo, sel[...])
        dpooled = _dot_nt(dpp, w_bp[...])
        for gi in range(len(POOL_WINDOWS)):
            cs = slice(gi * POOL_GROUP, (gi + 1) * POOL_GROUP)
            ypre = _dot(d_ref[:, cs], w_pool[gi])
            dps_a[:, cs] += jnp.sum(dpooled[:, cs] * ypre, axis=0, keepdims=True)
            dyp = (dpooled[:, cs] * pool_scale[:, cs]).astype(MX)
            dyp_o[:, cs] = dyp
            dd_o[:, cs] = _dot_nt(dyp, w_pool[gi])

    outs = [_sds(n_tok, D_MODEL, MX), _sds(n_tok, D_MODEL, MX), _sds(n_tok, D_MODEL, MX), _sds(n_tok, 2 * D_MODEL, MX),
            _sds(n_tok, N_HEADS * V_HEAD, MX), _sds(n_tok, HEAD_SLOT, F32), _sds(n_tok, POOL_WIDTH, MX),
            _sds(n_tok, POOL_WIDTH, F32)]
    accs = [_sds(1, D_MODEL, F32), _sds(1, 2 * D_MODEL, F32), _sds(1, POOL_WIDTH, F32)]
    res = [w["g_post_mix"], w["w_out"], w["b_gate"], w["w_branch_attn"], w["w_branch_pool"], w["w_pool"], w["pool_scale"],
           w["head_sel"]]
    return _tok_call("bwd_mix", body, n_tok, tm, [dh1, y, a_br, p_br, gl, attn, d], res, outs, accs)


def _bwd_heads(q_ref, k_ref, v_ref, do_ref, lse_ref, dl_ref, st_s, dpt_s, masked, tq, use, n_heads):
    def products(h):
        hs = slice(h * HEAD_SLOT, (h + 1) * HEAD_SLOT)
        vs = slice(h * V_HEAD, (h + 1) * V_HEAD)
        st_s[h % 2] = _dot_nt(k_ref[:, hs], q_ref[:, hs])
        dpt_s[h % 2] = _dot_nt(v_ref[:, vs], do_ref[:, vs])

    products(0)
    for h in range(n_heads):
        if h + 1 < n_heads:
            products(h + 1)
        st = st_s[h % 2]
        if masked:
            st = jnp.where(_keep_t(tq), st, NEG)
        pt = jnp.exp2(st - lse_ref[h:h + 1, :])
        use(h, pt, pt * (dpt_s[h % 2] - dl_ref[h:h + 1, :]))


HEAD_GROUP = 4


def _attn_bwd(q, k, kt, v, do, lse, delta, tq):
    n_tok = q.shape[0]
    n = n_tok // tq
    n_groups = N_HEADS // HEAD_GROUP
    gq, gv = HEAD_GROUP * HEAD_SLOT, HEAD_GROUP * V_HEAD
    qi, kj = _causal_pairs(n, by_kv=True)

    def kern(qi_ref, kj_ref, q_ref, k_ref, kt_ref, v_ref, do_ref, lse_ref, dl_ref, dq_ref, dk_ref, dv_ref,
             dk_s, dv_s, st_s, dpt_s):
        s_id = pl.program_id(1)
        i, j = qi_ref[s_id], kj_ref[s_id]
        cols = pl.ds(pl.multiple_of(i * tq, tq), tq)

        @pl.when(s_id == 0)
        def _():
            dq_ref[...] = jnp.zeros(dq_ref.shape, F32)

        def use(h, pt, dst):
            hs = slice(h * HEAD_SLOT, (h + 1) * HEAD_SLOT)
            dv_s[h] += _dot(pt, do_ref[:, h * V_HEAD:(h + 1) * V_HEAD])
            dk_s[:, hs] += _dot(dst, q_ref[:, hs])
            dq_ref[hs, cols] += _dot(kt_ref[hs, :], dst)

        def heads(masked):
            _bwd_heads(q_ref, k_ref, v_ref, do_ref, lse_ref.at[0], dl_ref.at[0], st_s, dpt_s, masked, tq, use, HEAD_GROUP)

        @pl.when(i == j)
        def _():
            dk_s[...] = jnp.zeros(dk_s.shape, F32)
            dv_s[...] = jnp.zeros(dv_s.shape, F32)
            heads(True)

        @pl.when(i > j)
        def _():
            heads(False)

        @pl.when(i == n - 1)
        def _():
            dk_ref[...] = (dk_s[...] * (1.0 / LOG2E)).astype(dk_ref.dtype)
            for h in range(HEAD_GROUP):
                dv_ref[:, h * V_HEAD:(h + 1) * V_HEAD] = dv_s[h].astype(dv_ref.dtype)

    at_q = lambda g, s, qi, kj: (qi[s], g)
    at_k = lambda g, s, qi, kj: (kj[s], g)
    at_kt = lambda g, s, qi, kj: (g, kj[s])
    at_stat = lambda g, s, qi, kj: (g, 0, qi[s])
    gs = pltpu.PrefetchScalarGridSpec(
        num_scalar_prefetch=2, grid=(n_groups, qi.shape[0]),
        in_specs=[pl.BlockSpec((tq, gq), at_q), pl.BlockSpec((tq, gq), at_k), pl.BlockSpec((gq, tq), at_kt),
                  pl.BlockSpec((tq, gv), at_k), pl.BlockSpec((tq, gv), at_q),
                  pl.BlockSpec((1, HEAD_GROUP, tq), at_stat), pl.BlockSpec((1, HEAD_GROUP, tq), at_stat)],
        out_specs=[pl.BlockSpec((gq, n_tok), lambda g, s, qi, kj: (g, 0), pipeline_mode=pl.Buffered(1)),
                   pl.BlockSpec((tq, gq), at_k), pl.BlockSpec((tq, gv), at_k)],
        scratch_shapes=[pltpu.VMEM((tq, gq), F32), pltpu.VMEM((HEAD_GROUP, tq, V_HEAD), F32),
                        pltpu.VMEM((2, tq, tq), F32), pltpu.VMEM((2, tq, tq), F32)])
    stat3 = lambda a: a.reshape(n_groups, HEAD_GROUP, n_tok)
    return pl.pallas_call(kern, name="attn_bwd", grid_spec=gs,
                          out_shape=[_sds(QK_WIDTH, n_tok, F32), _sds(n_tok, QK_WIDTH, MX), _sds(n_tok, N_HEADS * V_HEAD, MX)],
                          compiler_params=_params(("arbitrary", "arbitrary")))(qi, kj, q, k, kt, v, do, stat3(lse), stat3(delta))


def _bwd_inproj(dq, dk, dv, dd, dgl, ps, x, dh1, cc, sa, w, tm):
    n_tok = x.shape[0]
    n_tiles = n_tok // tm
    last_halo = n_tok // HALO - 1
    halo_spec = pl.BlockSpec((HALO, POOL_WIDTH), lambda i: (jnp.minimum((i + 1) * (tm // HALO), last_halo), 0))

    def body(i, tin, res, tout, acc, scr):
        dq_ref, dk_ref, dv_ref, dd_ref, ddh_ref, dgl_ref, ps_ref, x_ref, dh1_ref, c_ref, s_ref = tin
        w_uq, g_q, w_k, w_v, e_mat, g_kv, w_in, g_pre = res
        dqu_o, dproj_o, dx_o = tout
        dgq_a, dgkv_a, dgpre_a = acc
        ext, = scr
        cc_, sa_ = c_ref[...], s_ref[...]
        for h in range(N_HEADS):
            hs = slice(h * HEAD_SLOT, (h + 1) * HEAD_SLOT)
            dqu_o[:, hs] = (_unrope(dq_ref[:, hs].astype(F32), cc_, sa_) * SCALE).astype(dqu_o.dtype)
        gq = g_q[...]
        _, qh, rq = _rms(ps_ref[:, :Q_LORA].astype(F32), gq)
        dqd, dg = _rms_bwd(qh, rq, gq, _dot_nt(dqu_o[...], w_uq[...]))
        dgq_a[...] += dg
        dproj_o[:, :Q_LORA] = dqd.astype(dproj_o.dtype)
        gkv = g_kv[...]
        _, kh, rk = _rms(ps_ref[:, Q_LORA:Q_LORA + KV_LORA].astype(F32), gkv)
        dkvd, dg = _rms_bwd(kh, rk, gkv, _dot_nt(dk_ref[...], w_k[...]) + _dot_nt(dv_ref[...], w_v[...]))
        dgkv_a[...] += dg
        dproj_o[:, Q_LORA:Q_LORA + KV_LORA] = dkvd.astype(dproj_o.dtype)
        dproj_o[:, Q_LORA + KV_LORA:SMALL_COLS] = _unrope(_dot_nt(dk_ref[...], e_mat[...]), cc_, sa_).astype(dproj_o.dtype)
        row = i * tm + lax.broadcasted_iota(jnp.int32, (tm + HALO, 1), 0)
        for gi, wdw in enumerate(POOL_WINDOWS):
            cs = slice(gi * POOL_GROUP, (gi + 1) * POOL_GROUP)
            inv = 1.0 / jnp.minimum(row + 1, wdw).astype(F32)
            ext[pl.ds(0, tm), cs] = dd_ref[:, cs] * inv[:tm]
            ext[pl.ds(tm, HALO), cs] = jnp.where(i < n_tiles - 1, ddh_ref[:, cs] * inv[tm:], 0.0)
            s = ext[pl.ds(0, tm), cs]
            for k_ in range(1, wdw):
                s = s + ext[pl.ds(k_, tm), cs]
            dproj_o[:, SMALL_COLS + gi * POOL_GROUP:SMALL_COLS + (gi + 1) * POOL_GROUP] = (s - dd_ref[:, cs]).astype(dproj_o.dtype)
        dproj_o[:, SMALL_COLS + POOL_WIDTH:] = dgl_ref[...]
        da = _dot_nt(dproj_o[...], w_in[...])
        gp = g_pre[...]
        _, xh, rx = _rms(x_ref[...], gp)
        dx, dg = _rms_bwd(xh, rx, gp, da)
        dgpre_a[...] += dg
        dx_o[...] = dh1_ref[...] + dx

    outs = [_sds(n_tok, QK_WIDTH, MX), _sds(n_tok, IN_PAD, MX), _sds(n_tok, D_MODEL, F32)]
    accs = [_sds(1, Q_LORA, F32), _sds(1, KV_LORA, F32), _sds(1, D_MODEL, F32)]
    res = [w["w_uq"], w["g_q"], w["w_k"], w["w_v"], w["e_mat"], w["g_kv"], w["w_in"], w["g_pre_mix"]]
    return _tok_call("bwd_inproj", body, n_tok, tm, [dq, dk, dv, dd, (dd, halo_spec), dgl, ps, x, dh1, cc, sa], res, outs, accs,
                     scratch=[pltpu.VMEM((tm + HALO, POOL_WIDTH), F32)])


def _xtdy(name, x, dy, bk, bt):
    n_tok, kk = x.shape
    nn = dy.shape[1]

    def kern(x_ref, dy_ref, o_ref):
        @pl.when(pl.program_id(1) == 0)
        def _():
            o_ref[...] = jnp.zeros(o_ref.shape, F32)
        o_ref[...] += _dot_tn(x_ref[...], dy_ref[...])

    return pl.pallas_call(
        kern, name=name, grid=(kk // bk, n_tok // bt),
        in_specs=[pl.BlockSpec((bt, bk), lambda a, t: (t, a)), pl.BlockSpec((bt, nn), lambda a, t: (t, 0))],
        out_specs=pl.BlockSpec((bk, nn), lambda a, t: (a, 0)), out_shape=_sds(kk, nn, F32),
        compiler_params=_params(("arbitrary", "arbitrary")))(x, dy)


def _rope_tables(positions):
    inv_freq = ROPE_THETA ** (-jnp.arange(0, QK_ROPE, 2, dtype=F32) / QK_ROPE)
    ang = positions.astype(F32)[:, None] * inv_freq
    cos, sin = jnp.cos(ang), jnp.sin(ang)
    n_tok = positions.shape[0]
    ones, z64 = jnp.ones((n_tok, ROPE_LANE), F32), jnp.zeros((n_tok, ROPE_LANE), F32)
    z32 = jnp.zeros((n_tok, HEAD_SLOT - ROPE_LANE - QK_ROPE), F32)
    return jnp.concatenate([ones, cos, cos, z32], 1), jnp.concatenate([z64, -sin, sin, z32], 1)


def _kernel_weights(full):
    w_in, w_uq, w_ukv = full["w_in"], full["w_uq"], full["w_ukv"]
    c0 = Q_LORA + KV_LORA
    z = lambda n: jnp.zeros((D_MODEL, n), w_in.dtype)
    w = dict(full)
    w["w_in"] = jnp.concatenate([w_in[:, :c0], z(ROPE_LANE), w_in[:, c0:c0 + QK_ROPE], z(HEAD_SLOT - ROPE_LANE - QK_ROPE),
                                 w_in[:, c0 + QK_ROPE:]], 1)
    w["w_uq"] = jnp.pad(w_uq.reshape(Q_LORA, N_HEADS, QK_NOPE + QK_ROPE),
                        ((0, 0), (0, 0), (0, HEAD_SLOT - QK_NOPE - QK_ROPE))).reshape(Q_LORA, QK_WIDTH)
    kv = w_ukv.reshape(KV_LORA, N_HEADS, QK_NOPE + V_HEAD)
    w["w_k"] = jnp.pad(kv[:, :, :QK_NOPE], ((0, 0), (0, 0), (0, HEAD_SLOT - QK_NOPE))).reshape(KV_LORA, QK_WIDTH)
    w["w_v"] = kv[:, :, QK_NOPE:].reshape(KV_LORA, N_HEADS * V_HEAD)
    e = np.zeros((HEAD_SLOT, QK_WIDTH), np.float32)
    sel = np.zeros((N_HEADS * V_HEAD, HEAD_SLOT), np.float32)
    for h in range(N_HEADS):
        for r in range(QK_ROPE):
            e[ROPE_LANE + r, h * HEAD_SLOT + ROPE_LANE + r] = 1.0
        sel[h * V_HEAD:(h + 1) * V_HEAD, h] = 1.0
    w["e_mat"] = jnp.asarray(e, MX)
    w["head_sel"] = jnp.asarray(sel, MX)
    w["w_pool"] = full["w_pool"].astype(MX)
    return w


def _local_step(x, p, positions, target, full):
    n_tok = x.shape[0]
    tm = min(512, n_tok)
    tm_mlp = min(256, n_tok)
    tq = min(512, n_tok)
    w = _kernel_weights(full)
    cc, sa = _rope_tables(positions)

    a, ps, u, gl, qn, kvn, q, k, v = _fwd_inproj(x, cc, sa, w, tm)
    attn_t, lse = _attn_fwd(q, k, v.T, tq)
    attn = attn_t.T
    d, pooled, a_br, p_br, merged, y, h1 = _fwd_mix(x, u, gl, attn, w, tm)
    m, zr, a2, f, h2 = _fwd_mlp(h1, w, tm_mlp)
    dh2, de, dzg, loss_cols, dg_ple = _ple_fwd_bwd(h2, p, target, w, tm)
    df, dz, dh1, dg_post_mlp, dg_pre_mlp = _bwd_mlp(dh2, f, h1, zr, w, tm_mlp)
    dy, da_br, dp_br, dgl, do, delta, dyp, dd, dg_post_mix, db_gate, dpool_scale = _bwd_mix(dh1, y, a_br, p_br, gl, attn, d, w, tm)
    dq_t, dk, dv = _attn_bwd(q, k, k.T, v, do, lse, delta[:, :N_HEADS].T, tq)
    dq = dq_t.T.astype(MX)
    dqu, dproj, dx, dg_q, dg_kv, dg_pre_mix = _bwd_inproj(dq, dk, dv, dd, dgl, ps, x, dh1, cc, sa, w, tm)

    bt = min(512, n_tok)
    g_in = _xtdy("dw_in", a, dproj, 256, bt)
    g_uq = _xtdy("dw_uq", qn, dqu, Q_LORA, bt)
    g_k = _xtdy("dw_k", kvn, dk, KV_LORA, bt)
    g_v = _xtdy("dw_v", kvn, dv, KV_LORA, bt)
    g_pool = _xtdy("dw_pool", d, dyp, POOL_WIDTH, bt)
    g_ba = _xtdy("dw_ba", attn, da_br, 512, bt)
    g_bp = _xtdy("dw_bp", pooled, dp_br, 512, bt)
    g_out = _xtdy("dw_out", merged, dy, 512, bt)
    g_ff1 = _xtdy("dw_ff1", m, dz, 256, bt)
    g_ff2 = _xtdy("dw_ff2", a2, df, 1024, bt)
    g_pe = _xtdy("dw_pe", p, de, PLE_DIM, bt)
    g_pg = _xtdy("dw_pg", h2, dzg, 512, bt)

    c0 = Q_LORA + KV_LORA
    grads = {
        "g_pre_mix": dg_pre_mix,
        "w_in": jnp.concatenate([g_in[:, :c0], g_in[:, c0 + ROPE_LANE:c0 + ROPE_LANE + QK_ROPE], g_in[:, SMALL_COLS:]], 1),
        "b_gate": db_gate,
        "g_q": dg_q,
        "w_uq": g_uq.reshape(Q_LORA, N_HEADS, HEAD_SLOT)[:, :, :QK_NOPE + QK_ROPE].reshape(Q_LORA, N_HEADS * (QK_NOPE + QK_ROPE)),
        "g_kv": dg_kv,
        "w_ukv": jnp.concatenate([g_k.reshape(KV_LORA, N_HEADS, HEAD_SLOT)[:, :, :QK_NOPE],
                                  g_v.reshape(KV_LORA, N_HEADS, V_HEAD)], 2).reshape(KV_LORA, N_HEADS * (QK_NOPE + V_HEAD)),
        "w_pool": jnp.stack([g_pool[g * POOL_GROUP:(g + 1) * POOL_GROUP, g * POOL_GROUP:(g + 1) * POOL_GROUP]
                             for g in range(len(POOL_WINDOWS))]),
        "pool_scale": dpool_scale,
        "w_branch_attn": g_ba,
        "w_branch_pool": g_bp,
        "w_out": g_out,
        "g_post_mix": dg_post_mix,
        "g_pre_mlp": dg_pre_mlp,
        "w_ff1": g_ff1,
        "w_ff2": g_ff2,
        "g_post_mlp": dg_post_mlp,
        "w_ple_proj": g_pe,
        "w_ple_gate": g_pg,
        "g_ple": dg_ple,
    }
    return loss_cols, dx, grads


def _place():
    return lax.axis_index("x"), lax.axis_index("y"), lax.axis_index("c")


CHIP_FLIPS = ((1, 0), (0, 1), (1, 1))


def _flip(x, y, fx, fy):
    return (1 - x if fx else x), (1 - y if fy else y)


_HBM = pl.BlockSpec(memory_space=pl.ANY)


def _allgather_shards(wp):
    rows = wp.shape[0]
    half = rows // 2

    def body(w_ref, out_ref, send_sems, recv_sems):
        x, y, c = _place()
        my_chip = 2 * x + y
        sibling = (x, y, 1 - c)

        def half_of(chip, hc):
            return out_ref.at[chip, pl.ds(pl.multiple_of(hc * half, 16), half), :]

        src = w_ref.at[pl.ds(pl.multiple_of(c * half, 16), half), :]
        chips = [_flip(x, y, fx, fy) for fx, fy in CHIP_FLIPS]
        first = []
        for j, (px, py) in enumerate(chips):
            cp = pltpu.make_async_remote_copy(src, half_of(my_chip, c), send_sems.at[j], recv_sems.at[j],
                                              device_id=(px, py, c), device_id_type=MESH)
            cp.start()
            first.append(cp)
        passed = []
        for j, (px, py) in enumerate(chips):
            landed = half_of(2 * px + py, c)
            pltpu.make_async_remote_copy(src, landed, send_sems.at[j], recv_sems.at[j],
                                         device_id=(px, py, c), device_id_type=MESH).wait_recv()
            cp = pltpu.make_async_remote_copy(landed, landed, send_sems.at[3 + j], recv_sems.at[3 + j],
                                              device_id=sibling, device_id_type=MESH)
            cp.start()
            passed.append(cp)
        for j, (px, py) in enumerate(chips):
            theirs = half_of(2 * px + py, 1 - c)
            pltpu.make_async_remote_copy(theirs, theirs, send_sems.at[3 + j], recv_sems.at[3 + j],
                                         device_id=sibling, device_id_type=MESH).wait_recv()
        for cp in first + passed:
            cp.wait_send()

    return pl.pallas_call(
        body, name="allgather_shards", out_shape=jax.ShapeDtypeStruct((N_CHIPS, rows, PACK_COLS), wp.dtype),
        in_specs=[_HBM], out_specs=_HBM,
        scratch_shapes=[pltpu.SemaphoreType.DMA((6,)), pltpu.SemaphoreType.DMA((6,))],
    )(wp)


def _exchange_halves(g):
    rows = g.shape[1]
    half = rows // 2

    def body(g_ref, r_ref, send_sem, recv_sem):
        x, y, c = _place()
        src = g_ref.at[:, pl.ds(pl.multiple_of((1 - c) * half, 8), half), :]
        cp = pltpu.make_async_remote_copy(src, r_ref, send_sem, recv_sem, device_id=(x, y, 1 - c), device_id_type=MESH)
        cp.start()
        cp.wait()

    return pl.pallas_call(
        body, name="exchange_halves", out_shape=jax.ShapeDtypeStruct((N_CHIPS, half, PACK_COLS), g.dtype),
        in_specs=[_HBM], out_specs=_HBM, scratch_shapes=[pltpu.SemaphoreType.DMA, pltpu.SemaphoreType.DMA],
    )(g)


def _add_halves(g, r, c):
    rows = g.shape[1]
    half = rows // 2
    br = half // 8

    def kern(c_ref, g_ref, r_ref, o_ref):
        o_ref[...] = (g_ref[...] + r_ref[...]).astype(o_ref.dtype)

    gs = pltpu.PrefetchScalarGridSpec(
        num_scalar_prefetch=1, grid=(N_CHIPS, 8),
        in_specs=[pl.BlockSpec((1, br, PACK_COLS), lambda k, t, c: (k, c[0] * 8 + t, 0)),
                  pl.BlockSpec((1, br, PACK_COLS), lambda k, t, c: (k, t, 0))],
        out_specs=pl.BlockSpec((1, br, PACK_COLS), lambda k, t, c: (k, t, 0)))
    return pl.pallas_call(kern, name="add_halves", grid_spec=gs,
                          out_shape=jax.ShapeDtypeStruct((N_CHIPS, half, PACK_COLS), WIRE),
                          compiler_params=_params(("arbitrary", "arbitrary")))(c.reshape(1), g, r)


def _scatter_pieces(s):
    def body(s_ref, r_ref, send_sems, recv_sems):
        x, y, c = _place()
        my_chip = 2 * x + y
        chips = [_flip(x, y, fx, fy) for fx, fy in CHIP_FLIPS]
        sent = []
        for j, (px, py) in enumerate(chips):
            cp = pltpu.make_async_remote_copy(s_ref.at[2 * px + py], r_ref.at[my_chip], send_sems.at[j], recv_sems.at[j],
                                              device_id=(px, py, c), device_id_type=MESH)
            cp.start()
            sent.append(cp)
        for j, (px, py) in enumerate(chips):
            slot = r_ref.at[2 * px + py]
            pltpu.make_async_remote_copy(slot, slot, send_sems.at[j], recv_sems.at[j],
                                         device_id=(px, py, c), device_id_type=MESH).wait_recv()
        for cp in sent:
            cp.wait_send()

    return pl.pallas_call(
        body, name="scatter_pieces", out_shape=jax.ShapeDtypeStruct(s.shape, s.dtype), in_specs=[_HBM], out_specs=_HBM,
        scratch_shapes=[pltpu.SemaphoreType.DMA((3,)), pltpu.SemaphoreType.DMA((3,))],
    )(s)


def _sum_pieces(r):
    half = r.shape[1]
    br = half // 8

    def kern(r_ref, o_ref):
        o_ref[...] = ((r_ref[0].astype(F32) + r_ref[1].astype(F32)) + r_ref[2].astype(F32)) + r_ref[3].astype(F32)

    return pl.pallas_call(
        kern, name="sum_pieces", grid=(8,), in_specs=[pl.BlockSpec((N_CHIPS, br, PACK_COLS), lambda t: (0, t, 0))],
        out_specs=pl.BlockSpec((br, PACK_COLS), lambda t: (t, 0)), out_shape=_sds(half, PACK_COLS, F32),
        compiler_params=_params(("arbitrary",)))(r)


def _join_halves(f):
    def body(f_ref, o_ref, send_sem, recv_sem):
        x, y, c = _place()
        cp = pltpu.make_async_remote_copy(f_ref, o_ref, send_sem, recv_sem, device_id=(x, y, 1 - c), device_id_type=MESH)
        cp.start()
        cp.wait()

    return pl.pallas_call(
        body, name="join_halves", out_shape=jax.ShapeDtypeStruct(f.shape, f.dtype), in_specs=[_HBM], out_specs=_HBM,
        scratch_shapes=[pltpu.SemaphoreType.DMA, pltpu.SemaphoreType.DMA],
    )(f)


def _allreduce_small(g):
    n_dev = 8

    def body(g_ref, o_ref, buf, send_sems, recv_sems):
        x, y, c = _place()
        me = 4 * x + 2 * y + c
        buf[me] = g_ref[...]
        peers = []
        for f in range(1, n_dev):
            px, py = _flip(x, y, f & 4, f & 2)
            pc = 1 - c if f & 1 else c
            peers.append((px, py, pc))
        sent = []
        for f, peer in enumerate(peers):
            cp = pltpu.make_async_remote_copy(g_ref, buf.at[me], send_sems.at[f], recv_sems.at[f], device_id=peer,
                                              device_id_type=MESH)
            cp.start()
            sent.append(cp)
        for f, (px, py, pc) in enumerate(peers):
            slot = buf.at[4 * px + 2 * py + pc]
            pltpu.make_async_remote_copy(slot, slot, send_sems.at[f], recv_sems.at[f], device_id=(px, py, pc),
                                         device_id_type=MESH).wait_recv()
        for cp in sent:
            cp.wait_send()
        total = buf[0]
        for k in range(1, n_dev):
            total = total + buf[k]
        o_ref[...] = total

    vmem = pl.BlockSpec(memory_space=pltpu.VMEM)
    return pl.pallas_call(
        body, name="allreduce_small", out_shape=jax.ShapeDtypeStruct(g.shape, g.dtype), in_specs=[vmem], out_specs=vmem,
        scratch_shapes=[pltpu.VMEM((n_dev,) + g.shape, g.dtype), pltpu.SemaphoreType.DMA((n_dev - 1,)),
                        pltpu.SemaphoreType.DMA((n_dev - 1,))],
    )(g)


def _adamw_update(g_ref, w_ref, m_ref, v_ref, d_o, m_o, v_o):
    c1 = 1.0 - ADAM_B1 ** ADAM_STEP
    c2 = 1.0 - ADAM_B2 ** ADAM_STEP
    g_ = g_ref[...]
    m_new = ADAM_B1 * m_ref[...] + (1.0 - ADAM_B1) * g_
    v_new = ADAM_B2 * v_ref[...] + (1.0 - ADAM_B2) * (g_ * g_)
    m_o[...] = m_new
    v_o[...] = v_new
    d_o[...] = -ADAM_LR * ((m_new / c1) / (jnp.sqrt(v_new / c2) + ADAM_EPS) + ADAM_WD * w_ref[...])


ADAMW_ROWS = 256


def _adamw(name, g, w, m, v):
    _, rows, cols = w.shape
    br = int(np.gcd(ADAMW_ROWS, rows))

    def kern(*refs):
        _adamw_update(*refs)

    spec = pl.BlockSpec((1, br, cols), lambda t: (0, t, 0))
    out = jax.ShapeDtypeStruct(w.shape, F32)
    return pl.pallas_call(kern, name="adamw_" + name, grid=(rows // br,), in_specs=[spec] * 4, out_specs=[spec] * 3,
                          out_shape=[out, out, out], compiler_params=_params(("arbitrary",)))(g, w, m, v)


def _adamw_small(gs, ws, ms, vs):
    n = len(gs)

    def kern(*refs):
        ins, outs = refs[:4 * n], refs[4 * n:]
        for k in range(n):
            _adamw_update(ins[k], ins[n + k], ins[2 * n + k], ins[3 * n + k], outs[k], outs[n + k], outs[2 * n + k])

    vmem = pl.BlockSpec(memory_space=pltpu.VMEM)
    out = [jax.ShapeDtypeStruct(w.shape, F32) for w in ws]
    res = pl.pallas_call(kern, name="adamw_small", in_specs=[vmem] * (4 * n), out_specs=[vmem] * (3 * n),
                         out_shape=out * 3, compiler_params=pltpu.CompilerParams(vmem_limit_bytes=VMEM_LIMIT))(*gs, *ws, *ms, *vs)
    return [(res[k], res[n + k], res[2 * n + k]) for k in range(n)]


def _shard_rows(shape, axis):
    k, n = shape
    return (k * n // N_CHIPS) // PACK_COLS


def _pack_shards(shards, dtype):
    parts = [shards[name].astype(dtype).reshape(-1, PACK_COLS) for name, _, _ in SHARDED]
    used = sum(p.shape[0] for p in parts)
    parts.append(jnp.zeros((BIG_ROWS - used, PACK_COLS), dtype))
    return jnp.concatenate(parts, 0)


def _unpack_shards(packed):
    out, r0 = {}, 0
    for name, (k, n), axis in SHARDED:
        nr = _shard_rows((k, n), axis)
        shape = (k // N_CHIPS, n) if axis == 0 else (k, n // N_CHIPS)
        out[name] = packed[r0:r0 + nr].reshape(shape)
        r0 += nr
    return out


def _unpack_full(gathered):
    out, r0 = {}, 0
    for name, (k, n), axis in SHARDED:
        nr = _shard_rows((k, n), axis)
        part = gathered[:, r0:r0 + nr]
        if axis == 0:
            out[name] = part.reshape(k, n)
        else:
            out[name] = part.reshape(N_CHIPS, k, n // N_CHIPS).transpose(1, 0, 2).reshape(k, n)
        r0 += nr
    return out


def _pack_pieces(grads):
    parts = []
    for name, (k, n), axis in SHARDED:
        g = grads[name]
        if axis == 0:
            parts.append(g.reshape(N_CHIPS, -1, PACK_COLS))
        else:
            parts.append(g.reshape(k, N_CHIPS, n // N_CHIPS).transpose(1, 0, 2).reshape(N_CHIPS, -1, PACK_COLS))
    used = sum(p.shape[1] for p in parts)
    parts.append(jnp.zeros((N_CHIPS, BIG_ROWS - used, PACK_COLS), F32))
    return jnp.concatenate(parts, 1)


def _pack_small(vals):
    flat = jnp.concatenate([vals[name].astype(F32).reshape(-1) for name, _ in SMALL])
    flat = jnp.concatenate([flat, jnp.zeros((SMALL_ROWS * PACK_COLS - flat.shape[0],), F32)])
    return flat.reshape(SMALL_ROWS, PACK_COLS)


def _unpack_small(packed):
    flat, out, o = packed.reshape(-1), {}, 0
    for name, shape in SMALL:
        n = int(np.prod(shape))
        out[name] = flat[o:o + n].reshape(shape)
        o += n
    return out


def kernel(x, p, positions, g_pre_mix, w_in, b_gate, g_q, w_uq, g_kv, w_ukv, w_pool, pool_scale, w_branch_attn, w_branch_pool, w_out, g_post_mix, g_pre_mlp, w_ff1, w_ff2, g_post_mlp, w_ple_proj, w_ple_gate, g_ple, loss_target, m_g_pre_mix, m_w_in, m_b_gate, m_g_q, m_w_uq, m_g_kv, m_w_ukv, m_w_pool, m_pool_scale, m_w_branch_attn, m_w_branch_pool, m_w_out, m_g_post_mix, m_g_pre_mlp, m_w_ff1, m_w_ff2, m_g_post_mlp, m_w_ple_proj, m_w_ple_gate, m_g_ple, v_g_pre_mix, v_w_in, v_b_gate, v_g_q, v_w_uq, v_g_kv, v_w_ukv, v_w_pool, v_pool_scale, v_w_branch_attn, v_w_branch_pool, v_w_out, v_g_post_mix, v_g_pre_mlp, v_w_ff1, v_w_ff2, v_g_post_mlp, v_w_ple_proj, v_w_ple_gate, v_g_ple):
    given = dict(locals())
    weights = {n: given[n] for n in WEIGHT_ORDER}
    moments_m = {n: given["m_" + n] for n in WEIGHT_ORDER}
    moments_v = {n: given["v_" + n] for n in WEIGHT_ORDER}
    c = lax.axis_index("c")

    big_w = {name: weights[name][0] for name, _, _ in SHARDED}
    my_chip = 2 * lax.axis_index("x") + lax.axis_index("y")
    packed_w = _pack_shards(big_w, MX)
    full = _unpack_full(lax.dynamic_update_slice(_allgather_shards(packed_w), packed_w[None], (my_chip, 0, 0)))
    for name, _ in SMALL:
        full[name] = weights[name][0] if name == "w_pool" else weights[name]

    loss_cols, dx, grads = _local_step(x[0], p[0, 0], positions[0], loss_target[0], full)
    loss = lax.psum(0.5 * jnp.sum(loss_cols) / D_MODEL, ("x", "y", "c"))

    pieces = _pack_pieces(grads)
    summed = _add_halves(pieces, _exchange_halves(pieces), c)
    mine = lax.dynamic_slice(summed, (my_chip, 0, 0), (1,) + summed.shape[1:])
    reduced = _sum_pieces(lax.dynamic_update_slice(_scatter_pieces(summed), mine, (my_chip, 0, 0)))
    theirs = _join_halves(reduced)
    g_big = jnp.where(c == 0, jnp.concatenate([reduced, theirs]), jnp.concatenate([theirs, reduced]))
    g_small = _allreduce_small(_pack_small(grads))

    out = {}
    for name, g in _unpack_shards(g_big).items():
        out[name] = (g[None], *_adamw(name, g[None], weights[name], moments_m[name], moments_v[name]))
    small_g = _unpack_small(g_small)
    names = [n for n, _ in SMALL]
    updates = _adamw_small([small_g[n] for n in names], [weights[n] for n in names], [moments_m[n] for n in names],
                           [moments_v[n] for n in names])
    for n, upd in zip(names, updates):
        out[n] = (small_g[n], *upd)
    return (loss, dx[None], *[out[n][k] for k in range(4) for n in WEIGHT_ORDER])
```

```python
import functools

import numpy as np
import jax
import jax.numpy as jnp
from jax import lax
from jax.experimental import pallas as pl
from jax.experimental.pallas import tpu as pltpu

F32 = jnp.float32
MX = jnp.bfloat16
WIRE = jnp.bfloat16

D_MODEL = 1024
N_HEADS = 8
QK_NOPE = 64
QK_ROPE = 32
V_HEAD = 64
Q_LORA = 384
KV_LORA = 256
POOL_WINDOWS = (2, 4, 8, 16)
POOL_GROUP = 128
POOL_WIDTH = 512
D_FF = 4096
PLE_DIM = 256
ROPE_THETA = 10000.0
EPS = 1e-6
HEAD_SLOT = 128
QK_WIDTH = N_HEADS * HEAD_SLOT
ROPE_LANE = 64
SMALL_COLS = Q_LORA + KV_LORA + HEAD_SLOT
IN_PAD = SMALL_COLS + POOL_WIDTH + 2 * D_MODEL
SCALE = (QK_NOPE + QK_ROPE) ** -0.5
LOG2E = 1.4426950408889634
NEG = -1e30
HALO = 16

ADAM_LR = 0.001
ADAM_B1 = 0.9
ADAM_B2 = 0.999
ADAM_EPS = 1e-08
ADAM_WD = 0.01
ADAM_STEP = 10

VMEM_LIMIT = 56 * 2**20
MESH = pl.DeviceIdType.MESH

SHARDED = (
    ("w_in", (1024, 3232), 1),
    ("w_uq", (384, 768), 1),
    ("w_ukv", (256, 1024), 1),
    ("w_branch_attn", (512, 1024), 1),
    ("w_branch_pool", (512, 1024), 1),
    ("w_out", (1024, 1024), 0),
    ("w_ff1", (1024, 4096), 1),
    ("w_ff2", (4096, 1024), 0),
    ("w_ple_proj", (256, 1024), 1),
    ("w_ple_gate", (1024, 1024), 0),
)
SMALL = (
    ("g_pre_mix", (1, 1024)),
    ("b_gate", (1, 2048)),
    ("g_q", (1, 384)),
    ("g_kv", (1, 256)),
    ("w_pool", (1, 4, 128, 128)),
    ("pool_scale", (1, 512)),
    ("g_post_mix", (1, 1024)),
    ("g_pre_mlp", (1, 1024)),
    ("g_post_mlp", (1, 1024)),
    ("g_ple", (1, 1024)),
)
WEIGHT_ORDER = ("g_pre_mix", "w_in", "b_gate", "g_q", "w_uq", "g_kv", "w_ukv", "w_pool", "pool_scale", "w_branch_attn",
                "w_branch_pool", "w_out", "g_post_mix", "g_pre_mlp", "w_ff1", "w_ff2", "g_post_mlp", "w_ple_proj",
                "w_ple_gate", "g_ple")
N_CHIPS = 4
PACK_COLS = 1024
BIG_ROWS = 3840
SMALL_ROWS = 80


def _dot(a, b):
    return jnp.dot(a.astype(MX), b.astype(MX), preferred_element_type=F32)


def _dot_nt(a, b):
    return lax.dot_general(a.astype(MX), b.astype(MX), (((1,), (1,)), ((), ())), preferred_element_type=F32)


def _dot_tn(a, b):
    return lax.dot_general(a.astype(MX), b.astype(MX), (((0,), (0,)), ((), ())), preferred_element_type=F32)


def _sig(x):
    return 1.0 / (1.0 + jnp.exp(-x))


def _rms(x, g):
    r = lax.rsqrt(jnp.mean(x * x, axis=1, keepdims=True) + EPS)
    xh = x * r
    return xh * g, xh, r


def _rms_bwd(xh, r, g, dy):
    dxn = dy * g
    dx = r * (dxn - xh * jnp.mean(dxn * xh, axis=1, keepdims=True))
    return dx, jnp.sum(dy * xh, axis=0, keepdims=True)


def _rot_half(v):
    lane = lax.broadcasted_iota(jnp.int32, v.shape, 1)
    return jnp.where(lane < ROPE_LANE + QK_ROPE // 2, pltpu.roll(v, HEAD_SLOT - QK_ROPE // 2, 1), pltpu.roll(v, QK_ROPE // 2, 1))


def _rope(v, cc, sa):
    return v * cc + _rot_half(v) * sa


def _unrope(v, cc, sa):
    return v * cc - _rot_half(v) * sa


def _params(sem):
    return pltpu.CompilerParams(dimension_semantics=sem, vmem_limit_bytes=VMEM_LIMIT)


def _tok_call(name, body, n_tok, tm, tiled, resident, outs, accs=(), scratch=()):
    def as_pair(t):
        if isinstance(t, tuple):
            return t
        return t, pl.BlockSpec((tm, t.shape[1]), lambda i: (i, 0))
    tiled = [as_pair(t) for t in tiled]
    res_specs = [pl.BlockSpec(r.shape, lambda i, nd=r.ndim: (0,) * nd, pipeline_mode=pl.Buffered(1)) for r in resident]
    out_specs = [pl.BlockSpec((tm, o.shape[1]), lambda i: (i, 0)) for o in outs]
    out_specs += [pl.BlockSpec(a.shape, lambda i: (0, 0)) for a in accs]
    n_t, n_r, n_o, n_a = len(tiled), len(resident), len(outs), len(accs)

    def kern(*refs):
        tin, res = refs[:n_t], refs[n_t:n_t + n_r]
        tout = refs[n_t + n_r:n_t + n_r + n_o]
        acc = refs[n_t + n_r + n_o:n_t + n_r + n_o + n_a]
        scr = refs[n_t + n_r + n_o + n_a:]
        i = pl.program_id(0)

        @pl.when(i == 0)
        def _():
            for a in acc:
                a[...] = jnp.zeros(a.shape, a.dtype)
        body(i, tin, res, tout, acc, scr)

    return pl.pallas_call(
        kern, name=name, grid=(n_tok // tm,), in_specs=[s for _, s in tiled] + res_specs, out_specs=out_specs,
        out_shape=list(outs) + list(accs), scratch_shapes=list(scratch), compiler_params=_params(("arbitrary",)),
    )(*[a for a, _ in tiled], *resident)


def _sds(rows, cols, dtype):
    return jax.ShapeDtypeStruct((rows, cols), dtype)


def _fwd_inproj(x, cc, sa, w, tm):
    n_tok = x.shape[0]

    def body(i, tin, res, tout, acc, scr):
        x_ref, c_ref, s_ref = tin
        g_pre, w_in, g_q, w_uq, g_kv, w_k, w_v, e_mat = res
        a_o, ps_o, u_o, gl_o, qn_o, kvn_o, q_o, k_o, v_o = tout
        a = _rms(x_ref[...], g_pre[...])[0].astype(MX)
        a_o[...] = a
        ps = _dot(a, w_in[:, :SMALL_COLS])
        ps_o[...] = ps.astype(ps_o.dtype)
        u_o[...] = _dot(a, w_in[:, SMALL_COLS:SMALL_COLS + POOL_WIDTH]).astype(u_o.dtype)
        gl_o[...] = _dot(a, w_in[:, SMALL_COLS + POOL_WIDTH:]).astype(gl_o.dtype)
        cc_, sa_ = c_ref[...], s_ref[...]
        qn = _rms(ps[:, :Q_LORA], g_q[...])[0].astype(MX)
        qn_o[...] = qn
        q = _dot(qn, w_uq[...])
        for h in range(N_HEADS):
            hs = slice(h * HEAD_SLOT, (h + 1) * HEAD_SLOT)
            q_o[:, hs] = (_rope(q[:, hs], cc_, sa_) * (SCALE * LOG2E)).astype(q_o.dtype)
        kvn = _rms(ps[:, Q_LORA:Q_LORA + KV_LORA], g_kv[...])[0].astype(MX)
        kvn_o[...] = kvn
        kr = _rope(ps[:, Q_LORA + KV_LORA:], cc_, sa_)
        k_o[...] = (_dot(kvn, w_k[...]) + _dot(kr, e_mat[...])).astype(k_o.dtype)
        v_o[...] = _dot(kvn, w_v[...]).astype(v_o.dtype)

    outs = [_sds(n_tok, D_MODEL, MX), _sds(n_tok, SMALL_COLS, MX), _sds(n_tok, POOL_WIDTH, MX), _sds(n_tok, 2 * D_MODEL, MX),
            _sds(n_tok, Q_LORA, MX), _sds(n_tok, KV_LORA, MX), _sds(n_tok, QK_WIDTH, MX), _sds(n_tok, QK_WIDTH, MX),
            _sds(n_tok, N_HEADS * V_HEAD, MX)]
    res = [w["g_pre_mix"], w["w_in"], w["g_q"], w["w_uq"], w["g_kv"], w["w_k"], w["w_v"], w["e_mat"]]
    return _tok_call("fwd_inproj", body, n_tok, tm, [x, cc, sa], res, outs)


def _causal_pairs(nq, ratio, by_kv):
    if by_kv:
        pairs = [(i, j) for j in range(nq * ratio) for i in range(j // ratio, nq)]
    else:
        pairs = [(i, j) for i in range(nq) for j in range((i + 1) * ratio)]
    return (jnp.asarray(np.array([p[0] for p in pairs], np.int32)), jnp.asarray(np.array([p[1] for p in pairs], np.int32)))


def _keep_t(tk, tq, off):
    return lax.broadcasted_iota(jnp.int32, (tk, tq), 0) + off <= lax.broadcasted_iota(jnp.int32, (tk, tq), 1)


ATTN_KEYS = 512
ATTN_QUERIES_FWD = 1024
ATTN_QUERIES_BWD = 1024
V_ROWS = 80


def _attn_fwd(q, k, vt, tq, tk):
    n_tok = q.shape[0]
    nq, ratio = n_tok // tq, tq // tk
    qi, kj = _causal_pairs(nq, ratio, by_kv=False)

    def kern(qi_ref, kj_ref, q_ref, k_ref, vt_ref, ot_ref, lse_ref, m_s, acc_s, st_s):
        s_id = pl.program_id(0)
        i, j = qi_ref[s_id], kj_ref[s_id]

        @pl.when(j == 0)
        def _():
            m_s[...] = jnp.full(m_s.shape, NEG, F32)
            acc_s[...] = jnp.zeros(acc_s.shape, F32)

        def scores(h):
            hs = slice(h * HEAD_SLOT, (h + 1) * HEAD_SLOT)
            return _dot_nt(k_ref[:, hs], q_ref[:, hs])

        def heads(masked):
            keep = _keep_t(tk, tq, j * tk - i * tq) if masked else None
            st_s[0] = scores(0)
            for h in range(N_HEADS):
                if h + 1 < N_HEADS:
                    st_s[(h + 1) % 2] = scores(h + 1)
                st = st_s[h % 2]
                if masked:
                    st = jnp.where(keep, st, NEG)
                m_old = m_s[h]
                m_new = jnp.maximum(m_old, jnp.max(st, axis=0, keepdims=True))
                pt = jnp.exp2(st - m_new)
                acc_s[h] = jnp.exp2(m_old - m_new) * acc_s[h] + _dot(vt_ref[h * V_ROWS:(h + 1) * V_ROWS, :], pt)
                m_s[h] = m_new

        @pl.when(j < i * ratio)
        def _():
            heads(False)

        @pl.when(j >= i * ratio)
        def _():
            heads(True)

        @pl.when(j == (i + 1) * ratio - 1)
        def _():
            for h in range(N_HEADS):
                total = acc_s[h, V_HEAD:V_HEAD + 1, :]
                ot_ref[h * V_HEAD:(h + 1) * V_HEAD, :] = (acc_s[h, :V_HEAD, :] / total).astype(ot_ref.dtype)
                lse_ref[h:h + 1, :] = m_s[h] + jnp.log2(total)

    gs = pltpu.PrefetchScalarGridSpec(
        num_scalar_prefetch=2, grid=(qi.shape[0],),
        in_specs=[pl.BlockSpec((tq, QK_WIDTH), lambda s, qi, kj: (qi[s], 0)),
                  pl.BlockSpec((tk, QK_WIDTH), lambda s, qi, kj: (kj[s], 0)),
                  pl.BlockSpec((N_HEADS * V_ROWS, tk), lambda s, qi, kj: (0, kj[s]))],
        out_specs=[pl.BlockSpec((N_HEADS * V_HEAD, tq), lambda s, qi, kj: (0, qi[s])),
                   pl.BlockSpec((N_HEADS, tq), lambda s, qi, kj: (0, qi[s]))],
        scratch_shapes=[pltpu.VMEM((N_HEADS, 1, tq), F32), pltpu.VMEM((N_HEADS, V_ROWS, tq), F32),
                        pltpu.VMEM((2, tk, tq), F32)])
    return pl.pallas_call(kern, name="attn_fwd", grid_spec=gs,
                          out_shape=[_sds(N_HEADS * V_HEAD, n_tok, MX), _sds(N_HEADS, n_tok, F32)],
                          compiler_params=_params(("arbitrary",)))(qi, kj, q, k, vt)


def _pool_windows(ext, i, tm, first_row):
    row = i * tm + lax.broadcasted_iota(jnp.int32, (tm, 1), 0)
    out = []
    for g, w in enumerate(POOL_WINDOWS):
        cs = slice(g * POOL_GROUP, (g + 1) * POOL_GROUP)
        s = ext[pl.ds(first_row, tm), cs]
        for k in range(1, w):
            s = s + ext[pl.ds(first_row - k, tm), cs]
        cnt = jnp.minimum(row + 1, w).astype(F32)
        out.append(s / cnt)
    return out


def _fwd_mix(x, u, gl, attn, w, tm):
    n_tok = x.shape[0]
    halo_spec = pl.BlockSpec((HALO, POOL_WIDTH), lambda i: (jnp.maximum(i * (tm // HALO) - 1, 0), 0))

    def body(i, tin, res, tout, acc, scr):
        x_ref, u_ref, uh_ref, gl_ref, at_ref = tin
        w_pool, pool_scale, w_ba, w_bp, b_gate, w_out, g_post = res
        d_o, pooled_o, a_o, pp_o, merged_o, y_o, h1_o = tout
        ext, = scr
        ext[pl.ds(0, HALO), :] = jnp.where(i > 0, uh_ref[...].astype(F32), 0.0)
        ext[pl.ds(HALO, tm), :] = u_ref[...].astype(F32)
        means = _pool_windows(ext, i, tm, HALO)
        for g in range(len(POOL_WINDOWS)):
            cs = slice(g * POOL_GROUP, (g + 1) * POOL_GROUP)
            d = (means[g] - ext[pl.ds(HALO, tm), cs]).astype(MX)
            d_o[:, cs] = d
            pooled_o[:, cs] = (_dot(d, w_pool[g]) * pool_scale[:, cs]).astype(pooled_o.dtype)
        a_br = _dot(at_ref[...], w_ba[...])
        p_br = _dot(pooled_o[...], w_bp[...])
        a_o[...] = a_br.astype(a_o.dtype)
        pp_o[...] = p_br.astype(pp_o.dtype)
        gates = _sig(gl_ref[...].astype(F32) + b_gate[...])
        merged = (gates[:, :D_MODEL] * a_br + gates[:, D_MODEL:] * p_br).astype(MX)
        merged_o[...] = merged
        y = _dot(merged, w_out[...])
        y_o[...] = y.astype(y_o.dtype)
        h1_o[...] = x_ref[...] + _rms(y, g_post[...])[0]

    outs = [_sds(n_tok, POOL_WIDTH, MX), _sds(n_tok, POOL_WIDTH, MX), _sds(n_tok, D_MODEL, MX), _sds(n_tok, D_MODEL, MX),
            _sds(n_tok, D_MODEL, MX), _sds(n_tok, D_MODEL, MX), _sds(n_tok, D_MODEL, F32)]
    res = [w["w_pool"], w["pool_scale"], w["w_branch_attn"], w["w_branch_pool"], w["b_gate"], w["w_out"], w["g_post_mix"]]
    return _tok_call("fwd_mix", body, n_tok, tm, [x, u, (u, halo_spec), gl, attn], res, outs,
                     scratch=[pltpu.VMEM((tm + HALO, POOL_WIDTH), F32)])


def _fwd_mlp(h1, w, tm):
    n_tok = h1.shape[0]

    def body(i, tin, res, tout, acc, scr):
        h1_ref, = tin
        g_pre, w1, w2, g_post = res
        m_o, zr_o, a2_o, f_o, h2_o = tout
        h1_ = h1_ref[...]
        m = _rms(h1_, g_pre[...])[0].astype(MX)
        m_o[...] = m
        zr = jnp.maximum(_dot(m, w1[...]), 0.0)
        zr_o[...] = zr.astype(zr_o.dtype)
        a2 = (zr * zr).astype(MX)
        a2_o[...] = a2
        f = _dot(a2, w2[...])
        f_o[...] = f.astype(f_o.dtype)
        h2_o[...] = h1_ + _rms(f, g_post[...])[0]

    outs = [_sds(n_tok, D_MODEL, MX), _sds(n_tok, D_FF, MX), _sds(n_tok, D_FF, MX), _sds(n_tok, D_MODEL, MX),
            _sds(n_tok, D_MODEL, F32)]
    res = [w["g_pre_mlp"], w["w_ff1"], w["w_ff2"], w["g_post_mlp"]]
    return _tok_call("fwd_mlp", body, n_tok, tm, [h1], res, outs)


def _ple_fwd_bwd(h2, p, target, w, tm):
    n_tok = h2.shape[0]

    def body(i, tin, res, tout, acc, scr):
        h2_ref, p_ref, t_ref = tin
        w_pe, w_pg, g_ple = res
        dh2_o, de_o, dzg_o = tout
        loss_a, dg_a = acc
        h2_ = h2_ref[...]
        e = _dot(p_ref[...], w_pe[...])
        pg = _sig(_dot(h2_, w_pg[...]))
        t = pg * e
        g = g_ple[...]
        tn, th, r = _rms(t, g)
        diff = h2_ + tn - t_ref[...]
        loss_a[...] += jnp.sum(diff * diff, axis=0, keepdims=True)
        dh3 = diff * (1.0 / D_MODEL)
        dt, dg = _rms_bwd(th, r, g, dh3)
        dg_a[...] += dg
        de_o[...] = (dt * pg).astype(de_o.dtype)
        dzg = (dt * e * pg * (1.0 - pg)).astype(MX)
        dzg_o[...] = dzg
        dh2_o[...] = dh3 + _dot_nt(dzg, w_pg[...])

    outs = [_sds(n_tok, D_MODEL, F32), _sds(n_tok, D_MODEL, MX), _sds(n_tok, D_MODEL, MX)]
    accs = [_sds(1, D_MODEL, F32), _sds(1, D_MODEL, F32)]
    return _tok_call("ple_fwd_bwd", body, n_tok, tm, [h2, p, target], [w["w_ple_proj"], w["w_ple_gate"], w["g_ple"]], outs, accs)


def _bwd_mlp(dh2, f, h1, zr, w, tm):
    n_tok = dh2.shape[0]

    def body(i, tin, res, tout, acc, scr):
        dh2_ref, f_ref, h1_ref, zr_ref = tin
        g_pre, w1, w2, g_post = res
        df_o, dz_o, dh1_o = tout
        dg_post_a, dg_pre_a = acc
        dh2_ = dh2_ref[...]
        gp = g_post[...]
        _, fh, rf = _rms(f_ref[...].astype(F32), gp)
        df, dg = _rms_bwd(fh, rf, gp, dh2_)
        dg_post_a[...] += dg
        df = df.astype(MX)
        df_o[...] = df
        dz = (_dot_nt(df, w2[...]) * (2.0 * zr_ref[...].astype(F32))).astype(MX)
        dz_o[...] = dz
        dm = _dot_nt(dz, w1[...])
        gq = g_pre[...]
        _, hh, rh = _rms(h1_ref[...], gq)
        dh1, dg = _rms_bwd(hh, rh, gq, dm)
        dg_pre_a[...] += dg
        dh1_o[...] = dh2_ + dh1

    outs = [_sds(n_tok, D_MODEL, MX), _sds(n_tok, D_FF, MX), _sds(n_tok, D_MODEL, F32)]
    accs = [_sds(1, D_MODEL, F32), _sds(1, D_MODEL, F32)]
    res = [w["g_pre_mlp"], w["w_ff1"], w["w_ff2"], w["g_post_mlp"]]
    return _tok_call("bwd_mlp", body, n_tok, tm, [dh2, f, h1, zr], res, outs, accs)


def _bwd_mix(dh1, y, a_br, p_br, gl, attn, d, w, tm):
    n_tok = dh1.shape[0]

    def body(i, tin, res, tout, acc, scr):
        dh1_ref, y_ref, a_ref, pp_ref, gl_ref, at_ref, d_ref = tin
        g_post, w_out, b_gate, w_ba, w_bp, w_pool, pool_scale, sel = res
        dy_o, da_o, dpp_o, dgl_o, do_o, delta_o, dyp_o, dd_o = tout
        dg_post_a, db_a, dps_a = acc
        g = g_post[...]
        _, yh, r = _rms(y_ref[...].astype(F32), g)
        dy, dg = _rms_bwd(yh, r, g, dh1_ref[...])
        dg_post_a[...] += dg
        dy = dy.astype(MX)
        dy_o[...] = dy
        dmerged = _dot_nt(dy, w_out[...])
        gates = _sig(gl_ref[...].astype(F32) + b_gate[...])
        ga, gp = gates[:, :D_MODEL], gates[:, D_MODEL:]
        da = (dmerged * ga).astype(MX)
        dpp = (dmerged * gp).astype(MX)
        da_o[...] = da
        dpp_o[...] = dpp
        dgl_a = dmerged * a_ref[...].astype(F32) * ga * (1.0 - ga)
        dgl_p = dmerged * pp_ref[...].astype(F32) * gp * (1.0 - gp)
        dgl_o[:, :D_MODEL] = dgl_a.astype(dgl_o.dtype)
        dgl_o[:, D_MODEL:] = dgl_p.astype(dgl_o.dtype)
        db_a[:, :D_MODEL] += jnp.sum(dgl_a, axis=0, keepdims=True)
        db_a[:, D_MODEL:] += jnp.sum(dgl_p, axis=0, keepdims=True)
        do = _dot_nt(da, w_ba[...]).astype(MX)
        do_o[...] = do
        prod = do.astype(F32) * at_ref[...].astype(F32)
        hi = prod.astype(MX)
        lo = (prod - hi.astype(F32)).astype(MX)
        delta_o[...] = _dot(hi, sel[...]) + _dot(lo, sel[...])
        dpooled = _dot_nt(dpp, w_bp[...])
        for gi in range(len(POOL_WINDOWS)):
            cs = slice(gi * POOL_GROUP, (gi + 1) * POOL_GROUP)
            ypre = _dot(d_ref[:, cs], w_pool[gi])
            dps_a[:, cs] += jnp.sum(dpooled[:, cs] * ypre, axis=0, keepdims=True)
            dyp = (dpooled[:, cs] * pool_scale[:, cs]).astype(MX)
            dyp_o[:, cs] = dyp
            dd_o[:, cs] = _dot_nt(dyp, w_pool[gi])

    outs = [_sds(n_tok, D_MODEL, MX), _sds(n_tok, D_MODEL, MX), _sds(n_tok, D_MODEL, MX), _sds(n_tok, 2 * D_MODEL, MX),
            _sds(n_tok, N_HEADS * V_HEAD, MX), _sds(n_tok, HEAD_SLOT, F32), _sds(n_tok, POOL_WIDTH, MX),
            _sds(n_tok, POOL_WIDTH, F32)]
    accs = [_sds(1, D_MODEL, F32), _sds(1, 2 * D_MODEL, F32), _sds(1, POOL_WIDTH, F32)]
    res = [w["g_post_mix"], w["w_out"], w["b_gate"], w["w_branch_attn"], w["w_branch_pool"], w["w_pool"], w["pool_scale"],
           w["head_sel"]]
    return _tok_call("bwd_mix", body, n_tok, tm, [dh1, y, a_br, p_br, gl, attn, d], res, outs, accs)


def _bwd_heads(q_ref, k_ref, v_ref, do_ref, lse_ref, dl_ref, st_s, dpt_s, keep, use, n_heads):
    def products(h):
        hs = slice(h * HEAD_SLOT, (h + 1) * HEAD_SLOT)
        vs = slice(h * V_HEAD, (h + 1) * V_HEAD)
        st_s[h % 2] = _dot_nt(k_ref[:, hs], q_ref[:, hs])
        dpt_s[h % 2] = _dot_nt(v_ref[:, vs], do_ref[:, vs])

    products(0)
    for h in range(n_heads):
        if h + 1 < n_heads:
            products(h + 1)
        st = st_s[h % 2]
        if keep is not None:
            st = jnp.where(keep, st, NEG)
        pt = jnp.exp2(st - lse_ref[h:h + 1, :])
        use(h, pt, pt * (dpt_s[h % 2] - dl_ref[h:h + 1, :]))


HEAD_GROUP = 4


def _attn_bwd(q, k, kt, v, do, lse, delta, tq, tk):
    n_tok = q.shape[0]
    nq, ratio = n_tok // tq, tq // tk
    n_groups = N_HEADS // HEAD_GROUP
    gq, gv = HEAD_GROUP * HEAD_SLOT, HEAD_GROUP * V_HEAD
    qi, kj = _causal_pairs(nq, ratio, by_kv=True)

    def kern(qi_ref, kj_ref, q_ref, k_ref, kt_ref, v_ref, do_ref, lse_ref, dl_ref, dq_ref, dk_ref, dv_ref,
             dk_s, dv_s, st_s, dpt_s):
        s_id = pl.program_id(1)
        i, j = qi_ref[s_id], kj_ref[s_id]
        cols = pl.ds(pl.multiple_of(i * tq, tq), tq)

        @pl.when(s_id == 0)
        def _():
            dq_ref[...] = jnp.zeros(dq_ref.shape, F32)

        def use(h, pt, dst):
            hs = slice(h * HEAD_SLOT, (h + 1) * HEAD_SLOT)
            dv_s[h] += _dot(pt, do_ref[:, h * V_HEAD:(h + 1) * V_HEAD])
            dk_s[:, hs] += _dot(dst, q_ref[:, hs])
            dq_ref[hs, cols] += _dot(kt_ref[hs, :], dst)

        def heads(masked):
            keep = _keep_t(tk, tq, j * tk - i * tq) if masked else None
            _bwd_heads(q_ref, k_ref, v_ref, do_ref, lse_ref.at[0], dl_ref.at[0], st_s, dpt_s, keep, use, HEAD_GROUP)

        @pl.when(j >= i * ratio)
        def _():
            dk_s[...] = jnp.zeros(dk_s.shape, F32)
            dv_s[...] = jnp.zeros(dv_s.shape, F32)
            heads(True)

        @pl.when(j < i * ratio)
        def _():
            heads(False)

        @pl.when(i == nq - 1)
        def _():
            dk_ref[...] = (dk_s[...] * (1.0 / LOG2E)).astype(dk_ref.dtype)
            for h in range(HEAD_GROUP):
                dv_ref[:, h * V_HEAD:(h + 1) * V_HEAD] = dv_s[h].astype(dv_ref.dtype)

    at_q = lambda g, s, qi, kj: (qi[s], g)
    at_k = lambda g, s, qi, kj: (kj[s], g)
    at_kt = lambda g, s, qi, kj: (g, kj[s])
    at_stat = lambda g, s, qi, kj: (g, 0, qi[s])
    gs = pltpu.PrefetchScalarGridSpec(
        num_scalar_prefetch=2, grid=(n_groups, qi.shape[0]),
        in_specs=[pl.BlockSpec((tq, gq), at_q), pl.BlockSpec((tk, gq), at_k), pl.BlockSpec((gq, tk), at_kt),
                  pl.BlockSpec((tk, gv), at_k), pl.BlockSpec((tq, gv), at_q),
                  pl.BlockSpec((1, HEAD_GROUP, tq), at_stat), pl.BlockSpec((1, HEAD_GROUP, tq), at_stat)],
        out_specs=[pl.BlockSpec((gq, n_tok), lambda g, s, qi, kj: (g, 0), pipeline_mode=pl.Buffered(1)),
                   pl.BlockSpec((tk, gq), at_k), pl.BlockSpec((tk, gv), at_k)],
        scratch_shapes=[pltpu.VMEM((tk, gq), F32), pltpu.VMEM((HEAD_GROUP, tk, V_HEAD), F32),
                        pltpu.VMEM((2, tk, tq), F32), pltpu.VMEM((2, tk, tq), F32)])
    stat3 = lambda a: a.reshape(n_groups, HEAD_GROUP, n_tok)
    return pl.pallas_call(kern, name="attn_bwd", grid_spec=gs,
                          out_shape=[_sds(QK_WIDTH, n_tok, F32), _sds(n_tok, QK_WIDTH, MX), _sds(n_tok, N_HEADS * V_HEAD, MX)],
                          compiler_params=_params(("arbitrary", "arbitrary")))(qi, kj, q, k, kt, v, do, stat3(lse), stat3(delta))


def _bwd_inproj(dq, dk, dv, dd, dgl, ps, x, dh1, cc, sa, w, tm):
    n_tok = x.shape[0]
    n_tiles = n_tok // tm
    last_halo = n_tok // HALO - 1
    halo_spec = pl.BlockSpec((HALO, POOL_WIDTH), lambda i: (jnp.minimum((i + 1) * (tm // HALO), last_halo), 0))

    def body(i, tin, res, tout, acc, scr):
        dq_ref, dk_ref, dv_ref, dd_ref, ddh_ref, dgl_ref, ps_ref, x_ref, dh1_ref, c_ref, s_ref = tin
        w_uq, g_q, w_k, w_v, e_mat, g_kv, w_in, g_pre = res
        dqu_o, dproj_o, dx_o = tout
        dgq_a, dgkv_a, dgpre_a = acc
        ext, = scr
        cc_, sa_ = c_ref[...], s_ref[...]
        for h in range(N_HEADS):
            hs = slice(h * HEAD_SLOT, (h + 1) * HEAD_SLOT)
            dqu_o[:, hs] = (_unrope(dq_ref[:, hs].astype(F32), cc_, sa_) * SCALE).astype(dqu_o.dtype)
        gq = g_q[...]
        _, qh, rq = _rms(ps_ref[:, :Q_LORA].astype(F32), gq)
        dqd, dg = _rms_bwd(qh, rq, gq, _dot_nt(dqu_o[...], w_uq[...]))
        dgq_a[...] += dg
        dproj_o[:, :Q_LORA] = dqd.astype(dproj_o.dtype)
        gkv = g_kv[...]
        _, kh, rk = _rms(ps_ref[:, Q_LORA:Q_LORA + KV_LORA].astype(F32), gkv)
        dkvd, dg = _rms_bwd(kh, rk, gkv, _dot_nt(dk_ref[...], w_k[...]) + _dot_nt(dv_ref[...], w_v[...]))
        dgkv_a[...] += dg
        dproj_o[:, Q_LORA:Q_LORA + KV_LORA] = dkvd.astype(dproj_o.dtype)
        dproj_o[:, Q_LORA + KV_LORA:SMALL_COLS] = _unrope(_dot_nt(dk_ref[...], e_mat[...]), cc_, sa_).astype(dproj_o.dtype)
        row = i * tm + lax.broadcasted_iota(jnp.int32, (tm + HALO, 1), 0)
        for gi, wdw in enumerate(POOL_WINDOWS):
            cs = slice(gi * POOL_GROUP, (gi + 1) * POOL_GROUP)
            inv = 1.0 / jnp.minimum(row + 1, wdw).astype(F32)
            ext[pl.ds(0, tm), cs] = dd_ref[:, cs] * inv[:tm]
            ext[pl.ds(tm, HALO), cs] = jnp.where(i < n_tiles - 1, ddh_ref[:, cs] * inv[tm:], 0.0)
            s = ext[pl.ds(0, tm), cs]
            for k_ in range(1, wdw):
                s = s + ext[pl.ds(k_, tm), cs]
            dproj_o[:, SMALL_COLS + gi * POOL_GROUP:SMALL_COLS + (gi + 1) * POOL_GROUP] = (s - dd_ref[:, cs]).astype(dproj_o.dtype)
        dproj_o[:, SMALL_COLS + POOL_WIDTH:] = dgl_ref[...]
        da = _dot_nt(dproj_o[...], w_in[...])
        gp = g_pre[...]
        _, xh, rx = _rms(x_ref[...], gp)
        dx, dg = _rms_bwd(xh, rx, gp, da)
        dgpre_a[...] += dg
        dx_o[...] = dh1_ref[...] + dx

    outs = [_sds(n_tok, QK_WIDTH, MX), _sds(n_tok, IN_PAD, MX), _sds(n_tok, D_MODEL, F32)]
    accs = [_sds(1, Q_LORA, F32), _sds(1, KV_LORA, F32), _sds(1, D_MODEL, F32)]
    res = [w["w_uq"], w["g_q"], w["w_k"], w["w_v"], w["e_mat"], w["g_kv"], w["w_in"], w["g_pre_mix"]]
    return _tok_call("bwd_inproj", body, n_tok, tm, [dq, dk, dv, dd, (dd, halo_spec), dgl, ps, x, dh1, cc, sa], res, outs, accs,
                     scratch=[pltpu.VMEM((tm + HALO, POOL_WIDTH), F32)])


XTDY_TOKENS = 1024
XTDY_OUT_BYTES = 8 * 2**20


def _xtdy(name, x, dy):
    n_tok, kk = x.shape
    nn = dy.shape[1]
    bt = min(XTDY_TOKENS, n_tok)
    bk = kk
    while bk * nn * 4 > XTDY_OUT_BYTES and bk % 256 == 0:
        bk //= 2

    def kern(x_ref, dy_ref, o_ref):
        @pl.when(pl.program_id(1) == 0)
        def _():
            o_ref[...] = jnp.zeros(o_ref.shape, F32)
        o_ref[...] += _dot_tn(x_ref[...], dy_ref[...])

    return pl.pallas_call(
        kern, name=name, grid=(kk // bk, n_tok // bt),
        in_specs=[pl.BlockSpec((bt, bk), lambda a, t: (t, a)), pl.BlockSpec((bt, nn), lambda a, t: (t, 0))],
        out_specs=pl.BlockSpec((bk, nn), lambda a, t: (a, 0)), out_shape=_sds(kk, nn, F32),
        compiler_params=_params(("arbitrary", "arbitrary")))(x, dy)


def _rope_tables(positions):
    inv_freq = ROPE_THETA ** (-jnp.arange(0, QK_ROPE, 2, dtype=F32) / QK_ROPE)
    ang = positions.astype(F32)[:, None] * inv_freq
    cos, sin = jnp.cos(ang), jnp.sin(ang)
    n_tok = positions.shape[0]
    ones, z64 = jnp.ones((n_tok, ROPE_LANE), F32), jnp.zeros((n_tok, ROPE_LANE), F32)
    z32 = jnp.zeros((n_tok, HEAD_SLOT - ROPE_LANE - QK_ROPE), F32)
    return jnp.concatenate([ones, cos, cos, z32], 1), jnp.concatenate([z64, -sin, sin, z32], 1)


def _kernel_weights(full):
    w_in, w_uq, w_ukv = full["w_in"], full["w_uq"], full["w_ukv"]
    c0 = Q_LORA + KV_LORA
    z = lambda n: jnp.zeros((D_MODEL, n), w_in.dtype)
    w = dict(full)
    w["w_in"] = jnp.concatenate([w_in[:, :c0], z(ROPE_LANE), w_in[:, c0:c0 + QK_ROPE], z(HEAD_SLOT - ROPE_LANE - QK_ROPE),
                                 w_in[:, c0 + QK_ROPE:]], 1)
    w["w_uq"] = jnp.pad(w_uq.reshape(Q_LORA, N_HEADS, QK_NOPE + QK_ROPE),
                        ((0, 0), (0, 0), (0, HEAD_SLOT - QK_NOPE - QK_ROPE))).reshape(Q_LORA, QK_WIDTH)
    kv = w_ukv.reshape(KV_LORA, N_HEADS, QK_NOPE + V_HEAD)
    w["w_k"] = jnp.pad(kv[:, :, :QK_NOPE], ((0, 0), (0, 0), (0, HEAD_SLOT - QK_NOPE))).reshape(KV_LORA, QK_WIDTH)
    w["w_v"] = kv[:, :, QK_NOPE:].reshape(KV_LORA, N_HEADS * V_HEAD)
    e = np.zeros((HEAD_SLOT, QK_WIDTH), np.float32)
    sel = np.zeros((N_HEADS * V_HEAD, HEAD_SLOT), np.float32)
    for h in range(N_HEADS):
        for r in range(QK_ROPE):
            e[ROPE_LANE + r, h * HEAD_SLOT + ROPE_LANE + r] = 1.0
        sel[h * V_HEAD:(h + 1) * V_HEAD, h] = 1.0
    w["e_mat"] = jnp.asarray(e, MX)
    w["head_sel"] = jnp.asarray(sel, MX)
    w["w_pool"] = full["w_pool"].astype(MX)
    return w


def _local_step(x, p, positions, target, full):
    n_tok = x.shape[0]
    tm = min(512, n_tok)
    tm_mlp = min(256, n_tok)
    tk = min(ATTN_KEYS, n_tok)
    tq_fwd, tq_bwd = min(ATTN_QUERIES_FWD, n_tok), min(ATTN_QUERIES_BWD, n_tok)
    w = _kernel_weights(full)
    cc, sa = _rope_tables(positions)

    a, ps, u, gl, qn, kvn, q, k, v = _fwd_inproj(x, cc, sa, w, tm)
    vt = jnp.concatenate([v.T.reshape(N_HEADS, V_HEAD, n_tok), jnp.ones((N_HEADS, 1, n_tok), v.dtype),
                          jnp.zeros((N_HEADS, V_ROWS - V_HEAD - 1, n_tok), v.dtype)], 1).reshape(N_HEADS * V_ROWS, n_tok)
    attn_t, lse = _attn_fwd(q, k, vt, tq_fwd, tk)
    attn = attn_t.T
    d, pooled, a_br, p_br, merged, y, h1 = _fwd_mix(x, u, gl, attn, w, tm)
    m, zr, a2, f, h2 = _fwd_mlp(h1, w, tm_mlp)
    dh2, de, dzg, loss_cols, dg_ple = _ple_fwd_bwd(h2, p, target, w, tm)
    df, dz, dh1, dg_post_mlp, dg_pre_mlp = _bwd_mlp(dh2, f, h1, zr, w, tm_mlp)
    dy, da_br, dp_br, dgl, do, delta, dyp, dd, dg_post_mix, db_gate, dpool_scale = _bwd_mix(dh1, y, a_br, p_br, gl, attn, d, w, tm)
    dq_t, dk, dv = _attn_bwd(q, k, k.T, v, do, lse, delta[:, :N_HEADS].T, tq_bwd, tk)
    dq = dq_t.T.astype(MX)
    dqu, dproj, dx, dg_q, dg_kv, dg_pre_mix = _bwd_inproj(dq, dk, dv, dd, dgl, ps, x, dh1, cc, sa, w, tm)

    g_in = _xtdy("dw_in", a, dproj)
    g_uq = _xtdy("dw_uq", qn, dqu)
    g_k = _xtdy("dw_k", kvn, dk)
    g_v = _xtdy("dw_v", kvn, dv)
    g_pool = _xtdy("dw_pool", d, dyp)
    g_ba = _xtdy("dw_ba", attn, da_br)
    g_bp = _xtdy("dw_bp", pooled, dp_br)
    g_out = _xtdy("dw_out", merged, dy)
    g_ff1 = _xtdy("dw_ff1", m, dz)
    g_ff2 = _xtdy("dw_ff2", a2, df)
    g_pe = _xtdy("dw_pe", p, de)
    g_pg = _xtdy("dw_pg", h2, dzg)

    c0 = Q_LORA + KV_LORA
    grads = {
        "g_pre_mix": dg_pre_mix,
        "w_in": jnp.concatenate([g_in[:, :c0], g_in[:, c0 + ROPE_LANE:c0 + ROPE_LANE + QK_ROPE], g_in[:, SMALL_COLS:]], 1),
        "b_gate": db_gate,
        "g_q": dg_q,
        "w_uq": g_uq.reshape(Q_LORA, N_HEADS, HEAD_SLOT)[:, :, :QK_NOPE + QK_ROPE].reshape(Q_LORA, N_HEADS * (QK_NOPE + QK_ROPE)),
        "g_kv": dg_kv,
        "w_ukv": jnp.concatenate([g_k.reshape(KV_LORA, N_HEADS, HEAD_SLOT)[:, :, :QK_NOPE],
                                  g_v.reshape(KV_LORA, N_HEADS, V_HEAD)], 2).reshape(KV_LORA, N_HEADS * (QK_NOPE + V_HEAD)),
        "w_pool": jnp.stack([g_pool[g * POOL_GROUP:(g + 1) * POOL_GROUP, g * POOL_GROUP:(g + 1) * POOL_GROUP]
                             for g in range(len(POOL_WINDOWS))]),
        "pool_scale": dpool_scale,
        "w_branch_attn": g_ba,
        "w_branch_pool": g_bp,
        "w_out": g_out,
        "g_post_mix": dg_post_mix,
        "g_pre_mlp": dg_pre_mlp,
        "w_ff1": g_ff1,
        "w_ff2": g_ff2,
        "g_post_mlp": dg_post_mlp,
        "w_ple_proj": g_pe,
        "w_ple_gate": g_pg,
        "g_ple": dg_ple,
    }
    return loss_cols, dx, grads


def _place():
    return lax.axis_index("x"), lax.axis_index("y"), lax.axis_index("c")


CHIP_FLIPS = ((1, 0), (0, 1), (1, 1))


def _flip(x, y, fx, fy):
    return (1 - x if fx else x), (1 - y if fy else y)


_HBM = pl.BlockSpec(memory_space=pl.ANY)


def _allgather_shards(wp):
    rows = wp.shape[0]
    half = rows // 2

    def body(w_ref, out_ref, send_sems, recv_sems):
        x, y, c = _place()
        my_chip = 2 * x + y
        sibling = (x, y, 1 - c)

        def half_of(chip, hc):
            return out_ref.at[chip, pl.ds(pl.multiple_of(hc * half, 16), half), :]

        src = w_ref.at[pl.ds(pl.multiple_of(c * half, 16), half), :]
        chips = [_flip(x, y, fx, fy) for fx, fy in CHIP_FLIPS]
        first = []
        for j, (px, py) in enumerate(chips):
            cp = pltpu.make_async_remote_copy(src, half_of(my_chip, c), send_sems.at[j], recv_sems.at[j],
                                              device_id=(px, py, c), device_id_type=MESH)
            cp.start()
            first.append(cp)
        passed = []
        for j, (px, py) in enumerate(chips):
            landed = half_of(2 * px + py, c)
            pltpu.make_async_remote_copy(src, landed, send_sems.at[j], recv_sems.at[j],
                                         device_id=(px, py, c), device_id_type=MESH).wait_recv()
            cp = pltpu.make_async_remote_copy(landed, landed, send_sems.at[3 + j], recv_sems.at[3 + j],
                                              device_id=sibling, device_id_type=MESH)
            cp.start()
            passed.append(cp)
        for j, (px, py) in enumerate(chips):
            theirs = half_of(2 * px + py, 1 - c)
            pltpu.make_async_remote_copy(theirs, theirs, send_sems.at[3 + j], recv_sems.at[3 + j],
                                         device_id=sibling, device_id_type=MESH).wait_recv()
        for cp in first + passed:
            cp.wait_send()

    return pl.pallas_call(
        body, name="allgather_shards", out_shape=jax.ShapeDtypeStruct((N_CHIPS, rows, PACK_COLS), wp.dtype),
        in_specs=[_HBM], out_specs=_HBM,
        scratch_shapes=[pltpu.SemaphoreType.DMA((6,)), pltpu.SemaphoreType.DMA((6,))],
    )(wp)


def _exchange_halves(g):
    rows = g.shape[1]
    half = rows // 2

    def body(g_ref, r_ref, send_sem, recv_sem):
        x, y, c = _place()
        src = g_ref.at[:, pl.ds(pl.multiple_of((1 - c) * half, 8), half), :]
        cp = pltpu.make_async_remote_copy(src, r_ref, send_sem, recv_sem, device_id=(x, y, 1 - c), device_id_type=MESH)
        cp.start()
        cp.wait()

    return pl.pallas_call(
        body, name="exchange_halves", out_shape=jax.ShapeDtypeStruct((N_CHIPS, half, PACK_COLS), g.dtype),
        in_specs=[_HBM], out_specs=_HBM, scratch_shapes=[pltpu.SemaphoreType.DMA, pltpu.SemaphoreType.DMA],
    )(g)


def _add_halves(g, r, c):
    rows = g.shape[1]
    half = rows // 2
    br = half // 8

    def kern(c_ref, g_ref, r_ref, o_ref):
        o_ref[...] = (g_ref[...] + r_ref[...]).astype(o_ref.dtype)

    gs = pltpu.PrefetchScalarGridSpec(
        num_scalar_prefetch=1, grid=(N_CHIPS, 8),
        in_specs=[pl.BlockSpec((1, br, PACK_COLS), lambda k, t, c: (k, c[0] * 8 + t, 0)),
                  pl.BlockSpec((1, br, PACK_COLS), lambda k, t, c: (k, t, 0))],
        out_specs=pl.BlockSpec((1, br, PACK_COLS), lambda k, t, c: (k, t, 0)))
    return pl.pallas_call(kern, name="add_halves", grid_spec=gs,
                          out_shape=jax.ShapeDtypeStruct((N_CHIPS, half, PACK_COLS), WIRE),
                          compiler_params=_params(("arbitrary", "arbitrary")))(c.reshape(1), g, r)


def _scatter_pieces(s):
    def body(s_ref, r_ref, send_sems, recv_sems):
        x, y, c = _place()
        my_chip = 2 * x + y
        chips = [_flip(x, y, fx, fy) for fx, fy in CHIP_FLIPS]
        sent = []
        for j, (px, py) in enumerate(chips):
            cp = pltpu.make_async_remote_copy(s_ref.at[2 * px + py], r_ref.at[my_chip], send_sems.at[j], recv_sems.at[j],
                                              device_id=(px, py, c), device_id_type=MESH)
            cp.start()
            sent.append(cp)
        for j, (px, py) in enumerate(chips):
            slot = r_ref.at[2 * px + py]
            pltpu.make_async_remote_copy(slot, slot, send_sems.at[j], recv_sems.at[j],
                                         device_id=(px, py, c), device_id_type=MESH).wait_recv()
        for cp in sent:
            cp.wait_send()

    return pl.pallas_call(
        body, name="scatter_pieces", out_shape=jax.ShapeDtypeStruct(s.shape, s.dtype), in_specs=[_HBM], out_specs=_HBM,
        scratch_shapes=[pltpu.SemaphoreType.DMA((3,)), pltpu.SemaphoreType.DMA((3,))],
    )(s)


def _sum_pieces(r):
    half = r.shape[1]
    br = half // 8

    def kern(r_ref, o_ref):
        o_ref[...] = ((r_ref[0].astype(F32) + r_ref[1].astype(F32)) + r_ref[2].astype(F32)) + r_ref[3].astype(F32)

    return pl.pallas_call(
        kern, name="sum_pieces", grid=(8,), in_specs=[pl.BlockSpec((N_CHIPS, br, PACK_COLS), lambda t: (0, t, 0))],
        out_specs=pl.BlockSpec((br, PACK_COLS), lambda t: (t, 0)), out_shape=_sds(half, PACK_COLS, F32),
        compiler_params=_params(("arbitrary",)))(r)


def _join_halves(f):
    def body(f_ref, o_ref, send_sem, recv_sem):
        x, y, c = _place()
        cp = pltpu.make_async_remote_copy(f_ref, o_ref, send_sem, recv_sem, device_id=(x, y, 1 - c), device_id_type=MESH)
        cp.start()
        cp.wait()

    return pl.pallas_call(
        body, name="join_halves", out_shape=jax.ShapeDtypeStruct(f.shape, f.dtype), in_specs=[_HBM], out_specs=_HBM,
        scratch_shapes=[pltpu.SemaphoreType.DMA, pltpu.SemaphoreType.DMA],
    )(f)


def _allreduce_small(g):
    n_dev = 8

    def body(g_ref, o_ref, buf, send_sems, recv_sems):
        x, y, c = _place()
        me = 4 * x + 2 * y + c
        buf[me] = g_ref[...]
        peers = []
        for f in range(1, n_dev):
            px, py = _flip(x, y, f & 4, f & 2)
            pc = 1 - c if f & 1 else c
            peers.append((px, py, pc))
        sent = []
        for f, peer in enumerate(peers):
            cp = pltpu.make_async_remote_copy(g_ref, buf.at[me], send_sems.at[f], recv_sems.at[f], device_id=peer,
                                              device_id_type=MESH)
            cp.start()
            sent.append(cp)
        for f, (px, py, pc) in enumerate(peers):
            slot = buf.at[4 * px + 2 * py + pc]
            pltpu.make_async_remote_copy(slot, slot, send_sems.at[f], recv_sems.at[f], device_id=(px, py, pc),
                                         device_id_type=MESH).wait_recv()
        for cp in sent:
            cp.wait_send()
        total = buf[0]
        for k in range(1, n_dev):
            total = total + buf[k]
        o_ref[...] = total

    vmem = pl.BlockSpec(memory_space=pltpu.VMEM)
    return pl.pallas_call(
        body, name="allreduce_small", out_shape=jax.ShapeDtypeStruct(g.shape, g.dtype), in_specs=[vmem], out_specs=vmem,
        scratch_shapes=[pltpu.VMEM((n_dev,) + g.shape, g.dtype), pltpu.SemaphoreType.DMA((n_dev - 1,)),
                        pltpu.SemaphoreType.DMA((n_dev - 1,))],
    )(g)


def _adamw_update(g_ref, w_ref, m_ref, v_ref, d_o, m_o, v_o):
    c1 = 1.0 - ADAM_B1 ** ADAM_STEP
    c2 = 1.0 - ADAM_B2 ** ADAM_STEP
    g_ = g_ref[...]
    m_new = ADAM_B1 * m_ref[...] + (1.0 - ADAM_B1) * g_
    v_new = ADAM_B2 * v_ref[...] + (1.0 - ADAM_B2) * (g_ * g_)
    m_o[...] = m_new
    v_o[...] = v_new
    d_o[...] = -ADAM_LR * ((m_new / c1) / (jnp.sqrt(v_new / c2) + ADAM_EPS) + ADAM_WD * w_ref[...])


ADAMW_ROWS = 256


def _adamw(name, g, w, m, v):
    _, rows, cols = w.shape
    br = int(np.gcd(ADAMW_ROWS, rows))

    def kern(*refs):
        _adamw_update(*refs)

    spec = pl.BlockSpec((1, br, cols), lambda t: (0, t, 0))
    out = jax.ShapeDtypeStruct(w.shape, F32)
    return pl.pallas_call(kern, name="adamw_" + name, grid=(rows // br,), in_specs=[spec] * 4, out_specs=[spec] * 3,
                          out_shape=[out, out, out], compiler_params=_params(("arbitrary",)))(g, w, m, v)


def _adamw_small(gs, ws, ms, vs):
    n = len(gs)

    def kern(*refs):
        ins, outs = refs[:4 * n], refs[4 * n:]
        for k in range(n):
            _adamw_update(ins[k], ins[n + k], ins[2 * n + k], ins[3 * n + k], outs[k], outs[n + k], outs[2 * n + k])

    vmem = pl.BlockSpec(memory_space=pltpu.VMEM)
    out = [jax.ShapeDtypeStruct(w.shape, F32) for w in ws]
    res = pl.pallas_call(kern, name="adamw_small", in_specs=[vmem] * (4 * n), out_specs=[vmem] * (3 * n),
                         out_shape=out * 3, compiler_params=pltpu.CompilerParams(vmem_limit_bytes=VMEM_LIMIT))(*gs, *ws, *ms, *vs)
    return [(res[k], res[n + k], res[2 * n + k]) for k in range(n)]


def _shard_rows(shape, axis):
    k, n = shape
    return (k * n // N_CHIPS) // PACK_COLS


def _pack_shards(shards, dtype):
    parts = [shards[name].astype(dtype).reshape(-1, PACK_COLS) for name, _, _ in SHARDED]
    used = sum(p.shape[0] for p in parts)
    parts.append(jnp.zeros((BIG_ROWS - used, PACK_COLS), dtype))
    return jnp.concatenate(parts, 0)


def _unpack_shards(packed):
    out, r0 = {}, 0
    for name, (k, n), axis in SHARDED:
        nr = _shard_rows((k, n), axis)
        shape = (k // N_CHIPS, n) if axis == 0 else (k, n // N_CHIPS)
        out[name] = packed[r0:r0 + nr].reshape(shape)
        r0 += nr
    return out


def _unpack_full(gathered):
    out, r0 = {}, 0
    for name, (k, n), axis in SHARDED:
        nr = _shard_rows((k, n), axis)
        part = gathered[:, r0:r0 + nr]
        if axis == 0:
            out[name] = part.reshape(k, n)
        else:
            out[name] = part.reshape(N_CHIPS, k, n // N_CHIPS).transpose(1, 0, 2).reshape(k, n)
        r0 += nr
    return out


def _pack_pieces(grads):
    parts = []
    for name, (k, n), axis in SHARDED:
        g = grads[name]
        if axis == 0:
            parts.append(g.reshape(N_CHIPS, -1, PACK_COLS))
        else:
            parts.append(g.reshape(k, N_CHIPS, n // N_CHIPS).transpose(1, 0, 2).reshape(N_CHIPS, -1, PACK_COLS))
    used = sum(p.shape[1] for p in parts)
    parts.append(jnp.zeros((N_CHIPS, BIG_ROWS - used, PACK_COLS), F32))
    return jnp.concatenate(parts, 1)


def _pack_small(vals):
    flat = jnp.concatenate([vals[name].astype(F32).reshape(-1) for name, _ in SMALL])
    flat = jnp.concatenate([flat, jnp.zeros((SMALL_ROWS * PACK_COLS - flat.shape[0],), F32)])
    return flat.reshape(SMALL_ROWS, PACK_COLS)


def _unpack_small(packed):
    flat, out, o = packed.reshape(-1), {}, 0
    for name, shape in SMALL:
        n = int(np.prod(shape))
        out[name] = flat[o:o + n].reshape(shape)
        o += n
    return out


def kernel(x, p, positions, g_pre_mix, w_in, b_gate, g_q, w_uq, g_kv, w_ukv, w_pool, pool_scale, w_branch_attn, w_branch_pool, w_out, g_post_mix, g_pre_mlp, w_ff1, w_ff2, g_post_mlp, w_ple_proj, w_ple_gate, g_ple, loss_target, m_g_pre_mix, m_w_in, m_b_gate, m_g_q, m_w_uq, m_g_kv, m_w_ukv, m_w_pool, m_pool_scale, m_w_branch_attn, m_w_branch_pool, m_w_out, m_g_post_mix, m_g_pre_mlp, m_w_ff1, m_w_ff2, m_g_post_mlp, m_w_ple_proj, m_w_ple_gate, m_g_ple, v_g_pre_mix, v_w_in, v_b_gate, v_g_q, v_w_uq, v_g_kv, v_w_ukv, v_w_pool, v_pool_scale, v_w_branch_attn, v_w_branch_pool, v_w_out, v_g_post_mix, v_g_pre_mlp, v_w_ff1, v_w_ff2, v_g_post_mlp, v_w_ple_proj, v_w_ple_gate, v_g_ple):
    given = dict(locals())
    weights = {n: given[n] for n in WEIGHT_ORDER}
    moments_m = {n: given["m_" + n] for n in WEIGHT_ORDER}
    moments_v = {n: given["v_" + n] for n in WEIGHT_ORDER}
    c = lax.axis_index("c")

    big_w = {name: weights[name][0] for name, _, _ in SHARDED}
    my_chip = 2 * lax.axis_index("x") + lax.axis_index("y")
    packed_w = _pack_shards(big_w, MX)
    full = _unpack_full(lax.dynamic_update_slice(_allgather_shards(packed_w), packed_w[None], (my_chip, 0, 0)))
    for name, _ in SMALL:
        full[name] = weights[name][0] if name == "w_pool" else weights[name]

    loss_cols, dx, grads = _local_step(x[0], p[0, 0], positions[0], loss_target[0], full)
    loss = lax.psum(0.5 * jnp.sum(loss_cols) / D_MODEL, ("x", "y", "c"))

    pieces = _pack_pieces(grads)
    summed = _add_halves(pieces, _exchange_halves(pieces), c)
    mine = lax.dynamic_slice(summed, (my_chip, 0, 0), (1,) + summed.shape[1:])
    reduced = _sum_pieces(lax.dynamic_update_slice(_scatter_pieces(summed), mine, (my_chip, 0, 0)))
    theirs = _join_halves(reduced)
    g_big = jnp.where(c == 0, jnp.concatenate([reduced, theirs]), jnp.concatenate([theirs, reduced]))
    g_small = _allreduce_small(_pack_small(grads))

    out = {}
    for name, g in _unpack_shards(g_big).items():
        out[name] = (g[None], *_adamw(name, g[None], weights[name], moments_m[name], moments_v[name]))
    small_g = _unpack_small(g_small)
    names = [n for n, _ in SMALL]
    updates = _adamw_small([small_g[n] for n in names], [weights[n] for n in names], [moments_m[n] for n in names],
                           [moments_v[n] for n in names])
    for n, upd in zip(names, updates):
        out[n] = (small_g[n], *upd)
    return (loss, dx[None], *[out[n][k] for k in range(4) for n in WEIGHT_ORDER])
```

```python
import functools

import numpy as np
import jax
import jax.numpy as jnp
from jax import lax
from jax.experimental import pallas as pl
from jax.experimental.pallas import tpu as pltpu

F32 = jnp.float32
MX = jnp.bfloat16
WIRE = jnp.bfloat16

D_MODEL = 1024
N_HEADS = 8
QK_NOPE = 64
QK_ROPE = 32
V_HEAD = 64
Q_LORA = 384
KV_LORA = 256
POOL_WINDOWS = (2, 4, 8, 16)
POOL_GROUP = 128
POOL_WIDTH = 512
D_FF = 4096
PLE_DIM = 256
ROPE_THETA = 10000.0
EPS = 1e-6
HEAD_SLOT = 128
QK_WIDTH = N_HEADS * HEAD_SLOT
ROPE_LANE = 64
SMALL_COLS = Q_LORA + KV_LORA + HEAD_SLOT
IN_PAD = SMALL_COLS + POOL_WIDTH + 2 * D_MODEL
SCALE = (QK_NOPE + QK_ROPE) ** -0.5
LOG2E = 1.4426950408889634
NEG = -1e30
HALO = 16

ADAM_LR = 0.001
ADAM_B1 = 0.9
ADAM_B2 = 0.999
ADAM_EPS = 1e-08
ADAM_WD = 0.01
ADAM_STEP = 10

VMEM_LIMIT = 56 * 2**20
MESH = pl.DeviceIdType.MESH

SHARDED = (
    ("w_in", (1024, 3232), 1),
    ("w_uq", (384, 768), 1),
    ("w_ukv", (256, 1024), 1),
    ("w_branch_attn", (512, 1024), 1),
    ("w_branch_pool", (512, 1024), 1),
    ("w_out", (1024, 1024), 0),
    ("w_ff1", (1024, 4096), 1),
    ("w_ff2", (4096, 1024), 0),
    ("w_ple_proj", (256, 1024), 1),
    ("w_ple_gate", (1024, 1024), 0),
)
SMALL = (
    ("g_pre_mix", (1, 1024)),
    ("b_gate", (1, 2048)),
    ("g_q", (1, 384)),
    ("g_kv", (1, 256)),
    ("w_pool", (1, 4, 128, 128)),
    ("pool_scale", (1, 512)),
    ("g_post_mix", (1, 1024)),
    ("g_pre_mlp", (1, 1024)),
    ("g_post_mlp", (1, 1024)),
    ("g_ple", (1, 1024)),
)
WEIGHT_ORDER = ("g_pre_mix", "w_in", "b_gate", "g_q", "w_uq", "g_kv", "w_ukv", "w_pool", "pool_scale", "w_branch_attn",
                "w_branch_pool", "w_out", "g_post_mix", "g_pre_mlp", "w_ff1", "w_ff2", "g_post_mlp", "w_ple_proj",
                "w_ple_gate", "g_ple")
N_CHIPS = 4
PACK_COLS = 1024
REDUCE_ROWS = 160
SMALL_ROWS = 80
FIRST = ("w_in", "w_uq", "w_ukv")
REST = tuple(name for name, _, _ in SHARDED if name not in FIRST)


def _dot(a, b):
    return jnp.dot(a.astype(MX), b.astype(MX), preferred_element_type=F32)


def _dot_nt(a, b):
    return lax.dot_general(a.astype(MX), b.astype(MX), (((1,), (1,)), ((), ())), preferred_element_type=F32)


def _dot_tn(a, b):
    return lax.dot_general(a.astype(MX), b.astype(MX), (((0,), (0,)), ((), ())), preferred_element_type=F32)


def _sig(x):
    return 1.0 / (1.0 + jnp.exp(-x))


def _rms(x, g):
    r = lax.rsqrt(jnp.mean(x * x, axis=1, keepdims=True) + EPS)
    xh = x * r
    return xh * g, xh, r


def _rms_bwd(xh, r, g, dy):
    dxn = dy * g
    dx = r * (dxn - xh * jnp.mean(dxn * xh, axis=1, keepdims=True))
    return dx, jnp.sum(dy * xh, axis=0, keepdims=True)


def _rot_half(v):
    lane = lax.broadcasted_iota(jnp.int32, v.shape, 1)
    return jnp.where(lane < ROPE_LANE + QK_ROPE // 2, pltpu.roll(v, HEAD_SLOT - QK_ROPE // 2, 1), pltpu.roll(v, QK_ROPE // 2, 1))


def _rope(v, cc, sa):
    return v * cc + _rot_half(v) * sa


def _unrope(v, cc, sa):
    return v * cc - _rot_half(v) * sa


def _params(sem):
    return pltpu.CompilerParams(dimension_semantics=sem, vmem_limit_bytes=VMEM_LIMIT)


def _tok_call(name, body, n_tok, tm, tiled, resident, outs, accs=(), scratch=(), exchange=None):
    def as_pair(t):
        if isinstance(t, tuple):
            return t
        return t, pl.BlockSpec((tm, t.shape[1]), lambda i: (i, 0))
    tiled = [as_pair(t) for t in tiled]
    res_specs = [pl.BlockSpec(r.shape, lambda i, nd=r.ndim: (0,) * nd, pipeline_mode=pl.Buffered(1)) for r in resident]
    out_specs = [pl.BlockSpec((tm, o.shape[1]), lambda i: (i, 0)) for o in outs]
    out_specs += [pl.BlockSpec(a.shape, lambda i: (0, 0)) for a in accs]
    n_t, n_r, n_o, n_a, n_s = len(tiled), len(resident), len(outs), len(accs), len(scratch)
    n_steps = n_tok // tm
    operands = [a for a, _ in tiled] + list(resident)
    in_specs = [s for _, s in tiled] + res_specs
    out_shape = list(outs) + list(accs)
    scratch = list(scratch)
    if exchange is not None:
        ex_in, ex_out, ex_sems, ex_steps = exchange
        operands.append(ex_in)
        in_specs.append(_HBM)
        out_shape.append(ex_out)
        out_specs.append(_HBM)
        scratch += list(ex_sems)

    def kern(*refs):
        refs = list(refs)
        n_in = n_t + n_r + (exchange is not None)
        n_out = n_o + n_a + (exchange is not None)
        tin, res = refs[:n_t], refs[n_t:n_t + n_r]
        tout = refs[n_in:n_in + n_o]
        acc = refs[n_in + n_o:n_in + n_o + n_a]
        scr = refs[n_in + n_out:n_in + n_out + n_s]
        i = pl.program_id(0)
        if exchange is not None:
            ex_steps(i, n_steps, refs[n_in - 1], refs[n_in + n_out - 1], refs[n_in + n_out + n_s:])

        @pl.when(i == 0)
        def _():
            for a in acc:
                a[...] = jnp.zeros(a.shape, a.dtype)
        body(i, tin, res, tout, acc, scr)

    return pl.pallas_call(
        kern, name=name, grid=(n_steps,), in_specs=in_specs, out_specs=out_specs,
        out_shape=out_shape, scratch_shapes=scratch, compiler_params=_params(("arbitrary",)),
    )(*operands)


def _sds(rows, cols, dtype):
    return jax.ShapeDtypeStruct((rows, cols), dtype)


def _fwd_inproj(x, cc, sa, w, tm, gather=None):
    n_tok = x.shape[0]

    def body(i, tin, res, tout, acc, scr):
        x_ref, c_ref, s_ref = tin
        g_pre, w_in, g_q, w_uq, g_kv, w_k, w_v, e_mat = res
        a_o, ps_o, u_o, gl_o, qn_o, kvn_o, q_o, k_o, v_o = tout
        a = _rms(x_ref[...], g_pre[...])[0].astype(MX)
        a_o[...] = a
        ps = _dot(a, w_in[:, :SMALL_COLS])
        ps_o[...] = ps.astype(ps_o.dtype)
        u_o[...] = _dot(a, w_in[:, SMALL_COLS:SMALL_COLS + POOL_WIDTH]).astype(u_o.dtype)
        gl_o[...] = _dot(a, w_in[:, SMALL_COLS + POOL_WIDTH:]).astype(gl_o.dtype)
        cc_, sa_ = c_ref[...], s_ref[...]
        qn = _rms(ps[:, :Q_LORA], g_q[...])[0].astype(MX)
        qn_o[...] = qn
        q = _dot(qn, w_uq[...])
        for h in range(N_HEADS):
            hs = slice(h * HEAD_SLOT, (h + 1) * HEAD_SLOT)
            q_o[:, hs] = (_rope(q[:, hs], cc_, sa_) * (SCALE * LOG2E)).astype(q_o.dtype)
        kvn = _rms(ps[:, Q_LORA:Q_LORA + KV_LORA], g_kv[...])[0].astype(MX)
        kvn_o[...] = kvn
        kr = _rope(ps[:, Q_LORA + KV_LORA:], cc_, sa_)
        k_o[...] = (_dot(kvn, w_k[...]) + _dot(kr, e_mat[...])).astype(k_o.dtype)
        v_o[...] = _dot(kvn, w_v[...]).astype(v_o.dtype)

    outs = [_sds(n_tok, D_MODEL, MX), _sds(n_tok, SMALL_COLS, MX), _sds(n_tok, POOL_WIDTH, MX), _sds(n_tok, 2 * D_MODEL, MX),
            _sds(n_tok, Q_LORA, MX), _sds(n_tok, KV_LORA, MX), _sds(n_tok, QK_WIDTH, MX), _sds(n_tok, QK_WIDTH, MX),
            _sds(n_tok, N_HEADS * V_HEAD, MX)]
    res = [w["g_pre_mix"], w["w_in"], w["g_q"], w["w_uq"], w["g_kv"], w["w_k"], w["w_v"], w["e_mat"]]
    exchange = None
    if gather is not None:
        gathered = jax.ShapeDtypeStruct((N_CHIPS,) + gather.shape, gather.dtype)
        exchange = (gather, gathered, [pltpu.SemaphoreType.DMA((6,)), pltpu.SemaphoreType.DMA((6,))], _gather_steps)
    return _tok_call("fwd_inproj", body, n_tok, tm, [x, cc, sa], res, outs, exchange=exchange)


def _causal_pairs(nq, ratio, by_kv):
    if by_kv:
        pairs = [(i, j) for j in range(nq * ratio) for i in range(j // ratio, nq)]
    else:
        pairs = [(i, j) for i in range(nq) for j in range((i + 1) * ratio)]
    return (jnp.asarray(np.array([p[0] for p in pairs], np.int32)), jnp.asarray(np.array([p[1] for p in pairs], np.int32)))


def _keep_t(tk, tq, off):
    return lax.broadcasted_iota(jnp.int32, (tk, tq), 0) + off <= lax.broadcasted_iota(jnp.int32, (tk, tq), 1)


ATTN_KEYS = 512
ATTN_QUERIES_FWD = 1024
ATTN_QUERIES_BWD = 1024
V_ROWS = 80


def _attn_fwd(q, k, vt, tq, tk):
    n_tok = q.shape[0]
    nq, ratio = n_tok // tq, tq // tk
    qi, kj = _causal_pairs(nq, ratio, by_kv=False)

    def kern(qi_ref, kj_ref, q_ref, k_ref, vt_ref, ot_ref, lse_ref, m_s, acc_s, st_s):
        s_id = pl.program_id(0)
        i, j = qi_ref[s_id], kj_ref[s_id]

        @pl.when(j == 0)
        def _():
            m_s[...] = jnp.full(m_s.shape, NEG, F32)
            acc_s[...] = jnp.zeros(acc_s.shape, F32)

        def scores(h):
            hs = slice(h * HEAD_SLOT, (h + 1) * HEAD_SLOT)
            return _dot_nt(k_ref[:, hs], q_ref[:, hs])

        def heads(masked):
            keep = _keep_t(tk, tq, j * tk - i * tq) if masked else None
            st_s[0] = scores(0)
            for h in range(N_HEADS):
                if h + 1 < N_HEADS:
                    st_s[(h + 1) % 2] = scores(h + 1)
                st = st_s[h % 2]
                if masked:
                    st = jnp.where(keep, st, NEG)
                m_old = m_s[h]
                m_new = jnp.maximum(m_old, jnp.max(st, axis=0, keepdims=True))
                pt = jnp.exp2(st - m_new)
                acc_s[h] = jnp.exp2(m_old - m_new) * acc_s[h] + _dot(vt_ref[h * V_ROWS:(h + 1) * V_ROWS, :], pt)
                m_s[h] = m_new

        @pl.when(j < i * ratio)
        def _():
            heads(False)

        @pl.when(j >= i * ratio)
        def _():
            heads(True)

        @pl.when(j == (i + 1) * ratio - 1)
        def _():
            for h in range(N_HEADS):
                total = acc_s[h, V_HEAD:V_HEAD + 1, :]
                ot_ref[h * V_HEAD:(h + 1) * V_HEAD, :] = (acc_s[h, :V_HEAD, :] / total).astype(ot_ref.dtype)
                lse_ref[h:h + 1, :] = m_s[h] + jnp.log2(total)

    gs = pltpu.PrefetchScalarGridSpec(
        num_scalar_prefetch=2, grid=(qi.shape[0],),
        in_specs=[pl.BlockSpec((tq, QK_WIDTH), lambda s, qi, kj: (qi[s], 0)),
                  pl.BlockSpec((tk, QK_WIDTH), lambda s, qi, kj: (kj[s], 0)),
                  pl.BlockSpec((N_HEADS * V_ROWS, tk), lambda s, qi, kj: (0, kj[s]))],
        out_specs=[pl.BlockSpec((N_HEADS * V_HEAD, tq), lambda s, qi, kj: (0, qi[s])),
                   pl.BlockSpec((N_HEADS, tq), lambda s, qi, kj: (0, qi[s]))],
        scratch_shapes=[pltpu.VMEM((N_HEADS, 1, tq), F32), pltpu.VMEM((N_HEADS, V_ROWS, tq), F32),
                        pltpu.VMEM((2, tk, tq), F32)])
    return pl.pallas_call(kern, name="attn_fwd", grid_spec=gs,
                          out_shape=[_sds(N_HEADS * V_HEAD, n_tok, MX), _sds(N_HEADS, n_tok, F32)],
                          compiler_params=_params(("arbitrary",)))(qi, kj, q, k, vt)


def _pool_windows(ext, i, tm, first_row):
    row = i * tm + lax.broadcasted_iota(jnp.int32, (tm, 1), 0)
    out = []
    for g, w in enumerate(POOL_WINDOWS):
        cs = slice(g * POOL_GROUP, (g + 1) * POOL_GROUP)
        s = ext[pl.ds(first_row, tm), cs]
        for k in range(1, w):
            s = s + ext[pl.ds(first_row - k, tm), cs]
        cnt = jnp.minimum(row + 1, w).astype(F32)
        out.append(s / cnt)
    return out


def _fwd_mix(x, u, gl, attn, w, tm):
    n_tok = x.shape[0]
    halo_spec = pl.BlockSpec((HALO, POOL_WIDTH), lambda i: (jnp.maximum(i * (tm // HALO) - 1, 0), 0))

    def body(i, tin, res, tout, acc, scr):
        x_ref, u_ref, uh_ref, gl_ref, at_ref = tin
        w_pool, pool_scale, w_ba, w_bp, b_gate, w_out, g_post = res
        d_o, pooled_o, a_o, pp_o, merged_o, y_o, h1_o = tout
        ext, = scr
        ext[pl.ds(0, HALO), :] = jnp.where(i > 0, uh_ref[...].astype(F32), 0.0)
        ext[pl.ds(HALO, tm), :] = u_ref[...].astype(F32)
        means = _pool_windows(ext, i, tm, HALO)
        for g in range(len(POOL_WINDOWS)):
            cs = slice(g * POOL_GROUP, (g + 1) * POOL_GROUP)
            d = (means[g] - ext[pl.ds(HALO, tm), cs]).astype(MX)
            d_o[:, cs] = d
            pooled_o[:, cs] = (_dot(d, w_pool[g]) * pool_scale[:, cs]).astype(pooled_o.dtype)
        a_br = _dot(at_ref[...], w_ba[...])
        p_br = _dot(pooled_o[...], w_bp[...])
        a_o[...] = a_br.astype(a_o.dtype)
        pp_o[...] = p_br.astype(pp_o.dtype)
        gates = _sig(gl_ref[...].astype(F32) + b_gate[...])
        merged = (gates[:, :D_MODEL] * a_br + gates[:, D_MODEL:] * p_br).astype(MX)
        merged_o[...] = merged
        y = _dot(merged, w_out[...])
        y_o[...] = y.astype(y_o.dtype)
        h1_o[...] = x_ref[...] + _rms(y, g_post[...])[0]

    outs = [_sds(n_tok, POOL_WIDTH, MX), _sds(n_tok, POOL_WIDTH, MX), _sds(n_tok, D_MODEL, MX), _sds(n_tok, D_MODEL, MX),
            _sds(n_tok, D_MODEL, MX), _sds(n_tok, D_MODEL, MX), _sds(n_tok, D_MODEL, F32)]
    res = [w["w_pool"], w["pool_scale"], w["w_branch_attn"], w["w_branch_pool"], w["b_gate"], w["w_out"], w["g_post_mix"]]
    return _tok_call("fwd_mix", body, n_tok, tm, [x, u, (u, halo_spec), gl, attn], res, outs,
                     scratch=[pltpu.VMEM((tm + HALO, POOL_WIDTH), F32)])


def _fwd_mlp(h1, w, tm):
    n_tok = h1.shape[0]

    def body(i, tin, res, tout, acc, scr):
        h1_ref, = tin
        g_pre, w1, w2, g_post = res
        m_o, zr_o, a2_o, f_o, h2_o = tout
        h1_ = h1_ref[...]
        m = _rms(h1_, g_pre[...])[0].astype(MX)
        m_o[...] = m
        zr = jnp.maximum(_dot(m, w1[...]), 0.0)
        zr_o[...] = zr.astype(zr_o.dtype)
        a2 = (zr * zr).astype(MX)
        a2_o[...] = a2
        f = _dot(a2, w2[...])
        f_o[...] = f.astype(f_o.dtype)
        h2_o[...] = h1_ + _rms(f, g_post[...])[0]

    outs = [_sds(n_tok, D_MODEL, MX), _sds(n_tok, D_FF, MX), _sds(n_tok, D_FF, MX), _sds(n_tok, D_MODEL, MX),
            _sds(n_tok, D_MODEL, F32)]
    res = [w["g_pre_mlp"], w["w_ff1"], w["w_ff2"], w["g_post_mlp"]]
    return _tok_call("fwd_mlp", body, n_tok, tm, [h1], res, outs)


def _ple_fwd_bwd(h2, p, target, w, tm):
    n_tok = h2.shape[0]

    def body(i, tin, res, tout, acc, scr):
        h2_ref, p_ref, t_ref = tin
        w_pe, w_pg, g_ple = res
        dh2_o, de_o, dzg_o = tout
        loss_a, dg_a = acc
        h2_ = h2_ref[...]
        e = _dot(p_ref[...], w_pe[...])
        pg = _sig(_dot(h2_, w_pg[...]))
        t = pg * e
        g = g_ple[...]
        tn, th, r = _rms(t, g)
        diff = h2_ + tn - t_ref[...]
        loss_a[...] += jnp.sum(diff * diff, axis=0, keepdims=True)
        dh3 = diff * (1.0 / D_MODEL)
        dt, dg = _rms_bwd(th, r, g, dh3)
        dg_a[...] += dg
        de_o[...] = (dt * pg).astype(de_o.dtype)
        dzg = (dt * e * pg * (1.0 - pg)).astype(MX)
        dzg_o[...] = dzg
        dh2_o[...] = dh3 + _dot_nt(dzg, w_pg[...])

    outs = [_sds(n_tok, D_MODEL, F32), _sds(n_tok, D_MODEL, MX), _sds(n_tok, D_MODEL, MX)]
    accs = [_sds(1, D_MODEL, F32), _sds(1, D_MODEL, F32)]
    return _tok_call("ple_fwd_bwd", body, n_tok, tm, [h2, p, target], [w["w_ple_proj"], w["w_ple_gate"], w["g_ple"]], outs, accs)


def _bwd_mlp(dh2, f, h1, zr, w, tm):
    n_tok = dh2.shape[0]

    def body(i, tin, res, tout, acc, scr):
        dh2_ref, f_ref, h1_ref, zr_ref = tin
        g_pre, w1, w2, g_post = res
        df_o, dz_o, dh1_o = tout
        dg_post_a, dg_pre_a = acc
        dh2_ = dh2_ref[...]
        gp = g_post[...]
        _, fh, rf = _rms(f_ref[...].astype(F32), gp)
        df, dg = _rms_bwd(fh, rf, gp, dh2_)
        dg_post_a[...] += dg
        df = df.astype(MX)
        df_o[...] = df
        dz = (_dot_nt(df, w2[...]) * (2.0 * zr_ref[...].astype(F32))).astype(MX)
        dz_o[...] = dz
        dm = _dot_nt(dz, w1[...])
        gq = g_pre[...]
        _, hh, rh = _rms(h1_ref[...], gq)
        dh1, dg = _rms_bwd(hh, rh, gq, dm)
        dg_pre_a[...] += dg
        dh1_o[...] = dh2_ + dh1

    outs = [_sds(n_tok, D_MODEL, MX), _sds(n_tok, D_FF, MX), _sds(n_tok, D_MODEL, F32)]
    accs = [_sds(1, D_MODEL, F32), _sds(1, D_MODEL, F32)]
    res = [w["g_pre_mlp"], w["w_ff1"], w["w_ff2"], w["g_post_mlp"]]
    return _tok_call("bwd_mlp", body, n_tok, tm, [dh2, f, h1, zr], res, outs, accs)


def _bwd_mix(dh1, y, a_br, p_br, gl, attn, d, w, tm):
    n_tok = dh1.shape[0]

    def body(i, tin, res, tout, acc, scr):
        dh1_ref, y_ref, a_ref, pp_ref, gl_ref, at_ref, d_ref = tin
        g_post, w_out, b_gate, w_ba, w_bp, w_pool, pool_scale, sel = res
        dy_o, da_o, dpp_o, dgl_o, do_o, delta_o, dyp_o, dd_o = tout
        dg_post_a, db_a, dps_a = acc
        g = g_post[...]
        _, yh, r = _rms(y_ref[...].astype(F32), g)
        dy, dg = _rms_bwd(yh, r, g, dh1_ref[...])
        dg_post_a[...] += dg
        dy = dy.astype(MX)
        dy_o[...] = dy
        dmerged = _dot_nt(dy, w_out[...])
        gates = _sig(gl_ref[...].astype(F32) + b_gate[...])
        ga, gp = gates[:, :D_MODEL], gates[:, D_MODEL:]
        da = (dmerged * ga).astype(MX)
        dpp = (dmerged * gp).astype(MX)
        da_o[...] = da
        dpp_o[...] = dpp
        dgl_a = dmerged * a_ref[...].astype(F32) * ga * (1.0 - ga)
        dgl_p = dmerged * pp_ref[...].astype(F32) * gp * (1.0 - gp)
        dgl_o[:, :D_MODEL] = dgl_a.astype(dgl_o.dtype)
        dgl_o[:, D_MODEL:] = dgl_p.astype(dgl_o.dtype)
        db_a[:, :D_MODEL] += jnp.sum(dgl_a, axis=0, keepdims=True)
        db_a[:, D_MODEL:] += jnp.sum(dgl_p, axis=0, keepdims=True)
        do = _dot_nt(da, w_ba[...]).astype(MX)
        do_o[...] = do
        prod = do.astype(F32) * at_ref[...].astype(F32)
        hi = prod.astype(MX)
        lo = (prod - hi.astype(F32)).astype(MX)
        delta_o[...] = _dot(hi, sel[...]) + _dot(lo, sel[...])
        dpooled = _dot_nt(dpp, w_bp[...])
        for gi in range(len(POOL_WINDOWS)):
            cs = slice(gi * POOL_GROUP, (gi + 1) * POOL_GROUP)
            ypre = _dot(d_ref[:, cs], w_pool[gi])
            dps_a[:, cs] += jnp.sum(dpooled[:, cs] * ypre, axis=0, keepdims=True)
            dyp = (dpooled[:, cs] * pool_scale[:, cs]).astype(MX)
            dyp_o[:, cs] = dyp
            dd_o[:, cs] = _dot_nt(dyp, w_pool[gi])

    outs = [_sds(n_tok, D_MODEL, MX), _sds(n_tok, D_MODEL, MX), _sds(n_tok, D_MODEL, MX), _sds(n_tok, 2 * D_MODEL, MX),
            _sds(n_tok, N_HEADS * V_HEAD, MX), _sds(n_tok, HEAD_SLOT, F32), _sds(n_tok, POOL_WIDTH, MX),
            _sds(n_tok, POOL_WIDTH, F32)]
    accs = [_sds(1, D_MODEL, F32), _sds(1, 2 * D_MODEL, F32), _sds(1, POOL_WIDTH, F32)]
    res = [w["g_post_mix"], w["w_out"], w["b_gate"], w["w_branch_attn"], w["w_branch_pool"], w["w_pool"], w["pool_scale"],
           w["head_sel"]]
    return _tok_call("bwd_mix", body, n_tok, tm, [dh1, y, a_br, p_br, gl, attn, d], res, outs, accs)


def _bwd_heads(q_ref, k_ref, v_ref, do_ref, lse_ref, dl_ref, st_s, dpt_s, keep, use, n_heads):
    def products(h):
        hs = slice(h * HEAD_SLOT, (h + 1) * HEAD_SLOT)
        vs = slice(h * V_HEAD, (h + 1) * V_HEAD)
        st_s[h % 2] = _dot_nt(k_ref[:, hs], q_ref[:, hs])
        dpt_s[h % 2] = _dot_nt(v_ref[:, vs], do_ref[:, vs])

    products(0)
    for h in range(n_heads):
        if h + 1 < n_heads:
            products(h + 1)
        st = st_s[h % 2]
        if keep is not None:
            st = jnp.where(keep, st, NEG)
        pt = jnp.exp2(st - lse_ref[h:h + 1, :])
        use(h, pt, pt * (dpt_s[h % 2] - dl_ref[h:h + 1, :]))


HEAD_GROUP = 4


def _attn_bwd(q, k, kt, v, do, lse, delta, tq, tk, scatter=None):
    n_tok = q.shape[0]
    nq, ratio = n_tok // tq, tq // tk
    n_groups = N_HEADS // HEAD_GROUP
    gq, gv = HEAD_GROUP * HEAD_SLOT, HEAD_GROUP * V_HEAD
    qi, kj = _causal_pairs(nq, ratio, by_kv=True)

    n_pairs = qi.shape[0]

    def kern(qi_ref, kj_ref, q_ref, k_ref, kt_ref, v_ref, do_ref, lse_ref, dl_ref, *rest):
        if scatter is not None:
            s_hbm, dq_ref, dk_ref, dv_ref, r_hbm, dk_s, dv_s, st_s, dpt_s, send_sems, recv_sems = rest
        else:
            dq_ref, dk_ref, dv_ref, dk_s, dv_s, st_s, dpt_s = rest
        s_id = pl.program_id(1)
        i, j = qi_ref[s_id], kj_ref[s_id]
        cols = pl.ds(pl.multiple_of(i * tq, tq), tq)
        if scatter is not None:
            group = pl.program_id(0)
            _scatter_steps(jnp.logical_and(group == 0, s_id == 0),
                           jnp.logical_and(group == n_groups - 1, s_id == n_pairs - 1), s_hbm, r_hbm, (send_sems, recv_sems))

        @pl.when(s_id == 0)
        def _():
            dq_ref[...] = jnp.zeros(dq_ref.shape, F32)

        def use(h, pt, dst):
            hs = slice(h * HEAD_SLOT, (h + 1) * HEAD_SLOT)
            dv_s[h] += _dot(pt, do_ref[:, h * V_HEAD:(h + 1) * V_HEAD])
            dk_s[:, hs] += _dot(dst, q_ref[:, hs])
            dq_ref[hs, cols] += _dot(kt_ref[hs, :], dst)

        def heads(masked):
            keep = _keep_t(tk, tq, j * tk - i * tq) if masked else None
            _bwd_heads(q_ref, k_ref, v_ref, do_ref, lse_ref.at[0], dl_ref.at[0], st_s, dpt_s, keep, use, HEAD_GROUP)

        @pl.when(j >= i * ratio)
        def _():
            dk_s[...] = jnp.zeros(dk_s.shape, F32)
            dv_s[...] = jnp.zeros(dv_s.shape, F32)
            heads(True)

        @pl.when(j < i * ratio)
        def _():
            heads(False)

        @pl.when(i == nq - 1)
        def _():
            dk_ref[...] = (dk_s[...] * (1.0 / LOG2E)).astype(dk_ref.dtype)
            for h in range(HEAD_GROUP):
                dv_ref[:, h * V_HEAD:(h + 1) * V_HEAD] = dv_s[h].astype(dv_ref.dtype)

    at_q = lambda g, s, qi, kj: (qi[s], g)
    at_k = lambda g, s, qi, kj: (kj[s], g)
    at_kt = lambda g, s, qi, kj: (g, kj[s])
    at_stat = lambda g, s, qi, kj: (g, 0, qi[s])
    in_specs = [pl.BlockSpec((tq, gq), at_q), pl.BlockSpec((tk, gq), at_k), pl.BlockSpec((gq, tk), at_kt),
                pl.BlockSpec((tk, gv), at_k), pl.BlockSpec((tq, gv), at_q),
                pl.BlockSpec((1, HEAD_GROUP, tq), at_stat), pl.BlockSpec((1, HEAD_GROUP, tq), at_stat)]
    out_specs = [pl.BlockSpec((gq, n_tok), lambda g, s, qi, kj: (g, 0), pipeline_mode=pl.Buffered(1)),
                 pl.BlockSpec((tk, gq), at_k), pl.BlockSpec((tk, gv), at_k)]
    out_shape = [_sds(QK_WIDTH, n_tok, F32), _sds(n_tok, QK_WIDTH, MX), _sds(n_tok, N_HEADS * V_HEAD, MX)]
    scratch = [pltpu.VMEM((tk, gq), F32), pltpu.VMEM((HEAD_GROUP, tk, V_HEAD), F32),
               pltpu.VMEM((2, tk, tq), F32), pltpu.VMEM((2, tk, tq), F32)]
    stat3 = lambda a: a.reshape(n_groups, HEAD_GROUP, n_tok)
    operands = [qi, kj, q, k, kt, v, do, stat3(lse), stat3(delta)]
    if scatter is not None:
        operands.append(scatter)
        in_specs.append(_HBM)
        out_specs.append(_HBM)
        out_shape.append(jax.ShapeDtypeStruct(scatter.shape, scatter.dtype))
        scratch += [pltpu.SemaphoreType.DMA((3,)), pltpu.SemaphoreType.DMA((3,))]
    gs = pltpu.PrefetchScalarGridSpec(num_scalar_prefetch=2, grid=(n_groups, n_pairs), in_specs=in_specs,
                                      out_specs=out_specs, scratch_shapes=scratch)
    return pl.pallas_call(kern, name="attn_bwd", grid_spec=gs, out_shape=out_shape,
                          compiler_params=_params(("arbitrary", "arbitrary")))(*operands)


def _bwd_inproj(dq, dk, dv, dd, dgl, ps, x, dh1, cc, sa, w, tm):
    n_tok = x.shape[0]
    n_tiles = n_tok // tm
    last_halo = n_tok // HALO - 1
    halo_spec = pl.BlockSpec((HALO, POOL_WIDTH), lambda i: (jnp.minimum((i + 1) * (tm // HALO), last_halo), 0))

    def body(i, tin, res, tout, acc, scr):
        dq_ref, dk_ref, dv_ref, dd_ref, ddh_ref, dgl_ref, ps_ref, x_ref, dh1_ref, c_ref, s_ref = tin
        w_uq, g_q, w_k, w_v, e_mat, g_kv, w_in, g_pre = res
        dqu_o, dproj_o, dx_o = tout
        dgq_a, dgkv_a, dgpre_a = acc
        ext, = scr
        cc_, sa_ = c_ref[...], s_ref[...]
        for h in range(N_HEADS):
            hs = slice(h * HEAD_SLOT, (h + 1) * HEAD_SLOT)
            dqu_o[:, hs] = (_unrope(dq_ref[:, hs].astype(F32), cc_, sa_) * SCALE).astype(dqu_o.dtype)
        gq = g_q[...]
        _, qh, rq = _rms(ps_ref[:, :Q_LORA].astype(F32), gq)
        dqd, dg = _rms_bwd(qh, rq, gq, _dot_nt(dqu_o[...], w_uq[...]))
        dgq_a[...] += dg
        dproj_o[:, :Q_LORA] = dqd.astype(dproj_o.dtype)
        gkv = g_kv[...]
        _, kh, rk = _rms(ps_ref[:, Q_LORA:Q_LORA + KV_LORA].astype(F32), gkv)
        dkvd, dg = _rms_bwd(kh, rk, gkv, _dot_nt(dk_ref[...], w_k[...]) + _dot_nt(dv_ref[...], w_v[...]))
        dgkv_a[...] += dg
        dproj_o[:, Q_LORA:Q_LORA + KV_LORA] = dkvd.astype(dproj_o.dtype)
        dproj_o[:, Q_LORA + KV_LORA:SMALL_COLS] = _unrope(_dot_nt(dk_ref[...], e_mat[...]), cc_, sa_).astype(dproj_o.dtype)
        row = i * tm + lax.broadcasted_iota(jnp.int32, (tm + HALO, 1), 0)
        for gi, wdw in enumerate(POOL_WINDOWS):
            cs = slice(gi * POOL_GROUP, (gi + 1) * POOL_GROUP)
            inv = 1.0 / jnp.minimum(row + 1, wdw).astype(F32)
            ext[pl.ds(0, tm), cs] = dd_ref[:, cs] * inv[:tm]
            ext[pl.ds(tm, HALO), cs] = jnp.where(i < n_tiles - 1, ddh_ref[:, cs] * inv[tm:], 0.0)
            s = ext[pl.ds(0, tm), cs]
            for k_ in range(1, wdw):
                s = s + ext[pl.ds(k_, tm), cs]
            dproj_o[:, SMALL_COLS + gi * POOL_GROUP:SMALL_COLS + (gi + 1) * POOL_GROUP] = (s - dd_ref[:, cs]).astype(dproj_o.dtype)
        dproj_o[:, SMALL_COLS + POOL_WIDTH:] = dgl_ref[...]
        da = _dot_nt(dproj_o[...], w_in[...])
        gp = g_pre[...]
        _, xh, rx = _rms(x_ref[...], gp)
        dx, dg = _rms_bwd(xh, rx, gp, da)
        dgpre_a[...] += dg
        dx_o[...] = dh1_ref[...] + dx

    outs = [_sds(n_tok, QK_WIDTH, MX), _sds(n_tok, IN_PAD, MX), _sds(n_tok, D_MODEL, F32)]
    accs = [_sds(1, Q_LORA, F32), _sds(1, KV_LORA, F32), _sds(1, D_MODEL, F32)]
    res = [w["w_uq"], w["g_q"], w["w_k"], w["w_v"], w["e_mat"], w["g_kv"], w["w_in"], w["g_pre_mix"]]
    return _tok_call("bwd_inproj", body, n_tok, tm, [dq, dk, dv, dd, (dd, halo_spec), dgl, ps, x, dh1, cc, sa], res, outs, accs,
                     scratch=[pltpu.VMEM((tm + HALO, POOL_WIDTH), F32)])


XTDY_TOKENS = 1024
XTDY_OUT_BYTES = 8 * 2**20


def _xtdy(name, x, dy):
    n_tok, kk = x.shape
    nn = dy.shape[1]
    bt = min(XTDY_TOKENS, n_tok)
    bk = kk
    while bk * nn * 4 > XTDY_OUT_BYTES and bk % 256 == 0:
        bk //= 2

    def kern(x_ref, dy_ref, o_ref):
        @pl.when(pl.program_id(1) == 0)
        def _():
            o_ref[...] = jnp.zeros(o_ref.shape, F32)
        o_ref[...] += _dot_tn(x_ref[...], dy_ref[...])

    return pl.pallas_call(
        kern, name=name, grid=(kk // bk, n_tok // bt),
        in_specs=[pl.BlockSpec((bt, bk), lambda a, t: (t, a)), pl.BlockSpec((bt, nn), lambda a, t: (t, 0))],
        out_specs=pl.BlockSpec((bk, nn), lambda a, t: (a, 0)), out_shape=_sds(kk, nn, F32),
        compiler_params=_params(("arbitrary", "arbitrary")))(x, dy)


def _rope_tables(positions):
    inv_freq = ROPE_THETA ** (-jnp.arange(0, QK_ROPE, 2, dtype=F32) / QK_ROPE)
    ang = positions.astype(F32)[:, None] * inv_freq
    cos, sin = jnp.cos(ang), jnp.sin(ang)
    n_tok = positions.shape[0]
    ones, z64 = jnp.ones((n_tok, ROPE_LANE), F32), jnp.zeros((n_tok, ROPE_LANE), F32)
    z32 = jnp.zeros((n_tok, HEAD_SLOT - ROPE_LANE - QK_ROPE), F32)
    return jnp.concatenate([ones, cos, cos, z32], 1), jnp.concatenate([z64, -sin, sin, z32], 1)


def _kernel_weights(full):
    w_in, w_uq, w_ukv = full["w_in"], full["w_uq"], full["w_ukv"]
    c0 = Q_LORA + KV_LORA
    z = lambda n: jnp.zeros((D_MODEL, n), w_in.dtype)
    w = dict(full)
    w["w_in"] = jnp.concatenate([w_in[:, :c0], z(ROPE_LANE), w_in[:, c0:c0 + QK_ROPE], z(HEAD_SLOT - ROPE_LANE - QK_ROPE),
                                 w_in[:, c0 + QK_ROPE:]], 1)
    w["w_uq"] = jnp.pad(w_uq.reshape(Q_LORA, N_HEADS, QK_NOPE + QK_ROPE),
                        ((0, 0), (0, 0), (0, HEAD_SLOT - QK_NOPE - QK_ROPE))).reshape(Q_LORA, QK_WIDTH)
    kv = w_ukv.reshape(KV_LORA, N_HEADS, QK_NOPE + V_HEAD)
    w["w_k"] = jnp.pad(kv[:, :, :QK_NOPE], ((0, 0), (0, 0), (0, HEAD_SLOT - QK_NOPE))).reshape(KV_LORA, QK_WIDTH)
    w["w_v"] = kv[:, :, QK_NOPE:].reshape(KV_LORA, N_HEADS * V_HEAD)
    e = np.zeros((HEAD_SLOT, QK_WIDTH), np.float32)
    sel = np.zeros((N_HEADS * V_HEAD, HEAD_SLOT), np.float32)
    for h in range(N_HEADS):
        for r in range(QK_ROPE):
            e[ROPE_LANE + r, h * HEAD_SLOT + ROPE_LANE + r] = 1.0
        sel[h * V_HEAD:(h + 1) * V_HEAD, h] = 1.0
    w["e_mat"] = jnp.asarray(e, MX)
    w["head_sel"] = jnp.asarray(sel, MX)
    w["w_pool"] = full["w_pool"].astype(MX)
    return w


def _local_step(x, p, positions, target, full, mesh_place=None, packed_rest=None):
    n_tok = x.shape[0]
    tm = min(512, n_tok)
    tm_mlp = min(256, n_tok)
    tk = min(ATTN_KEYS, n_tok)
    tq_fwd, tq_bwd = min(ATTN_QUERIES_FWD, n_tok), min(ATTN_QUERIES_BWD, n_tok)
    w = _kernel_weights(full)
    cc, sa = _rope_tables(positions)

    if mesh_place is None:
        a, ps, u, gl, qn, kvn, q, k, v = _fwd_inproj(x, cc, sa, w, tm)
    else:
        my_chip, core = mesh_place
        a, ps, u, gl, qn, kvn, q, k, v, gathered = _fwd_inproj(x, cc, sa, w, tm, gather=packed_rest)
        w.update(_unpack_full(lax.dynamic_update_slice(gathered, packed_rest[None], (my_chip, 0, 0)), REST))
    vt = jnp.concatenate([v.T.reshape(N_HEADS, V_HEAD, n_tok), jnp.ones((N_HEADS, 1, n_tok), v.dtype),
                          jnp.zeros((N_HEADS, V_ROWS - V_HEAD - 1, n_tok), v.dtype)], 1).reshape(N_HEADS * V_ROWS, n_tok)
    attn_t, lse = _attn_fwd(q, k, vt, tq_fwd, tk)
    attn = attn_t.T
    d, pooled, a_br, p_br, merged, y, h1 = _fwd_mix(x, u, gl, attn, w, tm)
    m, zr, a2, f, h2 = _fwd_mlp(h1, w, tm_mlp)
    dh2, de, dzg, loss_cols, dg_ple = _ple_fwd_bwd(h2, p, target, w, tm)
    df, dz, dh1, dg_post_mlp, dg_pre_mlp = _bwd_mlp(dh2, f, h1, zr, w, tm_mlp)
    dy, da_br, dp_br, dgl, do, delta, dyp, dd, dg_post_mix, db_gate, dpool_scale = _bwd_mix(dh1, y, a_br, p_br, gl, attn, d, w, tm)
    grads = {"w_branch_attn": _xtdy("dw_ba", attn, da_br), "w_branch_pool": _xtdy("dw_bp", pooled, dp_br),
             "w_out": _xtdy("dw_out", merged, dy), "w_ff1": _xtdy("dw_ff1", m, dz), "w_ff2": _xtdy("dw_ff2", a2, df),
             "w_ple_proj": _xtdy("dw_pe", p, de), "w_ple_gate": _xtdy("dw_pg", h2, dzg)}
    delta_t = delta[:, :N_HEADS].T
    if mesh_place is None:
        travelling = None
        dq_t, dk, dv = _attn_bwd(q, k, k.T, v, do, lse, delta_t, tq_bwd, tk)
    else:
        pieces = _pack_pieces(grads, REST)
        sent = _add_halves(pieces, _exchange_halves(pieces), core)
        dq_t, dk, dv, received = _attn_bwd(q, k, k.T, v, do, lse, delta_t, tq_bwd, tk, scatter=sent)
        travelling = (sent, received)
        grads = {}
    dq = dq_t.T.astype(MX)
    dqu, dproj, dx, dg_q, dg_kv, dg_pre_mix = _bwd_inproj(dq, dk, dv, dd, dgl, ps, x, dh1, cc, sa, w, tm)

    g_in = _xtdy("dw_in", a, dproj)
    g_uq = _xtdy("dw_uq", qn, dqu)
    g_k = _xtdy("dw_k", kvn, dk)
    g_v = _xtdy("dw_v", kvn, dv)
    g_pool = _xtdy("dw_pool", d, dyp)

    c0 = Q_LORA + KV_LORA
    grads.update({
        "g_pre_mix": dg_pre_mix,
        "w_in": jnp.concatenate([g_in[:, :c0], g_in[:, c0 + ROPE_LANE:c0 + ROPE_LANE + QK_ROPE], g_in[:, SMALL_COLS:]], 1),
        "b_gate": db_gate,
        "g_q": dg_q,
        "w_uq": g_uq.reshape(Q_LORA, N_HEADS, HEAD_SLOT)[:, :, :QK_NOPE + QK_ROPE].reshape(Q_LORA, N_HEADS * (QK_NOPE + QK_ROPE)),
        "g_kv": dg_kv,
        "w_ukv": jnp.concatenate([g_k.reshape(KV_LORA, N_HEADS, HEAD_SLOT)[:, :, :QK_NOPE],
                                  g_v.reshape(KV_LORA, N_HEADS, V_HEAD)], 2).reshape(KV_LORA, N_HEADS * (QK_NOPE + V_HEAD)),
        "w_pool": jnp.stack([g_pool[g * POOL_GROUP:(g + 1) * POOL_GROUP, g * POOL_GROUP:(g + 1) * POOL_GROUP]
                             for g in range(len(POOL_WINDOWS))]),
        "pool_scale": dpool_scale,
        "g_post_mix": dg_post_mix,
        "g_pre_mlp": dg_pre_mlp,
        "g_post_mlp": dg_post_mlp,
        "g_ple": dg_ple,
    })
    return loss_cols, dx, grads, travelling


def _place():
    return lax.axis_index("x"), lax.axis_index("y"), lax.axis_index("c")


CHIP_FLIPS = ((1, 0), (0, 1), (1, 1))


def _flip(x, y, fx, fy):
    return (1 - x if fx else x), (1 - y if fy else y)


_HBM = pl.BlockSpec(memory_space=pl.ANY)


def _gather_copies(w_ref, out_ref, send_sems, recv_sems):
    half = w_ref.shape[0] // 2
    x, y, c = _place()
    my_chip = 2 * x + y
    sibling = (x, y, 1 - c)

    def half_of(chip, hc):
        return out_ref.at[chip, pl.ds(pl.multiple_of(hc * half, 16), half), :]

    src = w_ref.at[pl.ds(pl.multiple_of(c * half, 16), half), :]
    sends, landed, forwards, from_sibling = [], [], [], []
    for j, (fx, fy) in enumerate(CHIP_FLIPS):
        px, py = _flip(x, y, fx, fy)
        mine_there, theirs_here, theirs_other = half_of(my_chip, c), half_of(2 * px + py, c), half_of(2 * px + py, 1 - c)
        sends.append(pltpu.make_async_remote_copy(src, mine_there, send_sems.at[j], recv_sems.at[j],
                                                  device_id=(px, py, c), device_id_type=MESH))
        landed.append(pltpu.make_async_remote_copy(src, theirs_here, send_sems.at[j], recv_sems.at[j],
                                                   device_id=(px, py, c), device_id_type=MESH))
        forwards.append(pltpu.make_async_remote_copy(theirs_here, theirs_here, send_sems.at[3 + j], recv_sems.at[3 + j],
                                                     device_id=sibling, device_id_type=MESH))
        from_sibling.append(pltpu.make_async_remote_copy(theirs_other, theirs_other, send_sems.at[3 + j],
                                                         recv_sems.at[3 + j], device_id=sibling, device_id_type=MESH))
    return sends, landed, forwards, from_sibling


def _gather_steps(i, n_steps, w_ref, out_ref, sems):
    sends, landed, forwards, from_sibling = _gather_copies(w_ref, out_ref, *sems)

    @pl.when(i == 0)
    def _():
        for cp in sends:
            cp.start()

    @pl.when(i == (3 * n_steps) // 4)
    def _():
        for arrived, fwd in zip(landed, forwards):
            arrived.wait_recv()
            fwd.start()

    @pl.when(i == n_steps - 1)
    def _():
        for cp in from_sibling:
            cp.wait_recv()
        for cp in sends + forwards:
            cp.wait_send()


def _allgather_shards(wp):
    def body(w_ref, out_ref, send_sems, recv_sems):
        sends, landed, forwards, from_sibling = _gather_copies(w_ref, out_ref, send_sems, recv_sems)
        for cp in sends:
            cp.start()
        for arrived, fwd in zip(landed, forwards):
            arrived.wait_recv()
            fwd.start()
        for cp in from_sibling:
            cp.wait_recv()
        for cp in sends + forwards:
            cp.wait_send()

    return pl.pallas_call(
        body, name="allgather_shards", out_shape=jax.ShapeDtypeStruct((N_CHIPS,) + wp.shape, wp.dtype),
        in_specs=[_HBM], out_specs=_HBM,
        scratch_shapes=[pltpu.SemaphoreType.DMA((6,)), pltpu.SemaphoreType.DMA((6,))],
    )(wp)


def _exchange_halves(g):
    rows = g.shape[1]
    half = rows // 2

    def body(g_ref, r_ref, send_sem, recv_sem):
        x, y, c = _place()
        src = g_ref.at[:, pl.ds(pl.multiple_of((1 - c) * half, 8), half), :]
        cp = pltpu.make_async_remote_copy(src, r_ref, send_sem, recv_sem, device_id=(x, y, 1 - c), device_id_type=MESH)
        cp.start()
        cp.wait()

    return pl.pallas_call(
        body, name="exchange_halves", out_shape=jax.ShapeDtypeStruct((N_CHIPS, half, PACK_COLS), g.dtype),
        in_specs=[_HBM], out_specs=_HBM, scratch_shapes=[pltpu.SemaphoreType.DMA, pltpu.SemaphoreType.DMA],
    )(g)


def _add_halves(g, r, c):
    rows = g.shape[1]
    half = rows // 2
    br = REDUCE_ROWS
    nb = half // br

    def kern(c_ref, g_ref, r_ref, o_ref):
        o_ref[...] = (g_ref[...] + r_ref[...]).astype(o_ref.dtype)

    gs = pltpu.PrefetchScalarGridSpec(
        num_scalar_prefetch=1, grid=(N_CHIPS, nb),
        in_specs=[pl.BlockSpec((1, br, PACK_COLS), lambda k, t, c: (k, c[0] * nb + t, 0)),
                  pl.BlockSpec((1, br, PACK_COLS), lambda k, t, c: (k, t, 0))],
        out_specs=pl.BlockSpec((1, br, PACK_COLS), lambda k, t, c: (k, t, 0)))
    return pl.pallas_call(kern, name="add_halves", grid_spec=gs,
                          out_shape=jax.ShapeDtypeStruct((N_CHIPS, half, PACK_COLS), WIRE),
                          compiler_params=_params(("arbitrary", "arbitrary")))(c.reshape(1), g, r)


def _scatter_copies(s_ref, r_ref, send_sems, recv_sems):
    x, y, c = _place()
    my_chip = 2 * x + y
    sends, arrivals = [], []
    for j, (fx, fy) in enumerate(CHIP_FLIPS):
        px, py = _flip(x, y, fx, fy)
        slot = r_ref.at[2 * px + py]
        sends.append(pltpu.make_async_remote_copy(s_ref.at[2 * px + py], r_ref.at[my_chip], send_sems.at[j], recv_sems.at[j],
                                                  device_id=(px, py, c), device_id_type=MESH))
        arrivals.append(pltpu.make_async_remote_copy(slot, slot, send_sems.at[j], recv_sems.at[j],
                                                     device_id=(px, py, c), device_id_type=MESH))
    return sends, arrivals


def _scatter_steps(first, last, s_ref, r_ref, sems):
    sends, arrivals = _scatter_copies(s_ref, r_ref, *sems)

    @pl.when(first)
    def _():
        for cp in sends:
            cp.start()

    @pl.when(last)
    def _():
        for cp in arrivals:
            cp.wait_recv()
        for cp in sends:
            cp.wait_send()


def _scatter_pieces(s):
    def body(s_ref, r_ref, send_sems, recv_sems):
        sends, arrivals = _scatter_copies(s_ref, r_ref, send_sems, recv_sems)
        for cp in sends:
            cp.start()
        for cp in arrivals:
            cp.wait_recv()
        for cp in sends:
            cp.wait_send()

    return pl.pallas_call(
        body, name="scatter_pieces", out_shape=jax.ShapeDtypeStruct(s.shape, s.dtype), in_specs=[_HBM], out_specs=_HBM,
        scratch_shapes=[pltpu.SemaphoreType.DMA((3,)), pltpu.SemaphoreType.DMA((3,))],
    )(s)


def _sum_pieces(r):
    half = r.shape[1]
    br = REDUCE_ROWS

    def kern(r_ref, o_ref):
        o_ref[...] = ((r_ref[0].astype(F32) + r_ref[1].astype(F32)) + r_ref[2].astype(F32)) + r_ref[3].astype(F32)

    return pl.pallas_call(
        kern, name="sum_pieces", grid=(half // br,), in_specs=[pl.BlockSpec((N_CHIPS, br, PACK_COLS), lambda t: (0, t, 0))],
        out_specs=pl.BlockSpec((br, PACK_COLS), lambda t: (t, 0)), out_shape=_sds(half, PACK_COLS, F32),
        compiler_params=_params(("arbitrary",)))(r)


def _join_halves(f):
    def body(f_ref, o_ref, send_sem, recv_sem):
        x, y, c = _place()
        cp = pltpu.make_async_remote_copy(f_ref, o_ref, send_sem, recv_sem, device_id=(x, y, 1 - c), device_id_type=MESH)
        cp.start()
        cp.wait()

    return pl.pallas_call(
        body, name="join_halves", out_shape=jax.ShapeDtypeStruct(f.shape, f.dtype), in_specs=[_HBM], out_specs=_HBM,
        scratch_shapes=[pltpu.SemaphoreType.DMA, pltpu.SemaphoreType.DMA],
    )(f)


def _allreduce_small(g):
    n_dev = 8

    def body(g_ref, o_ref, buf, send_sems, recv_sems):
        x, y, c = _place()
        me = 4 * x + 2 * y + c
        buf[me] = g_ref[...]
        peers = []
        for f in range(1, n_dev):
            px, py = _flip(x, y, f & 4, f & 2)
            pc = 1 - c if f & 1 else c
            peers.append((px, py, pc))
        sent = []
        for f, peer in enumerate(peers):
            cp = pltpu.make_async_remote_copy(g_ref, buf.at[me], send_sems.at[f], recv_sems.at[f], device_id=peer,
                                              device_id_type=MESH)
            cp.start()
            sent.append(cp)
        for f, (px, py, pc) in enumerate(peers):
            slot = buf.at[4 * px + 2 * py + pc]
            pltpu.make_async_remote_copy(slot, slot, send_sems.at[f], recv_sems.at[f], device_id=(px, py, pc),
                                         device_id_type=MESH).wait_recv()
        for cp in sent:
            cp.wait_send()
        total = buf[0]
        for k in range(1, n_dev):
            total = total + buf[k]
        o_ref[...] = total

    vmem = pl.BlockSpec(memory_space=pltpu.VMEM)
    return pl.pallas_call(
        body, name="allreduce_small", out_shape=jax.ShapeDtypeStruct(g.shape, g.dtype), in_specs=[vmem], out_specs=vmem,
        scratch_shapes=[pltpu.VMEM((n_dev,) + g.shape, g.dtype), pltpu.SemaphoreType.DMA((n_dev - 1,)),
                        pltpu.SemaphoreType.DMA((n_dev - 1,))],
    )(g)


def _adamw_update(g_ref, w_ref, m_ref, v_ref, d_o, m_o, v_o):
    c1 = 1.0 - ADAM_B1 ** ADAM_STEP
    c2 = 1.0 - ADAM_B2 ** ADAM_STEP
    g_ = g_ref[...]
    m_new = ADAM_B1 * m_ref[...] + (1.0 - ADAM_B1) * g_
    v_new = ADAM_B2 * v_ref[...] + (1.0 - ADAM_B2) * (g_ * g_)
    m_o[...] = m_new
    v_o[...] = v_new
    d_o[...] = -ADAM_LR * ((m_new / c1) / (jnp.sqrt(v_new / c2) + ADAM_EPS) + ADAM_WD * w_ref[...])


ADAMW_ROWS = 256


def _adamw(name, g, w, m, v):
    _, rows, cols = w.shape
    br = int(np.gcd(ADAMW_ROWS, rows))

    def kern(*refs):
        _adamw_update(*refs)

    spec = pl.BlockSpec((1, br, cols), lambda t: (0, t, 0))
    out = jax.ShapeDtypeStruct(w.shape, F32)
    return pl.pallas_call(kern, name="adamw_" + name, grid=(rows // br,), in_specs=[spec] * 4, out_specs=[spec] * 3,
                          out_shape=[out, out, out], compiler_params=_params(("arbitrary",)))(g, w, m, v)


def _adamw_small(gs, ws, ms, vs):
    n = len(gs)

    def kern(*refs):
        ins, outs = refs[:4 * n], refs[4 * n:]
        for k in range(n):
            _adamw_update(ins[k], ins[n + k], ins[2 * n + k], ins[3 * n + k], outs[k], outs[n + k], outs[2 * n + k])

    vmem = pl.BlockSpec(memory_space=pltpu.VMEM)
    out = [jax.ShapeDtypeStruct(w.shape, F32) for w in ws]
    res = pl.pallas_call(kern, name="adamw_small", in_specs=[vmem] * (4 * n), out_specs=[vmem] * (3 * n),
                         out_shape=out * 3, compiler_params=pltpu.CompilerParams(vmem_limit_bytes=VMEM_LIMIT))(*gs, *ws, *ms, *vs)
    return [(res[k], res[n + k], res[2 * n + k]) for k in range(n)]


def _shard_rows(shape, axis):
    k, n = shape
    return (k * n // N_CHIPS) // PACK_COLS


def _group(names):
    entries = [e for e in SHARDED if e[0] in names]
    used = sum(_shard_rows(shape, axis) for _, shape, axis in entries)
    return entries, -(-used // (2 * REDUCE_ROWS)) * 2 * REDUCE_ROWS


def _pack_shards(shards, names, dtype):
    entries, rows = _group(names)
    parts = [shards[name].astype(dtype).reshape(-1, PACK_COLS) for name, _, _ in entries]
    used = sum(p.shape[0] for p in parts)
    if rows > used:
        parts.append(jnp.zeros((rows - used, PACK_COLS), dtype))
    return jnp.concatenate(parts, 0)


def _unpack_shards(packed, names):
    out, r0 = {}, 0
    for name, (k, n), axis in _group(names)[0]:
        nr = _shard_rows((k, n), axis)
        shape = (k // N_CHIPS, n) if axis == 0 else (k, n // N_CHIPS)
        out[name] = packed[r0:r0 + nr].reshape(shape)
        r0 += nr
    return out


def _unpack_full(gathered, names):
    out, r0 = {}, 0
    for name, (k, n), axis in _group(names)[0]:
        nr = _shard_rows((k, n), axis)
        part = gathered[:, r0:r0 + nr]
        if axis == 0:
            out[name] = part.reshape(k, n)
        else:
            out[name] = part.reshape(N_CHIPS, k, n // N_CHIPS).transpose(1, 0, 2).reshape(k, n)
        r0 += nr
    return out


def _pack_pieces(grads, names):
    entries, rows = _group(names)
    parts = []
    for name, (k, n), axis in entries:
        g = grads[name]
        if axis == 0:
            parts.append(g.reshape(N_CHIPS, -1, PACK_COLS))
        else:
            parts.append(g.reshape(k, N_CHIPS, n // N_CHIPS).transpose(1, 0, 2).reshape(N_CHIPS, -1, PACK_COLS))
    used = sum(p.shape[1] for p in parts)
    if rows > used:
        parts.append(jnp.zeros((N_CHIPS, rows - used, PACK_COLS), F32))
    return jnp.concatenate(parts, 1)


def _pack_small(vals):
    flat = jnp.concatenate([vals[name].astype(F32).reshape(-1) for name, _ in SMALL])
    flat = jnp.concatenate([flat, jnp.zeros((SMALL_ROWS * PACK_COLS - flat.shape[0],), F32)])
    return flat.reshape(SMALL_ROWS, PACK_COLS)


def _unpack_small(packed):
    flat, out, o = packed.reshape(-1), {}, 0
    for name, shape in SMALL:
        n = int(np.prod(shape))
        out[name] = flat[o:o + n].reshape(shape)
        o += n
    return out


def kernel(x, p, positions, g_pre_mix, w_in, b_gate, g_q, w_uq, g_kv, w_ukv, w_pool, pool_scale, w_branch_attn, w_branch_pool, w_out, g_post_mix, g_pre_mlp, w_ff1, w_ff2, g_post_mlp, w_ple_proj, w_ple_gate, g_ple, loss_target, m_g_pre_mix, m_w_in, m_b_gate, m_g_q, m_w_uq, m_g_kv, m_w_ukv, m_w_pool, m_pool_scale, m_w_branch_attn, m_w_branch_pool, m_w_out, m_g_post_mix, m_g_pre_mlp, m_w_ff1, m_w_ff2, m_g_post_mlp, m_w_ple_proj, m_w_ple_gate, m_g_ple, v_g_pre_mix, v_w_in, v_b_gate, v_g_q, v_w_uq, v_g_kv, v_w_ukv, v_w_pool, v_pool_scale, v_w_branch_attn, v_w_branch_pool, v_w_out, v_g_post_mix, v_g_pre_mlp, v_w_ff1, v_w_ff2, v_g_post_mlp, v_w_ple_proj, v_w_ple_gate, v_g_ple):
    given = dict(locals())
    weights = {n: given[n] for n in WEIGHT_ORDER}
    moments_m = {n: given["m_" + n] for n in WEIGHT_ORDER}
    moments_v = {n: given["v_" + n] for n in WEIGHT_ORDER}
    c = lax.axis_index("c")

    big_w = {name: weights[name][0] for name, _, _ in SHARDED}
    my_chip = 2 * lax.axis_index("x") + lax.axis_index("y")
    packed_first = _pack_shards(big_w, FIRST, MX)
    full = _unpack_full(lax.dynamic_update_slice(_allgather_shards(packed_first), packed_first[None], (my_chip, 0, 0)), FIRST)
    for name, _ in SMALL:
        full[name] = weights[name][0] if name == "w_pool" else weights[name]

    loss_cols, dx, grads, (sent_rest, received_rest) = _local_step(
        x[0], p[0, 0], positions[0], loss_target[0], full, (my_chip, c), _pack_shards(big_w, REST, MX))
    loss = lax.psum(0.5 * jnp.sum(loss_cols) / D_MODEL, ("x", "y", "c"))

    def finish(sent, received):
        mine = lax.dynamic_slice(sent, (my_chip, 0, 0), (1,) + sent.shape[1:])
        reduced = _sum_pieces(lax.dynamic_update_slice(received, mine, (my_chip, 0, 0)))
        theirs = _join_halves(reduced)
        return jnp.where(c == 0, jnp.concatenate([reduced, theirs]), jnp.concatenate([theirs, reduced]))

    pieces = _pack_pieces(grads, FIRST)
    sent_first = _add_halves(pieces, _exchange_halves(pieces), c)
    shards = _unpack_shards(finish(sent_first, _scatter_pieces(sent_first)), FIRST)
    shards.update(_unpack_shards(finish(sent_rest, received_rest), REST))
    g_small = _allreduce_small(_pack_small(grads))

    out = {}
    for name, g in shards.items():
        out[name] = (g[None], *_adamw(name, g[None], weights[name], moments_m[name], moments_v[name]))
    small_g = _unpack_small(g_small)
    names = [n for n, _ in SMALL]
    updates = _adamw_small([small_g[n] for n in names], [weights[n] for n in names], [moments_m[n] for n in names],
                           [moments_v[n] for n in names])
    for n, upd in zip(names, updates):
        out[n] = (small_g[n], *upd)
    return (loss, dx[None], *[out[n][k] for k in range(4) for n in WEIGHT_ORDER])
```

```python
import functools

import numpy as np
import jax
import jax.numpy as jnp
from jax import lax
from jax.experimental import pallas as pl
from jax.experimental.pallas import tpu as pltpu

F32 = jnp.float32
MX = jnp.bfloat16
WIRE = jnp.bfloat16

D_MODEL = 1024
N_HEADS = 8
QK_NOPE = 64
QK_ROPE = 32
V_HEAD = 64
Q_LORA = 384
KV_LORA = 256
POOL_WINDOWS = (2, 4, 8, 16)
POOL_GROUP = 128
POOL_WIDTH = 512
D_FF = 4096
PLE_DIM = 256
ROPE_THETA = 10000.0
EPS = 1e-6
HEAD_SLOT = 128
QK_WIDTH = N_HEADS * HEAD_SLOT
ROPE_LANE = 64
SMALL_COLS = Q_LORA + KV_LORA + HEAD_SLOT
IN_PAD = SMALL_COLS + POOL_WIDTH + 2 * D_MODEL
SCALE = (QK_NOPE + QK_ROPE) ** -0.5
LOG2E = 1.4426950408889634
NEG = -1e30
HALO = 16

ADAM_LR = 0.001
ADAM_B1 = 0.9
ADAM_B2 = 0.999
ADAM_EPS = 1e-08
ADAM_WD = 0.01
ADAM_STEP = 10

VMEM_LIMIT = 56 * 2**20
MESH = pl.DeviceIdType.MESH

SHARDED = (
    ("w_in", (1024, 3232), 1),
    ("w_uq", (384, 768), 1),
    ("w_ukv", (256, 1024), 1),
    ("w_branch_attn", (512, 1024), 1),
    ("w_branch_pool", (512, 1024), 1),
    ("w_out", (1024, 1024), 0),
    ("w_ff1", (1024, 4096), 1),
    ("w_ff2", (4096, 1024), 0),
    ("w_ple_proj", (256, 1024), 1),
    ("w_ple_gate", (1024, 1024), 0),
)
SMALL = (
    ("g_pre_mix", (1, 1024)),
    ("b_gate", (1, 2048)),
    ("g_q", (1, 384)),
    ("g_kv", (1, 256)),
    ("w_pool", (1, 4, 128, 128)),
    ("pool_scale", (1, 512)),
    ("g_post_mix", (1, 1024)),
    ("g_pre_mlp", (1, 1024)),
    ("g_post_mlp", (1, 1024)),
    ("g_ple", (1, 1024)),
)
WEIGHT_ORDER = ("g_pre_mix", "w_in", "b_gate", "g_q", "w_uq", "g_kv", "w_ukv", "w_pool", "pool_scale", "w_branch_attn",
                "w_branch_pool", "w_out", "g_post_mix", "g_pre_mlp", "w_ff1", "w_ff2", "g_post_mlp", "w_ple_proj",
                "w_ple_gate", "g_ple")
N_CHIPS = 4
PACK_COLS = 1024
REDUCE_ROWS = 160
SMALL_ROWS = 80
FIRST = ("w_in", "w_uq", "w_ukv")
REST = tuple(name for name, _, _ in SHARDED if name not in FIRST)


def _dot(a, b):
    return jnp.dot(a.astype(MX), b.astype(MX), preferred_element_type=F32)


def _dot_nt(a, b):
    return lax.dot_general(a.astype(MX), b.astype(MX), (((1,), (1,)), ((), ())), preferred_element_type=F32)


def _dot_tn(a, b):
    return lax.dot_general(a.astype(MX), b.astype(MX), (((0,), (0,)), ((), ())), preferred_element_type=F32)


def _sig(x):
    return 1.0 / (1.0 + jnp.exp(-x))


def _rms(x, g):
    r = lax.rsqrt(jnp.mean(x * x, axis=1, keepdims=True) + EPS)
    xh = x * r
    return xh * g, xh, r


def _rms_bwd(xh, r, g, dy):
    dxn = dy * g
    dx = r * (dxn - xh * jnp.mean(dxn * xh, axis=1, keepdims=True))
    return dx, jnp.sum(dy * xh, axis=0, keepdims=True)


def _rot_half(v):
    lane = lax.broadcasted_iota(jnp.int32, v.shape, 1)
    return jnp.where(lane < ROPE_LANE + QK_ROPE // 2, pltpu.roll(v, HEAD_SLOT - QK_ROPE // 2, 1), pltpu.roll(v, QK_ROPE // 2, 1))


def _rope(v, cc, sa):
    return v * cc + _rot_half(v) * sa


def _unrope(v, cc, sa):
    return v * cc - _rot_half(v) * sa


def _params(sem):
    return pltpu.CompilerParams(dimension_semantics=sem, vmem_limit_bytes=VMEM_LIMIT)


def _tok_call(name, body, n_tok, tm, tiled, resident, outs, accs=(), scratch=(), exchange=None):
    def as_pair(t):
        if isinstance(t, tuple):
            return t
        return t, pl.BlockSpec((tm, t.shape[1]), lambda i: (i, 0))
    tiled = [as_pair(t) for t in tiled]
    outs = [as_pair(o) for o in outs]
    res_specs = [pl.BlockSpec(r.shape, lambda i, nd=r.ndim: (0,) * nd, pipeline_mode=pl.Buffered(1)) for r in resident]
    out_specs = [s for _, s in outs] + [pl.BlockSpec(a.shape, lambda i: (0, 0)) for a in accs]
    n_t, n_r, n_o, n_a, n_s = len(tiled), len(resident), len(outs), len(accs), len(scratch)
    n_steps = n_tok // tm
    operands = [a for a, _ in tiled] + list(resident)
    in_specs = [s for _, s in tiled] + res_specs
    out_shape = [o for o, _ in outs] + list(accs)
    scratch = list(scratch)
    if exchange is not None:
        ex_in, ex_out, ex_sems, ex_steps = exchange
        operands.append(ex_in)
        in_specs.append(_HBM)
        out_shape.append(ex_out)
        out_specs.append(_HBM)
        scratch += list(ex_sems)

    def kern(*refs):
        refs = list(refs)
        n_in = n_t + n_r + (exchange is not None)
        n_out = n_o + n_a + (exchange is not None)
        tin, res = refs[:n_t], refs[n_t:n_t + n_r]
        tout = refs[n_in:n_in + n_o]
        acc = refs[n_in + n_o:n_in + n_o + n_a]
        scr = refs[n_in + n_out:n_in + n_out + n_s]
        i = pl.program_id(0)
        if exchange is not None:
            ex_steps(i, n_steps, refs[n_in - 1], refs[n_in + n_out - 1], refs[n_in + n_out + n_s:])

        @pl.when(i == 0)
        def _():
            for a in acc:
                a[...] = jnp.zeros(a.shape, a.dtype)
        body(i, tin, res, tout, acc, scr)

    return pl.pallas_call(
        kern, name=name, grid=(n_steps,), in_specs=in_specs, out_specs=out_specs,
        out_shape=out_shape, scratch_shapes=scratch, compiler_params=_params(("arbitrary",)),
    )(*operands)


def _sds(rows, cols, dtype):
    return jax.ShapeDtypeStruct((rows, cols), dtype)


def _fwd_inproj(x, cc, sa, w, tm, gather=None):
    n_tok = x.shape[0]

    def body(i, tin, res, tout, acc, scr):
        x_ref, c_ref, s_ref = tin
        g_pre, w_in, g_q, w_uq, g_kv, w_k, w_v, e_mat, w_kt, e_t, w_vt, v_ones = res
        a_o, ps_o, u_o, gl_o, qn_o, kvn_o, q_o, k_o, v_o, kt_o, vt_o = tout
        a = _rms(x_ref[...], g_pre[...])[0].astype(MX)
        a_o[...] = a
        ps = _dot(a, w_in[:, :SMALL_COLS])
        ps_o[...] = ps.astype(ps_o.dtype)
        u_o[...] = _dot(a, w_in[:, SMALL_COLS:SMALL_COLS + POOL_WIDTH]).astype(u_o.dtype)
        gl_o[...] = _dot(a, w_in[:, SMALL_COLS + POOL_WIDTH:]).astype(gl_o.dtype)
        cc_, sa_ = c_ref[...], s_ref[...]
        qn = _rms(ps[:, :Q_LORA], g_q[...])[0].astype(MX)
        qn_o[...] = qn
        q = _dot(qn, w_uq[...])
        for h in range(N_HEADS):
            hs = slice(h * HEAD_SLOT, (h + 1) * HEAD_SLOT)
            q_o[:, hs] = (_rope(q[:, hs], cc_, sa_) * (SCALE * LOG2E)).astype(q_o.dtype)
        kvn = _rms(ps[:, Q_LORA:Q_LORA + KV_LORA], g_kv[...])[0].astype(MX)
        kvn_o[...] = kvn
        kr = _rope(ps[:, Q_LORA + KV_LORA:], cc_, sa_)
        k_o[...] = (_dot(kvn, w_k[...]) + _dot(kr, e_mat[...])).astype(k_o.dtype)
        v_o[...] = _dot(kvn, w_v[...]).astype(v_o.dtype)
        kt_o[...] = (_dot_nt(w_kt[...], kvn) + _dot_nt(e_t[...], kr)).astype(kt_o.dtype)
        vt_o[...] = (_dot_nt(w_vt[...], kvn) + v_ones[...]).astype(vt_o.dtype)

    outs = [_sds(n_tok, D_MODEL, MX), _sds(n_tok, SMALL_COLS, MX), _sds(n_tok, POOL_WIDTH, MX), _sds(n_tok, 2 * D_MODEL, MX),
            _sds(n_tok, Q_LORA, MX), _sds(n_tok, KV_LORA, MX), _sds(n_tok, QK_WIDTH, MX), _sds(n_tok, QK_WIDTH, MX),
            _sds(n_tok, N_HEADS * V_HEAD, MX),
            (_sds(QK_WIDTH, n_tok, MX), pl.BlockSpec((QK_WIDTH, tm), lambda i: (0, i))),
            (_sds(N_HEADS * V_ROWS, n_tok, MX), pl.BlockSpec((N_HEADS * V_ROWS, tm), lambda i: (0, i)))]
    res = [w["g_pre_mix"], w["w_in"], w["g_q"], w["w_uq"], w["g_kv"], w["w_k"], w["w_v"], w["e_mat"], w["w_kt"], w["e_t"],
           w["w_vt"], w["v_ones"]]
    exchange = None
    if gather is not None:
        gathered = jax.ShapeDtypeStruct((N_CHIPS,) + gather.shape, gather.dtype)
        exchange = (gather, gathered, [pltpu.SemaphoreType.DMA((6,)), pltpu.SemaphoreType.DMA((6,))], _gather_steps)
    return _tok_call("fwd_inproj", body, n_tok, tm, [x, cc, sa], res, outs, exchange=exchange)


def _causal_pairs(nq, ratio, by_kv):
    if by_kv:
        pairs = [(i, j) for j in range(nq * ratio) for i in range(j // ratio, nq)]
    else:
        pairs = [(i, j) for i in range(nq) for j in range((i + 1) * ratio)]
    return (jnp.asarray(np.array([p[0] for p in pairs], np.int32)), jnp.asarray(np.array([p[1] for p in pairs], np.int32)))


def _keep_t(tk, tq, off):
    return lax.broadcasted_iota(jnp.int32, (tk, tq), 0) + off <= lax.broadcasted_iota(jnp.int32, (tk, tq), 1)


ATTN_KEYS = 512
ATTN_QUERIES_FWD = 1024
ATTN_QUERIES_BWD = 1024
V_ROWS = 80


def _attn_fwd(q, k, vt, tq, tk):
    n_tok = q.shape[0]
    nq, ratio = n_tok // tq, tq // tk
    qi, kj = _causal_pairs(nq, ratio, by_kv=False)

    def kern(qi_ref, kj_ref, q_ref, k_ref, vt_ref, o_ref, lse_ref, m_s, acc_s, st_s):
        s_id = pl.program_id(0)
        i, j = qi_ref[s_id], kj_ref[s_id]

        @pl.when(j == 0)
        def _():
            m_s[...] = jnp.full(m_s.shape, NEG, F32)
            acc_s[...] = jnp.zeros(acc_s.shape, F32)

        def scores(h):
            hs = slice(h * HEAD_SLOT, (h + 1) * HEAD_SLOT)
            return _dot_nt(k_ref[:, hs], q_ref[:, hs])

        def heads(masked):
            keep = _keep_t(tk, tq, j * tk - i * tq) if masked else None
            st_s[0] = scores(0)
            for h in range(N_HEADS):
                if h + 1 < N_HEADS:
                    st_s[(h + 1) % 2] = scores(h + 1)
                st = st_s[h % 2]
                if masked:
                    st = jnp.where(keep, st, NEG)
                m_old = m_s[h]
                m_new = jnp.maximum(m_old, jnp.max(st, axis=0, keepdims=True))
                pt = jnp.exp2(st - m_new)
                acc_s[h] = jnp.exp2(m_old - m_new) * acc_s[h] + _dot(vt_ref[h * V_ROWS:(h + 1) * V_ROWS, :], pt)
                m_s[h] = m_new

        @pl.when(j < i * ratio)
        def _():
            heads(False)

        @pl.when(j >= i * ratio)
        def _():
            heads(True)

        @pl.when(j == (i + 1) * ratio - 1)
        def _():
            heads_out = []
            for h in range(N_HEADS):
                total = acc_s[h, V_HEAD:V_HEAD + 1, :]
                heads_out.append(acc_s[h, :V_HEAD, :] / total)
                lse_ref[h:h + 1, :] = m_s[h] + jnp.log2(total)
            o_ref[...] = jnp.concatenate(heads_out, 0).T.astype(o_ref.dtype)

    gs = pltpu.PrefetchScalarGridSpec(
        num_scalar_prefetch=2, grid=(qi.shape[0],),
        in_specs=[pl.BlockSpec((tq, QK_WIDTH), lambda s, qi, kj: (qi[s], 0)),
                  pl.BlockSpec((tk, QK_WIDTH), lambda s, qi, kj: (kj[s], 0)),
                  pl.BlockSpec((N_HEADS * V_ROWS, tk), lambda s, qi, kj: (0, kj[s]))],
        out_specs=[pl.BlockSpec((tq, N_HEADS * V_HEAD), lambda s, qi, kj: (qi[s], 0)),
                   pl.BlockSpec((N_HEADS, tq), lambda s, qi, kj: (0, qi[s]))],
        scratch_shapes=[pltpu.VMEM((N_HEADS, 1, tq), F32), pltpu.VMEM((N_HEADS, V_ROWS, tq), F32),
                        pltpu.VMEM((2, tk, tq), F32)])
    return pl.pallas_call(kern, name="attn_fwd", grid_spec=gs,
                          out_shape=[_sds(n_tok, N_HEADS * V_HEAD, MX), _sds(N_HEADS, n_tok, F32)],
                          compiler_params=_params(("arbitrary",)))(qi, kj, q, k, vt)


def _pool_windows(ext, i, tm, first_row):
    row = i * tm + lax.broadcasted_iota(jnp.int32, (tm, 1), 0)
    out = []
    for g, w in enumerate(POOL_WINDOWS):
        cs = slice(g * POOL_GROUP, (g + 1) * POOL_GROUP)
        s = ext[pl.ds(first_row, tm), cs]
        for k in range(1, w):
            s = s + ext[pl.ds(first_row - k, tm), cs]
        cnt = jnp.minimum(row + 1, w).astype(F32)
        out.append(s / cnt)
    return out


def _fwd_mix(x, u, gl, attn, w, tm):
    n_tok = x.shape[0]
    halo_spec = pl.BlockSpec((HALO, POOL_WIDTH), lambda i: (jnp.maximum(i * (tm // HALO) - 1, 0), 0))

    def body(i, tin, res, tout, acc, scr):
        x_ref, u_ref, uh_ref, gl_ref, at_ref = tin
        w_pool, pool_scale, w_ba, w_bp, b_gate, w_out, g_post = res
        d_o, pooled_o, a_o, pp_o, merged_o, y_o, h1_o = tout
        ext, = scr
        ext[pl.ds(0, HALO), :] = jnp.where(i > 0, uh_ref[...].astype(F32), 0.0)
        ext[pl.ds(HALO, tm), :] = u_ref[...].astype(F32)
        means = _pool_windows(ext, i, tm, HALO)
        for g in range(len(POOL_WINDOWS)):
            cs = slice(g * POOL_GROUP, (g + 1) * POOL_GROUP)
            d = (means[g] - ext[pl.ds(HALO, tm), cs]).astype(MX)
            d_o[:, cs] = d
            pooled_o[:, cs] = (_dot(d, w_pool[g]) * pool_scale[:, cs]).astype(pooled_o.dtype)
        a_br = _dot(at_ref[...], w_ba[...])
        p_br = _dot(pooled_o[...], w_bp[...])
        a_o[...] = a_br.astype(a_o.dtype)
        pp_o[...] = p_br.astype(pp_o.dtype)
        gates = _sig(gl_ref[...].astype(F32) + b_gate[...])
        merged = (gates[:, :D_MODEL] * a_br + gates[:, D_MODEL:] * p_br).astype(MX)
        merged_o[...] = merged
        y = _dot(merged, w_out[...])
        y_o[...] = y.astype(y_o.dtype)
        h1_o[...] = x_ref[...] + _rms(y, g_post[...])[0]

    outs = [_sds(n_tok, POOL_WIDTH, MX), _sds(n_tok, POOL_WIDTH, MX), _sds(n_tok, D_MODEL, MX), _sds(n_tok, D_MODEL, MX),
            _sds(n_tok, D_MODEL, MX), _sds(n_tok, D_MODEL, MX), _sds(n_tok, D_MODEL, F32)]
    res = [w["w_pool"], w["pool_scale"], w["w_branch_attn"], w["w_branch_pool"], w["b_gate"], w["w_out"], w["g_post_mix"]]
    return _tok_call("fwd_mix", body, n_tok, tm, [x, u, (u, halo_spec), gl, attn], res, outs,
                     scratch=[pltpu.VMEM((tm + HALO, POOL_WIDTH), F32)])


def _fwd_mlp(h1, w, tm):
    n_tok = h1.shape[0]

    def body(i, tin, res, tout, acc, scr):
        h1_ref, = tin
        g_pre, w1, w2, g_post = res
        m_o, zr_o, a2_o, f_o, h2_o = tout
        h1_ = h1_ref[...]
        m = _rms(h1_, g_pre[...])[0].astype(MX)
        m_o[...] = m
        zr = jnp.maximum(_dot(m, w1[...]), 0.0)
        zr_o[...] = zr.astype(zr_o.dtype)
        a2 = (zr * zr).astype(MX)
        a2_o[...] = a2
        f = _dot(a2, w2[...])
        f_o[...] = f.astype(f_o.dtype)
        h2_o[...] = h1_ + _rms(f, g_post[...])[0]

    outs = [_sds(n_tok, D_MODEL, MX), _sds(n_tok, D_FF, MX), _sds(n_tok, D_FF, MX), _sds(n_tok, D_MODEL, MX),
            _sds(n_tok, D_MODEL, F32)]
    res = [w["g_pre_mlp"], w["w_ff1"], w["w_ff2"], w["g_post_mlp"]]
    return _tok_call("fwd_mlp", body, n_tok, tm, [h1], res, outs)


def _ple_fwd_bwd(h2, p, target, w, tm):
    n_tok = h2.shape[0]

    def body(i, tin, res, tout, acc, scr):
        h2_ref, p_ref, t_ref = tin
        w_pe, w_pg, g_ple = res
        dh2_o, de_o, dzg_o = tout
        loss_a, dg_a = acc
        h2_ = h2_ref[...]
        e = _dot(p_ref[...], w_pe[...])
        pg = _sig(_dot(h2_, w_pg[...]))
        t = pg * e
        g = g_ple[...]
        tn, th, r = _rms(t, g)
        diff = h2_ + tn - t_ref[...]
        loss_a[...] += jnp.sum(diff * diff, axis=0, keepdims=True)
        dh3 = diff * (1.0 / D_MODEL)
        dt, dg = _rms_bwd(th, r, g, dh3)
        dg_a[...] += dg
        de_o[...] = (dt * pg).astype(de_o.dtype)
        dzg = (dt * e * pg * (1.0 - pg)).astype(MX)
        dzg_o[...] = dzg
        dh2_o[...] = dh3 + _dot_nt(dzg, w_pg[...])

    outs = [_sds(n_tok, D_MODEL, F32), _sds(n_tok, D_MODEL, MX), _sds(n_tok, D_MODEL, MX)]
    accs = [_sds(1, D_MODEL, F32), _sds(1, D_MODEL, F32)]
    return _tok_call("ple_fwd_bwd", body, n_tok, tm, [h2, p, target], [w["w_ple_proj"], w["w_ple_gate"], w["g_ple"]], outs, accs)


def _bwd_mlp(dh2, f, h1, zr, w, tm):
    n_tok = dh2.shape[0]

    def body(i, tin, res, tout, acc, scr):
        dh2_ref, f_ref, h1_ref, zr_ref = tin
        g_pre, w1, w2, g_post = res
        df_o, dz_o, dh1_o = tout
        dg_post_a, dg_pre_a = acc
        dh2_ = dh2_ref[...]
        gp = g_post[...]
        _, fh, rf = _rms(f_ref[...].astype(F32), gp)
        df, dg = _rms_bwd(fh, rf, gp, dh2_)
        dg_post_a[...] += dg
        df = df.astype(MX)
        df_o[...] = df
        dz = (_dot_nt(df, w2[...]) * (2.0 * zr_ref[...].astype(F32))).astype(MX)
        dz_o[...] = dz
        dm = _dot_nt(dz, w1[...])
        gq = g_pre[...]
        _, hh, rh = _rms(h1_ref[...], gq)
        dh1, dg = _rms_bwd(hh, rh, gq, dm)
        dg_pre_a[...] += dg
        dh1_o[...] = dh2_ + dh1

    outs = [_sds(n_tok, D_MODEL, MX), _sds(n_tok, D_FF, MX), _sds(n_tok, D_MODEL, F32)]
    accs = [_sds(1, D_MODEL, F32), _sds(1, D_MODEL, F32)]
    res = [w["g_pre_mlp"], w["w_ff1"], w["w_ff2"], w["g_post_mlp"]]
    return _tok_call("bwd_mlp", body, n_tok, tm, [dh2, f, h1, zr], res, outs, accs)


def _bwd_mix(dh1, y, a_br, p_br, gl, attn, d, w, tm):
    n_tok = dh1.shape[0]

    def body(i, tin, res, tout, acc, scr):
        dh1_ref, y_ref, a_ref, pp_ref, gl_ref, at_ref, d_ref = tin
        g_post, w_out, b_gate, w_ba, w_bp, w_pool, pool_scale, sel = res
        dy_o, da_o, dpp_o, dgl_o, do_o, delta_o, dyp_o, dd_o = tout
        dg_post_a, db_a, dps_a = acc
        g = g_post[...]
        _, yh, r = _rms(y_ref[...].astype(F32), g)
        dy, dg = _rms_bwd(yh, r, g, dh1_ref[...])
        dg_post_a[...] += dg
        dy = dy.astype(MX)
        dy_o[...] = dy
        dmerged = _dot_nt(dy, w_out[...])
        gates = _sig(gl_ref[...].astype(F32) + b_gate[...])
        ga, gp = gates[:, :D_MODEL], gates[:, D_MODEL:]
        da = (dmerged * ga).astype(MX)
        dpp = (dmerged * gp).astype(MX)
        da_o[...] = da
        dpp_o[...] = dpp
        dgl_a = dmerged * a_ref[...].astype(F32) * ga * (1.0 - ga)
        dgl_p = dmerged * pp_ref[...].astype(F32) * gp * (1.0 - gp)
        dgl_o[:, :D_MODEL] = dgl_a.astype(dgl_o.dtype)
        dgl_o[:, D_MODEL:] = dgl_p.astype(dgl_o.dtype)
        db_a[:, :D_MODEL] += jnp.sum(dgl_a, axis=0, keepdims=True)
        db_a[:, D_MODEL:] += jnp.sum(dgl_p, axis=0, keepdims=True)
        do = _dot_nt(da, w_ba[...]).astype(MX)
        do_o[...] = do
        prod = do.astype(F32) * at_ref[...].astype(F32)
        hi = prod.astype(MX)
        lo = (prod - hi.astype(F32)).astype(MX)
        delta_o[...] = _dot(hi, sel[...]) + _dot(lo, sel[...])
        dpooled = _dot_nt(dpp, w_bp[...])
        for gi in range(len(POOL_WINDOWS)):
            cs = slice(gi * POOL_GROUP, (gi + 1) * POOL_GROUP)
            ypre = _dot(d_ref[:, cs], w_pool[gi])
            dps_a[:, cs] += jnp.sum(dpooled[:, cs] * ypre, axis=0, keepdims=True)
            dyp = (dpooled[:, cs] * pool_scale[:, cs]).astype(MX)
            dyp_o[:, cs] = dyp
            dd_o[:, cs] = _dot_nt(dyp, w_pool[gi])

    outs = [_sds(n_tok, D_MODEL, MX), _sds(n_tok, D_MODEL, MX), _sds(n_tok, D_MODEL, MX), _sds(n_tok, 2 * D_MODEL, MX),
            _sds(n_tok, N_HEADS * V_HEAD, MX), _sds(n_tok, HEAD_SLOT, F32), _sds(n_tok, POOL_WIDTH, MX),
            _sds(n_tok, POOL_WIDTH, F32)]
    accs = [_sds(1, D_MODEL, F32), _sds(1, 2 * D_MODEL, F32), _sds(1, POOL_WIDTH, F32)]
    res = [w["g_post_mix"], w["w_out"], w["b_gate"], w["w_branch_attn"], w["w_branch_pool"], w["w_pool"], w["pool_scale"],
           w["head_sel"]]
    return _tok_call("bwd_mix", body, n_tok, tm, [dh1, y, a_br, p_br, gl, attn, d], res, outs, accs)


def _bwd_heads(q_ref, k_ref, v_ref, do_ref, lse_ref, dl_ref, st_s, dpt_s, keep, use, n_heads):
    def products(h):
        hs = slice(h * HEAD_SLOT, (h + 1) * HEAD_SLOT)
        vs = slice(h * V_HEAD, (h + 1) * V_HEAD)
        st_s[h % 2] = _dot_nt(k_ref[:, hs], q_ref[:, hs])
        dpt_s[h % 2] = _dot_nt(v_ref[:, vs], do_ref[:, vs])

    products(0)
    for h in range(n_heads):
        if h + 1 < n_heads:
            products(h + 1)
        st = st_s[h % 2]
        if keep is not None:
            st = jnp.where(keep, st, NEG)
        pt = jnp.exp2(st - lse_ref[h:h + 1, :])
        use(h, pt, pt * (dpt_s[h % 2] - dl_ref[h:h + 1, :]))


HEAD_GROUP = 4


def _attn_bwd(q, k, kt, v, do, lse, delta, tq, tk, scatter=None):
    n_tok = q.shape[0]
    nq, ratio = n_tok // tq, tq // tk
    n_groups = N_HEADS // HEAD_GROUP
    gq, gv = HEAD_GROUP * HEAD_SLOT, HEAD_GROUP * V_HEAD
    qi, kj = _causal_pairs(nq, ratio, by_kv=True)

    n_pairs = qi.shape[0]

    def kern(qi_ref, kj_ref, q_ref, k_ref, kt_ref, v_ref, do_ref, lse_ref, dl_ref, *rest):
        if scatter is not None:
            s_hbm, dq_ref, dk_ref, dv_ref, r_hbm, dk_s, dv_s, st_s, dpt_s, send_sems, recv_sems = rest
        else:
            dq_ref, dk_ref, dv_ref, dk_s, dv_s, st_s, dpt_s = rest
        s_id = pl.program_id(1)
        i, j = qi_ref[s_id], kj_ref[s_id]
        cols = pl.ds(pl.multiple_of(i * tq, tq), tq)
        if scatter is not None:
            group = pl.program_id(0)
            _scatter_steps(jnp.logical_and(group == 0, s_id == 0),
                           jnp.logical_and(group == n_groups - 1, s_id == n_pairs - 1), s_hbm, r_hbm, (send_sems, recv_sems))

        @pl.when(s_id == 0)
        def _():
            dq_ref[...] = jnp.zeros(dq_ref.shape, F32)

        def use(h, pt, dst):
            hs = slice(h * HEAD_SLOT, (h + 1) * HEAD_SLOT)
            dv_s[h] += _dot(pt, do_ref[:, h * V_HEAD:(h + 1) * V_HEAD])
            dk_s[:, hs] += _dot(dst, q_ref[:, hs])
            dq_ref[hs, cols] += _dot(kt_ref[hs, :], dst)

        def heads(masked):
            keep = _keep_t(tk, tq, j * tk - i * tq) if masked else None
            _bwd_heads(q_ref, k_ref, v_ref, do_ref, lse_ref.at[0], dl_ref.at[0], st_s, dpt_s, keep, use, HEAD_GROUP)

        @pl.when(j >= i * ratio)
        def _():
            dk_s[...] = jnp.zeros(dk_s.shape, F32)
            dv_s[...] = jnp.zeros(dv_s.shape, F32)
            heads(True)

        @pl.when(j < i * ratio)
        def _():
            heads(False)

        @pl.when(i == nq - 1)
        def _():
            dk_ref[...] = (dk_s[...] * (1.0 / LOG2E)).astype(dk_ref.dtype)
            for h in range(HEAD_GROUP):
                dv_ref[:, h * V_HEAD:(h + 1) * V_HEAD] = dv_s[h].astype(dv_ref.dtype)

    at_q = lambda g, s, qi, kj: (qi[s], g)
    at_k = lambda g, s, qi, kj: (kj[s], g)
    at_kt = lambda g, s, qi, kj: (g, kj[s])
    at_stat = lambda g, s, qi, kj: (g, 0, qi[s])
    in_specs = [pl.BlockSpec((tq, gq), at_q), pl.BlockSpec((tk, gq), at_k), pl.BlockSpec((gq, tk), at_kt),
                pl.BlockSpec((tk, gv), at_k), pl.BlockSpec((tq, gv), at_q),
                pl.BlockSpec((1, HEAD_GROUP, tq), at_stat), pl.BlockSpec((1, HEAD_GROUP, tq), at_stat)]
    out_specs = [pl.BlockSpec((gq, n_tok), lambda g, s, qi, kj: (g, 0), pipeline_mode=pl.Buffered(1)),
                 pl.BlockSpec((tk, gq), at_k), pl.BlockSpec((tk, gv), at_k)]
    out_shape = [_sds(QK_WIDTH, n_tok, F32), _sds(n_tok, QK_WIDTH, MX), _sds(n_tok, N_HEADS * V_HEAD, MX)]
    scratch = [pltpu.VMEM((tk, gq), F32), pltpu.VMEM((HEAD_GROUP, tk, V_HEAD), F32),
               pltpu.VMEM((2, tk, tq), F32), pltpu.VMEM((2, tk, tq), F32)]
    stat3 = lambda a: a.reshape(n_groups, HEAD_GROUP, n_tok)
    operands = [qi, kj, q, k, kt, v, do, stat3(lse), stat3(delta)]
    if scatter is not None:
        operands.append(scatter)
        in_specs.append(_HBM)
        out_specs.append(_HBM)
        out_shape.append(jax.ShapeDtypeStruct(scatter.shape, scatter.dtype))
        scratch += [pltpu.SemaphoreType.DMA((3,)), pltpu.SemaphoreType.DMA((3,))]
    gs = pltpu.PrefetchScalarGridSpec(num_scalar_prefetch=2, grid=(n_groups, n_pairs), in_specs=in_specs,
                                      out_specs=out_specs, scratch_shapes=scratch)
    return pl.pallas_call(kern, name="attn_bwd", grid_spec=gs, out_shape=out_shape,
                          compiler_params=_params(("arbitrary", "arbitrary")))(*operands)


def _bwd_inproj(dq_t, dk, dv, dd, dgl, ps, x, dh1, cc, sa, w, tm):
    n_tok = x.shape[0]
    n_tiles = n_tok // tm
    last_halo = n_tok // HALO - 1
    halo_spec = pl.BlockSpec((HALO, POOL_WIDTH), lambda i: (jnp.minimum((i + 1) * (tm // HALO), last_halo), 0))

    def body(i, tin, res, tout, acc, scr):
        dq_ref, dk_ref, dv_ref, dd_ref, ddh_ref, dgl_ref, ps_ref, x_ref, dh1_ref, c_ref, s_ref = tin
        w_uq, g_q, w_k, w_v, e_mat, g_kv, w_in, g_pre = res
        dqu_o, dproj_o, dx_o = tout
        dgq_a, dgkv_a, dgpre_a = acc
        ext, = scr
        cc_, sa_ = c_ref[...], s_ref[...]
        for h in range(N_HEADS):
            hs = slice(h * HEAD_SLOT, (h + 1) * HEAD_SLOT)
            dqu_o[:, hs] = (_unrope(dq_ref[hs, :].T, cc_, sa_) * SCALE).astype(dqu_o.dtype)
        gq = g_q[...]
        _, qh, rq = _rms(ps_ref[:, :Q_LORA].astype(F32), gq)
        dqd, dg = _rms_bwd(qh, rq, gq, _dot_nt(dqu_o[...], w_uq[...]))
        dgq_a[...] += dg
        dproj_o[:, :Q_LORA] = dqd.astype(dproj_o.dtype)
        gkv = g_kv[...]
        _, kh, rk = _rms(ps_ref[:, Q_LORA:Q_LORA + KV_LORA].astype(F32), gkv)
        dkvd, dg = _rms_bwd(kh, rk, gkv, _dot_nt(dk_ref[...], w_k[...]) + _dot_nt(dv_ref[...], w_v[...]))
        dgkv_a[...] += dg
        dproj_o[:, Q_LORA:Q_LORA + KV_LORA] = dkvd.astype(dproj_o.dtype)
        dproj_o[:, Q_LORA + KV_LORA:SMALL_COLS] = _unrope(_dot_nt(dk_ref[...], e_mat[...]), cc_, sa_).astype(dproj_o.dtype)
        row = i * tm + lax.broadcasted_iota(jnp.int32, (tm + HALO, 1), 0)
        for gi, wdw in enumerate(POOL_WINDOWS):
            cs = slice(gi * POOL_GROUP, (gi + 1) * POOL_GROUP)
            inv = 1.0 / jnp.minimum(row + 1, wdw).astype(F32)
            ext[pl.ds(0, tm), cs] = dd_ref[:, cs] * inv[:tm]
            ext[pl.ds(tm, HALO), cs] = jnp.where(i < n_tiles - 1, ddh_ref[:, cs] * inv[tm:], 0.0)
            s = ext[pl.ds(0, tm), cs]
            for k_ in range(1, wdw):
                s = s + ext[pl.ds(k_, tm), cs]
            dproj_o[:, SMALL_COLS + gi * POOL_GROUP:SMALL_COLS + (gi + 1) * POOL_GROUP] = (s - dd_ref[:, cs]).astype(dproj_o.dtype)
        dproj_o[:, SMALL_COLS + POOL_WIDTH:] = dgl_ref[...]
        da = _dot_nt(dproj_o[...], w_in[...])
        gp = g_pre[...]
        _, xh, rx = _rms(x_ref[...], gp)
        dx, dg = _rms_bwd(xh, rx, gp, da)
        dgpre_a[...] += dg
        dx_o[...] = dh1_ref[...] + dx

    outs = [_sds(n_tok, QK_WIDTH, MX), _sds(n_tok, IN_PAD, MX), _sds(n_tok, D_MODEL, F32)]
    accs = [_sds(1, Q_LORA, F32), _sds(1, KV_LORA, F32), _sds(1, D_MODEL, F32)]
    res = [w["w_uq"], w["g_q"], w["w_k"], w["w_v"], w["e_mat"], w["g_kv"], w["w_in"], w["g_pre_mix"]]
    dq_spec = pl.BlockSpec((QK_WIDTH, tm), lambda i: (0, i))
    return _tok_call("bwd_inproj", body, n_tok, tm, [(dq_t, dq_spec), dk, dv, dd, (dd, halo_spec), dgl, ps, x, dh1, cc, sa], res, outs, accs,
                     scratch=[pltpu.VMEM((tm + HALO, POOL_WIDTH), F32)])


XTDY_TOKENS = 1024
XTDY_OUT_BYTES = 8 * 2**20


def _xtdy(name, x, dy):
    n_tok, kk = x.shape
    nn = dy.shape[1]
    bt = min(XTDY_TOKENS, n_tok)
    bk = kk
    while bk * nn * 4 > XTDY_OUT_BYTES and bk % 256 == 0:
        bk //= 2

    def kern(x_ref, dy_ref, o_ref):
        @pl.when(pl.program_id(1) == 0)
        def _():
            o_ref[...] = jnp.zeros(o_ref.shape, F32)
        o_ref[...] += _dot_tn(x_ref[...], dy_ref[...])

    return pl.pallas_call(
        kern, name=name, grid=(kk // bk, n_tok // bt),
        in_specs=[pl.BlockSpec((bt, bk), lambda a, t: (t, a)), pl.BlockSpec((bt, nn), lambda a, t: (t, 0))],
        out_specs=pl.BlockSpec((bk, nn), lambda a, t: (a, 0)), out_shape=_sds(kk, nn, F32),
        compiler_params=_params(("arbitrary", "arbitrary")))(x, dy)


def _rope_tables(positions):
    inv_freq = ROPE_THETA ** (-jnp.arange(0, QK_ROPE, 2, dtype=F32) / QK_ROPE)
    ang = positions.astype(F32)[:, None] * inv_freq
    cos, sin = jnp.cos(ang), jnp.sin(ang)
    n_tok = positions.shape[0]
    ones, z64 = jnp.ones((n_tok, ROPE_LANE), F32), jnp.zeros((n_tok, ROPE_LANE), F32)
    z32 = jnp.zeros((n_tok, HEAD_SLOT - ROPE_LANE - QK_ROPE), F32)
    return jnp.concatenate([ones, cos, cos, z32], 1), jnp.concatenate([z64, -sin, sin, z32], 1)


def _kernel_weights(full):
    w_in, w_uq, w_ukv = full["w_in"], full["w_uq"], full["w_ukv"]
    c0 = Q_LORA + KV_LORA
    z = lambda n: jnp.zeros((D_MODEL, n), w_in.dtype)
    w = dict(full)
    w["w_in"] = jnp.concatenate([w_in[:, :c0], z(ROPE_LANE), w_in[:, c0:c0 + QK_ROPE], z(HEAD_SLOT - ROPE_LANE - QK_ROPE),
                                 w_in[:, c0 + QK_ROPE:]], 1)
    w["w_uq"] = jnp.pad(w_uq.reshape(Q_LORA, N_HEADS, QK_NOPE + QK_ROPE),
                        ((0, 0), (0, 0), (0, HEAD_SLOT - QK_NOPE - QK_ROPE))).reshape(Q_LORA, QK_WIDTH)
    kv = w_ukv.reshape(KV_LORA, N_HEADS, QK_NOPE + V_HEAD)
    w["w_k"] = jnp.pad(kv[:, :, :QK_NOPE], ((0, 0), (0, 0), (0, HEAD_SLOT - QK_NOPE))).reshape(KV_LORA, QK_WIDTH)
    w["w_v"] = kv[:, :, QK_NOPE:].reshape(KV_LORA, N_HEADS * V_HEAD)
    e = np.zeros((HEAD_SLOT, QK_WIDTH), np.float32)
    sel = np.zeros((N_HEADS * V_HEAD, HEAD_SLOT), np.float32)
    for h in range(N_HEADS):
        for r in range(QK_ROPE):
            e[ROPE_LANE + r, h * HEAD_SLOT + ROPE_LANE + r] = 1.0
        sel[h * V_HEAD:(h + 1) * V_HEAD, h] = 1.0
    w["e_mat"] = jnp.asarray(e, MX)
    w["w_kt"], w["e_t"] = w["w_k"].T, jnp.asarray(e.T, MX)
    pad = ((0, 0), (0, V_ROWS - V_HEAD), (0, 0))
    w["w_vt"] = jnp.pad(w["w_v"].T.reshape(N_HEADS, V_HEAD, KV_LORA), pad).reshape(N_HEADS * V_ROWS, KV_LORA)
    ones = np.zeros((N_HEADS, V_ROWS, 1), np.float32)
    ones[:, V_HEAD] = 1.0
    w["v_ones"] = jnp.asarray(ones.reshape(N_HEADS * V_ROWS, 1))
    w["head_sel"] = jnp.asarray(sel, MX)
    w["w_pool"] = full["w_pool"].astype(MX)
    return w


def _local_step(x, p, positions, target, full, mesh_place=None, packed_rest=None):
    n_tok = x.shape[0]
    tm = min(512, n_tok)
    tm_mlp = min(256, n_tok)
    tk = min(ATTN_KEYS, n_tok)
    tq_fwd, tq_bwd = min(ATTN_QUERIES_FWD, n_tok), min(ATTN_QUERIES_BWD, n_tok)
    w = _kernel_weights(full)
    cc, sa = _rope_tables(positions)

    if mesh_place is None:
        a, ps, u, gl, qn, kvn, q, k, v, kt, vt = _fwd_inproj(x, cc, sa, w, tm)
    else:
        my_chip, core = mesh_place
        a, ps, u, gl, qn, kvn, q, k, v, kt, vt, gathered = _fwd_inproj(x, cc, sa, w, tm, gather=packed_rest)
        w.update(_unpack_full(lax.dynamic_update_slice(gathered, packed_rest[None], (my_chip, 0, 0)), REST))
    attn, lse = _attn_fwd(q, k, vt, tq_fwd, tk)
    d, pooled, a_br, p_br, merged, y, h1 = _fwd_mix(x, u, gl, attn, w, tm)
    m, zr, a2, f, h2 = _fwd_mlp(h1, w, tm_mlp)
    dh2, de, dzg, loss_cols, dg_ple = _ple_fwd_bwd(h2, p, target, w, tm)
    df, dz, dh1, dg_post_mlp, dg_pre_mlp = _bwd_mlp(dh2, f, h1, zr, w, tm_mlp)
    dy, da_br, dp_br, dgl, do, delta, dyp, dd, dg_post_mix, db_gate, dpool_scale = _bwd_mix(dh1, y, a_br, p_br, gl, attn, d, w, tm)
    grads = {"w_branch_attn": _xtdy("dw_ba", attn, da_br), "w_branch_pool": _xtdy("dw_bp", pooled, dp_br),
             "w_out": _xtdy("dw_out", merged, dy), "w_ff1": _xtdy("dw_ff1", m, dz), "w_ff2": _xtdy("dw_ff2", a2, df),
             "w_ple_proj": _xtdy("dw_pe", p, de), "w_ple_gate": _xtdy("dw_pg", h2, dzg)}
    delta_t = delta[:, :N_HEADS].T
    if mesh_place is None:
        travelling = None
        dq_t, dk, dv = _attn_bwd(q, k, kt, v, do, lse, delta_t, tq_bwd, tk)
    else:
        pieces = _pack_pieces(grads, REST)
        sent = _add_halves(pieces, _exchange_halves(pieces), core)
        dq_t, dk, dv, received = _attn_bwd(q, k, kt, v, do, lse, delta_t, tq_bwd, tk, scatter=sent)
        travelling = (sent, received)
        grads = {}
    dqu, dproj, dx, dg_q, dg_kv, dg_pre_mix = _bwd_inproj(dq_t, dk, dv, dd, dgl, ps, x, dh1, cc, sa, w, tm)

    g_in = _xtdy("dw_in", a, dproj)
    g_uq = _xtdy("dw_uq", qn, dqu)
    g_k = _xtdy("dw_k", kvn, dk)
    g_v = _xtdy("dw_v", kvn, dv)
    g_pool = _xtdy("dw_pool", d, dyp)

    c0 = Q_LORA + KV_LORA
    grads.update({
        "g_pre_mix": dg_pre_mix,
        "w_in": jnp.concatenate([g_in[:, :c0], g_in[:, c0 + ROPE_LANE:c0 + ROPE_LANE + QK_ROPE], g_in[:, SMALL_COLS:]], 1),
        "b_gate": db_gate,
        "g_q": dg_q,
        "w_uq": g_uq.reshape(Q_LORA, N_HEADS, HEAD_SLOT)[:, :, :QK_NOPE + QK_ROPE].reshape(Q_LORA, N_HEADS * (QK_NOPE + QK_ROPE)),
        "g_kv": dg_kv,
        "w_ukv": jnp.concatenate([g_k.reshape(KV_LORA, N_HEADS, HEAD_SLOT)[:, :, :QK_NOPE],
                                  g_v.reshape(KV_LORA, N_HEADS, V_HEAD)], 2).reshape(KV_LORA, N_HEADS * (QK_NOPE + V_HEAD)),
        "w_pool": jnp.stack([g_pool[g * POOL_GROUP:(g + 1) * POOL_GROUP, g * POOL_GROUP:(g + 1) * POOL_GROUP]
                             for g in range(len(POOL_WINDOWS))]),
        "pool_scale": dpool_scale,
        "g_post_mix": dg_post_mix,
        "g_pre_mlp": dg_pre_mlp,
        "g_post_mlp": dg_post_mlp,
        "g_ple": dg_ple,
    })
    return loss_cols, dx, grads, travelling


def _place():
    return lax.axis_index("x"), lax.axis_index("y"), lax.axis_index("c")


CHIP_FLIPS = ((1, 0), (0, 1), (1, 1))


def _flip(x, y, fx, fy):
    return (1 - x if fx else x), (1 - y if fy else y)


_HBM = pl.BlockSpec(memory_space=pl.ANY)


def _gather_copies(w_ref, out_ref, send_sems, recv_sems):
    half = w_ref.shape[0] // 2
    x, y, c = _place()
    my_chip = 2 * x + y
    sibling = (x, y, 1 - c)

    def half_of(chip, hc):
        return out_ref.at[chip, pl.ds(pl.multiple_of(hc * half, 16), half), :]

    src = w_ref.at[pl.ds(pl.multiple_of(c * half, 16), half), :]
    sends, landed, forwards, from_sibling = [], [], [], []
    for j, (fx, fy) in enumerate(CHIP_FLIPS):
        px, py = _flip(x, y, fx, fy)
        mine_there, theirs_here, theirs_other = half_of(my_chip, c), half_of(2 * px + py, c), half_of(2 * px + py, 1 - c)
        sends.append(pltpu.make_async_remote_copy(src, mine_there, send_sems.at[j], recv_sems.at[j],
                                                  device_id=(px, py, c), device_id_type=MESH))
        landed.append(pltpu.make_async_remote_copy(src, theirs_here, send_sems.at[j], recv_sems.at[j],
                                                   device_id=(px, py, c), device_id_type=MESH))
        forwards.append(pltpu.make_async_remote_copy(theirs_here, theirs_here, send_sems.at[3 + j], recv_sems.at[3 + j],
                                                     device_id=sibling, device_id_type=MESH))
        from_sibling.append(pltpu.make_async_remote_copy(theirs_other, theirs_other, send_sems.at[3 + j],
                                                         recv_sems.at[3 + j], device_id=sibling, device_id_type=MESH))
    return sends, landed, forwards, from_sibling


def _gather_steps(i, n_steps, w_ref, out_ref, sems):
    sends, landed, forwards, from_sibling = _gather_copies(w_ref, out_ref, *sems)

    @pl.when(i == 0)
    def _():
        for cp in sends:
            cp.start()

    @pl.when(i == (3 * n_steps) // 4)
    def _():
        for arrived, fwd in zip(landed, forwards):
            arrived.wait_recv()
            fwd.start()

    @pl.when(i == n_steps - 1)
    def _():
        for cp in from_sibling:
            cp.wait_recv()
        for cp in sends + forwards:
            cp.wait_send()


def _allgather_shards(wp):
    def body(w_ref, out_ref, send_sems, recv_sems):
        sends, landed, forwards, from_sibling = _gather_copies(w_ref, out_ref, send_sems, recv_sems)
        for cp in sends:
            cp.start()
        for arrived, fwd in zip(landed, forwards):
            arrived.wait_recv()
            fwd.start()
        for cp in from_sibling:
            cp.wait_recv()
        for cp in sends + forwards:
            cp.wait_send()

    return pl.pallas_call(
        body, name="allgather_shards", out_shape=jax.ShapeDtypeStruct((N_CHIPS,) + wp.shape, wp.dtype),
        in_specs=[_HBM], out_specs=_HBM,
        scratch_shapes=[pltpu.SemaphoreType.DMA((6,)), pltpu.SemaphoreType.DMA((6,))],
    )(wp)


def _exchange_halves(g):
    rows = g.shape[1]
    half = rows // 2

    def body(g_ref, r_ref, send_sem, recv_sem):
        x, y, c = _place()
        src = g_ref.at[:, pl.ds(pl.multiple_of((1 - c) * half, 8), half), :]
        cp = pltpu.make_async_remote_copy(src, r_ref, send_sem, recv_sem, device_id=(x, y, 1 - c), device_id_type=MESH)
        cp.start()
        cp.wait()

    return pl.pallas_call(
        body, name="exchange_halves", out_shape=jax.ShapeDtypeStruct((N_CHIPS, half, PACK_COLS), g.dtype),
        in_specs=[_HBM], out_specs=_HBM, scratch_shapes=[pltpu.SemaphoreType.DMA, pltpu.SemaphoreType.DMA],
    )(g)


def _add_halves(g, r, c):
    rows = g.shape[1]
    half = rows // 2
    br = REDUCE_ROWS
    nb = half // br

    def kern(c_ref, g_ref, r_ref, o_ref):
        o_ref[...] = (g_ref[...] + r_ref[...]).astype(o_ref.dtype)

    gs = pltpu.PrefetchScalarGridSpec(
        num_scalar_prefetch=1, grid=(N_CHIPS, nb),
        in_specs=[pl.BlockSpec((1, br, PACK_COLS), lambda k, t, c: (k, c[0] * nb + t, 0)),
                  pl.BlockSpec((1, br, PACK_COLS), lambda k, t, c: (k, t, 0))],
        out_specs=pl.BlockSpec((1, br, PACK_COLS), lambda k, t, c: (k, t, 0)))
    return pl.pallas_call(kern, name="add_halves", grid_spec=gs,
                          out_shape=jax.ShapeDtypeStruct((N_CHIPS, half, PACK_COLS), WIRE),
                          compiler_params=_params(("arbitrary", "arbitrary")))(c.reshape(1), g, r)


def _scatter_copies(s_ref, r_ref, send_sems, recv_sems):
    x, y, c = _place()
    my_chip = 2 * x + y
    sends, arrivals = [], []
    for j, (fx, fy) in enumerate(CHIP_FLIPS):
        px, py = _flip(x, y, fx, fy)
        slot = r_ref.at[2 * px + py]
        sends.append(pltpu.make_async_remote_copy(s_ref.at[2 * px + py], r_ref.at[my_chip], send_sems.at[j], recv_sems.at[j],
                                                  device_id=(px, py, c), device_id_type=MESH))
        arrivals.append(pltpu.make_async_remote_copy(slot, slot, send_sems.at[j], recv_sems.at[j],
                                                     device_id=(px, py, c), device_id_type=MESH))
    return sends, arrivals


def _scatter_steps(first, last, s_ref, r_ref, sems):
    sends, arrivals = _scatter_copies(s_ref, r_ref, *sems)

    @pl.when(first)
    def _():
        for cp in sends:
            cp.start()

    @pl.when(last)
    def _():
        for cp in arrivals:
            cp.wait_recv()
        for cp in sends:
            cp.wait_send()


def _scatter_pieces(s):
    def body(s_ref, r_ref, send_sems, recv_sems):
        sends, arrivals = _scatter_copies(s_ref, r_ref, send_sems, recv_sems)
        for cp in sends:
            cp.start()
        for cp in arrivals:
            cp.wait_recv()
        for cp in sends:
            cp.wait_send()

    return pl.pallas_call(
        body, name="scatter_pieces", out_shape=jax.ShapeDtypeStruct(s.shape, s.dtype), in_specs=[_HBM], out_specs=_HBM,
        scratch_shapes=[pltpu.SemaphoreType.DMA((3,)), pltpu.SemaphoreType.DMA((3,))],
    )(s)


def _sum_pieces(r):
    half = r.shape[1]
    br = REDUCE_ROWS

    def kern(r_ref, o_ref):
        o_ref[...] = ((r_ref[0].astype(F32) + r_ref[1].astype(F32)) + r_ref[2].astype(F32)) + r_ref[3].astype(F32)

    return pl.pallas_call(
        kern, name="sum_pieces", grid=(half // br,), in_specs=[pl.BlockSpec((N_CHIPS, br, PACK_COLS), lambda t: (0, t, 0))],
        out_specs=pl.BlockSpec((br, PACK_COLS), lambda t: (t, 0)), out_shape=_sds(half, PACK_COLS, F32),
        compiler_params=_params(("arbitrary",)))(r)


def _join_halves(f):
    def body(f_ref, o_ref, send_sem, recv_sem):
        x, y, c = _place()
        cp = pltpu.make_async_remote_copy(f_ref, o_ref, send_sem, recv_sem, device_id=(x, y, 1 - c), device_id_type=MESH)
        cp.start()
        cp.wait()

    return pl.pallas_call(
        body, name="join_halves", out_shape=jax.ShapeDtypeStruct(f.shape, f.dtype), in_specs=[_HBM], out_specs=_HBM,
        scratch_shapes=[pltpu.SemaphoreType.DMA, pltpu.SemaphoreType.DMA],
    )(f)


def _allreduce_small(g):
    n_dev = 8

    def body(g_ref, o_ref, buf, send_sems, recv_sems):
        x, y, c = _place()
        me = 4 * x + 2 * y + c
        buf[me] = g_ref[...]
        peers = []
        for f in range(1, n_dev):
            px, py = _flip(x, y, f & 4, f & 2)
            pc = 1 - c if f & 1 else c
            peers.append((px, py, pc))
        sent = []
        for f, peer in enumerate(peers):
            cp = pltpu.make_async_remote_copy(g_ref, buf.at[me], send_sems.at[f], recv_sems.at[f], device_id=peer,
                                              device_id_type=MESH)
            cp.start()
            sent.append(cp)
        for f, (px, py, pc) in enumerate(peers):
            slot = buf.at[4 * px + 2 * py + pc]
            pltpu.make_async_remote_copy(slot, slot, send_sems.at[f], recv_sems.at[f], device_id=(px, py, pc),
                                         device_id_type=MESH).wait_recv()
        for cp in sent:
            cp.wait_send()
        total = buf[0]
        for k in range(1, n_dev):
            total = total + buf[k]
        o_ref[...] = total

    vmem = pl.BlockSpec(memory_space=pltpu.VMEM)
    return pl.pallas_call(
        body, name="allreduce_small", out_shape=jax.ShapeDtypeStruct(g.shape, g.dtype), in_specs=[vmem], out_specs=vmem,
        scratch_shapes=[pltpu.VMEM((n_dev,) + g.shape, g.dtype), pltpu.SemaphoreType.DMA((n_dev - 1,)),
                        pltpu.SemaphoreType.DMA((n_dev - 1,))],
    )(g)


def _adamw_update(g_ref, w_ref, m_ref, v_ref, d_o, m_o, v_o):
    c1 = 1.0 - ADAM_B1 ** ADAM_STEP
    c2 = 1.0 - ADAM_B2 ** ADAM_STEP
    g_ = g_ref[...]
    m_new = ADAM_B1 * m_ref[...] + (1.0 - ADAM_B1) * g_
    v_new = ADAM_B2 * v_ref[...] + (1.0 - ADAM_B2) * (g_ * g_)
    m_o[...] = m_new
    v_o[...] = v_new
    d_o[...] = -ADAM_LR * ((m_new / c1) / (jnp.sqrt(v_new / c2) + ADAM_EPS) + ADAM_WD * w_ref[...])


ADAMW_ROWS = 256


def _adamw(name, g, w, m, v):
    _, rows, cols = w.shape
    br = int(np.gcd(ADAMW_ROWS, rows))

    def kern(*refs):
        _adamw_update(*refs)

    spec = pl.BlockSpec((1, br, cols), lambda t: (0, t, 0))
    out = jax.ShapeDtypeStruct(w.shape, F32)
    return pl.pallas_call(kern, name="adamw_" + name, grid=(rows // br,), in_specs=[spec] * 4, out_specs=[spec] * 3,
                          out_shape=[out, out, out], compiler_params=_params(("arbitrary",)))(g, w, m, v)


def _adamw_small(gs, ws, ms, vs):
    n = len(gs)

    def kern(*refs):
        ins, outs = refs[:4 * n], refs[4 * n:]
        for k in range(n):
            _adamw_update(ins[k], ins[n + k], ins[2 * n + k], ins[3 * n + k], outs[k], outs[n + k], outs[2 * n + k])

    vmem = pl.BlockSpec(memory_space=pltpu.VMEM)
    out = [jax.ShapeDtypeStruct(w.shape, F32) for w in ws]
    res = pl.pallas_call(kern, name="adamw_small", in_specs=[vmem] * (4 * n), out_specs=[vmem] * (3 * n),
                         out_shape=out * 3, compiler_params=pltpu.CompilerParams(vmem_limit_bytes=VMEM_LIMIT))(*gs, *ws, *ms, *vs)
    return [(res[k], res[n + k], res[2 * n + k]) for k in range(n)]


def _shard_rows(shape, axis):
    k, n = shape
    return (k * n // N_CHIPS) // PACK_COLS


def _group(names):
    entries = [e for e in SHARDED if e[0] in names]
    used = sum(_shard_rows(shape, axis) for _, shape, axis in entries)
    return entries, -(-used // (2 * REDUCE_ROWS)) * 2 * REDUCE_ROWS


def _pack_shards(shards, names, dtype):
    entries, rows = _group(names)
    parts = [shards[name].astype(dtype).reshape(-1, PACK_COLS) for name, _, _ in entries]
    used = sum(p.shape[0] for p in parts)
    if rows > used:
        parts.append(jnp.zeros((rows - used, PACK_COLS), dtype))
    return jnp.concatenate(parts, 0)


def _unpack_shards(packed, names):
    out, r0 = {}, 0
    for name, (k, n), axis in _group(names)[0]:
        nr = _shard_rows((k, n), axis)
        shape = (k // N_CHIPS, n) if axis == 0 else (k, n // N_CHIPS)
        out[name] = packed[r0:r0 + nr].reshape(shape)
        r0 += nr
    return out


def _unpack_full(gathered, names):
    out, r0 = {}, 0
    for name, (k, n), axis in _group(names)[0]:
        nr = _shard_rows((k, n), axis)
        part = gathered[:, r0:r0 + nr]
        if axis == 0:
            out[name] = part.reshape(k, n)
        else:
            out[name] = part.reshape(N_CHIPS, k, n // N_CHIPS).transpose(1, 0, 2).reshape(k, n)
        r0 += nr
    return out


def _pack_pieces(grads, names):
    entries, rows = _group(names)
    parts = []
    for name, (k, n), axis in entries:
        g = grads[name]
        if axis == 0:
            parts.append(g.reshape(N_CHIPS, -1, PACK_COLS))
        else:
            parts.append(g.reshape(k, N_CHIPS, n // N_CHIPS).transpose(1, 0, 2).reshape(N_CHIPS, -1, PACK_COLS))
    used = sum(p.shape[1] for p in parts)
    if rows > used:
        parts.append(jnp.zeros((N_CHIPS, rows - used, PACK_COLS), F32))
    return jnp.concatenate(parts, 1)


def _pack_small(vals):
    flat = jnp.concatenate([vals[name].astype(F32).reshape(-1) for name, _ in SMALL])
    flat = jnp.concatenate([flat, jnp.zeros((SMALL_ROWS * PACK_COLS - flat.shape[0],), F32)])
    return flat.reshape(SMALL_ROWS, PACK_COLS)


def _unpack_small(packed):
    flat, out, o = packed.reshape(-1), {}, 0
    for name, shape in SMALL:
        n = int(np.prod(shape))
        out[name] = flat[o:o + n].reshape(shape)
        o += n
    return out


def kernel(x, p, positions, g_pre_mix, w_in, b_gate, g_q, w_uq, g_kv, w_ukv, w_pool, pool_scale, w_branch_attn, w_branch_pool, w_out, g_post_mix, g_pre_mlp, w_ff1, w_ff2, g_post_mlp, w_ple_proj, w_ple_gate, g_ple, loss_target, m_g_pre_mix, m_w_in, m_b_gate, m_g_q, m_w_uq, m_g_kv, m_w_ukv, m_w_pool, m_pool_scale, m_w_branch_attn, m_w_branch_pool, m_w_out, m_g_post_mix, m_g_pre_mlp, m_w_ff1, m_w_ff2, m_g_post_mlp, m_w_ple_proj, m_w_ple_gate, m_g_ple, v_g_pre_mix, v_w_in, v_b_gate, v_g_q, v_w_uq, v_g_kv, v_w_ukv, v_w_pool, v_pool_scale, v_w_branch_attn, v_w_branch_pool, v_w_out, v_g_post_mix, v_g_pre_mlp, v_w_ff1, v_w_ff2, v_g_post_mlp, v_w_ple_proj, v_w_ple_gate, v_g_ple):
    given = dict(locals())
    weights = {n: given[n] for n in WEIGHT_ORDER}
    moments_m = {n: given["m_" + n] for n in WEIGHT_ORDER}
    moments_v = {n: given["v_" + n] for n in WEIGHT_ORDER}
    c = lax.axis_index("c")

    big_w = {name: weights[name][0] for name, _, _ in SHARDED}
    my_chip = 2 * lax.axis_index("x") + lax.axis_index("y")
    packed_first = _pack_shards(big_w, FIRST, MX)
    full = _unpack_full(lax.dynamic_update_slice(_allgather_shards(packed_first), packed_first[None], (my_chip, 0, 0)), FIRST)
    for name, _ in SMALL:
        full[name] = weights[name][0] if name == "w_pool" else weights[name]

    loss_cols, dx, grads, (sent_rest, received_rest) = _local_step(
        x[0], p[0, 0], positions[0], loss_target[0], full, (my_chip, c), _pack_shards(big_w, REST, MX))
    loss = lax.psum(0.5 * jnp.sum(loss_cols) / D_MODEL, ("x", "y", "c"))

    def finish(sent, received):
        mine = lax.dynamic_slice(sent, (my_chip, 0, 0), (1,) + sent.shape[1:])
        reduced = _sum_pieces(lax.dynamic_update_slice(received, mine, (my_chip, 0, 0)))
        theirs = _join_halves(reduced)
        return jnp.where(c == 0, jnp.concatenate([reduced, theirs]), jnp.concatenate([theirs, reduced]))

    pieces = _pack_pieces(grads, FIRST)
    sent_first = _add_halves(pieces, _exchange_halves(pieces), c)
    shards = _unpack_shards(finish(sent_first, _scatter_pieces(sent_first)), FIRST)
    shards.update(_unpack_shards(finish(sent_rest, received_rest), REST))
    g_small = _allreduce_small(_pack_small(grads))

    out = {}
    for name, g in shards.items():
        out[name] = (g[None], *_adamw(name, g[None], weights[name], moments_m[name], moments_v[name]))
    small_g = _unpack_small(g_small)
    names = [n for n, _ in SMALL]
    updates = _adamw_small([small_g[n] for n in names], [weights[n] for n in names], [moments_m[n] for n in names],
                           [moments_v[n] for n in names])
    for n, upd in zip(names, updates):
        out[n] = (small_g[n], *upd)
    return (loss, dx[None], *[out[n][k] for k in range(4) for n in WEIGHT_ORDER])
```

```python
import functools

import numpy as np
import jax
import jax.numpy as jnp
from jax import lax
from jax.experimental import pallas as pl
from jax.experimental.pallas import tpu as pltpu

F32 = jnp.float32
MX = jnp.bfloat16
WIRE = jnp.bfloat16

D_MODEL = 1024
N_HEADS = 8
QK_NOPE = 64
QK_ROPE = 32
V_HEAD = 64
Q_LORA = 384
KV_LORA = 256
POOL_WINDOWS = (2, 4, 8, 16)
POOL_GROUP = 128
POOL_WIDTH = 512
D_FF = 4096
PLE_DIM = 256
ROPE_THETA = 10000.0
EPS = 1e-6
HEAD_SLOT = 128
QK_WIDTH = N_HEADS * HEAD_SLOT
ROPE_LANE = 64
SMALL_COLS = Q_LORA + KV_LORA + HEAD_SLOT
IN_PAD = SMALL_COLS + POOL_WIDTH + 2 * D_MODEL
SCALE = (QK_NOPE + QK_ROPE) ** -0.5
LOG2E = 1.4426950408889634
NEG = -1e30
HALO = 16

ADAM_LR = 0.001
ADAM_B1 = 0.9
ADAM_B2 = 0.999
ADAM_EPS = 1e-08
ADAM_WD = 0.01
ADAM_STEP = 10

VMEM_LIMIT = 56 * 2**20
MESH = pl.DeviceIdType.MESH

SHARDED = (
    ("w_in", (1024, 3232), 1),
    ("w_uq", (384, 768), 1),
    ("w_ukv", (256, 1024), 1),
    ("w_branch_attn", (512, 1024), 1),
    ("w_branch_pool", (512, 1024), 1),
    ("w_out", (1024, 1024), 0),
    ("w_ff1", (1024, 4096), 1),
    ("w_ff2", (4096, 1024), 0),
    ("w_ple_proj", (256, 1024), 1),
    ("w_ple_gate", (1024, 1024), 0),
)
SMALL = (
    ("g_pre_mix", (1, 1024)),
    ("b_gate", (1, 2048)),
    ("g_q", (1, 384)),
    ("g_kv", (1, 256)),
    ("w_pool", (1, 4, 128, 128)),
    ("pool_scale", (1, 512)),
    ("g_post_mix", (1, 1024)),
    ("g_pre_mlp", (1, 1024)),
    ("g_post_mlp", (1, 1024)),
    ("g_ple", (1, 1024)),
)
WEIGHT_ORDER = ("g_pre_mix", "w_in", "b_gate", "g_q", "w_uq", "g_kv", "w_ukv", "w_pool", "pool_scale", "w_branch_attn",
                "w_branch_pool", "w_out", "g_post_mix", "g_pre_mlp", "w_ff1", "w_ff2", "g_post_mlp", "w_ple_proj",
                "w_ple_gate", "g_ple")
N_CHIPS = 4
PACK_COLS = 1024
REDUCE_ROWS = 160
SMALL_ROWS = 80
FIRST = ("w_in", "w_uq", "w_ukv")
REST = tuple(name for name, _, _ in SHARDED if name not in FIRST)


def _dot(a, b):
    return jnp.dot(a.astype(MX), b.astype(MX), preferred_element_type=F32)


def _dot_nt(a, b):
    return lax.dot_general(a.astype(MX), b.astype(MX), (((1,), (1,)), ((), ())), preferred_element_type=F32)


def _dot_tn(a, b):
    return lax.dot_general(a.astype(MX), b.astype(MX), (((0,), (0,)), ((), ())), preferred_element_type=F32)


def _sig(x):
    return 1.0 / (1.0 + jnp.exp(-x))


def _rms(x, g):
    r = lax.rsqrt(jnp.mean(x * x, axis=1, keepdims=True) + EPS)
    xh = x * r
    return xh * g, xh, r


def _rms_bwd(xh, r, g, dy):
    dxn = dy * g
    dx = r * (dxn - xh * jnp.mean(dxn * xh, axis=1, keepdims=True))
    return dx, jnp.sum(dy * xh, axis=0, keepdims=True)


def _rot_half(v):
    lane = lax.broadcasted_iota(jnp.int32, v.shape, 1)
    return jnp.where(lane < ROPE_LANE + QK_ROPE // 2, pltpu.roll(v, HEAD_SLOT - QK_ROPE // 2, 1), pltpu.roll(v, QK_ROPE // 2, 1))


def _rope(v, cc, sa):
    return v * cc + _rot_half(v) * sa


def _unrope(v, cc, sa):
    return v * cc - _rot_half(v) * sa


def _params(sem):
    return pltpu.CompilerParams(dimension_semantics=sem, vmem_limit_bytes=VMEM_LIMIT)


def _tok_call(name, body, n_tok, tm, tiled, resident, outs, accs=(), scratch=(), exchange=None):
    def as_pair(t):
        if isinstance(t, tuple):
            return t
        return t, pl.BlockSpec((tm, t.shape[1]), lambda i: (i, 0))
    tiled = [as_pair(t) for t in tiled]
    outs = [as_pair(o) for o in outs]
    res_specs = [pl.BlockSpec(r.shape, lambda i, nd=r.ndim: (0,) * nd, pipeline_mode=pl.Buffered(1)) for r in resident]
    out_specs = [s for _, s in outs] + [pl.BlockSpec(a.shape, lambda i: (0, 0)) for a in accs]
    n_t, n_r, n_o, n_a, n_s = len(tiled), len(resident), len(outs), len(accs), len(scratch)
    n_steps = n_tok // tm
    operands = [a for a, _ in tiled] + list(resident)
    in_specs = [s for _, s in tiled] + res_specs
    out_shape = [o for o, _ in outs] + list(accs)
    scratch = list(scratch)
    if exchange is not None:
        ex_in, ex_out, ex_sems, ex_steps = exchange
        operands.append(ex_in)
        in_specs.append(_HBM)
        out_shape.append(ex_out)
        out_specs.append(_HBM)
        scratch += list(ex_sems)

    def kern(*refs):
        refs = list(refs)
        n_in = n_t + n_r + (exchange is not None)
        n_out = n_o + n_a + (exchange is not None)
        tin, res = refs[:n_t], refs[n_t:n_t + n_r]
        tout = refs[n_in:n_in + n_o]
        acc = refs[n_in + n_o:n_in + n_o + n_a]
        scr = refs[n_in + n_out:n_in + n_out + n_s]
        i = pl.program_id(0)
        if exchange is not None:
            ex_steps(i, n_steps, refs[n_in - 1], refs[n_in + n_out - 1], refs[n_in + n_out + n_s:])

        @pl.when(i == 0)
        def _():
            for a in acc:
                a[...] = jnp.zeros(a.shape, a.dtype)
        body(i, tin, res, tout, acc, scr)

    return pl.pallas_call(
        kern, name=name, grid=(n_steps,), in_specs=in_specs, out_specs=out_specs,
        out_shape=out_shape, scratch_shapes=scratch, compiler_params=_params(("arbitrary",)),
    )(*operands)


def _sds(rows, cols, dtype):
    return jax.ShapeDtypeStruct((rows, cols), dtype)


def _fwd_inproj(x, cc, sa, w, tm, gather=None):
    n_tok = x.shape[0]

    def body(i, tin, res, tout, acc, scr):
        x_ref, c_ref, s_ref = tin
        g_pre, w_in, g_q, w_uq, g_kv, w_k, w_v, e_mat, w_kt, e_t, w_vt, v_ones = res
        a_o, ps_o, u_o, gl_o, qn_o, kvn_o, q_o, k_o, v_o, kt_o, vt_o = tout
        a = _rms(x_ref[...], g_pre[...])[0].astype(MX)
        a_o[...] = a
        ps = _dot(a, w_in[:, :SMALL_COLS])
        ps_o[...] = ps.astype(ps_o.dtype)
        u_o[...] = _dot(a, w_in[:, SMALL_COLS:SMALL_COLS + POOL_WIDTH]).astype(u_o.dtype)
        gl_o[...] = _dot(a, w_in[:, SMALL_COLS + POOL_WIDTH:]).astype(gl_o.dtype)
        cc_, sa_ = c_ref[...], s_ref[...]
        qn = _rms(ps[:, :Q_LORA], g_q[...])[0].astype(MX)
        qn_o[...] = qn
        q = _dot(qn, w_uq[...])
        for h in range(N_HEADS):
            hs = slice(h * HEAD_SLOT, (h + 1) * HEAD_SLOT)
            q_o[:, hs] = (_rope(q[:, hs], cc_, sa_) * (SCALE * LOG2E)).astype(q_o.dtype)
        kvn = _rms(ps[:, Q_LORA:Q_LORA + KV_LORA], g_kv[...])[0].astype(MX)
        kvn_o[...] = kvn
        kr = _rope(ps[:, Q_LORA + KV_LORA:], cc_, sa_)
        k_o[...] = (_dot(kvn, w_k[...]) + _dot(kr, e_mat[...])).astype(k_o.dtype)
        v_o[...] = _dot(kvn, w_v[...]).astype(v_o.dtype)
        kt_o[...] = (_dot_nt(w_kt[...], kvn) + _dot_nt(e_t[...], kr)).astype(kt_o.dtype)
        vt_o[...] = (_dot_nt(w_vt[...], kvn) + v_ones[...]).astype(vt_o.dtype)

    outs = [_sds(n_tok, D_MODEL, MX), _sds(n_tok, SMALL_COLS, MX), _sds(n_tok, POOL_WIDTH, MX), _sds(n_tok, 2 * D_MODEL, MX),
            _sds(n_tok, Q_LORA, MX), _sds(n_tok, KV_LORA, MX), _sds(n_tok, QK_WIDTH, MX), _sds(n_tok, QK_WIDTH, MX),
            _sds(n_tok, N_HEADS * V_HEAD, MX),
            (_sds(QK_WIDTH, n_tok, MX), pl.BlockSpec((QK_WIDTH, tm), lambda i: (0, i))),
            (_sds(N_HEADS * V_ROWS, n_tok, MX), pl.BlockSpec((N_HEADS * V_ROWS, tm), lambda i: (0, i)))]
    res = [w["g_pre_mix"], w["w_in"], w["g_q"], w["w_uq"], w["g_kv"], w["w_k"], w["w_v"], w["e_mat"], w["w_kt"], w["e_t"],
           w["w_vt"], w["v_ones"]]
    exchange = None
    if gather is not None:
        gathered = jax.ShapeDtypeStruct((N_CHIPS,) + gather.shape, gather.dtype)
        exchange = (gather, gathered, [pltpu.SemaphoreType.DMA((6,)), pltpu.SemaphoreType.DMA((6,))], _gather_steps)
    return _tok_call("fwd_inproj", body, n_tok, tm, [x, cc, sa], res, outs, exchange=exchange)


def _causal_pairs(nq, ratio, by_kv):
    if by_kv:
        pairs = [(i, j) for j in range(nq * ratio) for i in range(j // ratio, nq)]
    else:
        pairs = [(i, j) for i in range(nq) for j in range((i + 1) * ratio)]
    return (jnp.asarray(np.array([p[0] for p in pairs], np.int32)), jnp.asarray(np.array([p[1] for p in pairs], np.int32)))


def _keep_t(tk, tq, off):
    return lax.broadcasted_iota(jnp.int32, (tk, tq), 0) + off <= lax.broadcasted_iota(jnp.int32, (tk, tq), 1)


ATTN_FWD_TILE = (1024, 1024)
ATTN_BWD_TILE = (1024, 512)
V_ROWS = 80


def _attn_fwd(q, k, vt, tq, tk):
    n_tok = q.shape[0]
    nq, ratio = n_tok // tq, tq // tk
    qi, kj = _causal_pairs(nq, ratio, by_kv=False)

    def kern(qi_ref, kj_ref, q_ref, k_ref, vt_ref, o_ref, lse_ref, m_s, acc_s, st_s):
        s_id = pl.program_id(0)
        i, j = qi_ref[s_id], kj_ref[s_id]

        @pl.when(j == 0)
        def _():
            m_s[...] = jnp.full(m_s.shape, NEG, F32)
            acc_s[...] = jnp.zeros(acc_s.shape, F32)

        def scores(h):
            hs = slice(h * HEAD_SLOT, (h + 1) * HEAD_SLOT)
            return _dot_nt(k_ref[:, hs], q_ref[:, hs])

        def heads(masked):
            keep = _keep_t(tk, tq, j * tk - i * tq) if masked else None
            st_s[0] = scores(0)
            for h in range(N_HEADS):
                if h + 1 < N_HEADS:
                    st_s[(h + 1) % 2] = scores(h + 1)
                st = st_s[h % 2]
                if masked:
                    st = jnp.where(keep, st, NEG)
                m_old = m_s[h]
                m_new = jnp.maximum(m_old, jnp.max(st, axis=0, keepdims=True))
                pt = jnp.exp2(st - m_new)
                acc_s[h] = jnp.exp2(m_old - m_new) * acc_s[h] + _dot(vt_ref[h * V_ROWS:(h + 1) * V_ROWS, :], pt)
                m_s[h] = m_new

        @pl.when(j < i * ratio)
        def _():
            heads(False)

        @pl.when(j >= i * ratio)
        def _():
            heads(True)

        @pl.when(j == (i + 1) * ratio - 1)
        def _():
            heads_out = []
            for h in range(N_HEADS):
                total = acc_s[h, V_HEAD:V_HEAD + 1, :]
                heads_out.append(acc_s[h, :V_HEAD, :] / total)
                lse_ref[h:h + 1, :] = m_s[h] + jnp.log2(total)
            o_ref[...] = jnp.concatenate(heads_out, 0).T.astype(o_ref.dtype)

    gs = pltpu.PrefetchScalarGridSpec(
        num_scalar_prefetch=2, grid=(qi.shape[0],),
        in_specs=[pl.BlockSpec((tq, QK_WIDTH), lambda s, qi, kj: (qi[s], 0)),
                  pl.BlockSpec((tk, QK_WIDTH), lambda s, qi, kj: (kj[s], 0)),
                  pl.BlockSpec((N_HEADS * V_ROWS, tk), lambda s, qi, kj: (0, kj[s]))],
        out_specs=[pl.BlockSpec((tq, N_HEADS * V_HEAD), lambda s, qi, kj: (qi[s], 0)),
                   pl.BlockSpec((N_HEADS, tq), lambda s, qi, kj: (0, qi[s]))],
        scratch_shapes=[pltpu.VMEM((N_HEADS, 1, tq), F32), pltpu.VMEM((N_HEADS, V_ROWS, tq), F32),
                        pltpu.VMEM((2, tk, tq), F32)])
    return pl.pallas_call(kern, name="attn_fwd", grid_spec=gs,
                          out_shape=[_sds(n_tok, N_HEADS * V_HEAD, MX), _sds(N_HEADS, n_tok, F32)],
                          compiler_params=_params(("arbitrary",)))(qi, kj, q, k, vt)


def _pool_windows(ext, i, tm, first_row):
    row = i * tm + lax.broadcasted_iota(jnp.int32, (tm, 1), 0)
    out = []
    for g, w in enumerate(POOL_WINDOWS):
        cs = slice(g * POOL_GROUP, (g + 1) * POOL_GROUP)
        s = ext[pl.ds(first_row, tm), cs]
        for k in range(1, w):
            s = s + ext[pl.ds(first_row - k, tm), cs]
        cnt = jnp.minimum(row + 1, w).astype(F32)
        out.append(s / cnt)
    return out


def _fwd_mix(x, u, gl, attn, w, tm):
    n_tok = x.shape[0]
    halo_spec = pl.BlockSpec((HALO, POOL_WIDTH), lambda i: (jnp.maximum(i * (tm // HALO) - 1, 0), 0))

    def body(i, tin, res, tout, acc, scr):
        x_ref, u_ref, uh_ref, gl_ref, at_ref = tin
        w_pool, pool_scale, w_ba, w_bp, b_gate, w_out, g_post = res
        d_o, pooled_o, a_o, pp_o, merged_o, y_o, h1_o = tout
        ext, = scr
        ext[pl.ds(0, HALO), :] = jnp.where(i > 0, uh_ref[...].astype(F32), 0.0)
        ext[pl.ds(HALO, tm), :] = u_ref[...].astype(F32)
        means = _pool_windows(ext, i, tm, HALO)
        for g in range(len(POOL_WINDOWS)):
            cs = slice(g * POOL_GROUP, (g + 1) * POOL_GROUP)
            d = (means[g] - ext[pl.ds(HALO, tm), cs]).astype(MX)
            d_o[:, cs] = d
            pooled_o[:, cs] = (_dot(d, w_pool[g]) * pool_scale[:, cs]).astype(pooled_o.dtype)
        a_br = _dot(at_ref[...], w_ba[...])
        p_br = _dot(pooled_o[...], w_bp[...])
        a_o[...] = a_br.astype(a_o.dtype)
        pp_o[...] = p_br.astype(pp_o.dtype)
        gates = _sig(gl_ref[...].astype(F32) + b_gate[...])
        merged = (gates[:, :D_MODEL] * a_br + gates[:, D_MODEL:] * p_br).astype(MX)
        merged_o[...] = merged
        y = _dot(merged, w_out[...])
        y_o[...] = y.astype(y_o.dtype)
        h1_o[...] = x_ref[...] + _rms(y, g_post[...])[0]

    outs = [_sds(n_tok, POOL_WIDTH, MX), _sds(n_tok, POOL_WIDTH, MX), _sds(n_tok, D_MODEL, MX), _sds(n_tok, D_MODEL, MX),
            _sds(n_tok, D_MODEL, MX), _sds(n_tok, D_MODEL, MX), _sds(n_tok, D_MODEL, F32)]
    res = [w["w_pool"], w["pool_scale"], w["w_branch_attn"], w["w_branch_pool"], w["b_gate"], w["w_out"], w["g_post_mix"]]
    return _tok_call("fwd_mix", body, n_tok, tm, [x, u, (u, halo_spec), gl, attn], res, outs,
                     scratch=[pltpu.VMEM((tm + HALO, POOL_WIDTH), F32)])


def _fwd_mlp(h1, w, tm):
    n_tok = h1.shape[0]

    def body(i, tin, res, tout, acc, scr):
        h1_ref, = tin
        g_pre, w1, w2, g_post = res
        m_o, zr_o, a2_o, f_o, h2_o = tout
        h1_ = h1_ref[...]
        m = _rms(h1_, g_pre[...])[0].astype(MX)
        m_o[...] = m
        zr = jnp.maximum(_dot(m, w1[...]), 0.0)
        zr_o[...] = zr.astype(zr_o.dtype)
        a2 = (zr * zr).astype(MX)
        a2_o[...] = a2
        f = _dot(a2, w2[...])
        f_o[...] = f.astype(f_o.dtype)
        h2_o[...] = h1_ + _rms(f, g_post[...])[0]

    outs = [_sds(n_tok, D_MODEL, MX), _sds(n_tok, D_FF, MX), _sds(n_tok, D_FF, MX), _sds(n_tok, D_MODEL, MX),
            _sds(n_tok, D_MODEL, F32)]
    res = [w["g_pre_mlp"], w["w_ff1"], w["w_ff2"], w["g_post_mlp"]]
    return _tok_call("fwd_mlp", body, n_tok, tm, [h1], res, outs)


def _ple_fwd_bwd(h2, p, target, w, tm):
    n_tok = h2.shape[0]

    def body(i, tin, res, tout, acc, scr):
        h2_ref, p_ref, t_ref = tin
        w_pe, w_pg, g_ple = res
        dh2_o, de_o, dzg_o = tout
        loss_a, dg_a = acc
        h2_ = h2_ref[...]
        e = _dot(p_ref[...], w_pe[...])
        pg = _sig(_dot(h2_, w_pg[...]))
        t = pg * e
        g = g_ple[...]
        tn, th, r = _rms(t, g)
        diff = h2_ + tn - t_ref[...]
        loss_a[...] += jnp.sum(diff * diff, axis=0, keepdims=True)
        dh3 = diff * (1.0 / D_MODEL)
        dt, dg = _rms_bwd(th, r, g, dh3)
        dg_a[...] += dg
        de_o[...] = (dt * pg).astype(de_o.dtype)
        dzg = (dt * e * pg * (1.0 - pg)).astype(MX)
        dzg_o[...] = dzg
        dh2_o[...] = dh3 + _dot_nt(dzg, w_pg[...])

    outs = [_sds(n_tok, D_MODEL, F32), _sds(n_tok, D_MODEL, MX), _sds(n_tok, D_MODEL, MX)]
    accs = [_sds(1, D_MODEL, F32), _sds(1, D_MODEL, F32)]
    return _tok_call("ple_fwd_bwd", body, n_tok, tm, [h2, p, target], [w["w_ple_proj"], w["w_ple_gate"], w["g_ple"]], outs, accs)


def _bwd_mlp(dh2, f, h1, zr, w, tm):
    n_tok = dh2.shape[0]

    def body(i, tin, res, tout, acc, scr):
        dh2_ref, f_ref, h1_ref, zr_ref = tin
        g_pre, w1, w2, g_post = res
        df_o, dz_o, dh1_o = tout
        dg_post_a, dg_pre_a = acc
        dh2_ = dh2_ref[...]
        gp = g_post[...]
        _, fh, rf = _rms(f_ref[...].astype(F32), gp)
        df, dg = _rms_bwd(fh, rf, gp, dh2_)
        dg_post_a[...] += dg
        df = df.astype(MX)
        df_o[...] = df
        dz = (_dot_nt(df, w2[...]) * (2.0 * zr_ref[...].astype(F32))).astype(MX)
        dz_o[...] = dz
        dm = _dot_nt(dz, w1[...])
        gq = g_pre[...]
        _, hh, rh = _rms(h1_ref[...], gq)
        dh1, dg = _rms_bwd(hh, rh, gq, dm)
        dg_pre_a[...] += dg
        dh1_o[...] = dh2_ + dh1

    outs = [_sds(n_tok, D_MODEL, MX), _sds(n_tok, D_FF, MX), _sds(n_tok, D_MODEL, F32)]
    accs = [_sds(1, D_MODEL, F32), _sds(1, D_MODEL, F32)]
    res = [w["g_pre_mlp"], w["w_ff1"], w["w_ff2"], w["g_post_mlp"]]
    return _tok_call("bwd_mlp", body, n_tok, tm, [dh2, f, h1, zr], res, outs, accs)


def _bwd_mix(dh1, y, a_br, p_br, gl, attn, d, w, tm):
    n_tok = dh1.shape[0]

    def body(i, tin, res, tout, acc, scr):
        dh1_ref, y_ref, a_ref, pp_ref, gl_ref, at_ref, d_ref = tin
        g_post, w_out, b_gate, w_ba, w_bp, w_pool, pool_scale, sel = res
        dy_o, da_o, dpp_o, dgl_o, do_o, delta_o, dyp_o, dd_o = tout
        dg_post_a, db_a, dps_a = acc
        g = g_post[...]
        _, yh, r = _rms(y_ref[...].astype(F32), g)
        dy, dg = _rms_bwd(yh, r, g, dh1_ref[...])
        dg_post_a[...] += dg
        dy = dy.astype(MX)
        dy_o[...] = dy
        dmerged = _dot_nt(dy, w_out[...])
        gates = _sig(gl_ref[...].astype(F32) + b_gate[...])
        ga, gp = gates[:, :D_MODEL], gates[:, D_MODEL:]
        da = (dmerged * ga).astype(MX)
        dpp = (dmerged * gp).astype(MX)
        da_o[...] = da
        dpp_o[...] = dpp
        dgl_a = dmerged * a_ref[...].astype(F32) * ga * (1.0 - ga)
        dgl_p = dmerged * pp_ref[...].astype(F32) * gp * (1.0 - gp)
        dgl_o[:, :D_MODEL] = dgl_a.astype(dgl_o.dtype)
        dgl_o[:, D_MODEL:] = dgl_p.astype(dgl_o.dtype)
        db_a[:, :D_MODEL] += jnp.sum(dgl_a, axis=0, keepdims=True)
        db_a[:, D_MODEL:] += jnp.sum(dgl_p, axis=0, keepdims=True)
        do = _dot_nt(da, w_ba[...]).astype(MX)
        do_o[...] = do
        prod = do.astype(F32) * at_ref[...].astype(F32)
        hi = prod.astype(MX)
        lo = (prod - hi.astype(F32)).astype(MX)
        delta_o[...] = _dot(hi, sel[...]) + _dot(lo, sel[...])
        dpooled = _dot_nt(dpp, w_bp[...])
        for gi in range(len(POOL_WINDOWS)):
            cs = slice(gi * POOL_GROUP, (gi + 1) * POOL_GROUP)
            ypre = _dot(d_ref[:, cs], w_pool[gi])
            dps_a[:, cs] += jnp.sum(dpooled[:, cs] * ypre, axis=0, keepdims=True)
            dyp = (dpooled[:, cs] * pool_scale[:, cs]).astype(MX)
            dyp_o[:, cs] = dyp
            dd_o[:, cs] = _dot_nt(dyp, w_pool[gi])

    outs = [_sds(n_tok, D_MODEL, MX), _sds(n_tok, D_MODEL, MX), _sds(n_tok, D_MODEL, MX), _sds(n_tok, 2 * D_MODEL, MX),
            _sds(n_tok, N_HEADS * V_HEAD, MX), _sds(n_tok, HEAD_SLOT, F32), _sds(n_tok, POOL_WIDTH, MX),
            _sds(n_tok, POOL_WIDTH, F32)]
    accs = [_sds(1, D_MODEL, F32), _sds(1, 2 * D_MODEL, F32), _sds(1, POOL_WIDTH, F32)]
    res = [w["g_post_mix"], w["w_out"], w["b_gate"], w["w_branch_attn"], w["w_branch_pool"], w["w_pool"], w["pool_scale"],
           w["head_sel"]]
    return _tok_call("bwd_mix", body, n_tok, tm, [dh1, y, a_br, p_br, gl, attn, d], res, outs, accs)


def _bwd_heads(q_ref, k_ref, v_ref, do_ref, lse_ref, dl_ref, st_s, dpt_s, keep, use, n_heads):
    def products(h):
        hs = slice(h * HEAD_SLOT, (h + 1) * HEAD_SLOT)
        vs = slice(h * V_HEAD, (h + 1) * V_HEAD)
        st_s[h % 2] = _dot_nt(k_ref[:, hs], q_ref[:, hs])
        dpt_s[h % 2] = _dot_nt(v_ref[:, vs], do_ref[:, vs])

    products(0)
    for h in range(n_heads):
        if h + 1 < n_heads:
            products(h + 1)
        st = st_s[h % 2]
        if keep is not None:
            st = jnp.where(keep, st, NEG)
        pt = jnp.exp2(st - lse_ref[h:h + 1, :])
        use(h, pt, pt * (dpt_s[h % 2] - dl_ref[h:h + 1, :]))


HEAD_GROUP = 4


def _attn_bwd(q, k, kt, v, do, lse, delta, tq, tk, scatter=None):
    n_tok = q.shape[0]
    nq, ratio = n_tok // tq, tq // tk
    n_groups = N_HEADS // HEAD_GROUP
    gq, gv = HEAD_GROUP * HEAD_SLOT, HEAD_GROUP * V_HEAD
    qi, kj = _causal_pairs(nq, ratio, by_kv=True)

    n_pairs = qi.shape[0]

    def kern(qi_ref, kj_ref, q_ref, k_ref, kt_ref, v_ref, do_ref, lse_ref, dl_ref, *rest):
        if scatter is not None:
            s_hbm, dq_ref, dk_ref, dv_ref, r_hbm, dk_s, dv_s, st_s, dpt_s, send_sems, recv_sems = rest
        else:
            dq_ref, dk_ref, dv_ref, dk_s, dv_s, st_s, dpt_s = rest
        s_id = pl.program_id(1)
        i, j = qi_ref[s_id], kj_ref[s_id]
        cols = pl.ds(pl.multiple_of(i * tq, tq), tq)
        if scatter is not None:
            group = pl.program_id(0)
            _scatter_steps(jnp.logical_and(group == 0, s_id == 0),
                           jnp.logical_and(group == n_groups - 1, s_id == n_pairs - 1), s_hbm, r_hbm, (send_sems, recv_sems))

        @pl.when(s_id == 0)
        def _():
            dq_ref[...] = jnp.zeros(dq_ref.shape, F32)

        def use(h, pt, dst):
            hs = slice(h * HEAD_SLOT, (h + 1) * HEAD_SLOT)
            dv_s[h] += _dot(pt, do_ref[:, h * V_HEAD:(h + 1) * V_HEAD])
            dk_s[:, hs] += _dot(dst, q_ref[:, hs])
            dq_ref[hs, cols] += _dot(kt_ref[hs, :], dst)

        def heads(masked):
            keep = _keep_t(tk, tq, j * tk - i * tq) if masked else None
            _bwd_heads(q_ref, k_ref, v_ref, do_ref, lse_ref.at[0], dl_ref.at[0], st_s, dpt_s, keep, use, HEAD_GROUP)

        @pl.when(j >= i * ratio)
        def _():
            dk_s[...] = jnp.zeros(dk_s.shape, F32)
            dv_s[...] = jnp.zeros(dv_s.shape, F32)
            heads(True)

        @pl.when(j < i * ratio)
        def _():
            heads(False)

        @pl.when(i == nq - 1)
        def _():
            dk_ref[...] = (dk_s[...] * (1.0 / LOG2E)).astype(dk_ref.dtype)
            for h in range(HEAD_GROUP):
                dv_ref[:, h * V_HEAD:(h + 1) * V_HEAD] = dv_s[h].astype(dv_ref.dtype)

    at_q = lambda g, s, qi, kj: (qi[s], g)
    at_k = lambda g, s, qi, kj: (kj[s], g)
    at_kt = lambda g, s, qi, kj: (g, kj[s])
    at_stat = lambda g, s, qi, kj: (g, 0, qi[s])
    in_specs = [pl.BlockSpec((tq, gq), at_q), pl.BlockSpec((tk, gq), at_k), pl.BlockSpec((gq, tk), at_kt),
                pl.BlockSpec((tk, gv), at_k), pl.BlockSpec((tq, gv), at_q),
                pl.BlockSpec((1, HEAD_GROUP, tq), at_stat), pl.BlockSpec((1, HEAD_GROUP, tq), at_stat)]
    out_specs = [pl.BlockSpec((gq, n_tok), lambda g, s, qi, kj: (g, 0), pipeline_mode=pl.Buffered(1)),
                 pl.BlockSpec((tk, gq), at_k), pl.BlockSpec((tk, gv), at_k)]
    out_shape = [_sds(QK_WIDTH, n_tok, F32), _sds(n_tok, QK_WIDTH, MX), _sds(n_tok, N_HEADS * V_HEAD, MX)]
    scratch = [pltpu.VMEM((tk, gq), F32), pltpu.VMEM((HEAD_GROUP, tk, V_HEAD), F32),
               pltpu.VMEM((2, tk, tq), F32), pltpu.VMEM((2, tk, tq), F32)]
    stat3 = lambda a: a.reshape(n_groups, HEAD_GROUP, n_tok)
    operands = [qi, kj, q, k, kt, v, do, stat3(lse), stat3(delta)]
    if scatter is not None:
        operands.append(scatter)
        in_specs.append(_HBM)
        out_specs.append(_HBM)
        out_shape.append(jax.ShapeDtypeStruct(scatter.shape, scatter.dtype))
        scratch += [pltpu.SemaphoreType.DMA((3,)), pltpu.SemaphoreType.DMA((3,))]
    gs = pltpu.PrefetchScalarGridSpec(num_scalar_prefetch=2, grid=(n_groups, n_pairs), in_specs=in_specs,
                                      out_specs=out_specs, scratch_shapes=scratch)
    return pl.pallas_call(kern, name="attn_bwd", grid_spec=gs, out_shape=out_shape,
                          compiler_params=_params(("arbitrary", "arbitrary")))(*operands)


def _bwd_inproj(dq_t, dk, dv, dd, dgl, ps, x, dh1, cc, sa, w, tm):
    n_tok = x.shape[0]
    n_tiles = n_tok // tm
    last_halo = n_tok // HALO - 1
    halo_spec = pl.BlockSpec((HALO, POOL_WIDTH), lambda i: (jnp.minimum((i + 1) * (tm // HALO), last_halo), 0))

    def body(i, tin, res, tout, acc, scr):
        dq_ref, dk_ref, dv_ref, dd_ref, ddh_ref, dgl_ref, ps_ref, x_ref, dh1_ref, c_ref, s_ref = tin
        w_uq, g_q, w_k, w_v, e_mat, g_kv, w_in, g_pre = res
        dqu_o, dproj_o, dx_o = tout
        dgq_a, dgkv_a, dgpre_a = acc
        ext, = scr
        cc_, sa_ = c_ref[...], s_ref[...]
        for h in range(N_HEADS):
            hs = slice(h * HEAD_SLOT, (h + 1) * HEAD_SLOT)
            dqu_o[:, hs] = (_unrope(dq_ref[hs, :].T, cc_, sa_) * SCALE).astype(dqu_o.dtype)
        gq = g_q[...]
        _, qh, rq = _rms(ps_ref[:, :Q_LORA].astype(F32), gq)
        dqd, dg = _rms_bwd(qh, rq, gq, _dot_nt(dqu_o[...], w_uq[...]))
        dgq_a[...] += dg
        dproj_o[:, :Q_LORA] = dqd.astype(dproj_o.dtype)
        gkv = g_kv[...]
        _, kh, rk = _rms(ps_ref[:, Q_LORA:Q_LORA + KV_LORA].astype(F32), gkv)
        dkvd, dg = _rms_bwd(kh, rk, gkv, _dot_nt(dk_ref[...], w_k[...]) + _dot_nt(dv_ref[...], w_v[...]))
        dgkv_a[...] += dg
        dproj_o[:, Q_LORA:Q_LORA + KV_LORA] = dkvd.astype(dproj_o.dtype)
        dproj_o[:, Q_LORA + KV_LORA:SMALL_COLS] = _unrope(_dot_nt(dk_ref[...], e_mat[...]), cc_, sa_).astype(dproj_o.dtype)
        row = i * tm + lax.broadcasted_iota(jnp.int32, (tm + HALO, 1), 0)
        for gi, wdw in enumerate(POOL_WINDOWS):
            cs = slice(gi * POOL_GROUP, (gi + 1) * POOL_GROUP)
            inv = 1.0 / jnp.minimum(row + 1, wdw).astype(F32)
            ext[pl.ds(0, tm), cs] = dd_ref[:, cs] * inv[:tm]
            ext[pl.ds(tm, HALO), cs] = jnp.where(i < n_tiles - 1, ddh_ref[:, cs] * inv[tm:], 0.0)
            s = ext[pl.ds(0, tm), cs]
            for k_ in range(1, wdw):
                s = s + ext[pl.ds(k_, tm), cs]
            dproj_o[:, SMALL_COLS + gi * POOL_GROUP:SMALL_COLS + (gi + 1) * POOL_GROUP] = (s - dd_ref[:, cs]).astype(dproj_o.dtype)
        dproj_o[:, SMALL_COLS + POOL_WIDTH:] = dgl_ref[...]
        da = _dot_nt(dproj_o[...], w_in[...])
        gp = g_pre[...]
        _, xh, rx = _rms(x_ref[...], gp)
        dx, dg = _rms_bwd(xh, rx, gp, da)
        dgpre_a[...] += dg
        dx_o[...] = dh1_ref[...] + dx

    outs = [_sds(n_tok, QK_WIDTH, MX), _sds(n_tok, IN_PAD, MX), _sds(n_tok, D_MODEL, F32)]
    accs = [_sds(1, Q_LORA, F32), _sds(1, KV_LORA, F32), _sds(1, D_MODEL, F32)]
    res = [w["w_uq"], w["g_q"], w["w_k"], w["w_v"], w["e_mat"], w["g_kv"], w["w_in"], w["g_pre_mix"]]
    dq_spec = pl.BlockSpec((QK_WIDTH, tm), lambda i: (0, i))
    return _tok_call("bwd_inproj", body, n_tok, tm, [(dq_t, dq_spec), dk, dv, dd, (dd, halo_spec), dgl, ps, x, dh1, cc, sa], res, outs, accs,
                     scratch=[pltpu.VMEM((tm + HALO, POOL_WIDTH), F32)])


XTDY_TOKENS = 1024
XTDY_OUT_BYTES = 8 * 2**20


def _xtdy(name, x, dy):
    n_tok, kk = x.shape
    nn = dy.shape[1]
    bt = min(XTDY_TOKENS, n_tok)
    bk = kk
    while bk * nn * 4 > XTDY_OUT_BYTES and bk % 256 == 0:
        bk //= 2

    def kern(x_ref, dy_ref, o_ref):
        @pl.when(pl.program_id(1) == 0)
        def _():
            o_ref[...] = jnp.zeros(o_ref.shape, F32)
        o_ref[...] += _dot_tn(x_ref[...], dy_ref[...])

    return pl.pallas_call(
        kern, name=name, grid=(kk // bk, n_tok // bt),
        in_specs=[pl.BlockSpec((bt, bk), lambda a, t: (t, a)), pl.BlockSpec((bt, nn), lambda a, t: (t, 0))],
        out_specs=pl.BlockSpec((bk, nn), lambda a, t: (a, 0)), out_shape=_sds(kk, nn, F32),
        compiler_params=_params(("arbitrary", "arbitrary")))(x, dy)


def _rope_tables(positions):
    inv_freq = ROPE_THETA ** (-jnp.arange(0, QK_ROPE, 2, dtype=F32) / QK_ROPE)
    ang_t = inv_freq[:, None] * positions.astype(F32)[None, :]
    cos, sin = jnp.cos(ang_t).T, jnp.sin(ang_t).T
    n_tok = positions.shape[0]
    ones, z64 = jnp.ones((n_tok, ROPE_LANE), F32), jnp.zeros((n_tok, ROPE_LANE), F32)
    z32 = jnp.zeros((n_tok, HEAD_SLOT - ROPE_LANE - QK_ROPE), F32)
    return jnp.concatenate([ones, cos, cos, z32], 1), jnp.concatenate([z64, -sin, sin, z32], 1)


def _kernel_weights(full):
    w_in, w_uq, w_ukv = full["w_in"], full["w_uq"], full["w_ukv"]
    c0 = Q_LORA + KV_LORA
    z = lambda n: jnp.zeros((D_MODEL, n), w_in.dtype)
    w = dict(full)
    w["w_in"] = jnp.concatenate([w_in[:, :c0], z(ROPE_LANE), w_in[:, c0:c0 + QK_ROPE], z(HEAD_SLOT - ROPE_LANE - QK_ROPE),
                                 w_in[:, c0 + QK_ROPE:]], 1)
    w["w_uq"] = jnp.pad(w_uq.reshape(Q_LORA, N_HEADS, QK_NOPE + QK_ROPE),
                        ((0, 0), (0, 0), (0, HEAD_SLOT - QK_NOPE - QK_ROPE))).reshape(Q_LORA, QK_WIDTH)
    kv = w_ukv.reshape(KV_LORA, N_HEADS, QK_NOPE + V_HEAD)
    w["w_k"] = jnp.pad(kv[:, :, :QK_NOPE], ((0, 0), (0, 0), (0, HEAD_SLOT - QK_NOPE))).reshape(KV_LORA, QK_WIDTH)
    w["w_v"] = kv[:, :, QK_NOPE:].reshape(KV_LORA, N_HEADS * V_HEAD)
    e = np.zeros((HEAD_SLOT, QK_WIDTH), np.float32)
    sel = np.zeros((N_HEADS * V_HEAD, HEAD_SLOT), np.float32)
    for h in range(N_HEADS):
        for r in range(QK_ROPE):
            e[ROPE_LANE + r, h * HEAD_SLOT + ROPE_LANE + r] = 1.0
        sel[h * V_HEAD:(h + 1) * V_HEAD, h] = 1.0
    w["e_mat"] = jnp.asarray(e, MX)
    w["w_kt"], w["e_t"] = w["w_k"].T, jnp.asarray(e.T, MX)
    pad = ((0, 0), (0, V_ROWS - V_HEAD), (0, 0))
    w["w_vt"] = jnp.pad(w["w_v"].T.reshape(N_HEADS, V_HEAD, KV_LORA), pad).reshape(N_HEADS * V_ROWS, KV_LORA)
    ones = np.zeros((N_HEADS, V_ROWS, 1), np.float32)
    ones[:, V_HEAD] = 1.0
    w["v_ones"] = jnp.asarray(ones.reshape(N_HEADS * V_ROWS, 1))
    w["head_sel"] = jnp.asarray(sel, MX)
    w["w_pool"] = full["w_pool"].astype(MX)
    return w


def _local_step(x, p, positions, target, full, mesh_place=None, packed_rest=None):
    n_tok = x.shape[0]
    tm = min(512, n_tok)
    tm_mlp = min(256, n_tok)
    fwd_tile = [min(t, n_tok) for t in ATTN_FWD_TILE]
    bwd_tile = [min(t, n_tok) for t in ATTN_BWD_TILE]
    w = _kernel_weights(full)
    cc, sa = _rope_tables(positions)

    if mesh_place is None:
        a, ps, u, gl, qn, kvn, q, k, v, kt, vt = _fwd_inproj(x, cc, sa, w, tm)
    else:
        my_chip, core = mesh_place
        a, ps, u, gl, qn, kvn, q, k, v, kt, vt, gathered = _fwd_inproj(x, cc, sa, w, tm, gather=packed_rest)
        w.update(_unpack_full(lax.dynamic_update_slice(gathered, packed_rest[None], (my_chip, 0, 0)), REST))
    attn, lse = _attn_fwd(q, k, vt, *fwd_tile)
    d, pooled, a_br, p_br, merged, y, h1 = _fwd_mix(x, u, gl, attn, w, tm)
    m, zr, a2, f, h2 = _fwd_mlp(h1, w, tm_mlp)
    dh2, de, dzg, loss_cols, dg_ple = _ple_fwd_bwd(h2, p, target, w, tm)
    df, dz, dh1, dg_post_mlp, dg_pre_mlp = _bwd_mlp(dh2, f, h1, zr, w, tm_mlp)
    dy, da_br, dp_br, dgl, do, delta, dyp, dd, dg_post_mix, db_gate, dpool_scale = _bwd_mix(dh1, y, a_br, p_br, gl, attn, d, w, tm)
    grads = {"w_branch_attn": _xtdy("dw_ba", attn, da_br), "w_branch_pool": _xtdy("dw_bp", pooled, dp_br),
             "w_out": _xtdy("dw_out", merged, dy), "w_ff1": _xtdy("dw_ff1", m, dz), "w_ff2": _xtdy("dw_ff2", a2, df),
             "w_ple_proj": _xtdy("dw_pe", p, de), "w_ple_gate": _xtdy("dw_pg", h2, dzg)}
    delta_t = delta[:, :N_HEADS].T
    if mesh_place is None:
        travelling = None
        dq_t, dk, dv = _attn_bwd(q, k, kt, v, do, lse, delta_t, *bwd_tile)
    else:
        pieces = _pack_pieces(grads, REST)
        sent = _add_halves(pieces, _exchange_halves(pieces), core)
        dq_t, dk, dv, received = _attn_bwd(q, k, kt, v, do, lse, delta_t, *bwd_tile, scatter=sent)
        travelling = (sent, received)
        grads = {}
    dqu, dproj, dx, dg_q, dg_kv, dg_pre_mix = _bwd_inproj(dq_t, dk, dv, dd, dgl, ps, x, dh1, cc, sa, w, tm)

    g_in = _xtdy("dw_in", a, dproj)
    g_uq = _xtdy("dw_uq", qn, dqu)
    g_k = _xtdy("dw_k", kvn, dk)
    g_v = _xtdy("dw_v", kvn, dv)
    g_pool = _xtdy("dw_pool", d, dyp)

    c0 = Q_LORA + KV_LORA
    grads.update({
        "g_pre_mix": dg_pre_mix,
        "w_in": jnp.concatenate([g_in[:, :c0], g_in[:, c0 + ROPE_LANE:c0 + ROPE_LANE + QK_ROPE], g_in[:, SMALL_COLS:]], 1),
        "b_gate": db_gate,
        "g_q": dg_q,
        "w_uq": g_uq.reshape(Q_LORA, N_HEADS, HEAD_SLOT)[:, :, :QK_NOPE + QK_ROPE].reshape(Q_LORA, N_HEADS * (QK_NOPE + QK_ROPE)),
        "g_kv": dg_kv,
        "w_ukv": jnp.concatenate([g_k.reshape(KV_LORA, N_HEADS, HEAD_SLOT)[:, :, :QK_NOPE],
                                  g_v.reshape(KV_LORA, N_HEADS, V_HEAD)], 2).reshape(KV_LORA, N_HEADS * (QK_NOPE + V_HEAD)),
        "w_pool": jnp.stack([g_pool[g * POOL_GROUP:(g + 1) * POOL_GROUP, g * POOL_GROUP:(g + 1) * POOL_GROUP]
                             for g in range(len(POOL_WINDOWS))]),
        "pool_scale": dpool_scale,
        "g_post_mix": dg_post_mix,
        "g_pre_mlp": dg_pre_mlp,
        "g_post_mlp": dg_post_mlp,
        "g_ple": dg_ple,
    })
    return loss_cols, dx, grads, travelling


def _place():
    return lax.axis_index("x"), lax.axis_index("y"), lax.axis_index("c")


CHIP_FLIPS = ((1, 0), (0, 1), (1, 1))


def _flip(x, y, fx, fy):
    return (1 - x if fx else x), (1 - y if fy else y)


_HBM = pl.BlockSpec(memory_space=pl.ANY)


def _gather_copies(w_ref, out_ref, send_sems, recv_sems):
    half = w_ref.shape[0] // 2
    x, y, c = _place()
    my_chip = 2 * x + y
    sibling = (x, y, 1 - c)

    def half_of(chip, hc):
        return out_ref.at[chip, pl.ds(pl.multiple_of(hc * half, 16), half), :]

    src = w_ref.at[pl.ds(pl.multiple_of(c * half, 16), half), :]
    sends, landed, forwards, from_sibling = [], [], [], []
    for j, (fx, fy) in enumerate(CHIP_FLIPS):
        px, py = _flip(x, y, fx, fy)
        mine_there, theirs_here, theirs_other = half_of(my_chip, c), half_of(2 * px + py, c), half_of(2 * px + py, 1 - c)
        sends.append(pltpu.make_async_remote_copy(src, mine_there, send_sems.at[j], recv_sems.at[j],
                                                  device_id=(px, py, c), device_id_type=MESH))
        landed.append(pltpu.make_async_remote_copy(src, theirs_here, send_sems.at[j], recv_sems.at[j],
                                                   device_id=(px, py, c), device_id_type=MESH))
        forwards.append(pltpu.make_async_remote_copy(theirs_here, theirs_here, send_sems.at[3 + j], recv_sems.at[3 + j],
                                                     device_id=sibling, device_id_type=MESH))
        from_sibling.append(pltpu.make_async_remote_copy(theirs_other, theirs_other, send_sems.at[3 + j],
                                                         recv_sems.at[3 + j], device_id=sibling, device_id_type=MESH))
    return sends, landed, forwards, from_sibling


def _gather_steps(i, n_steps, w_ref, out_ref, sems):
    sends, landed, forwards, from_sibling = _gather_copies(w_ref, out_ref, *sems)

    @pl.when(i == 0)
    def _():
        for cp in sends:
            cp.start()

    @pl.when(i == (3 * n_steps) // 4)
    def _():
        for arrived, fwd in zip(landed, forwards):
            arrived.wait_recv()
            fwd.start()

    @pl.when(i == n_steps - 1)
    def _():
        for cp in from_sibling:
            cp.wait_recv()
        for cp in sends + forwards:
            cp.wait_send()


def _allgather_shards(wp):
    def body(w_ref, out_ref, send_sems, recv_sems):
        sends, landed, forwards, from_sibling = _gather_copies(w_ref, out_ref, send_sems, recv_sems)
        for cp in sends:
            cp.start()
        for arrived, fwd in zip(landed, forwards):
            arrived.wait_recv()
            fwd.start()
        for cp in from_sibling:
            cp.wait_recv()
        for cp in sends + forwards:
            cp.wait_send()

    return pl.pallas_call(
        body, name="allgather_shards", out_shape=jax.ShapeDtypeStruct((N_CHIPS,) + wp.shape, wp.dtype),
        in_specs=[_HBM], out_specs=_HBM,
        scratch_shapes=[pltpu.SemaphoreType.DMA((6,)), pltpu.SemaphoreType.DMA((6,))],
    )(wp)


def _exchange_halves(g):
    rows = g.shape[1]
    half = rows // 2

    def body(g_ref, r_ref, send_sem, recv_sem):
        x, y, c = _place()
        src = g_ref.at[:, pl.ds(pl.multiple_of((1 - c) * half, 8), half), :]
        cp = pltpu.make_async_remote_copy(src, r_ref, send_sem, recv_sem, device_id=(x, y, 1 - c), device_id_type=MESH)
        cp.start()
        cp.wait()

    return pl.pallas_call(
        body, name="exchange_halves", out_shape=jax.ShapeDtypeStruct((N_CHIPS, half, PACK_COLS), g.dtype),
        in_specs=[_HBM], out_specs=_HBM, scratch_shapes=[pltpu.SemaphoreType.DMA, pltpu.SemaphoreType.DMA],
    )(g)


def _add_halves(g, r, c):
    rows = g.shape[1]
    half = rows // 2
    br = REDUCE_ROWS
    nb = half // br

    def kern(c_ref, g_ref, r_ref, o_ref):
        o_ref[...] = (g_ref[...] + r_ref[...]).astype(o_ref.dtype)

    gs = pltpu.PrefetchScalarGridSpec(
        num_scalar_prefetch=1, grid=(N_CHIPS, nb),
        in_specs=[pl.BlockSpec((1, br, PACK_COLS), lambda k, t, c: (k, c[0] * nb + t, 0)),
                  pl.BlockSpec((1, br, PACK_COLS), lambda k, t, c: (k, t, 0))],
        out_specs=pl.BlockSpec((1, br, PACK_COLS), lambda k, t, c: (k, t, 0)))
    return pl.pallas_call(kern, name="add_halves", grid_spec=gs,
                          out_shape=jax.ShapeDtypeStruct((N_CHIPS, half, PACK_COLS), WIRE),
                          compiler_params=_params(("arbitrary", "arbitrary")))(c.reshape(1), g, r)


def _scatter_copies(s_ref, r_ref, send_sems, recv_sems):
    x, y, c = _place()
    my_chip = 2 * x + y
    sends, arrivals = [], []
    for j, (fx, fy) in enumerate(CHIP_FLIPS):
        px, py = _flip(x, y, fx, fy)
        slot = r_ref.at[2 * px + py]
        sends.append(pltpu.make_async_remote_copy(s_ref.at[2 * px + py], r_ref.at[my_chip], send_sems.at[j], recv_sems.at[j],
                                                  device_id=(px, py, c), device_id_type=MESH))
        arrivals.append(pltpu.make_async_remote_copy(slot, slot, send_sems.at[j], recv_sems.at[j],
                                                     device_id=(px, py, c), device_id_type=MESH))
    return sends, arrivals


def _scatter_steps(first, last, s_ref, r_ref, sems):
    sends, arrivals = _scatter_copies(s_ref, r_ref, *sems)

    @pl.when(first)
    def _():
        for cp in sends:
            cp.start()

    @pl.when(last)
    def _():
        for cp in arrivals:
            cp.wait_recv()
        for cp in sends:
            cp.wait_send()


def _scatter_pieces(s):
    def body(s_ref, r_ref, send_sems, recv_sems):
        sends, arrivals = _scatter_copies(s_ref, r_ref, send_sems, recv_sems)
        for cp in sends:
            cp.start()
        for cp in arrivals:
            cp.wait_recv()
        for cp in sends:
            cp.wait_send()

    return pl.pallas_call(
        body, name="scatter_pieces", out_shape=jax.ShapeDtypeStruct(s.shape, s.dtype), in_specs=[_HBM], out_specs=_HBM,
        scratch_shapes=[pltpu.SemaphoreType.DMA((3,)), pltpu.SemaphoreType.DMA((3,))],
    )(s)


def _sum_pieces(r):
    half = r.shape[1]
    br = REDUCE_ROWS

    def kern(r_ref, o_ref):
        o_ref[...] = ((r_ref[0].astype(F32) + r_ref[1].astype(F32)) + r_ref[2].astype(F32)) + r_ref[3].astype(F32)

    return pl.pallas_call(
        kern, name="sum_pieces", grid=(half // br,), in_specs=[pl.BlockSpec((N_CHIPS, br, PACK_COLS), lambda t: (0, t, 0))],
        out_specs=pl.BlockSpec((br, PACK_COLS), lambda t: (t, 0)), out_shape=_sds(half, PACK_COLS, F32),
        compiler_params=_params(("arbitrary",)))(r)


def _join_halves(f):
    def body(f_ref, o_ref, send_sem, recv_sem):
        x, y, c = _place()
        cp = pltpu.make_async_remote_copy(f_ref, o_ref, send_sem, recv_sem, device_id=(x, y, 1 - c), device_id_type=MESH)
        cp.start()
        cp.wait()

    return pl.pallas_call(
        body, name="join_halves", out_shape=jax.ShapeDtypeStruct(f.shape, f.dtype), in_specs=[_HBM], out_specs=_HBM,
        scratch_shapes=[pltpu.SemaphoreType.DMA, pltpu.SemaphoreType.DMA],
    )(f)


def _allreduce_small(g):
    n_dev = 8

    def body(g_ref, o_ref, buf, send_sems, recv_sems):
        x, y, c = _place()
        me = 4 * x + 2 * y + c
        buf[me] = g_ref[...]
        peers = []
        for f in range(1, n_dev):
            px, py = _flip(x, y, f & 4, f & 2)
            pc = 1 - c if f & 1 else c
            peers.append((px, py, pc))
        sent = []
        for f, peer in enumerate(peers):
            cp = pltpu.make_async_remote_copy(g_ref, buf.at[me], send_sems.at[f], recv_sems.at[f], device_id=peer,
                                              device_id_type=MESH)
            cp.start()
            sent.append(cp)
        for f, (px, py, pc) in enumerate(peers):
            slot = buf.at[4 * px + 2 * py + pc]
            pltpu.make_async_remote_copy(slot, slot, send_sems.at[f], recv_sems.at[f], device_id=(px, py, pc),
                                         device_id_type=MESH).wait_recv()
        for cp in sent:
            cp.wait_send()
        total = buf[0]
        for k in range(1, n_dev):
            total = total + buf[k]
        o_ref[...] = total

    vmem = pl.BlockSpec(memory_space=pltpu.VMEM)
    return pl.pallas_call(
        body, name="allreduce_small", out_shape=jax.ShapeDtypeStruct(g.shape, g.dtype), in_specs=[vmem], out_specs=vmem,
        scratch_shapes=[pltpu.VMEM((n_dev,) + g.shape, g.dtype), pltpu.SemaphoreType.DMA((n_dev - 1,)),
                        pltpu.SemaphoreType.DMA((n_dev - 1,))],
    )(g)


def _adamw_update(g_ref, w_ref, m_ref, v_ref, d_o, m_o, v_o):
    c1 = 1.0 - ADAM_B1 ** ADAM_STEP
    c2 = 1.0 - ADAM_B2 ** ADAM_STEP
    g_ = g_ref[...]
    m_new = ADAM_B1 * m_ref[...] + (1.0 - ADAM_B1) * g_
    v_new = ADAM_B2 * v_ref[...] + (1.0 - ADAM_B2) * (g_ * g_)
    m_o[...] = m_new
    v_o[...] = v_new
    d_o[...] = -ADAM_LR * ((m_new / c1) / (jnp.sqrt(v_new / c2) + ADAM_EPS) + ADAM_WD * w_ref[...])


ADAMW_ROWS = 256


def _adamw(name, g, w, m, v):
    _, rows, cols = w.shape
    br = int(np.gcd(ADAMW_ROWS, rows))

    def kern(*refs):
        _adamw_update(*refs)

    spec = pl.BlockSpec((1, br, cols), lambda t: (0, t, 0))
    out = jax.ShapeDtypeStruct(w.shape, F32)
    return pl.pallas_call(kern, name="adamw_" + name, grid=(rows // br,), in_specs=[spec] * 4, out_specs=[spec] * 3,
                          out_shape=[out, out, out], compiler_params=_params(("arbitrary",)))(g, w, m, v)


def _adamw_small(gs, ws, ms, vs):
    n = len(gs)

    def kern(*refs):
        ins, outs = refs[:4 * n], refs[4 * n:]
        for k in range(n):
            _adamw_update(ins[k], ins[n + k], ins[2 * n + k], ins[3 * n + k], outs[k], outs[n + k], outs[2 * n + k])

    vmem = pl.BlockSpec(memory_space=pltpu.VMEM)
    out = [jax.ShapeDtypeStruct(w.shape, F32) for w in ws]
    res = pl.pallas_call(kern, name="adamw_small", in_specs=[vmem] * (4 * n), out_specs=[vmem] * (3 * n),
                         out_shape=out * 3, compiler_params=pltpu.CompilerParams(vmem_limit_bytes=VMEM_LIMIT))(*gs, *ws, *ms, *vs)
    return [(res[k], res[n + k], res[2 * n + k]) for k in range(n)]


def _shard_rows(shape, axis):
    k, n = shape
    return (k * n // N_CHIPS) // PACK_COLS


def _group(names):
    entries = [e for e in SHARDED if e[0] in names]
    used = sum(_shard_rows(shape, axis) for _, shape, axis in entries)
    return entries, -(-used // (2 * REDUCE_ROWS)) * 2 * REDUCE_ROWS


def _pack_shards(shards, names, dtype):
    entries, rows = _group(names)
    parts = [shards[name].astype(dtype).reshape(-1, PACK_COLS) for name, _, _ in entries]
    used = sum(p.shape[0] for p in parts)
    if rows > used:
        parts.append(jnp.zeros((rows - used, PACK_COLS), dtype))
    return jnp.concatenate(parts, 0)


def _unpack_shards(packed, names):
    out, r0 = {}, 0
    for name, (k, n), axis in _group(names)[0]:
        nr = _shard_rows((k, n), axis)
        shape = (k // N_CHIPS, n) if axis == 0 else (k, n // N_CHIPS)
        out[name] = packed[r0:r0 + nr].reshape(shape)
        r0 += nr
    return out


def _unpack_full(gathered, names):
    out, r0 = {}, 0
    for name, (k, n), axis in _group(names)[0]:
        nr = _shard_rows((k, n), axis)
        part = gathered[:, r0:r0 + nr]
        if axis == 0:
            out[name] = part.reshape(k, n)
        else:
            out[name] = part.reshape(N_CHIPS, k, n // N_CHIPS).transpose(1, 0, 2).reshape(k, n)
        r0 += nr
    return out


def _pack_pieces(grads, names):
    entries, rows = _group(names)
    parts = []
    for name, (k, n), axis in entries:
        g = grads[name]
        if axis == 0:
            parts.append(g.reshape(N_CHIPS, -1, PACK_COLS))
        else:
            parts.append(g.reshape(k, N_CHIPS, n // N_CHIPS).transpose(1, 0, 2).reshape(N_CHIPS, -1, PACK_COLS))
    used = sum(p.shape[1] for p in parts)
    if rows > used:
        parts.append(jnp.zeros((N_CHIPS, rows - used, PACK_COLS), F32))
    return jnp.concatenate(parts, 1)


def _pack_small(vals):
    flat = jnp.concatenate([vals[name].astype(F32).reshape(-1) for name, _ in SMALL])
    flat = jnp.concatenate([flat, jnp.zeros((SMALL_ROWS * PACK_COLS - flat.shape[0],), F32)])
    return flat.reshape(SMALL_ROWS, PACK_COLS)


def _unpack_small(packed):
    flat, out, o = packed.reshape(-1), {}, 0
    for name, shape in SMALL:
        n = int(np.prod(shape))
        out[name] = flat[o:o + n].reshape(shape)
        o += n
    return out


def kernel(x, p, positions, g_pre_mix, w_in, b_gate, g_q, w_uq, g_kv, w_ukv, w_pool, pool_scale, w_branch_attn, w_branch_pool, w_out, g_post_mix, g_pre_mlp, w_ff1, w_ff2, g_post_mlp, w_ple_proj, w_ple_gate, g_ple, loss_target, m_g_pre_mix, m_w_in, m_b_gate, m_g_q, m_w_uq, m_g_kv, m_w_ukv, m_w_pool, m_pool_scale, m_w_branch_attn, m_w_branch_pool, m_w_out, m_g_post_mix, m_g_pre_mlp, m_w_ff1, m_w_ff2, m_g_post_mlp, m_w_ple_proj, m_w_ple_gate, m_g_ple, v_g_pre_mix, v_w_in, v_b_gate, v_g_q, v_w_uq, v_g_kv, v_w_ukv, v_w_pool, v_pool_scale, v_w_branch_attn, v_w_branch_pool, v_w_out, v_g_post_mix, v_g_pre_mlp, v_w_ff1, v_w_ff2, v_g_post_mlp, v_w_ple_proj, v_w_ple_gate, v_g_ple):
    given = dict(locals())
    weights = {n: given[n] for n in WEIGHT_ORDER}
    moments_m = {n: given["m_" + n] for n in WEIGHT_ORDER}
    moments_v = {n: given["v_" + n] for n in WEIGHT_ORDER}
    c = lax.axis_index("c")

    big_w = {name: weights[name][0] for name, _, _ in SHARDED}
    my_chip = 2 * lax.axis_index("x") + lax.axis_index("y")
    packed_first = _pack_shards(big_w, FIRST, MX)
    full = _unpack_full(lax.dynamic_update_slice(_allgather_shards(packed_first), packed_first[None], (my_chip, 0, 0)), FIRST)
    for name, _ in SMALL:
        full[name] = weights[name][0] if name == "w_pool" else weights[name]

    loss_cols, dx, grads, (sent_rest, received_rest) = _local_step(
        x[0], p[0, 0], positions[0], loss_target[0], full, (my_chip, c), _pack_shards(big_w, REST, MX))
    loss = lax.psum(0.5 * jnp.sum(loss_cols) / D_MODEL, ("x", "y", "c"))

    def finish(sent, received):
        mine = lax.dynamic_slice(sent, (my_chip, 0, 0), (1,) + sent.shape[1:])
        reduced = _sum_pieces(lax.dynamic_update_slice(received, mine, (my_chip, 0, 0)))
        theirs = _join_halves(reduced)
        return jnp.where(c == 0, jnp.concatenate([reduced, theirs]), jnp.concatenate([theirs, reduced]))

    pieces = _pack_pieces(grads, FIRST)
    sent_first = _add_halves(pieces, _exchange_halves(pieces), c)
    shards = _unpack_shards(finish(sent_first, _scatter_pieces(sent_first)), FIRST)
    shards.update(_unpack_shards(finish(sent_rest, received_rest), REST))
    g_small = _allreduce_small(_pack_small(grads))

    out = {}
    for name, g in shards.items():
        out[name] = (g[None], *_adamw(name, g[None], weights[name], moments_m[name], moments_v[name]))
    small_g = _unpack_small(g_small)
    names = [n for n, _ in SMALL]
    updates = _adamw_small([small_g[n] for n in names], [weights[n] for n in names], [moments_m[n] for n in names],
                           [moments_v[n] for n in names])
    for n, upd in zip(names, updates):
        out[n] = (small_g[n], *upd)
    return (loss, dx[None], *[out[n][k] for k in range(4) for n in WEIGHT_ORDER])
```

```python
import functools

import numpy as np
import jax
import jax.numpy as jnp
from jax import lax
from jax.experimental import pallas as pl
from jax.experimental.pallas import tpu as pltpu

F32 = jnp.float32
MX = jnp.bfloat16
WIRE = jnp.bfloat16

D_MODEL = 1024
N_HEADS = 8
QK_NOPE = 64
QK_ROPE = 32
V_HEAD = 64
Q_LORA = 384
KV_LORA = 256
POOL_WINDOWS = (2, 4, 8, 16)
POOL_GROUP = 128
POOL_WIDTH = 512
D_FF = 4096
PLE_DIM = 256
ROPE_THETA = 10000.0
EPS = 1e-6
HEAD_SLOT = 128
QK_WIDTH = N_HEADS * HEAD_SLOT
ROPE_LANE = 64
SMALL_COLS = Q_LORA + KV_LORA + HEAD_SLOT
IN_PAD = SMALL_COLS + POOL_WIDTH + 2 * D_MODEL
SCALE = (QK_NOPE + QK_ROPE) ** -0.5
LOG2E = 1.4426950408889634
NEG = -1e30
HALO = 16

ADAM_LR = 0.001
ADAM_B1 = 0.9
ADAM_B2 = 0.999
ADAM_EPS = 1e-08
ADAM_WD = 0.01
ADAM_STEP = 10

VMEM_LIMIT = 56 * 2**20
MESH = pl.DeviceIdType.MESH

SHARDED = (
    ("w_in", (1024, 3232), 1),
    ("w_uq", (384, 768), 1),
    ("w_ukv", (256, 1024), 1),
    ("w_branch_attn", (512, 1024), 1),
    ("w_branch_pool", (512, 1024), 1),
    ("w_out", (1024, 1024), 0),
    ("w_ff1", (1024, 4096), 1),
    ("w_ff2", (4096, 1024), 0),
    ("w_ple_proj", (256, 1024), 1),
    ("w_ple_gate", (1024, 1024), 0),
)
SMALL = (
    ("g_pre_mix", (1, 1024)),
    ("b_gate", (1, 2048)),
    ("g_q", (1, 384)),
    ("g_kv", (1, 256)),
    ("w_pool", (1, 4, 128, 128)),
    ("pool_scale", (1, 512)),
    ("g_post_mix", (1, 1024)),
    ("g_pre_mlp", (1, 1024)),
    ("g_post_mlp", (1, 1024)),
    ("g_ple", (1, 1024)),
)
WEIGHT_ORDER = ("g_pre_mix", "w_in", "b_gate", "g_q", "w_uq", "g_kv", "w_ukv", "w_pool", "pool_scale", "w_branch_attn",
                "w_branch_pool", "w_out", "g_post_mix", "g_pre_mlp", "w_ff1", "w_ff2", "g_post_mlp", "w_ple_proj",
                "w_ple_gate", "g_ple")
N_CHIPS = 4
PACK_COLS = 1024
REDUCE_ROWS = 160
SMALL_ROWS = 80
FIRST = ("w_in", "w_uq", "w_ukv")
REST = tuple(name for name, _, _ in SHARDED if name not in FIRST)


def _dot(a, b):
    return jnp.dot(a.astype(MX), b.astype(MX), preferred_element_type=F32)


def _dot_nt(a, b):
    return lax.dot_general(a.astype(MX), b.astype(MX), (((1,), (1,)), ((), ())), preferred_element_type=F32)


def _dot_tn(a, b):
    return lax.dot_general(a.astype(MX), b.astype(MX), (((0,), (0,)), ((), ())), preferred_element_type=F32)


def _sig(x):
    return 1.0 / (1.0 + jnp.exp(-x))


def _rms(x, g):
    r = lax.rsqrt(jnp.mean(x * x, axis=1, keepdims=True) + EPS)
    xh = x * r
    return xh * g, xh, r


def _rms_bwd(xh, r, g, dy):
    dxn = dy * g
    dx = r * (dxn - xh * jnp.mean(dxn * xh, axis=1, keepdims=True))
    return dx, jnp.sum(dy * xh, axis=0, keepdims=True)


def _rot_half(v):
    lane = lax.broadcasted_iota(jnp.int32, v.shape, 1)
    return jnp.where(lane < ROPE_LANE + QK_ROPE // 2, pltpu.roll(v, HEAD_SLOT - QK_ROPE // 2, 1), pltpu.roll(v, QK_ROPE // 2, 1))


def _rope(v, cc, sa):
    return v * cc + _rot_half(v) * sa


def _unrope(v, cc, sa):
    return v * cc - _rot_half(v) * sa


def _params(sem):
    return pltpu.CompilerParams(dimension_semantics=sem, vmem_limit_bytes=VMEM_LIMIT)


def _tok_call(name, body, n_tok, tm, tiled, resident, outs, accs=(), scratch=(), exchange=None):
    def as_pair(t):
        if isinstance(t, tuple):
            return t
        return t, pl.BlockSpec((tm, t.shape[1]), lambda i: (i, 0))
    tiled = [as_pair(t) for t in tiled]
    outs = [as_pair(o) for o in outs]
    res_specs = [pl.BlockSpec(r.shape, lambda i, nd=r.ndim: (0,) * nd, pipeline_mode=pl.Buffered(1)) for r in resident]
    out_specs = [s for _, s in outs] + [pl.BlockSpec(a.shape, lambda i: (0, 0)) for a in accs]
    n_t, n_r, n_o, n_a, n_s = len(tiled), len(resident), len(outs), len(accs), len(scratch)
    n_steps = n_tok // tm
    operands = [a for a, _ in tiled] + list(resident)
    in_specs = [s for _, s in tiled] + res_specs
    out_shape = [o for o, _ in outs] + list(accs)
    scratch = list(scratch)
    if exchange is not None:
        ex_in, ex_out, ex_sems, ex_steps = exchange
        operands.append(ex_in)
        in_specs.append(_HBM)
        out_shape.append(ex_out)
        out_specs.append(_HBM)
        scratch += list(ex_sems)

    def kern(*refs):
        refs = list(refs)
        n_in = n_t + n_r + (exchange is not None)
        n_out = n_o + n_a + (exchange is not None)
        tin, res = refs[:n_t], refs[n_t:n_t + n_r]
        tout = refs[n_in:n_in + n_o]
        acc = refs[n_in + n_o:n_in + n_o + n_a]
        scr = refs[n_in + n_out:n_in + n_out + n_s]
        i = pl.program_id(0)
        if exchange is not None:
            ex_steps(i, n_steps, refs[n_in - 1], refs[n_in + n_out - 1], refs[n_in + n_out + n_s:])

        @pl.when(i == 0)
        def _():
            for a in acc:
                a[...] = jnp.zeros(a.shape, a.dtype)
        body(i, tin, res, tout, acc, scr)

    return pl.pallas_call(
        kern, name=name, grid=(n_steps,), in_specs=in_specs, out_specs=out_specs,
        out_shape=out_shape, scratch_shapes=scratch, compiler_params=_params(("arbitrary",)),
    )(*operands)


def _sds(rows, cols, dtype):
    return jax.ShapeDtypeStruct((rows, cols), dtype)


def _fwd_inproj(x, cc, sa, w, tm, gather=None):
    n_tok = x.shape[0]

    def body(i, tin, res, tout, acc, scr):
        x_ref, c_ref, s_ref = tin
        g_pre, w_in, g_q, w_uq, g_kv, w_k, w_v, e_mat, w_kt, e_t, w_vt, v_ones = res
        a_o, ps_o, u_o, gl_o, qn_o, kvn_o, q_o, k_o, v_o, kt_o, vt_o = tout
        a = _rms(x_ref[...], g_pre[...])[0].astype(MX)
        a_o[...] = a
        ps = _dot(a, w_in[:, :SMALL_COLS])
        ps_o[...] = ps.astype(ps_o.dtype)
        u_o[...] = _dot(a, w_in[:, SMALL_COLS:SMALL_COLS + POOL_WIDTH]).astype(u_o.dtype)
        gl_o[...] = _dot(a, w_in[:, SMALL_COLS + POOL_WIDTH:]).astype(gl_o.dtype)
        cc_, sa_ = c_ref[...], s_ref[...]
        qn = _rms(ps[:, :Q_LORA], g_q[...])[0].astype(MX)
        qn_o[...] = qn
        q = _dot(qn, w_uq[...])
        for h in range(N_HEADS):
            hs = slice(h * HEAD_SLOT, (h + 1) * HEAD_SLOT)
            q_o[:, hs] = (_rope(q[:, hs], cc_, sa_) * (SCALE * LOG2E)).astype(q_o.dtype)
        kvn = _rms(ps[:, Q_LORA:Q_LORA + KV_LORA], g_kv[...])[0].astype(MX)
        kvn_o[...] = kvn
        kr = _rope(ps[:, Q_LORA + KV_LORA:], cc_, sa_)
        k_o[...] = (_dot(kvn, w_k[...]) + _dot(kr, e_mat[...])).astype(k_o.dtype)
        v_o[...] = _dot(kvn, w_v[...]).astype(v_o.dtype)
        kt_o[...] = (_dot_nt(w_kt[...], kvn) + _dot_nt(e_t[...], kr)).astype(kt_o.dtype)
        vt_o[...] = (_dot_nt(w_vt[...], kvn) + v_ones[...]).astype(vt_o.dtype)

    outs = [_sds(n_tok, D_MODEL, MX), _sds(n_tok, SMALL_COLS, MX), _sds(n_tok, POOL_WIDTH, MX), _sds(n_tok, 2 * D_MODEL, MX),
            _sds(n_tok, Q_LORA, MX), _sds(n_tok, KV_LORA, MX), _sds(n_tok, QK_WIDTH, MX), _sds(n_tok, QK_WIDTH, MX),
            _sds(n_tok, N_HEADS * V_HEAD, MX),
            (_sds(QK_WIDTH, n_tok, MX), pl.BlockSpec((QK_WIDTH, tm), lambda i: (0, i))),
            (_sds(N_HEADS * V_ROWS, n_tok, MX), pl.BlockSpec((N_HEADS * V_ROWS, tm), lambda i: (0, i)))]
    res = [w["g_pre_mix"], w["w_in"], w["g_q"], w["w_uq"], w["g_kv"], w["w_k"], w["w_v"], w["e_mat"], w["w_kt"], w["e_t"],
           w["w_vt"], w["v_ones"]]
    exchange = None
    if gather is not None:
        gathered = jax.ShapeDtypeStruct((N_CHIPS,) + gather.shape, gather.dtype)
        exchange = (gather, gathered, [pltpu.SemaphoreType.DMA((6,)), pltpu.SemaphoreType.DMA((6,))], _gather_steps)
    return _tok_call("fwd_inproj", body, n_tok, tm, [x, cc, sa], res, outs, exchange=exchange)


def _causal_pairs(nq, ratio, by_kv):
    if by_kv:
        pairs = [(i, j) for j in range(nq * ratio) for i in range(j // ratio, nq)]
    else:
        pairs = [(i, j) for i in range(nq) for j in range((i + 1) * ratio)]
    return (jnp.asarray(np.array([p[0] for p in pairs], np.int32)), jnp.asarray(np.array([p[1] for p in pairs], np.int32)))


def _keep_t(tk, tq, off):
    return lax.broadcasted_iota(jnp.int32, (tk, tq), 0) + off <= lax.broadcasted_iota(jnp.int32, (tk, tq), 1)


ATTN_FWD_TILE = (1024, 1024)
ATTN_BWD_TILE = (1024, 512)
V_ROWS = 80


def _attn_fwd(q, k, vt, tq, tk):
    n_tok = q.shape[0]
    nq, ratio = n_tok // tq, tq // tk
    qi, kj = _causal_pairs(nq, ratio, by_kv=False)

    def kern(qi_ref, kj_ref, q_ref, k_ref, vt_ref, o_ref, lse_ref, m_s, acc_s, st_s):
        s_id = pl.program_id(0)
        i, j = qi_ref[s_id], kj_ref[s_id]

        @pl.when(j == 0)
        def _():
            m_s[...] = jnp.full(m_s.shape, NEG, F32)
            acc_s[...] = jnp.zeros(acc_s.shape, F32)

        def scores(h):
            hs = slice(h * HEAD_SLOT, (h + 1) * HEAD_SLOT)
            return _dot_nt(k_ref[:, hs], q_ref[:, hs])

        def heads(masked):
            keep = _keep_t(tk, tq, j * tk - i * tq) if masked else None
            st_s[0] = scores(0)
            for h in range(N_HEADS):
                if h + 1 < N_HEADS:
                    st_s[(h + 1) % 2] = scores(h + 1)
                st = st_s[h % 2]
                if masked:
                    st = jnp.where(keep, st, NEG)
                m_old = m_s[h]
                m_new = jnp.maximum(m_old, jnp.max(st, axis=0, keepdims=True))
                pt = jnp.exp2(st - m_new)
                acc_s[h] = jnp.exp2(m_old - m_new) * acc_s[h] + _dot(vt_ref[h * V_ROWS:(h + 1) * V_ROWS, :], pt)
                m_s[h] = m_new

        @pl.when(j < i * ratio)
        def _():
            heads(False)

        @pl.when(j >= i * ratio)
        def _():
            heads(True)

        @pl.when(j == (i + 1) * ratio - 1)
        def _():
            heads_out = []
            for h in range(N_HEADS):
                total = acc_s[h, V_HEAD:V_HEAD + 1, :]
                heads_out.append(acc_s[h, :V_HEAD, :] / total)
                lse_ref[h:h + 1, :] = m_s[h] + jnp.log2(total)
            o_ref[...] = jnp.concatenate(heads_out, 0).T.astype(o_ref.dtype)

    gs = pltpu.PrefetchScalarGridSpec(
        num_scalar_prefetch=2, grid=(qi.shape[0],),
        in_specs=[pl.BlockSpec((tq, QK_WIDTH), lambda s, qi, kj: (qi[s], 0)),
                  pl.BlockSpec((tk, QK_WIDTH), lambda s, qi, kj: (kj[s], 0)),
                  pl.BlockSpec((N_HEADS * V_ROWS, tk), lambda s, qi, kj: (0, kj[s]))],
        out_specs=[pl.BlockSpec((tq, N_HEADS * V_HEAD), lambda s, qi, kj: (qi[s], 0)),
                   pl.BlockSpec((N_HEADS, tq), lambda s, qi, kj: (0, qi[s]))],
        scratch_shapes=[pltpu.VMEM((N_HEADS, 1, tq), F32), pltpu.VMEM((N_HEADS, V_ROWS, tq), F32),
                        pltpu.VMEM((2, tk, tq), F32)])
    return pl.pallas_call(kern, name="attn_fwd", grid_spec=gs,
                          out_shape=[_sds(n_tok, N_HEADS * V_HEAD, MX), _sds(N_HEADS, n_tok, F32)],
                          compiler_params=_params(("arbitrary",)))(qi, kj, q, k, vt)


def _pool_windows(ext, i, tm, first_row):
    row = i * tm + lax.broadcasted_iota(jnp.int32, (tm, 1), 0)
    out = []
    for g, w in enumerate(POOL_WINDOWS):
        cs = slice(g * POOL_GROUP, (g + 1) * POOL_GROUP)
        s = ext[pl.ds(first_row, tm), cs]
        for k in range(1, w):
            s = s + ext[pl.ds(first_row - k, tm), cs]
        cnt = jnp.minimum(row + 1, w).astype(F32)
        out.append(s / cnt)
    return out


def _fwd_mix(x, u, gl, attn, w, tm):
    n_tok = x.shape[0]
    halo_spec = pl.BlockSpec((HALO, POOL_WIDTH), lambda i: (jnp.maximum(i * (tm // HALO) - 1, 0), 0))

    def body(i, tin, res, tout, acc, scr):
        x_ref, u_ref, uh_ref, gl_ref, at_ref = tin
        w_pool, pool_scale, w_ba, w_bp, b_gate, w_out, g_post = res
        d_o, pooled_o, a_o, pp_o, merged_o, y_o, h1_o = tout
        ext, = scr
        ext[pl.ds(0, HALO), :] = jnp.where(i > 0, uh_ref[...].astype(F32), 0.0)
        ext[pl.ds(HALO, tm), :] = u_ref[...].astype(F32)
        means = _pool_windows(ext, i, tm, HALO)
        for g in range(len(POOL_WINDOWS)):
            cs = slice(g * POOL_GROUP, (g + 1) * POOL_GROUP)
            d = (means[g] - ext[pl.ds(HALO, tm), cs]).astype(MX)
            d_o[:, cs] = d
            pooled_o[:, cs] = (_dot(d, w_pool[g]) * pool_scale[:, cs]).astype(pooled_o.dtype)
        a_br = _dot(at_ref[...], w_ba[...])
        p_br = _dot(pooled_o[...], w_bp[...])
        a_o[...] = a_br.astype(a_o.dtype)
        pp_o[...] = p_br.astype(pp_o.dtype)
        gates = _sig(gl_ref[...].astype(F32) + b_gate[...])
        merged = (gates[:, :D_MODEL] * a_br + gates[:, D_MODEL:] * p_br).astype(MX)
        merged_o[...] = merged
        y = _dot(merged, w_out[...])
        y_o[...] = y.astype(y_o.dtype)
        h1_o[...] = x_ref[...] + _rms(y, g_post[...])[0]

    outs = [_sds(n_tok, POOL_WIDTH, MX), _sds(n_tok, POOL_WIDTH, MX), _sds(n_tok, D_MODEL, MX), _sds(n_tok, D_MODEL, MX),
            _sds(n_tok, D_MODEL, MX), _sds(n_tok, D_MODEL, MX), _sds(n_tok, D_MODEL, F32)]
    res = [w["w_pool"], w["pool_scale"], w["w_branch_attn"], w["w_branch_pool"], w["b_gate"], w["w_out"], w["g_post_mix"]]
    return _tok_call("fwd_mix", body, n_tok, tm, [x, u, (u, halo_spec), gl, attn], res, outs,
                     scratch=[pltpu.VMEM((tm + HALO, POOL_WIDTH), F32)])


def _fwd_mlp(h1, w, tm):
    n_tok = h1.shape[0]

    def body(i, tin, res, tout, acc, scr):
        h1_ref, = tin
        g_pre, w1, w2, g_post = res
        m_o, zr_o, a2_o, f_o, h2_o = tout
        h1_ = h1_ref[...]
        m = _rms(h1_, g_pre[...])[0].astype(MX)
        m_o[...] = m
        zr = jnp.maximum(_dot(m, w1[...]), 0.0)
        zr_o[...] = zr.astype(zr_o.dtype)
        a2 = (zr * zr).astype(MX)
        a2_o[...] = a2
        f = _dot(a2, w2[...])
        f_o[...] = f.astype(f_o.dtype)
        h2_o[...] = h1_ + _rms(f, g_post[...])[0]

    outs = [_sds(n_tok, D_MODEL, MX), _sds(n_tok, D_FF, MX), _sds(n_tok, D_FF, MX), _sds(n_tok, D_MODEL, MX),
            _sds(n_tok, D_MODEL, F32)]
    res = [w["g_pre_mlp"], w["w_ff1"], w["w_ff2"], w["g_post_mlp"]]
    return _tok_call("fwd_mlp", body, n_tok, tm, [h1], res, outs)


def _ple_fwd_bwd(h2, p, target, w, tm):
    n_tok = h2.shape[0]

    def body(i, tin, res, tout, acc, scr):
        h2_ref, p_ref, t_ref = tin
        w_pe, w_pg, g_ple = res
        dh2_o, de_o, dzg_o = tout
        loss_a, dg_a = acc
        h2_ = h2_ref[...]
        e = _dot(p_ref[...], w_pe[...])
        pg = _sig(_dot(h2_, w_pg[...]))
        t = pg * e
        g = g_ple[...]
        tn, th, r = _rms(t, g)
        diff = h2_ + tn - t_ref[...]
        loss_a[...] += jnp.sum(diff * diff, axis=0, keepdims=True)
        dh3 = diff * (1.0 / D_MODEL)
        dt, dg = _rms_bwd(th, r, g, dh3)
        dg_a[...] += dg
        de_o[...] = (dt * pg).astype(de_o.dtype)
        dzg = (dt * e * pg * (1.0 - pg)).astype(MX)
        dzg_o[...] = dzg
        dh2_o[...] = dh3 + _dot_nt(dzg, w_pg[...])

    outs = [_sds(n_tok, D_MODEL, F32), _sds(n_tok, D_MODEL, MX), _sds(n_tok, D_MODEL, MX)]
    accs = [_sds(1, D_MODEL, F32), _sds(1, D_MODEL, F32)]
    return _tok_call("ple_fwd_bwd", body, n_tok, tm, [h2, p, target], [w["w_ple_proj"], w["w_ple_gate"], w["g_ple"]], outs, accs)


def _bwd_mlp(dh2, f, h1, zr, w, tm):
    n_tok = dh2.shape[0]

    def body(i, tin, res, tout, acc, scr):
        dh2_ref, f_ref, h1_ref, zr_ref = tin
        g_pre, w1, w2, g_post = res
        df_o, dz_o, dh1_o = tout
        dg_post_a, dg_pre_a = acc
        dh2_ = dh2_ref[...]
        gp = g_post[...]
        _, fh, rf = _rms(f_ref[...].astype(F32), gp)
        df, dg = _rms_bwd(fh, rf, gp, dh2_)
        dg_post_a[...] += dg
        df = df.astype(MX)
        df_o[...] = df
        dz = (_dot_nt(df, w2[...]) * (2.0 * zr_ref[...].astype(F32))).astype(MX)
        dz_o[...] = dz
        dm = _dot_nt(dz, w1[...])
        gq = g_pre[...]
        _, hh, rh = _rms(h1_ref[...], gq)
        dh1, dg = _rms_bwd(hh, rh, gq, dm)
        dg_pre_a[...] += dg
        dh1_o[...] = dh2_ + dh1

    outs = [_sds(n_tok, D_MODEL, MX), _sds(n_tok, D_FF, MX), _sds(n_tok, D_MODEL, F32)]
    accs = [_sds(1, D_MODEL, F32), _sds(1, D_MODEL, F32)]
    res = [w["g_pre_mlp"], w["w_ff1"], w["w_ff2"], w["g_post_mlp"]]
    return _tok_call("bwd_mlp", body, n_tok, tm, [dh2, f, h1, zr], res, outs, accs)


def _bwd_mix(dh1, y, a_br, p_br, gl, attn, d, w, tm):
    n_tok = dh1.shape[0]

    def body(i, tin, res, tout, acc, scr):
        dh1_ref, y_ref, a_ref, pp_ref, gl_ref, at_ref, d_ref = tin
        g_post, w_out, b_gate, w_ba, w_bp, w_pool, pool_scale, sel = res
        dy_o, da_o, dpp_o, dgl_o, do_o, delta_o, dyp_o, dd_o = tout
        dg_post_a, db_a, dps_a = acc
        g = g_post[...]
        _, yh, r = _rms(y_ref[...].astype(F32), g)
        dy, dg = _rms_bwd(yh, r, g, dh1_ref[...])
        dg_post_a[...] += dg
        dy = dy.astype(MX)
        dy_o[...] = dy
        dmerged = _dot_nt(dy, w_out[...])
        gates = _sig(gl_ref[...].astype(F32) + b_gate[...])
        ga, gp = gates[:, :D_MODEL], gates[:, D_MODEL:]
        da = (dmerged * ga).astype(MX)
        dpp = (dmerged * gp).astype(MX)
        da_o[...] = da
        dpp_o[...] = dpp
        dgl_a = dmerged * a_ref[...].astype(F32) * ga * (1.0 - ga)
        dgl_p = dmerged * pp_ref[...].astype(F32) * gp * (1.0 - gp)
        dgl_o[:, :D_MODEL] = dgl_a.astype(dgl_o.dtype)
        dgl_o[:, D_MODEL:] = dgl_p.astype(dgl_o.dtype)
        db_a[:, :D_MODEL] += jnp.sum(dgl_a, axis=0, keepdims=True)
        db_a[:, D_MODEL:] += jnp.sum(dgl_p, axis=0, keepdims=True)
        do = _dot_nt(da, w_ba[...]).astype(MX)
        do_o[...] = do
        prod = do.astype(F32) * at_ref[...].astype(F32)
        hi = prod.astype(MX)
        lo = (prod - hi.astype(F32)).astype(MX)
        delta_o[...] = _dot(hi, sel[...]) + _dot(lo, sel[...])
        dpooled = _dot_nt(dpp, w_bp[...])
        for gi in range(len(POOL_WINDOWS)):
            cs = slice(gi * POOL_GROUP, (gi + 1) * POOL_GROUP)
            ypre = _dot(d_ref[:, cs], w_pool[gi])
            dps_a[:, cs] += jnp.sum(dpooled[:, cs] * ypre, axis=0, keepdims=True)
            dyp = (dpooled[:, cs] * pool_scale[:, cs]).astype(MX)
            dyp_o[:, cs] = dyp
            dd_o[:, cs] = _dot_nt(dyp, w_pool[gi])

    outs = [_sds(n_tok, D_MODEL, MX), _sds(n_tok, D_MODEL, MX), _sds(n_tok, D_MODEL, MX), _sds(n_tok, 2 * D_MODEL, MX),
            _sds(n_tok, N_HEADS * V_HEAD, MX), _sds(n_tok, HEAD_SLOT, F32), _sds(n_tok, POOL_WIDTH, MX),
            _sds(n_tok, POOL_WIDTH, F32)]
    accs = [_sds(1, D_MODEL, F32), _sds(1, 2 * D_MODEL, F32), _sds(1, POOL_WIDTH, F32)]
    res = [w["g_post_mix"], w["w_out"], w["b_gate"], w["w_branch_attn"], w["w_branch_pool"], w["w_pool"], w["pool_scale"],
           w["head_sel"]]
    return _tok_call("bwd_mix", body, n_tok, tm, [dh1, y, a_br, p_br, gl, attn, d], res, outs, accs)


def _bwd_heads(q_ref, k_ref, v_ref, do_ref, lse_ref, dl_ref, st_s, dpt_s, keep, use, n_heads):
    def products(h):
        hs = slice(h * HEAD_SLOT, (h + 1) * HEAD_SLOT)
        vs = slice(h * V_HEAD, (h + 1) * V_HEAD)
        st_s[h % 2] = _dot_nt(k_ref[:, hs], q_ref[:, hs])
        dpt_s[h % 2] = _dot_nt(v_ref[:, vs], do_ref[:, vs])

    products(0)
    for h in range(n_heads):
        if h + 1 < n_heads:
            products(h + 1)
        st = st_s[h % 2]
        if keep is not None:
            st = jnp.where(keep, st, NEG)
        pt = jnp.exp2(st - lse_ref[h:h + 1, :])
        use(h, pt, pt * (dpt_s[h % 2] - dl_ref[h:h + 1, :]))


HEAD_GROUP = 4


def _attn_bwd(q, k, kt, v, do, lse, delta, tq, tk, scatter=None):
    n_tok = q.shape[0]
    nq, ratio = n_tok // tq, tq // tk
    n_groups = N_HEADS // HEAD_GROUP
    gq, gv = HEAD_GROUP * HEAD_SLOT, HEAD_GROUP * V_HEAD
    qi, kj = _causal_pairs(nq, ratio, by_kv=True)

    n_pairs = qi.shape[0]

    def kern(qi_ref, kj_ref, q_ref, k_ref, kt_ref, v_ref, do_ref, lse_ref, dl_ref, *rest):
        if scatter is not None:
            s_hbm, dq_ref, dk_ref, dv_ref, r_hbm, dk_s, dv_s, st_s, dpt_s, send_sems, recv_sems = rest
        else:
            dq_ref, dk_ref, dv_ref, dk_s, dv_s, st_s, dpt_s = rest
        s_id = pl.program_id(1)
        i, j = qi_ref[s_id], kj_ref[s_id]
        cols = pl.ds(pl.multiple_of(i * tq, tq), tq)
        if scatter is not None:
            group = pl.program_id(0)
            _scatter_steps(jnp.logical_and(group == 0, s_id == 0),
                           jnp.logical_and(group == n_groups - 1, s_id == n_pairs - 1), s_hbm, r_hbm, (send_sems, recv_sems))

        @pl.when(s_id == 0)
        def _():
            dq_ref[...] = jnp.zeros(dq_ref.shape, F32)

        def use(h, pt, dst):
            hs = slice(h * HEAD_SLOT, (h + 1) * HEAD_SLOT)
            dv_s[h] += _dot(pt, do_ref[:, h * V_HEAD:(h + 1) * V_HEAD])
            dk_s[:, hs] += _dot(dst, q_ref[:, hs])
            dq_ref[hs, cols] += _dot(kt_ref[hs, :], dst)

        def heads(masked):
            keep = _keep_t(tk, tq, j * tk - i * tq) if masked else None
            _bwd_heads(q_ref, k_ref, v_ref, do_ref, lse_ref.at[0], dl_ref.at[0], st_s, dpt_s, keep, use, HEAD_GROUP)

        @pl.when(j >= i * ratio)
        def _():
            dk_s[...] = jnp.zeros(dk_s.shape, F32)
            dv_s[...] = jnp.zeros(dv_s.shape, F32)
            heads(True)

        @pl.when(j < i * ratio)
        def _():
            heads(False)

        @pl.when(i == nq - 1)
        def _():
            dk_ref[...] = (dk_s[...] * (1.0 / LOG2E)).astype(dk_ref.dtype)
            for h in range(HEAD_GROUP):
                dv_ref[:, h * V_HEAD:(h + 1) * V_HEAD] = dv_s[h].astype(dv_ref.dtype)

    at_q = lambda g, s, qi, kj: (qi[s], g)
    at_k = lambda g, s, qi, kj: (kj[s], g)
    at_kt = lambda g, s, qi, kj: (g, kj[s])
    at_stat = lambda g, s, qi, kj: (g, 0, qi[s])
    in_specs = [pl.BlockSpec((tq, gq), at_q), pl.BlockSpec((tk, gq), at_k), pl.BlockSpec((gq, tk), at_kt),
                pl.BlockSpec((tk, gv), at_k), pl.BlockSpec((tq, gv), at_q),
                pl.BlockSpec((1, HEAD_GROUP, tq), at_stat), pl.BlockSpec((1, HEAD_GROUP, tq), at_stat)]
    out_specs = [pl.BlockSpec((gq, n_tok), lambda g, s, qi, kj: (g, 0), pipeline_mode=pl.Buffered(1)),
                 pl.BlockSpec((tk, gq), at_k), pl.BlockSpec((tk, gv), at_k)]
    out_shape = [_sds(QK_WIDTH, n_tok, F32), _sds(n_tok, QK_WIDTH, MX), _sds(n_tok, N_HEADS * V_HEAD, MX)]
    scratch = [pltpu.VMEM((tk, gq), F32), pltpu.VMEM((HEAD_GROUP, tk, V_HEAD), F32),
               pltpu.VMEM((2, tk, tq), F32), pltpu.VMEM((2, tk, tq), F32)]
    stat3 = lambda a: a.reshape(n_groups, HEAD_GROUP, n_tok)
    operands = [qi, kj, q, k, kt, v, do, stat3(lse), stat3(delta)]
    if scatter is not None:
        operands.append(scatter)
        in_specs.append(_HBM)
        out_specs.append(_HBM)
        out_shape.append(jax.ShapeDtypeStruct((N_DEVICES, scatter.shape[1] // 2, PACK_COLS), scatter.dtype))
        scratch += [pltpu.SemaphoreType.DMA((N_DEVICES - 1,)), pltpu.SemaphoreType.DMA((N_DEVICES - 1,))]
    gs = pltpu.PrefetchScalarGridSpec(num_scalar_prefetch=2, grid=(n_groups, n_pairs), in_specs=in_specs,
                                      out_specs=out_specs, scratch_shapes=scratch)
    return pl.pallas_call(kern, name="attn_bwd", grid_spec=gs, out_shape=out_shape,
                          compiler_params=_params(("arbitrary", "arbitrary")))(*operands)


def _bwd_inproj(dq_t, dk, dv, dd, dgl, ps, x, dh1, cc, sa, w, tm):
    n_tok = x.shape[0]
    n_tiles = n_tok // tm
    last_halo = n_tok // HALO - 1
    halo_spec = pl.BlockSpec((HALO, POOL_WIDTH), lambda i: (jnp.minimum((i + 1) * (tm // HALO), last_halo), 0))

    def body(i, tin, res, tout, acc, scr):
        dq_ref, dk_ref, dv_ref, dd_ref, ddh_ref, dgl_ref, ps_ref, x_ref, dh1_ref, c_ref, s_ref = tin
        w_uq, g_q, w_k, w_v, e_mat, g_kv, w_in, g_pre = res
        dqu_o, dproj_o, dx_o = tout
        dgq_a, dgkv_a, dgpre_a = acc
        ext, = scr
        cc_, sa_ = c_ref[...], s_ref[...]
        for h in range(N_HEADS):
            hs = slice(h * HEAD_SLOT, (h + 1) * HEAD_SLOT)
            dqu_o[:, hs] = (_unrope(dq_ref[hs, :].T, cc_, sa_) * SCALE).astype(dqu_o.dtype)
        gq = g_q[...]
        _, qh, rq = _rms(ps_ref[:, :Q_LORA].astype(F32), gq)
        dqd, dg = _rms_bwd(qh, rq, gq, _dot_nt(dqu_o[...], w_uq[...]))
        dgq_a[...] += dg
        dproj_o[:, :Q_LORA] = dqd.astype(dproj_o.dtype)
        gkv = g_kv[...]
        _, kh, rk = _rms(ps_ref[:, Q_LORA:Q_LORA + KV_LORA].astype(F32), gkv)
        dkvd, dg = _rms_bwd(kh, rk, gkv, _dot_nt(dk_ref[...], w_k[...]) + _dot_nt(dv_ref[...], w_v[...]))
        dgkv_a[...] += dg
        dproj_o[:, Q_LORA:Q_LORA + KV_LORA] = dkvd.astype(dproj_o.dtype)
        dproj_o[:, Q_LORA + KV_LORA:SMALL_COLS] = _unrope(_dot_nt(dk_ref[...], e_mat[...]), cc_, sa_).astype(dproj_o.dtype)
        row = i * tm + lax.broadcasted_iota(jnp.int32, (tm + HALO, 1), 0)
        for gi, wdw in enumerate(POOL_WINDOWS):
            cs = slice(gi * POOL_GROUP, (gi + 1) * POOL_GROUP)
            inv = 1.0 / jnp.minimum(row + 1, wdw).astype(F32)
            ext[pl.ds(0, tm), cs] = dd_ref[:, cs] * inv[:tm]
            ext[pl.ds(tm, HALO), cs] = jnp.where(i < n_tiles - 1, ddh_ref[:, cs] * inv[tm:], 0.0)
            s = ext[pl.ds(0, tm), cs]
            for k_ in range(1, wdw):
                s = s + ext[pl.ds(k_, tm), cs]
            dproj_o[:, SMALL_COLS + gi * POOL_GROUP:SMALL_COLS + (gi + 1) * POOL_GROUP] = (s - dd_ref[:, cs]).astype(dproj_o.dtype)
        dproj_o[:, SMALL_COLS + POOL_WIDTH:] = dgl_ref[...]
        da = _dot_nt(dproj_o[...], w_in[...])
        gp = g_pre[...]
        _, xh, rx = _rms(x_ref[...], gp)
        dx, dg = _rms_bwd(xh, rx, gp, da)
        dgpre_a[...] += dg
        dx_o[...] = dh1_ref[...] + dx

    outs = [_sds(n_tok, QK_WIDTH, MX), _sds(n_tok, IN_PAD, MX), _sds(n_tok, D_MODEL, F32)]
    accs = [_sds(1, Q_LORA, F32), _sds(1, KV_LORA, F32), _sds(1, D_MODEL, F32)]
    res = [w["w_uq"], w["g_q"], w["w_k"], w["w_v"], w["e_mat"], w["g_kv"], w["w_in"], w["g_pre_mix"]]
    dq_spec = pl.BlockSpec((QK_WIDTH, tm), lambda i: (0, i))
    return _tok_call("bwd_inproj", body, n_tok, tm, [(dq_t, dq_spec), dk, dv, dd, (dd, halo_spec), dgl, ps, x, dh1, cc, sa], res, outs, accs,
                     scratch=[pltpu.VMEM((tm + HALO, POOL_WIDTH), F32)])


XTDY_TOKENS = 1024
XTDY_OUT_BYTES = 8 * 2**20


def _xtdy(name, x, dy):
    n_tok, kk = x.shape
    nn = dy.shape[1]
    bt = min(XTDY_TOKENS, n_tok)
    bk = kk
    while bk * nn * 4 > XTDY_OUT_BYTES and bk % 256 == 0:
        bk //= 2

    def kern(x_ref, dy_ref, o_ref):
        @pl.when(pl.program_id(1) == 0)
        def _():
            o_ref[...] = jnp.zeros(o_ref.shape, F32)
        o_ref[...] += _dot_tn(x_ref[...], dy_ref[...])

    return pl.pallas_call(
        kern, name=name, grid=(kk // bk, n_tok // bt),
        in_specs=[pl.BlockSpec((bt, bk), lambda a, t: (t, a)), pl.BlockSpec((bt, nn), lambda a, t: (t, 0))],
        out_specs=pl.BlockSpec((bk, nn), lambda a, t: (a, 0)), out_shape=_sds(kk, nn, F32),
        compiler_params=_params(("arbitrary", "arbitrary")))(x, dy)


def _rope_tables(positions):
    inv_freq = ROPE_THETA ** (-jnp.arange(0, QK_ROPE, 2, dtype=F32) / QK_ROPE)
    ang_t = inv_freq[:, None] * positions.astype(F32)[None, :]
    cos_t, sin_t = lax.optimization_barrier((jnp.cos(ang_t), jnp.sin(ang_t)))
    cos, sin = cos_t.T, sin_t.T
    n_tok = positions.shape[0]
    ones, z64 = jnp.ones((n_tok, ROPE_LANE), F32), jnp.zeros((n_tok, ROPE_LANE), F32)
    z32 = jnp.zeros((n_tok, HEAD_SLOT - ROPE_LANE - QK_ROPE), F32)
    return jnp.concatenate([ones, cos, cos, z32], 1), jnp.concatenate([z64, -sin, sin, z32], 1)


def _kernel_weights(full):
    w_in, w_uq, w_ukv = full["w_in"], full["w_uq"], full["w_ukv"]
    c0 = Q_LORA + KV_LORA
    z = lambda n: jnp.zeros((D_MODEL, n), w_in.dtype)
    w = dict(full)
    w["w_in"] = jnp.concatenate([w_in[:, :c0], z(ROPE_LANE), w_in[:, c0:c0 + QK_ROPE], z(HEAD_SLOT - ROPE_LANE - QK_ROPE),
                                 w_in[:, c0 + QK_ROPE:]], 1)
    w["w_uq"] = jnp.pad(w_uq.reshape(Q_LORA, N_HEADS, QK_NOPE + QK_ROPE),
                        ((0, 0), (0, 0), (0, HEAD_SLOT - QK_NOPE - QK_ROPE))).reshape(Q_LORA, QK_WIDTH)
    kv = w_ukv.reshape(KV_LORA, N_HEADS, QK_NOPE + V_HEAD)
    w["w_k"] = jnp.pad(kv[:, :, :QK_NOPE], ((0, 0), (0, 0), (0, HEAD_SLOT - QK_NOPE))).reshape(KV_LORA, QK_WIDTH)
    w["w_v"] = kv[:, :, QK_NOPE:].reshape(KV_LORA, N_HEADS * V_HEAD)
    e = np.zeros((HEAD_SLOT, QK_WIDTH), np.float32)
    sel = np.zeros((N_HEADS * V_HEAD, HEAD_SLOT), np.float32)
    for h in range(N_HEADS):
        for r in range(QK_ROPE):
            e[ROPE_LANE + r, h * HEAD_SLOT + ROPE_LANE + r] = 1.0
        sel[h * V_HEAD:(h + 1) * V_HEAD, h] = 1.0
    w["e_mat"] = jnp.asarray(e, MX)
    w["w_kt"], w["e_t"] = w["w_k"].T, jnp.asarray(e.T, MX)
    pad = ((0, 0), (0, V_ROWS - V_HEAD), (0, 0))
    w["w_vt"] = jnp.pad(w["w_v"].T.reshape(N_HEADS, V_HEAD, KV_LORA), pad).reshape(N_HEADS * V_ROWS, KV_LORA)
    ones = np.zeros((N_HEADS, V_ROWS, 1), np.float32)
    ones[:, V_HEAD] = 1.0
    w["v_ones"] = jnp.asarray(ones.reshape(N_HEADS * V_ROWS, 1))
    w["head_sel"] = jnp.asarray(sel, MX)
    w["w_pool"] = full["w_pool"].astype(MX)
    return w


def _local_step(x, p, positions, target, full, mesh_place=None, packed_rest=None):
    n_tok = x.shape[0]
    tm = min(512, n_tok)
    tm_mlp = min(256, n_tok)
    fwd_tile = [min(t, n_tok) for t in ATTN_FWD_TILE]
    bwd_tile = [min(t, n_tok) for t in ATTN_BWD_TILE]
    w = _kernel_weights(full)
    cc, sa = _rope_tables(positions)

    if mesh_place is None:
        a, ps, u, gl, qn, kvn, q, k, v, kt, vt = _fwd_inproj(x, cc, sa, w, tm)
    else:
        my_chip, core = mesh_place
        a, ps, u, gl, qn, kvn, q, k, v, kt, vt, gathered = _fwd_inproj(x, cc, sa, w, tm, gather=packed_rest)
        w.update(_unpack_full(lax.dynamic_update_slice(gathered, packed_rest[None], (my_chip, 0, 0)), REST))
    attn, lse = _attn_fwd(q, k, vt, *fwd_tile)
    d, pooled, a_br, p_br, merged, y, h1 = _fwd_mix(x, u, gl, attn, w, tm)
    m, zr, a2, f, h2 = _fwd_mlp(h1, w, tm_mlp)
    dh2, de, dzg, loss_cols, dg_ple = _ple_fwd_bwd(h2, p, target, w, tm)
    df, dz, dh1, dg_post_mlp, dg_pre_mlp = _bwd_mlp(dh2, f, h1, zr, w, tm_mlp)
    dy, da_br, dp_br, dgl, do, delta, dyp, dd, dg_post_mix, db_gate, dpool_scale = _bwd_mix(dh1, y, a_br, p_br, gl, attn, d, w, tm)
    grads = {"w_branch_attn": _xtdy("dw_ba", attn, da_br), "w_branch_pool": _xtdy("dw_bp", pooled, dp_br),
             "w_out": _xtdy("dw_out", merged, dy), "w_ff1": _xtdy("dw_ff1", m, dz), "w_ff2": _xtdy("dw_ff2", a2, df),
             "w_ple_proj": _xtdy("dw_pe", p, de), "w_ple_gate": _xtdy("dw_pg", h2, dzg)}
    delta_t = delta[:, :N_HEADS].T
    if mesh_place is None:
        travelling = None
        dq_t, dk, dv = _attn_bwd(q, k, kt, v, do, lse, delta_t, *bwd_tile)
    else:
        pieces = _pack_pieces(grads, REST, WIRE)
        dq_t, dk, dv, received = _attn_bwd(q, k, kt, v, do, lse, delta_t, *bwd_tile, scatter=pieces)
        travelling = (pieces, received)
        grads = {}
    dqu, dproj, dx, dg_q, dg_kv, dg_pre_mix = _bwd_inproj(dq_t, dk, dv, dd, dgl, ps, x, dh1, cc, sa, w, tm)

    g_in = _xtdy("dw_in", a, dproj)
    g_uq = _xtdy("dw_uq", qn, dqu)
    g_k = _xtdy("dw_k", kvn, dk)
    g_v = _xtdy("dw_v", kvn, dv)
    g_pool = _xtdy("dw_pool", d, dyp)

    c0 = Q_LORA + KV_LORA
    grads.update({
        "g_pre_mix": dg_pre_mix,
        "w_in": jnp.concatenate([g_in[:, :c0], g_in[:, c0 + ROPE_LANE:c0 + ROPE_LANE + QK_ROPE], g_in[:, SMALL_COLS:]], 1),
        "b_gate": db_gate,
        "g_q": dg_q,
        "w_uq": g_uq.reshape(Q_LORA, N_HEADS, HEAD_SLOT)[:, :, :QK_NOPE + QK_ROPE].reshape(Q_LORA, N_HEADS * (QK_NOPE + QK_ROPE)),
        "g_kv": dg_kv,
        "w_ukv": jnp.concatenate([g_k.reshape(KV_LORA, N_HEADS, HEAD_SLOT)[:, :, :QK_NOPE],
                                  g_v.reshape(KV_LORA, N_HEADS, V_HEAD)], 2).reshape(KV_LORA, N_HEADS * (QK_NOPE + V_HEAD)),
        "w_pool": jnp.stack([g_pool[g * POOL_GROUP:(g + 1) * POOL_GROUP, g * POOL_GROUP:(g + 1) * POOL_GROUP]
                             for g in range(len(POOL_WINDOWS))]),
        "pool_scale": dpool_scale,
        "g_post_mix": dg_post_mix,
        "g_pre_mlp": dg_pre_mlp,
        "g_post_mlp": dg_post_mlp,
        "g_ple": dg_ple,
    })
    return loss_cols, dx, grads, travelling


def _place():
    return lax.axis_index("x"), lax.axis_index("y"), lax.axis_index("c")


CHIP_FLIPS = ((1, 0), (0, 1), (1, 1))


def _flip(x, y, fx, fy):
    return (1 - x if fx else x), (1 - y if fy else y)


_HBM = pl.BlockSpec(memory_space=pl.ANY)


def _gather_copies(w_ref, out_ref, send_sems, recv_sems):
    half = w_ref.shape[0] // 2
    x, y, c = _place()
    my_chip = 2 * x + y
    sibling = (x, y, 1 - c)

    def half_of(chip, hc):
        return out_ref.at[chip, pl.ds(pl.multiple_of(hc * half, 16), half), :]

    src = w_ref.at[pl.ds(pl.multiple_of(c * half, 16), half), :]
    sends, landed, forwards, from_sibling = [], [], [], []
    for j, (fx, fy) in enumerate(CHIP_FLIPS):
        px, py = _flip(x, y, fx, fy)
        mine_there, theirs_here, theirs_other = half_of(my_chip, c), half_of(2 * px + py, c), half_of(2 * px + py, 1 - c)
        sends.append(pltpu.make_async_remote_copy(src, mine_there, send_sems.at[j], recv_sems.at[j],
                                                  device_id=(px, py, c), device_id_type=MESH))
        landed.append(pltpu.make_async_remote_copy(src, theirs_here, send_sems.at[j], recv_sems.at[j],
                                                   device_id=(px, py, c), device_id_type=MESH))
        forwards.append(pltpu.make_async_remote_copy(theirs_here, theirs_here, send_sems.at[3 + j], recv_sems.at[3 + j],
                                                     device_id=sibling, device_id_type=MESH))
        from_sibling.append(pltpu.make_async_remote_copy(theirs_other, theirs_other, send_sems.at[3 + j],
                                                         recv_sems.at[3 + j], device_id=sibling, device_id_type=MESH))
    return sends, landed, forwards, from_sibling


def _gather_steps(i, n_steps, w_ref, out_ref, sems):
    sends, landed, forwards, from_sibling = _gather_copies(w_ref, out_ref, *sems)

    @pl.when(i == 0)
    def _():
        for cp in sends:
            cp.start()

    @pl.when(i == (3 * n_steps) // 4)
    def _():
        for arrived, fwd in zip(landed, forwards):
            arrived.wait_recv()
            fwd.start()

    @pl.when(i == n_steps - 1)
    def _():
        for cp in from_sibling:
            cp.wait_recv()
        for cp in sends + forwards:
            cp.wait_send()


def _allgather_shards(wp):
    def body(w_ref, out_ref, send_sems, recv_sems):
        sends, landed, forwards, from_sibling = _gather_copies(w_ref, out_ref, send_sems, recv_sems)
        for cp in sends:
            cp.start()
        for arrived, fwd in zip(landed, forwards):
            arrived.wait_recv()
            fwd.start()
        for cp in from_sibling:
            cp.wait_recv()
        for cp in sends + forwards:
            cp.wait_send()

    return pl.pallas_call(
        body, name="allgather_shards", out_shape=jax.ShapeDtypeStruct((N_CHIPS,) + wp.shape, wp.dtype),
        in_specs=[_HBM], out_specs=_HBM,
        scratch_shapes=[pltpu.SemaphoreType.DMA((6,)), pltpu.SemaphoreType.DMA((6,))],
    )(wp)


def _exchange_halves(g):
    rows = g.shape[1]
    half = rows // 2

    def body(g_ref, r_ref, send_sem, recv_sem):
        x, y, c = _place()
        src = g_ref.at[:, pl.ds(pl.multiple_of((1 - c) * half, 8), half), :]
        cp = pltpu.make_async_remote_copy(src, r_ref, send_sem, recv_sem, device_id=(x, y, 1 - c), device_id_type=MESH)
        cp.start()
        cp.wait()

    return pl.pallas_call(
        body, name="exchange_halves", out_shape=jax.ShapeDtypeStruct((N_CHIPS, half, PACK_COLS), g.dtype),
        in_specs=[_HBM], out_specs=_HBM, scratch_shapes=[pltpu.SemaphoreType.DMA, pltpu.SemaphoreType.DMA],
    )(g)


def _add_halves(g, r, c):
    rows = g.shape[1]
    half = rows // 2
    br = REDUCE_ROWS
    nb = half // br

    def kern(c_ref, g_ref, r_ref, o_ref):
        o_ref[...] = (g_ref[...] + r_ref[...]).astype(o_ref.dtype)

    gs = pltpu.PrefetchScalarGridSpec(
        num_scalar_prefetch=1, grid=(N_CHIPS, nb),
        in_specs=[pl.BlockSpec((1, br, PACK_COLS), lambda k, t, c: (k, c[0] * nb + t, 0)),
                  pl.BlockSpec((1, br, PACK_COLS), lambda k, t, c: (k, t, 0))],
        out_specs=pl.BlockSpec((1, br, PACK_COLS), lambda k, t, c: (k, t, 0)))
    return pl.pallas_call(kern, name="add_halves", grid_spec=gs,
                          out_shape=jax.ShapeDtypeStruct((N_CHIPS, half, PACK_COLS), WIRE),
                          compiler_params=_params(("arbitrary", "arbitrary")))(c.reshape(1), g, r)


def _scatter_copies(s_ref, r_ref, send_sems, recv_sems):
    x, y, c = _place()
    my_chip = 2 * x + y
    sends, arrivals = [], []
    for j, (fx, fy) in enumerate(CHIP_FLIPS):
        px, py = _flip(x, y, fx, fy)
        slot = r_ref.at[2 * px + py]
        sends.append(pltpu.make_async_remote_copy(s_ref.at[2 * px + py], r_ref.at[my_chip], send_sems.at[j], recv_sems.at[j],
                                                  device_id=(px, py, c), device_id_type=MESH))
        arrivals.append(pltpu.make_async_remote_copy(slot, slot, send_sems.at[j], recv_sems.at[j],
                                                     device_id=(px, py, c), device_id_type=MESH))
    return sends, arrivals


N_DEVICES = 8


def _peer(x, y, c, f):
    px, py = _flip(x, y, f & 4, f & 2)
    return px, py, (1 - c if f & 1 else c)


def _scatter_all_copies(p_ref, r_ref, send_sems, recv_sems):
    half = p_ref.shape[1] // 2
    x, y, c = _place()
    me = 4 * x + 2 * y + c
    sends, arrivals = [], []
    for f in range(1, N_DEVICES):
        px, py, pc = _peer(x, y, c, f)
        theirs = p_ref.at[2 * px + py, pl.ds(pl.multiple_of(pc * half, 16), half), :]
        slot = r_ref.at[4 * px + 2 * py + pc]
        sends.append(pltpu.make_async_remote_copy(theirs, r_ref.at[me], send_sems.at[f - 1], recv_sems.at[f - 1],
                                                  device_id=(px, py, pc), device_id_type=MESH))
        arrivals.append(pltpu.make_async_remote_copy(slot, slot, send_sems.at[f - 1], recv_sems.at[f - 1],
                                                     device_id=(px, py, pc), device_id_type=MESH))
    return sends, arrivals


def _scatter_steps(first, last, p_ref, r_ref, sems):
    sends, arrivals = _scatter_all_copies(p_ref, r_ref, *sems)

    @pl.when(first)
    def _():
        for cp in sends:
            cp.start()

    @pl.when(last)
    def _():
        for cp in arrivals:
            cp.wait_recv()
        for cp in sends:
            cp.wait_send()


def _scatter_pieces(s):
    def body(s_ref, r_ref, send_sems, recv_sems):
        sends, arrivals = _scatter_copies(s_ref, r_ref, send_sems, recv_sems)
        for cp in sends:
            cp.start()
        for cp in arrivals:
            cp.wait_recv()
        for cp in sends:
            cp.wait_send()

    return pl.pallas_call(
        body, name="scatter_pieces", out_shape=jax.ShapeDtypeStruct(s.shape, s.dtype), in_specs=[_HBM], out_specs=_HBM,
        scratch_shapes=[pltpu.SemaphoreType.DMA((3,)), pltpu.SemaphoreType.DMA((3,))],
    )(s)


def _sum_pieces(r):
    slots, half = r.shape[:2]
    br = REDUCE_ROWS

    def kern(r_ref, o_ref):
        total = r_ref[0].astype(F32)
        for k in range(1, slots):
            total = total + r_ref[k].astype(F32)
        o_ref[...] = total

    return pl.pallas_call(
        kern, name="sum_pieces", grid=(half // br,), in_specs=[pl.BlockSpec((slots, br, PACK_COLS), lambda t: (0, t, 0))],
        out_specs=pl.BlockSpec((br, PACK_COLS), lambda t: (t, 0)), out_shape=_sds(half, PACK_COLS, F32),
        compiler_params=_params(("arbitrary",)))(r)


def _join_halves(f):
    def body(f_ref, o_ref, send_sem, recv_sem):
        x, y, c = _place()
        cp = pltpu.make_async_remote_copy(f_ref, o_ref, send_sem, recv_sem, device_id=(x, y, 1 - c), device_id_type=MESH)
        cp.start()
        cp.wait()

    return pl.pallas_call(
        body, name="join_halves", out_shape=jax.ShapeDtypeStruct(f.shape, f.dtype), in_specs=[_HBM], out_specs=_HBM,
        scratch_shapes=[pltpu.SemaphoreType.DMA, pltpu.SemaphoreType.DMA],
    )(f)


def _allreduce_small(g):
    n_dev = N_DEVICES

    def body(g_ref, o_ref, buf, send_sems, recv_sems):
        x, y, c = _place()
        me = 4 * x + 2 * y + c
        buf[me] = g_ref[...]
        peers = [_peer(x, y, c, f) for f in range(1, n_dev)]
        sent = []
        for f, peer in enumerate(peers):
            cp = pltpu.make_async_remote_copy(g_ref, buf.at[me], send_sems.at[f], recv_sems.at[f], device_id=peer,
                                              device_id_type=MESH)
            cp.start()
            sent.append(cp)
        for f, (px, py, pc) in enumerate(peers):
            slot = buf.at[4 * px + 2 * py + pc]
            pltpu.make_async_remote_copy(slot, slot, send_sems.at[f], recv_sems.at[f], device_id=(px, py, pc),
                                         device_id_type=MESH).wait_recv()
        for cp in sent:
            cp.wait_send()
        total = buf[0]
        for k in range(1, n_dev):
            total = total + buf[k]
        o_ref[...] = total

    vmem = pl.BlockSpec(memory_space=pltpu.VMEM)
    return pl.pallas_call(
        body, name="allreduce_small", out_shape=jax.ShapeDtypeStruct(g.shape, g.dtype), in_specs=[vmem], out_specs=vmem,
        scratch_shapes=[pltpu.VMEM((n_dev,) + g.shape, g.dtype), pltpu.SemaphoreType.DMA((n_dev - 1,)),
                        pltpu.SemaphoreType.DMA((n_dev - 1,))],
    )(g)


def _adamw_update(g_ref, w_ref, m_ref, v_ref, d_o, m_o, v_o):
    c1 = 1.0 - ADAM_B1 ** ADAM_STEP
    c2 = 1.0 - ADAM_B2 ** ADAM_STEP
    g_ = g_ref[...]
    m_new = ADAM_B1 * m_ref[...] + (1.0 - ADAM_B1) * g_
    v_new = ADAM_B2 * v_ref[...] + (1.0 - ADAM_B2) * (g_ * g_)
    m_o[...] = m_new
    v_o[...] = v_new
    d_o[...] = -ADAM_LR * ((m_new / c1) / (jnp.sqrt(v_new / c2) + ADAM_EPS) + ADAM_WD * w_ref[...])


ADAMW_ROWS = 256


def _adamw(name, g, w, m, v):
    _, rows, cols = w.shape
    br = int(np.gcd(ADAMW_ROWS, rows))

    def kern(*refs):
        _adamw_update(*refs)

    spec = pl.BlockSpec((1, br, cols), lambda t: (0, t, 0))
    out = jax.ShapeDtypeStruct(w.shape, F32)
    return pl.pallas_call(kern, name="adamw_" + name, grid=(rows // br,), in_specs=[spec] * 4, out_specs=[spec] * 3,
                          out_shape=[out, out, out], compiler_params=_params(("arbitrary",)))(g, w, m, v)


def _adamw_small(gs, ws, ms, vs):
    n = len(gs)

    def kern(*refs):
        ins, outs = refs[:4 * n], refs[4 * n:]
        for k in range(n):
            _adamw_update(ins[k], ins[n + k], ins[2 * n + k], ins[3 * n + k], outs[k], outs[n + k], outs[2 * n + k])

    vmem = pl.BlockSpec(memory_space=pltpu.VMEM)
    out = [jax.ShapeDtypeStruct(w.shape, F32) for w in ws]
    res = pl.pallas_call(kern, name="adamw_small", in_specs=[vmem] * (4 * n), out_specs=[vmem] * (3 * n),
                         out_shape=out * 3, compiler_params=pltpu.CompilerParams(vmem_limit_bytes=VMEM_LIMIT))(*gs, *ws, *ms, *vs)
    return [(res[k], res[n + k], res[2 * n + k]) for k in range(n)]


def _shard_rows(shape, axis):
    k, n = shape
    return (k * n // N_CHIPS) // PACK_COLS


def _group(names):
    entries = [e for e in SHARDED if e[0] in names]
    used = sum(_shard_rows(shape, axis) for _, shape, axis in entries)
    return entries, -(-used // (2 * REDUCE_ROWS)) * 2 * REDUCE_ROWS


def _pack_shards(shards, names, dtype):
    entries, rows = _group(names)
    parts = [shards[name].astype(dtype).reshape(-1, PACK_COLS) for name, _, _ in entries]
    used = sum(p.shape[0] for p in parts)
    if rows > used:
        parts.append(jnp.zeros((rows - used, PACK_COLS), dtype))
    return jnp.concatenate(parts, 0)


def _unpack_shards(packed, names):
    out, r0 = {}, 0
    for name, (k, n), axis in _group(names)[0]:
        nr = _shard_rows((k, n), axis)
        shape = (k // N_CHIPS, n) if axis == 0 else (k, n // N_CHIPS)
        out[name] = packed[r0:r0 + nr].reshape(shape)
        r0 += nr
    return out


def _unpack_full(gathered, names):
    out, r0 = {}, 0
    for name, (k, n), axis in _group(names)[0]:
        nr = _shard_rows((k, n), axis)
        part = gathered[:, r0:r0 + nr]
        if axis == 0:
            out[name] = part.reshape(k, n)
        else:
            out[name] = part.reshape(N_CHIPS, k, n // N_CHIPS).transpose(1, 0, 2).reshape(k, n)
        r0 += nr
    return out


def _pack_pieces(grads, names, dtype=F32):
    entries, rows = _group(names)
    parts = []
    for name, (k, n), axis in entries:
        g = grads[name].astype(dtype)
        if axis == 0:
            parts.append(g.reshape(N_CHIPS, -1, PACK_COLS))
        else:
            parts.append(g.reshape(k, N_CHIPS, n // N_CHIPS).transpose(1, 0, 2).reshape(N_CHIPS, -1, PACK_COLS))
    used = sum(p.shape[1] for p in parts)
    if rows > used:
        parts.append(jnp.zeros((N_CHIPS, rows - used, PACK_COLS), dtype))
    return jnp.concatenate(parts, 1)


def _pack_small(vals):
    flat = jnp.concatenate([vals[name].astype(F32).reshape(-1) for name, _ in SMALL])
    flat = jnp.concatenate([flat, jnp.zeros((SMALL_ROWS * PACK_COLS - flat.shape[0],), F32)])
    return flat.reshape(SMALL_ROWS, PACK_COLS)


def _unpack_small(packed):
    flat, out, o = packed.reshape(-1), {}, 0
    for name, shape in SMALL:
        n = int(np.prod(shape))
        out[name] = flat[o:o + n].reshape(shape)
        o += n
    return out


def kernel(x, p, positions, g_pre_mix, w_in, b_gate, g_q, w_uq, g_kv, w_ukv, w_pool, pool_scale, w_branch_attn, w_branch_pool, w_out, g_post_mix, g_pre_mlp, w_ff1, w_ff2, g_post_mlp, w_ple_proj, w_ple_gate, g_ple, loss_target, m_g_pre_mix, m_w_in, m_b_gate, m_g_q, m_w_uq, m_g_kv, m_w_ukv, m_w_pool, m_pool_scale, m_w_branch_attn, m_w_branch_pool, m_w_out, m_g_post_mix, m_g_pre_mlp, m_w_ff1, m_w_ff2, m_g_post_mlp, m_w_ple_proj, m_w_ple_gate, m_g_ple, v_g_pre_mix, v_w_in, v_b_gate, v_g_q, v_w_uq, v_g_kv, v_w_ukv, v_w_pool, v_pool_scale, v_w_branch_attn, v_w_branch_pool, v_w_out, v_g_post_mix, v_g_pre_mlp, v_w_ff1, v_w_ff2, v_g_post_mlp, v_w_ple_proj, v_w_ple_gate, v_g_ple):
    given = dict(locals())
    weights = {n: given[n] for n in WEIGHT_ORDER}
    moments_m = {n: given["m_" + n] for n in WEIGHT_ORDER}
    moments_v = {n: given["v_" + n] for n in WEIGHT_ORDER}
    c = lax.axis_index("c")

    big_w = {name: weights[name][0] for name, _, _ in SHARDED}
    my_chip = 2 * lax.axis_index("x") + lax.axis_index("y")
    packed_first = _pack_shards(big_w, FIRST, MX)
    full = _unpack_full(lax.dynamic_update_slice(_allgather_shards(packed_first), packed_first[None], (my_chip, 0, 0)), FIRST)
    for name, _ in SMALL:
        full[name] = weights[name][0] if name == "w_pool" else weights[name]

    loss_cols, dx, grads, (pieces_rest, received_rest) = _local_step(
        x[0], p[0, 0], positions[0], loss_target[0], full, (my_chip, c), _pack_shards(big_w, REST, MX))
    loss = lax.psum(0.5 * jnp.sum(loss_cols) / D_MODEL, ("x", "y", "c"))

    def finish(received, mine, slot):
        reduced = _sum_pieces(lax.dynamic_update_slice(received, mine, (slot, 0, 0)))
        theirs = _join_halves(reduced)
        return jnp.where(c == 0, jnp.concatenate([reduced, theirs]), jnp.concatenate([theirs, reduced]))

    pieces = _pack_pieces(grads, FIRST)
    sent = _add_halves(pieces, _exchange_halves(pieces), c)
    mine = lax.dynamic_slice(sent, (my_chip, 0, 0), (1,) + sent.shape[1:])
    shards = _unpack_shards(finish(_scatter_pieces(sent), mine, my_chip), FIRST)
    half = received_rest.shape[1]
    mine = lax.dynamic_slice(pieces_rest, (my_chip, c * half, 0), (1, half, PACK_COLS))
    shards.update(_unpack_shards(finish(received_rest, mine, 2 * my_chip + c), REST))
    g_small = _allreduce_small(_pack_small(grads))

    out = {}
    for name, g in shards.items():
        out[name] = (g[None], *_adamw(name, g[None], weights[name], moments_m[name], moments_v[name]))
    small_g = _unpack_small(g_small)
    names = [n for n, _ in SMALL]
    updates = _adamw_small([small_g[n] for n in names], [weights[n] for n in names], [moments_m[n] for n in names],
                           [moments_v[n] for n in names])
    for n, upd in zip(names, updates):
        out[n] = (small_g[n], *upd)
    return (loss, dx[None], *[out[n][k] for k in range(4) for n in WEIGHT_ORDER])
```

```python
import functools

import numpy as np
import jax
import jax.numpy as jnp
from jax import lax
from jax.experimental import pallas as pl
from jax.experimental.pallas import tpu as pltpu

F32 = jnp.float32
MX = jnp.bfloat16
WIRE = jnp.bfloat16

D_MODEL = 1024
N_HEADS = 8
QK_NOPE = 64
QK_ROPE = 32
V_HEAD = 64
Q_LORA = 384
KV_LORA = 256
POOL_WINDOWS = (2, 4, 8, 16)
POOL_GROUP = 128
POOL_WIDTH = 512
D_FF = 4096
PLE_DIM = 256
ROPE_THETA = 10000.0
EPS = 1e-6
HEAD_SLOT = 128
QK_WIDTH = N_HEADS * HEAD_SLOT
ROPE_LANE = 64
SMALL_COLS = Q_LORA + KV_LORA + HEAD_SLOT
IN_PAD = SMALL_COLS + POOL_WIDTH + 2 * D_MODEL
SCALE = (QK_NOPE + QK_ROPE) ** -0.5
LOG2E = 1.4426950408889634
NEG = -1e30
HALO = 16

ADAM_LR = 0.001
ADAM_B1 = 0.9
ADAM_B2 = 0.999
ADAM_EPS = 1e-08
ADAM_WD = 0.01
ADAM_STEP = 10

VMEM_LIMIT = 56 * 2**20
MESH = pl.DeviceIdType.MESH

SHARDED = (
    ("w_in", (1024, 3232), 1),
    ("w_uq", (384, 768), 1),
    ("w_ukv", (256, 1024), 1),
    ("w_branch_attn", (512, 1024), 1),
    ("w_branch_pool", (512, 1024), 1),
    ("w_out", (1024, 1024), 0),
    ("w_ff1", (1024, 4096), 1),
    ("w_ff2", (4096, 1024), 0),
    ("w_ple_proj", (256, 1024), 1),
    ("w_ple_gate", (1024, 1024), 0),
)
SMALL = (
    ("g_pre_mix", (1, 1024)),
    ("b_gate", (1, 2048)),
    ("g_q", (1, 384)),
    ("g_kv", (1, 256)),
    ("w_pool", (1, 4, 128, 128)),
    ("pool_scale", (1, 512)),
    ("g_post_mix", (1, 1024)),
    ("g_pre_mlp", (1, 1024)),
    ("g_post_mlp", (1, 1024)),
    ("g_ple", (1, 1024)),
)
WEIGHT_ORDER = ("g_pre_mix", "w_in", "b_gate", "g_q", "w_uq", "g_kv", "w_ukv", "w_pool", "pool_scale", "w_branch_attn",
                "w_branch_pool", "w_out", "g_post_mix", "g_pre_mlp", "w_ff1", "w_ff2", "g_post_mlp", "w_ple_proj",
                "w_ple_gate", "g_ple")
N_CHIPS = 4
PACK_COLS = 1024
REDUCE_ROWS = 160
SMALL_ROWS = 80
FIRST = ("w_in", "w_uq", "w_ukv")
REST = tuple(name for name, _, _ in SHARDED if name not in FIRST)


def _dot(a, b):
    return jnp.dot(a.astype(MX), b.astype(MX), preferred_element_type=F32)


def _dot_nt(a, b):
    return lax.dot_general(a.astype(MX), b.astype(MX), (((1,), (1,)), ((), ())), preferred_element_type=F32)


def _dot_tn(a, b):
    return lax.dot_general(a.astype(MX), b.astype(MX), (((0,), (0,)), ((), ())), preferred_element_type=F32)


def _sig(x):
    return 1.0 / (1.0 + jnp.exp(-x))


def _rms(x, g):
    r = lax.rsqrt(jnp.mean(x * x, axis=1, keepdims=True) + EPS)
    xh = x * r
    return xh * g, xh, r


def _rms_bwd(xh, r, g, dy):
    dxn = dy * g
    dx = r * (dxn - xh * jnp.mean(dxn * xh, axis=1, keepdims=True))
    return dx, jnp.sum(dy * xh, axis=0, keepdims=True)


def _rot_half(v):
    lane = lax.broadcasted_iota(jnp.int32, v.shape, 1)
    return jnp.where(lane < ROPE_LANE + QK_ROPE // 2, pltpu.roll(v, HEAD_SLOT - QK_ROPE // 2, 1), pltpu.roll(v, QK_ROPE // 2, 1))


def _rope(v, cc, sa):
    return v * cc + _rot_half(v) * sa


def _unrope(v, cc, sa):
    return v * cc - _rot_half(v) * sa


def _params(sem):
    return pltpu.CompilerParams(dimension_semantics=sem, vmem_limit_bytes=VMEM_LIMIT)


def _tok_call(name, body, n_tok, tm, tiled, resident, outs, accs=(), scratch=(), exchange=None):
    def as_pair(t):
        if isinstance(t, tuple):
            return t
        return t, pl.BlockSpec((tm, t.shape[1]), lambda i: (i, 0))
    tiled = [as_pair(t) for t in tiled]
    outs = [as_pair(o) for o in outs]
    res_specs = [pl.BlockSpec(r.shape, lambda i, nd=r.ndim: (0,) * nd, pipeline_mode=pl.Buffered(1)) for r in resident]
    out_specs = [s for _, s in outs] + [pl.BlockSpec(a.shape, lambda i: (0, 0)) for a in accs]
    n_t, n_r, n_o, n_a, n_s = len(tiled), len(resident), len(outs), len(accs), len(scratch)
    n_steps = n_tok // tm
    operands = [a for a, _ in tiled] + list(resident)
    in_specs = [s for _, s in tiled] + res_specs
    out_shape = [o for o, _ in outs] + list(accs)
    scratch = list(scratch)
    if exchange is not None:
        ex_in, ex_out, ex_sems, ex_steps = exchange
        operands.append(ex_in)
        in_specs.append(_HBM)
        out_shape.append(ex_out)
        out_specs.append(_HBM)
        scratch += list(ex_sems)

    def kern(*refs):
        refs = list(refs)
        n_in = n_t + n_r + (exchange is not None)
        n_out = n_o + n_a + (exchange is not None)
        tin, res = refs[:n_t], refs[n_t:n_t + n_r]
        tout = refs[n_in:n_in + n_o]
        acc = refs[n_in + n_o:n_in + n_o + n_a]
        scr = refs[n_in + n_out:n_in + n_out + n_s]
        i = pl.program_id(0)
        if exchange is not None:
            ex_steps(i, n_steps, refs[n_in - 1], refs[n_in + n_out - 1], refs[n_in + n_out + n_s:])

        @pl.when(i == 0)
        def _():
            for a in acc:
                a[...] = jnp.zeros(a.shape, a.dtype)
        body(i, tin, res, tout, acc, scr)

    return pl.pallas_call(
        kern, name=name, grid=(n_steps,), in_specs=in_specs, out_specs=out_specs,
        out_shape=out_shape, scratch_shapes=scratch, compiler_params=_params(("arbitrary",)),
    )(*operands)


def _sds(rows, cols, dtype):
    return jax.ShapeDtypeStruct((rows, cols), dtype)


def _fwd_inproj(x, cc, sa, w, tm, gather=None):
    n_tok = x.shape[0]

    def body(i, tin, res, tout, acc, scr):
        x_ref, c_ref, s_ref = tin
        g_pre, w_in, g_q, w_uq, g_kv, w_k, w_v, e_mat, w_kt, e_t, w_vt, v_ones = res
        a_o, ps_o, u_o, gl_o, qn_o, kvn_o, q_o, k_o, v_o, kt_o, vt_o = tout
        a = _rms(x_ref[...], g_pre[...])[0].astype(MX)
        a_o[...] = a
        ps = _dot(a, w_in[:, :SMALL_COLS])
        ps_o[...] = ps.astype(ps_o.dtype)
        u_o[...] = _dot(a, w_in[:, SMALL_COLS:SMALL_COLS + POOL_WIDTH]).astype(u_o.dtype)
        gl_o[...] = _dot(a, w_in[:, SMALL_COLS + POOL_WIDTH:]).astype(gl_o.dtype)
        cc_, sa_ = c_ref[...], s_ref[...]
        qn = _rms(ps[:, :Q_LORA], g_q[...])[0].astype(MX)
        qn_o[...] = qn
        q = _dot(qn, w_uq[...])
        for h in range(N_HEADS):
            hs = slice(h * HEAD_SLOT, (h + 1) * HEAD_SLOT)
            q_o[:, hs] = (_rope(q[:, hs], cc_, sa_) * (SCALE * LOG2E)).astype(q_o.dtype)
        kvn = _rms(ps[:, Q_LORA:Q_LORA + KV_LORA], g_kv[...])[0].astype(MX)
        kvn_o[...] = kvn
        kr = _rope(ps[:, Q_LORA + KV_LORA:], cc_, sa_)
        k_o[...] = (_dot(kvn, w_k[...]) + _dot(kr, e_mat[...])).astype(k_o.dtype)
        v_o[...] = _dot(kvn, w_v[...]).astype(v_o.dtype)
        kt_o[...] = (_dot_nt(w_kt[...], kvn) + _dot_nt(e_t[...], kr)).astype(kt_o.dtype)
        vt_o[...] = (_dot_nt(w_vt[...], kvn) + v_ones[...]).astype(vt_o.dtype)

    outs = [_sds(n_tok, D_MODEL, MX), _sds(n_tok, SMALL_COLS, MX), _sds(n_tok, POOL_WIDTH, MX), _sds(n_tok, 2 * D_MODEL, MX),
            _sds(n_tok, Q_LORA, MX), _sds(n_tok, KV_LORA, MX), _sds(n_tok, QK_WIDTH, MX), _sds(n_tok, QK_WIDTH, MX),
            _sds(n_tok, N_HEADS * V_HEAD, MX),
            (_sds(QK_WIDTH, n_tok, MX), pl.BlockSpec((QK_WIDTH, tm), lambda i: (0, i))),
            (_sds(N_HEADS * V_ROWS, n_tok, MX), pl.BlockSpec((N_HEADS * V_ROWS, tm), lambda i: (0, i)))]
    res = [w["g_pre_mix"], w["w_in"], w["g_q"], w["w_uq"], w["g_kv"], w["w_k"], w["w_v"], w["e_mat"], w["w_kt"], w["e_t"],
           w["w_vt"], w["v_ones"]]
    exchange = None
    if gather is not None:
        gathered = jax.ShapeDtypeStruct((N_CHIPS,) + gather.shape, gather.dtype)
        exchange = (gather, gathered, [pltpu.SemaphoreType.DMA((6,)), pltpu.SemaphoreType.DMA((6,))], _gather_steps)
    return _tok_call("fwd_inproj", body, n_tok, tm, [x, cc, sa], res, outs, exchange=exchange)


def _causal_pairs(nq, ratio, by_kv):
    if by_kv:
        pairs = [(i, j) for j in range(nq * ratio) for i in range(j // ratio, nq)]
    else:
        pairs = [(i, j) for i in range(nq) for j in range((i + 1) * ratio)]
    return (jnp.asarray(np.array([p[0] for p in pairs], np.int32)), jnp.asarray(np.array([p[1] for p in pairs], np.int32)))


def _keep_t(tk, tq, off):
    return lax.broadcasted_iota(jnp.int32, (tk, tq), 0) + off <= lax.broadcasted_iota(jnp.int32, (tk, tq), 1)


ATTN_FWD_TILE = (1024, 1024)
ATTN_BWD_TILE = (1024, 512)
V_ROWS = 80


def _attn_fwd(q, k, vt, tq, tk):
    n_tok = q.shape[0]
    nq, ratio = n_tok // tq, tq // tk
    qi, kj = _causal_pairs(nq, ratio, by_kv=False)

    def kern(qi_ref, kj_ref, q_ref, k_ref, vt_ref, o_ref, lse_ref, m_s, acc_s, st_s):
        s_id = pl.program_id(0)
        i, j = qi_ref[s_id], kj_ref[s_id]

        @pl.when(j == 0)
        def _():
            m_s[...] = jnp.full(m_s.shape, NEG, F32)
            acc_s[...] = jnp.zeros(acc_s.shape, F32)

        def scores(h):
            hs = slice(h * HEAD_SLOT, (h + 1) * HEAD_SLOT)
            return _dot_nt(k_ref[:, hs], q_ref[:, hs])

        def heads(masked):
            keep = _keep_t(tk, tq, j * tk - i * tq) if masked else None
            st_s[0] = scores(0)
            for h in range(N_HEADS):
                if h + 1 < N_HEADS:
                    st_s[(h + 1) % 2] = scores(h + 1)
                st = st_s[h % 2]
                if masked:
                    st = jnp.where(keep, st, NEG)
                m_old = m_s[h]
                m_new = jnp.maximum(m_old, jnp.max(st, axis=0, keepdims=True))
                pt = jnp.exp2(st - m_new)
                acc_s[h] = jnp.exp2(m_old - m_new) * acc_s[h] + _dot(vt_ref[h * V_ROWS:(h + 1) * V_ROWS, :], pt)
                m_s[h] = m_new

        @pl.when(j < i * ratio)
        def _():
            heads(False)

        @pl.when(j >= i * ratio)
        def _():
            heads(True)

        @pl.when(j == (i + 1) * ratio - 1)
        def _():
            heads_out = []
            for h in range(N_HEADS):
                total = acc_s[h, V_HEAD:V_HEAD + 1, :]
                heads_out.append(acc_s[h, :V_HEAD, :] / total)
                lse_ref[h:h + 1, :] = m_s[h] + jnp.log2(total)
            o_ref[...] = jnp.concatenate(heads_out, 0).T.astype(o_ref.dtype)

    gs = pltpu.PrefetchScalarGridSpec(
        num_scalar_prefetch=2, grid=(qi.shape[0],),
        in_specs=[pl.BlockSpec((tq, QK_WIDTH), lambda s, qi, kj: (qi[s], 0)),
                  pl.BlockSpec((tk, QK_WIDTH), lambda s, qi, kj: (kj[s], 0)),
                  pl.BlockSpec((N_HEADS * V_ROWS, tk), lambda s, qi, kj: (0, kj[s]))],
        out_specs=[pl.BlockSpec((tq, N_HEADS * V_HEAD), lambda s, qi, kj: (qi[s], 0)),
                   pl.BlockSpec((N_HEADS, tq), lambda s, qi, kj: (0, qi[s]))],
        scratch_shapes=[pltpu.VMEM((N_HEADS, 1, tq), F32), pltpu.VMEM((N_HEADS, V_ROWS, tq), F32),
                        pltpu.VMEM((2, tk, tq), F32)])
    return pl.pallas_call(kern, name="attn_fwd", grid_spec=gs,
                          out_shape=[_sds(n_tok, N_HEADS * V_HEAD, MX), _sds(N_HEADS, n_tok, F32)],
                          compiler_params=_params(("arbitrary",)))(qi, kj, q, k, vt)


def _pool_windows(ext, i, tm, first_row):
    row = i * tm + lax.broadcasted_iota(jnp.int32, (tm, 1), 0)
    out = []
    for g, w in enumerate(POOL_WINDOWS):
        cs = slice(g * POOL_GROUP, (g + 1) * POOL_GROUP)
        s = ext[pl.ds(first_row, tm), cs]
        for k in range(1, w):
            s = s + ext[pl.ds(first_row - k, tm), cs]
        cnt = jnp.minimum(row + 1, w).astype(F32)
        out.append(s / cnt)
    return out


def _fwd_mix(x, u, gl, attn, w, tm):
    n_tok = x.shape[0]
    halo_spec = pl.BlockSpec((HALO, POOL_WIDTH), lambda i: (jnp.maximum(i * (tm // HALO) - 1, 0), 0))

    def body(i, tin, res, tout, acc, scr):
        x_ref, u_ref, uh_ref, gl_ref, at_ref = tin
        w_pool, pool_scale, w_ba, w_bp, b_gate, w_out, g_post = res
        d_o, pooled_o, a_o, pp_o, merged_o, y_o, h1_o = tout
        ext, = scr
        ext[pl.ds(0, HALO), :] = jnp.where(i > 0, uh_ref[...].astype(F32), 0.0)
        ext[pl.ds(HALO, tm), :] = u_ref[...].astype(F32)
        means = _pool_windows(ext, i, tm, HALO)
        for g in range(len(POOL_WINDOWS)):
            cs = slice(g * POOL_GROUP, (g + 1) * POOL_GROUP)
            d = (means[g] - ext[pl.ds(HALO, tm), cs]).astype(MX)
            d_o[:, cs] = d
            pooled_o[:, cs] = (_dot(d, w_pool[g]) * pool_scale[:, cs]).astype(pooled_o.dtype)
        a_br = _dot(at_ref[...], w_ba[...])
        p_br = _dot(pooled_o[...], w_bp[...])
        a_o[...] = a_br.astype(a_o.dtype)
        pp_o[...] = p_br.astype(pp_o.dtype)
        gates = _sig(gl_ref[...].astype(F32) + b_gate[...])
        merged = (gates[:, :D_MODEL] * a_br + gates[:, D_MODEL:] * p_br).astype(MX)
        merged_o[...] = merged
        y = _dot(merged, w_out[...])
        y_o[...] = y.astype(y_o.dtype)
        h1_o[...] = x_ref[...] + _rms(y, g_post[...])[0]

    outs = [_sds(n_tok, POOL_WIDTH, MX), _sds(n_tok, POOL_WIDTH, MX), _sds(n_tok, D_MODEL, MX), _sds(n_tok, D_MODEL, MX),
            _sds(n_tok, D_MODEL, MX), _sds(n_tok, D_MODEL, MX), _sds(n_tok, D_MODEL, F32)]
    res = [w["w_pool"], w["pool_scale"], w["w_branch_attn"], w["w_branch_pool"], w["b_gate"], w["w_out"], w["g_post_mix"]]
    return _tok_call("fwd_mix", body, n_tok, tm, [x, u, (u, halo_spec), gl, attn], res, outs,
                     scratch=[pltpu.VMEM((tm + HALO, POOL_WIDTH), F32)])


def _fwd_mlp(h1, w, tm):
    n_tok = h1.shape[0]

    def body(i, tin, res, tout, acc, scr):
        h1_ref, = tin
        g_pre, w1, w2, g_post = res
        m_o, zr_o, a2_o, f_o, h2_o = tout
        h1_ = h1_ref[...]
        m = _rms(h1_, g_pre[...])[0].astype(MX)
        m_o[...] = m
        zr = jnp.maximum(_dot(m, w1[...]), 0.0)
        zr_o[...] = zr.astype(zr_o.dtype)
        a2 = (zr * zr).astype(MX)
        a2_o[...] = a2
        f = _dot(a2, w2[...])
        f_o[...] = f.astype(f_o.dtype)
        h2_o[...] = h1_ + _rms(f, g_post[...])[0]

    outs = [_sds(n_tok, D_MODEL, MX), _sds(n_tok, D_FF, MX), _sds(n_tok, D_FF, MX), _sds(n_tok, D_MODEL, MX),
            _sds(n_tok, D_MODEL, F32)]
    res = [w["g_pre_mlp"], w["w_ff1"], w["w_ff2"], w["g_post_mlp"]]
    return _tok_call("fwd_mlp", body, n_tok, tm, [h1], res, outs)


def _ple_fwd_bwd(h2, p, target, w, tm):
    n_tok = h2.shape[0]

    def body(i, tin, res, tout, acc, scr):
        h2_ref, p_ref, t_ref = tin
        w_pe, w_pg, g_ple = res
        dh2_o, de_o, dzg_o = tout
        loss_a, dg_a = acc
        h2_ = h2_ref[...]
        e = _dot(p_ref[...], w_pe[...])
        pg = _sig(_dot(h2_, w_pg[...]))
        t = pg * e
        g = g_ple[...]
        tn, th, r = _rms(t, g)
        diff = h2_ + tn - t_ref[...]
        loss_a[...] += jnp.sum(diff * diff, axis=0, keepdims=True)
        dh3 = diff * (1.0 / D_MODEL)
        dt, dg = _rms_bwd(th, r, g, dh3)
        dg_a[...] += dg
        de_o[...] = (dt * pg).astype(de_o.dtype)
        dzg = (dt * e * pg * (1.0 - pg)).astype(MX)
        dzg_o[...] = dzg
        dh2_o[...] = dh3 + _dot_nt(dzg, w_pg[...])

    outs = [_sds(n_tok, D_MODEL, F32), _sds(n_tok, D_MODEL, MX), _sds(n_tok, D_MODEL, MX)]
    accs = [_sds(1, D_MODEL, F32), _sds(1, D_MODEL, F32)]
    return _tok_call("ple_fwd_bwd", body, n_tok, tm, [h2, p, target], [w["w_ple_proj"], w["w_ple_gate"], w["g_ple"]], outs, accs)


def _bwd_mlp(dh2, f, h1, zr, w, tm):
    n_tok = dh2.shape[0]

    def body(i, tin, res, tout, acc, scr):
        dh2_ref, f_ref, h1_ref, zr_ref = tin
        g_pre, w1, w2, g_post = res
        df_o, dz_o, dh1_o = tout
        dg_post_a, dg_pre_a = acc
        dh2_ = dh2_ref[...]
        gp = g_post[...]
        _, fh, rf = _rms(f_ref[...].astype(F32), gp)
        df, dg = _rms_bwd(fh, rf, gp, dh2_)
        dg_post_a[...] += dg
        df = df.astype(MX)
        df_o[...] = df
        dz = (_dot_nt(df, w2[...]) * (2.0 * zr_ref[...].astype(F32))).astype(MX)
        dz_o[...] = dz
        dm = _dot_nt(dz, w1[...])
        gq = g_pre[...]
        _, hh, rh = _rms(h1_ref[...], gq)
        dh1, dg = _rms_bwd(hh, rh, gq, dm)
        dg_pre_a[...] += dg
        dh1_o[...] = dh2_ + dh1

    outs = [_sds(n_tok, D_MODEL, MX), _sds(n_tok, D_FF, MX), _sds(n_tok, D_MODEL, F32)]
    accs = [_sds(1, D_MODEL, F32), _sds(1, D_MODEL, F32)]
    res = [w["g_pre_mlp"], w["w_ff1"], w["w_ff2"], w["g_post_mlp"]]
    return _tok_call("bwd_mlp", body, n_tok, tm, [dh2, f, h1, zr], res, outs, accs)


def _bwd_mix(dh1, y, a_br, p_br, gl, attn, d, w, tm):
    n_tok = dh1.shape[0]

    def body(i, tin, res, tout, acc, scr):
        dh1_ref, y_ref, a_ref, pp_ref, gl_ref, at_ref, d_ref = tin
        g_post, w_out, b_gate, w_ba, w_bp, w_pool, pool_scale, sel = res
        dy_o, da_o, dpp_o, dgl_o, do_o, delta_o, dyp_o, dd_o = tout
        dg_post_a, db_a, dps_a = acc
        g = g_post[...]
        _, yh, r = _rms(y_ref[...].astype(F32), g)
        dy, dg = _rms_bwd(yh, r, g, dh1_ref[...])
        dg_post_a[...] += dg
        dy = dy.astype(MX)
        dy_o[...] = dy
        dmerged = _dot_nt(dy, w_out[...])
        gates = _sig(gl_ref[...].astype(F32) + b_gate[...])
        ga, gp = gates[:, :D_MODEL], gates[:, D_MODEL:]
        da = (dmerged * ga).astype(MX)
        dpp = (dmerged * gp).astype(MX)
        da_o[...] = da
        dpp_o[...] = dpp
        dgl_a = dmerged * a_ref[...].astype(F32) * ga * (1.0 - ga)
        dgl_p = dmerged * pp_ref[...].astype(F32) * gp * (1.0 - gp)
        dgl_o[:, :D_MODEL] = dgl_a.astype(dgl_o.dtype)
        dgl_o[:, D_MODEL:] = dgl_p.astype(dgl_o.dtype)
        db_a[:, :D_MODEL] += jnp.sum(dgl_a, axis=0, keepdims=True)
        db_a[:, D_MODEL:] += jnp.sum(dgl_p, axis=0, keepdims=True)
        do = _dot_nt(da, w_ba[...]).astype(MX)
        do_o[...] = do
        prod = do.astype(F32) * at_ref[...].astype(F32)
        hi = prod.astype(MX)
        lo = (prod - hi.astype(F32)).astype(MX)
        delta_o[...] = _dot(hi, sel[...]) + _dot(lo, sel[...])
        dpooled = _dot_nt(dpp, w_bp[...])
        for gi in range(len(POOL_WINDOWS)):
            cs = slice(gi * POOL_GROUP, (gi + 1) * POOL_GROUP)
            ypre = _dot(d_ref[:, cs], w_pool[gi])
            dps_a[:, cs] += jnp.sum(dpooled[:, cs] * ypre, axis=0, keepdims=True)
            dyp = (dpooled[:, cs] * pool_scale[:, cs]).astype(MX)
            dyp_o[:, cs] = dyp
            dd_o[:, cs] = _dot_nt(dyp, w_pool[gi])

    outs = [_sds(n_tok, D_MODEL, MX), _sds(n_tok, D_MODEL, MX), _sds(n_tok, D_MODEL, MX), _sds(n_tok, 2 * D_MODEL, MX),
            _sds(n_tok, N_HEADS * V_HEAD, MX), _sds(n_tok, HEAD_SLOT, F32), _sds(n_tok, POOL_WIDTH, MX),
            _sds(n_tok, POOL_WIDTH, F32)]
    accs = [_sds(1, D_MODEL, F32), _sds(1, 2 * D_MODEL, F32), _sds(1, POOL_WIDTH, F32)]
    res = [w["g_post_mix"], w["w_out"], w["b_gate"], w["w_branch_attn"], w["w_branch_pool"], w["w_pool"], w["pool_scale"],
           w["head_sel"]]
    return _tok_call("bwd_mix", body, n_tok, tm, [dh1, y, a_br, p_br, gl, attn, d], res, outs, accs)


def _bwd_heads(q_ref, k_ref, v_ref, do_ref, lse_ref, dl_ref, st_s, dpt_s, keep, use, n_heads):
    def products(h):
        hs = slice(h * HEAD_SLOT, (h + 1) * HEAD_SLOT)
        vs = slice(h * V_HEAD, (h + 1) * V_HEAD)
        st_s[h % 2] = _dot_nt(k_ref[:, hs], q_ref[:, hs])
        dpt_s[h % 2] = _dot_nt(v_ref[:, vs], do_ref[:, vs])

    products(0)
    for h in range(n_heads):
        if h + 1 < n_heads:
            products(h + 1)
        st = st_s[h % 2]
        if keep is not None:
            st = jnp.where(keep, st, NEG)
        pt = jnp.exp2(st - lse_ref[h:h + 1, :])
        use(h, pt, pt * (dpt_s[h % 2] - dl_ref[h:h + 1, :]))


HEAD_GROUP = 4


def _attn_bwd(q, k, kt, v, do, lse, delta, tq, tk, scatter=None):
    n_tok = q.shape[0]
    nq, ratio = n_tok // tq, tq // tk
    n_groups = N_HEADS // HEAD_GROUP
    gq, gv = HEAD_GROUP * HEAD_SLOT, HEAD_GROUP * V_HEAD
    qi, kj = _causal_pairs(nq, ratio, by_kv=True)

    n_pairs = qi.shape[0]

    def kern(qi_ref, kj_ref, q_ref, k_ref, kt_ref, v_ref, do_ref, lse_ref, dl_ref, *rest):
        if scatter is not None:
            s_hbm, dq_ref, dk_ref, dv_ref, r_hbm, dk_s, dv_s, st_s, dpt_s, send_sems, recv_sems = rest
        else:
            dq_ref, dk_ref, dv_ref, dk_s, dv_s, st_s, dpt_s = rest
        s_id = pl.program_id(1)
        i, j = qi_ref[s_id], kj_ref[s_id]
        cols = pl.ds(pl.multiple_of(i * tq, tq), tq)
        if scatter is not None:
            group = pl.program_id(0)
            _scatter_steps(jnp.logical_and(group == 0, s_id == 0),
                           jnp.logical_and(group == n_groups - 1, s_id == n_pairs - 1), s_hbm, r_hbm, (send_sems, recv_sems))

        @pl.when(s_id == 0)
        def _():
            dq_ref[...] = jnp.zeros(dq_ref.shape, F32)

        def use(h, pt, dst):
            hs = slice(h * HEAD_SLOT, (h + 1) * HEAD_SLOT)
            dv_s[h] += _dot(pt, do_ref[:, h * V_HEAD:(h + 1) * V_HEAD])
            dk_s[:, hs] += _dot(dst, q_ref[:, hs])
            dq_ref[hs, cols] += _dot(kt_ref[hs, :], dst)

        def heads(masked):
            keep = _keep_t(tk, tq, j * tk - i * tq) if masked else None
            _bwd_heads(q_ref, k_ref, v_ref, do_ref, lse_ref.at[0], dl_ref.at[0], st_s, dpt_s, keep, use, HEAD_GROUP)

        @pl.when(j >= i * ratio)
        def _():
            dk_s[...] = jnp.zeros(dk_s.shape, F32)
            dv_s[...] = jnp.zeros(dv_s.shape, F32)
            heads(True)

        @pl.when(j < i * ratio)
        def _():
            heads(False)

        @pl.when(i == nq - 1)
        def _():
            dk_ref[...] = (dk_s[...] * (1.0 / LOG2E)).astype(dk_ref.dtype)
            for h in range(HEAD_GROUP):
                dv_ref[:, h * V_HEAD:(h + 1) * V_HEAD] = dv_s[h].astype(dv_ref.dtype)

    at_q = lambda g, s, qi, kj: (qi[s], g)
    at_k = lambda g, s, qi, kj: (kj[s], g)
    at_kt = lambda g, s, qi, kj: (g, kj[s])
    at_stat = lambda g, s, qi, kj: (g, 0, qi[s])
    in_specs = [pl.BlockSpec((tq, gq), at_q), pl.BlockSpec((tk, gq), at_k), pl.BlockSpec((gq, tk), at_kt),
                pl.BlockSpec((tk, gv), at_k), pl.BlockSpec((tq, gv), at_q),
                pl.BlockSpec((1, HEAD_GROUP, tq), at_stat), pl.BlockSpec((1, HEAD_GROUP, tq), at_stat)]
    out_specs = [pl.BlockSpec((gq, n_tok), lambda g, s, qi, kj: (g, 0), pipeline_mode=pl.Buffered(1)),
                 pl.BlockSpec((tk, gq), at_k), pl.BlockSpec((tk, gv), at_k)]
    out_shape = [_sds(QK_WIDTH, n_tok, F32), _sds(n_tok, QK_WIDTH, MX), _sds(n_tok, N_HEADS * V_HEAD, MX)]
    scratch = [pltpu.VMEM((tk, gq), F32), pltpu.VMEM((HEAD_GROUP, tk, V_HEAD), F32),
               pltpu.VMEM((2, tk, tq), F32), pltpu.VMEM((2, tk, tq), F32)]
    stat3 = lambda a: a.reshape(n_groups, HEAD_GROUP, n_tok)
    operands = [qi, kj, q, k, kt, v, do, stat3(lse), stat3(delta)]
    if scatter is not None:
        operands.append(scatter)
        in_specs.append(_HBM)
        out_specs.append(_HBM)
        out_shape.append(jax.ShapeDtypeStruct((N_DEVICES, scatter.shape[1] // 2, PACK_COLS), scatter.dtype))
        scratch += [pltpu.SemaphoreType.DMA((N_DEVICES - 1,)), pltpu.SemaphoreType.DMA((N_DEVICES - 1,))]
    gs = pltpu.PrefetchScalarGridSpec(num_scalar_prefetch=2, grid=(n_groups, n_pairs), in_specs=in_specs,
                                      out_specs=out_specs, scratch_shapes=scratch)
    return pl.pallas_call(kern, name="attn_bwd", grid_spec=gs, out_shape=out_shape,
                          compiler_params=_params(("arbitrary", "arbitrary")))(*operands)


def _bwd_inproj(dq_t, dk, dv, dd, dgl, ps, x, dh1, cc, sa, w, tm):
    n_tok = x.shape[0]
    n_tiles = n_tok // tm
    last_halo = n_tok // HALO - 1
    halo_spec = pl.BlockSpec((HALO, POOL_WIDTH), lambda i: (jnp.minimum((i + 1) * (tm // HALO), last_halo), 0))

    def body(i, tin, res, tout, acc, scr):
        dq_ref, dk_ref, dv_ref, dd_ref, ddh_ref, dgl_ref, ps_ref, x_ref, dh1_ref, c_ref, s_ref = tin
        w_uq, g_q, w_k, w_v, e_mat, g_kv, w_in, g_pre = res
        dqu_o, dproj_o, dx_o = tout
        dgq_a, dgkv_a, dgpre_a = acc
        ext, = scr
        cc_, sa_ = c_ref[...], s_ref[...]
        for h in range(N_HEADS):
            hs = slice(h * HEAD_SLOT, (h + 1) * HEAD_SLOT)
            dqu_o[:, hs] = (_unrope(dq_ref[hs, :].T, cc_, sa_) * SCALE).astype(dqu_o.dtype)
        gq = g_q[...]
        _, qh, rq = _rms(ps_ref[:, :Q_LORA].astype(F32), gq)
        dqd, dg = _rms_bwd(qh, rq, gq, _dot_nt(dqu_o[...], w_uq[...]))
        dgq_a[...] += dg
        dproj_o[:, :Q_LORA] = dqd.astype(dproj_o.dtype)
        gkv = g_kv[...]
        _, kh, rk = _rms(ps_ref[:, Q_LORA:Q_LORA + KV_LORA].astype(F32), gkv)
        dkvd, dg = _rms_bwd(kh, rk, gkv, _dot_nt(dk_ref[...], w_k[...]) + _dot_nt(dv_ref[...], w_v[...]))
        dgkv_a[...] += dg
        dproj_o[:, Q_LORA:Q_LORA + KV_LORA] = dkvd.astype(dproj_o.dtype)
        dproj_o[:, Q_LORA + KV_LORA:SMALL_COLS] = _unrope(_dot_nt(dk_ref[...], e_mat[...]), cc_, sa_).astype(dproj_o.dtype)
        row = i * tm + lax.broadcasted_iota(jnp.int32, (tm + HALO, 1), 0)
        for gi, wdw in enumerate(POOL_WINDOWS):
            cs = slice(gi * POOL_GROUP, (gi + 1) * POOL_GROUP)
            inv = 1.0 / jnp.minimum(row + 1, wdw).astype(F32)
            ext[pl.ds(0, tm), cs] = dd_ref[:, cs] * inv[:tm]
            ext[pl.ds(tm, HALO), cs] = jnp.where(i < n_tiles - 1, ddh_ref[:, cs] * inv[tm:], 0.0)
            s = ext[pl.ds(0, tm), cs]
            for k_ in range(1, wdw):
                s = s + ext[pl.ds(k_, tm), cs]
            dproj_o[:, SMALL_COLS + gi * POOL_GROUP:SMALL_COLS + (gi + 1) * POOL_GROUP] = (s - dd_ref[:, cs]).astype(dproj_o.dtype)
        dproj_o[:, SMALL_COLS + POOL_WIDTH:] = dgl_ref[...]
        da = _dot_nt(dproj_o[...], w_in[...])
        gp = g_pre[...]
        _, xh, rx = _rms(x_ref[...], gp)
        dx, dg = _rms_bwd(xh, rx, gp, da)
        dgpre_a[...] += dg
        dx_o[...] = dh1_ref[...] + dx

    outs = [_sds(n_tok, QK_WIDTH, MX), _sds(n_tok, IN_PAD, MX), _sds(n_tok, D_MODEL, F32)]
    accs = [_sds(1, Q_LORA, F32), _sds(1, KV_LORA, F32), _sds(1, D_MODEL, F32)]
    res = [w["w_uq"], w["g_q"], w["w_k"], w["w_v"], w["e_mat"], w["g_kv"], w["w_in"], w["g_pre_mix"]]
    dq_spec = pl.BlockSpec((QK_WIDTH, tm), lambda i: (0, i))
    return _tok_call("bwd_inproj", body, n_tok, tm, [(dq_t, dq_spec), dk, dv, dd, (dd, halo_spec), dgl, ps, x, dh1, cc, sa], res, outs, accs,
                     scratch=[pltpu.VMEM((tm + HALO, POOL_WIDTH), F32)])


XTDY_TOKENS = 1024
XTDY_OUT_BYTES = 8 * 2**20


def _xtdy(name, x, dy, allreduce=None):
    n_tok, kk = x.shape
    nn = dy.shape[1]
    bt = min(XTDY_TOKENS, n_tok)
    bk = kk
    while bk * nn * 4 > XTDY_OUT_BYTES and bk % 256 == 0:
        bk //= 2

    grid = (kk // bk, n_tok // bt)

    def kern(x_ref, dy_ref, *rest):
        o_ref = rest[1] if allreduce is not None else rest[0]
        if allreduce is not None:
            g_ref, _, sum_ref, buf, send_sems, recv_sems = rest
            step = pl.program_id(0) * grid[1] + pl.program_id(1)
            _allreduce_steps(step == 0, step == grid[0] * grid[1] - 1, g_ref, sum_ref, buf, send_sems, recv_sems)

        @pl.when(pl.program_id(1) == 0)
        def _():
            o_ref[...] = jnp.zeros(o_ref.shape, F32)
        o_ref[...] += _dot_tn(x_ref[...], dy_ref[...])

    operands = [x, dy]
    in_specs = [pl.BlockSpec((bt, bk), lambda a, t: (t, a)), pl.BlockSpec((bt, nn), lambda a, t: (t, 0))]
    out_specs = [pl.BlockSpec((bk, nn), lambda a, t: (a, 0))]
    out_shape = [_sds(kk, nn, F32)]
    scratch = []
    if allreduce is not None:
        vmem = pl.BlockSpec(memory_space=pltpu.VMEM)
        operands.append(allreduce)
        in_specs.append(vmem)
        out_specs.append(vmem)
        out_shape.append(jax.ShapeDtypeStruct(allreduce.shape, allreduce.dtype))
        scratch = [pltpu.VMEM((N_DEVICES,) + allreduce.shape, allreduce.dtype), pltpu.SemaphoreType.DMA((N_DEVICES - 1,)),
                   pltpu.SemaphoreType.DMA((N_DEVICES - 1,))]
    res = pl.pallas_call(kern, name=name, grid=grid, in_specs=in_specs, out_specs=out_specs, out_shape=out_shape,
                         scratch_shapes=scratch, compiler_params=_params(("arbitrary", "arbitrary")))(*operands)
    return res if allreduce is not None else res[0]


def _rope_tables(positions):
    inv_freq = ROPE_THETA ** (-jnp.arange(0, QK_ROPE, 2, dtype=F32) / QK_ROPE)
    ang_t = inv_freq[:, None] * positions.astype(F32)[None, :]
    cos_t, sin_t = lax.optimization_barrier((jnp.cos(ang_t), jnp.sin(ang_t)))
    cos, sin = cos_t.T, sin_t.T
    n_tok = positions.shape[0]
    ones, z64 = jnp.ones((n_tok, ROPE_LANE), F32), jnp.zeros((n_tok, ROPE_LANE), F32)
    z32 = jnp.zeros((n_tok, HEAD_SLOT - ROPE_LANE - QK_ROPE), F32)
    return jnp.concatenate([ones, cos, cos, z32], 1), jnp.concatenate([z64, -sin, sin, z32], 1)


def _kernel_weights(full):
    w_in, w_uq, w_ukv = full["w_in"], full["w_uq"], full["w_ukv"]
    c0 = Q_LORA + KV_LORA
    z = lambda n: jnp.zeros((D_MODEL, n), w_in.dtype)
    w = dict(full)
    w["w_in"] = jnp.concatenate([w_in[:, :c0], z(ROPE_LANE), w_in[:, c0:c0 + QK_ROPE], z(HEAD_SLOT - ROPE_LANE - QK_ROPE),
                                 w_in[:, c0 + QK_ROPE:]], 1)
    w["w_uq"] = jnp.pad(w_uq.reshape(Q_LORA, N_HEADS, QK_NOPE + QK_ROPE),
                        ((0, 0), (0, 0), (0, HEAD_SLOT - QK_NOPE - QK_ROPE))).reshape(Q_LORA, QK_WIDTH)
    kv = w_ukv.reshape(KV_LORA, N_HEADS, QK_NOPE + V_HEAD)
    w["w_k"] = jnp.pad(kv[:, :, :QK_NOPE], ((0, 0), (0, 0), (0, HEAD_SLOT - QK_NOPE))).reshape(KV_LORA, QK_WIDTH)
    w["w_v"] = kv[:, :, QK_NOPE:].reshape(KV_LORA, N_HEADS * V_HEAD)
    e = np.zeros((HEAD_SLOT, QK_WIDTH), np.float32)
    sel = np.zeros((N_HEADS * V_HEAD, HEAD_SLOT), np.float32)
    for h in range(N_HEADS):
        for r in range(QK_ROPE):
            e[ROPE_LANE + r, h * HEAD_SLOT + ROPE_LANE + r] = 1.0
        sel[h * V_HEAD:(h + 1) * V_HEAD, h] = 1.0
    w["e_mat"] = jnp.asarray(e, MX)
    w["w_kt"], w["e_t"] = w["w_k"].T, jnp.asarray(e.T, MX)
    pad = ((0, 0), (0, V_ROWS - V_HEAD), (0, 0))
    w["w_vt"] = jnp.pad(w["w_v"].T.reshape(N_HEADS, V_HEAD, KV_LORA), pad).reshape(N_HEADS * V_ROWS, KV_LORA)
    ones = np.zeros((N_HEADS, V_ROWS, 1), np.float32)
    ones[:, V_HEAD] = 1.0
    w["v_ones"] = jnp.asarray(ones.reshape(N_HEADS * V_ROWS, 1))
    w["head_sel"] = jnp.asarray(sel, MX)
    w["w_pool"] = full["w_pool"].astype(MX)
    return w


def _local_step(x, p, positions, target, full, mesh_place=None, packed_rest=None):
    n_tok = x.shape[0]
    tm = min(512, n_tok)
    tm_mlp = min(256, n_tok)
    fwd_tile = [min(t, n_tok) for t in ATTN_FWD_TILE]
    bwd_tile = [min(t, n_tok) for t in ATTN_BWD_TILE]
    w = _kernel_weights(full)
    cc, sa = _rope_tables(positions)

    if mesh_place is None:
        a, ps, u, gl, qn, kvn, q, k, v, kt, vt = _fwd_inproj(x, cc, sa, w, tm)
    else:
        my_chip, core = mesh_place
        a, ps, u, gl, qn, kvn, q, k, v, kt, vt, gathered = _fwd_inproj(x, cc, sa, w, tm, gather=packed_rest)
        w.update(_unpack_full(lax.dynamic_update_slice(gathered, packed_rest[None], (my_chip, 0, 0)), REST))
    attn, lse = _attn_fwd(q, k, vt, *fwd_tile)
    d, pooled, a_br, p_br, merged, y, h1 = _fwd_mix(x, u, gl, attn, w, tm)
    m, zr, a2, f, h2 = _fwd_mlp(h1, w, tm_mlp)
    dh2, de, dzg, loss_cols, dg_ple = _ple_fwd_bwd(h2, p, target, w, tm)
    df, dz, dh1, dg_post_mlp, dg_pre_mlp = _bwd_mlp(dh2, f, h1, zr, w, tm_mlp)
    dy, da_br, dp_br, dgl, do, delta, dyp, dd, dg_post_mix, db_gate, dpool_scale = _bwd_mix(dh1, y, a_br, p_br, gl, attn, d, w, tm)
    grads = {"w_branch_attn": _xtdy("dw_ba", attn, da_br), "w_branch_pool": _xtdy("dw_bp", pooled, dp_br),
             "w_out": _xtdy("dw_out", merged, dy), "w_ff1": _xtdy("dw_ff1", m, dz), "w_ff2": _xtdy("dw_ff2", a2, df),
             "w_ple_proj": _xtdy("dw_pe", p, de), "w_ple_gate": _xtdy("dw_pg", h2, dzg)}
    delta_t = delta[:, :N_HEADS].T
    if mesh_place is None:
        travelling = None
        dq_t, dk, dv = _attn_bwd(q, k, kt, v, do, lse, delta_t, *bwd_tile)
    else:
        pieces = _pack_pieces(grads, REST, WIRE)
        dq_t, dk, dv, received = _attn_bwd(q, k, kt, v, do, lse, delta_t, *bwd_tile, scatter=pieces)
        travelling = (pieces, received)
        grads = {}
    dqu, dproj, dx, dg_q, dg_kv, dg_pre_mix = _bwd_inproj(dq_t, dk, dv, dd, dgl, ps, x, dh1, cc, sa, w, tm)

    g_uq = _xtdy("dw_uq", qn, dqu)
    g_k = _xtdy("dw_k", kvn, dk)
    g_v = _xtdy("dw_v", kvn, dv)
    g_pool = _xtdy("dw_pool", d, dyp)
    small = {"g_pre_mix": dg_pre_mix, "b_gate": db_gate, "g_q": dg_q, "g_kv": dg_kv, "pool_scale": dpool_scale,
             "g_post_mix": dg_post_mix, "g_pre_mlp": dg_pre_mlp, "g_post_mlp": dg_post_mlp, "g_ple": dg_ple,
             "w_pool": jnp.stack([g_pool[g * POOL_GROUP:(g + 1) * POOL_GROUP, g * POOL_GROUP:(g + 1) * POOL_GROUP]
                                  for g in range(len(POOL_WINDOWS))])}
    if mesh_place is None:
        g_in = _xtdy("dw_in", a, dproj)
    else:
        g_in, small_sum = _xtdy("dw_in", a, dproj, allreduce=_pack_small(small))
        small = _unpack_small(small_sum)

    c0 = Q_LORA + KV_LORA
    grads.update(small)
    grads.update({
        "w_in": jnp.concatenate([g_in[:, :c0], g_in[:, c0 + ROPE_LANE:c0 + ROPE_LANE + QK_ROPE], g_in[:, SMALL_COLS:]], 1),
        "w_uq": g_uq.reshape(Q_LORA, N_HEADS, HEAD_SLOT)[:, :, :QK_NOPE + QK_ROPE].reshape(Q_LORA, N_HEADS * (QK_NOPE + QK_ROPE)),
        "w_ukv": jnp.concatenate([g_k.reshape(KV_LORA, N_HEADS, HEAD_SLOT)[:, :, :QK_NOPE],
                                  g_v.reshape(KV_LORA, N_HEADS, V_HEAD)], 2).reshape(KV_LORA, N_HEADS * (QK_NOPE + V_HEAD)),
    })
    return loss_cols, dx, grads, travelling


def _place():
    return lax.axis_index("x"), lax.axis_index("y"), lax.axis_index("c")


CHIP_FLIPS = ((1, 0), (0, 1), (1, 1))


def _flip(x, y, fx, fy):
    return (1 - x if fx else x), (1 - y if fy else y)


_HBM = pl.BlockSpec(memory_space=pl.ANY)


def _gather_copies(w_ref, out_ref, send_sems, recv_sems):
    half = w_ref.shape[0] // 2
    x, y, c = _place()
    my_chip = 2 * x + y
    sibling = (x, y, 1 - c)

    def half_of(chip, hc):
        return out_ref.at[chip, pl.ds(pl.multiple_of(hc * half, 16), half), :]

    src = w_ref.at[pl.ds(pl.multiple_of(c * half, 16), half), :]
    sends, landed, forwards, from_sibling = [], [], [], []
    for j, (fx, fy) in enumerate(CHIP_FLIPS):
        px, py = _flip(x, y, fx, fy)
        mine_there, theirs_here, theirs_other = half_of(my_chip, c), half_of(2 * px + py, c), half_of(2 * px + py, 1 - c)
        sends.append(pltpu.make_async_remote_copy(src, mine_there, send_sems.at[j], recv_sems.at[j],
                                                  device_id=(px, py, c), device_id_type=MESH))
        landed.append(pltpu.make_async_remote_copy(src, theirs_here, send_sems.at[j], recv_sems.at[j],
                                                   device_id=(px, py, c), device_id_type=MESH))
        forwards.append(pltpu.make_async_remote_copy(theirs_here, theirs_here, send_sems.at[3 + j], recv_sems.at[3 + j],
                                                     device_id=sibling, device_id_type=MESH))
        from_sibling.append(pltpu.make_async_remote_copy(theirs_other, theirs_other, send_sems.at[3 + j],
                                                         recv_sems.at[3 + j], device_id=sibling, device_id_type=MESH))
    return sends, landed, forwards, from_sibling


def _gather_steps(i, n_steps, w_ref, out_ref, sems):
    sends, landed, forwards, from_sibling = _gather_copies(w_ref, out_ref, *sems)

    @pl.when(i == 0)
    def _():
        for cp in sends:
            cp.start()

    @pl.when(i == (3 * n_steps) // 4)
    def _():
        for arrived, fwd in zip(landed, forwards):
            arrived.wait_recv()
            fwd.start()

    @pl.when(i == n_steps - 1)
    def _():
        for cp in from_sibling:
            cp.wait_recv()
        for cp in sends + forwards:
            cp.wait_send()


def _allgather_shards(wp):
    def body(w_ref, out_ref, send_sems, recv_sems):
        sends, landed, forwards, from_sibling = _gather_copies(w_ref, out_ref, send_sems, recv_sems)
        for cp in sends:
            cp.start()
        for arrived, fwd in zip(landed, forwards):
            arrived.wait_recv()
            fwd.start()
        for cp in from_sibling:
            cp.wait_recv()
        for cp in sends + forwards:
            cp.wait_send()

    return pl.pallas_call(
        body, name="allgather_shards", out_shape=jax.ShapeDtypeStruct((N_CHIPS,) + wp.shape, wp.dtype),
        in_specs=[_HBM], out_specs=_HBM,
        scratch_shapes=[pltpu.SemaphoreType.DMA((6,)), pltpu.SemaphoreType.DMA((6,))],
    )(wp)


def _exchange_halves(g):
    rows = g.shape[1]
    half = rows // 2

    def body(g_ref, r_ref, send_sem, recv_sem):
        x, y, c = _place()
        src = g_ref.at[:, pl.ds(pl.multiple_of((1 - c) * half, 8), half), :]
        cp = pltpu.make_async_remote_copy(src, r_ref, send_sem, recv_sem, device_id=(x, y, 1 - c), device_id_type=MESH)
        cp.start()
        cp.wait()

    return pl.pallas_call(
        body, name="exchange_halves", out_shape=jax.ShapeDtypeStruct((N_CHIPS, half, PACK_COLS), g.dtype),
        in_specs=[_HBM], out_specs=_HBM, scratch_shapes=[pltpu.SemaphoreType.DMA, pltpu.SemaphoreType.DMA],
    )(g)


def _add_halves(g, r, c):
    rows = g.shape[1]
    half = rows // 2
    br = REDUCE_ROWS
    nb = half // br

    def kern(c_ref, g_ref, r_ref, o_ref):
        o_ref[...] = (g_ref[...] + r_ref[...]).astype(o_ref.dtype)

    gs = pltpu.PrefetchScalarGridSpec(
        num_scalar_prefetch=1, grid=(N_CHIPS, nb),
        in_specs=[pl.BlockSpec((1, br, PACK_COLS), lambda k, t, c: (k, c[0] * nb + t, 0)),
                  pl.BlockSpec((1, br, PACK_COLS), lambda k, t, c: (k, t, 0))],
        out_specs=pl.BlockSpec((1, br, PACK_COLS), lambda k, t, c: (k, t, 0)))
    return pl.pallas_call(kern, name="add_halves", grid_spec=gs,
                          out_shape=jax.ShapeDtypeStruct((N_CHIPS, half, PACK_COLS), WIRE),
                          compiler_params=_params(("arbitrary", "arbitrary")))(c.reshape(1), g, r)


def _scatter_copies(s_ref, r_ref, send_sems, recv_sems):
    x, y, c = _place()
    my_chip = 2 * x + y
    sends, arrivals = [], []
    for j, (fx, fy) in enumerate(CHIP_FLIPS):
        px, py = _flip(x, y, fx, fy)
        slot = r_ref.at[2 * px + py]
        sends.append(pltpu.make_async_remote_copy(s_ref.at[2 * px + py], r_ref.at[my_chip], send_sems.at[j], recv_sems.at[j],
                                                  device_id=(px, py, c), device_id_type=MESH))
        arrivals.append(pltpu.make_async_remote_copy(slot, slot, send_sems.at[j], recv_sems.at[j],
                                                     device_id=(px, py, c), device_id_type=MESH))
    return sends, arrivals


N_DEVICES = 8


def _peer(x, y, c, f):
    px, py = _flip(x, y, f & 4, f & 2)
    return px, py, (1 - c if f & 1 else c)


def _scatter_all_copies(p_ref, r_ref, send_sems, recv_sems):
    half = p_ref.shape[1] // 2
    x, y, c = _place()
    me = 4 * x + 2 * y + c
    sends, arrivals = [], []
    for f in range(1, N_DEVICES):
        px, py, pc = _peer(x, y, c, f)
        theirs = p_ref.at[2 * px + py, pl.ds(pl.multiple_of(pc * half, 16), half), :]
        slot = r_ref.at[4 * px + 2 * py + pc]
        sends.append(pltpu.make_async_remote_copy(theirs, r_ref.at[me], send_sems.at[f - 1], recv_sems.at[f - 1],
                                                  device_id=(px, py, pc), device_id_type=MESH))
        arrivals.append(pltpu.make_async_remote_copy(slot, slot, send_sems.at[f - 1], recv_sems.at[f - 1],
                                                     device_id=(px, py, pc), device_id_type=MESH))
    return sends, arrivals


def _scatter_steps(first, last, p_ref, r_ref, sems):
    sends, arrivals = _scatter_all_copies(p_ref, r_ref, *sems)

    @pl.when(first)
    def _():
        for cp in sends:
            cp.start()

    @pl.when(last)
    def _():
        for cp in arrivals:
            cp.wait_recv()
        for cp in sends:
            cp.wait_send()


def _scatter_pieces(s):
    def body(s_ref, r_ref, send_sems, recv_sems):
        sends, arrivals = _scatter_copies(s_ref, r_ref, send_sems, recv_sems)
        for cp in sends:
            cp.start()
        for cp in arrivals:
            cp.wait_recv()
        for cp in sends:
            cp.wait_send()

    return pl.pallas_call(
        body, name="scatter_pieces", out_shape=jax.ShapeDtypeStruct(s.shape, s.dtype), in_specs=[_HBM], out_specs=_HBM,
        scratch_shapes=[pltpu.SemaphoreType.DMA((3,)), pltpu.SemaphoreType.DMA((3,))],
    )(s)


def _sum_pieces(r):
    slots, half = r.shape[:2]
    br = REDUCE_ROWS

    def kern(r_ref, o_ref):
        total = r_ref[0].astype(F32)
        for k in range(1, slots):
            total = total + r_ref[k].astype(F32)
        o_ref[...] = total

    return pl.pallas_call(
        kern, name="sum_pieces", grid=(half // br,), in_specs=[pl.BlockSpec((slots, br, PACK_COLS), lambda t: (0, t, 0))],
        out_specs=pl.BlockSpec((br, PACK_COLS), lambda t: (t, 0)), out_shape=_sds(half, PACK_COLS, F32),
        compiler_params=_params(("arbitrary",)))(r)


def _join_halves(f):
    def body(f_ref, o_ref, send_sem, recv_sem):
        x, y, c = _place()
        cp = pltpu.make_async_remote_copy(f_ref, o_ref, send_sem, recv_sem, device_id=(x, y, 1 - c), device_id_type=MESH)
        cp.start()
        cp.wait()

    return pl.pallas_call(
        body, name="join_halves", out_shape=jax.ShapeDtypeStruct(f.shape, f.dtype), in_specs=[_HBM], out_specs=_HBM,
        scratch_shapes=[pltpu.SemaphoreType.DMA, pltpu.SemaphoreType.DMA],
    )(f)


def _allreduce_steps(first, last, g_ref, o_ref, buf, send_sems, recv_sems):
    x, y, c = _place()
    me = 4 * x + 2 * y + c
    sends, arrivals = [], []
    for f in range(1, N_DEVICES):
        px, py, pc = _peer(x, y, c, f)
        slot = buf.at[4 * px + 2 * py + pc]
        sends.append(pltpu.make_async_remote_copy(g_ref, buf.at[me], send_sems.at[f - 1], recv_sems.at[f - 1],
                                                  device_id=(px, py, pc), device_id_type=MESH))
        arrivals.append(pltpu.make_async_remote_copy(slot, slot, send_sems.at[f - 1], recv_sems.at[f - 1],
                                                     device_id=(px, py, pc), device_id_type=MESH))

    @pl.when(first)
    def _():
        buf[me] = g_ref[...]
        for cp in sends:
            cp.start()

    @pl.when(last)
    def _():
        for cp in arrivals:
            cp.wait_recv()
        for cp in sends:
            cp.wait_send()
        total = buf[0]
        for k in range(1, N_DEVICES):
            total = total + buf[k]
        o_ref[...] = total


def _adamw_update(g_ref, w_ref, m_ref, v_ref, d_o, m_o, v_o):
    c1 = 1.0 - ADAM_B1 ** ADAM_STEP
    c2 = 1.0 - ADAM_B2 ** ADAM_STEP
    g_ = g_ref[...]
    m_new = ADAM_B1 * m_ref[...] + (1.0 - ADAM_B1) * g_
    v_new = ADAM_B2 * v_ref[...] + (1.0 - ADAM_B2) * (g_ * g_)
    m_o[...] = m_new
    v_o[...] = v_new
    d_o[...] = -ADAM_LR * ((m_new / c1) / (jnp.sqrt(v_new / c2) + ADAM_EPS) + ADAM_WD * w_ref[...])


ADAMW_ROWS = 256


def _adamw(name, g, w, m, v):
    _, rows, cols = w.shape
    br = int(np.gcd(ADAMW_ROWS, rows))

    def kern(*refs):
        _adamw_update(*refs)

    spec = pl.BlockSpec((1, br, cols), lambda t: (0, t, 0))
    out = jax.ShapeDtypeStruct(w.shape, F32)
    return pl.pallas_call(kern, name="adamw_" + name, grid=(rows // br,), in_specs=[spec] * 4, out_specs=[spec] * 3,
                          out_shape=[out, out, out], compiler_params=_params(("arbitrary",)))(g, w, m, v)


def _adamw_small(gs, ws, ms, vs):
    n = len(gs)

    def kern(*refs):
        ins, outs = refs[:4 * n], refs[4 * n:]
        for k in range(n):
            _adamw_update(ins[k], ins[n + k], ins[2 * n + k], ins[3 * n + k], outs[k], outs[n + k], outs[2 * n + k])

    vmem = pl.BlockSpec(memory_space=pltpu.VMEM)
    out = [jax.ShapeDtypeStruct(w.shape, F32) for w in ws]
    res = pl.pallas_call(kern, name="adamw_small", in_specs=[vmem] * (4 * n), out_specs=[vmem] * (3 * n),
                         out_shape=out * 3, compiler_params=pltpu.CompilerParams(vmem_limit_bytes=VMEM_LIMIT))(*gs, *ws, *ms, *vs)
    return [(res[k], res[n + k], res[2 * n + k]) for k in range(n)]


def _shard_rows(shape, axis):
    k, n = shape
    return (k * n // N_CHIPS) // PACK_COLS


def _group(names):
    entries = [e for e in SHARDED if e[0] in names]
    used = sum(_shard_rows(shape, axis) for _, shape, axis in entries)
    return entries, -(-used // (2 * REDUCE_ROWS)) * 2 * REDUCE_ROWS


def _pack_shards(shards, names, dtype):
    entries, rows = _group(names)
    parts = [shards[name].astype(dtype).reshape(-1, PACK_COLS) for name, _, _ in entries]
    used = sum(p.shape[0] for p in parts)
    if rows > used:
        parts.append(jnp.zeros((rows - used, PACK_COLS), dtype))
    return jnp.concatenate(parts, 0)


def _unpack_shards(packed, names):
    out, r0 = {}, 0
    for name, (k, n), axis in _group(names)[0]:
        nr = _shard_rows((k, n), axis)
        shape = (k // N_CHIPS, n) if axis == 0 else (k, n // N_CHIPS)
        out[name] = packed[r0:r0 + nr].reshape(shape)
        r0 += nr
    return out


def _unpack_full(gathered, names):
    out, r0 = {}, 0
    for name, (k, n), axis in _group(names)[0]:
        nr = _shard_rows((k, n), axis)
        part = gathered[:, r0:r0 + nr]
        if axis == 0:
            out[name] = part.reshape(k, n)
        else:
            out[name] = part.reshape(N_CHIPS, k, n // N_CHIPS).transpose(1, 0, 2).reshape(k, n)
        r0 += nr
    return out


def _pack_pieces(grads, names, dtype=F32):
    entries, rows = _group(names)
    parts = []
    for name, (k, n), axis in entries:
        g = grads[name].astype(dtype)
        if axis == 0:
            parts.append(g.reshape(N_CHIPS, -1, PACK_COLS))
        else:
            parts.append(g.reshape(k, N_CHIPS, n // N_CHIPS).transpose(1, 0, 2).reshape(N_CHIPS, -1, PACK_COLS))
    used = sum(p.shape[1] for p in parts)
    if rows > used:
        parts.append(jnp.zeros((N_CHIPS, rows - used, PACK_COLS), dtype))
    return jnp.concatenate(parts, 1)


def _pack_small(vals):
    flat = jnp.concatenate([vals[name].astype(F32).reshape(-1) for name, _ in SMALL])
    flat = jnp.concatenate([flat, jnp.zeros((SMALL_ROWS * PACK_COLS - flat.shape[0],), F32)])
    return flat.reshape(SMALL_ROWS, PACK_COLS)


def _unpack_small(packed):
    flat, out, o = packed.reshape(-1), {}, 0
    for name, shape in SMALL:
        n = int(np.prod(shape))
        out[name] = flat[o:o + n].reshape(shape)
        o += n
    return out


def kernel(x, p, positions, g_pre_mix, w_in, b_gate, g_q, w_uq, g_kv, w_ukv, w_pool, pool_scale, w_branch_attn, w_branch_pool, w_out, g_post_mix, g_pre_mlp, w_ff1, w_ff2, g_post_mlp, w_ple_proj, w_ple_gate, g_ple, loss_target, m_g_pre_mix, m_w_in, m_b_gate, m_g_q, m_w_uq, m_g_kv, m_w_ukv, m_w_pool, m_pool_scale, m_w_branch_attn, m_w_branch_pool, m_w_out, m_g_post_mix, m_g_pre_mlp, m_w_ff1, m_w_ff2, m_g_post_mlp, m_w_ple_proj, m_w_ple_gate, m_g_ple, v_g_pre_mix, v_w_in, v_b_gate, v_g_q, v_w_uq, v_g_kv, v_w_ukv, v_w_pool, v_pool_scale, v_w_branch_attn, v_w_branch_pool, v_w_out, v_g_post_mix, v_g_pre_mlp, v_w_ff1, v_w_ff2, v_g_post_mlp, v_w_ple_proj, v_w_ple_gate, v_g_ple):
    given = dict(locals())
    weights = {n: given[n] for n in WEIGHT_ORDER}
    moments_m = {n: given["m_" + n] for n in WEIGHT_ORDER}
    moments_v = {n: given["v_" + n] for n in WEIGHT_ORDER}
    c = lax.axis_index("c")

    big_w = {name: weights[name][0] for name, _, _ in SHARDED}
    my_chip = 2 * lax.axis_index("x") + lax.axis_index("y")
    packed_first = _pack_shards(big_w, FIRST, MX)
    full = _unpack_full(lax.dynamic_update_slice(_allgather_shards(packed_first), packed_first[None], (my_chip, 0, 0)), FIRST)
    for name, _ in SMALL:
        full[name] = weights[name][0] if name == "w_pool" else weights[name]

    loss_cols, dx, grads, (pieces_rest, received_rest) = _local_step(
        x[0], p[0, 0], positions[0], loss_target[0], full, (my_chip, c), _pack_shards(big_w, REST, MX))
    loss = lax.psum(0.5 * jnp.sum(loss_cols) / D_MODEL, ("x", "y", "c"))

    def finish(received, mine, slot):
        reduced = _sum_pieces(lax.dynamic_update_slice(received, mine, (slot, 0, 0)))
        theirs = _join_halves(reduced)
        return jnp.where(c == 0, jnp.concatenate([reduced, theirs]), jnp.concatenate([theirs, reduced]))

    pieces = _pack_pieces(grads, FIRST)
    sent = _add_halves(pieces, _exchange_halves(pieces), c)
    mine = lax.dynamic_slice(sent, (my_chip, 0, 0), (1,) + sent.shape[1:])
    shards = _unpack_shards(finish(_scatter_pieces(sent), mine, my_chip), FIRST)
    half = received_rest.shape[1]
    mine = lax.dynamic_slice(pieces_rest, (my_chip, c * half, 0), (1, half, PACK_COLS))
    shards.update(_unpack_shards(finish(received_rest, mine, 2 * my_chip + c), REST))

    out = {}
    for name, g in shards.items():
        out[name] = (g[None], *_adamw(name, g[None], weights[name], moments_m[name], moments_v[name]))
    small_g = {n: grads[n] for n, _ in SMALL}
    names = [n for n, _ in SMALL]
    updates = _adamw_small([small_g[n] for n in names], [weights[n] for n in names], [moments_m[n] for n in names],
                           [moments_v[n] for n in names])
    for n, upd in zip(names, updates):
        out[n] = (small_g[n], *upd)
    return (loss, dx[None], *[out[n][k] for k in range(4) for n in WEIGHT_ORDER])
```

```python
import functools

import numpy as np
import jax
import jax.numpy as jnp
from jax import lax
from jax.experimental import pallas as pl
from jax.experimental.pallas import tpu as pltpu

F32 = jnp.float32
MX = jnp.bfloat16
WIRE = jnp.bfloat16

D_MODEL = 1024
N_HEADS = 8
QK_NOPE = 64
QK_ROPE = 32
V_HEAD = 64
Q_LORA = 384
KV_LORA = 256
POOL_WINDOWS = (2, 4, 8, 16)
POOL_GROUP = 128
POOL_WIDTH = 512
D_FF = 4096
PLE_DIM = 256
ROPE_THETA = 10000.0
EPS = 1e-6
HEAD_SLOT = 128
QK_WIDTH = N_HEADS * HEAD_SLOT
ROPE_LANE = 64
SMALL_COLS = Q_LORA + KV_LORA + HEAD_SLOT
IN_PAD = SMALL_COLS + POOL_WIDTH + 2 * D_MODEL
SCALE = (QK_NOPE + QK_ROPE) ** -0.5
LOG2E = 1.4426950408889634
NEG = -1e30
HALO = 16

ADAM_LR = 0.001
ADAM_B1 = 0.9
ADAM_B2 = 0.999
ADAM_EPS = 1e-08
ADAM_WD = 0.01
ADAM_STEP = 10

VMEM_LIMIT = 56 * 2**20
TOKEN_TILE = 512
MESH = pl.DeviceIdType.MESH

SHARDED = (
    ("w_in", (1024, 3232), 1),
    ("w_uq", (384, 768), 1),
    ("w_ukv", (256, 1024), 1),
    ("w_branch_attn", (512, 1024), 1),
    ("w_branch_pool", (512, 1024), 1),
    ("w_out", (1024, 1024), 0),
    ("w_ff1", (1024, 4096), 1),
    ("w_ff2", (4096, 1024), 0),
    ("w_ple_proj", (256, 1024), 1),
    ("w_ple_gate", (1024, 1024), 0),
)
SMALL = (
    ("g_pre_mix", (1, 1024)),
    ("b_gate", (1, 2048)),
    ("g_q", (1, 384)),
    ("g_kv", (1, 256)),
    ("w_pool", (1, 4, 128, 128)),
    ("pool_scale", (1, 512)),
    ("g_post_mix", (1, 1024)),
    ("g_pre_mlp", (1, 1024)),
    ("g_post_mlp", (1, 1024)),
    ("g_ple", (1, 1024)),
)
WEIGHT_ORDER = ("g_pre_mix", "w_in", "b_gate", "g_q", "w_uq", "g_kv", "w_ukv", "w_pool", "pool_scale", "w_branch_attn",
                "w_branch_pool", "w_out", "g_post_mix", "g_pre_mlp", "w_ff1", "w_ff2", "g_post_mlp", "w_ple_proj",
                "w_ple_gate", "g_ple")
N_CHIPS = 4
PACK_COLS = 1024
REDUCE_ROWS = 160
SMALL_ROWS = 80
FIRST = ("w_in", "w_uq", "w_ukv")
REST = tuple(name for name, _, _ in SHARDED if name not in FIRST)


def _dot(a, b):
    return jnp.dot(a.astype(MX), b.astype(MX), preferred_element_type=F32)


def _dot_nt(a, b):
    return lax.dot_general(a.astype(MX), b.astype(MX), (((1,), (1,)), ((), ())), preferred_element_type=F32)


def _dot_tn(a, b):
    return lax.dot_general(a.astype(MX), b.astype(MX), (((0,), (0,)), ((), ())), preferred_element_type=F32)


def _sig(x):
    return 1.0 / (1.0 + jnp.exp(-x))


def _rms(x, g):
    r = lax.rsqrt(jnp.mean(x * x, axis=1, keepdims=True) + EPS)
    xh = x * r
    return xh * g, xh, r


def _rms_bwd(xh, r, g, dy):
    dxn = dy * g
    dx = r * (dxn - xh * jnp.mean(dxn * xh, axis=1, keepdims=True))
    return dx, jnp.sum(dy * xh, axis=0, keepdims=True)


def _rot_half(v):
    lane = lax.broadcasted_iota(jnp.int32, v.shape, 1)
    return jnp.where(lane < ROPE_LANE + QK_ROPE // 2, pltpu.roll(v, HEAD_SLOT - QK_ROPE // 2, 1), pltpu.roll(v, QK_ROPE // 2, 1))


def _rope(v, cc, sa):
    return v * cc + _rot_half(v) * sa


def _unrope(v, cc, sa):
    return v * cc - _rot_half(v) * sa


def _params(sem):
    return pltpu.CompilerParams(dimension_semantics=sem, vmem_limit_bytes=VMEM_LIMIT)


def _tok_call(name, body, n_tok, tm, tiled, resident, outs, accs=(), scratch=(), exchange=None):
    def as_pair(t):
        if isinstance(t, tuple):
            return t
        return t, pl.BlockSpec((tm, t.shape[1]), lambda i: (i, 0))
    tiled = [as_pair(t) for t in tiled]
    outs = [as_pair(o) for o in outs]
    res_specs = [pl.BlockSpec(r.shape, lambda i, nd=r.ndim: (0,) * nd, pipeline_mode=pl.Buffered(1)) for r in resident]
    out_specs = [s for _, s in outs] + [pl.BlockSpec(a.shape, lambda i: (0, 0)) for a in accs]
    n_t, n_r, n_o, n_a, n_s = len(tiled), len(resident), len(outs), len(accs), len(scratch)
    n_steps = n_tok // tm
    operands = [a for a, _ in tiled] + list(resident)
    in_specs = [s for _, s in tiled] + res_specs
    out_shape = [o for o, _ in outs] + list(accs)
    scratch = list(scratch)
    if exchange is not None:
        ex_in, ex_out, ex_sems, ex_steps = exchange
        operands.append(ex_in)
        in_specs.append(_HBM)
        out_shape.append(ex_out)
        out_specs.append(_HBM)
        scratch += list(ex_sems)

    def kern(*refs):
        refs = list(refs)
        n_in = n_t + n_r + (exchange is not None)
        n_out = n_o + n_a + (exchange is not None)
        tin, res = refs[:n_t], refs[n_t:n_t + n_r]
        tout = refs[n_in:n_in + n_o]
        acc = refs[n_in + n_o:n_in + n_o + n_a]
        scr = refs[n_in + n_out:n_in + n_out + n_s]
        i = pl.program_id(0)
        if exchange is not None:
            ex_steps(i, n_steps, refs[n_in - 1], refs[n_in + n_out - 1], refs[n_in + n_out + n_s:])

        @pl.when(i == 0)
        def _():
            for a in acc:
                a[...] = jnp.zeros(a.shape, a.dtype)
        body(i, tin, res, tout, acc, scr)

    return pl.pallas_call(
        kern, name=name, grid=(n_steps,), in_specs=in_specs, out_specs=out_specs,
        out_shape=out_shape, scratch_shapes=scratch, compiler_params=_params(("arbitrary",)),
    )(*operands)


def _sds(rows, cols, dtype):
    return jax.ShapeDtypeStruct((rows, cols), dtype)


def _fwd_inproj(x, cc, sa, w, tm, gather=None):
    n_tok = x.shape[0]

    def body(i, tin, res, tout, acc, scr):
        x_ref, c_ref, s_ref = tin
        g_pre, w_in, g_q, w_uq, g_kv, w_k, w_v, e_mat, w_kt, e_t, w_vt, v_ones = res
        a_o, ps_o, u_o, gl_o, qn_o, kvn_o, q_o, k_o, v_o, kt_o, vt_o = tout
        a = _rms(x_ref[...], g_pre[...])[0].astype(MX)
        a_o[...] = a
        ps = _dot(a, w_in[:, :SMALL_COLS])
        ps_o[...] = ps.astype(ps_o.dtype)
        u_o[...] = _dot(a, w_in[:, SMALL_COLS:SMALL_COLS + POOL_WIDTH]).astype(u_o.dtype)
        gl_o[...] = _dot(a, w_in[:, SMALL_COLS + POOL_WIDTH:]).astype(gl_o.dtype)
        cc_, sa_ = c_ref[...], s_ref[...]
        qn = _rms(ps[:, :Q_LORA], g_q[...])[0].astype(MX)
        qn_o[...] = qn
        q = _dot(qn, w_uq[...])
        for h in range(N_HEADS):
            hs = slice(h * HEAD_SLOT, (h + 1) * HEAD_SLOT)
            q_o[:, hs] = (_rope(q[:, hs], cc_, sa_) * (SCALE * LOG2E)).astype(q_o.dtype)
        kvn = _rms(ps[:, Q_LORA:Q_LORA + KV_LORA], g_kv[...])[0].astype(MX)
        kvn_o[...] = kvn
        kr = _rope(ps[:, Q_LORA + KV_LORA:], cc_, sa_)
        k_o[...] = (_dot(kvn, w_k[...]) + _dot(kr, e_mat[...])).astype(k_o.dtype)
        v_o[...] = _dot(kvn, w_v[...]).astype(v_o.dtype)
        kt_o[...] = (_dot_nt(w_kt[...], kvn) + _dot_nt(e_t[...], kr)).astype(kt_o.dtype)
        vt_o[...] = (_dot_nt(w_vt[...], kvn) + v_ones[...]).astype(vt_o.dtype)

    outs = [_sds(n_tok, D_MODEL, MX), _sds(n_tok, SMALL_COLS, MX), _sds(n_tok, POOL_WIDTH, MX), _sds(n_tok, 2 * D_MODEL, MX),
            _sds(n_tok, Q_LORA, MX), _sds(n_tok, KV_LORA, MX), _sds(n_tok, QK_WIDTH, MX), _sds(n_tok, QK_WIDTH, MX),
            _sds(n_tok, N_HEADS * V_HEAD, MX),
            (_sds(QK_WIDTH, n_tok, MX), pl.BlockSpec((QK_WIDTH, tm), lambda i: (0, i))),
            (_sds(N_HEADS * V_ROWS, n_tok, MX), pl.BlockSpec((N_HEADS * V_ROWS, tm), lambda i: (0, i)))]
    res = [w["g_pre_mix"], w["w_in"], w["g_q"], w["w_uq"], w["g_kv"], w["w_k"], w["w_v"], w["e_mat"], w["w_kt"], w["e_t"],
           w["w_vt"], w["v_ones"]]
    exchange = None
    if gather is not None:
        gathered = jax.ShapeDtypeStruct((N_CHIPS,) + gather.shape, gather.dtype)
        exchange = (gather, gathered, [pltpu.SemaphoreType.DMA((6,)), pltpu.SemaphoreType.DMA((6,))], _gather_steps)
    return _tok_call("fwd_inproj", body, n_tok, tm, [x, cc, sa], res, outs, exchange=exchange)


def _causal_pairs(nq, ratio, by_kv):
    if by_kv:
        pairs = [(i, j) for j in range(nq * ratio) for i in range(j // ratio, nq)]
    else:
        pairs = [(i, j) for i in range(nq) for j in range((i + 1) * ratio)]
    return (jnp.asarray(np.array([p[0] for p in pairs], np.int32)), jnp.asarray(np.array([p[1] for p in pairs], np.int32)))


def _keep_t(tk, tq, off):
    return lax.broadcasted_iota(jnp.int32, (tk, tq), 0) + off <= lax.broadcasted_iota(jnp.int32, (tk, tq), 1)


ATTN_FWD_TILE = (1024, 1024)
ATTN_BWD_TILE = (1024, 512)
V_ROWS = 80


def _attn_fwd(q, k, vt, tq, tk):
    n_tok = q.shape[0]
    nq, ratio = n_tok // tq, tq // tk
    qi, kj = _causal_pairs(nq, ratio, by_kv=False)

    def kern(qi_ref, kj_ref, q_ref, k_ref, vt_ref, o_ref, lse_ref, m_s, acc_s, st_s):
        s_id = pl.program_id(0)
        i, j = qi_ref[s_id], kj_ref[s_id]

        @pl.when(j == 0)
        def _():
            m_s[...] = jnp.full(m_s.shape, NEG, F32)
            acc_s[...] = jnp.zeros(acc_s.shape, F32)

        def scores(h):
            hs = slice(h * HEAD_SLOT, (h + 1) * HEAD_SLOT)
            return _dot_nt(k_ref[:, hs], q_ref[:, hs])

        def heads(masked):
            keep = _keep_t(tk, tq, j * tk - i * tq) if masked else None
            st_s[0] = scores(0)
            for h in range(N_HEADS):
                if h + 1 < N_HEADS:
                    st_s[(h + 1) % 2] = scores(h + 1)
                st = st_s[h % 2]
                if masked:
                    st = jnp.where(keep, st, NEG)
                m_old = m_s[h]
                m_new = jnp.maximum(m_old, jnp.max(st, axis=0, keepdims=True))
                pt = jnp.exp2(st - m_new)
                acc_s[h] = jnp.exp2(m_old - m_new) * acc_s[h] + _dot(vt_ref[h * V_ROWS:(h + 1) * V_ROWS, :], pt)
                m_s[h] = m_new

        @pl.when(j < i * ratio)
        def _():
            heads(False)

        @pl.when(j >= i * ratio)
        def _():
            heads(True)

        @pl.when(j == (i + 1) * ratio - 1)
        def _():
            heads_out = []
            for h in range(N_HEADS):
                total = acc_s[h, V_HEAD:V_HEAD + 1, :]
                heads_out.append(acc_s[h, :V_HEAD, :] / total)
                lse_ref[h:h + 1, :] = m_s[h] + jnp.log2(total)
            o_ref[...] = jnp.concatenate(heads_out, 0).T.astype(o_ref.dtype)

    gs = pltpu.PrefetchScalarGridSpec(
        num_scalar_prefetch=2, grid=(qi.shape[0],),
        in_specs=[pl.BlockSpec((tq, QK_WIDTH), lambda s, qi, kj: (qi[s], 0)),
                  pl.BlockSpec((tk, QK_WIDTH), lambda s, qi, kj: (kj[s], 0)),
                  pl.BlockSpec((N_HEADS * V_ROWS, tk), lambda s, qi, kj: (0, kj[s]))],
        out_specs=[pl.BlockSpec((tq, N_HEADS * V_HEAD), lambda s, qi, kj: (qi[s], 0)),
                   pl.BlockSpec((N_HEADS, tq), lambda s, qi, kj: (0, qi[s]))],
        scratch_shapes=[pltpu.VMEM((N_HEADS, 1, tq), F32), pltpu.VMEM((N_HEADS, V_ROWS, tq), F32),
                        pltpu.VMEM((2, tk, tq), F32)])
    return pl.pallas_call(kern, name="attn_fwd", grid_spec=gs,
                          out_shape=[_sds(n_tok, N_HEADS * V_HEAD, MX), _sds(N_HEADS, n_tok, F32)],
                          compiler_params=_params(("arbitrary",)))(qi, kj, q, k, vt)


def _pool_windows(ext, i, tm, first_row):
    row = i * tm + lax.broadcasted_iota(jnp.int32, (tm, 1), 0)
    out = []
    for g, w in enumerate(POOL_WINDOWS):
        cs = slice(g * POOL_GROUP, (g + 1) * POOL_GROUP)
        s = ext[pl.ds(first_row, tm), cs]
        for k in range(1, w):
            s = s + ext[pl.ds(first_row - k, tm), cs]
        cnt = jnp.minimum(row + 1, w).astype(F32)
        out.append(s / cnt)
    return out


def _fwd_mix(x, u, gl, attn, w, tm):
    n_tok = x.shape[0]
    halo_spec = pl.BlockSpec((HALO, POOL_WIDTH), lambda i: (jnp.maximum(i * (tm // HALO) - 1, 0), 0))

    def body(i, tin, res, tout, acc, scr):
        x_ref, u_ref, uh_ref, gl_ref, at_ref = tin
        w_pool, pool_scale, w_ba, w_bp, b_gate, w_out, g_post = res
        d_o, pooled_o, a_o, pp_o, merged_o, y_o, h1_o = tout
        ext, = scr
        ext[pl.ds(0, HALO), :] = jnp.where(i > 0, uh_ref[...].astype(F32), 0.0)
        ext[pl.ds(HALO, tm), :] = u_ref[...].astype(F32)
        means = _pool_windows(ext, i, tm, HALO)
        for g in range(len(POOL_WINDOWS)):
            cs = slice(g * POOL_GROUP, (g + 1) * POOL_GROUP)
            d = (means[g] - ext[pl.ds(HALO, tm), cs]).astype(MX)
            d_o[:, cs] = d
            pooled_o[:, cs] = (_dot(d, w_pool[g]) * pool_scale[:, cs]).astype(pooled_o.dtype)
        a_br = _dot(at_ref[...], w_ba[...])
        p_br = _dot(pooled_o[...], w_bp[...])
        a_o[...] = a_br.astype(a_o.dtype)
        pp_o[...] = p_br.astype(pp_o.dtype)
        gates = _sig(gl_ref[...].astype(F32) + b_gate[...])
        merged = (gates[:, :D_MODEL] * a_br + gates[:, D_MODEL:] * p_br).astype(MX)
        merged_o[...] = merged
        y = _dot(merged, w_out[...])
        y_o[...] = y.astype(y_o.dtype)
        h1_o[...] = x_ref[...] + _rms(y, g_post[...])[0]

    outs = [_sds(n_tok, POOL_WIDTH, MX), _sds(n_tok, POOL_WIDTH, MX), _sds(n_tok, D_MODEL, MX), _sds(n_tok, D_MODEL, MX),
            _sds(n_tok, D_MODEL, MX), _sds(n_tok, D_MODEL, MX), _sds(n_tok, D_MODEL, F32)]
    res = [w["w_pool"], w["pool_scale"], w["w_branch_attn"], w["w_branch_pool"], w["b_gate"], w["w_out"], w["g_post_mix"]]
    return _tok_call("fwd_mix", body, n_tok, tm, [x, u, (u, halo_spec), gl, attn], res, outs,
                     scratch=[pltpu.VMEM((tm + HALO, POOL_WIDTH), F32)])


def _fwd_mlp(h1, w, tm):
    n_tok = h1.shape[0]

    def body(i, tin, res, tout, acc, scr):
        h1_ref, = tin
        g_pre, w1, w2, g_post = res
        m_o, zr_o, f_o, h2_o = tout
        h1_ = h1_ref[...]
        m = _rms(h1_, g_pre[...])[0].astype(MX)
        m_o[...] = m
        zr = jnp.maximum(_dot(m, w1[...]), 0.0)
        zr_o[...] = zr.astype(zr_o.dtype)
        a2 = (zr * zr).astype(MX)
        f = _dot(a2, w2[...])
        f_o[...] = f.astype(f_o.dtype)
        h2_o[...] = h1_ + _rms(f, g_post[...])[0]

    outs = [_sds(n_tok, D_MODEL, MX), _sds(n_tok, D_FF, MX), _sds(n_tok, D_MODEL, MX),
            _sds(n_tok, D_MODEL, F32)]
    res = [w["g_pre_mlp"], w["w_ff1"], w["w_ff2"], w["g_post_mlp"]]
    return _tok_call("fwd_mlp", body, n_tok, tm, [h1], res, outs)


def _ple_fwd_bwd(h2, p, target, w, tm):
    n_tok = h2.shape[0]

    def body(i, tin, res, tout, acc, scr):
        h2_ref, p_ref, t_ref = tin
        w_pe, w_pg, g_ple = res
        dh2_o, de_o, dzg_o = tout
        loss_a, dg_a = acc
        h2_ = h2_ref[...]
        e = _dot(p_ref[...], w_pe[...])
        pg = _sig(_dot(h2_, w_pg[...]))
        t = pg * e
        g = g_ple[...]
        tn, th, r = _rms(t, g)
        diff = h2_ + tn - t_ref[...]
        loss_a[...] += jnp.sum(diff * diff, axis=0, keepdims=True)
        dh3 = diff * (1.0 / D_MODEL)
        dt, dg = _rms_bwd(th, r, g, dh3)
        dg_a[...] += dg
        de_o[...] = (dt * pg).astype(de_o.dtype)
        dzg = (dt * e * pg * (1.0 - pg)).astype(MX)
        dzg_o[...] = dzg
        dh2_o[...] = dh3 + _dot_nt(dzg, w_pg[...])

    outs = [_sds(n_tok, D_MODEL, F32), _sds(n_tok, D_MODEL, MX), _sds(n_tok, D_MODEL, MX)]
    accs = [_sds(1, D_MODEL, F32), _sds(1, D_MODEL, F32)]
    return _tok_call("ple_fwd_bwd", body, n_tok, tm, [h2, p, target], [w["w_ple_proj"], w["w_ple_gate"], w["g_ple"]], outs, accs)


def _bwd_mlp(dh2, f, h1, zr, w, tm):
    n_tok = dh2.shape[0]

    def body(i, tin, res, tout, acc, scr):
        dh2_ref, f_ref, h1_ref, zr_ref = tin
        g_pre, w1, w2, g_post = res
        df_o, dz_o, dh1_o = tout
        dg_post_a, dg_pre_a = acc
        dh2_ = dh2_ref[...]
        gp = g_post[...]
        _, fh, rf = _rms(f_ref[...].astype(F32), gp)
        df, dg = _rms_bwd(fh, rf, gp, dh2_)
        dg_post_a[...] += dg
        df = df.astype(MX)
        df_o[...] = df
        dz = (_dot_nt(df, w2[...]) * (2.0 * zr_ref[...].astype(F32))).astype(MX)
        dz_o[...] = dz
        dm = _dot_nt(dz, w1[...])
        gq = g_pre[...]
        _, hh, rh = _rms(h1_ref[...], gq)
        dh1, dg = _rms_bwd(hh, rh, gq, dm)
        dg_pre_a[...] += dg
        dh1_o[...] = dh2_ + dh1

    outs = [_sds(n_tok, D_MODEL, MX), _sds(n_tok, D_FF, MX), _sds(n_tok, D_MODEL, F32)]
    accs = [_sds(1, D_MODEL, F32), _sds(1, D_MODEL, F32)]
    res = [w["g_pre_mlp"], w["w_ff1"], w["w_ff2"], w["g_post_mlp"]]
    return _tok_call("bwd_mlp", body, n_tok, tm, [dh2, f, h1, zr], res, outs, accs)


def _bwd_mix(dh1, y, a_br, p_br, gl, attn, d, w, tm):
    n_tok = dh1.shape[0]

    def body(i, tin, res, tout, acc, scr):
        dh1_ref, y_ref, a_ref, pp_ref, gl_ref, at_ref, d_ref = tin
        g_post, w_out, b_gate, w_ba, w_bp, w_pool, pool_scale, sel = res
        dy_o, da_o, dpp_o, dgl_o, do_o, delta_o, dyp_o, dd_o = tout
        dg_post_a, db_a, dps_a = acc
        g = g_post[...]
        _, yh, r = _rms(y_ref[...].astype(F32), g)
        dy, dg = _rms_bwd(yh, r, g, dh1_ref[...])
        dg_post_a[...] += dg
        dy = dy.astype(MX)
        dy_o[...] = dy
        dmerged = _dot_nt(dy, w_out[...])
        gates = _sig(gl_ref[...].astype(F32) + b_gate[...])
        ga, gp = gates[:, :D_MODEL], gates[:, D_MODEL:]
        da = (dmerged * ga).astype(MX)
        dpp = (dmerged * gp).astype(MX)
        da_o[...] = da
        dpp_o[...] = dpp
        dgl_a = dmerged * a_ref[...].astype(F32) * ga * (1.0 - ga)
        dgl_p = dmerged * pp_ref[...].astype(F32) * gp * (1.0 - gp)
        dgl_o[:, :D_MODEL] = dgl_a.astype(dgl_o.dtype)
        dgl_o[:, D_MODEL:] = dgl_p.astype(dgl_o.dtype)
        db_a[:, :D_MODEL] += jnp.sum(dgl_a, axis=0, keepdims=True)
        db_a[:, D_MODEL:] += jnp.sum(dgl_p, axis=0, keepdims=True)
        do = _dot_nt(da, w_ba[...]).astype(MX)
        do_o[...] = do
        prod = do.astype(F32) * at_ref[...].astype(F32)
        hi = prod.astype(MX)
        lo = (prod - hi.astype(F32)).astype(MX)
        delta_o[...] = _dot(hi, sel[...]) + _dot(lo, sel[...])
        dpooled = _dot_nt(dpp, w_bp[...])
        for gi in range(len(POOL_WINDOWS)):
            cs = slice(gi * POOL_GROUP, (gi + 1) * POOL_GROUP)
            ypre = _dot(d_ref[:, cs], w_pool[gi])
            dps_a[:, cs] += jnp.sum(dpooled[:, cs] * ypre, axis=0, keepdims=True)
            dyp = (dpooled[:, cs] * pool_scale[:, cs]).astype(MX)
            dyp_o[:, cs] = dyp
            dd_o[:, cs] = _dot_nt(dyp, w_pool[gi])

    outs = [_sds(n_tok, D_MODEL, MX), _sds(n_tok, D_MODEL, MX), _sds(n_tok, D_MODEL, MX), _sds(n_tok, 2 * D_MODEL, MX),
            _sds(n_tok, N_HEADS * V_HEAD, MX), _sds(n_tok, HEAD_SLOT, F32), _sds(n_tok, POOL_WIDTH, MX),
            _sds(n_tok, POOL_WIDTH, F32)]
    accs = [_sds(1, D_MODEL, F32), _sds(1, 2 * D_MODEL, F32), _sds(1, POOL_WIDTH, F32)]
    res = [w["g_post_mix"], w["w_out"], w["b_gate"], w["w_branch_attn"], w["w_branch_pool"], w["w_pool"], w["pool_scale"],
           w["head_sel"]]
    return _tok_call("bwd_mix", body, n_tok, tm, [dh1, y, a_br, p_br, gl, attn, d], res, outs, accs)


def _bwd_heads(q_ref, k_ref, v_ref, do_ref, lse_ref, dl_ref, st_s, dpt_s, keep, use, n_heads):
    def products(h):
        hs = slice(h * HEAD_SLOT, (h + 1) * HEAD_SLOT)
        vs = slice(h * V_HEAD, (h + 1) * V_HEAD)
        st_s[h % 2] = _dot_nt(k_ref[:, hs], q_ref[:, hs])
        dpt_s[h % 2] = _dot_nt(v_ref[:, vs], do_ref[:, vs])

    products(0)
    for h in range(n_heads):
        if h + 1 < n_heads:
            products(h + 1)
        st = st_s[h % 2]
        if keep is not None:
            st = jnp.where(keep, st, NEG)
        pt = jnp.exp2(st - lse_ref[h:h + 1, :])
        use(h, pt, pt * (dpt_s[h % 2] - dl_ref[h:h + 1, :]))


HEAD_GROUP = 4


def _attn_bwd(q, k, kt, v, do, lse, delta, tq, tk, scatter=None):
    n_tok = q.shape[0]
    nq, ratio = n_tok // tq, tq // tk
    n_groups = N_HEADS // HEAD_GROUP
    gq, gv = HEAD_GROUP * HEAD_SLOT, HEAD_GROUP * V_HEAD
    qi, kj = _causal_pairs(nq, ratio, by_kv=True)

    n_pairs = qi.shape[0]

    def kern(qi_ref, kj_ref, q_ref, k_ref, kt_ref, v_ref, do_ref, lse_ref, dl_ref, *rest):
        if scatter is not None:
            s_hbm, dq_ref, dk_ref, dv_ref, r_hbm, dk_s, dv_s, st_s, dpt_s, send_sems, recv_sems = rest
        else:
            dq_ref, dk_ref, dv_ref, dk_s, dv_s, st_s, dpt_s = rest
        s_id = pl.program_id(1)
        i, j = qi_ref[s_id], kj_ref[s_id]
        cols = pl.ds(pl.multiple_of(i * tq, tq), tq)
        if scatter is not None:
            group = pl.program_id(0)
            _scatter_steps(jnp.logical_and(group == 0, s_id == 0),
                           jnp.logical_and(group == n_groups - 1, s_id == n_pairs - 1), s_hbm, r_hbm, (send_sems, recv_sems))

        @pl.when(s_id == 0)
        def _():
            dq_ref[...] = jnp.zeros(dq_ref.shape, F32)

        def use(h, pt, dst):
            hs = slice(h * HEAD_SLOT, (h + 1) * HEAD_SLOT)
            dv_s[h] += _dot(pt, do_ref[:, h * V_HEAD:(h + 1) * V_HEAD])
            dk_s[:, hs] += _dot(dst, q_ref[:, hs])
            dq_ref[hs, cols] += _dot(kt_ref[hs, :], dst)

        def heads(masked):
            keep = _keep_t(tk, tq, j * tk - i * tq) if masked else None
            _bwd_heads(q_ref, k_ref, v_ref, do_ref, lse_ref.at[0], dl_ref.at[0], st_s, dpt_s, keep, use, HEAD_GROUP)

        @pl.when(j >= i * ratio)
        def _():
            dk_s[...] = jnp.zeros(dk_s.shape, F32)
            dv_s[...] = jnp.zeros(dv_s.shape, F32)
            heads(True)

        @pl.when(j < i * ratio)
        def _():
            heads(False)

        @pl.when(i == nq - 1)
        def _():
            dk_ref[...] = (dk_s[...] * (1.0 / LOG2E)).astype(dk_ref.dtype)
            for h in range(HEAD_GROUP):
                dv_ref[:, h * V_HEAD:(h + 1) * V_HEAD] = dv_s[h].astype(dv_ref.dtype)

    at_q = lambda g, s, qi, kj: (qi[s], g)
    at_k = lambda g, s, qi, kj: (kj[s], g)
    at_kt = lambda g, s, qi, kj: (g, kj[s])
    at_stat = lambda g, s, qi, kj: (g, 0, qi[s])
    in_specs = [pl.BlockSpec((tq, gq), at_q), pl.BlockSpec((tk, gq), at_k), pl.BlockSpec((gq, tk), at_kt),
                pl.BlockSpec((tk, gv), at_k), pl.BlockSpec((tq, gv), at_q),
                pl.BlockSpec((1, HEAD_GROUP, tq), at_stat), pl.BlockSpec((1, HEAD_GROUP, tq), at_stat)]
    out_specs = [pl.BlockSpec((gq, n_tok), lambda g, s, qi, kj: (g, 0), pipeline_mode=pl.Buffered(1)),
                 pl.BlockSpec((tk, gq), at_k), pl.BlockSpec((tk, gv), at_k)]
    out_shape = [_sds(QK_WIDTH, n_tok, F32), _sds(n_tok, QK_WIDTH, MX), _sds(n_tok, N_HEADS * V_HEAD, MX)]
    scratch = [pltpu.VMEM((tk, gq), F32), pltpu.VMEM((HEAD_GROUP, tk, V_HEAD), F32),
               pltpu.VMEM((2, tk, tq), F32), pltpu.VMEM((2, tk, tq), F32)]
    stat3 = lambda a: a.reshape(n_groups, HEAD_GROUP, n_tok)
    operands = [qi, kj, q, k, kt, v, do, stat3(lse), stat3(delta)]
    if scatter is not None:
        operands.append(scatter)
        in_specs.append(_HBM)
        out_specs.append(_HBM)
        out_shape.append(jax.ShapeDtypeStruct((N_DEVICES, scatter.shape[1] // 2, PACK_COLS), scatter.dtype))
        scratch += [pltpu.SemaphoreType.DMA((N_DEVICES - 1,)), pltpu.SemaphoreType.DMA((N_DEVICES - 1,))]
    gs = pltpu.PrefetchScalarGridSpec(num_scalar_prefetch=2, grid=(n_groups, n_pairs), in_specs=in_specs,
                                      out_specs=out_specs, scratch_shapes=scratch)
    return pl.pallas_call(kern, name="attn_bwd", grid_spec=gs, out_shape=out_shape,
                          compiler_params=_params(("arbitrary", "arbitrary")))(*operands)


def _bwd_inproj(dq_t, dk, dv, dd, dgl, ps, x, dh1, cc, sa, w, tm):
    n_tok = x.shape[0]
    n_tiles = n_tok // tm
    last_halo = n_tok // HALO - 1
    halo_spec = pl.BlockSpec((HALO, POOL_WIDTH), lambda i: (jnp.minimum((i + 1) * (tm // HALO), last_halo), 0))

    def body(i, tin, res, tout, acc, scr):
        dq_ref, dk_ref, dv_ref, dd_ref, ddh_ref, dgl_ref, ps_ref, x_ref, dh1_ref, c_ref, s_ref = tin
        w_uq, g_q, w_k, w_v, e_mat, g_kv, w_in, g_pre = res
        dqu_o, dproj_o, dx_o = tout
        dgq_a, dgkv_a, dgpre_a = acc
        ext, = scr
        cc_, sa_ = c_ref[...], s_ref[...]
        for h in range(N_HEADS):
            hs = slice(h * HEAD_SLOT, (h + 1) * HEAD_SLOT)
            dqu_o[:, hs] = (_unrope(dq_ref[hs, :].T, cc_, sa_) * SCALE).astype(dqu_o.dtype)
        gq = g_q[...]
        _, qh, rq = _rms(ps_ref[:, :Q_LORA].astype(F32), gq)
        dqd, dg = _rms_bwd(qh, rq, gq, _dot_nt(dqu_o[...], w_uq[...]))
        dgq_a[...] += dg
        dproj_o[:, :Q_LORA] = dqd.astype(dproj_o.dtype)
        gkv = g_kv[...]
        _, kh, rk = _rms(ps_ref[:, Q_LORA:Q_LORA + KV_LORA].astype(F32), gkv)
        dkvd, dg = _rms_bwd(kh, rk, gkv, _dot_nt(dk_ref[...], w_k[...]) + _dot_nt(dv_ref[...], w_v[...]))
        dgkv_a[...] += dg
        dproj_o[:, Q_LORA:Q_LORA + KV_LORA] = dkvd.astype(dproj_o.dtype)
        dproj_o[:, Q_LORA + KV_LORA:SMALL_COLS] = _unrope(_dot_nt(dk_ref[...], e_mat[...]), cc_, sa_).astype(dproj_o.dtype)
        row = i * tm + lax.broadcasted_iota(jnp.int32, (tm + HALO, 1), 0)
        for gi, wdw in enumerate(POOL_WINDOWS):
            cs = slice(gi * POOL_GROUP, (gi + 1) * POOL_GROUP)
            inv = 1.0 / jnp.minimum(row + 1, wdw).astype(F32)
            ext[pl.ds(0, tm), cs] = dd_ref[:, cs] * inv[:tm]
            ext[pl.ds(tm, HALO), cs] = jnp.where(i < n_tiles - 1, ddh_ref[:, cs] * inv[tm:], 0.0)
            s = ext[pl.ds(0, tm), cs]
            for k_ in range(1, wdw):
                s = s + ext[pl.ds(k_, tm), cs]
            dproj_o[:, SMALL_COLS + gi * POOL_GROUP:SMALL_COLS + (gi + 1) * POOL_GROUP] = (s - dd_ref[:, cs]).astype(dproj_o.dtype)
        dproj_o[:, SMALL_COLS + POOL_WIDTH:] = dgl_ref[...]
        da = _dot_nt(dproj_o[...], w_in[...])
        gp = g_pre[...]
        _, xh, rx = _rms(x_ref[...], gp)
        dx, dg = _rms_bwd(xh, rx, gp, da)
        dgpre_a[...] += dg
        dx_o[...] = dh1_ref[...] + dx

    outs = [_sds(n_tok, QK_WIDTH, MX), _sds(n_tok, IN_PAD, MX), _sds(n_tok, D_MODEL, F32)]
    accs = [_sds(1, Q_LORA, F32), _sds(1, KV_LORA, F32), _sds(1, D_MODEL, F32)]
    res = [w["w_uq"], w["g_q"], w["w_k"], w["w_v"], w["e_mat"], w["g_kv"], w["w_in"], w["g_pre_mix"]]
    dq_spec = pl.BlockSpec((QK_WIDTH, tm), lambda i: (0, i))
    return _tok_call("bwd_inproj", body, n_tok, tm, [(dq_t, dq_spec), dk, dv, dd, (dd, halo_spec), dgl, ps, x, dh1, cc, sa], res, outs, accs,
                     scratch=[pltpu.VMEM((tm + HALO, POOL_WIDTH), F32)])


XTDY_TOKENS = 1024
XTDY_OUT_BYTES = 8 * 2**20


def _xtdy(name, x, dy, allreduce=None, square_x=False):
    n_tok, kk = x.shape
    nn = dy.shape[1]
    bt = min(XTDY_TOKENS, n_tok)
    bk = kk
    while bk * nn * 4 > XTDY_OUT_BYTES and bk % 256 == 0:
        bk //= 2

    grid = (kk // bk, n_tok // bt)

    def kern(x_ref, dy_ref, *rest):
        o_ref = rest[1] if allreduce is not None else rest[0]
        if allreduce is not None:
            g_ref, _, sum_ref, buf, send_sems, recv_sems = rest
            step = pl.program_id(0) * grid[1] + pl.program_id(1)
            _allreduce_steps(step == 0, step == grid[0] * grid[1] - 1, g_ref, sum_ref, buf, send_sems, recv_sems)

        @pl.when(pl.program_id(1) == 0)
        def _():
            o_ref[...] = jnp.zeros(o_ref.shape, F32)
        xv = x_ref[...]
        if square_x:
            xv = xv.astype(F32)
            xv = xv * xv
        o_ref[...] += _dot_tn(xv, dy_ref[...])

    operands = [x, dy]
    in_specs = [pl.BlockSpec((bt, bk), lambda a, t: (t, a)), pl.BlockSpec((bt, nn), lambda a, t: (t, 0))]
    out_specs = [pl.BlockSpec((bk, nn), lambda a, t: (a, 0))]
    out_shape = [_sds(kk, nn, F32)]
    scratch = []
    if allreduce is not None:
        vmem = pl.BlockSpec(memory_space=pltpu.VMEM)
        operands.append(allreduce)
        in_specs.append(vmem)
        out_specs.append(vmem)
        out_shape.append(jax.ShapeDtypeStruct(allreduce.shape, allreduce.dtype))
        scratch = [pltpu.VMEM((N_DEVICES,) + allreduce.shape, allreduce.dtype), pltpu.SemaphoreType.DMA((N_DEVICES - 1,)),
                   pltpu.SemaphoreType.DMA((N_DEVICES - 1,))]
    res = pl.pallas_call(kern, name=name, grid=grid, in_specs=in_specs, out_specs=out_specs, out_shape=out_shape,
                         scratch_shapes=scratch, compiler_params=_params(("arbitrary", "arbitrary")))(*operands)
    return res if allreduce is not None else res[0]


def _rope_tables(positions):
    inv_freq = ROPE_THETA ** (-jnp.arange(0, QK_ROPE, 2, dtype=F32) / QK_ROPE)
    ang_t = inv_freq[:, None] * positions.astype(F32)[None, :]
    cos_t, sin_t = lax.optimization_barrier((jnp.cos(ang_t), jnp.sin(ang_t)))
    cos, sin = cos_t.T, sin_t.T
    n_tok = positions.shape[0]
    ones, z64 = jnp.ones((n_tok, ROPE_LANE), F32), jnp.zeros((n_tok, ROPE_LANE), F32)
    z32 = jnp.zeros((n_tok, HEAD_SLOT - ROPE_LANE - QK_ROPE), F32)
    return jnp.concatenate([ones, cos, cos, z32], 1), jnp.concatenate([z64, -sin, sin, z32], 1)


def _kernel_weights(full):
    w_in, w_uq, w_ukv = full["w_in"], full["w_uq"], full["w_ukv"]
    c0 = Q_LORA + KV_LORA
    z = lambda n: jnp.zeros((D_MODEL, n), w_in.dtype)
    w = dict(full)
    w["w_in"] = jnp.concatenate([w_in[:, :c0], z(ROPE_LANE), w_in[:, c0:c0 + QK_ROPE], z(HEAD_SLOT - ROPE_LANE - QK_ROPE),
                                 w_in[:, c0 + QK_ROPE:]], 1)
    w["w_uq"] = jnp.pad(w_uq.reshape(Q_LORA, N_HEADS, QK_NOPE + QK_ROPE),
                        ((0, 0), (0, 0), (0, HEAD_SLOT - QK_NOPE - QK_ROPE))).reshape(Q_LORA, QK_WIDTH)
    kv = w_ukv.reshape(KV_LORA, N_HEADS, QK_NOPE + V_HEAD)
    w["w_k"] = jnp.pad(kv[:, :, :QK_NOPE], ((0, 0), (0, 0), (0, HEAD_SLOT - QK_NOPE))).reshape(KV_LORA, QK_WIDTH)
    w["w_v"] = kv[:, :, QK_NOPE:].reshape(KV_LORA, N_HEADS * V_HEAD)
    e = np.zeros((HEAD_SLOT, QK_WIDTH), np.float32)
    sel = np.zeros((N_HEADS * V_HEAD, HEAD_SLOT), np.float32)
    for h in range(N_HEADS):
        for r in range(QK_ROPE):
            e[ROPE_LANE + r, h * HEAD_SLOT + ROPE_LANE + r] = 1.0
        sel[h * V_HEAD:(h + 1) * V_HEAD, h] = 1.0
    w["e_mat"] = jnp.asarray(e, MX)
    w["w_kt"], w["e_t"] = w["w_k"].T, jnp.asarray(e.T, MX)
    pad = ((0, 0), (0, V_ROWS - V_HEAD), (0, 0))
    w["w_vt"] = jnp.pad(w["w_v"].T.reshape(N_HEADS, V_HEAD, KV_LORA), pad).reshape(N_HEADS * V_ROWS, KV_LORA)
    ones = np.zeros((N_HEADS, V_ROWS, 1), np.float32)
    ones[:, V_HEAD] = 1.0
    w["v_ones"] = jnp.asarray(ones.reshape(N_HEADS * V_ROWS, 1))
    w["head_sel"] = jnp.asarray(sel, MX)
    w["w_pool"] = full["w_pool"].astype(MX)
    return w


def _local_step(x, p, positions, target, full, mesh_place=None, packed_rest=None):
    n_tok = x.shape[0]
    tm = tm_mlp = min(TOKEN_TILE, n_tok)
    fwd_tile = [min(t, n_tok) for t in ATTN_FWD_TILE]
    bwd_tile = [min(t, n_tok) for t in ATTN_BWD_TILE]
    w = _kernel_weights(full)
    cc, sa = _rope_tables(positions)

    if mesh_place is None:
        a, ps, u, gl, qn, kvn, q, k, v, kt, vt = _fwd_inproj(x, cc, sa, w, tm)
    else:
        my_chip, core = mesh_place
        a, ps, u, gl, qn, kvn, q, k, v, kt, vt, gathered = _fwd_inproj(x, cc, sa, w, tm, gather=packed_rest)
        w.update(_unpack_full(lax.dynamic_update_slice(gathered, packed_rest[None], (my_chip, 0, 0)), REST))
    attn, lse = _attn_fwd(q, k, vt, *fwd_tile)
    d, pooled, a_br, p_br, merged, y, h1 = _fwd_mix(x, u, gl, attn, w, tm)
    m, zr, f, h2 = _fwd_mlp(h1, w, tm_mlp)
    dh2, de, dzg, loss_cols, dg_ple = _ple_fwd_bwd(h2, p, target, w, tm)
    df, dz, dh1, dg_post_mlp, dg_pre_mlp = _bwd_mlp(dh2, f, h1, zr, w, tm_mlp)
    dy, da_br, dp_br, dgl, do, delta, dyp, dd, dg_post_mix, db_gate, dpool_scale = _bwd_mix(dh1, y, a_br, p_br, gl, attn, d, w, tm)
    grads = {"w_branch_attn": _xtdy("dw_ba", attn, da_br), "w_branch_pool": _xtdy("dw_bp", pooled, dp_br),
             "w_out": _xtdy("dw_out", merged, dy), "w_ff1": _xtdy("dw_ff1", m, dz), "w_ff2": _xtdy("dw_ff2", zr, df, square_x=True),
             "w_ple_proj": _xtdy("dw_pe", p, de), "w_ple_gate": _xtdy("dw_pg", h2, dzg)}
    delta_t = delta[:, :N_HEADS].T
    if mesh_place is None:
        travelling = None
        dq_t, dk, dv = _attn_bwd(q, k, kt, v, do, lse, delta_t, *bwd_tile)
    else:
        pieces = _pack_pieces(grads, REST, WIRE)
        dq_t, dk, dv, received = _attn_bwd(q, k, kt, v, do, lse, delta_t, *bwd_tile, scatter=pieces)
        travelling = (pieces, received)
        grads = {}
    dqu, dproj, dx, dg_q, dg_kv, dg_pre_mix = _bwd_inproj(dq_t, dk, dv, dd, dgl, ps, x, dh1, cc, sa, w, tm)

    g_uq = _xtdy("dw_uq", qn, dqu)
    g_k = _xtdy("dw_k", kvn, dk)
    g_v = _xtdy("dw_v", kvn, dv)
    g_pool = _xtdy("dw_pool", d, dyp)
    small = {"g_pre_mix": dg_pre_mix, "b_gate": db_gate, "g_q": dg_q, "g_kv": dg_kv, "pool_scale": dpool_scale,
             "g_post_mix": dg_post_mix, "g_pre_mlp": dg_pre_mlp, "g_post_mlp": dg_post_mlp, "g_ple": dg_ple,
             "w_pool": jnp.stack([g_pool[g * POOL_GROUP:(g + 1) * POOL_GROUP, g * POOL_GROUP:(g + 1) * POOL_GROUP]
                                  for g in range(len(POOL_WINDOWS))])}
    if mesh_place is None:
        g_in = _xtdy("dw_in", a, dproj)
    else:
        g_in, small_sum = _xtdy("dw_in", a, dproj, allreduce=_pack_small(small))
        small = _unpack_small(small_sum)

    c0 = Q_LORA + KV_LORA
    grads.update(small)
    grads.update({
        "w_in": jnp.concatenate([g_in[:, :c0], g_in[:, c0 + ROPE_LANE:c0 + ROPE_LANE + QK_ROPE], g_in[:, SMALL_COLS:]], 1),
        "w_uq": g_uq.reshape(Q_LORA, N_HEADS, HEAD_SLOT)[:, :, :QK_NOPE + QK_ROPE].reshape(Q_LORA, N_HEADS * (QK_NOPE + QK_ROPE)),
        "w_ukv": jnp.concatenate([g_k.reshape(KV_LORA, N_HEADS, HEAD_SLOT)[:, :, :QK_NOPE],
                                  g_v.reshape(KV_LORA, N_HEADS, V_HEAD)], 2).reshape(KV_LORA, N_HEADS * (QK_NOPE + V_HEAD)),
    })
    return loss_cols, dx, grads, travelling


def _place():
    return lax.axis_index("x"), lax.axis_index("y"), lax.axis_index("c")


CHIP_FLIPS = ((1, 0), (0, 1), (1, 1))


def _flip(x, y, fx, fy):
    return (1 - x if fx else x), (1 - y if fy else y)


_HBM = pl.BlockSpec(memory_space=pl.ANY)


def _gather_copies(w_ref, out_ref, send_sems, recv_sems):
    half = w_ref.shape[0] // 2
    x, y, c = _place()
    my_chip = 2 * x + y
    sibling = (x, y, 1 - c)

    def half_of(chip, hc):
        return out_ref.at[chip, pl.ds(pl.multiple_of(hc * half, 16), half), :]

    src = w_ref.at[pl.ds(pl.multiple_of(c * half, 16), half), :]
    sends, landed, forwards, from_sibling = [], [], [], []
    for j, (fx, fy) in enumerate(CHIP_FLIPS):
        px, py = _flip(x, y, fx, fy)
        mine_there, theirs_here, theirs_other = half_of(my_chip, c), half_of(2 * px + py, c), half_of(2 * px + py, 1 - c)
        sends.append(pltpu.make_async_remote_copy(src, mine_there, send_sems.at[j], recv_sems.at[j],
                                                  device_id=(px, py, c), device_id_type=MESH))
        landed.append(pltpu.make_async_remote_copy(src, theirs_here, send_sems.at[j], recv_sems.at[j],
                                                   device_id=(px, py, c), device_id_type=MESH))
        forwards.append(pltpu.make_async_remote_copy(theirs_here, theirs_here, send_sems.at[3 + j], recv_sems.at[3 + j],
                                                     device_id=sibling, device_id_type=MESH))
        from_sibling.append(pltpu.make_async_remote_copy(theirs_other, theirs_other, send_sems.at[3 + j],
                                                         recv_sems.at[3 + j], device_id=sibling, device_id_type=MESH))
    return sends, landed, forwards, from_sibling


def _gather_steps(i, n_steps, w_ref, out_ref, sems):
    sends, landed, forwards, from_sibling = _gather_copies(w_ref, out_ref, *sems)

    @pl.when(i == 0)
    def _():
        for cp in sends:
            cp.start()

    @pl.when(i == (3 * n_steps) // 4)
    def _():
        for arrived, fwd in zip(landed, forwards):
            arrived.wait_recv()
            fwd.start()

    @pl.when(i == n_steps - 1)
    def _():
        for cp in from_sibling:
            cp.wait_recv()
        for cp in sends + forwards:
            cp.wait_send()


def _allgather_shards(wp):
    def body(w_ref, out_ref, send_sems, recv_sems):
        sends, landed, forwards, from_sibling = _gather_copies(w_ref, out_ref, send_sems, recv_sems)
        for cp in sends:
            cp.start()
        for arrived, fwd in zip(landed, forwards):
            arrived.wait_recv()
            fwd.start()
        for cp in from_sibling:
            cp.wait_recv()
        for cp in sends + forwards:
            cp.wait_send()

    return pl.pallas_call(
        body, name="allgather_shards", out_shape=jax.ShapeDtypeStruct((N_CHIPS,) + wp.shape, wp.dtype),
        in_specs=[_HBM], out_specs=_HBM,
        scratch_shapes=[pltpu.SemaphoreType.DMA((6,)), pltpu.SemaphoreType.DMA((6,))],
    )(wp)


def _exchange_halves(g):
    rows = g.shape[1]
    half = rows // 2

    def body(g_ref, r_ref, send_sem, recv_sem):
        x, y, c = _place()
        src = g_ref.at[:, pl.ds(pl.multiple_of((1 - c) * half, 8), half), :]
        cp = pltpu.make_async_remote_copy(src, r_ref, send_sem, recv_sem, device_id=(x, y, 1 - c), device_id_type=MESH)
        cp.start()
        cp.wait()

    return pl.pallas_call(
        body, name="exchange_halves", out_shape=jax.ShapeDtypeStruct((N_CHIPS, half, PACK_COLS), g.dtype),
        in_specs=[_HBM], out_specs=_HBM, scratch_shapes=[pltpu.SemaphoreType.DMA, pltpu.SemaphoreType.DMA],
    )(g)


def _add_halves(g, r, c):
    rows = g.shape[1]
    half = rows // 2
    br = REDUCE_ROWS
    nb = half // br

    def kern(c_ref, g_ref, r_ref, o_ref):
        o_ref[...] = (g_ref[...] + r_ref[...]).astype(o_ref.dtype)

    gs = pltpu.PrefetchScalarGridSpec(
        num_scalar_prefetch=1, grid=(N_CHIPS, nb),
        in_specs=[pl.BlockSpec((1, br, PACK_COLS), lambda k, t, c: (k, c[0] * nb + t, 0)),
                  pl.BlockSpec((1, br, PACK_COLS), lambda k, t, c: (k, t, 0))],
        out_specs=pl.BlockSpec((1, br, PACK_COLS), lambda k, t, c: (k, t, 0)))
    return pl.pallas_call(kern, name="add_halves", grid_spec=gs,
                          out_shape=jax.ShapeDtypeStruct((N_CHIPS, half, PACK_COLS), WIRE),
                          compiler_params=_params(("arbitrary", "arbitrary")))(c.reshape(1), g, r)


def _scatter_copies(s_ref, r_ref, send_sems, recv_sems):
    x, y, c = _place()
    my_chip = 2 * x + y
    sends, arrivals = [], []
    for j, (fx, fy) in enumerate(CHIP_FLIPS):
        px, py = _flip(x, y, fx, fy)
        slot = r_ref.at[2 * px + py]
        sends.append(pltpu.make_async_remote_copy(s_ref.at[2 * px + py], r_ref.at[my_chip], send_sems.at[j], recv_sems.at[j],
                                                  device_id=(px, py, c), device_id_type=MESH))
        arrivals.append(pltpu.make_async_remote_copy(slot, slot, send_sems.at[j], recv_sems.at[j],
                                                     device_id=(px, py, c), device_id_type=MESH))
    return sends, arrivals


N_DEVICES = 8


def _peer(x, y, c, f):
    px, py = _flip(x, y, f & 4, f & 2)
    return px, py, (1 - c if f & 1 else c)


def _scatter_all_copies(p_ref, r_ref, send_sems, recv_sems):
    half = p_ref.shape[1] // 2
    x, y, c = _place()
    me = 4 * x + 2 * y + c
    sends, arrivals = [], []
    for f in range(1, N_DEVICES):
        px, py, pc = _peer(x, y, c, f)
        theirs = p_ref.at[2 * px + py, pl.ds(pl.multiple_of(pc * half, 16), half), :]
        slot = r_ref.at[4 * px + 2 * py + pc]
        sends.append(pltpu.make_async_remote_copy(theirs, r_ref.at[me], send_sems.at[f - 1], recv_sems.at[f - 1],
                                                  device_id=(px, py, pc), device_id_type=MESH))
        arrivals.append(pltpu.make_async_remote_copy(slot, slot, send_sems.at[f - 1], recv_sems.at[f - 1],
                                                     device_id=(px, py, pc), device_id_type=MESH))
    return sends, arrivals


def _scatter_steps(first, last, p_ref, r_ref, sems):
    sends, arrivals = _scatter_all_copies(p_ref, r_ref, *sems)

    @pl.when(first)
    def _():
        for cp in sends:
            cp.start()

    @pl.when(last)
    def _():
        for cp in arrivals:
            cp.wait_recv()
        for cp in sends:
            cp.wait_send()


def _scatter_pieces(s):
    def body(s_ref, r_ref, send_sems, recv_sems):
        sends, arrivals = _scatter_copies(s_ref, r_ref, send_sems, recv_sems)
        for cp in sends:
            cp.start()
        for cp in arrivals:
            cp.wait_recv()
        for cp in sends:
            cp.wait_send()

    return pl.pallas_call(
        body, name="scatter_pieces", out_shape=jax.ShapeDtypeStruct(s.shape, s.dtype), in_specs=[_HBM], out_specs=_HBM,
        scratch_shapes=[pltpu.SemaphoreType.DMA((3,)), pltpu.SemaphoreType.DMA((3,))],
    )(s)


def _sum_pieces(r):
    slots, half = r.shape[:2]
    br = REDUCE_ROWS

    def kern(r_ref, o_ref):
        total = r_ref[0].astype(F32)
        for k in range(1, slots):
            total = total + r_ref[k].astype(F32)
        o_ref[...] = total

    return pl.pallas_call(
        kern, name="sum_pieces", grid=(half // br,), in_specs=[pl.BlockSpec((slots, br, PACK_COLS), lambda t: (0, t, 0))],
        out_specs=pl.BlockSpec((br, PACK_COLS), lambda t: (t, 0)), out_shape=_sds(half, PACK_COLS, F32),
        compiler_params=_params(("arbitrary",)))(r)


def _join_halves(f):
    def body(f_ref, o_ref, send_sem, recv_sem):
        x, y, c = _place()
        cp = pltpu.make_async_remote_copy(f_ref, o_ref, send_sem, recv_sem, device_id=(x, y, 1 - c), device_id_type=MESH)
        cp.start()
        cp.wait()

    return pl.pallas_call(
        body, name="join_halves", out_shape=jax.ShapeDtypeStruct(f.shape, f.dtype), in_specs=[_HBM], out_specs=_HBM,
        scratch_shapes=[pltpu.SemaphoreType.DMA, pltpu.SemaphoreType.DMA],
    )(f)


def _allreduce_steps(first, last, g_ref, o_ref, buf, send_sems, recv_sems):
    x, y, c = _place()
    me = 4 * x + 2 * y + c
    sends, arrivals = [], []
    for f in range(1, N_DEVICES):
        px, py, pc = _peer(x, y, c, f)
        slot = buf.at[4 * px + 2 * py + pc]
        sends.append(pltpu.make_async_remote_copy(g_ref, buf.at[me], send_sems.at[f - 1], recv_sems.at[f - 1],
                                                  device_id=(px, py, pc), device_id_type=MESH))
        arrivals.append(pltpu.make_async_remote_copy(slot, slot, send_sems.at[f - 1], recv_sems.at[f - 1],
                                                     device_id=(px, py, pc), device_id_type=MESH))

    @pl.when(first)
    def _():
        buf[me] = g_ref[...]
        for cp in sends:
            cp.start()

    @pl.when(last)
    def _():
        for cp in arrivals:
            cp.wait_recv()
        for cp in sends:
            cp.wait_send()
        total = buf[0]
        for k in range(1, N_DEVICES):
            total = total + buf[k]
        o_ref[...] = total


def _adamw_update(g_ref, w_ref, m_ref, v_ref, d_o, m_o, v_o):
    c1 = 1.0 - ADAM_B1 ** ADAM_STEP
    c2 = 1.0 - ADAM_B2 ** ADAM_STEP
    g_ = g_ref[...]
    m_new = ADAM_B1 * m_ref[...] + (1.0 - ADAM_B1) * g_
    v_new = ADAM_B2 * v_ref[...] + (1.0 - ADAM_B2) * (g_ * g_)
    m_o[...] = m_new
    v_o[...] = v_new
    d_o[...] = -ADAM_LR * ((m_new / c1) / (jnp.sqrt(v_new / c2) + ADAM_EPS) + ADAM_WD * w_ref[...])


ADAMW_ROWS = 256


def _adamw(name, g, w, m, v):
    _, rows, cols = w.shape
    br = int(np.gcd(ADAMW_ROWS, rows))

    def kern(*refs):
        _adamw_update(*refs)

    spec = pl.BlockSpec((1, br, cols), lambda t: (0, t, 0))
    out = jax.ShapeDtypeStruct(w.shape, F32)
    return pl.pallas_call(kern, name="adamw_" + name, grid=(rows // br,), in_specs=[spec] * 4, out_specs=[spec] * 3,
                          out_shape=[out, out, out], compiler_params=_params(("arbitrary",)))(g, w, m, v)


def _adamw_small(gs, ws, ms, vs):
    n = len(gs)

    def kern(*refs):
        ins, outs = refs[:4 * n], refs[4 * n:]
        for k in range(n):
            _adamw_update(ins[k], ins[n + k], ins[2 * n + k], ins[3 * n + k], outs[k], outs[n + k], outs[2 * n + k])

    vmem = pl.BlockSpec(memory_space=pltpu.VMEM)
    out = [jax.ShapeDtypeStruct(w.shape, F32) for w in ws]
    res = pl.pallas_call(kern, name="adamw_small", in_specs=[vmem] * (4 * n), out_specs=[vmem] * (3 * n),
                         out_shape=out * 3, compiler_params=pltpu.CompilerParams(vmem_limit_bytes=VMEM_LIMIT))(*gs, *ws, *ms, *vs)
    return [(res[k], res[n + k], res[2 * n + k]) for k in range(n)]


def _shard_rows(shape, axis):
    k, n = shape
    return (k * n // N_CHIPS) // PACK_COLS


def _group(names):
    entries = [e for e in SHARDED if e[0] in names]
    used = sum(_shard_rows(shape, axis) for _, shape, axis in entries)
    return entries, -(-used // (2 * REDUCE_ROWS)) * 2 * REDUCE_ROWS


def _pack_shards(shards, names, dtype):
    entries, rows = _group(names)
    parts = [shards[name].astype(dtype).reshape(-1, PACK_COLS) for name, _, _ in entries]
    used = sum(p.shape[0] for p in parts)
    if rows > used:
        parts.append(jnp.zeros((rows - used, PACK_COLS), dtype))
    return jnp.concatenate(parts, 0)


def _unpack_shards(packed, names):
    out, r0 = {}, 0
    for name, (k, n), axis in _group(names)[0]:
        nr = _shard_rows((k, n), axis)
        shape = (k // N_CHIPS, n) if axis == 0 else (k, n // N_CHIPS)
        out[name] = packed[r0:r0 + nr].reshape(shape)
        r0 += nr
    return out


def _unpack_full(gathered, names):
    out, r0 = {}, 0
    for name, (k, n), axis in _group(names)[0]:
        nr = _shard_rows((k, n), axis)
        part = gathered[:, r0:r0 + nr]
        if axis == 0:
            out[name] = part.reshape(k, n)
        else:
            out[name] = part.reshape(N_CHIPS, k, n // N_CHIPS).transpose(1, 0, 2).reshape(k, n)
        r0 += nr
    return out


def _pack_pieces(grads, names, dtype=F32):
    entries, rows = _group(names)
    parts = []
    for name, (k, n), axis in entries:
        g = grads[name].astype(dtype)
        if axis == 0:
            parts.append(g.reshape(N_CHIPS, -1, PACK_COLS))
        else:
            parts.append(g.reshape(k, N_CHIPS, n // N_CHIPS).transpose(1, 0, 2).reshape(N_CHIPS, -1, PACK_COLS))
    used = sum(p.shape[1] for p in parts)
    if rows > used:
        parts.append(jnp.zeros((N_CHIPS, rows - used, PACK_COLS), dtype))
    return jnp.concatenate(parts, 1)


def _pack_small(vals):
    flat = jnp.concatenate([vals[name].astype(F32).reshape(-1) for name, _ in SMALL])
    flat = jnp.concatenate([flat, jnp.zeros((SMALL_ROWS * PACK_COLS - flat.shape[0],), F32)])
    return flat.reshape(SMALL_ROWS, PACK_COLS)


def _unpack_small(packed):
    flat, out, o = packed.reshape(-1), {}, 0
    for name, shape in SMALL:
        n = int(np.prod(shape))
        out[name] = flat[o:o + n].reshape(shape)
        o += n
    return out


def kernel(x, p, positions, g_pre_mix, w_in, b_gate, g_q, w_uq, g_kv, w_ukv, w_pool, pool_scale, w_branch_attn, w_branch_pool, w_out, g_post_mix, g_pre_mlp, w_ff1, w_ff2, g_post_mlp, w_ple_proj, w_ple_gate, g_ple, loss_target, m_g_pre_mix, m_w_in, m_b_gate, m_g_q, m_w_uq, m_g_kv, m_w_ukv, m_w_pool, m_pool_scale, m_w_branch_attn, m_w_branch_pool, m_w_out, m_g_post_mix, m_g_pre_mlp, m_w_ff1, m_w_ff2, m_g_post_mlp, m_w_ple_proj, m_w_ple_gate, m_g_ple, v_g_pre_mix, v_w_in, v_b_gate, v_g_q, v_w_uq, v_g_kv, v_w_ukv, v_w_pool, v_pool_scale, v_w_branch_attn, v_w_branch_pool, v_w_out, v_g_post_mix, v_g_pre_mlp, v_w_ff1, v_w_ff2, v_g_post_mlp, v_w_ple_proj, v_w_ple_gate, v_g_ple):
    given = dict(locals())
    weights = {n: given[n] for n in WEIGHT_ORDER}
    moments_m = {n: given["m_" + n] for n in WEIGHT_ORDER}
    moments_v = {n: given["v_" + n] for n in WEIGHT_ORDER}
    c = lax.axis_index("c")

    big_w = {name: weights[name][0] for name, _, _ in SHARDED}
    my_chip = 2 * lax.axis_index("x") + lax.axis_index("y")
    packed_first = _pack_shards(big_w, FIRST, MX)
    full = _unpack_full(lax.dynamic_update_slice(_allgather_shards(packed_first), packed_first[None], (my_chip, 0, 0)), FIRST)
    for name, _ in SMALL:
        full[name] = weights[name][0] if name == "w_pool" else weights[name]

    loss_cols, dx, grads, (pieces_rest, received_rest) = _local_step(
        x[0], p[0, 0], positions[0], loss_target[0], full, (my_chip, c), _pack_shards(big_w, REST, MX))
    loss = lax.psum(0.5 * jnp.sum(loss_cols) / D_MODEL, ("x", "y", "c"))

    def finish(received, mine, slot):
        reduced = _sum_pieces(lax.dynamic_update_slice(received, mine, (slot, 0, 0)))
        theirs = _join_halves(reduced)
        return jnp.where(c == 0, jnp.concatenate([reduced, theirs]), jnp.concatenate([theirs, reduced]))

    pieces = _pack_pieces(grads, FIRST)
    sent = _add_halves(pieces, _exchange_halves(pieces), c)
    mine = lax.dynamic_slice(sent, (my_chip, 0, 0), (1,) + sent.shape[1:])
    shards = _unpack_shards(finish(_scatter_pieces(sent), mine, my_chip), FIRST)
    half = received_rest.shape[1]
    mine = lax.dynamic_slice(pieces_rest, (my_chip, c * half, 0), (1, half, PACK_COLS))
    shards.update(_unpack_shards(finish(received_rest, mine, 2 * my_chip + c), REST))

    out = {}
    for name, g in shards.items():
        out[name] = (g[None], *_adamw(name, g[None], weights[name], moments_m[name], moments_v[name]))
    small_g = {n: grads[n] for n, _ in SMALL}
    names = [n for n, _ in SMALL]
    updates = _adamw_small([small_g[n] for n in names], [weights[n] for n in names], [moments_m[n] for n in names],
                           [moments_v[n] for n in names])
    for n, upd in zip(names, updates):
        out[n] = (small_g[n], *upd)
    return (loss, dx[None], *[out[n][k] for k in range(4) for n in WEIGHT_ORDER])
```

```python
import numpy as np
import jax
import jax.numpy as jnp
from jax import lax
from jax.experimental import pallas as pl
from jax.experimental.pallas import tpu as pltpu

F32 = jnp.float32
MX = jnp.bfloat16
WIRE = jnp.bfloat16

D_MODEL = 1024
N_HEADS = 8
QK_NOPE = 64
QK_ROPE = 32
V_HEAD = 64
Q_LORA = 384
KV_LORA = 256
POOL_WINDOWS = (2, 4, 8, 16)
POOL_GROUP = 128
POOL_WIDTH = 512
D_FF = 4096
ROPE_THETA = 10000.0
EPS = 1e-6
HEAD_SLOT = 128
QK_WIDTH = N_HEADS * HEAD_SLOT
ROPE_LANE = 64
SMALL_COLS = Q_LORA + KV_LORA + HEAD_SLOT
IN_PAD = SMALL_COLS + POOL_WIDTH + 2 * D_MODEL
SCALE = (QK_NOPE + QK_ROPE) ** -0.5
LOG2E = 1.4426950408889634
NEG = -1e30
HALO = 16

ADAM_LR = 0.001
ADAM_B1 = 0.9
ADAM_B2 = 0.999
ADAM_EPS = 1e-08
ADAM_WD = 0.01
ADAM_STEP = 10

VMEM_LIMIT = 56 * 2**20
TOKEN_TILE = 512
MESH = pl.DeviceIdType.MESH

SHARDED = (
    ("w_in", (1024, 3232), 1),
    ("w_uq", (384, 768), 1),
    ("w_ukv", (256, 1024), 1),
    ("w_branch_attn", (512, 1024), 1),
    ("w_branch_pool", (512, 1024), 1),
    ("w_out", (1024, 1024), 0),
    ("w_ff1", (1024, 4096), 1),
    ("w_ff2", (4096, 1024), 0),
    ("w_ple_proj", (256, 1024), 1),
    ("w_ple_gate", (1024, 1024), 0),
)
SMALL = (
    ("g_pre_mix", (1, 1024)),
    ("b_gate", (1, 2048)),
    ("g_q", (1, 384)),
    ("g_kv", (1, 256)),
    ("w_pool", (1, 4, 128, 128)),
    ("pool_scale", (1, 512)),
    ("g_post_mix", (1, 1024)),
    ("g_pre_mlp", (1, 1024)),
    ("g_post_mlp", (1, 1024)),
    ("g_ple", (1, 1024)),
)
WEIGHT_ORDER = ("g_pre_mix", "w_in", "b_gate", "g_q", "w_uq", "g_kv", "w_ukv", "w_pool", "pool_scale", "w_branch_attn",
                "w_branch_pool", "w_out", "g_post_mix", "g_pre_mlp", "w_ff1", "w_ff2", "g_post_mlp", "w_ple_proj",
                "w_ple_gate", "g_ple")
N_CHIPS = 4
PACK_COLS = 1024
REDUCE_ROWS = 160
SMALL_ROWS = 80
FIRST = ("w_in", "w_uq", "w_ukv")
REST = tuple(name for name, _, _ in SHARDED if name not in FIRST)


def _dot(a, b):
    return jnp.dot(a.astype(MX), b.astype(MX), preferred_element_type=F32)


def _dot_nt(a, b):
    return lax.dot_general(a.astype(MX), b.astype(MX), (((1,), (1,)), ((), ())), preferred_element_type=F32)


def _dot_tn(a, b):
    return lax.dot_general(a.astype(MX), b.astype(MX), (((0,), (0,)), ((), ())), preferred_element_type=F32)


def _sig(x):
    return 1.0 / (1.0 + jnp.exp(-x))


def _rms(x, g):
    r = lax.rsqrt(jnp.mean(x * x, axis=1, keepdims=True) + EPS)
    xh = x * r
    return xh * g, xh, r


def _rms_bwd(xh, r, g, dy):
    dxn = dy * g
    dx = r * (dxn - xh * jnp.mean(dxn * xh, axis=1, keepdims=True))
    return dx, jnp.sum(dy * xh, axis=0, keepdims=True)


def _rot_half(v):
    lane = lax.broadcasted_iota(jnp.int32, v.shape, 1)
    return jnp.where(lane < ROPE_LANE + QK_ROPE // 2, pltpu.roll(v, HEAD_SLOT - QK_ROPE // 2, 1), pltpu.roll(v, QK_ROPE // 2, 1))


def _rope(v, cc, sa):
    return v * cc + _rot_half(v) * sa


def _unrope(v, cc, sa):
    return v * cc - _rot_half(v) * sa


def _params(sem):
    return pltpu.CompilerParams(dimension_semantics=sem, vmem_limit_bytes=VMEM_LIMIT)


def _tok_call(name, body, n_tok, tm, tiled, resident, outs, accs=(), scratch=(), exchange=None):
    def as_pair(t):
        if isinstance(t, tuple):
            return t
        return t, pl.BlockSpec((tm, t.shape[1]), lambda i: (i, 0))
    tiled = [as_pair(t) for t in tiled]
    outs = [as_pair(o) for o in outs]
    res_specs = [pl.BlockSpec(r.shape, lambda i, nd=r.ndim: (0,) * nd, pipeline_mode=pl.Buffered(1)) for r in resident]
    out_specs = [s for _, s in outs] + [pl.BlockSpec(a.shape, lambda i: (0, 0)) for a in accs]
    n_t, n_r, n_o, n_a, n_s = len(tiled), len(resident), len(outs), len(accs), len(scratch)
    n_steps = n_tok // tm
    operands = [a for a, _ in tiled] + list(resident)
    in_specs = [s for _, s in tiled] + res_specs
    out_shape = [o for o, _ in outs] + list(accs)
    scratch = list(scratch)
    if exchange is not None:
        ex_in, ex_out, ex_sems, ex_steps = exchange
        operands.append(ex_in)
        in_specs.append(_HBM)
        out_shape.append(ex_out)
        out_specs.append(_HBM)
        scratch += list(ex_sems)

    def kern(*refs):
        refs = list(refs)
        n_in = n_t + n_r + (exchange is not None)
        n_out = n_o + n_a + (exchange is not None)
        tin, res = refs[:n_t], refs[n_t:n_t + n_r]
        tout = refs[n_in:n_in + n_o]
        acc = refs[n_in + n_o:n_in + n_o + n_a]
        scr = refs[n_in + n_out:n_in + n_out + n_s]
        i = pl.program_id(0)
        if exchange is not None:
            ex_steps(i, n_steps, refs[n_in - 1], refs[n_in + n_out - 1], refs[n_in + n_out + n_s:])

        @pl.when(i == 0)
        def _():
            for a in acc:
                a[...] = jnp.zeros(a.shape, a.dtype)
        body(i, tin, res, tout, acc, scr)

    return pl.pallas_call(
        kern, name=name, grid=(n_steps,), in_specs=in_specs, out_specs=out_specs,
        out_shape=out_shape, scratch_shapes=scratch, compiler_params=_params(("arbitrary",)),
    )(*operands)


def _sds(rows, cols, dtype):
    return jax.ShapeDtypeStruct((rows, cols), dtype)


def _fwd_inproj(x, cc, sa, w, tm, gather=None):
    n_tok = x.shape[0]

    def body(i, tin, res, tout, acc, scr):
        x_ref, c_ref, s_ref = tin
        g_pre, w_in, g_q, w_uq, g_kv, w_k, w_v, e_mat, w_kt, e_t, w_vt, v_ones = res
        a_o, ps_o, u_o, gl_o, qn_o, kvn_o, q_o, k_o, v_o, kt_o, vt_o = tout
        a = _rms(x_ref[...], g_pre[...])[0].astype(MX)
        a_o[...] = a
        ps = _dot(a, w_in[:, :SMALL_COLS])
        ps_o[...] = ps.astype(ps_o.dtype)
        u_o[...] = _dot(a, w_in[:, SMALL_COLS:SMALL_COLS + POOL_WIDTH]).astype(u_o.dtype)
        gl_o[...] = _dot(a, w_in[:, SMALL_COLS + POOL_WIDTH:]).astype(gl_o.dtype)
        cc_, sa_ = c_ref[...], s_ref[...]
        qn = _rms(ps[:, :Q_LORA], g_q[...])[0].astype(MX)
        qn_o[...] = qn
        q = _dot(qn, w_uq[...])
        for h in range(N_HEADS):
            hs = slice(h * HEAD_SLOT, (h + 1) * HEAD_SLOT)
            q_o[:, hs] = (_rope(q[:, hs], cc_, sa_) * (SCALE * LOG2E)).astype(q_o.dtype)
        kvn = _rms(ps[:, Q_LORA:Q_LORA + KV_LORA], g_kv[...])[0].astype(MX)
        kvn_o[...] = kvn
        kr = _rope(ps[:, Q_LORA + KV_LORA:], cc_, sa_)
        k_o[...] = (_dot(kvn, w_k[...]) + _dot(kr, e_mat[...])).astype(k_o.dtype)
        v_o[...] = _dot(kvn, w_v[...]).astype(v_o.dtype)
        kt_o[...] = (_dot_nt(w_kt[...], kvn) + _dot_nt(e_t[...], kr)).astype(kt_o.dtype)
        vt_o[...] = (_dot_nt(w_vt[...], kvn) + v_ones[...]).astype(vt_o.dtype)

    outs = [_sds(n_tok, D_MODEL, MX), _sds(n_tok, SMALL_COLS, MX), _sds(n_tok, POOL_WIDTH, MX), _sds(n_tok, 2 * D_MODEL, MX),
            _sds(n_tok, Q_LORA, MX), _sds(n_tok, KV_LORA, MX), _sds(n_tok, QK_WIDTH, MX), _sds(n_tok, QK_WIDTH, MX),
            _sds(n_tok, N_HEADS * V_HEAD, MX),
            (_sds(QK_WIDTH, n_tok, MX), pl.BlockSpec((QK_WIDTH, tm), lambda i: (0, i))),
            (_sds(N_HEADS * V_ROWS, n_tok, MX), pl.BlockSpec((N_HEADS * V_ROWS, tm), lambda i: (0, i)))]
    res = [w["g_pre_mix"], w["w_in"], w["g_q"], w["w_uq"], w["g_kv"], w["w_k"], w["w_v"], w["e_mat"], w["w_kt"], w["e_t"],
           w["w_vt"], w["v_ones"]]
    exchange = None
    if gather is not None:
        gathered = jax.ShapeDtypeStruct((N_CHIPS,) + gather.shape, gather.dtype)
        exchange = (gather, gathered, [pltpu.SemaphoreType.DMA((6,)), pltpu.SemaphoreType.DMA((6,))], _gather_steps)
    return _tok_call("fwd_inproj", body, n_tok, tm, [x, cc, sa], res, outs, exchange=exchange)


def _causal_pairs(nq, ratio, by_kv):
    if by_kv:
        pairs = [(i, j) for j in range(nq * ratio) for i in range(j // ratio, nq)]
    else:
        pairs = [(i, j) for i in range(nq) for j in range((i + 1) * ratio)]
    return (jnp.asarray(np.array([p[0] for p in pairs], np.int32)), jnp.asarray(np.array([p[1] for p in pairs], np.int32)))


def _keep_t(tk, tq, off):
    return lax.broadcasted_iota(jnp.int32, (tk, tq), 0) + off <= lax.broadcasted_iota(jnp.int32, (tk, tq), 1)


ATTN_FWD_TILE = (1024, 1024)
ATTN_BWD_TILE = (1024, 512)
V_ROWS = 80


def _attn_fwd(q, k, vt, tq, tk):
    n_tok = q.shape[0]
    nq, ratio = n_tok // tq, tq // tk
    qi, kj = _causal_pairs(nq, ratio, by_kv=False)

    def kern(qi_ref, kj_ref, q_ref, k_ref, vt_ref, o_ref, lse_ref, m_s, acc_s, st_s):
        s_id = pl.program_id(0)
        i, j = qi_ref[s_id], kj_ref[s_id]

        @pl.when(j == 0)
        def _():
            m_s[...] = jnp.full(m_s.shape, NEG, F32)
            acc_s[...] = jnp.zeros(acc_s.shape, F32)

        def scores(h):
            hs = slice(h * HEAD_SLOT, (h + 1) * HEAD_SLOT)
            return _dot_nt(k_ref[:, hs], q_ref[:, hs])

        def heads(masked):
            keep = _keep_t(tk, tq, j * tk - i * tq) if masked else None
            st_s[0] = scores(0)
            for h in range(N_HEADS):
                if h + 1 < N_HEADS:
                    st_s[(h + 1) % 2] = scores(h + 1)
                st = st_s[h % 2]
                if masked:
                    st = jnp.where(keep, st, NEG)
                m_old = m_s[h]
                m_new = jnp.maximum(m_old, jnp.max(st, axis=0, keepdims=True))
                pt = jnp.exp2(st - m_new)
                acc_s[h] = jnp.exp2(m_old - m_new) * acc_s[h] + _dot(vt_ref[h * V_ROWS:(h + 1) * V_ROWS, :], pt)
                m_s[h] = m_new

        @pl.when(j < i * ratio)
        def _():
            heads(False)

        @pl.when(j >= i * ratio)
        def _():
            heads(True)

        @pl.when(j == (i + 1) * ratio - 1)
        def _():
            heads_out = []
            for h in range(N_HEADS):
                total = acc_s[h, V_HEAD:V_HEAD + 1, :]
                heads_out.append(acc_s[h, :V_HEAD, :] / total)
                lse_ref[h:h + 1, :] = m_s[h] + jnp.log2(total)
            o_ref[...] = jnp.concatenate(heads_out, 0).T.astype(o_ref.dtype)

    gs = pltpu.PrefetchScalarGridSpec(
        num_scalar_prefetch=2, grid=(qi.shape[0],),
        in_specs=[pl.BlockSpec((tq, QK_WIDTH), lambda s, qi, kj: (qi[s], 0)),
                  pl.BlockSpec((tk, QK_WIDTH), lambda s, qi, kj: (kj[s], 0)),
                  pl.BlockSpec((N_HEADS * V_ROWS, tk), lambda s, qi, kj: (0, kj[s]))],
        out_specs=[pl.BlockSpec((tq, N_HEADS * V_HEAD), lambda s, qi, kj: (qi[s], 0)),
                   pl.BlockSpec((N_HEADS, tq), lambda s, qi, kj: (0, qi[s]))],
        scratch_shapes=[pltpu.VMEM((N_HEADS, 1, tq), F32), pltpu.VMEM((N_HEADS, V_ROWS, tq), F32),
                        pltpu.VMEM((2, tk, tq), F32)])
    return pl.pallas_call(kern, name="attn_fwd", grid_spec=gs,
                          out_shape=[_sds(n_tok, N_HEADS * V_HEAD, MX), _sds(N_HEADS, n_tok, F32)],
                          compiler_params=_params(("arbitrary",)))(qi, kj, q, k, vt)


def _pool_windows(ext, i, tm, first_row):
    row = i * tm + lax.broadcasted_iota(jnp.int32, (tm, 1), 0)
    out = []
    for g, w in enumerate(POOL_WINDOWS):
        cs = slice(g * POOL_GROUP, (g + 1) * POOL_GROUP)
        s = ext[pl.ds(first_row, tm), cs]
        for k in range(1, w):
            s = s + ext[pl.ds(first_row - k, tm), cs]
        cnt = jnp.minimum(row + 1, w).astype(F32)
        out.append(s / cnt)
    return out


def _fwd_mix(x, u, gl, attn, w, tm):
    n_tok = x.shape[0]
    halo_spec = pl.BlockSpec((HALO, POOL_WIDTH), lambda i: (jnp.maximum(i * (tm // HALO) - 1, 0), 0))

    def body(i, tin, res, tout, acc, scr):
        x_ref, u_ref, uh_ref, gl_ref, at_ref = tin
        w_pool, pool_scale, w_ba, w_bp, b_gate, w_out, g_post = res
        d_o, pooled_o, a_o, pp_o, merged_o, y_o, h1_o = tout
        ext, = scr
        ext[pl.ds(0, HALO), :] = jnp.where(i > 0, uh_ref[...].astype(F32), 0.0)
        ext[pl.ds(HALO, tm), :] = u_ref[...].astype(F32)
        means = _pool_windows(ext, i, tm, HALO)
        for g in range(len(POOL_WINDOWS)):
            cs = slice(g * POOL_GROUP, (g + 1) * POOL_GROUP)
            d = (means[g] - ext[pl.ds(HALO, tm), cs]).astype(MX)
            d_o[:, cs] = d
            pooled_o[:, cs] = (_dot(d, w_pool[g]) * pool_scale[:, cs]).astype(pooled_o.dtype)
        a_br = _dot(at_ref[...], w_ba[...])
        p_br = _dot(pooled_o[...], w_bp[...])
        a_o[...] = a_br.astype(a_o.dtype)
        pp_o[...] = p_br.astype(pp_o.dtype)
        gates = _sig(gl_ref[...].astype(F32) + b_gate[...])
        merged = (gates[:, :D_MODEL] * a_br + gates[:, D_MODEL:] * p_br).astype(MX)
        merged_o[...] = merged
        y = _dot(merged, w_out[...])
        y_o[...] = y.astype(y_o.dtype)
        h1_o[...] = x_ref[...] + _rms(y, g_post[...])[0]

    outs = [_sds(n_tok, POOL_WIDTH, MX), _sds(n_tok, POOL_WIDTH, MX), _sds(n_tok, D_MODEL, MX), _sds(n_tok, D_MODEL, MX),
            _sds(n_tok, D_MODEL, MX), _sds(n_tok, D_MODEL, MX), _sds(n_tok, D_MODEL, F32)]
    res = [w["w_pool"], w["pool_scale"], w["w_branch_attn"], w["w_branch_pool"], w["b_gate"], w["w_out"], w["g_post_mix"]]
    return _tok_call("fwd_mix", body, n_tok, tm, [x, u, (u, halo_spec), gl, attn], res, outs,
                     scratch=[pltpu.VMEM((tm + HALO, POOL_WIDTH), F32)])


def _fwd_mlp(h1, w, tm):
    n_tok = h1.shape[0]

    def body(i, tin, res, tout, acc, scr):
        h1_ref, = tin
        g_pre, w1, w2, g_post = res
        m_o, zr_o, f_o, h2_o = tout
        h1_ = h1_ref[...]
        m = _rms(h1_, g_pre[...])[0].astype(MX)
        m_o[...] = m
        zr = jnp.maximum(_dot(m, w1[...]), 0.0)
        zr_o[...] = zr.astype(zr_o.dtype)
        a2 = (zr * zr).astype(MX)
        f = _dot(a2, w2[...])
        f_o[...] = f.astype(f_o.dtype)
        h2_o[...] = h1_ + _rms(f, g_post[...])[0]

    outs = [_sds(n_tok, D_MODEL, MX), _sds(n_tok, D_FF, MX), _sds(n_tok, D_MODEL, MX),
            _sds(n_tok, D_MODEL, F32)]
    res = [w["g_pre_mlp"], w["w_ff1"], w["w_ff2"], w["g_post_mlp"]]
    return _tok_call("fwd_mlp", body, n_tok, tm, [h1], res, outs)


def _ple_fwd_bwd(h2, p, target, w, tm):
    n_tok = h2.shape[0]

    def body(i, tin, res, tout, acc, scr):
        h2_ref, p_ref, t_ref = tin
        w_pe, w_pg, g_ple = res
        dh2_o, de_o, dzg_o = tout
        loss_a, dg_a = acc
        h2_ = h2_ref[...]
        e = _dot(p_ref[...], w_pe[...])
        pg = _sig(_dot(h2_, w_pg[...]))
        t = pg * e
        g = g_ple[...]
        tn, th, r = _rms(t, g)
        diff = h2_ + tn - t_ref[...]
        loss_a[...] += jnp.sum(diff * diff, axis=0, keepdims=True)
        dh3 = diff * (1.0 / D_MODEL)
        dt, dg = _rms_bwd(th, r, g, dh3)
        dg_a[...] += dg
        de_o[...] = (dt * pg).astype(de_o.dtype)
        dzg = (dt * e * pg * (1.0 - pg)).astype(MX)
        dzg_o[...] = dzg
        dh2_o[...] = dh3 + _dot_nt(dzg, w_pg[...])

    outs = [_sds(n_tok, D_MODEL, F32), _sds(n_tok, D_MODEL, MX), _sds(n_tok, D_MODEL, MX)]
    accs = [_sds(1, D_MODEL, F32), _sds(1, D_MODEL, F32)]
    return _tok_call("ple_fwd_bwd", body, n_tok, tm, [h2, p, target], [w["w_ple_proj"], w["w_ple_gate"], w["g_ple"]], outs, accs)


def _bwd_mlp(dh2, f, h1, zr, w, tm):
    n_tok = dh2.shape[0]

    def body(i, tin, res, tout, acc, scr):
        dh2_ref, f_ref, h1_ref, zr_ref = tin
        g_pre, w1, w2, g_post = res
        df_o, dz_o, dh1_o = tout
        dg_post_a, dg_pre_a = acc
        dh2_ = dh2_ref[...]
        gp = g_post[...]
        _, fh, rf = _rms(f_ref[...].astype(F32), gp)
        df, dg = _rms_bwd(fh, rf, gp, dh2_)
        dg_post_a[...] += dg
        df = df.astype(MX)
        df_o[...] = df
        dz = (_dot_nt(df, w2[...]) * (2.0 * zr_ref[...].astype(F32))).astype(MX)
        dz_o[...] = dz
        dm = _dot_nt(dz, w1[...])
        gq = g_pre[...]
        _, hh, rh = _rms(h1_ref[...], gq)
        dh1, dg = _rms_bwd(hh, rh, gq, dm)
        dg_pre_a[...] += dg
        dh1_o[...] = dh2_ + dh1

    outs = [_sds(n_tok, D_MODEL, MX), _sds(n_tok, D_FF, MX), _sds(n_tok, D_MODEL, F32)]
    accs = [_sds(1, D_MODEL, F32), _sds(1, D_MODEL, F32)]
    res = [w["g_pre_mlp"], w["w_ff1"], w["w_ff2"], w["g_post_mlp"]]
    return _tok_call("bwd_mlp", body, n_tok, tm, [dh2, f, h1, zr], res, outs, accs)


def _bwd_mix(dh1, y, a_br, p_br, gl, attn, d, w, tm):
    n_tok = dh1.shape[0]

    def body(i, tin, res, tout, acc, scr):
        dh1_ref, y_ref, a_ref, pp_ref, gl_ref, at_ref, d_ref = tin
        g_post, w_out, b_gate, w_ba, w_bp, w_pool, pool_scale, sel = res
        dy_o, da_o, dpp_o, dgl_o, do_o, delta_o, dyp_o, dd_o = tout
        dg_post_a, db_a, dps_a = acc
        g = g_post[...]
        _, yh, r = _rms(y_ref[...].astype(F32), g)
        dy, dg = _rms_bwd(yh, r, g, dh1_ref[...])
        dg_post_a[...] += dg
        dy = dy.astype(MX)
        dy_o[...] = dy
        dmerged = _dot_nt(dy, w_out[...])
        gates = _sig(gl_ref[...].astype(F32) + b_gate[...])
        ga, gp = gates[:, :D_MODEL], gates[:, D_MODEL:]
        da = (dmerged * ga).astype(MX)
        dpp = (dmerged * gp).astype(MX)
        da_o[...] = da
        dpp_o[...] = dpp
        dgl_a = dmerged * a_ref[...].astype(F32) * ga * (1.0 - ga)
        dgl_p = dmerged * pp_ref[...].astype(F32) * gp * (1.0 - gp)
        dgl_o[:, :D_MODEL] = dgl_a.astype(dgl_o.dtype)
        dgl_o[:, D_MODEL:] = dgl_p.astype(dgl_o.dtype)
        db_a[:, :D_MODEL] += jnp.sum(dgl_a, axis=0, keepdims=True)
        db_a[:, D_MODEL:] += jnp.sum(dgl_p, axis=0, keepdims=True)
        do = _dot_nt(da, w_ba[...]).astype(MX)
        do_o[...] = do
        prod = do.astype(F32) * at_ref[...].astype(F32)
        hi = prod.astype(MX)
        lo = (prod - hi.astype(F32)).astype(MX)
        delta_o[...] = _dot(hi, sel[...]) + _dot(lo, sel[...])
        dpooled = _dot_nt(dpp, w_bp[...])
        for gi in range(len(POOL_WINDOWS)):
            cs = slice(gi * POOL_GROUP, (gi + 1) * POOL_GROUP)
            ypre = _dot(d_ref[:, cs], w_pool[gi])
            dps_a[:, cs] += jnp.sum(dpooled[:, cs] * ypre, axis=0, keepdims=True)
            dyp = (dpooled[:, cs] * pool_scale[:, cs]).astype(MX)
            dyp_o[:, cs] = dyp
            dd_o[:, cs] = _dot_nt(dyp, w_pool[gi])

    outs = [_sds(n_tok, D_MODEL, MX), _sds(n_tok, D_MODEL, MX), _sds(n_tok, D_MODEL, MX), _sds(n_tok, 2 * D_MODEL, MX),
            _sds(n_tok, N_HEADS * V_HEAD, MX), _sds(n_tok, HEAD_SLOT, F32), _sds(n_tok, POOL_WIDTH, MX),
            _sds(n_tok, POOL_WIDTH, F32)]
    accs = [_sds(1, D_MODEL, F32), _sds(1, 2 * D_MODEL, F32), _sds(1, POOL_WIDTH, F32)]
    res = [w["g_post_mix"], w["w_out"], w["b_gate"], w["w_branch_attn"], w["w_branch_pool"], w["w_pool"], w["pool_scale"],
           w["head_sel"]]
    return _tok_call("bwd_mix", body, n_tok, tm, [dh1, y, a_br, p_br, gl, attn, d], res, outs, accs)


def _bwd_heads(q_ref, k_ref, v_ref, do_ref, lse_ref, dl_ref, st_s, dpt_s, keep, use, n_heads):
    def products(h):
        hs = slice(h * HEAD_SLOT, (h + 1) * HEAD_SLOT)
        vs = slice(h * V_HEAD, (h + 1) * V_HEAD)
        st_s[h % 2] = _dot_nt(k_ref[:, hs], q_ref[:, hs])
        dpt_s[h % 2] = _dot_nt(v_ref[:, vs], do_ref[:, vs])

    products(0)
    for h in range(n_heads):
        if h + 1 < n_heads:
            products(h + 1)
        st = st_s[h % 2]
        if keep is not None:
            st = jnp.where(keep, st, NEG)
        pt = jnp.exp2(st - lse_ref[h:h + 1, :])
        use(h, pt, pt * (dpt_s[h % 2] - dl_ref[h:h + 1, :]))


HEAD_GROUP = 4


def _attn_bwd(q, k, kt, v, do, lse, delta, tq, tk, scatter=None):
    n_tok = q.shape[0]
    nq, ratio = n_tok // tq, tq // tk
    n_groups = N_HEADS // HEAD_GROUP
    gq, gv = HEAD_GROUP * HEAD_SLOT, HEAD_GROUP * V_HEAD
    qi, kj = _causal_pairs(nq, ratio, by_kv=True)

    n_pairs = qi.shape[0]

    def kern(qi_ref, kj_ref, q_ref, k_ref, kt_ref, v_ref, do_ref, lse_ref, dl_ref, *rest):
        if scatter is not None:
            s_hbm, dq_ref, dk_ref, dv_ref, r_hbm, dk_s, dv_s, st_s, dpt_s, send_sems, recv_sems = rest
        else:
            dq_ref, dk_ref, dv_ref, dk_s, dv_s, st_s, dpt_s = rest
        s_id = pl.program_id(1)
        i, j = qi_ref[s_id], kj_ref[s_id]
        cols = pl.ds(pl.multiple_of(i * tq, tq), tq)
        if scatter is not None:
            group = pl.program_id(0)
            _scatter_steps(jnp.logical_and(group == 0, s_id == 0),
                           jnp.logical_and(group == n_groups - 1, s_id == n_pairs - 1), s_hbm, r_hbm, (send_sems, recv_sems))

        @pl.when(s_id == 0)
        def _():
            dq_ref[...] = jnp.zeros(dq_ref.shape, F32)

        def use(h, pt, dst):
            hs = slice(h * HEAD_SLOT, (h + 1) * HEAD_SLOT)
            dv_s[h] += _dot(pt, do_ref[:, h * V_HEAD:(h + 1) * V_HEAD])
            dk_s[:, hs] += _dot(dst, q_ref[:, hs])
            dq_ref[hs, cols] += _dot(kt_ref[hs, :], dst)

        def heads(masked):
            keep = _keep_t(tk, tq, j * tk - i * tq) if masked else None
            _bwd_heads(q_ref, k_ref, v_ref, do_ref, lse_ref.at[0], dl_ref.at[0], st_s, dpt_s, keep, use, HEAD_GROUP)

        @pl.when(j >= i * ratio)
        def _():
            dk_s[...] = jnp.zeros(dk_s.shape, F32)
            dv_s[...] = jnp.zeros(dv_s.shape, F32)
            heads(True)

        @pl.when(j < i * ratio)
        def _():
            heads(False)

        @pl.when(i == nq - 1)
        def _():
            dk_ref[...] = (dk_s[...] * (1.0 / LOG2E)).astype(dk_ref.dtype)
            for h in range(HEAD_GROUP):
                dv_ref[:, h * V_HEAD:(h + 1) * V_HEAD] = dv_s[h].astype(dv_ref.dtype)

    at_q = lambda g, s, qi, kj: (qi[s], g)
    at_k = lambda g, s, qi, kj: (kj[s], g)
    at_kt = lambda g, s, qi, kj: (g, kj[s])
    at_stat = lambda g, s, qi, kj: (g, 0, qi[s])
    in_specs = [pl.BlockSpec((tq, gq), at_q), pl.BlockSpec((tk, gq), at_k), pl.BlockSpec((gq, tk), at_kt),
                pl.BlockSpec((tk, gv), at_k), pl.BlockSpec((tq, gv), at_q),
                pl.BlockSpec((1, HEAD_GROUP, tq), at_stat), pl.BlockSpec((1, HEAD_GROUP, tq), at_stat)]
    out_specs = [pl.BlockSpec((gq, n_tok), lambda g, s, qi, kj: (g, 0), pipeline_mode=pl.Buffered(1)),
                 pl.BlockSpec((tk, gq), at_k), pl.BlockSpec((tk, gv), at_k)]
    out_shape = [_sds(QK_WIDTH, n_tok, F32), _sds(n_tok, QK_WIDTH, MX), _sds(n_tok, N_HEADS * V_HEAD, MX)]
    scratch = [pltpu.VMEM((tk, gq), F32), pltpu.VMEM((HEAD_GROUP, tk, V_HEAD), F32),
               pltpu.VMEM((2, tk, tq), F32), pltpu.VMEM((2, tk, tq), F32)]
    stat3 = lambda a: a.reshape(n_groups, HEAD_GROUP, n_tok)
    operands = [qi, kj, q, k, kt, v, do, stat3(lse), stat3(delta)]
    if scatter is not None:
        operands.append(scatter)
        in_specs.append(_HBM)
        out_specs.append(_HBM)
        out_shape.append(jax.ShapeDtypeStruct((N_DEVICES, scatter.shape[1] // 2, PACK_COLS), scatter.dtype))
        scratch += [pltpu.SemaphoreType.DMA((N_DEVICES - 1,)), pltpu.SemaphoreType.DMA((N_DEVICES - 1,))]
    gs = pltpu.PrefetchScalarGridSpec(num_scalar_prefetch=2, grid=(n_groups, n_pairs), in_specs=in_specs,
                                      out_specs=out_specs, scratch_shapes=scratch)
    return pl.pallas_call(kern, name="attn_bwd", grid_spec=gs, out_shape=out_shape,
                          compiler_params=_params(("arbitrary", "arbitrary")))(*operands)


def _bwd_inproj(dq_t, dk, dv, dd, dgl, ps, x, dh1, cc, sa, w, tm):
    n_tok = x.shape[0]
    n_tiles = n_tok // tm
    last_halo = n_tok // HALO - 1
    halo_spec = pl.BlockSpec((HALO, POOL_WIDTH), lambda i: (jnp.minimum((i + 1) * (tm // HALO), last_halo), 0))

    def body(i, tin, res, tout, acc, scr):
        dq_ref, dk_ref, dv_ref, dd_ref, ddh_ref, dgl_ref, ps_ref, x_ref, dh1_ref, c_ref, s_ref = tin
        w_uq, g_q, w_k, w_v, e_mat, g_kv, w_in, g_pre = res
        dqu_o, dproj_o, dx_o = tout
        dgq_a, dgkv_a, dgpre_a = acc
        ext, = scr
        cc_, sa_ = c_ref[...], s_ref[...]
        for h in range(N_HEADS):
            hs = slice(h * HEAD_SLOT, (h + 1) * HEAD_SLOT)
            dqu_o[:, hs] = (_unrope(dq_ref[hs, :].T, cc_, sa_) * SCALE).astype(dqu_o.dtype)
        gq = g_q[...]
        _, qh, rq = _rms(ps_ref[:, :Q_LORA].astype(F32), gq)
        dqd, dg = _rms_bwd(qh, rq, gq, _dot_nt(dqu_o[...], w_uq[...]))
        dgq_a[...] += dg
        dproj_o[:, :Q_LORA] = dqd.astype(dproj_o.dtype)
        gkv = g_kv[...]
        _, kh, rk = _rms(ps_ref[:, Q_LORA:Q_LORA + KV_LORA].astype(F32), gkv)
        dkvd, dg = _rms_bwd(kh, rk, gkv, _dot_nt(dk_ref[...], w_k[...]) + _dot_nt(dv_ref[...], w_v[...]))
        dgkv_a[...] += dg
        dproj_o[:, Q_LORA:Q_LORA + KV_LORA] = dkvd.astype(dproj_o.dtype)
        dproj_o[:, Q_LORA + KV_LORA:SMALL_COLS] = _unrope(_dot_nt(dk_ref[...], e_mat[...]), cc_, sa_).astype(dproj_o.dtype)
        row = i * tm + lax.broadcasted_iota(jnp.int32, (tm + HALO, 1), 0)
        for gi, wdw in enumerate(POOL_WINDOWS):
            cs = slice(gi * POOL_GROUP, (gi + 1) * POOL_GROUP)
            inv = 1.0 / jnp.minimum(row + 1, wdw).astype(F32)
            ext[pl.ds(0, tm), cs] = dd_ref[:, cs] * inv[:tm]
            ext[pl.ds(tm, HALO), cs] = jnp.where(i < n_tiles - 1, ddh_ref[:, cs] * inv[tm:], 0.0)
            s = ext[pl.ds(0, tm), cs]
            for k_ in range(1, wdw):
                s = s + ext[pl.ds(k_, tm), cs]
            dproj_o[:, SMALL_COLS + gi * POOL_GROUP:SMALL_COLS + (gi + 1) * POOL_GROUP] = (s - dd_ref[:, cs]).astype(dproj_o.dtype)
        dproj_o[:, SMALL_COLS + POOL_WIDTH:] = dgl_ref[...]
        da = _dot_nt(dproj_o[...], w_in[...])
        gp = g_pre[...]
        _, xh, rx = _rms(x_ref[...], gp)
        dx, dg = _rms_bwd(xh, rx, gp, da)
        dgpre_a[...] += dg
        dx_o[...] = dh1_ref[...] + dx

    outs = [_sds(n_tok, QK_WIDTH, MX), _sds(n_tok, IN_PAD, MX), _sds(n_tok, D_MODEL, F32)]
    accs = [_sds(1, Q_LORA, F32), _sds(1, KV_LORA, F32), _sds(1, D_MODEL, F32)]
    res = [w["w_uq"], w["g_q"], w["w_k"], w["w_v"], w["e_mat"], w["g_kv"], w["w_in"], w["g_pre_mix"]]
    dq_spec = pl.BlockSpec((QK_WIDTH, tm), lambda i: (0, i))
    return _tok_call("bwd_inproj", body, n_tok, tm, [(dq_t, dq_spec), dk, dv, dd, (dd, halo_spec), dgl, ps, x, dh1, cc, sa], res, outs, accs,
                     scratch=[pltpu.VMEM((tm + HALO, POOL_WIDTH), F32)])


XTDY_TOKENS = 1024
XTDY_OUT_BYTES = 8 * 2**20
XTDY_IN_BYTES = 8 * 2**20


def _xtdy(name, x, dy, allreduce=None, square_x=False):
    n_tok, kk = x.shape
    nn = dy.shape[1]
    bk = kk
    while bk * nn * 4 > XTDY_OUT_BYTES and bk % 256 == 0:
        bk //= 2
    bt = min(XTDY_TOKENS, n_tok)
    while (2 * bt <= n_tok and n_tok % (2 * bt) == 0 and 2 * bt * nn * dy.dtype.itemsize <= XTDY_IN_BYTES
           and 2 * bt * bk * x.dtype.itemsize <= XTDY_IN_BYTES):
        bt *= 2

    grid = (kk // bk, n_tok // bt)

    def kern(x_ref, dy_ref, *rest):
        o_ref = rest[1] if allreduce is not None else rest[0]
        if allreduce is not None:
            g_ref, _, sum_ref, buf, send_sems, recv_sems = rest
            step = pl.program_id(0) * grid[1] + pl.program_id(1)
            _allreduce_steps(step == 0, step == grid[0] * grid[1] - 1, g_ref, sum_ref, buf, send_sems, recv_sems)

        @pl.when(pl.program_id(1) == 0)
        def _():
            o_ref[...] = jnp.zeros(o_ref.shape, F32)
        xv = x_ref[...]
        if square_x:
            xv = xv.astype(F32)
            xv = xv * xv
        o_ref[...] += _dot_tn(xv, dy_ref[...])

    operands = [x, dy]
    in_specs = [pl.BlockSpec((bt, bk), lambda a, t: (t, a)), pl.BlockSpec((bt, nn), lambda a, t: (t, 0))]
    out_specs = [pl.BlockSpec((bk, nn), lambda a, t: (a, 0))]
    out_shape = [_sds(kk, nn, F32)]
    scratch = []
    if allreduce is not None:
        vmem = pl.BlockSpec(memory_space=pltpu.VMEM)
        operands.append(allreduce)
        in_specs.append(vmem)
        out_specs.append(vmem)
        out_shape.append(jax.ShapeDtypeStruct(allreduce.shape, allreduce.dtype))
        scratch = [pltpu.VMEM((N_DEVICES,) + allreduce.shape, allreduce.dtype), pltpu.SemaphoreType.DMA((N_DEVICES - 1,)),
                   pltpu.SemaphoreType.DMA((N_DEVICES - 1,))]
    res = pl.pallas_call(kern, name=name, grid=grid, in_specs=in_specs, out_specs=out_specs, out_shape=out_shape,
                         scratch_shapes=scratch, compiler_params=_params(("arbitrary", "arbitrary")))(*operands)
    return res if allreduce is not None else res[0]


def _rope_tables(positions):
    inv_freq = ROPE_THETA ** (-jnp.arange(0, QK_ROPE, 2, dtype=F32) / QK_ROPE)
    ang_t = inv_freq[:, None] * positions.astype(F32)[None, :]
    cos_t, sin_t = lax.optimization_barrier((jnp.cos(ang_t), jnp.sin(ang_t)))
    cos, sin = cos_t.T, sin_t.T
    n_tok = positions.shape[0]
    ones, z64 = jnp.ones((n_tok, ROPE_LANE), F32), jnp.zeros((n_tok, ROPE_LANE), F32)
    z32 = jnp.zeros((n_tok, HEAD_SLOT - ROPE_LANE - QK_ROPE), F32)
    return jnp.concatenate([ones, cos, cos, z32], 1), jnp.concatenate([z64, -sin, sin, z32], 1)


def _kernel_weights(full):
    w_in, w_uq, w_ukv = full["w_in"], full["w_uq"], full["w_ukv"]
    c0 = Q_LORA + KV_LORA
    z = lambda n: jnp.zeros((D_MODEL, n), w_in.dtype)
    w = dict(full)
    w["w_in"] = jnp.concatenate([w_in[:, :c0], z(ROPE_LANE), w_in[:, c0:c0 + QK_ROPE], z(HEAD_SLOT - ROPE_LANE - QK_ROPE),
                                 w_in[:, c0 + QK_ROPE:]], 1)
    w["w_uq"] = jnp.pad(w_uq.reshape(Q_LORA, N_HEADS, QK_NOPE + QK_ROPE),
                        ((0, 0), (0, 0), (0, HEAD_SLOT - QK_NOPE - QK_ROPE))).reshape(Q_LORA, QK_WIDTH)
    kv = w_ukv.reshape(KV_LORA, N_HEADS, QK_NOPE + V_HEAD)
    w["w_k"] = jnp.pad(kv[:, :, :QK_NOPE], ((0, 0), (0, 0), (0, HEAD_SLOT - QK_NOPE))).reshape(KV_LORA, QK_WIDTH)
    w["w_v"] = kv[:, :, QK_NOPE:].reshape(KV_LORA, N_HEADS * V_HEAD)
    e = np.zeros((HEAD_SLOT, QK_WIDTH), np.float32)
    sel = np.zeros((N_HEADS * V_HEAD, HEAD_SLOT), np.float32)
    for h in range(N_HEADS):
        for r in range(QK_ROPE):
            e[ROPE_LANE + r, h * HEAD_SLOT + ROPE_LANE + r] = 1.0
        sel[h * V_HEAD:(h + 1) * V_HEAD, h] = 1.0
    w["e_mat"] = jnp.asarray(e, MX)
    w["w_kt"], w["e_t"] = w["w_k"].T, jnp.asarray(e.T, MX)
    pad = ((0, 0), (0, V_ROWS - V_HEAD), (0, 0))
    w["w_vt"] = jnp.pad(w["w_v"].T.reshape(N_HEADS, V_HEAD, KV_LORA), pad).reshape(N_HEADS * V_ROWS, KV_LORA)
    ones = np.zeros((N_HEADS, V_ROWS, 1), np.float32)
    ones[:, V_HEAD] = 1.0
    w["v_ones"] = jnp.asarray(ones.reshape(N_HEADS * V_ROWS, 1))
    w["head_sel"] = jnp.asarray(sel, MX)
    w["w_pool"] = full["w_pool"].astype(MX)
    return w


def _local_step(x, p, positions, target, full, mesh_place=None, packed_rest=None):
    n_tok = x.shape[0]
    tm = tm_mlp = min(TOKEN_TILE, n_tok)
    fwd_tile = [min(t, n_tok) for t in ATTN_FWD_TILE]
    bwd_tile = [min(t, n_tok) for t in ATTN_BWD_TILE]
    w = _kernel_weights(full)
    cc, sa = _rope_tables(positions)

    if mesh_place is None:
        a, ps, u, gl, qn, kvn, q, k, v, kt, vt = _fwd_inproj(x, cc, sa, w, tm)
    else:
        my_chip, core = mesh_place
        a, ps, u, gl, qn, kvn, q, k, v, kt, vt, gathered = _fwd_inproj(x, cc, sa, w, tm, gather=packed_rest)
        w.update(_unpack_full(lax.dynamic_update_slice(gathered, packed_rest[None], (my_chip, 0, 0)), REST))
    attn, lse = _attn_fwd(q, k, vt, *fwd_tile)
    d, pooled, a_br, p_br, merged, y, h1 = _fwd_mix(x, u, gl, attn, w, tm)
    m, zr, f, h2 = _fwd_mlp(h1, w, tm_mlp)
    dh2, de, dzg, loss_cols, dg_ple = _ple_fwd_bwd(h2, p, target, w, tm)
    df, dz, dh1, dg_post_mlp, dg_pre_mlp = _bwd_mlp(dh2, f, h1, zr, w, tm_mlp)
    dy, da_br, dp_br, dgl, do, delta, dyp, dd, dg_post_mix, db_gate, dpool_scale = _bwd_mix(dh1, y, a_br, p_br, gl, attn, d, w, tm)
    grads = {"w_branch_attn": _xtdy("dw_ba", attn, da_br), "w_branch_pool": _xtdy("dw_bp", pooled, dp_br),
             "w_out": _xtdy("dw_out", merged, dy), "w_ff1": _xtdy("dw_ff1", m, dz), "w_ff2": _xtdy("dw_ff2", zr, df, square_x=True),
             "w_ple_proj": _xtdy("dw_pe", p, de), "w_ple_gate": _xtdy("dw_pg", h2, dzg)}
    delta_t = delta[:, :N_HEADS].T
    if mesh_place is None:
        travelling = None
        dq_t, dk, dv = _attn_bwd(q, k, kt, v, do, lse, delta_t, *bwd_tile)
    else:
        pieces = _pack_pieces(grads, REST, WIRE)
        dq_t, dk, dv, received = _attn_bwd(q, k, kt, v, do, lse, delta_t, *bwd_tile, scatter=pieces)
        travelling = (pieces, received)
        grads = {}
    dqu, dproj, dx, dg_q, dg_kv, dg_pre_mix = _bwd_inproj(dq_t, dk, dv, dd, dgl, ps, x, dh1, cc, sa, w, tm)

    g_uq = _xtdy("dw_uq", qn, dqu)
    g_k = _xtdy("dw_k", kvn, dk)
    g_v = _xtdy("dw_v", kvn, dv)
    g_pool = _xtdy("dw_pool", d, dyp)
    small = {"g_pre_mix": dg_pre_mix, "b_gate": db_gate, "g_q": dg_q, "g_kv": dg_kv, "pool_scale": dpool_scale,
             "g_post_mix": dg_post_mix, "g_pre_mlp": dg_pre_mlp, "g_post_mlp": dg_post_mlp, "g_ple": dg_ple,
             "w_pool": jnp.stack([g_pool[g * POOL_GROUP:(g + 1) * POOL_GROUP, g * POOL_GROUP:(g + 1) * POOL_GROUP]
                                  for g in range(len(POOL_WINDOWS))])}
    if mesh_place is None:
        g_in = _xtdy("dw_in", a, dproj)
    else:
        g_in, small_sum = _xtdy("dw_in", a, dproj, allreduce=_pack_small(small))
        small = _unpack_small(small_sum)

    c0 = Q_LORA + KV_LORA
    grads.update(small)
    grads.update({
        "w_in": jnp.concatenate([g_in[:, :c0], g_in[:, c0 + ROPE_LANE:c0 + ROPE_LANE + QK_ROPE], g_in[:, SMALL_COLS:]], 1),
        "w_uq": g_uq.reshape(Q_LORA, N_HEADS, HEAD_SLOT)[:, :, :QK_NOPE + QK_ROPE].reshape(Q_LORA, N_HEADS * (QK_NOPE + QK_ROPE)),
        "w_ukv": jnp.concatenate([g_k.reshape(KV_LORA, N_HEADS, HEAD_SLOT)[:, :, :QK_NOPE],
                                  g_v.reshape(KV_LORA, N_HEADS, V_HEAD)], 2).reshape(KV_LORA, N_HEADS * (QK_NOPE + V_HEAD)),
    })
    return loss_cols, dx, grads, travelling


def _place():
    return lax.axis_index("x"), lax.axis_index("y"), lax.axis_index("c")


CHIP_FLIPS = ((1, 0), (0, 1), (1, 1))


def _flip(x, y, fx, fy):
    return (1 - x if fx else x), (1 - y if fy else y)


_HBM = pl.BlockSpec(memory_space=pl.ANY)


def _gather_copies(w_ref, out_ref, send_sems, recv_sems):
    half = w_ref.shape[0] // 2
    x, y, c = _place()
    my_chip = 2 * x + y
    sibling = (x, y, 1 - c)

    def half_of(chip, hc):
        return out_ref.at[chip, pl.ds(pl.multiple_of(hc * half, 16), half), :]

    src = w_ref.at[pl.ds(pl.multiple_of(c * half, 16), half), :]
    sends, landed, forwards, from_sibling = [], [], [], []
    for j, (fx, fy) in enumerate(CHIP_FLIPS):
        px, py = _flip(x, y, fx, fy)
        mine_there, theirs_here, theirs_other = half_of(my_chip, c), half_of(2 * px + py, c), half_of(2 * px + py, 1 - c)
        sends.append(pltpu.make_async_remote_copy(src, mine_there, send_sems.at[j], recv_sems.at[j],
                                                  device_id=(px, py, c), device_id_type=MESH))
        landed.append(pltpu.make_async_remote_copy(src, theirs_here, send_sems.at[j], recv_sems.at[j],
                                                   device_id=(px, py, c), device_id_type=MESH))
        forwards.append(pltpu.make_async_remote_copy(theirs_here, theirs_here, send_sems.at[3 + j], recv_sems.at[3 + j],
                                                     device_id=sibling, device_id_type=MESH))
        from_sibling.append(pltpu.make_async_remote_copy(theirs_other, theirs_other, send_sems.at[3 + j],
                                                         recv_sems.at[3 + j], device_id=sibling, device_id_type=MESH))
    return sends, landed, forwards, from_sibling


def _gather_steps(i, n_steps, w_ref, out_ref, sems):
    sends, landed, forwards, from_sibling = _gather_copies(w_ref, out_ref, *sems)

    @pl.when(i == 0)
    def _():
        for cp in sends:
            cp.start()

    @pl.when(i == (3 * n_steps) // 4)
    def _():
        for arrived, fwd in zip(landed, forwards):
            arrived.wait_recv()
            fwd.start()

    @pl.when(i == n_steps - 1)
    def _():
        for cp in from_sibling:
            cp.wait_recv()
        for cp in sends + forwards:
            cp.wait_send()


def _allgather_shards(wp):
    def body(w_ref, out_ref, send_sems, recv_sems):
        sends, landed, forwards, from_sibling = _gather_copies(w_ref, out_ref, send_sems, recv_sems)
        for cp in sends:
            cp.start()
        for arrived, fwd in zip(landed, forwards):
            arrived.wait_recv()
            fwd.start()
        for cp in from_sibling:
            cp.wait_recv()
        for cp in sends + forwards:
            cp.wait_send()

    return pl.pallas_call(
        body, name="allgather_shards", out_shape=jax.ShapeDtypeStruct((N_CHIPS,) + wp.shape, wp.dtype),
        in_specs=[_HBM], out_specs=_HBM,
        scratch_shapes=[pltpu.SemaphoreType.DMA((6,)), pltpu.SemaphoreType.DMA((6,))],
    )(wp)


def _exchange_halves(g):
    rows = g.shape[1]
    half = rows // 2

    def body(g_ref, r_ref, send_sem, recv_sem):
        x, y, c = _place()
        src = g_ref.at[:, pl.ds(pl.multiple_of((1 - c) * half, 8), half), :]
        cp = pltpu.make_async_remote_copy(src, r_ref, send_sem, recv_sem, device_id=(x, y, 1 - c), device_id_type=MESH)
        cp.start()
        cp.wait()

    return pl.pallas_call(
        body, name="exchange_halves", out_shape=jax.ShapeDtypeStruct((N_CHIPS, half, PACK_COLS), g.dtype),
        in_specs=[_HBM], out_specs=_HBM, scratch_shapes=[pltpu.SemaphoreType.DMA, pltpu.SemaphoreType.DMA],
    )(g)


def _add_halves(g, r, c):
    rows = g.shape[1]
    half = rows // 2
    br = REDUCE_ROWS
    nb = half // br

    def kern(c_ref, g_ref, r_ref, o_ref):
        o_ref[...] = (g_ref[...] + r_ref[...]).astype(o_ref.dtype)

    gs = pltpu.PrefetchScalarGridSpec(
        num_scalar_prefetch=1, grid=(N_CHIPS, nb),
        in_specs=[pl.BlockSpec((1, br, PACK_COLS), lambda k, t, c: (k, c[0] * nb + t, 0)),
                  pl.BlockSpec((1, br, PACK_COLS), lambda k, t, c: (k, t, 0))],
        out_specs=pl.BlockSpec((1, br, PACK_COLS), lambda k, t, c: (k, t, 0)))
    return pl.pallas_call(kern, name="add_halves", grid_spec=gs,
                          out_shape=jax.ShapeDtypeStruct((N_CHIPS, half, PACK_COLS), WIRE),
                          compiler_params=_params(("arbitrary", "arbitrary")))(c.reshape(1), g, r)


def _scatter_copies(s_ref, r_ref, send_sems, recv_sems):
    x, y, c = _place()
    my_chip = 2 * x + y
    sends, arrivals = [], []
    for j, (fx, fy) in enumerate(CHIP_FLIPS):
        px, py = _flip(x, y, fx, fy)
        slot = r_ref.at[2 * px + py]
        sends.append(pltpu.make_async_remote_copy(s_ref.at[2 * px + py], r_ref.at[my_chip], send_sems.at[j], recv_sems.at[j],
                                                  device_id=(px, py, c), device_id_type=MESH))
        arrivals.append(pltpu.make_async_remote_copy(slot, slot, send_sems.at[j], recv_sems.at[j],
                                                     device_id=(px, py, c), device_id_type=MESH))
    return sends, arrivals


N_DEVICES = 8


def _peer(x, y, c, f):
    px, py = _flip(x, y, f & 4, f & 2)
    return px, py, (1 - c if f & 1 else c)


def _scatter_all_copies(p_ref, r_ref, send_sems, recv_sems):
    half = p_ref.shape[1] // 2
    x, y, c = _place()
    me = 4 * x + 2 * y + c
    sends, arrivals = [], []
    for f in range(1, N_DEVICES):
        px, py, pc = _peer(x, y, c, f)
        theirs = p_ref.at[2 * px + py, pl.ds(pl.multiple_of(pc * half, 16), half), :]
        slot = r_ref.at[4 * px + 2 * py + pc]
        sends.append(pltpu.make_async_remote_copy(theirs, r_ref.at[me], send_sems.at[f - 1], recv_sems.at[f - 1],
                                                  device_id=(px, py, pc), device_id_type=MESH))
        arrivals.append(pltpu.make_async_remote_copy(slot, slot, send_sems.at[f - 1], recv_sems.at[f - 1],
                                                     device_id=(px, py, pc), device_id_type=MESH))
    return sends, arrivals


def _scatter_steps(first, last, p_ref, r_ref, sems):
    sends, arrivals = _scatter_all_copies(p_ref, r_ref, *sems)

    @pl.when(first)
    def _():
        for cp in sends:
            cp.start()

    @pl.when(last)
    def _():
        for cp in arrivals:
            cp.wait_recv()
        for cp in sends:
            cp.wait_send()


def _scatter_pieces(s):
    def body(s_ref, r_ref, send_sems, recv_sems):
        sends, arrivals = _scatter_copies(s_ref, r_ref, send_sems, recv_sems)
        for cp in sends:
            cp.start()
        for cp in arrivals:
            cp.wait_recv()
        for cp in sends:
            cp.wait_send()

    return pl.pallas_call(
        body, name="scatter_pieces", out_shape=jax.ShapeDtypeStruct(s.shape, s.dtype), in_specs=[_HBM], out_specs=_HBM,
        scratch_shapes=[pltpu.SemaphoreType.DMA((3,)), pltpu.SemaphoreType.DMA((3,))],
    )(s)


def _sum_pieces(r):
    slots, half = r.shape[:2]
    br = REDUCE_ROWS

    def kern(r_ref, o_ref):
        total = r_ref[0].astype(F32)
        for k in range(1, slots):
            total = total + r_ref[k].astype(F32)
        o_ref[...] = total

    return pl.pallas_call(
        kern, name="sum_pieces", grid=(half // br,), in_specs=[pl.BlockSpec((slots, br, PACK_COLS), lambda t: (0, t, 0))],
        out_specs=pl.BlockSpec((br, PACK_COLS), lambda t: (t, 0)), out_shape=_sds(half, PACK_COLS, F32),
        compiler_params=_params(("arbitrary",)))(r)


def _join_halves(f):
    def body(f_ref, o_ref, send_sem, recv_sem):
        x, y, c = _place()
        cp = pltpu.make_async_remote_copy(f_ref, o_ref, send_sem, recv_sem, device_id=(x, y, 1 - c), device_id_type=MESH)
        cp.start()
        cp.wait()

    return pl.pallas_call(
        body, name="join_halves", out_shape=jax.ShapeDtypeStruct(f.shape, f.dtype), in_specs=[_HBM], out_specs=_HBM,
        scratch_shapes=[pltpu.SemaphoreType.DMA, pltpu.SemaphoreType.DMA],
    )(f)


def _allreduce_steps(first, last, g_ref, o_ref, buf, send_sems, recv_sems):
    x, y, c = _place()
    me = 4 * x + 2 * y + c
    sends, arrivals = [], []
    for f in range(1, N_DEVICES):
        px, py, pc = _peer(x, y, c, f)
        slot = buf.at[4 * px + 2 * py + pc]
        sends.append(pltpu.make_async_remote_copy(g_ref, buf.at[me], send_sems.at[f - 1], recv_sems.at[f - 1],
                                                  device_id=(px, py, pc), device_id_type=MESH))
        arrivals.append(pltpu.make_async_remote_copy(slot, slot, send_sems.at[f - 1], recv_sems.at[f - 1],
                                                     device_id=(px, py, pc), device_id_type=MESH))

    @pl.when(first)
    def _():
        buf[me] = g_ref[...]
        for cp in sends:
            cp.start()

    @pl.when(last)
    def _():
        for cp in arrivals:
            cp.wait_recv()
        for cp in sends:
            cp.wait_send()
        total = buf[0]
        for k in range(1, N_DEVICES):
            total = total + buf[k]
        o_ref[...] = total


def _adamw_update(g_ref, w_ref, m_ref, v_ref, d_o, m_o, v_o):
    c1 = 1.0 - ADAM_B1 ** ADAM_STEP
    c2 = 1.0 - ADAM_B2 ** ADAM_STEP
    g_ = g_ref[...]
    m_new = ADAM_B1 * m_ref[...] + (1.0 - ADAM_B1) * g_
    v_new = ADAM_B2 * v_ref[...] + (1.0 - ADAM_B2) * (g_ * g_)
    m_o[...] = m_new
    v_o[...] = v_new
    d_o[...] = -ADAM_LR * ((m_new / c1) / (jnp.sqrt(v_new / c2) + ADAM_EPS) + ADAM_WD * w_ref[...])


ADAMW_ROWS = 256


def _adamw(name, g, w, m, v):
    _, rows, cols = w.shape
    br = int(np.gcd(ADAMW_ROWS, rows))

    def kern(*refs):
        _adamw_update(*refs)

    spec = pl.BlockSpec((1, br, cols), lambda t: (0, t, 0))
    out = jax.ShapeDtypeStruct(w.shape, F32)
    return pl.pallas_call(kern, name="adamw_" + name, grid=(rows // br,), in_specs=[spec] * 4, out_specs=[spec] * 3,
                          out_shape=[out, out, out], compiler_params=_params(("arbitrary",)))(g, w, m, v)


def _adamw_small(gs, ws, ms, vs):
    n = len(gs)

    def kern(*refs):
        ins, outs = refs[:4 * n], refs[4 * n:]
        for k in range(n):
            _adamw_update(ins[k], ins[n + k], ins[2 * n + k], ins[3 * n + k], outs[k], outs[n + k], outs[2 * n + k])

    vmem = pl.BlockSpec(memory_space=pltpu.VMEM)
    out = [jax.ShapeDtypeStruct(w.shape, F32) for w in ws]
    res = pl.pallas_call(kern, name="adamw_small", in_specs=[vmem] * (4 * n), out_specs=[vmem] * (3 * n),
                         out_shape=out * 3, compiler_params=pltpu.CompilerParams(vmem_limit_bytes=VMEM_LIMIT))(*gs, *ws, *ms, *vs)
    return [(res[k], res[n + k], res[2 * n + k]) for k in range(n)]


def _shard_rows(shape, axis):
    k, n = shape
    return (k * n // N_CHIPS) // PACK_COLS


def _group(names):
    entries = [e for e in SHARDED if e[0] in names]
    used = sum(_shard_rows(shape, axis) for _, shape, axis in entries)
    return entries, -(-used // (2 * REDUCE_ROWS)) * 2 * REDUCE_ROWS


def _pack_shards(shards, names, dtype):
    entries, rows = _group(names)
    parts = [shards[name].astype(dtype).reshape(-1, PACK_COLS) for name, _, _ in entries]
    used = sum(p.shape[0] for p in parts)
    if rows > used:
        parts.append(jnp.zeros((rows - used, PACK_COLS), dtype))
    return jnp.concatenate(parts, 0)


def _unpack_shards(packed, names):
    out, r0 = {}, 0
    for name, (k, n), axis in _group(names)[0]:
        nr = _shard_rows((k, n), axis)
        shape = (k // N_CHIPS, n) if axis == 0 else (k, n // N_CHIPS)
        out[name] = packed[r0:r0 + nr].reshape(shape)
        r0 += nr
    return out


def _unpack_full(gathered, names):
    out, r0 = {}, 0
    for name, (k, n), axis in _group(names)[0]:
        nr = _shard_rows((k, n), axis)
        part = gathered[:, r0:r0 + nr]
        if axis == 0:
            out[name] = part.reshape(k, n)
        else:
            out[name] = part.reshape(N_CHIPS, k, n // N_CHIPS).transpose(1, 0, 2).reshape(k, n)
        r0 += nr
    return out


def _pack_pieces(grads, names, dtype=F32):
    entries, rows = _group(names)
    parts = []
    for name, (k, n), axis in entries:
        g = grads[name].astype(dtype)
        if axis == 0:
            parts.append(g.reshape(N_CHIPS, -1, PACK_COLS))
        else:
            parts.append(g.reshape(k, N_CHIPS, n // N_CHIPS).transpose(1, 0, 2).reshape(N_CHIPS, -1, PACK_COLS))
    used = sum(p.shape[1] for p in parts)
    if rows > used:
        parts.append(jnp.zeros((N_CHIPS, rows - used, PACK_COLS), dtype))
    return jnp.concatenate(parts, 1)


def _pack_small(vals):
    flat = jnp.concatenate([vals[name].astype(F32).reshape(-1) for name, _ in SMALL])
    flat = jnp.concatenate([flat, jnp.zeros((SMALL_ROWS * PACK_COLS - flat.shape[0],), F32)])
    return flat.reshape(SMALL_ROWS, PACK_COLS)


def _unpack_small(packed):
    flat, out, o = packed.reshape(-1), {}, 0
    for name, shape in SMALL:
        n = int(np.prod(shape))
        out[name] = flat[o:o + n].reshape(shape)
        o += n
    return out


def kernel(x, p, positions, g_pre_mix, w_in, b_gate, g_q, w_uq, g_kv, w_ukv, w_pool, pool_scale, w_branch_attn, w_branch_pool, w_out, g_post_mix, g_pre_mlp, w_ff1, w_ff2, g_post_mlp, w_ple_proj, w_ple_gate, g_ple, loss_target, m_g_pre_mix, m_w_in, m_b_gate, m_g_q, m_w_uq, m_g_kv, m_w_ukv, m_w_pool, m_pool_scale, m_w_branch_attn, m_w_branch_pool, m_w_out, m_g_post_mix, m_g_pre_mlp, m_w_ff1, m_w_ff2, m_g_post_mlp, m_w_ple_proj, m_w_ple_gate, m_g_ple, v_g_pre_mix, v_w_in, v_b_gate, v_g_q, v_w_uq, v_g_kv, v_w_ukv, v_w_pool, v_pool_scale, v_w_branch_attn, v_w_branch_pool, v_w_out, v_g_post_mix, v_g_pre_mlp, v_w_ff1, v_w_ff2, v_g_post_mlp, v_w_ple_proj, v_w_ple_gate, v_g_ple):
    given = dict(locals())
    weights = {n: given[n] for n in WEIGHT_ORDER}
    moments_m = {n: given["m_" + n] for n in WEIGHT_ORDER}
    moments_v = {n: given["v_" + n] for n in WEIGHT_ORDER}
    c = lax.axis_index("c")

    big_w = {name: weights[name][0] for name, _, _ in SHARDED}
    my_chip = 2 * lax.axis_index("x") + lax.axis_index("y")
    packed_first = _pack_shards(big_w, FIRST, MX)
    full = _unpack_full(lax.dynamic_update_slice(_allgather_shards(packed_first), packed_first[None], (my_chip, 0, 0)), FIRST)
    for name, _ in SMALL:
        full[name] = weights[name][0] if name == "w_pool" else weights[name]

    loss_cols, dx, grads, (pieces_rest, received_rest) = _local_step(
        x[0], p[0, 0], positions[0], loss_target[0], full, (my_chip, c), _pack_shards(big_w, REST, MX))
    loss = lax.psum(0.5 * jnp.sum(loss_cols) / D_MODEL, ("x", "y", "c"))

    def finish(received, mine, slot):
        reduced = _sum_pieces(lax.dynamic_update_slice(received, mine, (slot, 0, 0)))
        theirs = _join_halves(reduced)
        return jnp.where(c == 0, jnp.concatenate([reduced, theirs]), jnp.concatenate([theirs, reduced]))

    pieces = _pack_pieces(grads, FIRST)
    sent = _add_halves(pieces, _exchange_halves(pieces), c)
    mine = lax.dynamic_slice(sent, (my_chip, 0, 0), (1,) + sent.shape[1:])
    shards = _unpack_shards(finish(_scatter_pieces(sent), mine, my_chip), FIRST)
    half = received_rest.shape[1]
    mine = lax.dynamic_slice(pieces_rest, (my_chip, c * half, 0), (1, half, PACK_COLS))
    shards.update(_unpack_shards(finish(received_rest, mine, 2 * my_chip + c), REST))

    out = {}
    for name, g in shards.items():
        out[name] = (g[None], *_adamw(name, g[None], weights[name], moments_m[name], moments_v[name]))
    small_g = {n: grads[n] for n, _ in SMALL}
    names = [n for n, _ in SMALL]
    updates = _adamw_small([small_g[n] for n in names], [weights[n] for n in names], [moments_m[n] for n in names],
                           [moments_v[n] for n in names])
    for n, upd in zip(names, updates):
        out[n] = (small_g[n], *upd)
    return (loss, dx[None], *[out[n][k] for k in range(4) for n in WEIGHT_ORDER])
```

```python
import numpy as np
import jax
import jax.numpy as jnp
from jax import lax
from jax.experimental import pallas as pl
from jax.experimental.pallas import tpu as pltpu

F32 = jnp.float32
MX = jnp.bfloat16
WIRE = jnp.bfloat16

D_MODEL = 1024
N_HEADS = 8
QK_NOPE = 64
QK_ROPE = 32
V_HEAD = 64
Q_LORA = 384
KV_LORA = 256
POOL_WINDOWS = (2, 4, 8, 16)
POOL_GROUP = 128
POOL_WIDTH = 512
D_FF = 4096
ROPE_THETA = 10000.0
EPS = 1e-6
HEAD_SLOT = 128
QK_WIDTH = N_HEADS * HEAD_SLOT
ROPE_LANE = 64
SMALL_COLS = Q_LORA + KV_LORA + HEAD_SLOT
IN_PAD = SMALL_COLS + POOL_WIDTH + 2 * D_MODEL
SCALE = (QK_NOPE + QK_ROPE) ** -0.5
LOG2E = 1.4426950408889634
NEG = -1e30
HALO = 16

ADAM_LR = 0.001
ADAM_B1 = 0.9
ADAM_B2 = 0.999
ADAM_EPS = 1e-08
ADAM_WD = 0.01
ADAM_STEP = 10

VMEM_LIMIT = 56 * 2**20
TOKEN_TILE = 512
MESH = pl.DeviceIdType.MESH

SHARDED = (
    ("w_in", (1024, 3232), 1),
    ("w_uq", (384, 768), 1),
    ("w_ukv", (256, 1024), 1),
    ("w_branch_attn", (512, 1024), 1),
    ("w_branch_pool", (512, 1024), 1),
    ("w_out", (1024, 1024), 0),
    ("w_ff1", (1024, 4096), 1),
    ("w_ff2", (4096, 1024), 0),
    ("w_ple_proj", (256, 1024), 1),
    ("w_ple_gate", (1024, 1024), 0),
)
SMALL = (
    ("g_pre_mix", (1, 1024)),
    ("b_gate", (1, 2048)),
    ("g_q", (1, 384)),
    ("g_kv", (1, 256)),
    ("w_pool", (1, 4, 128, 128)),
    ("pool_scale", (1, 512)),
    ("g_post_mix", (1, 1024)),
    ("g_pre_mlp", (1, 1024)),
    ("g_post_mlp", (1, 1024)),
    ("g_ple", (1, 1024)),
)
WEIGHT_ORDER = ("g_pre_mix", "w_in", "b_gate", "g_q", "w_uq", "g_kv", "w_ukv", "w_pool", "pool_scale", "w_branch_attn",
                "w_branch_pool", "w_out", "g_post_mix", "g_pre_mlp", "w_ff1", "w_ff2", "g_post_mlp", "w_ple_proj",
                "w_ple_gate", "g_ple")
N_CHIPS = 4
PACK_COLS = 1024
REDUCE_ROWS = 160
SMALL_ROWS = 80
FIRST = ("w_in", "w_uq", "w_ukv")
REST = tuple(name for name, _, _ in SHARDED if name not in FIRST)


def _dot(a, b):
    return jnp.dot(a.astype(MX), b.astype(MX), preferred_element_type=F32)


def _dot_nt(a, b):
    return lax.dot_general(a.astype(MX), b.astype(MX), (((1,), (1,)), ((), ())), preferred_element_type=F32)


def _dot_tn(a, b):
    return lax.dot_general(a.astype(MX), b.astype(MX), (((0,), (0,)), ((), ())), preferred_element_type=F32)


def _sig(x):
    return 1.0 / (1.0 + jnp.exp(-x))


def _rms(x, g):
    r = lax.rsqrt(jnp.mean(x * x, axis=1, keepdims=True) + EPS)
    xh = x * r
    return xh * g, xh, r


def _rms_bwd(xh, r, g, dy):
    dxn = dy * g
    dx = r * (dxn - xh * jnp.mean(dxn * xh, axis=1, keepdims=True))
    return dx, jnp.sum(dy * xh, axis=0, keepdims=True)


def _rot_half(v):
    lane = lax.broadcasted_iota(jnp.int32, v.shape, 1)
    return jnp.where(lane < ROPE_LANE + QK_ROPE // 2, pltpu.roll(v, HEAD_SLOT - QK_ROPE // 2, 1), pltpu.roll(v, QK_ROPE // 2, 1))


def _rope(v, cc, sa):
    return v * cc + _rot_half(v) * sa


def _unrope(v, cc, sa):
    return v * cc - _rot_half(v) * sa


def _params(sem):
    return pltpu.CompilerParams(dimension_semantics=sem, vmem_limit_bytes=VMEM_LIMIT)


def _tok_call(name, body, n_tok, tm, tiled, resident, outs, accs=(), scratch=(), exchange=None):
    def as_pair(t):
        if isinstance(t, tuple):
            return t
        return t, pl.BlockSpec((tm, t.shape[1]), lambda i: (i, 0))
    tiled = [as_pair(t) for t in tiled]
    outs = [as_pair(o) for o in outs]
    res_specs = [pl.BlockSpec(r.shape, lambda i, nd=r.ndim: (0,) * nd, pipeline_mode=pl.Buffered(1)) for r in resident]
    out_specs = [s for _, s in outs] + [pl.BlockSpec(a.shape, lambda i: (0, 0)) for a in accs]
    n_t, n_r, n_o, n_a, n_s = len(tiled), len(resident), len(outs), len(accs), len(scratch)
    n_steps = n_tok // tm
    operands = [a for a, _ in tiled] + list(resident)
    in_specs = [s for _, s in tiled] + res_specs
    out_shape = [o for o, _ in outs] + list(accs)
    scratch = list(scratch)
    if exchange is not None:
        ex_in, ex_out, ex_sems, ex_steps = exchange
        operands.append(ex_in)
        in_specs.append(_HBM)
        out_shape.append(ex_out)
        out_specs.append(_HBM)
        scratch += list(ex_sems)

    def kern(*refs):
        refs = list(refs)
        n_in = n_t + n_r + (exchange is not None)
        n_out = n_o + n_a + (exchange is not None)
        tin, res = refs[:n_t], refs[n_t:n_t + n_r]
        tout = refs[n_in:n_in + n_o]
        acc = refs[n_in + n_o:n_in + n_o + n_a]
        scr = refs[n_in + n_out:n_in + n_out + n_s]
        i = pl.program_id(0)
        if exchange is not None:
            ex_steps(i, n_steps, refs[n_in - 1], refs[n_in + n_out - 1], refs[n_in + n_out + n_s:])

        @pl.when(i == 0)
        def _():
            for a in acc:
                a[...] = jnp.zeros(a.shape, a.dtype)
        body(i, tin, res, tout, acc, scr)

    return pl.pallas_call(
        kern, name=name, grid=(n_steps,), in_specs=in_specs, out_specs=out_specs,
        out_shape=out_shape, scratch_shapes=scratch, compiler_params=_params(("arbitrary",)),
    )(*operands)


def _sds(rows, cols, dtype):
    return jax.ShapeDtypeStruct((rows, cols), dtype)


def _fwd_inproj(x, cc, sa, w, tm, gather=None):
    n_tok = x.shape[0]

    def body(i, tin, res, tout, acc, scr):
        x_ref, c_ref, s_ref = tin
        g_pre, w_in, g_q, w_uq, g_kv, w_k, w_v, e_mat, w_kt, e_t, w_vt, v_ones = res
        a_o, ps_o, u_o, gl_o, qn_o, kvn_o, q_o, k_o, v_o, kt_o, vt_o = tout
        a = _rms(x_ref[...], g_pre[...])[0].astype(MX)
        a_o[...] = a
        ps = _dot(a, w_in[:, :SMALL_COLS])
        ps_o[...] = ps.astype(ps_o.dtype)
        u_o[...] = _dot(a, w_in[:, SMALL_COLS:SMALL_COLS + POOL_WIDTH]).astype(u_o.dtype)
        gl_o[...] = _dot(a, w_in[:, SMALL_COLS + POOL_WIDTH:]).astype(gl_o.dtype)
        cc_, sa_ = c_ref[...], s_ref[...]
        qn = _rms(ps[:, :Q_LORA], g_q[...])[0].astype(MX)
        qn_o[...] = qn
        q = _dot(qn, w_uq[...])
        for h in range(N_HEADS):
            hs = slice(h * HEAD_SLOT, (h + 1) * HEAD_SLOT)
            q_o[:, hs] = (_rope(q[:, hs], cc_, sa_) * (SCALE * LOG2E)).astype(q_o.dtype)
        kvn = _rms(ps[:, Q_LORA:Q_LORA + KV_LORA], g_kv[...])[0].astype(MX)
        kvn_o[...] = kvn
        kr = _rope(ps[:, Q_LORA + KV_LORA:], cc_, sa_)
        k_o[...] = (_dot(kvn, w_k[...]) + _dot(kr, e_mat[...])).astype(k_o.dtype)
        v_o[...] = _dot(kvn, w_v[...]).astype(v_o.dtype)
        kt_o[...] = (_dot_nt(w_kt[...], kvn) + _dot_nt(e_t[...], kr)).astype(kt_o.dtype)
        vt_o[...] = (_dot_nt(w_vt[...], kvn) + v_ones[...]).astype(vt_o.dtype)

    outs = [_sds(n_tok, D_MODEL, MX), _sds(n_tok, SMALL_COLS, MX), _sds(n_tok, POOL_WIDTH, MX), _sds(n_tok, 2 * D_MODEL, MX),
            _sds(n_tok, Q_LORA, MX), _sds(n_tok, KV_LORA, MX), _sds(n_tok, QK_WIDTH, MX), _sds(n_tok, QK_WIDTH, MX),
            _sds(n_tok, N_HEADS * V_HEAD, MX),
            (_sds(QK_WIDTH, n_tok, MX), pl.BlockSpec((QK_WIDTH, tm), lambda i: (0, i))),
            (_sds(N_HEADS * V_ROWS, n_tok, MX), pl.BlockSpec((N_HEADS * V_ROWS, tm), lambda i: (0, i)))]
    res = [w["g_pre_mix"], w["w_in"], w["g_q"], w["w_uq"], w["g_kv"], w["w_k"], w["w_v"], w["e_mat"], w["w_kt"], w["e_t"],
           w["w_vt"], w["v_ones"]]
    exchange = None
    if gather is not None:
        gathered = jax.ShapeDtypeStruct((N_CHIPS,) + gather.shape, gather.dtype)
        exchange = (gather, gathered, [pltpu.SemaphoreType.DMA((6,)), pltpu.SemaphoreType.DMA((6,))], _gather_steps)
    return _tok_call("fwd_inproj", body, n_tok, tm, [x, cc, sa], res, outs, exchange=exchange)


def _causal_pairs(nq, ratio, by_kv):
    if by_kv:
        pairs = [(i, j) for j in range(nq * ratio) for i in range(j // ratio, nq)]
    else:
        pairs = [(i, j) for i in range(nq) for j in range((i + 1) * ratio)]
    return (jnp.asarray(np.array([p[0] for p in pairs], np.int32)), jnp.asarray(np.array([p[1] for p in pairs], np.int32)))


def _keep_t(tk, tq, off):
    return lax.broadcasted_iota(jnp.int32, (tk, tq), 0) + off <= lax.broadcasted_iota(jnp.int32, (tk, tq), 1)


ATTN_FWD_TILE = (1024, 1024)
ATTN_BWD_TILE = (1024, 512)
V_ROWS = 80


def _attn_fwd(q, k, vt, tq, tk):
    n_tok = q.shape[0]
    nq, ratio = n_tok // tq, tq // tk
    qi, kj = _causal_pairs(nq, ratio, by_kv=False)

    def kern(qi_ref, kj_ref, q_ref, k_ref, vt_ref, o_ref, lse_ref, m_s, acc_s, st_s):
        s_id = pl.program_id(0)
        i, j = qi_ref[s_id], kj_ref[s_id]

        @pl.when(j == 0)
        def _():
            m_s[...] = jnp.full(m_s.shape, NEG, F32)
            acc_s[...] = jnp.zeros(acc_s.shape, F32)

        def scores(h):
            hs = slice(h * HEAD_SLOT, (h + 1) * HEAD_SLOT)
            return _dot_nt(k_ref[:, hs], q_ref[:, hs])

        def heads(masked):
            keep = _keep_t(tk, tq, j * tk - i * tq) if masked else None
            st_s[0] = scores(0)
            for h in range(N_HEADS):
                if h + 1 < N_HEADS:
                    st_s[(h + 1) % 2] = scores(h + 1)
                st = st_s[h % 2]
                if masked:
                    st = jnp.where(keep, st, NEG)
                m_old = m_s[h]
                m_new = jnp.maximum(m_old, jnp.max(st, axis=0, keepdims=True))
                pt = jnp.exp2(st - m_new)
                acc_s[h] = jnp.exp2(m_old - m_new) * acc_s[h] + _dot(vt_ref[h * V_ROWS:(h + 1) * V_ROWS, :], pt)
                m_s[h] = m_new

        @pl.when(j < i * ratio)
        def _():
            heads(False)

        @pl.when(j >= i * ratio)
        def _():
            heads(True)

        @pl.when(j == (i + 1) * ratio - 1)
        def _():
            heads_out = []
            for h in range(N_HEADS):
                total = acc_s[h, V_HEAD:V_HEAD + 1, :]
                heads_out.append(acc_s[h, :V_HEAD, :] / total)
                lse_ref[h:h + 1, :] = m_s[h] + jnp.log2(total)
            o_ref[...] = jnp.concatenate(heads_out, 0).T.astype(o_ref.dtype)

    gs = pltpu.PrefetchScalarGridSpec(
        num_scalar_prefetch=2, grid=(qi.shape[0],),
        in_specs=[pl.BlockSpec((tq, QK_WIDTH), lambda s, qi, kj: (qi[s], 0)),
                  pl.BlockSpec((tk, QK_WIDTH), lambda s, qi, kj: (kj[s], 0)),
                  pl.BlockSpec((N_HEADS * V_ROWS, tk), lambda s, qi, kj: (0, kj[s]))],
        out_specs=[pl.BlockSpec((tq, N_HEADS * V_HEAD), lambda s, qi, kj: (qi[s], 0)),
                   pl.BlockSpec((N_HEADS, tq), lambda s, qi, kj: (0, qi[s]))],
        scratch_shapes=[pltpu.VMEM((N_HEADS, 1, tq), F32), pltpu.VMEM((N_HEADS, V_ROWS, tq), F32),
                        pltpu.VMEM((2, tk, tq), F32)])
    return pl.pallas_call(kern, name="attn_fwd", grid_spec=gs,
                          out_shape=[_sds(n_tok, N_HEADS * V_HEAD, MX), _sds(N_HEADS, n_tok, F32)],
                          compiler_params=_params(("arbitrary",)))(qi, kj, q, k, vt)


def _pool_windows(ext, i, tm, first_row):
    row = i * tm + lax.broadcasted_iota(jnp.int32, (tm, 1), 0)
    out = []
    for g, w in enumerate(POOL_WINDOWS):
        cs = slice(g * POOL_GROUP, (g + 1) * POOL_GROUP)
        s = ext[pl.ds(first_row, tm), cs]
        for k in range(1, w):
            s = s + ext[pl.ds(first_row - k, tm), cs]
        cnt = jnp.minimum(row + 1, w).astype(F32)
        out.append(s / cnt)
    return out


def _fwd_mix(x, u, gl, attn, w, tm):
    n_tok = x.shape[0]
    halo_spec = pl.BlockSpec((HALO, POOL_WIDTH), lambda i: (jnp.maximum(i * (tm // HALO) - 1, 0), 0))

    def body(i, tin, res, tout, acc, scr):
        x_ref, u_ref, uh_ref, gl_ref, at_ref = tin
        w_pool, pool_scale, w_ba, w_bp, b_gate, w_out, g_post = res
        d_o, pooled_o, a_o, pp_o, merged_o, y_o, h1_o = tout
        ext, = scr
        ext[pl.ds(0, HALO), :] = jnp.where(i > 0, uh_ref[...].astype(F32), 0.0)
        ext[pl.ds(HALO, tm), :] = u_ref[...].astype(F32)
        means = _pool_windows(ext, i, tm, HALO)
        for g in range(len(POOL_WINDOWS)):
            cs = slice(g * POOL_GROUP, (g + 1) * POOL_GROUP)
            d = (means[g] - ext[pl.ds(HALO, tm), cs]).astype(MX)
            d_o[:, cs] = d
            pooled_o[:, cs] = (_dot(d, w_pool[g]) * pool_scale[:, cs]).astype(pooled_o.dtype)
        a_br = _dot(at_ref[...], w_ba[...])
        p_br = _dot(pooled_o[...], w_bp[...])
        a_o[...] = a_br.astype(a_o.dtype)
        pp_o[...] = p_br.astype(pp_o.dtype)
        gates = _sig(gl_ref[...].astype(F32) + b_gate[...])
        merged = (gates[:, :D_MODEL] * a_br + gates[:, D_MODEL:] * p_br).astype(MX)
        merged_o[...] = merged
        y = _dot(merged, w_out[...])
        y_o[...] = y.astype(y_o.dtype)
        h1_o[...] = x_ref[...] + _rms(y, g_post[...])[0]

    outs = [_sds(n_tok, POOL_WIDTH, MX), _sds(n_tok, POOL_WIDTH, MX), _sds(n_tok, D_MODEL, MX), _sds(n_tok, D_MODEL, MX),
            _sds(n_tok, D_MODEL, MX), _sds(n_tok, D_MODEL, MX), _sds(n_tok, D_MODEL, F32)]
    res = [w["w_pool"], w["pool_scale"], w["w_branch_attn"], w["w_branch_pool"], w["b_gate"], w["w_out"], w["g_post_mix"]]
    return _tok_call("fwd_mix", body, n_tok, tm, [x, u, (u, halo_spec), gl, attn], res, outs,
                     scratch=[pltpu.VMEM((tm + HALO, POOL_WIDTH), F32)])


def _fwd_mlp(h1, w, tm):
    n_tok = h1.shape[0]

    def body(i, tin, res, tout, acc, scr):
        h1_ref, = tin
        g_pre, w1, w2, g_post = res
        m_o, zr_o, f_o, h2_o = tout
        h1_ = h1_ref[...]
        m = _rms(h1_, g_pre[...])[0].astype(MX)
        m_o[...] = m
        zr = jnp.maximum(_dot(m, w1[...]), 0.0)
        zr_o[...] = zr.astype(zr_o.dtype)
        a2 = (zr * zr).astype(MX)
        f = _dot(a2, w2[...])
        f_o[...] = f.astype(f_o.dtype)
        h2_o[...] = h1_ + _rms(f, g_post[...])[0]

    outs = [_sds(n_tok, D_MODEL, MX), _sds(n_tok, D_FF, MX), _sds(n_tok, D_MODEL, MX),
            _sds(n_tok, D_MODEL, F32)]
    res = [w["g_pre_mlp"], w["w_ff1"], w["w_ff2"], w["g_post_mlp"]]
    return _tok_call("fwd_mlp", body, n_tok, tm, [h1], res, outs)


def _ple_fwd_bwd(h2, p, target, w, tm):
    n_tok = h2.shape[0]

    def body(i, tin, res, tout, acc, scr):
        h2_ref, p_ref, t_ref = tin
        w_pe, w_pg, g_ple = res
        dh2_o, de_o, dzg_o = tout
        loss_a, dg_a = acc
        h2_ = h2_ref[...]
        e = _dot(p_ref[...], w_pe[...])
        pg = _sig(_dot(h2_, w_pg[...]))
        t = pg * e
        g = g_ple[...]
        tn, th, r = _rms(t, g)
        diff = h2_ + tn - t_ref[...]
        loss_a[...] += jnp.sum(diff * diff, axis=0, keepdims=True)
        dh3 = diff * (1.0 / D_MODEL)
        dt, dg = _rms_bwd(th, r, g, dh3)
        dg_a[...] += dg
        de_o[...] = (dt * pg).astype(de_o.dtype)
        dzg = (dt * e * pg * (1.0 - pg)).astype(MX)
        dzg_o[...] = dzg
        dh2_o[...] = dh3 + _dot_nt(dzg, w_pg[...])

    outs = [_sds(n_tok, D_MODEL, F32), _sds(n_tok, D_MODEL, MX), _sds(n_tok, D_MODEL, MX)]
    accs = [_sds(1, D_MODEL, F32), _sds(1, D_MODEL, F32)]
    return _tok_call("ple_fwd_bwd", body, n_tok, tm, [h2, p, target], [w["w_ple_proj"], w["w_ple_gate"], w["g_ple"]], outs, accs)


def _bwd_mlp(dh2, f, h1, zr, w, tm):
    n_tok = dh2.shape[0]

    def body(i, tin, res, tout, acc, scr):
        dh2_ref, f_ref, h1_ref, zr_ref = tin
        g_pre, w1, w2, g_post = res
        df_o, dz_o, dh1_o = tout
        dg_post_a, dg_pre_a = acc
        dh2_ = dh2_ref[...]
        gp = g_post[...]
        _, fh, rf = _rms(f_ref[...].astype(F32), gp)
        df, dg = _rms_bwd(fh, rf, gp, dh2_)
        dg_post_a[...] += dg
        df = df.astype(MX)
        df_o[...] = df
        dz = (_dot_nt(df, w2[...]) * (2.0 * zr_ref[...].astype(F32))).astype(MX)
        dz_o[...] = dz
        dm = _dot_nt(dz, w1[...])
        gq = g_pre[...]
        _, hh, rh = _rms(h1_ref[...], gq)
        dh1, dg = _rms_bwd(hh, rh, gq, dm)
        dg_pre_a[...] += dg
        dh1_o[...] = dh2_ + dh1

    outs = [_sds(n_tok, D_MODEL, MX), _sds(n_tok, D_FF, MX), _sds(n_tok, D_MODEL, F32)]
    accs = [_sds(1, D_MODEL, F32), _sds(1, D_MODEL, F32)]
    res = [w["g_pre_mlp"], w["w_ff1"], w["w_ff2"], w["g_post_mlp"]]
    return _tok_call("bwd_mlp", body, n_tok, tm, [dh2, f, h1, zr], res, outs, accs)


def _bwd_mix(dh1, y, a_br, p_br, gl, attn, d, w, tm):
    n_tok = dh1.shape[0]

    def body(i, tin, res, tout, acc, scr):
        dh1_ref, y_ref, a_ref, pp_ref, gl_ref, at_ref, d_ref = tin
        g_post, w_out, b_gate, w_ba, w_bp, w_pool, pool_scale, sel = res
        dy_o, da_o, dpp_o, dgl_o, do_o, delta_o, dyp_o, dd_o = tout
        dg_post_a, db_a, dps_a = acc
        g = g_post[...]
        _, yh, r = _rms(y_ref[...].astype(F32), g)
        dy, dg = _rms_bwd(yh, r, g, dh1_ref[...])
        dg_post_a[...] += dg
        dy = dy.astype(MX)
        dy_o[...] = dy
        dmerged = _dot_nt(dy, w_out[...])
        gates = _sig(gl_ref[...].astype(F32) + b_gate[...])
        ga, gp = gates[:, :D_MODEL], gates[:, D_MODEL:]
        da = (dmerged * ga).astype(MX)
        dpp = (dmerged * gp).astype(MX)
        da_o[...] = da
        dpp_o[...] = dpp
        dgl_a = dmerged * a_ref[...].astype(F32) * ga * (1.0 - ga)
        dgl_p = dmerged * pp_ref[...].astype(F32) * gp * (1.0 - gp)
        dgl_o[:, :D_MODEL] = dgl_a.astype(dgl_o.dtype)
        dgl_o[:, D_MODEL:] = dgl_p.astype(dgl_o.dtype)
        db_a[:, :D_MODEL] += jnp.sum(dgl_a, axis=0, keepdims=True)
        db_a[:, D_MODEL:] += jnp.sum(dgl_p, axis=0, keepdims=True)
        do = _dot_nt(da, w_ba[...]).astype(MX)
        do_o[...] = do
        prod = do.astype(F32) * at_ref[...].astype(F32)
        hi = prod.astype(MX)
        lo = (prod - hi.astype(F32)).astype(MX)
        delta_o[...] = _dot(hi, sel[...]) + _dot(lo, sel[...])
        dpooled = _dot_nt(dpp, w_bp[...])
        for gi in range(len(POOL_WINDOWS)):
            cs = slice(gi * POOL_GROUP, (gi + 1) * POOL_GROUP)
            ypre = _dot(d_ref[:, cs], w_pool[gi])
            dps_a[:, cs] += jnp.sum(dpooled[:, cs] * ypre, axis=0, keepdims=True)
            dyp = (dpooled[:, cs] * pool_scale[:, cs]).astype(MX)
            dyp_o[:, cs] = dyp
            dd_o[:, cs] = _dot_nt(dyp, w_pool[gi])

    outs = [_sds(n_tok, D_MODEL, MX), _sds(n_tok, D_MODEL, MX), _sds(n_tok, D_MODEL, MX), _sds(n_tok, 2 * D_MODEL, MX),
            _sds(n_tok, N_HEADS * V_HEAD, MX), _sds(n_tok, HEAD_SLOT, F32), _sds(n_tok, POOL_WIDTH, MX),
            _sds(n_tok, POOL_WIDTH, F32)]
    accs = [_sds(1, D_MODEL, F32), _sds(1, 2 * D_MODEL, F32), _sds(1, POOL_WIDTH, F32)]
    res = [w["g_post_mix"], w["w_out"], w["b_gate"], w["w_branch_attn"], w["w_branch_pool"], w["w_pool"], w["pool_scale"],
           w["head_sel"]]
    return _tok_call("bwd_mix", body, n_tok, tm, [dh1, y, a_br, p_br, gl, attn, d], res, outs, accs)


def _bwd_heads(q_ref, k_ref, v_ref, do_ref, lse_ref, dl_ref, st_s, dpt_s, keep, use, n_heads):
    def products(h):
        hs = slice(h * HEAD_SLOT, (h + 1) * HEAD_SLOT)
        vs = slice(h * V_HEAD, (h + 1) * V_HEAD)
        st_s[h % 2] = _dot_nt(k_ref[:, hs], q_ref[:, hs])
        dpt_s[h % 2] = _dot_nt(v_ref[:, vs], do_ref[:, vs])

    products(0)
    for h in range(n_heads):
        if h + 1 < n_heads:
            products(h + 1)
        st = st_s[h % 2]
        if keep is not None:
            st = jnp.where(keep, st, NEG)
        pt = jnp.exp2(st - lse_ref[h:h + 1, :])
        use(h, pt, pt * (dpt_s[h % 2] - dl_ref[h:h + 1, :]))


HEAD_GROUP = 4


def _attn_bwd(q, k, kt, v, do, lse, delta, tq, tk, scatter=None):
    n_tok = q.shape[0]
    nq, ratio = n_tok // tq, tq // tk
    n_groups = N_HEADS // HEAD_GROUP
    gq, gv = HEAD_GROUP * HEAD_SLOT, HEAD_GROUP * V_HEAD
    qi, kj = _causal_pairs(nq, ratio, by_kv=True)

    n_pairs = qi.shape[0]

    def kern(qi_ref, kj_ref, q_ref, k_ref, kt_ref, v_ref, do_ref, lse_ref, dl_ref, *rest):
        if scatter is not None:
            s_hbm, dq_ref, dk_ref, dv_ref, r_hbm, dk_s, dv_s, st_s, dpt_s, send_sems, recv_sems = rest
        else:
            dq_ref, dk_ref, dv_ref, dk_s, dv_s, st_s, dpt_s = rest
        s_id = pl.program_id(1)
        i, j = qi_ref[s_id], kj_ref[s_id]
        cols = pl.ds(pl.multiple_of(i * tq, tq), tq)
        if scatter is not None:
            group = pl.program_id(0)
            _scatter_steps(jnp.logical_and(group == 0, s_id == 0),
                           jnp.logical_and(group == n_groups - 1, s_id == n_pairs - 1), s_hbm, r_hbm, (send_sems, recv_sems))

        @pl.when(s_id == 0)
        def _():
            dq_ref[...] = jnp.zeros(dq_ref.shape, F32)

        def use(h, pt, dst):
            hs = slice(h * HEAD_SLOT, (h + 1) * HEAD_SLOT)
            dv_s[h] += _dot(pt, do_ref[:, h * V_HEAD:(h + 1) * V_HEAD])
            dk_s[:, hs] += _dot(dst, q_ref[:, hs])
            dq_ref[hs, cols] += _dot(kt_ref[hs, :], dst)

        def heads(masked):
            keep = _keep_t(tk, tq, j * tk - i * tq) if masked else None
            _bwd_heads(q_ref, k_ref, v_ref, do_ref, lse_ref.at[0], dl_ref.at[0], st_s, dpt_s, keep, use, HEAD_GROUP)

        @pl.when(j >= i * ratio)
        def _():
            dk_s[...] = jnp.zeros(dk_s.shape, F32)
            dv_s[...] = jnp.zeros(dv_s.shape, F32)
            heads(True)

        @pl.when(j < i * ratio)
        def _():
            heads(False)

        @pl.when(i == nq - 1)
        def _():
            dk_ref[...] = (dk_s[...] * (1.0 / LOG2E)).astype(dk_ref.dtype)
            for h in range(HEAD_GROUP):
                dv_ref[:, h * V_HEAD:(h + 1) * V_HEAD] = dv_s[h].astype(dv_ref.dtype)

    at_q = lambda g, s, qi, kj: (qi[s], g)
    at_k = lambda g, s, qi, kj: (kj[s], g)
    at_kt = lambda g, s, qi, kj: (g, kj[s])
    at_stat = lambda g, s, qi, kj: (g, 0, qi[s])
    in_specs = [pl.BlockSpec((tq, gq), at_q), pl.BlockSpec((tk, gq), at_k), pl.BlockSpec((gq, tk), at_kt),
                pl.BlockSpec((tk, gv), at_k), pl.BlockSpec((tq, gv), at_q),
                pl.BlockSpec((1, HEAD_GROUP, tq), at_stat), pl.BlockSpec((1, HEAD_GROUP, tq), at_stat)]
    out_specs = [pl.BlockSpec((gq, n_tok), lambda g, s, qi, kj: (g, 0), pipeline_mode=pl.Buffered(1)),
                 pl.BlockSpec((tk, gq), at_k), pl.BlockSpec((tk, gv), at_k)]
    out_shape = [_sds(QK_WIDTH, n_tok, F32), _sds(n_tok, QK_WIDTH, MX), _sds(n_tok, N_HEADS * V_HEAD, MX)]
    scratch = [pltpu.VMEM((tk, gq), F32), pltpu.VMEM((HEAD_GROUP, tk, V_HEAD), F32),
               pltpu.VMEM((2, tk, tq), F32), pltpu.VMEM((2, tk, tq), F32)]
    stat3 = lambda a: a.reshape(n_groups, HEAD_GROUP, n_tok)
    operands = [qi, kj, q, k, kt, v, do, stat3(lse), stat3(delta)]
    if scatter is not None:
        operands.append(scatter)
        in_specs.append(_HBM)
        out_specs.append(_HBM)
        out_shape.append(jax.ShapeDtypeStruct((N_DEVICES, scatter.shape[1] // 2, PACK_COLS), scatter.dtype))
        scratch += [pltpu.SemaphoreType.DMA((N_DEVICES - 1,)), pltpu.SemaphoreType.DMA((N_DEVICES - 1,))]
    gs = pltpu.PrefetchScalarGridSpec(num_scalar_prefetch=2, grid=(n_groups, n_pairs), in_specs=in_specs,
                                      out_specs=out_specs, scratch_shapes=scratch)
    return pl.pallas_call(kern, name="attn_bwd", grid_spec=gs, out_shape=out_shape,
                          compiler_params=_params(("arbitrary", "arbitrary")))(*operands)


def _bwd_inproj(dq_t, dk, dv, dd, dgl, ps, x, dh1, cc, sa, w, tm):
    n_tok = x.shape[0]
    n_tiles = n_tok // tm
    last_halo = n_tok // HALO - 1
    halo_spec = pl.BlockSpec((HALO, POOL_WIDTH), lambda i: (jnp.minimum((i + 1) * (tm // HALO), last_halo), 0))

    def body(i, tin, res, tout, acc, scr):
        dq_ref, dk_ref, dv_ref, dd_ref, ddh_ref, dgl_ref, ps_ref, x_ref, dh1_ref, c_ref, s_ref = tin
        w_uq, g_q, w_k, w_v, e_mat, g_kv, w_in, g_pre = res
        dqu_o, dproj_o, dx_o = tout
        dgq_a, dgkv_a, dgpre_a = acc
        ext, = scr
        cc_, sa_ = c_ref[...], s_ref[...]
        for h in range(N_HEADS):
            hs = slice(h * HEAD_SLOT, (h + 1) * HEAD_SLOT)
            dqu_o[:, hs] = (_unrope(dq_ref[hs, :].T, cc_, sa_) * SCALE).astype(dqu_o.dtype)
        gq = g_q[...]
        _, qh, rq = _rms(ps_ref[:, :Q_LORA].astype(F32), gq)
        dqd, dg = _rms_bwd(qh, rq, gq, _dot_nt(dqu_o[...], w_uq[...]))
        dgq_a[...] += dg
        dproj_o[:, :Q_LORA] = dqd.astype(dproj_o.dtype)
        gkv = g_kv[...]
        _, kh, rk = _rms(ps_ref[:, Q_LORA:Q_LORA + KV_LORA].astype(F32), gkv)
        dkvd, dg = _rms_bwd(kh, rk, gkv, _dot_nt(dk_ref[...], w_k[...]) + _dot_nt(dv_ref[...], w_v[...]))
        dgkv_a[...] += dg
        dproj_o[:, Q_LORA:Q_LORA + KV_LORA] = dkvd.astype(dproj_o.dtype)
        dproj_o[:, Q_LORA + KV_LORA:SMALL_COLS] = _unrope(_dot_nt(dk_ref[...], e_mat[...]), cc_, sa_).astype(dproj_o.dtype)
        row = i * tm + lax.broadcasted_iota(jnp.int32, (tm + HALO, 1), 0)
        for gi, wdw in enumerate(POOL_WINDOWS):
            cs = slice(gi * POOL_GROUP, (gi + 1) * POOL_GROUP)
            inv = 1.0 / jnp.minimum(row + 1, wdw).astype(F32)
            ext[pl.ds(0, tm), cs] = dd_ref[:, cs] * inv[:tm]
            ext[pl.ds(tm, HALO), cs] = jnp.where(i < n_tiles - 1, ddh_ref[:, cs] * inv[tm:], 0.0)
            s = ext[pl.ds(0, tm), cs]
            for k_ in range(1, wdw):
                s = s + ext[pl.ds(k_, tm), cs]
            dproj_o[:, SMALL_COLS + gi * POOL_GROUP:SMALL_COLS + (gi + 1) * POOL_GROUP] = (s - dd_ref[:, cs]).astype(dproj_o.dtype)
        dproj_o[:, SMALL_COLS + POOL_WIDTH:] = dgl_ref[...]
        da = _dot_nt(dproj_o[...], w_in[...])
        gp = g_pre[...]
        _, xh, rx = _rms(x_ref[...], gp)
        dx, dg = _rms_bwd(xh, rx, gp, da)
        dgpre_a[...] += dg
        dx_o[...] = dh1_ref[...] + dx

    outs = [_sds(n_tok, QK_WIDTH, MX), _sds(n_tok, IN_PAD, MX), _sds(n_tok, D_MODEL, F32)]
    accs = [_sds(1, Q_LORA, F32), _sds(1, KV_LORA, F32), _sds(1, D_MODEL, F32)]
    res = [w["w_uq"], w["g_q"], w["w_k"], w["w_v"], w["e_mat"], w["g_kv"], w["w_in"], w["g_pre_mix"]]
    dq_spec = pl.BlockSpec((QK_WIDTH, tm), lambda i: (0, i))
    return _tok_call("bwd_inproj", body, n_tok, tm, [(dq_t, dq_spec), dk, dv, dd, (dd, halo_spec), dgl, ps, x, dh1, cc, sa], res, outs, accs,
                     scratch=[pltpu.VMEM((tm + HALO, POOL_WIDTH), F32)])


XTDY_TOKENS = 1024
XTDY_OUT_BYTES = 8 * 2**20
XTDY_IN_BYTES = 8 * 2**20


def _xtdy(name, x, dy, allreduce=None, square_x=False):
    n_tok, kk = x.shape
    nn = dy.shape[1]
    bk = kk
    while bk * nn * 4 > XTDY_OUT_BYTES and bk % 256 == 0:
        bk //= 2
    bt = min(XTDY_TOKENS, n_tok)
    while (2 * bt <= n_tok and n_tok % (2 * bt) == 0 and 2 * bt * nn * dy.dtype.itemsize <= XTDY_IN_BYTES
           and 2 * bt * bk * x.dtype.itemsize <= XTDY_IN_BYTES):
        bt *= 2

    grid = (kk // bk, n_tok // bt)

    def kern(x_ref, dy_ref, *rest):
        o_ref = rest[1] if allreduce is not None else rest[0]
        if allreduce is not None:
            g_ref, _, sum_ref, buf, send_sems, recv_sems = rest
            step = pl.program_id(0) * grid[1] + pl.program_id(1)
            _allreduce_steps(step == 0, step == grid[0] * grid[1] - 1, g_ref, sum_ref, buf, send_sems, recv_sems)

        @pl.when(pl.program_id(1) == 0)
        def _():
            o_ref[...] = jnp.zeros(o_ref.shape, F32)
        xv = x_ref[...]
        if square_x:
            xv = xv.astype(F32)
            xv = xv * xv
        o_ref[...] += _dot_tn(xv, dy_ref[...])

    operands = [x, dy]
    in_specs = [pl.BlockSpec((bt, bk), lambda a, t: (t, a)), pl.BlockSpec((bt, nn), lambda a, t: (t, 0))]
    out_specs = [pl.BlockSpec((bk, nn), lambda a, t: (a, 0))]
    out_shape = [_sds(kk, nn, F32)]
    scratch = []
    if allreduce is not None:
        vmem = pl.BlockSpec(memory_space=pltpu.VMEM)
        operands.append(allreduce)
        in_specs.append(vmem)
        out_specs.append(vmem)
        out_shape.append(jax.ShapeDtypeStruct(allreduce.shape, allreduce.dtype))
        scratch = [pltpu.VMEM((N_DEVICES,) + allreduce.shape, allreduce.dtype), pltpu.SemaphoreType.DMA((N_DEVICES - 1,)),
                   pltpu.SemaphoreType.DMA((N_DEVICES - 1,))]
    res = pl.pallas_call(kern, name=name, grid=grid, in_specs=in_specs, out_specs=out_specs, out_shape=out_shape,
                         scratch_shapes=scratch, compiler_params=_params(("arbitrary", "arbitrary")))(*operands)
    return res if allreduce is not None else res[0]


def _rope_tables(positions):
    inv_freq = ROPE_THETA ** (-jnp.arange(0, QK_ROPE, 2, dtype=F32) / QK_ROPE)
    ang_t = inv_freq[:, None] * positions.astype(F32)[None, :]
    cos_t, sin_t = lax.optimization_barrier((jnp.cos(ang_t), jnp.sin(ang_t)))
    cos, sin = cos_t.T, sin_t.T
    n_tok = positions.shape[0]
    ones, z64 = jnp.ones((n_tok, ROPE_LANE), F32), jnp.zeros((n_tok, ROPE_LANE), F32)
    z32 = jnp.zeros((n_tok, HEAD_SLOT - ROPE_LANE - QK_ROPE), F32)
    return jnp.concatenate([ones, cos, cos, z32], 1), jnp.concatenate([z64, -sin, sin, z32], 1)


def _kernel_weights(full):
    w_in, w_uq, w_ukv = full["w_in"], full["w_uq"], full["w_ukv"]
    c0 = Q_LORA + KV_LORA
    z = lambda n: jnp.zeros((D_MODEL, n), w_in.dtype)
    w = dict(full)
    w["w_in"] = jnp.concatenate([w_in[:, :c0], z(ROPE_LANE), w_in[:, c0:c0 + QK_ROPE], z(HEAD_SLOT - ROPE_LANE - QK_ROPE),
                                 w_in[:, c0 + QK_ROPE:]], 1)
    w["w_uq"] = jnp.pad(w_uq.reshape(Q_LORA, N_HEADS, QK_NOPE + QK_ROPE),
                        ((0, 0), (0, 0), (0, HEAD_SLOT - QK_NOPE - QK_ROPE))).reshape(Q_LORA, QK_WIDTH)
    kv = w_ukv.reshape(KV_LORA, N_HEADS, QK_NOPE + V_HEAD)
    w["w_k"] = jnp.pad(kv[:, :, :QK_NOPE], ((0, 0), (0, 0), (0, HEAD_SLOT - QK_NOPE))).reshape(KV_LORA, QK_WIDTH)
    w["w_v"] = kv[:, :, QK_NOPE:].reshape(KV_LORA, N_HEADS * V_HEAD)
    e = np.zeros((HEAD_SLOT, QK_WIDTH), np.float32)
    sel = np.zeros((N_HEADS * V_HEAD, HEAD_SLOT), np.float32)
    for h in range(N_HEADS):
        for r in range(QK_ROPE):
            e[ROPE_LANE + r, h * HEAD_SLOT + ROPE_LANE + r] = 1.0
        sel[h * V_HEAD:(h + 1) * V_HEAD, h] = 1.0
    w["e_mat"] = jnp.asarray(e, MX)
    w["w_kt"], w["e_t"] = w["w_k"].T, jnp.asarray(e.T, MX)
    pad = ((0, 0), (0, V_ROWS - V_HEAD), (0, 0))
    w["w_vt"] = jnp.pad(w["w_v"].T.reshape(N_HEADS, V_HEAD, KV_LORA), pad).reshape(N_HEADS * V_ROWS, KV_LORA)
    ones = np.zeros((N_HEADS, V_ROWS, 1), np.float32)
    ones[:, V_HEAD] = 1.0
    w["v_ones"] = jnp.asarray(ones.reshape(N_HEADS * V_ROWS, 1))
    w["head_sel"] = jnp.asarray(sel, MX)
    w["w_pool"] = full["w_pool"].astype(MX)
    return w


def _local_step(x, p, positions, target, full, mesh_place=None, packed_rest=None):
    n_tok = x.shape[0]
    tm = tm_mlp = min(TOKEN_TILE, n_tok)
    fwd_tile = [min(t, n_tok) for t in ATTN_FWD_TILE]
    bwd_tile = [min(t, n_tok) for t in ATTN_BWD_TILE]
    w = _kernel_weights(full)
    cc, sa = _rope_tables(positions)

    if mesh_place is None:
        a, ps, u, gl, qn, kvn, q, k, v, kt, vt = _fwd_inproj(x, cc, sa, w, tm)
    else:
        my_chip, core = mesh_place
        a, ps, u, gl, qn, kvn, q, k, v, kt, vt, gathered = _fwd_inproj(x, cc, sa, w, tm, gather=packed_rest)
        w.update(_unpack_full(gathered, packed_rest, my_chip, REST))
    attn, lse = _attn_fwd(q, k, vt, *fwd_tile)
    d, pooled, a_br, p_br, merged, y, h1 = _fwd_mix(x, u, gl, attn, w, tm)
    m, zr, f, h2 = _fwd_mlp(h1, w, tm_mlp)
    dh2, de, dzg, loss_cols, dg_ple = _ple_fwd_bwd(h2, p, target, w, tm)
    df, dz, dh1, dg_post_mlp, dg_pre_mlp = _bwd_mlp(dh2, f, h1, zr, w, tm_mlp)
    dy, da_br, dp_br, dgl, do, delta, dyp, dd, dg_post_mix, db_gate, dpool_scale = _bwd_mix(dh1, y, a_br, p_br, gl, attn, d, w, tm)
    grads = {"w_branch_attn": _xtdy("dw_ba", attn, da_br), "w_branch_pool": _xtdy("dw_bp", pooled, dp_br),
             "w_out": _xtdy("dw_out", merged, dy), "w_ff1": _xtdy("dw_ff1", m, dz), "w_ff2": _xtdy("dw_ff2", zr, df, square_x=True),
             "w_ple_proj": _xtdy("dw_pe", p, de), "w_ple_gate": _xtdy("dw_pg", h2, dzg)}
    delta_t = delta[:, :N_HEADS].T
    if mesh_place is None:
        travelling = None
        dq_t, dk, dv = _attn_bwd(q, k, kt, v, do, lse, delta_t, *bwd_tile)
    else:
        pieces = _pack_pieces(grads, REST, WIRE)
        dq_t, dk, dv, received = _attn_bwd(q, k, kt, v, do, lse, delta_t, *bwd_tile, scatter=pieces)
        travelling = (pieces, received)
        grads = {}
    dqu, dproj, dx, dg_q, dg_kv, dg_pre_mix = _bwd_inproj(dq_t, dk, dv, dd, dgl, ps, x, dh1, cc, sa, w, tm)

    g_uq = _xtdy("dw_uq", qn, dqu)
    g_k = _xtdy("dw_k", kvn, dk)
    g_v = _xtdy("dw_v", kvn, dv)
    g_pool = _xtdy("dw_pool", d, dyp)
    small = {"g_pre_mix": dg_pre_mix, "b_gate": db_gate, "g_q": dg_q, "g_kv": dg_kv, "pool_scale": dpool_scale,
             "g_post_mix": dg_post_mix, "g_pre_mlp": dg_pre_mlp, "g_post_mlp": dg_post_mlp, "g_ple": dg_ple,
             "w_pool": jnp.stack([g_pool[g * POOL_GROUP:(g + 1) * POOL_GROUP, g * POOL_GROUP:(g + 1) * POOL_GROUP]
                                  for g in range(len(POOL_WINDOWS))])}
    if mesh_place is None:
        g_in = _xtdy("dw_in", a, dproj)
    else:
        g_in, small_sum = _xtdy("dw_in", a, dproj, allreduce=_pack_small(small, loss_cols))
        small, loss_cols = _unpack_small(small_sum), small_sum[-1:]

    c0 = Q_LORA + KV_LORA
    grads.update(small)
    grads.update({
        "w_in": jnp.concatenate([g_in[:, :c0], g_in[:, c0 + ROPE_LANE:c0 + ROPE_LANE + QK_ROPE], g_in[:, SMALL_COLS:]], 1),
        "w_uq": g_uq.reshape(Q_LORA, N_HEADS, HEAD_SLOT)[:, :, :QK_NOPE + QK_ROPE].reshape(Q_LORA, N_HEADS * (QK_NOPE + QK_ROPE)),
        "w_ukv": jnp.concatenate([g_k.reshape(KV_LORA, N_HEADS, HEAD_SLOT)[:, :, :QK_NOPE],
                                  g_v.reshape(KV_LORA, N_HEADS, V_HEAD)], 2).reshape(KV_LORA, N_HEADS * (QK_NOPE + V_HEAD)),
    })
    return loss_cols, dx, grads, travelling


def _place():
    return lax.axis_index("x"), lax.axis_index("y"), lax.axis_index("c")


CHIP_FLIPS = ((1, 0), (0, 1), (1, 1))


def _flip(x, y, fx, fy):
    return (1 - x if fx else x), (1 - y if fy else y)


_HBM = pl.BlockSpec(memory_space=pl.ANY)


def _gather_copies(w_ref, out_ref, send_sems, recv_sems):
    half = w_ref.shape[0] // 2
    x, y, c = _place()
    my_chip = 2 * x + y
    sibling = (x, y, 1 - c)

    def half_of(chip, hc):
        return out_ref.at[chip, pl.ds(pl.multiple_of(hc * half, 16), half), :]

    src = w_ref.at[pl.ds(pl.multiple_of(c * half, 16), half), :]
    sends, landed, forwards, from_sibling = [], [], [], []
    for j, (fx, fy) in enumerate(CHIP_FLIPS):
        px, py = _flip(x, y, fx, fy)
        mine_there, theirs_here, theirs_other = half_of(my_chip, c), half_of(2 * px + py, c), half_of(2 * px + py, 1 - c)
        sends.append(pltpu.make_async_remote_copy(src, mine_there, send_sems.at[j], recv_sems.at[j],
                                                  device_id=(px, py, c), device_id_type=MESH))
        landed.append(pltpu.make_async_remote_copy(src, theirs_here, send_sems.at[j], recv_sems.at[j],
                                                   device_id=(px, py, c), device_id_type=MESH))
        forwards.append(pltpu.make_async_remote_copy(theirs_here, theirs_here, send_sems.at[3 + j], recv_sems.at[3 + j],
                                                     device_id=sibling, device_id_type=MESH))
        from_sibling.append(pltpu.make_async_remote_copy(theirs_other, theirs_other, send_sems.at[3 + j],
                                                         recv_sems.at[3 + j], device_id=sibling, device_id_type=MESH))
    return sends, landed, forwards, from_sibling


def _gather_steps(i, n_steps, w_ref, out_ref, sems):
    sends, landed, forwards, from_sibling = _gather_copies(w_ref, out_ref, *sems)

    @pl.when(i == 0)
    def _():
        for cp in sends:
            cp.start()

    @pl.when(i == (3 * n_steps) // 4)
    def _():
        for arrived, fwd in zip(landed, forwards):
            arrived.wait_recv()
            fwd.start()

    @pl.when(i == n_steps - 1)
    def _():
        for cp in from_sibling:
            cp.wait_recv()
        for cp in sends + forwards:
            cp.wait_send()


def _allgather_shards(wp):
    def body(w_ref, out_ref, send_sems, recv_sems):
        sends, landed, forwards, from_sibling = _gather_copies(w_ref, out_ref, send_sems, recv_sems)
        for cp in sends:
            cp.start()
        for arrived, fwd in zip(landed, forwards):
            arrived.wait_recv()
            fwd.start()
        for cp in from_sibling:
            cp.wait_recv()
        for cp in sends + forwards:
            cp.wait_send()

    return pl.pallas_call(
        body, name="allgather_shards", out_shape=jax.ShapeDtypeStruct((N_CHIPS,) + wp.shape, wp.dtype),
        in_specs=[_HBM], out_specs=_HBM,
        scratch_shapes=[pltpu.SemaphoreType.DMA((6,)), pltpu.SemaphoreType.DMA((6,))],
    )(wp)


def _exchange_halves(g):
    rows = g.shape[1]
    half = rows // 2

    def body(g_ref, r_ref, send_sem, recv_sem):
        x, y, c = _place()
        src = g_ref.at[:, pl.ds(pl.multiple_of((1 - c) * half, 8), half), :]
        cp = pltpu.make_async_remote_copy(src, r_ref, send_sem, recv_sem, device_id=(x, y, 1 - c), device_id_type=MESH)
        cp.start()
        cp.wait()

    return pl.pallas_call(
        body, name="exchange_halves", out_shape=jax.ShapeDtypeStruct((N_CHIPS, half, PACK_COLS), g.dtype),
        in_specs=[_HBM], out_specs=_HBM, scratch_shapes=[pltpu.SemaphoreType.DMA, pltpu.SemaphoreType.DMA],
    )(g)


def _add_halves(g, r, c):
    rows = g.shape[1]
    half = rows // 2
    br = REDUCE_ROWS
    nb = half // br

    def kern(c_ref, g_ref, r_ref, o_ref):
        o_ref[...] = (g_ref[...] + r_ref[...]).astype(o_ref.dtype)

    gs = pltpu.PrefetchScalarGridSpec(
        num_scalar_prefetch=1, grid=(N_CHIPS, nb),
        in_specs=[pl.BlockSpec((1, br, PACK_COLS), lambda k, t, c: (k, c[0] * nb + t, 0)),
                  pl.BlockSpec((1, br, PACK_COLS), lambda k, t, c: (k, t, 0))],
        out_specs=pl.BlockSpec((1, br, PACK_COLS), lambda k, t, c: (k, t, 0)))
    return pl.pallas_call(kern, name="add_halves", grid_spec=gs,
                          out_shape=jax.ShapeDtypeStruct((N_CHIPS, half, PACK_COLS), WIRE),
                          compiler_params=_params(("arbitrary", "arbitrary")))(c.reshape(1), g, r)


def _scatter_copies(s_ref, r_ref, send_sems, recv_sems):
    x, y, c = _place()
    my_chip = 2 * x + y
    sends, arrivals = [], []
    for j, (fx, fy) in enumerate(CHIP_FLIPS):
        px, py = _flip(x, y, fx, fy)
        slot = r_ref.at[2 * px + py]
        sends.append(pltpu.make_async_remote_copy(s_ref.at[2 * px + py], r_ref.at[my_chip], send_sems.at[j], recv_sems.at[j],
                                                  device_id=(px, py, c), device_id_type=MESH))
        arrivals.append(pltpu.make_async_remote_copy(slot, slot, send_sems.at[j], recv_sems.at[j],
                                                     device_id=(px, py, c), device_id_type=MESH))
    return sends, arrivals


N_DEVICES = 8


def _peer(x, y, c, f):
    px, py = _flip(x, y, f & 4, f & 2)
    return px, py, (1 - c if f & 1 else c)


def _scatter_all_copies(p_ref, r_ref, send_sems, recv_sems):
    half = p_ref.shape[1] // 2
    x, y, c = _place()
    me = 4 * x + 2 * y + c
    sends, arrivals = [], []
    for f in range(1, N_DEVICES):
        px, py, pc = _peer(x, y, c, f)
        theirs = p_ref.at[2 * px + py, pl.ds(pl.multiple_of(pc * half, 16), half), :]
        slot = r_ref.at[4 * px + 2 * py + pc]
        sends.append(pltpu.make_async_remote_copy(theirs, r_ref.at[me], send_sems.at[f - 1], recv_sems.at[f - 1],
                                                  device_id=(px, py, pc), device_id_type=MESH))
        arrivals.append(pltpu.make_async_remote_copy(slot, slot, send_sems.at[f - 1], recv_sems.at[f - 1],
                                                     device_id=(px, py, pc), device_id_type=MESH))
    return sends, arrivals


def _scatter_steps(first, last, p_ref, r_ref, sems):
    sends, arrivals = _scatter_all_copies(p_ref, r_ref, *sems)

    @pl.when(first)
    def _():
        for cp in sends:
            cp.start()

    @pl.when(last)
    def _():
        for cp in arrivals:
            cp.wait_recv()
        for cp in sends:
            cp.wait_send()


def _scatter_pieces(s):
    def body(s_ref, r_ref, send_sems, recv_sems):
        sends, arrivals = _scatter_copies(s_ref, r_ref, send_sems, recv_sems)
        for cp in sends:
            cp.start()
        for cp in arrivals:
            cp.wait_recv()
        for cp in sends:
            cp.wait_send()

    return pl.pallas_call(
        body, name="scatter_pieces", out_shape=jax.ShapeDtypeStruct(s.shape, s.dtype), in_specs=[_HBM], out_specs=_HBM,
        scratch_shapes=[pltpu.SemaphoreType.DMA((3,)), pltpu.SemaphoreType.DMA((3,))],
    )(s)


def _sum_pieces(r, mine, slot):
    slots, half = r.shape[:2]
    br = REDUCE_ROWS

    def kern(slot_ref, r_ref, m_ref, o_ref):
        total = None
        for k in range(slots):
            term = jnp.where(slot_ref[0] == k, m_ref[0], r_ref[k]).astype(F32)
            total = term if total is None else total + term
        o_ref[...] = total

    gs = pltpu.PrefetchScalarGridSpec(
        num_scalar_prefetch=1, grid=(half // br,),
        in_specs=[pl.BlockSpec((slots, br, PACK_COLS), lambda t, s: (0, t, 0)),
                  pl.BlockSpec((1, br, PACK_COLS), lambda t, s: (0, t, 0))],
        out_specs=pl.BlockSpec((br, PACK_COLS), lambda t, s: (t, 0)))
    return pl.pallas_call(kern, name="sum_pieces", grid_spec=gs, out_shape=_sds(half, PACK_COLS, F32),
                          compiler_params=_params(("arbitrary",)))(slot.reshape(1), r, mine)


def _join_halves(f):
    def body(f_ref, o_ref, send_sem, recv_sem):
        x, y, c = _place()
        cp = pltpu.make_async_remote_copy(f_ref, o_ref, send_sem, recv_sem, device_id=(x, y, 1 - c), device_id_type=MESH)
        cp.start()
        cp.wait()

    return pl.pallas_call(
        body, name="join_halves", out_shape=jax.ShapeDtypeStruct(f.shape, f.dtype), in_specs=[_HBM], out_specs=_HBM,
        scratch_shapes=[pltpu.SemaphoreType.DMA, pltpu.SemaphoreType.DMA],
    )(f)


def _allreduce_steps(first, last, g_ref, o_ref, buf, send_sems, recv_sems):
    x, y, c = _place()
    me = 4 * x + 2 * y + c
    sends, arrivals = [], []
    for f in range(1, N_DEVICES):
        px, py, pc = _peer(x, y, c, f)
        slot = buf.at[4 * px + 2 * py + pc]
        sends.append(pltpu.make_async_remote_copy(g_ref, buf.at[me], send_sems.at[f - 1], recv_sems.at[f - 1],
                                                  device_id=(px, py, pc), device_id_type=MESH))
        arrivals.append(pltpu.make_async_remote_copy(slot, slot, send_sems.at[f - 1], recv_sems.at[f - 1],
                                                     device_id=(px, py, pc), device_id_type=MESH))

    @pl.when(first)
    def _():
        buf[me] = g_ref[...]
        for cp in sends:
            cp.start()

    @pl.when(last)
    def _():
        for cp in arrivals:
            cp.wait_recv()
        for cp in sends:
            cp.wait_send()
        total = buf[0]
        for k in range(1, N_DEVICES):
            total = total + buf[k]
        o_ref[...] = total


def _adamw_update(g_ref, w_ref, m_ref, v_ref, d_o, m_o, v_o):
    c1 = 1.0 - ADAM_B1 ** ADAM_STEP
    c2 = 1.0 - ADAM_B2 ** ADAM_STEP
    g_ = g_ref[...]
    m_new = ADAM_B1 * m_ref[...] + (1.0 - ADAM_B1) * g_
    v_new = ADAM_B2 * v_ref[...] + (1.0 - ADAM_B2) * (g_ * g_)
    m_o[...] = m_new
    v_o[...] = v_new
    d_o[...] = -ADAM_LR * ((m_new / c1) / (jnp.sqrt(v_new / c2) + ADAM_EPS) + ADAM_WD * w_ref[...])


ADAMW_ROWS = 256


def _adamw(name, g, w, m, v):
    _, rows, cols = w.shape
    br = int(np.gcd(ADAMW_ROWS, rows))

    def kern(*refs):
        _adamw_update(*refs)

    spec = pl.BlockSpec((1, br, cols), lambda t: (0, t, 0))
    out = jax.ShapeDtypeStruct(w.shape, F32)
    return pl.pallas_call(kern, name="adamw_" + name, grid=(rows // br,), in_specs=[spec] * 4, out_specs=[spec] * 3,
                          out_shape=[out, out, out], compiler_params=_params(("arbitrary",)))(g, w, m, v)


def _adamw_small(gs, ws, ms, vs):
    n = len(gs)

    def kern(*refs):
        ins, outs = refs[:4 * n], refs[4 * n:]
        for k in range(n):
            _adamw_update(ins[k], ins[n + k], ins[2 * n + k], ins[3 * n + k], outs[k], outs[n + k], outs[2 * n + k])

    vmem = pl.BlockSpec(memory_space=pltpu.VMEM)
    out = [jax.ShapeDtypeStruct(w.shape, F32) for w in ws]
    res = pl.pallas_call(kern, name="adamw_small", in_specs=[vmem] * (4 * n), out_specs=[vmem] * (3 * n),
                         out_shape=out * 3, compiler_params=pltpu.CompilerParams(vmem_limit_bytes=VMEM_LIMIT))(*gs, *ws, *ms, *vs)
    return [(res[k], res[n + k], res[2 * n + k]) for k in range(n)]


def _shard_rows(shape, axis):
    k, n = shape
    return (k * n // N_CHIPS) // PACK_COLS


def _group(names):
    entries = [e for e in SHARDED if e[0] in names]
    used = sum(_shard_rows(shape, axis) for _, shape, axis in entries)
    return entries, -(-used // (2 * REDUCE_ROWS)) * 2 * REDUCE_ROWS


def _pack_shards(shards, names, dtype):
    entries, rows = _group(names)
    parts = [shards[name].astype(dtype).reshape(-1, PACK_COLS) for name, _, _ in entries]
    used = sum(p.shape[0] for p in parts)
    if rows > used:
        parts.append(jnp.zeros((rows - used, PACK_COLS), dtype))
    return jnp.concatenate(parts, 0)


def _unpack_shards(packed, names):
    out, r0 = {}, 0
    for name, (k, n), axis in _group(names)[0]:
        nr = _shard_rows((k, n), axis)
        shape = (k // N_CHIPS, n) if axis == 0 else (k, n // N_CHIPS)
        out[name] = packed[r0:r0 + nr].reshape(shape)
        r0 += nr
    return out


def _unpack_full(gathered, own, my_chip, names):
    chip = lax.broadcasted_iota(jnp.int32, (N_CHIPS, 1, 1), 0)
    gathered = jnp.where(chip == my_chip, own[None], gathered)
    out, r0 = {}, 0
    for name, (k, n), axis in _group(names)[0]:
        nr = _shard_rows((k, n), axis)
        part = gathered[:, r0:r0 + nr]
        if axis == 0:
            out[name] = part.reshape(k, n)
        else:
            out[name] = part.reshape(N_CHIPS, k, n // N_CHIPS).transpose(1, 0, 2).reshape(k, n)
        r0 += nr
    return out


def _pack_pieces(grads, names, dtype=F32):
    entries, rows = _group(names)
    parts = []
    for name, (k, n), axis in entries:
        g = grads[name].astype(dtype)
        if axis == 0:
            parts.append(g.reshape(N_CHIPS, -1, PACK_COLS))
        else:
            parts.append(g.reshape(k, N_CHIPS, n // N_CHIPS).transpose(1, 0, 2).reshape(N_CHIPS, -1, PACK_COLS))
    used = sum(p.shape[1] for p in parts)
    if rows > used:
        parts.append(jnp.zeros((N_CHIPS, rows - used, PACK_COLS), dtype))
    return jnp.concatenate(parts, 1)


def _pack_small(vals, last_row):
    flat = jnp.concatenate([vals[name].astype(F32).reshape(-1) for name, _ in SMALL])
    spare = jnp.zeros(((SMALL_ROWS - 1) * PACK_COLS - flat.shape[0],), F32)
    return jnp.concatenate([flat, spare, last_row.reshape(-1)]).reshape(SMALL_ROWS, PACK_COLS)


def _unpack_small(packed):
    flat, out, o = packed.reshape(-1), {}, 0
    for name, shape in SMALL:
        n = int(np.prod(shape))
        out[name] = flat[o:o + n].reshape(shape)
        o += n
    return out


def kernel(x, p, positions, g_pre_mix, w_in, b_gate, g_q, w_uq, g_kv, w_ukv, w_pool, pool_scale, w_branch_attn, w_branch_pool, w_out, g_post_mix, g_pre_mlp, w_ff1, w_ff2, g_post_mlp, w_ple_proj, w_ple_gate, g_ple, loss_target, m_g_pre_mix, m_w_in, m_b_gate, m_g_q, m_w_uq, m_g_kv, m_w_ukv, m_w_pool, m_pool_scale, m_w_branch_attn, m_w_branch_pool, m_w_out, m_g_post_mix, m_g_pre_mlp, m_w_ff1, m_w_ff2, m_g_post_mlp, m_w_ple_proj, m_w_ple_gate, m_g_ple, v_g_pre_mix, v_w_in, v_b_gate, v_g_q, v_w_uq, v_g_kv, v_w_ukv, v_w_pool, v_pool_scale, v_w_branch_attn, v_w_branch_pool, v_w_out, v_g_post_mix, v_g_pre_mlp, v_w_ff1, v_w_ff2, v_g_post_mlp, v_w_ple_proj, v_w_ple_gate, v_g_ple):
    given = dict(locals())
    weights = {n: given[n] for n in WEIGHT_ORDER}
    moments_m = {n: given["m_" + n] for n in WEIGHT_ORDER}
    moments_v = {n: given["v_" + n] for n in WEIGHT_ORDER}
    c = lax.axis_index("c")

    big_w = {name: weights[name][0] for name, _, _ in SHARDED}
    my_chip = 2 * lax.axis_index("x") + lax.axis_index("y")
    packed_first = _pack_shards(big_w, FIRST, MX)
    full = _unpack_full(_allgather_shards(packed_first), packed_first, my_chip, FIRST)
    for name, _ in SMALL:
        full[name] = weights[name][0] if name == "w_pool" else weights[name]

    loss_cols, dx, grads, (pieces_rest, received_rest) = _local_step(
        x[0], p[0, 0], positions[0], loss_target[0], full, (my_chip, c), _pack_shards(big_w, REST, MX))
    loss = 0.5 * jnp.sum(loss_cols) / D_MODEL

    def finish(received, mine, slot):
        reduced = _sum_pieces(received, mine, slot)
        theirs = _join_halves(reduced)
        return jnp.where(c == 0, jnp.concatenate([reduced, theirs]), jnp.concatenate([theirs, reduced]))

    pieces = _pack_pieces(grads, FIRST)
    sent = _add_halves(pieces, _exchange_halves(pieces), c)
    mine = lax.dynamic_slice(sent, (my_chip, 0, 0), (1,) + sent.shape[1:])
    shards = _unpack_shards(finish(_scatter_pieces(sent), mine, my_chip), FIRST)
    half = received_rest.shape[1]
    mine = lax.dynamic_slice(pieces_rest, (my_chip, c * half, 0), (1, half, PACK_COLS))
    shards.update(_unpack_shards(finish(received_rest, mine, 2 * my_chip + c), REST))

    out = {}
    for name, g in shards.items():
        out[name] = (g[None], *_adamw(name, g[None], weights[name], moments_m[name], moments_v[name]))
    small_g = {n: grads[n] for n, _ in SMALL}
    names = [n for n, _ in SMALL]
    updates = _adamw_small([small_g[n] for n in names], [weights[n] for n in names], [moments_m[n] for n in names],
                           [moments_v[n] for n in names])
    for n, upd in zip(names, updates):
        out[n] = (small_g[n], *upd)
    return (loss, dx[None], *[out[n][k] for k in range(4) for n in WEIGHT_ORDER])
```

```python
import numpy as np
import jax
import jax.numpy as jnp
from jax import lax
from jax.experimental import pallas as pl
from jax.experimental.pallas import tpu as pltpu

F32 = jnp.float32
MX = jnp.bfloat16
WIRE = jnp.bfloat16

D_MODEL = 1024
N_HEADS = 8
QK_NOPE = 64
QK_ROPE = 32
V_HEAD = 64
Q_LORA = 384
KV_LORA = 256
POOL_WINDOWS = (2, 4, 8, 16)
POOL_GROUP = 128
POOL_WIDTH = 512
D_FF = 4096
ROPE_THETA = 10000.0
EPS = 1e-6
HEAD_SLOT = 128
QK_WIDTH = N_HEADS * HEAD_SLOT
ROPE_LANE = 64
SMALL_COLS = Q_LORA + KV_LORA + HEAD_SLOT
IN_PAD = SMALL_COLS + POOL_WIDTH + 2 * D_MODEL
SCALE = (QK_NOPE + QK_ROPE) ** -0.5
LOG2E = 1.4426950408889634
NEG = -1e30
HALO = 16

ADAM_LR = 0.001
ADAM_B1 = 0.9
ADAM_B2 = 0.999
ADAM_EPS = 1e-08
ADAM_WD = 0.01
ADAM_STEP = 10

VMEM_LIMIT = 56 * 2**20
TOKEN_TILE = 512
MESH = pl.DeviceIdType.MESH

SHARDED = (
    ("w_in", (1024, 3232), 1),
    ("w_uq", (384, 768), 1),
    ("w_ukv", (256, 1024), 1),
    ("w_branch_attn", (512, 1024), 1),
    ("w_branch_pool", (512, 1024), 1),
    ("w_out", (1024, 1024), 0),
    ("w_ff1", (1024, 4096), 1),
    ("w_ff2", (4096, 1024), 0),
    ("w_ple_proj", (256, 1024), 1),
    ("w_ple_gate", (1024, 1024), 0),
)
SMALL = (
    ("g_pre_mix", (1, 1024)),
    ("b_gate", (1, 2048)),
    ("g_q", (1, 384)),
    ("g_kv", (1, 256)),
    ("w_pool", (1, 4, 128, 128)),
    ("pool_scale", (1, 512)),
    ("g_post_mix", (1, 1024)),
    ("g_pre_mlp", (1, 1024)),
    ("g_post_mlp", (1, 1024)),
    ("g_ple", (1, 1024)),
)
WEIGHT_ORDER = ("g_pre_mix", "w_in", "b_gate", "g_q", "w_uq", "g_kv", "w_ukv", "w_pool", "pool_scale", "w_branch_attn",
                "w_branch_pool", "w_out", "g_post_mix", "g_pre_mlp", "w_ff1", "w_ff2", "g_post_mlp", "w_ple_proj",
                "w_ple_gate", "g_ple")
N_CHIPS = 4
PACK_COLS = 1024
REDUCE_ROWS = 160
SMALL_ROWS = 80
FIRST = ("w_in", "w_uq", "w_ukv")
REST = tuple(name for name, _, _ in SHARDED if name not in FIRST)


def _dot(a, b):
    return jnp.dot(a.astype(MX), b.astype(MX), preferred_element_type=F32)


def _dot_nt(a, b):
    return lax.dot_general(a.astype(MX), b.astype(MX), (((1,), (1,)), ((), ())), preferred_element_type=F32)


def _dot_tn(a, b):
    return lax.dot_general(a.astype(MX), b.astype(MX), (((0,), (0,)), ((), ())), preferred_element_type=F32)


def _sig(x):
    return 1.0 / (1.0 + jnp.exp(-x))


def _rms(x, g):
    r = lax.rsqrt(jnp.mean(x * x, axis=1, keepdims=True) + EPS)
    xh = x * r
    return xh * g, xh, r


def _rms_bwd(xh, r, g, dy):
    dxn = dy * g
    dx = r * (dxn - xh * jnp.mean(dxn * xh, axis=1, keepdims=True))
    return dx, jnp.sum(dy * xh, axis=0, keepdims=True)


def _rot_half(v):
    lane = lax.broadcasted_iota(jnp.int32, v.shape, 1)
    return jnp.where(lane < ROPE_LANE + QK_ROPE // 2, pltpu.roll(v, HEAD_SLOT - QK_ROPE // 2, 1), pltpu.roll(v, QK_ROPE // 2, 1))


def _rope(v, cc, sa):
    return v * cc + _rot_half(v) * sa


def _unrope(v, cc, sa):
    return v * cc - _rot_half(v) * sa


def _params(sem):
    return pltpu.CompilerParams(dimension_semantics=sem, vmem_limit_bytes=VMEM_LIMIT)


def _tok_call(name, body, n_tok, tm, tiled, resident, outs, accs=(), scratch=(), exchange=None):
    def as_pair(t):
        if isinstance(t, tuple):
            return t
        return t, pl.BlockSpec((tm, t.shape[1]), lambda i: (i, 0))
    tiled = [as_pair(t) for t in tiled]
    outs = [as_pair(o) for o in outs]
    res_specs = [pl.BlockSpec(r.shape, lambda i, nd=r.ndim: (0,) * nd, pipeline_mode=pl.Buffered(1)) for r in resident]
    out_specs = [s for _, s in outs] + [pl.BlockSpec(a.shape, lambda i: (0, 0)) for a in accs]
    n_t, n_r, n_o, n_a, n_s = len(tiled), len(resident), len(outs), len(accs), len(scratch)
    n_steps = n_tok // tm
    operands = [a for a, _ in tiled] + list(resident)
    in_specs = [s for _, s in tiled] + res_specs
    out_shape = [o for o, _ in outs] + list(accs)
    scratch = list(scratch)
    if exchange is not None:
        ex_in, ex_out, ex_sems, ex_steps = exchange
        operands.append(ex_in)
        in_specs.append(_HBM)
        out_shape.append(ex_out)
        out_specs.append(_HBM)
        scratch += list(ex_sems)

    def kern(*refs):
        refs = list(refs)
        n_in = n_t + n_r + (exchange is not None)
        n_out = n_o + n_a + (exchange is not None)
        tin, res = refs[:n_t], refs[n_t:n_t + n_r]
        tout = refs[n_in:n_in + n_o]
        acc = refs[n_in + n_o:n_in + n_o + n_a]
        scr = refs[n_in + n_out:n_in + n_out + n_s]
        i = pl.program_id(0)
        if exchange is not None:
            ex_steps(i, n_steps, refs[n_in - 1], refs[n_in + n_out - 1], refs[n_in + n_out + n_s:])

        @pl.when(i == 0)
        def _():
            for a in acc:
                a[...] = jnp.zeros(a.shape, a.dtype)
        body(i, tin, res, tout, acc, scr)

    return pl.pallas_call(
        kern, name=name, grid=(n_steps,), in_specs=in_specs, out_specs=out_specs,
        out_shape=out_shape, scratch_shapes=scratch, compiler_params=_params(("arbitrary",)),
    )(*operands)


def _sds(rows, cols, dtype):
    return jax.ShapeDtypeStruct((rows, cols), dtype)


def _fwd_inproj(x, cc, sa, w, tm, gather=None):
    n_tok = x.shape[0]

    def body(i, tin, res, tout, acc, scr):
        x_ref, c_ref, s_ref = tin
        g_pre, w_in, g_q, w_uq, g_kv, w_k, w_v, e_mat, w_kt, e_t, w_vt, v_ones = res
        a_o, ps_o, u_o, gl_o, qn_o, kvn_o, q_o, k_o, v_o, kt_o, vt_o = tout
        a = _rms(x_ref[...], g_pre[...])[0].astype(MX)
        a_o[...] = a
        ps = _dot(a, w_in[:, :SMALL_COLS])
        ps_o[...] = ps.astype(ps_o.dtype)
        u_o[...] = _dot(a, w_in[:, SMALL_COLS:SMALL_COLS + POOL_WIDTH]).astype(u_o.dtype)
        gl_o[...] = _dot(a, w_in[:, SMALL_COLS + POOL_WIDTH:]).astype(gl_o.dtype)
        cc_, sa_ = c_ref[...], s_ref[...]
        qn = _rms(ps[:, :Q_LORA], g_q[...])[0].astype(MX)
        qn_o[...] = qn
        q = _dot(qn, w_uq[...])
        for h in range(N_HEADS):
            hs = slice(h * HEAD_SLOT, (h + 1) * HEAD_SLOT)
            q_o[:, hs] = (_rope(q[:, hs], cc_, sa_) * (SCALE * LOG2E)).astype(q_o.dtype)
        kvn = _rms(ps[:, Q_LORA:Q_LORA + KV_LORA], g_kv[...])[0].astype(MX)
        kvn_o[...] = kvn
        kr = _rope(ps[:, Q_LORA + KV_LORA:], cc_, sa_)
        k_o[...] = (_dot(kvn, w_k[...]) + _dot(kr, e_mat[...])).astype(k_o.dtype)
        v_o[...] = _dot(kvn, w_v[...]).astype(v_o.dtype)
        kt_o[...] = (_dot_nt(w_kt[...], kvn) + _dot_nt(e_t[...], kr)).astype(kt_o.dtype)
        vt_o[...] = (_dot_nt(w_vt[...], kvn) + v_ones[...]).astype(vt_o.dtype)

    outs = [_sds(n_tok, D_MODEL, MX), _sds(n_tok, SMALL_COLS, MX), _sds(n_tok, POOL_WIDTH, MX), _sds(n_tok, 2 * D_MODEL, MX),
            _sds(n_tok, Q_LORA, MX), _sds(n_tok, KV_LORA, MX), _sds(n_tok, QK_WIDTH, MX), _sds(n_tok, QK_WIDTH, MX),
            _sds(n_tok, N_HEADS * V_HEAD, MX),
            (_sds(QK_WIDTH, n_tok, MX), pl.BlockSpec((QK_WIDTH, tm), lambda i: (0, i))),
            (_sds(N_HEADS * V_ROWS, n_tok, MX), pl.BlockSpec((N_HEADS * V_ROWS, tm), lambda i: (0, i)))]
    res = [w["g_pre_mix"], w["w_in"], w["g_q"], w["w_uq"], w["g_kv"], w["w_k"], w["w_v"], w["e_mat"], w["w_kt"], w["e_t"],
           w["w_vt"], w["v_ones"]]
    exchange = None
    if gather is not None:
        gathered = jax.ShapeDtypeStruct((N_CHIPS,) + gather.shape, gather.dtype)
        exchange = (gather, gathered, [pltpu.SemaphoreType.DMA((6,)), pltpu.SemaphoreType.DMA((6,))], _gather_steps)
    return _tok_call("fwd_inproj", body, n_tok, tm, [x, cc, sa], res, outs, exchange=exchange)


def _causal_pairs(nq, ratio, by_kv):
    if by_kv:
        pairs = [(i, j) for j in range(nq * ratio) for i in range(j // ratio, nq)]
    else:
        pairs = [(i, j) for i in range(nq) for j in range((i + 1) * ratio)]
    return (jnp.asarray(np.array([p[0] for p in pairs], np.int32)), jnp.asarray(np.array([p[1] for p in pairs], np.int32)))


def _keep_t(tk, tq, off):
    return lax.broadcasted_iota(jnp.int32, (tk, tq), 0) + off <= lax.broadcasted_iota(jnp.int32, (tk, tq), 1)


ATTN_FWD_TILE = (1024, 1024)
ATTN_BWD_TILE = (1024, 512)
V_ROWS = 80


def _attn_fwd(q, k, vt, tq, tk):
    n_tok = q.shape[0]
    nq, ratio = n_tok // tq, tq // tk
    qi, kj = _causal_pairs(nq, ratio, by_kv=False)

    def kern(qi_ref, kj_ref, q_ref, k_ref, vt_ref, o_ref, lse_ref, m_s, acc_s, st_s):
        s_id = pl.program_id(0)
        i, j = qi_ref[s_id], kj_ref[s_id]

        @pl.when(j == 0)
        def _():
            m_s[...] = jnp.full(m_s.shape, NEG, F32)
            acc_s[...] = jnp.zeros(acc_s.shape, F32)

        def scores(h):
            hs = slice(h * HEAD_SLOT, (h + 1) * HEAD_SLOT)
            return _dot_nt(k_ref[:, hs], q_ref[:, hs])

        def heads(masked):
            keep = _keep_t(tk, tq, j * tk - i * tq) if masked else None
            st_s[0] = scores(0)
            for h in range(N_HEADS):
                if h + 1 < N_HEADS:
                    st_s[(h + 1) % 2] = scores(h + 1)
                st = st_s[h % 2]
                if masked:
                    st = jnp.where(keep, st, NEG)
                m_old = m_s[h]
                m_new = jnp.maximum(m_old, jnp.max(st, axis=0, keepdims=True))
                pt = jnp.exp2(st - m_new)
                acc_s[h] = jnp.exp2(m_old - m_new) * acc_s[h] + _dot(vt_ref[h * V_ROWS:(h + 1) * V_ROWS, :], pt)
                m_s[h] = m_new

        @pl.when(j < i * ratio)
        def _():
            heads(False)

        @pl.when(j >= i * ratio)
        def _():
            heads(True)

        @pl.when(j == (i + 1) * ratio - 1)
        def _():
            heads_out = []
            for h in range(N_HEADS):
                total = acc_s[h, V_HEAD:V_HEAD + 1, :]
                heads_out.append(acc_s[h, :V_HEAD, :] / total)
                lse_ref[h:h + 1, :] = m_s[h] + jnp.log2(total)
            o_ref[...] = jnp.concatenate(heads_out, 0).T.astype(o_ref.dtype)

    gs = pltpu.PrefetchScalarGridSpec(
        num_scalar_prefetch=2, grid=(qi.shape[0],),
        in_specs=[pl.BlockSpec((tq, QK_WIDTH), lambda s, qi, kj: (qi[s], 0)),
                  pl.BlockSpec((tk, QK_WIDTH), lambda s, qi, kj: (kj[s], 0)),
                  pl.BlockSpec((N_HEADS * V_ROWS, tk), lambda s, qi, kj: (0, kj[s]))],
        out_specs=[pl.BlockSpec((tq, N_HEADS * V_HEAD), lambda s, qi, kj: (qi[s], 0)),
                   pl.BlockSpec((N_HEADS, tq), lambda s, qi, kj: (0, qi[s]))],
        scratch_shapes=[pltpu.VMEM((N_HEADS, 1, tq), F32), pltpu.VMEM((N_HEADS, V_ROWS, tq), F32),
                        pltpu.VMEM((2, tk, tq), F32)])
    return pl.pallas_call(kern, name="attn_fwd", grid_spec=gs,
                          out_shape=[_sds(n_tok, N_HEADS * V_HEAD, MX), _sds(N_HEADS, n_tok, F32)],
                          compiler_params=_params(("arbitrary",)))(qi, kj, q, k, vt)


def _pool_windows(ext, i, tm, first_row):
    row = i * tm + lax.broadcasted_iota(jnp.int32, (tm, 1), 0)
    out = []
    for g, w in enumerate(POOL_WINDOWS):
        cs = slice(g * POOL_GROUP, (g + 1) * POOL_GROUP)
        s = ext[pl.ds(first_row, tm), cs]
        for k in range(1, w):
            s = s + ext[pl.ds(first_row - k, tm), cs]
        cnt = jnp.minimum(row + 1, w).astype(F32)
        out.append(s / cnt)
    return out


def _fwd_mix(x, u, gl, attn, w, tm):
    n_tok = x.shape[0]
    halo_spec = pl.BlockSpec((HALO, POOL_WIDTH), lambda i: (jnp.maximum(i * (tm // HALO) - 1, 0), 0))

    def body(i, tin, res, tout, acc, scr):
        x_ref, u_ref, uh_ref, gl_ref, at_ref = tin
        w_pool, pool_scale, w_ba, w_bp, b_gate, w_out, g_post = res
        d_o, pooled_o, a_o, pp_o, merged_o, y_o, h1_o = tout
        ext, = scr
        ext[pl.ds(0, HALO), :] = jnp.where(i > 0, uh_ref[...].astype(F32), 0.0)
        ext[pl.ds(HALO, tm), :] = u_ref[...].astype(F32)
        means = _pool_windows(ext, i, tm, HALO)
        for g in range(len(POOL_WINDOWS)):
            cs = slice(g * POOL_GROUP, (g + 1) * POOL_GROUP)
            d = (means[g] - ext[pl.ds(HALO, tm), cs]).astype(MX)
            d_o[:, cs] = d
            pooled_o[:, cs] = (_dot(d, w_pool[g]) * pool_scale[:, cs]).astype(pooled_o.dtype)
        a_br = _dot(at_ref[...], w_ba[...])
        p_br = _dot(pooled_o[...], w_bp[...])
        a_o[...] = a_br.astype(a_o.dtype)
        pp_o[...] = p_br.astype(pp_o.dtype)
        gates = _sig(gl_ref[...].astype(F32) + b_gate[...])
        merged = (gates[:, :D_MODEL] * a_br + gates[:, D_MODEL:] * p_br).astype(MX)
        merged_o[...] = merged
        y = _dot(merged, w_out[...])
        y_o[...] = y.astype(y_o.dtype)
        h1_o[...] = x_ref[...] + _rms(y, g_post[...])[0]

    outs = [_sds(n_tok, POOL_WIDTH, MX), _sds(n_tok, POOL_WIDTH, MX), _sds(n_tok, D_MODEL, MX), _sds(n_tok, D_MODEL, MX),
            _sds(n_tok, D_MODEL, MX), _sds(n_tok, D_MODEL, MX), _sds(n_tok, D_MODEL, F32)]
    res = [w["w_pool"], w["pool_scale"], w["w_branch_attn"], w["w_branch_pool"], w["b_gate"], w["w_out"], w["g_post_mix"]]
    return _tok_call("fwd_mix", body, n_tok, tm, [x, u, (u, halo_spec), gl, attn], res, outs,
                     scratch=[pltpu.VMEM((tm + HALO, POOL_WIDTH), F32)])


def _fwd_mlp(h1, w, tm):
    n_tok = h1.shape[0]

    def body(i, tin, res, tout, acc, scr):
        h1_ref, = tin
        g_pre, w1, w2, g_post = res
        m_o, zr_o, f_o, h2_o = tout
        h1_ = h1_ref[...]
        m = _rms(h1_, g_pre[...])[0].astype(MX)
        m_o[...] = m
        zr = jnp.maximum(_dot(m, w1[...]), 0.0)
        zr_o[...] = zr.astype(zr_o.dtype)
        a2 = (zr * zr).astype(MX)
        f = _dot(a2, w2[...])
        f_o[...] = f.astype(f_o.dtype)
        h2_o[...] = h1_ + _rms(f, g_post[...])[0]

    outs = [_sds(n_tok, D_MODEL, MX), _sds(n_tok, D_FF, MX), _sds(n_tok, D_MODEL, MX),
            _sds(n_tok, D_MODEL, F32)]
    res = [w["g_pre_mlp"], w["w_ff1"], w["w_ff2"], w["g_post_mlp"]]
    return _tok_call("fwd_mlp", body, n_tok, tm, [h1], res, outs)


def _ple_fwd_bwd(h2, p, target, w, tm):
    n_tok = h2.shape[0]

    def body(i, tin, res, tout, acc, scr):
        h2_ref, p_ref, t_ref = tin
        w_pe, w_pg, g_ple = res
        dh2_o, de_o, dzg_o = tout
        loss_a, dg_a = acc
        h2_ = h2_ref[...]
        e = _dot(p_ref[...], w_pe[...])
        pg = _sig(_dot(h2_, w_pg[...]))
        t = pg * e
        g = g_ple[...]
        tn, th, r = _rms(t, g)
        diff = h2_ + tn - t_ref[...]
        loss_a[...] += jnp.sum(diff * diff, axis=0, keepdims=True)
        dh3 = diff * (1.0 / D_MODEL)
        dt, dg = _rms_bwd(th, r, g, dh3)
        dg_a[...] += dg
        de_o[...] = (dt * pg).astype(de_o.dtype)
        dzg = (dt * e * pg * (1.0 - pg)).astype(MX)
        dzg_o[...] = dzg
        dh2_o[...] = dh3 + _dot_nt(dzg, w_pg[...])

    outs = [_sds(n_tok, D_MODEL, F32), _sds(n_tok, D_MODEL, MX), _sds(n_tok, D_MODEL, MX)]
    accs = [_sds(1, D_MODEL, F32), _sds(1, D_MODEL, F32)]
    return _tok_call("ple_fwd_bwd", body, n_tok, tm, [h2, p, target], [w["w_ple_proj"], w["w_ple_gate"], w["g_ple"]], outs, accs)


def _bwd_mlp(dh2, f, h1, zr, w, tm):
    n_tok = dh2.shape[0]

    def body(i, tin, res, tout, acc, scr):
        dh2_ref, f_ref, h1_ref, zr_ref = tin
        g_pre, w1, w2, g_post = res
        df_o, dz_o, dh1_o = tout
        dg_post_a, dg_pre_a = acc
        dh2_ = dh2_ref[...]
        gp = g_post[...]
        _, fh, rf = _rms(f_ref[...].astype(F32), gp)
        df, dg = _rms_bwd(fh, rf, gp, dh2_)
        dg_post_a[...] += dg
        df = df.astype(MX)
        df_o[...] = df
        dz = (_dot_nt(df, w2[...]) * (2.0 * zr_ref[...].astype(F32))).astype(MX)
        dz_o[...] = dz
        dm = _dot_nt(dz, w1[...])
        gq = g_pre[...]
        _, hh, rh = _rms(h1_ref[...], gq)
        dh1, dg = _rms_bwd(hh, rh, gq, dm)
        dg_pre_a[...] += dg
        dh1_o[...] = dh2_ + dh1

    outs = [_sds(n_tok, D_MODEL, MX), _sds(n_tok, D_FF, MX), _sds(n_tok, D_MODEL, F32)]
    accs = [_sds(1, D_MODEL, F32), _sds(1, D_MODEL, F32)]
    res = [w["g_pre_mlp"], w["w_ff1"], w["w_ff2"], w["g_post_mlp"]]
    return _tok_call("bwd_mlp", body, n_tok, tm, [dh2, f, h1, zr], res, outs, accs)


def _bwd_mix(dh1, y, a_br, p_br, gl, attn, d, w, tm):
    n_tok = dh1.shape[0]

    def body(i, tin, res, tout, acc, scr):
        dh1_ref, y_ref, a_ref, pp_ref, gl_ref, at_ref, d_ref = tin
        g_post, w_out, b_gate, w_ba, w_bp, w_pool, pool_scale, sel = res
        dy_o, da_o, dpp_o, dgl_o, do_o, delta_o, dyp_o, dd_o = tout
        dg_post_a, db_a, dps_a = acc
        g = g_post[...]
        _, yh, r = _rms(y_ref[...].astype(F32), g)
        dy, dg = _rms_bwd(yh, r, g, dh1_ref[...])
        dg_post_a[...] += dg
        dy = dy.astype(MX)
        dy_o[...] = dy
        dmerged = _dot_nt(dy, w_out[...])
        gates = _sig(gl_ref[...].astype(F32) + b_gate[...])
        ga, gp = gates[:, :D_MODEL], gates[:, D_MODEL:]
        da = (dmerged * ga).astype(MX)
        dpp = (dmerged * gp).astype(MX)
        da_o[...] = da
        dpp_o[...] = dpp
        dgl_a = dmerged * a_ref[...].astype(F32) * ga * (1.0 - ga)
        dgl_p = dmerged * pp_ref[...].astype(F32) * gp * (1.0 - gp)
        dgl_o[:, :D_MODEL] = dgl_a.astype(dgl_o.dtype)
        dgl_o[:, D_MODEL:] = dgl_p.astype(dgl_o.dtype)
        db_a[:, :D_MODEL] += jnp.sum(dgl_a, axis=0, keepdims=True)
        db_a[:, D_MODEL:] += jnp.sum(dgl_p, axis=0, keepdims=True)
        do = _dot_nt(da, w_ba[...]).astype(MX)
        do_o[...] = do
        prod = do.astype(F32) * at_ref[...].astype(F32)
        hi = prod.astype(MX)
        lo = (prod - hi.astype(F32)).astype(MX)
        delta_o[...] = _dot(hi, sel[...]) + _dot(lo, sel[...])
        dpooled = _dot_nt(dpp, w_bp[...])
        for gi in range(len(POOL_WINDOWS)):
            cs = slice(gi * POOL_GROUP, (gi + 1) * POOL_GROUP)
            ypre = _dot(d_ref[:, cs], w_pool[gi])
            dps_a[:, cs] += jnp.sum(dpooled[:, cs] * ypre, axis=0, keepdims=True)
            dyp = (dpooled[:, cs] * pool_scale[:, cs]).astype(MX)
            dyp_o[:, cs] = dyp
            dd_o[:, cs] = _dot_nt(dyp, w_pool[gi])

    outs = [_sds(n_tok, D_MODEL, MX), _sds(n_tok, D_MODEL, MX), _sds(n_tok, D_MODEL, MX), _sds(n_tok, 2 * D_MODEL, MX),
            _sds(n_tok, N_HEADS * V_HEAD, MX), _sds(n_tok, HEAD_SLOT, F32), _sds(n_tok, POOL_WIDTH, MX),
            _sds(n_tok, POOL_WIDTH, F32)]
    accs = [_sds(1, D_MODEL, F32), _sds(1, 2 * D_MODEL, F32), _sds(1, POOL_WIDTH, F32)]
    res = [w["g_post_mix"], w["w_out"], w["b_gate"], w["w_branch_attn"], w["w_branch_pool"], w["w_pool"], w["pool_scale"],
           w["head_sel"]]
    return _tok_call("bwd_mix", body, n_tok, tm, [dh1, y, a_br, p_br, gl, attn, d], res, outs, accs)


def _bwd_heads(q_ref, k_ref, v_ref, do_ref, lse_ref, dl_ref, st_s, dpt_s, keep, use, n_heads, qs):
    def products(h):
        hs = slice(h * HEAD_SLOT, (h + 1) * HEAD_SLOT)
        vs = slice(h * V_HEAD, (h + 1) * V_HEAD)
        st_s[h % 2, :, qs] = _dot_nt(k_ref[:, hs], q_ref[qs, hs])
        dpt_s[h % 2, :, qs] = _dot_nt(v_ref[:, vs], do_ref[qs, vs])

    products(0)
    for h in range(n_heads):
        if h + 1 < n_heads:
            products(h + 1)
        st = st_s[h % 2, :, qs]
        if keep is not None:
            st = jnp.where(keep, st, NEG)
        pt = jnp.exp2(st - lse_ref[h:h + 1, qs])
        use(h, pt, pt * (dpt_s[h % 2, :, qs] - dl_ref[h:h + 1, qs]))


HEAD_GROUP = 4


def _attn_bwd(q, k, kt, v, do, lse, delta, tq, tk, scatter=None):
    n_tok = q.shape[0]
    nq, ratio = n_tok // tq, tq // tk
    n_groups = N_HEADS // HEAD_GROUP
    gq, gv = HEAD_GROUP * HEAD_SLOT, HEAD_GROUP * V_HEAD
    qi, kj = _causal_pairs(nq, ratio, by_kv=True)

    n_pairs = qi.shape[0]

    def kern(qi_ref, kj_ref, q_ref, k_ref, kt_ref, v_ref, do_ref, lse_ref, dl_ref, *rest):
        if scatter is not None:
            s_hbm, dq_ref, dk_ref, dv_ref, r_hbm, dk_s, dv_s, st_s, dpt_s, send_sems, recv_sems = rest
        else:
            dq_ref, dk_ref, dv_ref, dk_s, dv_s, st_s, dpt_s = rest
        s_id = pl.program_id(1)
        i, j = qi_ref[s_id], kj_ref[s_id]
        if scatter is not None:
            group = pl.program_id(0)
            _scatter_steps(jnp.logical_and(group == 0, s_id == 0),
                           jnp.logical_and(group == n_groups - 1, s_id == n_pairs - 1), s_hbm, r_hbm, (send_sems, recv_sems))

        @pl.when(s_id == 0)
        def _():
            dq_ref[...] = jnp.zeros(dq_ref.shape, F32)

        def heads(first_query, masked):
            qs = slice(first_query, tq)
            cols = pl.ds(pl.multiple_of(i * tq + first_query, tk), tq - first_query)

            def use(h, pt, dst):
                hs = slice(h * HEAD_SLOT, (h + 1) * HEAD_SLOT)
                dv_s[h] += _dot(pt, do_ref[qs, h * V_HEAD:(h + 1) * V_HEAD])
                dk_s[:, hs] += _dot(dst, q_ref[qs, hs])
                dq_ref[hs, cols] += _dot(kt_ref[hs, :], dst)

            keep = _keep_t(tk, tq - first_query, 0) if masked else None
            _bwd_heads(q_ref, k_ref, v_ref, do_ref, lse_ref.at[0], dl_ref.at[0], st_s, dpt_s, keep, use, HEAD_GROUP, qs)

        @pl.when(j >= i * ratio)
        def _():
            dk_s[...] = jnp.zeros(dk_s.shape, F32)
            dv_s[...] = jnp.zeros(dv_s.shape, F32)

        for part in range(ratio):
            @pl.when(j == i * ratio + part)
            def _():
                heads(part * tk, True)

        @pl.when(j < i * ratio)
        def _():
            heads(0, False)

        @pl.when(i == nq - 1)
        def _():
            dk_ref[...] = (dk_s[...] * (1.0 / LOG2E)).astype(dk_ref.dtype)
            for h in range(HEAD_GROUP):
                dv_ref[:, h * V_HEAD:(h + 1) * V_HEAD] = dv_s[h].astype(dv_ref.dtype)

    at_q = lambda g, s, qi, kj: (qi[s], g)
    at_k = lambda g, s, qi, kj: (kj[s], g)
    at_kt = lambda g, s, qi, kj: (g, kj[s])
    at_stat = lambda g, s, qi, kj: (g, 0, qi[s])
    in_specs = [pl.BlockSpec((tq, gq), at_q), pl.BlockSpec((tk, gq), at_k), pl.BlockSpec((gq, tk), at_kt),
                pl.BlockSpec((tk, gv), at_k), pl.BlockSpec((tq, gv), at_q),
                pl.BlockSpec((1, HEAD_GROUP, tq), at_stat), pl.BlockSpec((1, HEAD_GROUP, tq), at_stat)]
    out_specs = [pl.BlockSpec((gq, n_tok), lambda g, s, qi, kj: (g, 0), pipeline_mode=pl.Buffered(1)),
                 pl.BlockSpec((tk, gq), at_k), pl.BlockSpec((tk, gv), at_k)]
    out_shape = [_sds(QK_WIDTH, n_tok, F32), _sds(n_tok, QK_WIDTH, MX), _sds(n_tok, N_HEADS * V_HEAD, MX)]
    scratch = [pltpu.VMEM((tk, gq), F32), pltpu.VMEM((HEAD_GROUP, tk, V_HEAD), F32),
               pltpu.VMEM((2, tk, tq), F32), pltpu.VMEM((2, tk, tq), F32)]
    stat3 = lambda a: a.reshape(n_groups, HEAD_GROUP, n_tok)
    operands = [qi, kj, q, k, kt, v, do, stat3(lse), stat3(delta)]
    if scatter is not None:
        operands.append(scatter)
        in_specs.append(_HBM)
        out_specs.append(_HBM)
        out_shape.append(jax.ShapeDtypeStruct((N_DEVICES, scatter.shape[1] // 2, PACK_COLS), scatter.dtype))
        scratch += [pltpu.SemaphoreType.DMA((N_DEVICES - 1,)), pltpu.SemaphoreType.DMA((N_DEVICES - 1,))]
    gs = pltpu.PrefetchScalarGridSpec(num_scalar_prefetch=2, grid=(n_groups, n_pairs), in_specs=in_specs,
                                      out_specs=out_specs, scratch_shapes=scratch)
    return pl.pallas_call(kern, name="attn_bwd", grid_spec=gs, out_shape=out_shape,
                          compiler_params=_params(("arbitrary", "arbitrary")))(*operands)


def _bwd_inproj(dq_t, dk, dv, dd, dgl, ps, x, dh1, cc, sa, w, tm):
    n_tok = x.shape[0]
    n_tiles = n_tok // tm
    last_halo = n_tok // HALO - 1
    halo_spec = pl.BlockSpec((HALO, POOL_WIDTH), lambda i: (jnp.minimum((i + 1) * (tm // HALO), last_halo), 0))

    def body(i, tin, res, tout, acc, scr):
        dq_ref, dk_ref, dv_ref, dd_ref, ddh_ref, dgl_ref, ps_ref, x_ref, dh1_ref, c_ref, s_ref = tin
        w_uq, g_q, w_k, w_v, e_mat, g_kv, w_in, g_pre = res
        dqu_o, dproj_o, dx_o = tout
        dgq_a, dgkv_a, dgpre_a = acc
        ext, = scr
        cc_, sa_ = c_ref[...], s_ref[...]
        for h in range(N_HEADS):
            hs = slice(h * HEAD_SLOT, (h + 1) * HEAD_SLOT)
            dqu_o[:, hs] = (_unrope(dq_ref[hs, :].T, cc_, sa_) * SCALE).astype(dqu_o.dtype)
        gq = g_q[...]
        _, qh, rq = _rms(ps_ref[:, :Q_LORA].astype(F32), gq)
        dqd, dg = _rms_bwd(qh, rq, gq, _dot_nt(dqu_o[...], w_uq[...]))
        dgq_a[...] += dg
        dproj_o[:, :Q_LORA] = dqd.astype(dproj_o.dtype)
        gkv = g_kv[...]
        _, kh, rk = _rms(ps_ref[:, Q_LORA:Q_LORA + KV_LORA].astype(F32), gkv)
        dkvd, dg = _rms_bwd(kh, rk, gkv, _dot_nt(dk_ref[...], w_k[...]) + _dot_nt(dv_ref[...], w_v[...]))
        dgkv_a[...] += dg
        dproj_o[:, Q_LORA:Q_LORA + KV_LORA] = dkvd.astype(dproj_o.dtype)
        dproj_o[:, Q_LORA + KV_LORA:SMALL_COLS] = _unrope(_dot_nt(dk_ref[...], e_mat[...]), cc_, sa_).astype(dproj_o.dtype)
        row = i * tm + lax.broadcasted_iota(jnp.int32, (tm + HALO, 1), 0)
        for gi, wdw in enumerate(POOL_WINDOWS):
            cs = slice(gi * POOL_GROUP, (gi + 1) * POOL_GROUP)
            inv = 1.0 / jnp.minimum(row + 1, wdw).astype(F32)
            ext[pl.ds(0, tm), cs] = dd_ref[:, cs] * inv[:tm]
            ext[pl.ds(tm, HALO), cs] = jnp.where(i < n_tiles - 1, ddh_ref[:, cs] * inv[tm:], 0.0)
            s = ext[pl.ds(0, tm), cs]
            for k_ in range(1, wdw):
                s = s + ext[pl.ds(k_, tm), cs]
            dproj_o[:, SMALL_COLS + gi * POOL_GROUP:SMALL_COLS + (gi + 1) * POOL_GROUP] = (s - dd_ref[:, cs]).astype(dproj_o.dtype)
        dproj_o[:, SMALL_COLS + POOL_WIDTH:] = dgl_ref[...]
        da = _dot_nt(dproj_o[...], w_in[...])
        gp = g_pre[...]
        _, xh, rx = _rms(x_ref[...], gp)
        dx, dg = _rms_bwd(xh, rx, gp, da)
        dgpre_a[...] += dg
        dx_o[...] = dh1_ref[...] + dx

    outs = [_sds(n_tok, QK_WIDTH, MX), _sds(n_tok, IN_PAD, MX), _sds(n_tok, D_MODEL, F32)]
    accs = [_sds(1, Q_LORA, F32), _sds(1, KV_LORA, F32), _sds(1, D_MODEL, F32)]
    res = [w["w_uq"], w["g_q"], w["w_k"], w["w_v"], w["e_mat"], w["g_kv"], w["w_in"], w["g_pre_mix"]]
    dq_spec = pl.BlockSpec((QK_WIDTH, tm), lambda i: (0, i))
    return _tok_call("bwd_inproj", body, n_tok, tm, [(dq_t, dq_spec), dk, dv, dd, (dd, halo_spec), dgl, ps, x, dh1, cc, sa], res, outs, accs,
                     scratch=[pltpu.VMEM((tm + HALO, POOL_WIDTH), F32)])


XTDY_TOKENS = 1024
XTDY_OUT_BYTES = 8 * 2**20
XTDY_IN_BYTES = 8 * 2**20


def _xtdy(name, x, dy, allreduce=None, square_x=False):
    n_tok, kk = x.shape
    nn = dy.shape[1]
    bk = kk
    while bk * nn * 4 > XTDY_OUT_BYTES and bk % 256 == 0:
        bk //= 2
    bt = min(XTDY_TOKENS, n_tok)
    while (2 * bt <= n_tok and n_tok % (2 * bt) == 0 and 2 * bt * nn * dy.dtype.itemsize <= XTDY_IN_BYTES
           and 2 * bt * bk * x.dtype.itemsize <= XTDY_IN_BYTES):
        bt *= 2

    grid = (kk // bk, n_tok // bt)

    def kern(x_ref, dy_ref, *rest):
        o_ref = rest[1] if allreduce is not None else rest[0]
        if allreduce is not None:
            g_ref, _, sum_ref, buf, send_sems, recv_sems = rest
            step = pl.program_id(0) * grid[1] + pl.program_id(1)
            _allreduce_steps(step == 0, step == grid[0] * grid[1] - 1, g_ref, sum_ref, buf, send_sems, recv_sems)

        @pl.when(pl.program_id(1) == 0)
        def _():
            o_ref[...] = jnp.zeros(o_ref.shape, F32)
        xv = x_ref[...]
        if square_x:
            xv = xv.astype(F32)
            xv = xv * xv
        o_ref[...] += _dot_tn(xv, dy_ref[...])

    operands = [x, dy]
    in_specs = [pl.BlockSpec((bt, bk), lambda a, t: (t, a)), pl.BlockSpec((bt, nn), lambda a, t: (t, 0))]
    out_specs = [pl.BlockSpec((bk, nn), lambda a, t: (a, 0))]
    out_shape = [_sds(kk, nn, F32)]
    scratch = []
    if allreduce is not None:
        vmem = pl.BlockSpec(memory_space=pltpu.VMEM)
        operands.append(allreduce)
        in_specs.append(vmem)
        out_specs.append(vmem)
        out_shape.append(jax.ShapeDtypeStruct(allreduce.shape, allreduce.dtype))
        scratch = [pltpu.VMEM((N_DEVICES,) + allreduce.shape, allreduce.dtype), pltpu.SemaphoreType.DMA((N_DEVICES - 1,)),
                   pltpu.SemaphoreType.DMA((N_DEVICES - 1,))]
    res = pl.pallas_call(kern, name=name, grid=grid, in_specs=in_specs, out_specs=out_specs, out_shape=out_shape,
                         scratch_shapes=scratch, compiler_params=_params(("arbitrary", "arbitrary")))(*operands)
    return res if allreduce is not None else res[0]


def _rope_tables(positions):
    inv_freq = ROPE_THETA ** (-jnp.arange(0, QK_ROPE, 2, dtype=F32) / QK_ROPE)
    ang_t = inv_freq[:, None] * positions.astype(F32)[None, :]
    cos_t, sin_t = lax.optimization_barrier((jnp.cos(ang_t), jnp.sin(ang_t)))
    cos, sin = cos_t.T, sin_t.T
    n_tok = positions.shape[0]
    ones, z64 = jnp.ones((n_tok, ROPE_LANE), F32), jnp.zeros((n_tok, ROPE_LANE), F32)
    z32 = jnp.zeros((n_tok, HEAD_SLOT - ROPE_LANE - QK_ROPE), F32)
    return jnp.concatenate([ones, cos, cos, z32], 1), jnp.concatenate([z64, -sin, sin, z32], 1)


def _kernel_weights(full):
    w_in, w_uq, w_ukv = full["w_in"], full["w_uq"], full["w_ukv"]
    c0 = Q_LORA + KV_LORA
    z = lambda n: jnp.zeros((D_MODEL, n), w_in.dtype)
    w = dict(full)
    w["w_in"] = jnp.concatenate([w_in[:, :c0], z(ROPE_LANE), w_in[:, c0:c0 + QK_ROPE], z(HEAD_SLOT - ROPE_LANE - QK_ROPE),
                                 w_in[:, c0 + QK_ROPE:]], 1)
    w["w_uq"] = jnp.pad(w_uq.reshape(Q_LORA, N_HEADS, QK_NOPE + QK_ROPE),
                        ((0, 0), (0, 0), (0, HEAD_SLOT - QK_NOPE - QK_ROPE))).reshape(Q_LORA, QK_WIDTH)
    kv = w_ukv.reshape(KV_LORA, N_HEADS, QK_NOPE + V_HEAD)
    w["w_k"] = jnp.pad(kv[:, :, :QK_NOPE], ((0, 0), (0, 0), (0, HEAD_SLOT - QK_NOPE))).reshape(KV_LORA, QK_WIDTH)
    w["w_v"] = kv[:, :, QK_NOPE:].reshape(KV_LORA, N_HEADS * V_HEAD)
    e = np.zeros((HEAD_SLOT, QK_WIDTH), np.float32)
    sel = np.zeros((N_HEADS * V_HEAD, HEAD_SLOT), np.float32)
    for h in range(N_HEADS):
        for r in range(QK_ROPE):
            e[ROPE_LANE + r, h * HEAD_SLOT + ROPE_LANE + r] = 1.0
        sel[h * V_HEAD:(h + 1) * V_HEAD, h] = 1.0
    w["e_mat"] = jnp.asarray(e, MX)
    w["w_kt"], w["e_t"] = w["w_k"].T, jnp.asarray(e.T, MX)
    pad = ((0, 0), (0, V_ROWS - V_HEAD), (0, 0))
    w["w_vt"] = jnp.pad(w["w_v"].T.reshape(N_HEADS, V_HEAD, KV_LORA), pad).reshape(N_HEADS * V_ROWS, KV_LORA)
    ones = np.zeros((N_HEADS, V_ROWS, 1), np.float32)
    ones[:, V_HEAD] = 1.0
    w["v_ones"] = jnp.asarray(ones.reshape(N_HEADS * V_ROWS, 1))
    w["head_sel"] = jnp.asarray(sel, MX)
    w["w_pool"] = full["w_pool"].astype(MX)
    return w


def _local_step(x, p, positions, target, full, mesh_place=None, packed_rest=None):
    n_tok = x.shape[0]
    tm = tm_mlp = min(TOKEN_TILE, n_tok)
    fwd_tile = [min(t, n_tok) for t in ATTN_FWD_TILE]
    bwd_tile = [min(t, n_tok) for t in ATTN_BWD_TILE]
    w = _kernel_weights(full)
    cc, sa = _rope_tables(positions)

    if mesh_place is None:
        a, ps, u, gl, qn, kvn, q, k, v, kt, vt = _fwd_inproj(x, cc, sa, w, tm)
    else:
        my_chip, core = mesh_place
        a, ps, u, gl, qn, kvn, q, k, v, kt, vt, gathered = _fwd_inproj(x, cc, sa, w, tm, gather=packed_rest)
        w.update(_unpack_full(gathered, packed_rest, my_chip, REST))
    attn, lse = _attn_fwd(q, k, vt, *fwd_tile)
    d, pooled, a_br, p_br, merged, y, h1 = _fwd_mix(x, u, gl, attn, w, tm)
    m, zr, f, h2 = _fwd_mlp(h1, w, tm_mlp)
    dh2, de, dzg, loss_cols, dg_ple = _ple_fwd_bwd(h2, p, target, w, tm)
    df, dz, dh1, dg_post_mlp, dg_pre_mlp = _bwd_mlp(dh2, f, h1, zr, w, tm_mlp)
    dy, da_br, dp_br, dgl, do, delta, dyp, dd, dg_post_mix, db_gate, dpool_scale = _bwd_mix(dh1, y, a_br, p_br, gl, attn, d, w, tm)
    grads = {"w_branch_attn": _xtdy("dw_ba", attn, da_br), "w_branch_pool": _xtdy("dw_bp", pooled, dp_br),
             "w_out": _xtdy("dw_out", merged, dy), "w_ff1": _xtdy("dw_ff1", m, dz), "w_ff2": _xtdy("dw_ff2", zr, df, square_x=True),
             "w_ple_proj": _xtdy("dw_pe", p, de), "w_ple_gate": _xtdy("dw_pg", h2, dzg)}
    delta_t = delta[:, :N_HEADS].T
    if mesh_place is None:
        travelling = None
        dq_t, dk, dv = _attn_bwd(q, k, kt, v, do, lse, delta_t, *bwd_tile)
    else:
        pieces = _pack_pieces(grads, REST, WIRE)
        dq_t, dk, dv, received = _attn_bwd(q, k, kt, v, do, lse, delta_t, *bwd_tile, scatter=pieces)
        travelling = (pieces, received)
        grads = {}
    dqu, dproj, dx, dg_q, dg_kv, dg_pre_mix = _bwd_inproj(dq_t, dk, dv, dd, dgl, ps, x, dh1, cc, sa, w, tm)

    g_uq = _xtdy("dw_uq", qn, dqu)
    g_k = _xtdy("dw_k", kvn, dk)
    g_v = _xtdy("dw_v", kvn, dv)
    g_pool = _xtdy("dw_pool", d, dyp)
    small = {"g_pre_mix": dg_pre_mix, "b_gate": db_gate, "g_q": dg_q, "g_kv": dg_kv, "pool_scale": dpool_scale,
             "g_post_mix": dg_post_mix, "g_pre_mlp": dg_pre_mlp, "g_post_mlp": dg_post_mlp, "g_ple": dg_ple,
             "w_pool": jnp.stack([g_pool[g * POOL_GROUP:(g + 1) * POOL_GROUP, g * POOL_GROUP:(g + 1) * POOL_GROUP]
                                  for g in range(len(POOL_WINDOWS))])}
    if mesh_place is None:
        g_in = _xtdy("dw_in", a, dproj)
    else:
        g_in, small_sum = _xtdy("dw_in", a, dproj, allreduce=_pack_small(small, loss_cols))
        small, loss_cols = _unpack_small(small_sum), small_sum[-1:]

    c0 = Q_LORA + KV_LORA
    grads.update(small)
    grads.update({
        "w_in": jnp.concatenate([g_in[:, :c0], g_in[:, c0 + ROPE_LANE:c0 + ROPE_LANE + QK_ROPE], g_in[:, SMALL_COLS:]], 1),
        "w_uq": g_uq.reshape(Q_LORA, N_HEADS, HEAD_SLOT)[:, :, :QK_NOPE + QK_ROPE].reshape(Q_LORA, N_HEADS * (QK_NOPE + QK_ROPE)),
        "w_ukv": jnp.concatenate([g_k.reshape(KV_LORA, N_HEADS, HEAD_SLOT)[:, :, :QK_NOPE],
                                  g_v.reshape(KV_LORA, N_HEADS, V_HEAD)], 2).reshape(KV_LORA, N_HEADS * (QK_NOPE + V_HEAD)),
    })
    return loss_cols, dx, grads, travelling


def _place():
    return lax.axis_index("x"), lax.axis_index("y"), lax.axis_index("c")


CHIP_FLIPS = ((1, 0), (0, 1), (1, 1))


def _flip(x, y, fx, fy):
    return (1 - x if fx else x), (1 - y if fy else y)


_HBM = pl.BlockSpec(memory_space=pl.ANY)


def _gather_copies(w_ref, out_ref, send_sems, recv_sems):
    half = w_ref.shape[0] // 2
    x, y, c = _place()
    my_chip = 2 * x + y
    sibling = (x, y, 1 - c)

    def half_of(chip, hc):
        return out_ref.at[chip, pl.ds(pl.multiple_of(hc * half, 16), half), :]

    src = w_ref.at[pl.ds(pl.multiple_of(c * half, 16), half), :]
    sends, landed, forwards, from_sibling = [], [], [], []
    for j, (fx, fy) in enumerate(CHIP_FLIPS):
        px, py = _flip(x, y, fx, fy)
        mine_there, theirs_here, theirs_other = half_of(my_chip, c), half_of(2 * px + py, c), half_of(2 * px + py, 1 - c)
        sends.append(pltpu.make_async_remote_copy(src, mine_there, send_sems.at[j], recv_sems.at[j],
                                                  device_id=(px, py, c), device_id_type=MESH))
        landed.append(pltpu.make_async_remote_copy(src, theirs_here, send_sems.at[j], recv_sems.at[j],
                                                   device_id=(px, py, c), device_id_type=MESH))
        forwards.append(pltpu.make_async_remote_copy(theirs_here, theirs_here, send_sems.at[3 + j], recv_sems.at[3 + j],
                                                     device_id=sibling, device_id_type=MESH))
        from_sibling.append(pltpu.make_async_remote_copy(theirs_other, theirs_other, send_sems.at[3 + j],
                                                         recv_sems.at[3 + j], device_id=sibling, device_id_type=MESH))
    return sends, landed, forwards, from_sibling


def _gather_steps(i, n_steps, w_ref, out_ref, sems):
    sends, landed, forwards, from_sibling = _gather_copies(w_ref, out_ref, *sems)

    @pl.when(i == 0)
    def _():
        for cp in sends:
            cp.start()

    @pl.when(i == (3 * n_steps) // 4)
    def _():
        for arrived, fwd in zip(landed, forwards):
            arrived.wait_recv()
            fwd.start()

    @pl.when(i == n_steps - 1)
    def _():
        for cp in from_sibling:
            cp.wait_recv()
        for cp in sends + forwards:
            cp.wait_send()


def _allgather_shards(wp):
    def body(w_ref, out_ref, send_sems, recv_sems):
        sends, landed, forwards, from_sibling = _gather_copies(w_ref, out_ref, send_sems, recv_sems)
        for cp in sends:
            cp.start()
        for arrived, fwd in zip(landed, forwards):
            arrived.wait_recv()
            fwd.start()
        for cp in from_sibling:
            cp.wait_recv()
        for cp in sends + forwards:
            cp.wait_send()

    return pl.pallas_call(
        body, name="allgather_shards", out_shape=jax.ShapeDtypeStruct((N_CHIPS,) + wp.shape, wp.dtype),
        in_specs=[_HBM], out_specs=_HBM,
        scratch_shapes=[pltpu.SemaphoreType.DMA((6,)), pltpu.SemaphoreType.DMA((6,))],
    )(wp)


def _exchange_halves(g):
    rows = g.shape[1]
    half = rows // 2

    def body(g_ref, r_ref, send_sem, recv_sem):
        x, y, c = _place()
        src = g_ref.at[:, pl.ds(pl.multiple_of((1 - c) * half, 8), half), :]
        cp = pltpu.make_async_remote_copy(src, r_ref, send_sem, recv_sem, device_id=(x, y, 1 - c), device_id_type=MESH)
        cp.start()
        cp.wait()

    return pl.pallas_call(
        body, name="exchange_halves", out_shape=jax.ShapeDtypeStruct((N_CHIPS, half, PACK_COLS), g.dtype),
        in_specs=[_HBM], out_specs=_HBM, scratch_shapes=[pltpu.SemaphoreType.DMA, pltpu.SemaphoreType.DMA],
    )(g)


def _add_halves(g, r, c):
    rows = g.shape[1]
    half = rows // 2
    br = REDUCE_ROWS
    nb = half // br

    def kern(c_ref, g_ref, r_ref, o_ref):
        o_ref[...] = (g_ref[...] + r_ref[...]).astype(o_ref.dtype)

    gs = pltpu.PrefetchScalarGridSpec(
        num_scalar_prefetch=1, grid=(N_CHIPS, nb),
        in_specs=[pl.BlockSpec((1, br, PACK_COLS), lambda k, t, c: (k, c[0] * nb + t, 0)),
                  pl.BlockSpec((1, br, PACK_COLS), lambda k, t, c: (k, t, 0))],
        out_specs=pl.BlockSpec((1, br, PACK_COLS), lambda k, t, c: (k, t, 0)))
    return pl.pallas_call(kern, name="add_halves", grid_spec=gs,
                          out_shape=jax.ShapeDtypeStruct((N_CHIPS, half, PACK_COLS), WIRE),
                          compiler_params=_params(("arbitrary", "arbitrary")))(c.reshape(1), g, r)


def _scatter_copies(s_ref, r_ref, send_sems, recv_sems):
    x, y, c = _place()
    my_chip = 2 * x + y
    sends, arrivals = [], []
    for j, (fx, fy) in enumerate(CHIP_FLIPS):
        px, py = _flip(x, y, fx, fy)
        slot = r_ref.at[2 * px + py]
        sends.append(pltpu.make_async_remote_copy(s_ref.at[2 * px + py], r_ref.at[my_chip], send_sems.at[j], recv_sems.at[j],
                                                  device_id=(px, py, c), device_id_type=MESH))
        arrivals.append(pltpu.make_async_remote_copy(slot, slot, send_sems.at[j], recv_sems.at[j],
                                                     device_id=(px, py, c), device_id_type=MESH))
    return sends, arrivals


N_DEVICES = 8


def _peer(x, y, c, f):
    px, py = _flip(x, y, f & 4, f & 2)
    return px, py, (1 - c if f & 1 else c)


def _scatter_all_copies(p_ref, r_ref, send_sems, recv_sems):
    half = p_ref.shape[1] // 2
    x, y, c = _place()
    me = 4 * x + 2 * y + c
    sends, arrivals = [], []
    for f in range(1, N_DEVICES):
        px, py, pc = _peer(x, y, c, f)
        theirs = p_ref.at[2 * px + py, pl.ds(pl.multiple_of(pc * half, 16), half), :]
        slot = r_ref.at[4 * px + 2 * py + pc]
        sends.append(pltpu.make_async_remote_copy(theirs, r_ref.at[me], send_sems.at[f - 1], recv_sems.at[f - 1],
                                                  device_id=(px, py, pc), device_id_type=MESH))
        arrivals.append(pltpu.make_async_remote_copy(slot, slot, send_sems.at[f - 1], recv_sems.at[f - 1],
                                                     device_id=(px, py, pc), device_id_type=MESH))
    return sends, arrivals


def _scatter_steps(first, last, p_ref, r_ref, sems):
    sends, arrivals = _scatter_all_copies(p_ref, r_ref, *sems)

    @pl.when(first)
    def _():
        for cp in sends:
            cp.start()

    @pl.when(last)
    def _():
        for cp in arrivals:
            cp.wait_recv()
        for cp in sends:
            cp.wait_send()


def _scatter_pieces(s):
    def body(s_ref, r_ref, send_sems, recv_sems):
        sends, arrivals = _scatter_copies(s_ref, r_ref, send_sems, recv_sems)
        for cp in sends:
            cp.start()
        for cp in arrivals:
            cp.wait_recv()
        for cp in sends:
            cp.wait_send()

    return pl.pallas_call(
        body, name="scatter_pieces", out_shape=jax.ShapeDtypeStruct(s.shape, s.dtype), in_specs=[_HBM], out_specs=_HBM,
        scratch_shapes=[pltpu.SemaphoreType.DMA((3,)), pltpu.SemaphoreType.DMA((3,))],
    )(s)


def _sum_pieces(r, mine, slot):
    slots, half = r.shape[:2]
    br = REDUCE_ROWS

    def kern(slot_ref, r_ref, m_ref, o_ref):
        total = None
        for k in range(slots):
            term = jnp.where(slot_ref[0] == k, m_ref[0], r_ref[k]).astype(F32)
            total = term if total is None else total + term
        o_ref[...] = total

    gs = pltpu.PrefetchScalarGridSpec(
        num_scalar_prefetch=1, grid=(half // br,),
        in_specs=[pl.BlockSpec((slots, br, PACK_COLS), lambda t, s: (0, t, 0)),
                  pl.BlockSpec((1, br, PACK_COLS), lambda t, s: (0, t, 0))],
        out_specs=pl.BlockSpec((br, PACK_COLS), lambda t, s: (t, 0)))
    return pl.pallas_call(kern, name="sum_pieces", grid_spec=gs, out_shape=_sds(half, PACK_COLS, F32),
                          compiler_params=_params(("arbitrary",)))(slot.reshape(1), r, mine)


def _join_halves(f):
    def body(f_ref, o_ref, send_sem, recv_sem):
        x, y, c = _place()
        cp = pltpu.make_async_remote_copy(f_ref, o_ref, send_sem, recv_sem, device_id=(x, y, 1 - c), device_id_type=MESH)
        cp.start()
        cp.wait()

    return pl.pallas_call(
        body, name="join_halves", out_shape=jax.ShapeDtypeStruct(f.shape, f.dtype), in_specs=[_HBM], out_specs=_HBM,
        scratch_shapes=[pltpu.SemaphoreType.DMA, pltpu.SemaphoreType.DMA],
    )(f)


def _allreduce_steps(first, last, g_ref, o_ref, buf, send_sems, recv_sems):
    x, y, c = _place()
    me = 4 * x + 2 * y + c
    sends, arrivals = [], []
    for f in range(1, N_DEVICES):
        px, py, pc = _peer(x, y, c, f)
        slot = buf.at[4 * px + 2 * py + pc]
        sends.append(pltpu.make_async_remote_copy(g_ref, buf.at[me], send_sems.at[f - 1], recv_sems.at[f - 1],
                                                  device_id=(px, py, pc), device_id_type=MESH))
        arrivals.append(pltpu.make_async_remote_copy(slot, slot, send_sems.at[f - 1], recv_sems.at[f - 1],
                                                     device_id=(px, py, pc), device_id_type=MESH))

    @pl.when(first)
    def _():
        buf[me] = g_ref[...]
        for cp in sends:
            cp.start()

    @pl.when(last)
    def _():
        for cp in arrivals:
            cp.wait_recv()
        for cp in sends:
            cp.wait_send()
        total = buf[0]
        for k in range(1, N_DEVICES):
            total = total + buf[k]
        o_ref[...] = total


def _adamw_update(g_ref, w_ref, m_ref, v_ref, d_o, m_o, v_o):
    c1 = 1.0 - ADAM_B1 ** ADAM_STEP
    c2 = 1.0 - ADAM_B2 ** ADAM_STEP
    g_ = g_ref[...]
    m_new = ADAM_B1 * m_ref[...] + (1.0 - ADAM_B1) * g_
    v_new = ADAM_B2 * v_ref[...] + (1.0 - ADAM_B2) * (g_ * g_)
    m_o[...] = m_new
    v_o[...] = v_new
    d_o[...] = -ADAM_LR * ((m_new / c1) / (jnp.sqrt(v_new / c2) + ADAM_EPS) + ADAM_WD * w_ref[...])


ADAMW_ROWS = 256


def _adamw(name, g, w, m, v):
    _, rows, cols = w.shape
    br = int(np.gcd(ADAMW_ROWS, rows))

    def kern(*refs):
        _adamw_update(*refs)

    spec = pl.BlockSpec((1, br, cols), lambda t: (0, t, 0))
    out = jax.ShapeDtypeStruct(w.shape, F32)
    return pl.pallas_call(kern, name="adamw_" + name, grid=(rows // br,), in_specs=[spec] * 4, out_specs=[spec] * 3,
                          out_shape=[out, out, out], compiler_params=_params(("arbitrary",)))(g, w, m, v)


def _adamw_small(gs, ws, ms, vs):
    n = len(gs)

    def kern(*refs):
        ins, outs = refs[:4 * n], refs[4 * n:]
        for k in range(n):
            _adamw_update(ins[k], ins[n + k], ins[2 * n + k], ins[3 * n + k], outs[k], outs[n + k], outs[2 * n + k])

    vmem = pl.BlockSpec(memory_space=pltpu.VMEM)
    out = [jax.ShapeDtypeStruct(w.shape, F32) for w in ws]
    res = pl.pallas_call(kern, name="adamw_small", in_specs=[vmem] * (4 * n), out_specs=[vmem] * (3 * n),
                         out_shape=out * 3, compiler_params=pltpu.CompilerParams(vmem_limit_bytes=VMEM_LIMIT))(*gs, *ws, *ms, *vs)
    return [(res[k], res[n + k], res[2 * n + k]) for k in range(n)]


def _shard_rows(shape, axis):
    k, n = shape
    return (k * n // N_CHIPS) // PACK_COLS


def _group(names):
    entries = [e for e in SHARDED if e[0] in names]
    used = sum(_shard_rows(shape, axis) for _, shape, axis in entries)
    return entries, -(-used // (2 * REDUCE_ROWS)) * 2 * REDUCE_ROWS


def _pack_shards(shards, names, dtype):
    entries, rows = _group(names)
    parts = [shards[name].astype(dtype).reshape(-1, PACK_COLS) for name, _, _ in entries]
    used = sum(p.shape[0] for p in parts)
    if rows > used:
        parts.append(jnp.zeros((rows - used, PACK_COLS), dtype))
    return jnp.concatenate(parts, 0)


def _unpack_shards(packed, names):
    out, r0 = {}, 0
    for name, (k, n), axis in _group(names)[0]:
        nr = _shard_rows((k, n), axis)
        shape = (k // N_CHIPS, n) if axis == 0 else (k, n // N_CHIPS)
        out[name] = packed[r0:r0 + nr].reshape(shape)
        r0 += nr
    return out


def _unpack_full(gathered, own, my_chip, names):
    chip = lax.broadcasted_iota(jnp.int32, (N_CHIPS, 1, 1), 0)
    gathered = jnp.where(chip == my_chip, own[None], gathered)
    out, r0 = {}, 0
    for name, (k, n), axis in _group(names)[0]:
        nr = _shard_rows((k, n), axis)
        part = gathered[:, r0:r0 + nr]
        if axis == 0:
            out[name] = part.reshape(k, n)
        else:
            out[name] = part.reshape(N_CHIPS, k, n // N_CHIPS).transpose(1, 0, 2).reshape(k, n)
        r0 += nr
    return out


def _pack_pieces(grads, names, dtype=F32):
    entries, rows = _group(names)
    parts = []
    for name, (k, n), axis in entries:
        g = grads[name].astype(dtype)
        if axis == 0:
            parts.append(g.reshape(N_CHIPS, -1, PACK_COLS))
        else:
            parts.append(g.reshape(k, N_CHIPS, n // N_CHIPS).transpose(1, 0, 2).reshape(N_CHIPS, -1, PACK_COLS))
    used = sum(p.shape[1] for p in parts)
    if rows > used:
        parts.append(jnp.zeros((N_CHIPS, rows - used, PACK_COLS), dtype))
    return jnp.concatenate(parts, 1)


def _pack_small(vals, last_row):
    flat = jnp.concatenate([vals[name].astype(F32).reshape(-1) for name, _ in SMALL])
    spare = jnp.zeros(((SMALL_ROWS - 1) * PACK_COLS - flat.shape[0],), F32)
    return jnp.concatenate([flat, spare, last_row.reshape(-1)]).reshape(SMALL_ROWS, PACK_COLS)


def _unpack_small(packed):
    flat, out, o = packed.reshape(-1), {}, 0
    for name, shape in SMALL:
        n = int(np.prod(shape))
        out[name] = flat[o:o + n].reshape(shape)
        o += n
    return out


def kernel(x, p, positions, g_pre_mix, w_in, b_gate, g_q, w_uq, g_kv, w_ukv, w_pool, pool_scale, w_branch_attn, w_branch_pool, w_out, g_post_mix, g_pre_mlp, w_ff1, w_ff2, g_post_mlp, w_ple_proj, w_ple_gate, g_ple, loss_target, m_g_pre_mix, m_w_in, m_b_gate, m_g_q, m_w_uq, m_g_kv, m_w_ukv, m_w_pool, m_pool_scale, m_w_branch_attn, m_w_branch_pool, m_w_out, m_g_post_mix, m_g_pre_mlp, m_w_ff1, m_w_ff2, m_g_post_mlp, m_w_ple_proj, m_w_ple_gate, m_g_ple, v_g_pre_mix, v_w_in, v_b_gate, v_g_q, v_w_uq, v_g_kv, v_w_ukv, v_w_pool, v_pool_scale, v_w_branch_attn, v_w_branch_pool, v_w_out, v_g_post_mix, v_g_pre_mlp, v_w_ff1, v_w_ff2, v_g_post_mlp, v_w_ple_proj, v_w_ple_gate, v_g_ple):
    given = dict(locals())
    weights = {n: given[n] for n in WEIGHT_ORDER}
    moments_m = {n: given["m_" + n] for n in WEIGHT_ORDER}
    moments_v = {n: given["v_" + n] for n in WEIGHT_ORDER}
    c = lax.axis_index("c")

    big_w = {name: weights[name][0] for name, _, _ in SHARDED}
    my_chip = 2 * lax.axis_index("x") + lax.axis_index("y")
    packed_first = _pack_shards(big_w, FIRST, MX)
    full = _unpack_full(_allgather_shards(packed_first), packed_first, my_chip, FIRST)
    for name, _ in SMALL:
        full[name] = weights[name][0] if name == "w_pool" else weights[name]

    loss_cols, dx, grads, (pieces_rest, received_rest) = _local_step(
        x[0], p[0, 0], positions[0], loss_target[0], full, (my_chip, c), _pack_shards(big_w, REST, MX))
    loss = 0.5 * jnp.sum(loss_cols) / D_MODEL

    def finish(received, mine, slot):
        reduced = _sum_pieces(received, mine, slot)
        theirs = _join_halves(reduced)
        return jnp.where(c == 0, jnp.concatenate([reduced, theirs]), jnp.concatenate([theirs, reduced]))

    pieces = _pack_pieces(grads, FIRST)
    sent = _add_halves(pieces, _exchange_halves(pieces), c)
    mine = lax.dynamic_slice(sent, (my_chip, 0, 0), (1,) + sent.shape[1:])
    shards = _unpack_shards(finish(_scatter_pieces(sent), mine, my_chip), FIRST)
    half = received_rest.shape[1]
    mine = lax.dynamic_slice(pieces_rest, (my_chip, c * half, 0), (1, half, PACK_COLS))
    shards.update(_unpack_shards(finish(received_rest, mine, 2 * my_chip + c), REST))

    out = {}
    for name, g in shards.items():
        out[name] = (g[None], *_adamw(name, g[None], weights[name], moments_m[name], moments_v[name]))
    small_g = {n: grads[n] for n, _ in SMALL}
    names = [n for n, _ in SMALL]
    updates = _adamw_small([small_g[n] for n in names], [weights[n] for n in names], [moments_m[n] for n in names],
                           [moments_v[n] for n in names])
    for n, upd in zip(names, updates):
        out[n] = (small_g[n], *upd)
    return (loss, dx[None], *[out[n][k] for k in range(4) for n in WEIGHT_ORDER])
```

```python
import numpy as np
import jax
import jax.numpy as jnp
from jax import lax
from jax.experimental import pallas as pl
from jax.experimental.pallas import tpu as pltpu

F32 = jnp.float32
MX = jnp.bfloat16
WIRE = jnp.bfloat16

D_MODEL = 1024
N_HEADS = 8
QK_NOPE = 64
QK_ROPE = 32
V_HEAD = 64
Q_LORA = 384
KV_LORA = 256
POOL_WINDOWS = (2, 4, 8, 16)
POOL_GROUP = 128
POOL_WIDTH = 512
D_FF = 4096
ROPE_THETA = 10000.0
EPS = 1e-6
HEAD_SLOT = 128
QK_WIDTH = N_HEADS * HEAD_SLOT
ROPE_LANE = 64
SMALL_COLS = Q_LORA + KV_LORA + HEAD_SLOT
IN_PAD = SMALL_COLS + POOL_WIDTH + 2 * D_MODEL
SCALE = (QK_NOPE + QK_ROPE) ** -0.5
LOG2E = 1.4426950408889634
NEG = -1e30
HALO = 16

ADAM_LR = 0.001
ADAM_B1 = 0.9
ADAM_B2 = 0.999
ADAM_EPS = 1e-08
ADAM_WD = 0.01
ADAM_STEP = 10

VMEM_LIMIT = 56 * 2**20
TOKEN_TILE = 512
MESH = pl.DeviceIdType.MESH

SHARDED = (
    ("w_in", (1024, 3232), 1),
    ("w_uq", (384, 768), 1),
    ("w_ukv", (256, 1024), 1),
    ("w_branch_attn", (512, 1024), 1),
    ("w_branch_pool", (512, 1024), 1),
    ("w_out", (1024, 1024), 0),
    ("w_ff1", (1024, 4096), 1),
    ("w_ff2", (4096, 1024), 0),
    ("w_ple_proj", (256, 1024), 1),
    ("w_ple_gate", (1024, 1024), 0),
)
SMALL = (
    ("g_pre_mix", (1, 1024)),
    ("b_gate", (1, 2048)),
    ("g_q", (1, 384)),
    ("g_kv", (1, 256)),
    ("w_pool", (1, 4, 128, 128)),
    ("pool_scale", (1, 512)),
    ("g_post_mix", (1, 1024)),
    ("g_pre_mlp", (1, 1024)),
    ("g_post_mlp", (1, 1024)),
    ("g_ple", (1, 1024)),
)
WEIGHT_ORDER = ("g_pre_mix", "w_in", "b_gate", "g_q", "w_uq", "g_kv", "w_ukv", "w_pool", "pool_scale", "w_branch_attn",
                "w_branch_pool", "w_out", "g_post_mix", "g_pre_mlp", "w_ff1", "w_ff2", "g_post_mlp", "w_ple_proj",
                "w_ple_gate", "g_ple")
N_CHIPS = 4
PACK_COLS = 1024
REDUCE_ROWS = 160
SMALL_ROWS = 80
FIRST = ("w_in", "w_uq", "w_ukv")
REST = tuple(name for name, _, _ in SHARDED if name not in FIRST)


def _dot(a, b):
    return jnp.dot(a.astype(MX), b.astype(MX), preferred_element_type=F32)


def _dot_nt(a, b):
    return lax.dot_general(a.astype(MX), b.astype(MX), (((1,), (1,)), ((), ())), preferred_element_type=F32)


def _dot_tn(a, b):
    return lax.dot_general(a.astype(MX), b.astype(MX), (((0,), (0,)), ((), ())), preferred_element_type=F32)


def _sig(x):
    return 1.0 / (1.0 + jnp.exp(-x))


def _rms(x, g):
    r = lax.rsqrt(jnp.mean(x * x, axis=1, keepdims=True) + EPS)
    xh = x * r
    return xh * g, xh, r


def _rms_bwd(xh, r, g, dy):
    dxn = dy * g
    dx = r * (dxn - xh * jnp.mean(dxn * xh, axis=1, keepdims=True))
    return dx, jnp.sum(dy * xh, axis=0, keepdims=True)


def _rot_half(v):
    lane = lax.broadcasted_iota(jnp.int32, v.shape, 1)
    return jnp.where(lane < ROPE_LANE + QK_ROPE // 2, pltpu.roll(v, HEAD_SLOT - QK_ROPE // 2, 1), pltpu.roll(v, QK_ROPE // 2, 1))


def _rope(v, cc, sa):
    return v * cc + _rot_half(v) * sa


def _unrope(v, cc, sa):
    return v * cc - _rot_half(v) * sa


def _params(sem):
    return pltpu.CompilerParams(dimension_semantics=sem, vmem_limit_bytes=VMEM_LIMIT)


def _tok_call(name, body, n_tok, tm, tiled, resident, outs, accs=(), scratch=(), exchange=None):
    def as_pair(t):
        if isinstance(t, tuple):
            return t
        return t, pl.BlockSpec((tm, t.shape[1]), lambda i: (i, 0))
    tiled = [as_pair(t) for t in tiled]
    outs = [as_pair(o) for o in outs]
    res_specs = [pl.BlockSpec(r.shape, lambda i, nd=r.ndim: (0,) * nd, pipeline_mode=pl.Buffered(1)) for r in resident]
    out_specs = [s for _, s in outs] + [pl.BlockSpec(a.shape, lambda i: (0, 0)) for a in accs]
    n_t, n_r, n_o, n_a, n_s = len(tiled), len(resident), len(outs), len(accs), len(scratch)
    n_steps = n_tok // tm
    operands = [a for a, _ in tiled] + list(resident)
    in_specs = [s for _, s in tiled] + res_specs
    out_shape = [o for o, _ in outs] + list(accs)
    scratch = list(scratch)
    if exchange is not None:
        ex_in, ex_out, ex_sems, ex_steps = exchange
        operands.append(ex_in)
        in_specs.append(_HBM)
        out_shape.append(ex_out)
        out_specs.append(_HBM)
        scratch += list(ex_sems)

    def kern(*refs):
        refs = list(refs)
        n_in = n_t + n_r + (exchange is not None)
        n_out = n_o + n_a + (exchange is not None)
        tin, res = refs[:n_t], refs[n_t:n_t + n_r]
        tout = refs[n_in:n_in + n_o]
        acc = refs[n_in + n_o:n_in + n_o + n_a]
        scr = refs[n_in + n_out:n_in + n_out + n_s]
        i = pl.program_id(0)
        if exchange is not None:
            ex_steps(i, n_steps, refs[n_in - 1], refs[n_in + n_out - 1], refs[n_in + n_out + n_s:])

        @pl.when(i == 0)
        def _():
            for a in acc:
                a[...] = jnp.zeros(a.shape, a.dtype)
        body(i, tin, res, tout, acc, scr)

    return pl.pallas_call(
        kern, name=name, grid=(n_steps,), in_specs=in_specs, out_specs=out_specs,
        out_shape=out_shape, scratch_shapes=scratch, compiler_params=_params(("arbitrary",)),
    )(*operands)


def _sds(rows, cols, dtype):
    return jax.ShapeDtypeStruct((rows, cols), dtype)


def _fwd_inproj(x, cc, sa, w, tm, gather=None):
    n_tok = x.shape[0]

    def body(i, tin, res, tout, acc, scr):
        x_ref, c_ref, s_ref = tin
        g_pre, w_in, g_q, w_uq, g_kv, w_k, w_v, e_mat, w_kt, e_t, w_vt, v_ones = res
        a_o, ps_o, u_o, gl_o, qn_o, kvn_o, q_o, k_o, v_o, kt_o, vt_o = tout
        a = _rms(x_ref[...], g_pre[...])[0].astype(MX)
        a_o[...] = a
        ps = _dot(a, w_in[:, :SMALL_COLS])
        ps_o[...] = ps.astype(ps_o.dtype)
        u_o[...] = _dot(a, w_in[:, SMALL_COLS:SMALL_COLS + POOL_WIDTH]).astype(u_o.dtype)
        gl_o[...] = _dot(a, w_in[:, SMALL_COLS + POOL_WIDTH:]).astype(gl_o.dtype)
        cc_, sa_ = c_ref[...], s_ref[...]
        qn = _rms(ps[:, :Q_LORA], g_q[...])[0].astype(MX)
        qn_o[...] = qn
        q = _dot(qn, w_uq[...])
        for h in range(N_HEADS):
            hs = slice(h * HEAD_SLOT, (h + 1) * HEAD_SLOT)
            q_o[:, hs] = (_rope(q[:, hs], cc_, sa_) * (SCALE * LOG2E)).astype(q_o.dtype)
        kvn = _rms(ps[:, Q_LORA:Q_LORA + KV_LORA], g_kv[...])[0].astype(MX)
        kvn_o[...] = kvn
        kr = _rope(ps[:, Q_LORA + KV_LORA:], cc_, sa_)
        k_o[...] = (_dot(kvn, w_k[...]) + _dot(kr, e_mat[...])).astype(k_o.dtype)
        v_o[...] = _dot(kvn, w_v[...]).astype(v_o.dtype)
        kt_o[...] = (_dot_nt(w_kt[...], kvn) + _dot_nt(e_t[...], kr)).astype(kt_o.dtype)
        vt_o[...] = (_dot_nt(w_vt[...], kvn) + v_ones[...]).astype(vt_o.dtype)

    outs = [_sds(n_tok, D_MODEL, MX), _sds(n_tok, SMALL_COLS, MX), _sds(n_tok, POOL_WIDTH, MX), _sds(n_tok, 2 * D_MODEL, MX),
            _sds(n_tok, Q_LORA, MX), _sds(n_tok, KV_LORA, MX), _sds(n_tok, QK_WIDTH, MX), _sds(n_tok, QK_WIDTH, MX),
            _sds(n_tok, N_HEADS * V_HEAD, MX),
            (_sds(QK_WIDTH, n_tok, MX), pl.BlockSpec((QK_WIDTH, tm), lambda i: (0, i))),
            (_sds(N_HEADS * V_ROWS, n_tok, MX), pl.BlockSpec((N_HEADS * V_ROWS, tm), lambda i: (0, i)))]
    res = [w["g_pre_mix"], w["w_in"], w["g_q"], w["w_uq"], w["g_kv"], w["w_k"], w["w_v"], w["e_mat"], w["w_kt"], w["e_t"],
           w["w_vt"], w["v_ones"]]
    exchange = None
    if gather is not None:
        gathered = jax.ShapeDtypeStruct((N_CHIPS,) + gather.shape, gather.dtype)
        exchange = (gather, gathered, [pltpu.SemaphoreType.DMA((6,)), pltpu.SemaphoreType.DMA((6,))], _gather_steps)
    return _tok_call("fwd_inproj", body, n_tok, tm, [x, cc, sa], res, outs, exchange=exchange)


def _causal_pairs(nq, ratio, by_kv):
    if by_kv:
        pairs = [(i, j) for j in range(nq * ratio) for i in range(j // ratio, nq)]
    else:
        pairs = [(i, j) for i in range(nq) for j in range((i + 1) * ratio)]
    return (jnp.asarray(np.array([p[0] for p in pairs], np.int32)), jnp.asarray(np.array([p[1] for p in pairs], np.int32)))


def _keep_t(tk, tq, off):
    return lax.broadcasted_iota(jnp.int32, (tk, tq), 0) + off <= lax.broadcasted_iota(jnp.int32, (tk, tq), 1)


ATTN_DIAG_KEYS = 512
ATTN_FWD_TILE = (1024, 1024)
ATTN_BWD_TILE = (1024, 512)
V_ROWS = 80


def _attn_fwd(q, k, vt, tq, tk):
    n_tok = q.shape[0]
    nq, ratio = n_tok // tq, tq // tk
    qi, kj = _causal_pairs(nq, ratio, by_kv=False)

    def kern(qi_ref, kj_ref, q_ref, k_ref, vt_ref, o_ref, lse_ref, m_s, acc_s, st_s):
        s_id = pl.program_id(0)
        i, j = qi_ref[s_id], kj_ref[s_id]

        @pl.when(j == 0)
        def _():
            m_s[...] = jnp.full(m_s.shape, NEG, F32)
            acc_s[...] = jnp.zeros(acc_s.shape, F32)

        def heads(ks, qs, masked):
            n_keys = ks.stop - ks.start
            keep = _keep_t(n_keys, qs.stop - qs.start, 0) if masked else None

            def scores(h):
                hs = slice(h * HEAD_SLOT, (h + 1) * HEAD_SLOT)
                return _dot_nt(k_ref[ks, hs], q_ref[qs, hs])

            st_s[0, :n_keys, qs] = scores(0)
            for h in range(N_HEADS):
                if h + 1 < N_HEADS:
                    st_s[(h + 1) % 2, :n_keys, qs] = scores(h + 1)
                st = st_s[h % 2, :n_keys, qs]
                if masked:
                    st = jnp.where(keep, st, NEG)
                m_old = m_s[h, :, qs]
                m_new = jnp.maximum(m_old, jnp.max(st, axis=0, keepdims=True))
                pt = jnp.exp2(st - m_new)
                acc_s[h, :, qs] = (jnp.exp2(m_old - m_new) * acc_s[h, :, qs]
                                   + _dot(vt_ref[h * V_ROWS:(h + 1) * V_ROWS, ks], pt))
                m_s[h, :, qs] = m_new

        @pl.when(j < i * ratio)
        def _():
            heads(slice(0, tk), slice(0, tq), False)

        sub = min(tk, ATTN_DIAG_KEYS)
        for part in range(ratio):
            @pl.when(j == i * ratio + part)
            def _():
                for first in range(0, tk, sub):
                    heads(slice(first, first + sub), slice(part * tk + first, tq), True)

        @pl.when(j == (i + 1) * ratio - 1)
        def _():
            heads_out = []
            for h in range(N_HEADS):
                total = acc_s[h, V_HEAD:V_HEAD + 1, :]
                heads_out.append(acc_s[h, :V_HEAD, :] / total)
                lse_ref[h:h + 1, :] = m_s[h] + jnp.log2(total)
            o_ref[...] = jnp.concatenate(heads_out, 0).T.astype(o_ref.dtype)

    gs = pltpu.PrefetchScalarGridSpec(
        num_scalar_prefetch=2, grid=(qi.shape[0],),
        in_specs=[pl.BlockSpec((tq, QK_WIDTH), lambda s, qi, kj: (qi[s], 0)),
                  pl.BlockSpec((tk, QK_WIDTH), lambda s, qi, kj: (kj[s], 0)),
                  pl.BlockSpec((N_HEADS * V_ROWS, tk), lambda s, qi, kj: (0, kj[s]))],
        out_specs=[pl.BlockSpec((tq, N_HEADS * V_HEAD), lambda s, qi, kj: (qi[s], 0)),
                   pl.BlockSpec((N_HEADS, tq), lambda s, qi, kj: (0, qi[s]))],
        scratch_shapes=[pltpu.VMEM((N_HEADS, 1, tq), F32), pltpu.VMEM((N_HEADS, V_ROWS, tq), F32),
                        pltpu.VMEM((2, tk, tq), F32)])
    return pl.pallas_call(kern, name="attn_fwd", grid_spec=gs,
                          out_shape=[_sds(n_tok, N_HEADS * V_HEAD, MX), _sds(N_HEADS, n_tok, F32)],
                          compiler_params=_params(("arbitrary",)))(qi, kj, q, k, vt)


def _pool_windows(ext, i, tm, first_row):
    row = i * tm + lax.broadcasted_iota(jnp.int32, (tm, 1), 0)
    out = []
    for g, w in enumerate(POOL_WINDOWS):
        cs = slice(g * POOL_GROUP, (g + 1) * POOL_GROUP)
        s = ext[pl.ds(first_row, tm), cs]
        for k in range(1, w):
            s = s + ext[pl.ds(first_row - k, tm), cs]
        cnt = jnp.minimum(row + 1, w).astype(F32)
        out.append(s / cnt)
    return out


def _fwd_mix(x, u, gl, attn, w, tm):
    n_tok = x.shape[0]
    halo_spec = pl.BlockSpec((HALO, POOL_WIDTH), lambda i: (jnp.maximum(i * (tm // HALO) - 1, 0), 0))

    def body(i, tin, res, tout, acc, scr):
        x_ref, u_ref, uh_ref, gl_ref, at_ref = tin
        w_pool, pool_scale, w_ba, w_bp, b_gate, w_out, g_post = res
        d_o, pooled_o, a_o, pp_o, merged_o, y_o, h1_o = tout
        ext, = scr
        ext[pl.ds(0, HALO), :] = jnp.where(i > 0, uh_ref[...].astype(F32), 0.0)
        ext[pl.ds(HALO, tm), :] = u_ref[...].astype(F32)
        means = _pool_windows(ext, i, tm, HALO)
        for g in range(len(POOL_WINDOWS)):
            cs = slice(g * POOL_GROUP, (g + 1) * POOL_GROUP)
            d = (means[g] - ext[pl.ds(HALO, tm), cs]).astype(MX)
            d_o[:, cs] = d
            pooled_o[:, cs] = (_dot(d, w_pool[g]) * pool_scale[:, cs]).astype(pooled_o.dtype)
        a_br = _dot(at_ref[...], w_ba[...])
        p_br = _dot(pooled_o[...], w_bp[...])
        a_o[...] = a_br.astype(a_o.dtype)
        pp_o[...] = p_br.astype(pp_o.dtype)
        gates = _sig(gl_ref[...].astype(F32) + b_gate[...])
        merged = (gates[:, :D_MODEL] * a_br + gates[:, D_MODEL:] * p_br).astype(MX)
        merged_o[...] = merged
        y = _dot(merged, w_out[...])
        y_o[...] = y.astype(y_o.dtype)
        h1_o[...] = x_ref[...] + _rms(y, g_post[...])[0]

    outs = [_sds(n_tok, POOL_WIDTH, MX), _sds(n_tok, POOL_WIDTH, MX), _sds(n_tok, D_MODEL, MX), _sds(n_tok, D_MODEL, MX),
            _sds(n_tok, D_MODEL, MX), _sds(n_tok, D_MODEL, MX), _sds(n_tok, D_MODEL, F32)]
    res = [w["w_pool"], w["pool_scale"], w["w_branch_attn"], w["w_branch_pool"], w["b_gate"], w["w_out"], w["g_post_mix"]]
    return _tok_call("fwd_mix", body, n_tok, tm, [x, u, (u, halo_spec), gl, attn], res, outs,
                     scratch=[pltpu.VMEM((tm + HALO, POOL_WIDTH), F32)])


def _fwd_mlp(h1, w, tm):
    n_tok = h1.shape[0]

    def body(i, tin, res, tout, acc, scr):
        h1_ref, = tin
        g_pre, w1, w2, g_post = res
        m_o, zr_o, f_o, h2_o = tout
        h1_ = h1_ref[...]
        m = _rms(h1_, g_pre[...])[0].astype(MX)
        m_o[...] = m
        zr = jnp.maximum(_dot(m, w1[...]), 0.0)
        zr_o[...] = zr.astype(zr_o.dtype)
        a2 = (zr * zr).astype(MX)
        f = _dot(a2, w2[...])
        f_o[...] = f.astype(f_o.dtype)
        h2_o[...] = h1_ + _rms(f, g_post[...])[0]

    outs = [_sds(n_tok, D_MODEL, MX), _sds(n_tok, D_FF, MX), _sds(n_tok, D_MODEL, MX),
            _sds(n_tok, D_MODEL, F32)]
    res = [w["g_pre_mlp"], w["w_ff1"], w["w_ff2"], w["g_post_mlp"]]
    return _tok_call("fwd_mlp", body, n_tok, tm, [h1], res, outs)


def _ple_fwd_bwd(h2, p, target, w, tm):
    n_tok = h2.shape[0]

    def body(i, tin, res, tout, acc, scr):
        h2_ref, p_ref, t_ref = tin
        w_pe, w_pg, g_ple = res
        dh2_o, de_o, dzg_o = tout
        loss_a, dg_a = acc
        h2_ = h2_ref[...]
        e = _dot(p_ref[...], w_pe[...])
        pg = _sig(_dot(h2_, w_pg[...]))
        t = pg * e
        g = g_ple[...]
        tn, th, r = _rms(t, g)
        diff = h2_ + tn - t_ref[...]
        loss_a[...] += jnp.sum(diff * diff, axis=0, keepdims=True)
        dh3 = diff * (1.0 / D_MODEL)
        dt, dg = _rms_bwd(th, r, g, dh3)
        dg_a[...] += dg
        de_o[...] = (dt * pg).astype(de_o.dtype)
        dzg = (dt * e * pg * (1.0 - pg)).astype(MX)
        dzg_o[...] = dzg
        dh2_o[...] = dh3 + _dot_nt(dzg, w_pg[...])

    outs = [_sds(n_tok, D_MODEL, F32), _sds(n_tok, D_MODEL, MX), _sds(n_tok, D_MODEL, MX)]
    accs = [_sds(1, D_MODEL, F32), _sds(1, D_MODEL, F32)]
    return _tok_call("ple_fwd_bwd", body, n_tok, tm, [h2, p, target], [w["w_ple_proj"], w["w_ple_gate"], w["g_ple"]], outs, accs)


def _bwd_mlp(dh2, f, h1, zr, w, tm):
    n_tok = dh2.shape[0]

    def body(i, tin, res, tout, acc, scr):
        dh2_ref, f_ref, h1_ref, zr_ref = tin
        g_pre, w1, w2, g_post = res
        df_o, dz_o, dh1_o = tout
        dg_post_a, dg_pre_a = acc
        dh2_ = dh2_ref[...]
        gp = g_post[...]
        _, fh, rf = _rms(f_ref[...].astype(F32), gp)
        df, dg = _rms_bwd(fh, rf, gp, dh2_)
        dg_post_a[...] += dg
        df = df.astype(MX)
        df_o[...] = df
        dz = (_dot_nt(df, w2[...]) * (2.0 * zr_ref[...].astype(F32))).astype(MX)
        dz_o[...] = dz
        dm = _dot_nt(dz, w1[...])
        gq = g_pre[...]
        _, hh, rh = _rms(h1_ref[...], gq)
        dh1, dg = _rms_bwd(hh, rh, gq, dm)
        dg_pre_a[...] += dg
        dh1_o[...] = dh2_ + dh1

    outs = [_sds(n_tok, D_MODEL, MX), _sds(n_tok, D_FF, MX), _sds(n_tok, D_MODEL, F32)]
    accs = [_sds(1, D_MODEL, F32), _sds(1, D_MODEL, F32)]
    res = [w["g_pre_mlp"], w["w_ff1"], w["w_ff2"], w["g_post_mlp"]]
    return _tok_call("bwd_mlp", body, n_tok, tm, [dh2, f, h1, zr], res, outs, accs)


def _bwd_mix(dh1, y, a_br, p_br, gl, attn, d, w, tm):
    n_tok = dh1.shape[0]

    def body(i, tin, res, tout, acc, scr):
        dh1_ref, y_ref, a_ref, pp_ref, gl_ref, at_ref, d_ref = tin
        g_post, w_out, b_gate, w_ba, w_bp, w_pool, pool_scale, sel = res
        dy_o, da_o, dpp_o, dgl_o, do_o, delta_o, dyp_o, dd_o = tout
        dg_post_a, db_a, dps_a = acc
        g = g_post[...]
        _, yh, r = _rms(y_ref[...].astype(F32), g)
        dy, dg = _rms_bwd(yh, r, g, dh1_ref[...])
        dg_post_a[...] += dg
        dy = dy.astype(MX)
        dy_o[...] = dy
        dmerged = _dot_nt(dy, w_out[...])
        gates = _sig(gl_ref[...].astype(F32) + b_gate[...])
        ga, gp = gates[:, :D_MODEL], gates[:, D_MODEL:]
        da = (dmerged * ga).astype(MX)
        dpp = (dmerged * gp).astype(MX)
        da_o[...] = da
        dpp_o[...] = dpp
        dgl_a = dmerged * a_ref[...].astype(F32) * ga * (1.0 - ga)
        dgl_p = dmerged * pp_ref[...].astype(F32) * gp * (1.0 - gp)
        dgl_o[:, :D_MODEL] = dgl_a.astype(dgl_o.dtype)
        dgl_o[:, D_MODEL:] = dgl_p.astype(dgl_o.dtype)
        db_a[:, :D_MODEL] += jnp.sum(dgl_a, axis=0, keepdims=True)
        db_a[:, D_MODEL:] += jnp.sum(dgl_p, axis=0, keepdims=True)
        do = _dot_nt(da, w_ba[...]).astype(MX)
        do_o[...] = do
        prod = do.astype(F32) * at_ref[...].astype(F32)
        hi = prod.astype(MX)
        lo = (prod - hi.astype(F32)).astype(MX)
        delta_o[...] = _dot(hi, sel[...]) + _dot(lo, sel[...])
        dpooled = _dot_nt(dpp, w_bp[...])
        for gi in range(len(POOL_WINDOWS)):
            cs = slice(gi * POOL_GROUP, (gi + 1) * POOL_GROUP)
            ypre = _dot(d_ref[:, cs], w_pool[gi])
            dps_a[:, cs] += jnp.sum(dpooled[:, cs] * ypre, axis=0, keepdims=True)
            dyp = (dpooled[:, cs] * pool_scale[:, cs]).astype(MX)
            dyp_o[:, cs] = dyp
            dd_o[:, cs] = _dot_nt(dyp, w_pool[gi])

    outs = [_sds(n_tok, D_MODEL, MX), _sds(n_tok, D_MODEL, MX), _sds(n_tok, D_MODEL, MX), _sds(n_tok, 2 * D_MODEL, MX),
            _sds(n_tok, N_HEADS * V_HEAD, MX), _sds(n_tok, HEAD_SLOT, F32), _sds(n_tok, POOL_WIDTH, MX),
            _sds(n_tok, POOL_WIDTH, F32)]
    accs = [_sds(1, D_MODEL, F32), _sds(1, 2 * D_MODEL, F32), _sds(1, POOL_WIDTH, F32)]
    res = [w["g_post_mix"], w["w_out"], w["b_gate"], w["w_branch_attn"], w["w_branch_pool"], w["w_pool"], w["pool_scale"],
           w["head_sel"]]
    return _tok_call("bwd_mix", body, n_tok, tm, [dh1, y, a_br, p_br, gl, attn, d], res, outs, accs)


def _bwd_heads(q_ref, k_ref, v_ref, do_ref, lse_ref, dl_ref, st_s, dpt_s, keep, use, n_heads, qs):
    def products(h):
        hs = slice(h * HEAD_SLOT, (h + 1) * HEAD_SLOT)
        vs = slice(h * V_HEAD, (h + 1) * V_HEAD)
        st_s[h % 2, :, qs] = _dot_nt(k_ref[:, hs], q_ref[qs, hs])
        dpt_s[h % 2, :, qs] = _dot_nt(v_ref[:, vs], do_ref[qs, vs])

    products(0)
    for h in range(n_heads):
        if h + 1 < n_heads:
            products(h + 1)
        st = st_s[h % 2, :, qs]
        if keep is not None:
            st = jnp.where(keep, st, NEG)
        pt = jnp.exp2(st - lse_ref[h:h + 1, qs])
        use(h, pt, pt * (dpt_s[h % 2, :, qs] - dl_ref[h:h + 1, qs]))


HEAD_GROUP = 4


def _attn_bwd(q, k, kt, v, do, lse, delta, tq, tk, scatter=None):
    n_tok = q.shape[0]
    nq, ratio = n_tok // tq, tq // tk
    n_groups = N_HEADS // HEAD_GROUP
    gq, gv = HEAD_GROUP * HEAD_SLOT, HEAD_GROUP * V_HEAD
    qi, kj = _causal_pairs(nq, ratio, by_kv=True)

    n_pairs = qi.shape[0]

    def kern(qi_ref, kj_ref, q_ref, k_ref, kt_ref, v_ref, do_ref, lse_ref, dl_ref, *rest):
        if scatter is not None:
            s_hbm, dq_ref, dk_ref, dv_ref, r_hbm, dk_s, dv_s, st_s, dpt_s, send_sems, recv_sems = rest
        else:
            dq_ref, dk_ref, dv_ref, dk_s, dv_s, st_s, dpt_s = rest
        s_id = pl.program_id(1)
        i, j = qi_ref[s_id], kj_ref[s_id]
        if scatter is not None:
            group = pl.program_id(0)
            _scatter_steps(jnp.logical_and(group == 0, s_id == 0),
                           jnp.logical_and(group == n_groups - 1, s_id == n_pairs - 1), s_hbm, r_hbm, (send_sems, recv_sems))

        @pl.when(s_id == 0)
        def _():
            dq_ref[...] = jnp.zeros(dq_ref.shape, F32)

        def heads(first_query, masked):
            qs = slice(first_query, tq)
            cols = pl.ds(pl.multiple_of(i * tq + first_query, tk), tq - first_query)

            def use(h, pt, dst):
                hs = slice(h * HEAD_SLOT, (h + 1) * HEAD_SLOT)
                dv_s[h] += _dot(pt, do_ref[qs, h * V_HEAD:(h + 1) * V_HEAD])
                dk_s[:, hs] += _dot(dst, q_ref[qs, hs])
                dq_ref[hs, cols] += _dot(kt_ref[hs, :], dst)

            keep = _keep_t(tk, tq - first_query, 0) if masked else None
            _bwd_heads(q_ref, k_ref, v_ref, do_ref, lse_ref.at[0], dl_ref.at[0], st_s, dpt_s, keep, use, HEAD_GROUP, qs)

        @pl.when(j >= i * ratio)
        def _():
            dk_s[...] = jnp.zeros(dk_s.shape, F32)
            dv_s[...] = jnp.zeros(dv_s.shape, F32)

        for part in range(ratio):
            @pl.when(j == i * ratio + part)
            def _():
                heads(part * tk, True)

        @pl.when(j < i * ratio)
        def _():
            heads(0, False)

        @pl.when(i == nq - 1)
        def _():
            dk_ref[...] = (dk_s[...] * (1.0 / LOG2E)).astype(dk_ref.dtype)
            for h in range(HEAD_GROUP):
                dv_ref[:, h * V_HEAD:(h + 1) * V_HEAD] = dv_s[h].astype(dv_ref.dtype)

    at_q = lambda g, s, qi, kj: (qi[s], g)
    at_k = lambda g, s, qi, kj: (kj[s], g)
    at_kt = lambda g, s, qi, kj: (g, kj[s])
    at_stat = lambda g, s, qi, kj: (g, 0, qi[s])
    in_specs = [pl.BlockSpec((tq, gq), at_q), pl.BlockSpec((tk, gq), at_k), pl.BlockSpec((gq, tk), at_kt),
                pl.BlockSpec((tk, gv), at_k), pl.BlockSpec((tq, gv), at_q),
                pl.BlockSpec((1, HEAD_GROUP, tq), at_stat), pl.BlockSpec((1, HEAD_GROUP, tq), at_stat)]
    out_specs = [pl.BlockSpec((gq, n_tok), lambda g, s, qi, kj: (g, 0), pipeline_mode=pl.Buffered(1)),
                 pl.BlockSpec((tk, gq), at_k), pl.BlockSpec((tk, gv), at_k)]
    out_shape = [_sds(QK_WIDTH, n_tok, F32), _sds(n_tok, QK_WIDTH, MX), _sds(n_tok, N_HEADS * V_HEAD, MX)]
    scratch = [pltpu.VMEM((tk, gq), F32), pltpu.VMEM((HEAD_GROUP, tk, V_HEAD), F32),
               pltpu.VMEM((2, tk, tq), F32), pltpu.VMEM((2, tk, tq), F32)]
    stat3 = lambda a: a.reshape(n_groups, HEAD_GROUP, n_tok)
    operands = [qi, kj, q, k, kt, v, do, stat3(lse), stat3(delta)]
    if scatter is not None:
        operands.append(scatter)
        in_specs.append(_HBM)
        out_specs.append(_HBM)
        out_shape.append(jax.ShapeDtypeStruct((N_DEVICES, scatter.shape[1] // 2, PACK_COLS), scatter.dtype))
        scratch += [pltpu.SemaphoreType.DMA((N_DEVICES - 1,)), pltpu.SemaphoreType.DMA((N_DEVICES - 1,))]
    gs = pltpu.PrefetchScalarGridSpec(num_scalar_prefetch=2, grid=(n_groups, n_pairs), in_specs=in_specs,
                                      out_specs=out_specs, scratch_shapes=scratch)
    return pl.pallas_call(kern, name="attn_bwd", grid_spec=gs, out_shape=out_shape,
                          compiler_params=_params(("arbitrary", "arbitrary")))(*operands)


def _bwd_inproj(dq_t, dk, dv, dd, dgl, ps, x, dh1, cc, sa, w, tm):
    n_tok = x.shape[0]
    n_tiles = n_tok // tm
    last_halo = n_tok // HALO - 1
    halo_spec = pl.BlockSpec((HALO, POOL_WIDTH), lambda i: (jnp.minimum((i + 1) * (tm // HALO), last_halo), 0))

    def body(i, tin, res, tout, acc, scr):
        dq_ref, dk_ref, dv_ref, dd_ref, ddh_ref, dgl_ref, ps_ref, x_ref, dh1_ref, c_ref, s_ref = tin
        w_uq, g_q, w_k, w_v, e_mat, g_kv, w_in, g_pre = res
        dqu_o, dproj_o, dx_o = tout
        dgq_a, dgkv_a, dgpre_a = acc
        ext, = scr
        cc_, sa_ = c_ref[...], s_ref[...]
        for h in range(N_HEADS):
            hs = slice(h * HEAD_SLOT, (h + 1) * HEAD_SLOT)
            dqu_o[:, hs] = (_unrope(dq_ref[hs, :].T, cc_, sa_) * SCALE).astype(dqu_o.dtype)
        gq = g_q[...]
        _, qh, rq = _rms(ps_ref[:, :Q_LORA].astype(F32), gq)
        dqd, dg = _rms_bwd(qh, rq, gq, _dot_nt(dqu_o[...], w_uq[...]))
        dgq_a[...] += dg
        dproj_o[:, :Q_LORA] = dqd.astype(dproj_o.dtype)
        gkv = g_kv[...]
        _, kh, rk = _rms(ps_ref[:, Q_LORA:Q_LORA + KV_LORA].astype(F32), gkv)
        dkvd, dg = _rms_bwd(kh, rk, gkv, _dot_nt(dk_ref[...], w_k[...]) + _dot_nt(dv_ref[...], w_v[...]))
        dgkv_a[...] += dg
        dproj_o[:, Q_LORA:Q_LORA + KV_LORA] = dkvd.astype(dproj_o.dtype)
        dproj_o[:, Q_LORA + KV_LORA:SMALL_COLS] = _unrope(_dot_nt(dk_ref[...], e_mat[...]), cc_, sa_).astype(dproj_o.dtype)
        row = i * tm + lax.broadcasted_iota(jnp.int32, (tm + HALO, 1), 0)
        for gi, wdw in enumerate(POOL_WINDOWS):
            cs = slice(gi * POOL_GROUP, (gi + 1) * POOL_GROUP)
            inv = 1.0 / jnp.minimum(row + 1, wdw).astype(F32)
            ext[pl.ds(0, tm), cs] = dd_ref[:, cs] * inv[:tm]
            ext[pl.ds(tm, HALO), cs] = jnp.where(i < n_tiles - 1, ddh_ref[:, cs] * inv[tm:], 0.0)
            s = ext[pl.ds(0, tm), cs]
            for k_ in range(1, wdw):
                s = s + ext[pl.ds(k_, tm), cs]
            dproj_o[:, SMALL_COLS + gi * POOL_GROUP:SMALL_COLS + (gi + 1) * POOL_GROUP] = (s - dd_ref[:, cs]).astype(dproj_o.dtype)
        dproj_o[:, SMALL_COLS + POOL_WIDTH:] = dgl_ref[...]
        da = _dot_nt(dproj_o[...], w_in[...])
        gp = g_pre[...]
        _, xh, rx = _rms(x_ref[...], gp)
        dx, dg = _rms_bwd(xh, rx, gp, da)
        dgpre_a[...] += dg
        dx_o[...] = dh1_ref[...] + dx

    outs = [_sds(n_tok, QK_WIDTH, MX), _sds(n_tok, IN_PAD, MX), _sds(n_tok, D_MODEL, F32)]
    accs = [_sds(1, Q_LORA, F32), _sds(1, KV_LORA, F32), _sds(1, D_MODEL, F32)]
    res = [w["w_uq"], w["g_q"], w["w_k"], w["w_v"], w["e_mat"], w["g_kv"], w["w_in"], w["g_pre_mix"]]
    dq_spec = pl.BlockSpec((QK_WIDTH, tm), lambda i: (0, i))
    return _tok_call("bwd_inproj", body, n_tok, tm, [(dq_t, dq_spec), dk, dv, dd, (dd, halo_spec), dgl, ps, x, dh1, cc, sa], res, outs, accs,
                     scratch=[pltpu.VMEM((tm + HALO, POOL_WIDTH), F32)])


XTDY_TOKENS = 1024
XTDY_OUT_BYTES = 8 * 2**20
XTDY_IN_BYTES = 8 * 2**20


def _xtdy(name, x, dy, allreduce=None, square_x=False):
    n_tok, kk = x.shape
    nn = dy.shape[1]
    bk = kk
    while bk * nn * 4 > XTDY_OUT_BYTES and bk % 256 == 0:
        bk //= 2
    bt = min(XTDY_TOKENS, n_tok)
    while (2 * bt <= n_tok and n_tok % (2 * bt) == 0 and 2 * bt * nn * dy.dtype.itemsize <= XTDY_IN_BYTES
           and 2 * bt * bk * x.dtype.itemsize <= XTDY_IN_BYTES):
        bt *= 2

    grid = (kk // bk, n_tok // bt)

    def kern(x_ref, dy_ref, *rest):
        o_ref = rest[1] if allreduce is not None else rest[0]
        if allreduce is not None:
            g_ref, _, sum_ref, buf, send_sems, recv_sems = rest
            step = pl.program_id(0) * grid[1] + pl.program_id(1)
            _allreduce_steps(step == 0, step == grid[0] * grid[1] - 1, g_ref, sum_ref, buf, send_sems, recv_sems)

        @pl.when(pl.program_id(1) == 0)
        def _():
            o_ref[...] = jnp.zeros(o_ref.shape, F32)
        xv = x_ref[...]
        if square_x:
            xv = xv.astype(F32)
            xv = xv * xv
        o_ref[...] += _dot_tn(xv, dy_ref[...])

    operands = [x, dy]
    in_specs = [pl.BlockSpec((bt, bk), lambda a, t: (t, a)), pl.BlockSpec((bt, nn), lambda a, t: (t, 0))]
    out_specs = [pl.BlockSpec((bk, nn), lambda a, t: (a, 0))]
    out_shape = [_sds(kk, nn, F32)]
    scratch = []
    if allreduce is not None:
        vmem = pl.BlockSpec(memory_space=pltpu.VMEM)
        operands.append(allreduce)
        in_specs.append(vmem)
        out_specs.append(vmem)
        out_shape.append(jax.ShapeDtypeStruct(allreduce.shape, allreduce.dtype))
        scratch = [pltpu.VMEM((N_DEVICES,) + allreduce.shape, allreduce.dtype), pltpu.SemaphoreType.DMA((N_DEVICES - 1,)),
                   pltpu.SemaphoreType.DMA((N_DEVICES - 1,))]
    res = pl.pallas_call(kern, name=name, grid=grid, in_specs=in_specs, out_specs=out_specs, out_shape=out_shape,
                         scratch_shapes=scratch, compiler_params=_params(("arbitrary", "arbitrary")))(*operands)
    return res if allreduce is not None else res[0]


def _rope_tables(positions):
    inv_freq = ROPE_THETA ** (-jnp.arange(0, QK_ROPE, 2, dtype=F32) / QK_ROPE)
    ang_t = inv_freq[:, None] * positions.astype(F32)[None, :]
    cos_t, sin_t = lax.optimization_barrier((jnp.cos(ang_t), jnp.sin(ang_t)))
    cos, sin = cos_t.T, sin_t.T
    n_tok = positions.shape[0]
    ones, z64 = jnp.ones((n_tok, ROPE_LANE), F32), jnp.zeros((n_tok, ROPE_LANE), F32)
    z32 = jnp.zeros((n_tok, HEAD_SLOT - ROPE_LANE - QK_ROPE), F32)
    return jnp.concatenate([ones, cos, cos, z32], 1), jnp.concatenate([z64, -sin, sin, z32], 1)


def _kernel_weights(full):
    w_in, w_uq, w_ukv = full["w_in"], full["w_uq"], full["w_ukv"]
    c0 = Q_LORA + KV_LORA
    z = lambda n: jnp.zeros((D_MODEL, n), w_in.dtype)
    w = dict(full)
    w["w_in"] = jnp.concatenate([w_in[:, :c0], z(ROPE_LANE), w_in[:, c0:c0 + QK_ROPE], z(HEAD_SLOT - ROPE_LANE - QK_ROPE),
                                 w_in[:, c0 + QK_ROPE:]], 1)
    w["w_uq"] = jnp.pad(w_uq.reshape(Q_LORA, N_HEADS, QK_NOPE + QK_ROPE),
                        ((0, 0), (0, 0), (0, HEAD_SLOT - QK_NOPE - QK_ROPE))).reshape(Q_LORA, QK_WIDTH)
    kv = w_ukv.reshape(KV_LORA, N_HEADS, QK_NOPE + V_HEAD)
    w["w_k"] = jnp.pad(kv[:, :, :QK_NOPE], ((0, 0), (0, 0), (0, HEAD_SLOT - QK_NOPE))).reshape(KV_LORA, QK_WIDTH)
    w["w_v"] = kv[:, :, QK_NOPE:].reshape(KV_LORA, N_HEADS * V_HEAD)
    e = np.zeros((HEAD_SLOT, QK_WIDTH), np.float32)
    sel = np.zeros((N_HEADS * V_HEAD, HEAD_SLOT), np.float32)
    for h in range(N_HEADS):
        for r in range(QK_ROPE):
            e[ROPE_LANE + r, h * HEAD_SLOT + ROPE_LANE + r] = 1.0
        sel[h * V_HEAD:(h + 1) * V_HEAD, h] = 1.0
    w["e_mat"] = jnp.asarray(e, MX)
    w["w_kt"], w["e_t"] = w["w_k"].T, jnp.asarray(e.T, MX)
    pad = ((0, 0), (0, V_ROWS - V_HEAD), (0, 0))
    w["w_vt"] = jnp.pad(w["w_v"].T.reshape(N_HEADS, V_HEAD, KV_LORA), pad).reshape(N_HEADS * V_ROWS, KV_LORA)
    ones = np.zeros((N_HEADS, V_ROWS, 1), np.float32)
    ones[:, V_HEAD] = 1.0
    w["v_ones"] = jnp.asarray(ones.reshape(N_HEADS * V_ROWS, 1))
    w["head_sel"] = jnp.asarray(sel, MX)
    w["w_pool"] = full["w_pool"].astype(MX)
    return w


def _local_step(x, p, positions, target, full, mesh_place=None, packed_rest=None):
    n_tok = x.shape[0]
    tm = tm_mlp = min(TOKEN_TILE, n_tok)
    fwd_tile = [min(t, n_tok) for t in ATTN_FWD_TILE]
    bwd_tile = [min(t, n_tok) for t in ATTN_BWD_TILE]
    w = _kernel_weights(full)
    cc, sa = _rope_tables(positions)

    if mesh_place is None:
        a, ps, u, gl, qn, kvn, q, k, v, kt, vt = _fwd_inproj(x, cc, sa, w, tm)
    else:
        my_chip, core = mesh_place
        a, ps, u, gl, qn, kvn, q, k, v, kt, vt, gathered = _fwd_inproj(x, cc, sa, w, tm, gather=packed_rest)
        w.update(_unpack_full(gathered, packed_rest, my_chip, REST))
    attn, lse = _attn_fwd(q, k, vt, *fwd_tile)
    d, pooled, a_br, p_br, merged, y, h1 = _fwd_mix(x, u, gl, attn, w, tm)
    m, zr, f, h2 = _fwd_mlp(h1, w, tm_mlp)
    dh2, de, dzg, loss_cols, dg_ple = _ple_fwd_bwd(h2, p, target, w, tm)
    df, dz, dh1, dg_post_mlp, dg_pre_mlp = _bwd_mlp(dh2, f, h1, zr, w, tm_mlp)
    dy, da_br, dp_br, dgl, do, delta, dyp, dd, dg_post_mix, db_gate, dpool_scale = _bwd_mix(dh1, y, a_br, p_br, gl, attn, d, w, tm)
    grads = {"w_branch_attn": _xtdy("dw_ba", attn, da_br), "w_branch_pool": _xtdy("dw_bp", pooled, dp_br),
             "w_out": _xtdy("dw_out", merged, dy), "w_ff1": _xtdy("dw_ff1", m, dz), "w_ff2": _xtdy("dw_ff2", zr, df, square_x=True),
             "w_ple_proj": _xtdy("dw_pe", p, de), "w_ple_gate": _xtdy("dw_pg", h2, dzg)}
    delta_t = delta[:, :N_HEADS].T
    if mesh_place is None:
        travelling = None
        dq_t, dk, dv = _attn_bwd(q, k, kt, v, do, lse, delta_t, *bwd_tile)
    else:
        pieces = _pack_pieces(grads, REST, WIRE)
        dq_t, dk, dv, received = _attn_bwd(q, k, kt, v, do, lse, delta_t, *bwd_tile, scatter=pieces)
        travelling = (pieces, received)
        grads = {}
    dqu, dproj, dx, dg_q, dg_kv, dg_pre_mix = _bwd_inproj(dq_t, dk, dv, dd, dgl, ps, x, dh1, cc, sa, w, tm)

    g_uq = _xtdy("dw_uq", qn, dqu)
    g_k = _xtdy("dw_k", kvn, dk)
    g_v = _xtdy("dw_v", kvn, dv)
    g_pool = _xtdy("dw_pool", d, dyp)
    small = {"g_pre_mix": dg_pre_mix, "b_gate": db_gate, "g_q": dg_q, "g_kv": dg_kv, "pool_scale": dpool_scale,
             "g_post_mix": dg_post_mix, "g_pre_mlp": dg_pre_mlp, "g_post_mlp": dg_post_mlp, "g_ple": dg_ple,
             "w_pool": jnp.stack([g_pool[g * POOL_GROUP:(g + 1) * POOL_GROUP, g * POOL_GROUP:(g + 1) * POOL_GROUP]
                                  for g in range(len(POOL_WINDOWS))])}
    if mesh_place is None:
        g_in = _xtdy("dw_in", a, dproj)
    else:
        g_in, small_sum = _xtdy("dw_in", a, dproj, allreduce=_pack_small(small, loss_cols))
        small, loss_cols = _unpack_small(small_sum), small_sum[-1:]

    c0 = Q_LORA + KV_LORA
    grads.update(small)
    grads.update({
        "w_in": jnp.concatenate([g_in[:, :c0], g_in[:, c0 + ROPE_LANE:c0 + ROPE_LANE + QK_ROPE], g_in[:, SMALL_COLS:]], 1),
        "w_uq": g_uq.reshape(Q_LORA, N_HEADS, HEAD_SLOT)[:, :, :QK_NOPE + QK_ROPE].reshape(Q_LORA, N_HEADS * (QK_NOPE + QK_ROPE)),
        "w_ukv": jnp.concatenate([g_k.reshape(KV_LORA, N_HEADS, HEAD_SLOT)[:, :, :QK_NOPE],
                                  g_v.reshape(KV_LORA, N_HEADS, V_HEAD)], 2).reshape(KV_LORA, N_HEADS * (QK_NOPE + V_HEAD)),
    })
    return loss_cols, dx, grads, travelling


def _place():
    return lax.axis_index("x"), lax.axis_index("y"), lax.axis_index("c")


CHIP_FLIPS = ((1, 0), (0, 1), (1, 1))


def _flip(x, y, fx, fy):
    return (1 - x if fx else x), (1 - y if fy else y)


_HBM = pl.BlockSpec(memory_space=pl.ANY)


def _gather_copies(w_ref, out_ref, send_sems, recv_sems):
    half = w_ref.shape[0] // 2
    x, y, c = _place()
    my_chip = 2 * x + y
    sibling = (x, y, 1 - c)

    def half_of(chip, hc):
        return out_ref.at[chip, pl.ds(pl.multiple_of(hc * half, 16), half), :]

    src = w_ref.at[pl.ds(pl.multiple_of(c * half, 16), half), :]
    sends, landed, forwards, from_sibling = [], [], [], []
    for j, (fx, fy) in enumerate(CHIP_FLIPS):
        px, py = _flip(x, y, fx, fy)
        mine_there, theirs_here, theirs_other = half_of(my_chip, c), half_of(2 * px + py, c), half_of(2 * px + py, 1 - c)
        sends.append(pltpu.make_async_remote_copy(src, mine_there, send_sems.at[j], recv_sems.at[j],
                                                  device_id=(px, py, c), device_id_type=MESH))
        landed.append(pltpu.make_async_remote_copy(src, theirs_here, send_sems.at[j], recv_sems.at[j],
                                                   device_id=(px, py, c), device_id_type=MESH))
        forwards.append(pltpu.make_async_remote_copy(theirs_here, theirs_here, send_sems.at[3 + j], recv_sems.at[3 + j],
                                                     device_id=sibling, device_id_type=MESH))
        from_sibling.append(pltpu.make_async_remote_copy(theirs_other, theirs_other, send_sems.at[3 + j],
                                                         recv_sems.at[3 + j], device_id=sibling, device_id_type=MESH))
    return sends, landed, forwards, from_sibling


def _gather_steps(i, n_steps, w_ref, out_ref, sems):
    sends, landed, forwards, from_sibling = _gather_copies(w_ref, out_ref, *sems)

    @pl.when(i == 0)
    def _():
        for cp in sends:
            cp.start()

    @pl.when(i == (3 * n_steps) // 4)
    def _():
        for arrived, fwd in zip(landed, forwards):
            arrived.wait_recv()
            fwd.start()

    @pl.when(i == n_steps - 1)
    def _():
        for cp in from_sibling:
            cp.wait_recv()
        for cp in sends + forwards:
            cp.wait_send()


def _allgather_shards(wp):
    def body(w_ref, out_ref, send_sems, recv_sems):
        sends, landed, forwards, from_sibling = _gather_copies(w_ref, out_ref, send_sems, recv_sems)
        for cp in sends:
            cp.start()
        for arrived, fwd in zip(landed, forwards):
            arrived.wait_recv()
            fwd.start()
        for cp in from_sibling:
            cp.wait_recv()
        for cp in sends + forwards:
            cp.wait_send()

    return pl.pallas_call(
        body, name="allgather_shards", out_shape=jax.ShapeDtypeStruct((N_CHIPS,) + wp.shape, wp.dtype),
        in_specs=[_HBM], out_specs=_HBM,
        scratch_shapes=[pltpu.SemaphoreType.DMA((6,)), pltpu.SemaphoreType.DMA((6,))],
    )(wp)


def _exchange_halves(g):
    rows = g.shape[1]
    half = rows // 2

    def body(g_ref, r_ref, send_sem, recv_sem):
        x, y, c = _place()
        src = g_ref.at[:, pl.ds(pl.multiple_of((1 - c) * half, 8), half), :]
        cp = pltpu.make_async_remote_copy(src, r_ref, send_sem, recv_sem, device_id=(x, y, 1 - c), device_id_type=MESH)
        cp.start()
        cp.wait()

    return pl.pallas_call(
        body, name="exchange_halves", out_shape=jax.ShapeDtypeStruct((N_CHIPS, half, PACK_COLS), g.dtype),
        in_specs=[_HBM], out_specs=_HBM, scratch_shapes=[pltpu.SemaphoreType.DMA, pltpu.SemaphoreType.DMA],
    )(g)


def _add_halves(g, r, c):
    rows = g.shape[1]
    half = rows // 2
    br = REDUCE_ROWS
    nb = half // br

    def kern(c_ref, g_ref, r_ref, o_ref):
        o_ref[...] = (g_ref[...] + r_ref[...]).astype(o_ref.dtype)

    gs = pltpu.PrefetchScalarGridSpec(
        num_scalar_prefetch=1, grid=(N_CHIPS, nb),
        in_specs=[pl.BlockSpec((1, br, PACK_COLS), lambda k, t, c: (k, c[0] * nb + t, 0)),
                  pl.BlockSpec((1, br, PACK_COLS), lambda k, t, c: (k, t, 0))],
        out_specs=pl.BlockSpec((1, br, PACK_COLS), lambda k, t, c: (k, t, 0)))
    return pl.pallas_call(kern, name="add_halves", grid_spec=gs,
                          out_shape=jax.ShapeDtypeStruct((N_CHIPS, half, PACK_COLS), WIRE),
                          compiler_params=_params(("arbitrary", "arbitrary")))(c.reshape(1), g, r)


def _scatter_copies(s_ref, r_ref, send_sems, recv_sems):
    x, y, c = _place()
    my_chip = 2 * x + y
    sends, arrivals = [], []
    for j, (fx, fy) in enumerate(CHIP_FLIPS):
        px, py = _flip(x, y, fx, fy)
        slot = r_ref.at[2 * px + py]
        sends.append(pltpu.make_async_remote_copy(s_ref.at[2 * px + py], r_ref.at[my_chip], send_sems.at[j], recv_sems.at[j],
                                                  device_id=(px, py, c), device_id_type=MESH))
        arrivals.append(pltpu.make_async_remote_copy(slot, slot, send_sems.at[j], recv_sems.at[j],
                                                     device_id=(px, py, c), device_id_type=MESH))
    return sends, arrivals


N_DEVICES = 8


def _peer(x, y, c, f):
    px, py = _flip(x, y, f & 4, f & 2)
    return px, py, (1 - c if f & 1 else c)


def _scatter_all_copies(p_ref, r_ref, send_sems, recv_sems):
    half = p_ref.shape[1] // 2
    x, y, c = _place()
    me = 4 * x + 2 * y + c
    sends, arrivals = [], []
    for f in range(1, N_DEVICES):
        px, py, pc = _peer(x, y, c, f)
        theirs = p_ref.at[2 * px + py, pl.ds(pl.multiple_of(pc * half, 16), half), :]
        slot = r_ref.at[4 * px + 2 * py + pc]
        sends.append(pltpu.make_async_remote_copy(theirs, r_ref.at[me], send_sems.at[f - 1], recv_sems.at[f - 1],
                                                  device_id=(px, py, pc), device_id_type=MESH))
        arrivals.append(pltpu.make_async_remote_copy(slot, slot, send_sems.at[f - 1], recv_sems.at[f - 1],
                                                     device_id=(px, py, pc), device_id_type=MESH))
    return sends, arrivals


def _scatter_steps(first, last, p_ref, r_ref, sems):
    sends, arrivals = _scatter_all_copies(p_ref, r_ref, *sems)

    @pl.when(first)
    def _():
        for cp in sends:
            cp.start()

    @pl.when(last)
    def _():
        for cp in arrivals:
            cp.wait_recv()
        for cp in sends:
            cp.wait_send()


def _scatter_pieces(s):
    def body(s_ref, r_ref, send_sems, recv_sems):
        sends, arrivals = _scatter_copies(s_ref, r_ref, send_sems, recv_sems)
        for cp in sends:
            cp.start()
        for cp in arrivals:
            cp.wait_recv()
        for cp in sends:
            cp.wait_send()

    return pl.pallas_call(
        body, name="scatter_pieces", out_shape=jax.ShapeDtypeStruct(s.shape, s.dtype), in_specs=[_HBM], out_specs=_HBM,
        scratch_shapes=[pltpu.SemaphoreType.DMA((3,)), pltpu.SemaphoreType.DMA((3,))],
    )(s)


def _sum_pieces(r, mine, slot):
    slots, half = r.shape[:2]
    br = REDUCE_ROWS

    def kern(slot_ref, r_ref, m_ref, o_ref):
        total = None
        for k in range(slots):
            term = jnp.where(slot_ref[0] == k, m_ref[0], r_ref[k]).astype(F32)
            total = term if total is None else total + term
        o_ref[...] = total

    gs = pltpu.PrefetchScalarGridSpec(
        num_scalar_prefetch=1, grid=(half // br,),
        in_specs=[pl.BlockSpec((slots, br, PACK_COLS), lambda t, s: (0, t, 0)),
                  pl.BlockSpec((1, br, PACK_COLS), lambda t, s: (0, t, 0))],
        out_specs=pl.BlockSpec((br, PACK_COLS), lambda t, s: (t, 0)))
    return pl.pallas_call(kern, name="sum_pieces", grid_spec=gs, out_shape=_sds(half, PACK_COLS, F32),
                          compiler_params=_params(("arbitrary",)))(slot.reshape(1), r, mine)


def _join_halves(f):
    def body(f_ref, o_ref, send_sem, recv_sem):
        x, y, c = _place()
        cp = pltpu.make_async_remote_copy(f_ref, o_ref, send_sem, recv_sem, device_id=(x, y, 1 - c), device_id_type=MESH)
        cp.start()
        cp.wait()

    return pl.pallas_call(
        body, name="join_halves", out_shape=jax.ShapeDtypeStruct(f.shape, f.dtype), in_specs=[_HBM], out_specs=_HBM,
        scratch_shapes=[pltpu.SemaphoreType.DMA, pltpu.SemaphoreType.DMA],
    )(f)


def _allreduce_steps(first, last, g_ref, o_ref, buf, send_sems, recv_sems):
    x, y, c = _place()
    me = 4 * x + 2 * y + c
    sends, arrivals = [], []
    for f in range(1, N_DEVICES):
        px, py, pc = _peer(x, y, c, f)
        slot = buf.at[4 * px + 2 * py + pc]
        sends.append(pltpu.make_async_remote_copy(g_ref, buf.at[me], send_sems.at[f - 1], recv_sems.at[f - 1],
                                                  device_id=(px, py, pc), device_id_type=MESH))
        arrivals.append(pltpu.make_async_remote_copy(slot, slot, send_sems.at[f - 1], recv_sems.at[f - 1],
                                                     device_id=(px, py, pc), device_id_type=MESH))

    @pl.when(first)
    def _():
        buf[me] = g_ref[...]
        for cp in sends:
            cp.start()

    @pl.when(last)
    def _():
        for cp in arrivals:
            cp.wait_recv()
        for cp in sends:
            cp.wait_send()
        total = buf[0]
        for k in range(1, N_DEVICES):
            total = total + buf[k]
        o_ref[...] = total


def _adamw_update(g_ref, w_ref, m_ref, v_ref, d_o, m_o, v_o):
    c1 = 1.0 - ADAM_B1 ** ADAM_STEP
    c2 = 1.0 - ADAM_B2 ** ADAM_STEP
    g_ = g_ref[...]
    m_new = ADAM_B1 * m_ref[...] + (1.0 - ADAM_B1) * g_
    v_new = ADAM_B2 * v_ref[...] + (1.0 - ADAM_B2) * (g_ * g_)
    m_o[...] = m_new
    v_o[...] = v_new
    d_o[...] = -ADAM_LR * ((m_new / c1) / (jnp.sqrt(v_new / c2) + ADAM_EPS) + ADAM_WD * w_ref[...])


ADAMW_ROWS = 256


def _adamw(name, g, w, m, v):
    _, rows, cols = w.shape
    br = int(np.gcd(ADAMW_ROWS, rows))

    def kern(*refs):
        _adamw_update(*refs)

    spec = pl.BlockSpec((1, br, cols), lambda t: (0, t, 0))
    out = jax.ShapeDtypeStruct(w.shape, F32)
    return pl.pallas_call(kern, name="adamw_" + name, grid=(rows // br,), in_specs=[spec] * 4, out_specs=[spec] * 3,
                          out_shape=[out, out, out], compiler_params=_params(("arbitrary",)))(g, w, m, v)


def _adamw_small(gs, ws, ms, vs):
    n = len(gs)

    def kern(*refs):
        ins, outs = refs[:4 * n], refs[4 * n:]
        for k in range(n):
            _adamw_update(ins[k], ins[n + k], ins[2 * n + k], ins[3 * n + k], outs[k], outs[n + k], outs[2 * n + k])

    vmem = pl.BlockSpec(memory_space=pltpu.VMEM)
    out = [jax.ShapeDtypeStruct(w.shape, F32) for w in ws]
    res = pl.pallas_call(kern, name="adamw_small", in_specs=[vmem] * (4 * n), out_specs=[vmem] * (3 * n),
                         out_shape=out * 3, compiler_params=pltpu.CompilerParams(vmem_limit_bytes=VMEM_LIMIT))(*gs, *ws, *ms, *vs)
    return [(res[k], res[n + k], res[2 * n + k]) for k in range(n)]


def _shard_rows(shape, axis):
    k, n = shape
    return (k * n // N_CHIPS) // PACK_COLS


def _group(names):
    entries = [e for e in SHARDED if e[0] in names]
    used = sum(_shard_rows(shape, axis) for _, shape, axis in entries)
    return entries, -(-used // (2 * REDUCE_ROWS)) * 2 * REDUCE_ROWS


def _pack_shards(shards, names, dtype):
    entries, rows = _group(names)
    parts = [shards[name].astype(dtype).reshape(-1, PACK_COLS) for name, _, _ in entries]
    used = sum(p.shape[0] for p in parts)
    if rows > used:
        parts.append(jnp.zeros((rows - used, PACK_COLS), dtype))
    return jnp.concatenate(parts, 0)


def _unpack_shards(packed, names):
    out, r0 = {}, 0
    for name, (k, n), axis in _group(names)[0]:
        nr = _shard_rows((k, n), axis)
        shape = (k // N_CHIPS, n) if axis == 0 else (k, n // N_CHIPS)
        out[name] = packed[r0:r0 + nr].reshape(shape)
        r0 += nr
    return out


def _unpack_full(gathered, own, my_chip, names):
    chip = lax.broadcasted_iota(jnp.int32, (N_CHIPS, 1, 1), 0)
    gathered = jnp.where(chip == my_chip, own[None], gathered)
    out, r0 = {}, 0
    for name, (k, n), axis in _group(names)[0]:
        nr = _shard_rows((k, n), axis)
        part = gathered[:, r0:r0 + nr]
        if axis == 0:
            out[name] = part.reshape(k, n)
        else:
            out[name] = part.reshape(N_CHIPS, k, n // N_CHIPS).transpose(1, 0, 2).reshape(k, n)
        r0 += nr
    return out


def _pack_pieces(grads, names, dtype=F32):
    entries, rows = _group(names)
    parts = []
    for name, (k, n), axis in entries:
        g = grads[name].astype(dtype)
        if axis == 0:
            parts.append(g.reshape(N_CHIPS, -1, PACK_COLS))
        else:
            parts.append(g.reshape(k, N_CHIPS, n // N_CHIPS).transpose(1, 0, 2).reshape(N_CHIPS, -1, PACK_COLS))
    used = sum(p.shape[1] for p in parts)
    if rows > used:
        parts.append(jnp.zeros((N_CHIPS, rows - used, PACK_COLS), dtype))
    return jnp.concatenate(parts, 1)


def _pack_small(vals, last_row):
    flat = jnp.concatenate([vals[name].astype(F32).reshape(-1) for name, _ in SMALL])
    spare = jnp.zeros(((SMALL_ROWS - 1) * PACK_COLS - flat.shape[0],), F32)
    return jnp.concatenate([flat, spare, last_row.reshape(-1)]).reshape(SMALL_ROWS, PACK_COLS)


def _unpack_small(packed):
    flat, out, o = packed.reshape(-1), {}, 0
    for name, shape in SMALL:
        n = int(np.prod(shape))
        out[name] = flat[o:o + n].reshape(shape)
        o += n
    return out


def kernel(x, p, positions, g_pre_mix, w_in, b_gate, g_q, w_uq, g_kv, w_ukv, w_pool, pool_scale, w_branch_attn, w_branch_pool, w_out, g_post_mix, g_pre_mlp, w_ff1, w_ff2, g_post_mlp, w_ple_proj, w_ple_gate, g_ple, loss_target, m_g_pre_mix, m_w_in, m_b_gate, m_g_q, m_w_uq, m_g_kv, m_w_ukv, m_w_pool, m_pool_scale, m_w_branch_attn, m_w_branch_pool, m_w_out, m_g_post_mix, m_g_pre_mlp, m_w_ff1, m_w_ff2, m_g_post_mlp, m_w_ple_proj, m_w_ple_gate, m_g_ple, v_g_pre_mix, v_w_in, v_b_gate, v_g_q, v_w_uq, v_g_kv, v_w_ukv, v_w_pool, v_pool_scale, v_w_branch_attn, v_w_branch_pool, v_w_out, v_g_post_mix, v_g_pre_mlp, v_w_ff1, v_w_ff2, v_g_post_mlp, v_w_ple_proj, v_w_ple_gate, v_g_ple):
    given = dict(locals())
    weights = {n: given[n] for n in WEIGHT_ORDER}
    moments_m = {n: given["m_" + n] for n in WEIGHT_ORDER}
    moments_v = {n: given["v_" + n] for n in WEIGHT_ORDER}
    c = lax.axis_index("c")

    big_w = {name: weights[name][0] for name, _, _ in SHARDED}
    my_chip = 2 * lax.axis_index("x") + lax.axis_index("y")
    packed_first = _pack_shards(big_w, FIRST, MX)
    full = _unpack_full(_allgather_shards(packed_first), packed_first, my_chip, FIRST)
    for name, _ in SMALL:
        full[name] = weights[name][0] if name == "w_pool" else weights[name]

    loss_cols, dx, grads, (pieces_rest, received_rest) = _local_step(
        x[0], p[0, 0], positions[0], loss_target[0], full, (my_chip, c), _pack_shards(big_w, REST, MX))
    loss = 0.5 * jnp.sum(loss_cols) / D_MODEL

    def finish(received, mine, slot):
        reduced = _sum_pieces(received, mine, slot)
        theirs = _join_halves(reduced)
        return jnp.where(c == 0, jnp.concatenate([reduced, theirs]), jnp.concatenate([theirs, reduced]))

    pieces = _pack_pieces(grads, FIRST)
    sent = _add_halves(pieces, _exchange_halves(pieces), c)
    mine = lax.dynamic_slice(sent, (my_chip, 0, 0), (1,) + sent.shape[1:])
    shards = _unpack_shards(finish(_scatter_pieces(sent), mine, my_chip), FIRST)
    half = received_rest.shape[1]
    mine = lax.dynamic_slice(pieces_rest, (my_chip, c * half, 0), (1, half, PACK_COLS))
    shards.update(_unpack_shards(finish(received_rest, mine, 2 * my_chip + c), REST))

    out = {}
    for name, g in shards.items():
        out[name] = (g[None], *_adamw(name, g[None], weights[name], moments_m[name], moments_v[name]))
    small_g = {n: grads[n] for n, _ in SMALL}
    names = [n for n, _ in SMALL]
    updates = _adamw_small([small_g[n] for n in names], [weights[n] for n in names], [moments_m[n] for n in names],
                           [moments_v[n] for n in names])
    for n, upd in zip(names, updates):
        out[n] = (small_g[n], *upd)
    return (loss, dx[None], *[out[n][k] for k in range(4) for n in WEIGHT_ORDER])
```

```python
import numpy as np
import jax
import jax.numpy as jnp
from jax import lax
from jax.experimental import pallas as pl
from jax.experimental.pallas import tpu as pltpu

F32 = jnp.float32
MX = jnp.bfloat16
WIRE = jnp.bfloat16

D_MODEL = 1024
N_HEADS = 8
QK_NOPE = 64
QK_ROPE = 32
V_HEAD = 64
Q_LORA = 384
KV_LORA = 256
POOL_WINDOWS = (2, 4, 8, 16)
POOL_GROUP = 128
POOL_WIDTH = 512
D_FF = 4096
ROPE_THETA = 10000.0
EPS = 1e-6
HEAD_SLOT = 128
QK_WIDTH = N_HEADS * HEAD_SLOT
ROPE_LANE = 64
SMALL_COLS = Q_LORA + KV_LORA + HEAD_SLOT
IN_PAD = SMALL_COLS + POOL_WIDTH + 2 * D_MODEL
SCALE = (QK_NOPE + QK_ROPE) ** -0.5
LOG2E = 1.4426950408889634
NEG = -1e30
HALO = 16

ADAM_LR = 0.001
ADAM_B1 = 0.9
ADAM_B2 = 0.999
ADAM_EPS = 1e-08
ADAM_WD = 0.01
ADAM_STEP = 10

VMEM_LIMIT = 56 * 2**20
TOKEN_TILE = 512
MESH = pl.DeviceIdType.MESH

SHARDED = (
    ("w_in", (1024, 3232), 1),
    ("w_uq", (384, 768), 1),
    ("w_ukv", (256, 1024), 1),
    ("w_branch_attn", (512, 1024), 1),
    ("w_branch_pool", (512, 1024), 1),
    ("w_out", (1024, 1024), 0),
    ("w_ff1", (1024, 4096), 1),
    ("w_ff2", (4096, 1024), 0),
    ("w_ple_proj", (256, 1024), 1),
    ("w_ple_gate", (1024, 1024), 0),
)
SMALL = (
    ("g_pre_mix", (1, 1024)),
    ("b_gate", (1, 2048)),
    ("g_q", (1, 384)),
    ("g_kv", (1, 256)),
    ("w_pool", (1, 4, 128, 128)),
    ("pool_scale", (1, 512)),
    ("g_post_mix", (1, 1024)),
    ("g_pre_mlp", (1, 1024)),
    ("g_post_mlp", (1, 1024)),
    ("g_ple", (1, 1024)),
)
WEIGHT_ORDER = ("g_pre_mix", "w_in", "b_gate", "g_q", "w_uq", "g_kv", "w_ukv", "w_pool", "pool_scale", "w_branch_attn",
                "w_branch_pool", "w_out", "g_post_mix", "g_pre_mlp", "w_ff1", "w_ff2", "g_post_mlp", "w_ple_proj",
                "w_ple_gate", "g_ple")
N_CHIPS = 4
PACK_COLS = 1024
REDUCE_ROWS = 160
SMALL_ROWS = 80
FIRST = ("w_in", "w_uq", "w_ukv")
REST = tuple(name for name, _, _ in SHARDED if name not in FIRST)


def _dot(a, b):
    return jnp.dot(a.astype(MX), b.astype(MX), preferred_element_type=F32)


def _dot_nt(a, b):
    return lax.dot_general(a.astype(MX), b.astype(MX), (((1,), (1,)), ((), ())), preferred_element_type=F32)


def _dot_tn(a, b):
    return lax.dot_general(a.astype(MX), b.astype(MX), (((0,), (0,)), ((), ())), preferred_element_type=F32)


def _sig(x):
    return 1.0 / (1.0 + jnp.exp(-x))


def _rms(x, g):
    r = lax.rsqrt(jnp.mean(x * x, axis=1, keepdims=True) + EPS)
    xh = x * r
    return xh * g, xh, r


def _rms_bwd(xh, r, g, dy):
    dxn = dy * g
    dx = r * (dxn - xh * jnp.mean(dxn * xh, axis=1, keepdims=True))
    return dx, jnp.sum(dy * xh, axis=0, keepdims=True)


def _rot_half(v):
    lane = lax.broadcasted_iota(jnp.int32, v.shape, 1)
    return jnp.where(lane < ROPE_LANE + QK_ROPE // 2, pltpu.roll(v, HEAD_SLOT - QK_ROPE // 2, 1), pltpu.roll(v, QK_ROPE // 2, 1))


def _rope(v, cc, sa):
    return v * cc + _rot_half(v) * sa


def _unrope(v, cc, sa):
    return v * cc - _rot_half(v) * sa


def _params(sem):
    return pltpu.CompilerParams(dimension_semantics=sem, vmem_limit_bytes=VMEM_LIMIT)


def _tok_call(name, body, n_tok, tm, tiled, resident, outs, accs=(), scratch=(), exchange=None):
    def as_pair(t):
        if isinstance(t, tuple):
            return t
        return t, pl.BlockSpec((tm, t.shape[1]), lambda i: (i, 0))
    tiled = [as_pair(t) for t in tiled]
    outs = [as_pair(o) for o in outs]
    res_specs = [pl.BlockSpec(r.shape, lambda i, nd=r.ndim: (0,) * nd, pipeline_mode=pl.Buffered(1)) for r in resident]
    out_specs = [s for _, s in outs] + [pl.BlockSpec(a.shape, lambda i: (0, 0)) for a in accs]
    n_t, n_r, n_o, n_a, n_s = len(tiled), len(resident), len(outs), len(accs), len(scratch)
    n_steps = n_tok // tm
    operands = [a for a, _ in tiled] + list(resident)
    in_specs = [s for _, s in tiled] + res_specs
    out_shape = [o for o, _ in outs] + list(accs)
    scratch = list(scratch)
    if exchange is not None:
        ex_in, ex_out, ex_sems, ex_steps = exchange
        operands.append(ex_in)
        in_specs.append(_HBM)
        out_shape.append(ex_out)
        out_specs.append(_HBM)
        scratch += list(ex_sems)

    def kern(*refs):
        refs = list(refs)
        n_in = n_t + n_r + (exchange is not None)
        n_out = n_o + n_a + (exchange is not None)
        tin, res = refs[:n_t], refs[n_t:n_t + n_r]
        tout = refs[n_in:n_in + n_o]
        acc = refs[n_in + n_o:n_in + n_o + n_a]
        scr = refs[n_in + n_out:n_in + n_out + n_s]
        i = pl.program_id(0)
        if exchange is not None:
            ex_steps(i, n_steps, refs[n_in - 1], refs[n_in + n_out - 1], refs[n_in + n_out + n_s:])

        @pl.when(i == 0)
        def _():
            for a in acc:
                a[...] = jnp.zeros(a.shape, a.dtype)
        body(i, tin, res, tout, acc, scr)

    return pl.pallas_call(
        kern, name=name, grid=(n_steps,), in_specs=in_specs, out_specs=out_specs,
        out_shape=out_shape, scratch_shapes=scratch, compiler_params=_params(("arbitrary",)),
    )(*operands)


def _sds(rows, cols, dtype):
    return jax.ShapeDtypeStruct((rows, cols), dtype)


def _fwd_inproj(x, cc, sa, w, tm, gather=None):
    n_tok = x.shape[0]

    def body(i, tin, res, tout, acc, scr):
        x_ref, c_ref, s_ref = tin
        g_pre, w_in, g_q, w_uq, g_kv, w_k, w_v, e_mat, w_kt, e_t, w_vt, v_ones = res
        a_o, ps_o, u_o, gl_o, qn_o, kvn_o, q_o, k_o, v_o, kt_o, vt_o = tout
        a = _rms(x_ref[...], g_pre[...])[0].astype(MX)
        a_o[...] = a
        ps = _dot(a, w_in[:, :SMALL_COLS])
        ps_o[...] = ps.astype(ps_o.dtype)
        u_o[...] = _dot(a, w_in[:, SMALL_COLS:SMALL_COLS + POOL_WIDTH]).astype(u_o.dtype)
        gl_o[...] = _dot(a, w_in[:, SMALL_COLS + POOL_WIDTH:]).astype(gl_o.dtype)
        cc_, sa_ = c_ref[...], s_ref[...]
        qn = _rms(ps[:, :Q_LORA], g_q[...])[0].astype(MX)
        qn_o[...] = qn
        q = _dot(qn, w_uq[...])
        for h in range(N_HEADS):
            hs = slice(h * HEAD_SLOT, (h + 1) * HEAD_SLOT)
            q_o[:, hs] = (_rope(q[:, hs], cc_, sa_) * (SCALE * LOG2E)).astype(q_o.dtype)
        kvn = _rms(ps[:, Q_LORA:Q_LORA + KV_LORA], g_kv[...])[0].astype(MX)
        kvn_o[...] = kvn
        kr = _rope(ps[:, Q_LORA + KV_LORA:], cc_, sa_)
        k_o[...] = (_dot(kvn, w_k[...]) + _dot(kr, e_mat[...])).astype(k_o.dtype)
        v_o[...] = _dot(kvn, w_v[...]).astype(v_o.dtype)
        kt_o[...] = (_dot_nt(w_kt[...], kvn) + _dot_nt(e_t[...], kr)).astype(kt_o.dtype)
        vt_o[...] = (_dot_nt(w_vt[...], kvn) + v_ones[...]).astype(vt_o.dtype)

    outs = [_sds(n_tok, D_MODEL, MX), _sds(n_tok, SMALL_COLS, MX), _sds(n_tok, POOL_WIDTH, MX), _sds(n_tok, 2 * D_MODEL, MX),
            _sds(n_tok, Q_LORA, MX), _sds(n_tok, KV_LORA, MX), _sds(n_tok, QK_WIDTH, MX), _sds(n_tok, QK_WIDTH, MX),
            _sds(n_tok, N_HEADS * V_HEAD, MX),
            (_sds(QK_WIDTH, n_tok, MX), pl.BlockSpec((QK_WIDTH, tm), lambda i: (0, i))),
            (_sds(N_HEADS * V_ROWS, n_tok, MX), pl.BlockSpec((N_HEADS * V_ROWS, tm), lambda i: (0, i)))]
    res = [w["g_pre_mix"], w["w_in"], w["g_q"], w["w_uq"], w["g_kv"], w["w_k"], w["w_v"], w["e_mat"], w["w_kt"], w["e_t"],
           w["w_vt"], w["v_ones"]]
    exchange = None
    if gather is not None:
        gathered = jax.ShapeDtypeStruct((N_CHIPS,) + gather.shape, gather.dtype)
        exchange = (gather, gathered, [pltpu.SemaphoreType.DMA((6,)), pltpu.SemaphoreType.DMA((6,))], _gather_steps)
    return _tok_call("fwd_inproj", body, n_tok, tm, [x, cc, sa], res, outs, exchange=exchange)


def _causal_pairs(nq, ratio, by_kv):
    if by_kv:
        pairs = [(i, j) for j in range(nq * ratio) for i in range(j // ratio, nq)]
    else:
        pairs = [(i, j) for i in range(nq) for j in range((i + 1) * ratio)]
    return (jnp.asarray(np.array([p[0] for p in pairs], np.int32)), jnp.asarray(np.array([p[1] for p in pairs], np.int32)))


def _keep_t(tk, tq, off):
    return lax.broadcasted_iota(jnp.int32, (tk, tq), 0) + off <= lax.broadcasted_iota(jnp.int32, (tk, tq), 1)


ATTN_DIAG_KEYS = 512
ATTN_FWD_TILE = (1024, 1024)
ATTN_DIAG_KEYS_BWD = 256
ATTN_BWD_TILE = (1024, 512)
V_ROWS = 80


def _attn_fwd(q, k, vt, tq, tk):
    n_tok = q.shape[0]
    nq, ratio = n_tok // tq, tq // tk
    qi, kj = _causal_pairs(nq, ratio, by_kv=False)

    def kern(qi_ref, kj_ref, q_ref, k_ref, vt_ref, o_ref, lse_ref, m_s, acc_s, st_s):
        s_id = pl.program_id(0)
        i, j = qi_ref[s_id], kj_ref[s_id]

        @pl.when(j == 0)
        def _():
            m_s[...] = jnp.full(m_s.shape, NEG, F32)
            acc_s[...] = jnp.zeros(acc_s.shape, F32)

        def heads(ks, qs, masked):
            n_keys = ks.stop - ks.start
            keep = _keep_t(n_keys, qs.stop - qs.start, 0) if masked else None

            def scores(h):
                hs = slice(h * HEAD_SLOT, (h + 1) * HEAD_SLOT)
                return _dot_nt(k_ref[ks, hs], q_ref[qs, hs])

            st_s[0, :n_keys, qs] = scores(0)
            for h in range(N_HEADS):
                if h + 1 < N_HEADS:
                    st_s[(h + 1) % 2, :n_keys, qs] = scores(h + 1)
                st = st_s[h % 2, :n_keys, qs]
                if masked:
                    st = jnp.where(keep, st, NEG)
                m_old = m_s[h, :, qs]
                m_new = jnp.maximum(m_old, jnp.max(st, axis=0, keepdims=True))
                pt = jnp.exp2(st - m_new)
                acc_s[h, :, qs] = (jnp.exp2(m_old - m_new) * acc_s[h, :, qs]
                                   + _dot(vt_ref[h * V_ROWS:(h + 1) * V_ROWS, ks], pt))
                m_s[h, :, qs] = m_new

        @pl.when(j < i * ratio)
        def _():
            heads(slice(0, tk), slice(0, tq), False)

        sub = min(tk, ATTN_DIAG_KEYS)
        for part in range(ratio):
            @pl.when(j == i * ratio + part)
            def _():
                for first in range(0, tk, sub):
                    heads(slice(first, first + sub), slice(part * tk + first, tq), True)

        @pl.when(j == (i + 1) * ratio - 1)
        def _():
            heads_out = []
            for h in range(N_HEADS):
                total = acc_s[h, V_HEAD:V_HEAD + 1, :]
                heads_out.append(acc_s[h, :V_HEAD, :] / total)
                lse_ref[h:h + 1, :] = m_s[h] + jnp.log2(total)
            o_ref[...] = jnp.concatenate(heads_out, 0).T.astype(o_ref.dtype)

    gs = pltpu.PrefetchScalarGridSpec(
        num_scalar_prefetch=2, grid=(qi.shape[0],),
        in_specs=[pl.BlockSpec((tq, QK_WIDTH), lambda s, qi, kj: (qi[s], 0)),
                  pl.BlockSpec((tk, QK_WIDTH), lambda s, qi, kj: (kj[s], 0)),
                  pl.BlockSpec((N_HEADS * V_ROWS, tk), lambda s, qi, kj: (0, kj[s]))],
        out_specs=[pl.BlockSpec((tq, N_HEADS * V_HEAD), lambda s, qi, kj: (qi[s], 0)),
                   pl.BlockSpec((N_HEADS, tq), lambda s, qi, kj: (0, qi[s]))],
        scratch_shapes=[pltpu.VMEM((N_HEADS, 1, tq), F32), pltpu.VMEM((N_HEADS, V_ROWS, tq), F32),
                        pltpu.VMEM((2, tk, tq), F32)])
    return pl.pallas_call(kern, name="attn_fwd", grid_spec=gs,
                          out_shape=[_sds(n_tok, N_HEADS * V_HEAD, MX), _sds(N_HEADS, n_tok, F32)],
                          compiler_params=_params(("arbitrary",)))(qi, kj, q, k, vt)


def _pool_windows(ext, i, tm, first_row):
    row = i * tm + lax.broadcasted_iota(jnp.int32, (tm, 1), 0)
    out = []
    for g, w in enumerate(POOL_WINDOWS):
        cs = slice(g * POOL_GROUP, (g + 1) * POOL_GROUP)
        s = ext[pl.ds(first_row, tm), cs]
        for k in range(1, w):
            s = s + ext[pl.ds(first_row - k, tm), cs]
        cnt = jnp.minimum(row + 1, w).astype(F32)
        out.append(s / cnt)
    return out


def _fwd_mix(x, u, gl, attn, w, tm):
    n_tok = x.shape[0]
    halo_spec = pl.BlockSpec((HALO, POOL_WIDTH), lambda i: (jnp.maximum(i * (tm // HALO) - 1, 0), 0))

    def body(i, tin, res, tout, acc, scr):
        x_ref, u_ref, uh_ref, gl_ref, at_ref = tin
        w_pool, pool_scale, w_ba, w_bp, b_gate, w_out, g_post = res
        d_o, pooled_o, a_o, pp_o, merged_o, y_o, h1_o = tout
        ext, = scr
        ext[pl.ds(0, HALO), :] = jnp.where(i > 0, uh_ref[...].astype(F32), 0.0)
        ext[pl.ds(HALO, tm), :] = u_ref[...].astype(F32)
        means = _pool_windows(ext, i, tm, HALO)
        for g in range(len(POOL_WINDOWS)):
            cs = slice(g * POOL_GROUP, (g + 1) * POOL_GROUP)
            d = (means[g] - ext[pl.ds(HALO, tm), cs]).astype(MX)
            d_o[:, cs] = d
            pooled_o[:, cs] = (_dot(d, w_pool[g]) * pool_scale[:, cs]).astype(pooled_o.dtype)
        a_br = _dot(at_ref[...], w_ba[...])
        p_br = _dot(pooled_o[...], w_bp[...])
        a_o[...] = a_br.astype(a_o.dtype)
        pp_o[...] = p_br.astype(pp_o.dtype)
        gates = _sig(gl_ref[...].astype(F32) + b_gate[...])
        merged = (gates[:, :D_MODEL] * a_br + gates[:, D_MODEL:] * p_br).astype(MX)
        merged_o[...] = merged
        y = _dot(merged, w_out[...])
        y_o[...] = y.astype(y_o.dtype)
        h1_o[...] = x_ref[...] + _rms(y, g_post[...])[0]

    outs = [_sds(n_tok, POOL_WIDTH, MX), _sds(n_tok, POOL_WIDTH, MX), _sds(n_tok, D_MODEL, MX), _sds(n_tok, D_MODEL, MX),
            _sds(n_tok, D_MODEL, MX), _sds(n_tok, D_MODEL, MX), _sds(n_tok, D_MODEL, F32)]
    res = [w["w_pool"], w["pool_scale"], w["w_branch_attn"], w["w_branch_pool"], w["b_gate"], w["w_out"], w["g_post_mix"]]
    return _tok_call("fwd_mix", body, n_tok, tm, [x, u, (u, halo_spec), gl, attn], res, outs,
                     scratch=[pltpu.VMEM((tm + HALO, POOL_WIDTH), F32)])


def _fwd_mlp(h1, w, tm):
    n_tok = h1.shape[0]

    def body(i, tin, res, tout, acc, scr):
        h1_ref, = tin
        g_pre, w1, w2, g_post = res
        m_o, zr_o, f_o, h2_o = tout
        h1_ = h1_ref[...]
        m = _rms(h1_, g_pre[...])[0].astype(MX)
        m_o[...] = m
        zr = jnp.maximum(_dot(m, w1[...]), 0.0)
        zr_o[...] = zr.astype(zr_o.dtype)
        a2 = (zr * zr).astype(MX)
        f = _dot(a2, w2[...])
        f_o[...] = f.astype(f_o.dtype)
        h2_o[...] = h1_ + _rms(f, g_post[...])[0]

    outs = [_sds(n_tok, D_MODEL, MX), _sds(n_tok, D_FF, MX), _sds(n_tok, D_MODEL, MX),
            _sds(n_tok, D_MODEL, F32)]
    res = [w["g_pre_mlp"], w["w_ff1"], w["w_ff2"], w["g_post_mlp"]]
    return _tok_call("fwd_mlp", body, n_tok, tm, [h1], res, outs)


def _ple_fwd_bwd(h2, p, target, w, tm):
    n_tok = h2.shape[0]

    def body(i, tin, res, tout, acc, scr):
        h2_ref, p_ref, t_ref = tin
        w_pe, w_pg, g_ple = res
        dh2_o, de_o, dzg_o = tout
        loss_a, dg_a = acc
        h2_ = h2_ref[...]
        e = _dot(p_ref[...], w_pe[...])
        pg = _sig(_dot(h2_, w_pg[...]))
        t = pg * e
        g = g_ple[...]
        tn, th, r = _rms(t, g)
        diff = h2_ + tn - t_ref[...]
        loss_a[...] += jnp.sum(diff * diff, axis=0, keepdims=True)
        dh3 = diff * (1.0 / D_MODEL)
        dt, dg = _rms_bwd(th, r, g, dh3)
        dg_a[...] += dg
        de_o[...] = (dt * pg).astype(de_o.dtype)
        dzg = (dt * e * pg * (1.0 - pg)).astype(MX)
        dzg_o[...] = dzg
        dh2_o[...] = dh3 + _dot_nt(dzg, w_pg[...])

    outs = [_sds(n_tok, D_MODEL, F32), _sds(n_tok, D_MODEL, MX), _sds(n_tok, D_MODEL, MX)]
    accs = [_sds(1, D_MODEL, F32), _sds(1, D_MODEL, F32)]
    return _tok_call("ple_fwd_bwd", body, n_tok, tm, [h2, p, target], [w["w_ple_proj"], w["w_ple_gate"], w["g_ple"]], outs, accs)


def _bwd_mlp(dh2, f, h1, zr, w, tm):
    n_tok = dh2.shape[0]

    def body(i, tin, res, tout, acc, scr):
        dh2_ref, f_ref, h1_ref, zr_ref = tin
        g_pre, w1, w2, g_post = res
        df_o, dz_o, dh1_o = tout
        dg_post_a, dg_pre_a = acc
        dh2_ = dh2_ref[...]
        gp = g_post[...]
        _, fh, rf = _rms(f_ref[...].astype(F32), gp)
        df, dg = _rms_bwd(fh, rf, gp, dh2_)
        dg_post_a[...] += dg
        df = df.astype(MX)
        df_o[...] = df
        dz = (_dot_nt(df, w2[...]) * (2.0 * zr_ref[...].astype(F32))).astype(MX)
        dz_o[...] = dz
        dm = _dot_nt(dz, w1[...])
        gq = g_pre[...]
        _, hh, rh = _rms(h1_ref[...], gq)
        dh1, dg = _rms_bwd(hh, rh, gq, dm)
        dg_pre_a[...] += dg
        dh1_o[...] = dh2_ + dh1

    outs = [_sds(n_tok, D_MODEL, MX), _sds(n_tok, D_FF, MX), _sds(n_tok, D_MODEL, F32)]
    accs = [_sds(1, D_MODEL, F32), _sds(1, D_MODEL, F32)]
    res = [w["g_pre_mlp"], w["w_ff1"], w["w_ff2"], w["g_post_mlp"]]
    return _tok_call("bwd_mlp", body, n_tok, tm, [dh2, f, h1, zr], res, outs, accs)


def _bwd_mix(dh1, y, a_br, p_br, gl, attn, d, w, tm):
    n_tok = dh1.shape[0]

    def body(i, tin, res, tout, acc, scr):
        dh1_ref, y_ref, a_ref, pp_ref, gl_ref, at_ref, d_ref = tin
        g_post, w_out, b_gate, w_ba, w_bp, w_pool, pool_scale, sel = res
        dy_o, da_o, dpp_o, dgl_o, do_o, delta_o, dyp_o, dd_o = tout
        dg_post_a, db_a, dps_a = acc
        g = g_post[...]
        _, yh, r = _rms(y_ref[...].astype(F32), g)
        dy, dg = _rms_bwd(yh, r, g, dh1_ref[...])
        dg_post_a[...] += dg
        dy = dy.astype(MX)
        dy_o[...] = dy
        dmerged = _dot_nt(dy, w_out[...])
        gates = _sig(gl_ref[...].astype(F32) + b_gate[...])
        ga, gp = gates[:, :D_MODEL], gates[:, D_MODEL:]
        da = (dmerged * ga).astype(MX)
        dpp = (dmerged * gp).astype(MX)
        da_o[...] = da
        dpp_o[...] = dpp
        dgl_a = dmerged * a_ref[...].astype(F32) * ga * (1.0 - ga)
        dgl_p = dmerged * pp_ref[...].astype(F32) * gp * (1.0 - gp)
        dgl_o[:, :D_MODEL] = dgl_a.astype(dgl_o.dtype)
        dgl_o[:, D_MODEL:] = dgl_p.astype(dgl_o.dtype)
        db_a[:, :D_MODEL] += jnp.sum(dgl_a, axis=0, keepdims=True)
        db_a[:, D_MODEL:] += jnp.sum(dgl_p, axis=0, keepdims=True)
        do = _dot_nt(da, w_ba[...]).astype(MX)
        do_o[...] = do
        prod = do.astype(F32) * at_ref[...].astype(F32)
        hi = prod.astype(MX)
        lo = (prod - hi.astype(F32)).astype(MX)
        delta_o[...] = _dot(hi, sel[...]) + _dot(lo, sel[...])
        dpooled = _dot_nt(dpp, w_bp[...])
        for gi in range(len(POOL_WINDOWS)):
            cs = slice(gi * POOL_GROUP, (gi + 1) * POOL_GROUP)
            ypre = _dot(d_ref[:, cs], w_pool[gi])
            dps_a[:, cs] += jnp.sum(dpooled[:, cs] * ypre, axis=0, keepdims=True)
            dyp = (dpooled[:, cs] * pool_scale[:, cs]).astype(MX)
            dyp_o[:, cs] = dyp
            dd_o[:, cs] = _dot_nt(dyp, w_pool[gi])

    outs = [_sds(n_tok, D_MODEL, MX), _sds(n_tok, D_MODEL, MX), _sds(n_tok, D_MODEL, MX), _sds(n_tok, 2 * D_MODEL, MX),
            _sds(n_tok, N_HEADS * V_HEAD, MX), _sds(n_tok, HEAD_SLOT, F32), _sds(n_tok, POOL_WIDTH, MX),
            _sds(n_tok, POOL_WIDTH, F32)]
    accs = [_sds(1, D_MODEL, F32), _sds(1, 2 * D_MODEL, F32), _sds(1, POOL_WIDTH, F32)]
    res = [w["g_post_mix"], w["w_out"], w["b_gate"], w["w_branch_attn"], w["w_branch_pool"], w["w_pool"], w["pool_scale"],
           w["head_sel"]]
    return _tok_call("bwd_mix", body, n_tok, tm, [dh1, y, a_br, p_br, gl, attn, d], res, outs, accs)


def _bwd_heads(q_ref, k_ref, v_ref, do_ref, lse_ref, dl_ref, st_s, dpt_s, keep, use, n_heads, qs, ks):
    def products(h):
        hs = slice(h * HEAD_SLOT, (h + 1) * HEAD_SLOT)
        vs = slice(h * V_HEAD, (h + 1) * V_HEAD)
        st_s[h % 2, ks, qs] = _dot_nt(k_ref[ks, hs], q_ref[qs, hs])
        dpt_s[h % 2, ks, qs] = _dot_nt(v_ref[ks, vs], do_ref[qs, vs])

    products(0)
    for h in range(n_heads):
        if h + 1 < n_heads:
            products(h + 1)
        st = st_s[h % 2, ks, qs]
        if keep is not None:
            st = jnp.where(keep, st, NEG)
        pt = jnp.exp2(st - lse_ref[h:h + 1, qs])
        use(h, pt, pt * (dpt_s[h % 2, ks, qs] - dl_ref[h:h + 1, qs]))


HEAD_GROUP = 4


def _attn_bwd(q, k, kt, v, do, lse, delta, tq, tk, scatter=None):
    n_tok = q.shape[0]
    nq, ratio = n_tok // tq, tq // tk
    n_groups = N_HEADS // HEAD_GROUP
    gq, gv = HEAD_GROUP * HEAD_SLOT, HEAD_GROUP * V_HEAD
    qi, kj = _causal_pairs(nq, ratio, by_kv=True)

    n_pairs = qi.shape[0]

    def kern(qi_ref, kj_ref, q_ref, k_ref, kt_ref, v_ref, do_ref, lse_ref, dl_ref, *rest):
        if scatter is not None:
            s_hbm, dq_ref, dk_ref, dv_ref, r_hbm, dk_s, dv_s, st_s, dpt_s, send_sems, recv_sems = rest
        else:
            dq_ref, dk_ref, dv_ref, dk_s, dv_s, st_s, dpt_s = rest
        s_id = pl.program_id(1)
        i, j = qi_ref[s_id], kj_ref[s_id]
        if scatter is not None:
            group = pl.program_id(0)
            _scatter_steps(jnp.logical_and(group == 0, s_id == 0),
                           jnp.logical_and(group == n_groups - 1, s_id == n_pairs - 1), s_hbm, r_hbm, (send_sems, recv_sems))

        @pl.when(s_id == 0)
        def _():
            dq_ref[...] = jnp.zeros(dq_ref.shape, F32)

        def heads(ks, first_query, masked):
            qs = slice(first_query, tq)
            cols = pl.ds(pl.multiple_of(i * tq + first_query, ATTN_DIAG_KEYS_BWD), tq - first_query)

            def use(h, pt, dst):
                hs = slice(h * HEAD_SLOT, (h + 1) * HEAD_SLOT)
                dv_s[h, ks, :] += _dot(pt, do_ref[qs, h * V_HEAD:(h + 1) * V_HEAD])
                dk_s[ks, hs] += _dot(dst, q_ref[qs, hs])
                dq_ref[hs, cols] += _dot(kt_ref[hs, ks], dst)

            keep = _keep_t(ks.stop - ks.start, tq - first_query, 0) if masked else None
            _bwd_heads(q_ref, k_ref, v_ref, do_ref, lse_ref.at[0], dl_ref.at[0], st_s, dpt_s, keep, use, HEAD_GROUP, qs, ks)

        @pl.when(j >= i * ratio)
        def _():
            dk_s[...] = jnp.zeros(dk_s.shape, F32)
            dv_s[...] = jnp.zeros(dv_s.shape, F32)

        sub = min(tk, ATTN_DIAG_KEYS_BWD)
        for part in range(ratio):
            @pl.when(j == i * ratio + part)
            def _():
                for first in range(0, tk, sub):
                    heads(slice(first, first + sub), part * tk + first, True)

        @pl.when(j < i * ratio)
        def _():
            heads(slice(0, tk), 0, False)

        @pl.when(i == nq - 1)
        def _():
            dk_ref[...] = (dk_s[...] * (1.0 / LOG2E)).astype(dk_ref.dtype)
            for h in range(HEAD_GROUP):
                dv_ref[:, h * V_HEAD:(h + 1) * V_HEAD] = dv_s[h].astype(dv_ref.dtype)

    at_q = lambda g, s, qi, kj: (qi[s], g)
    at_k = lambda g, s, qi, kj: (kj[s], g)
    at_kt = lambda g, s, qi, kj: (g, kj[s])
    at_stat = lambda g, s, qi, kj: (g, 0, qi[s])
    in_specs = [pl.BlockSpec((tq, gq), at_q), pl.BlockSpec((tk, gq), at_k), pl.BlockSpec((gq, tk), at_kt),
                pl.BlockSpec((tk, gv), at_k), pl.BlockSpec((tq, gv), at_q),
                pl.BlockSpec((1, HEAD_GROUP, tq), at_stat), pl.BlockSpec((1, HEAD_GROUP, tq), at_stat)]
    out_specs = [pl.BlockSpec((gq, n_tok), lambda g, s, qi, kj: (g, 0), pipeline_mode=pl.Buffered(1)),
                 pl.BlockSpec((tk, gq), at_k), pl.BlockSpec((tk, gv), at_k)]
    out_shape = [_sds(QK_WIDTH, n_tok, F32), _sds(n_tok, QK_WIDTH, MX), _sds(n_tok, N_HEADS * V_HEAD, MX)]
    scratch = [pltpu.VMEM((tk, gq), F32), pltpu.VMEM((HEAD_GROUP, tk, V_HEAD), F32),
               pltpu.VMEM((2, tk, tq), F32), pltpu.VMEM((2, tk, tq), F32)]
    stat3 = lambda a: a.reshape(n_groups, HEAD_GROUP, n_tok)
    operands = [qi, kj, q, k, kt, v, do, stat3(lse), stat3(delta)]
    if scatter is not None:
        operands.append(scatter)
        in_specs.append(_HBM)
        out_specs.append(_HBM)
        out_shape.append(jax.ShapeDtypeStruct((N_DEVICES, scatter.shape[1] // 2, PACK_COLS), scatter.dtype))
        scratch += [pltpu.SemaphoreType.DMA((N_DEVICES - 1,)), pltpu.SemaphoreType.DMA((N_DEVICES - 1,))]
    gs = pltpu.PrefetchScalarGridSpec(num_scalar_prefetch=2, grid=(n_groups, n_pairs), in_specs=in_specs,
                                      out_specs=out_specs, scratch_shapes=scratch)
    return pl.pallas_call(kern, name="attn_bwd", grid_spec=gs, out_shape=out_shape,
                          compiler_params=_params(("arbitrary", "arbitrary")))(*operands)


def _bwd_inproj(dq_t, dk, dv, dd, dgl, ps, x, dh1, cc, sa, w, tm):
    n_tok = x.shape[0]
    n_tiles = n_tok // tm
    last_halo = n_tok // HALO - 1
    halo_spec = pl.BlockSpec((HALO, POOL_WIDTH), lambda i: (jnp.minimum((i + 1) * (tm // HALO), last_halo), 0))

    def body(i, tin, res, tout, acc, scr):
        dq_ref, dk_ref, dv_ref, dd_ref, ddh_ref, dgl_ref, ps_ref, x_ref, dh1_ref, c_ref, s_ref = tin
        w_uq, g_q, w_k, w_v, e_mat, g_kv, w_in, g_pre = res
        dqu_o, dproj_o, dx_o = tout
        dgq_a, dgkv_a, dgpre_a = acc
        ext, = scr
        cc_, sa_ = c_ref[...], s_ref[...]
        for h in range(N_HEADS):
            hs = slice(h * HEAD_SLOT, (h + 1) * HEAD_SLOT)
            dqu_o[:, hs] = (_unrope(dq_ref[hs, :].T, cc_, sa_) * SCALE).astype(dqu_o.dtype)
        gq = g_q[...]
        _, qh, rq = _rms(ps_ref[:, :Q_LORA].astype(F32), gq)
        dqd, dg = _rms_bwd(qh, rq, gq, _dot_nt(dqu_o[...], w_uq[...]))
        dgq_a[...] += dg
        dproj_o[:, :Q_LORA] = dqd.astype(dproj_o.dtype)
        gkv = g_kv[...]
        _, kh, rk = _rms(ps_ref[:, Q_LORA:Q_LORA + KV_LORA].astype(F32), gkv)
        dkvd, dg = _rms_bwd(kh, rk, gkv, _dot_nt(dk_ref[...], w_k[...]) + _dot_nt(dv_ref[...], w_v[...]))
        dgkv_a[...] += dg
        dproj_o[:, Q_LORA:Q_LORA + KV_LORA] = dkvd.astype(dproj_o.dtype)
        dproj_o[:, Q_LORA + KV_LORA:SMALL_COLS] = _unrope(_dot_nt(dk_ref[...], e_mat[...]), cc_, sa_).astype(dproj_o.dtype)
        row = i * tm + lax.broadcasted_iota(jnp.int32, (tm + HALO, 1), 0)
        for gi, wdw in enumerate(POOL_WINDOWS):
            cs = slice(gi * POOL_GROUP, (gi + 1) * POOL_GROUP)
            inv = 1.0 / jnp.minimum(row + 1, wdw).astype(F32)
            ext[pl.ds(0, tm), cs] = dd_ref[:, cs] * inv[:tm]
            ext[pl.ds(tm, HALO), cs] = jnp.where(i < n_tiles - 1, ddh_ref[:, cs] * inv[tm:], 0.0)
            s = ext[pl.ds(0, tm), cs]
            for k_ in range(1, wdw):
                s = s + ext[pl.ds(k_, tm), cs]
            dproj_o[:, SMALL_COLS + gi * POOL_GROUP:SMALL_COLS + (gi + 1) * POOL_GROUP] = (s - dd_ref[:, cs]).astype(dproj_o.dtype)
        dproj_o[:, SMALL_COLS + POOL_WIDTH:] = dgl_ref[...]
        da = _dot_nt(dproj_o[...], w_in[...])
        gp = g_pre[...]
        _, xh, rx = _rms(x_ref[...], gp)
        dx, dg = _rms_bwd(xh, rx, gp, da)
        dgpre_a[...] += dg
        dx_o[...] = dh1_ref[...] + dx

    outs = [_sds(n_tok, QK_WIDTH, MX), _sds(n_tok, IN_PAD, MX), _sds(n_tok, D_MODEL, F32)]
    accs = [_sds(1, Q_LORA, F32), _sds(1, KV_LORA, F32), _sds(1, D_MODEL, F32)]
    res = [w["w_uq"], w["g_q"], w["w_k"], w["w_v"], w["e_mat"], w["g_kv"], w["w_in"], w["g_pre_mix"]]
    dq_spec = pl.BlockSpec((QK_WIDTH, tm), lambda i: (0, i))
    return _tok_call("bwd_inproj", body, n_tok, tm, [(dq_t, dq_spec), dk, dv, dd, (dd, halo_spec), dgl, ps, x, dh1, cc, sa], res, outs, accs,
                     scratch=[pltpu.VMEM((tm + HALO, POOL_WIDTH), F32)])


XTDY_TOKENS = 1024
XTDY_OUT_BYTES = 8 * 2**20
XTDY_IN_BYTES = 8 * 2**20


def _xtdy(name, x, dy, allreduce=None, square_x=False):
    n_tok, kk = x.shape
    nn = dy.shape[1]
    bk = kk
    while bk * nn * 4 > XTDY_OUT_BYTES and bk % 256 == 0:
        bk //= 2
    bt = min(XTDY_TOKENS, n_tok)
    while (2 * bt <= n_tok and n_tok % (2 * bt) == 0 and 2 * bt * nn * dy.dtype.itemsize <= XTDY_IN_BYTES
           and 2 * bt * bk * x.dtype.itemsize <= XTDY_IN_BYTES):
        bt *= 2

    grid = (kk // bk, n_tok // bt)

    def kern(x_ref, dy_ref, *rest):
        o_ref = rest[1] if allreduce is not None else rest[0]
        if allreduce is not None:
            g_ref, _, sum_ref, buf, send_sems, recv_sems = rest
            step = pl.program_id(0) * grid[1] + pl.program_id(1)
            _allreduce_steps(step == 0, step == grid[0] * grid[1] - 1, g_ref, sum_ref, buf, send_sems, recv_sems)

        @pl.when(pl.program_id(1) == 0)
        def _():
            o_ref[...] = jnp.zeros(o_ref.shape, F32)
        xv = x_ref[...]
        if square_x:
            xv = xv.astype(F32)
            xv = xv * xv
        o_ref[...] += _dot_tn(xv, dy_ref[...])

    operands = [x, dy]
    in_specs = [pl.BlockSpec((bt, bk), lambda a, t: (t, a)), pl.BlockSpec((bt, nn), lambda a, t: (t, 0))]
    out_specs = [pl.BlockSpec((bk, nn), lambda a, t: (a, 0))]
    out_shape = [_sds(kk, nn, F32)]
    scratch = []
    if allreduce is not None:
        vmem = pl.BlockSpec(memory_space=pltpu.VMEM)
        operands.append(allreduce)
        in_specs.append(vmem)
        out_specs.append(vmem)
        out_shape.append(jax.ShapeDtypeStruct(allreduce.shape, allreduce.dtype))
        scratch = [pltpu.VMEM((N_DEVICES,) + allreduce.shape, allreduce.dtype), pltpu.SemaphoreType.DMA((N_DEVICES - 1,)),
                   pltpu.SemaphoreType.DMA((N_DEVICES - 1,))]
    res = pl.pallas_call(kern, name=name, grid=grid, in_specs=in_specs, out_specs=out_specs, out_shape=out_shape,
                         scratch_shapes=scratch, compiler_params=_params(("arbitrary", "arbitrary")))(*operands)
    return res if allreduce is not None else res[0]


def _rope_tables(positions):
    inv_freq = ROPE_THETA ** (-jnp.arange(0, QK_ROPE, 2, dtype=F32) / QK_ROPE)
    ang_t = inv_freq[:, None] * positions.astype(F32)[None, :]
    cos_t, sin_t = lax.optimization_barrier((jnp.cos(ang_t), jnp.sin(ang_t)))
    cos, sin = cos_t.T, sin_t.T
    n_tok = positions.shape[0]
    ones, z64 = jnp.ones((n_tok, ROPE_LANE), F32), jnp.zeros((n_tok, ROPE_LANE), F32)
    z32 = jnp.zeros((n_tok, HEAD_SLOT - ROPE_LANE - QK_ROPE), F32)
    return jnp.concatenate([ones, cos, cos, z32], 1), jnp.concatenate([z64, -sin, sin, z32], 1)


def _kernel_weights(full):
    w_in, w_uq, w_ukv = full["w_in"], full["w_uq"], full["w_ukv"]
    c0 = Q_LORA + KV_LORA
    z = lambda n: jnp.zeros((D_MODEL, n), w_in.dtype)
    w = dict(full)
    w["w_in"] = jnp.concatenate([w_in[:, :c0], z(ROPE_LANE), w_in[:, c0:c0 + QK_ROPE], z(HEAD_SLOT - ROPE_LANE - QK_ROPE),
                                 w_in[:, c0 + QK_ROPE:]], 1)
    w["w_uq"] = jnp.pad(w_uq.reshape(Q_LORA, N_HEADS, QK_NOPE + QK_ROPE),
                        ((0, 0), (0, 0), (0, HEAD_SLOT - QK_NOPE - QK_ROPE))).reshape(Q_LORA, QK_WIDTH)
    kv = w_ukv.reshape(KV_LORA, N_HEADS, QK_NOPE + V_HEAD)
    w["w_k"] = jnp.pad(kv[:, :, :QK_NOPE], ((0, 0), (0, 0), (0, HEAD_SLOT - QK_NOPE))).reshape(KV_LORA, QK_WIDTH)
    w["w_v"] = kv[:, :, QK_NOPE:].reshape(KV_LORA, N_HEADS * V_HEAD)
    e = np.zeros((HEAD_SLOT, QK_WIDTH), np.float32)
    sel = np.zeros((N_HEADS * V_HEAD, HEAD_SLOT), np.float32)
    for h in range(N_HEADS):
        for r in range(QK_ROPE):
            e[ROPE_LANE + r, h * HEAD_SLOT + ROPE_LANE + r] = 1.0
        sel[h * V_HEAD:(h + 1) * V_HEAD, h] = 1.0
    w["e_mat"] = jnp.asarray(e, MX)
    w["w_kt"], w["e_t"] = w["w_k"].T, jnp.asarray(e.T, MX)
    pad = ((0, 0), (0, V_ROWS - V_HEAD), (0, 0))
    w["w_vt"] = jnp.pad(w["w_v"].T.reshape(N_HEADS, V_HEAD, KV_LORA), pad).reshape(N_HEADS * V_ROWS, KV_LORA)
    ones = np.zeros((N_HEADS, V_ROWS, 1), np.float32)
    ones[:, V_HEAD] = 1.0
    w["v_ones"] = jnp.asarray(ones.reshape(N_HEADS * V_ROWS, 1))
    w["head_sel"] = jnp.asarray(sel, MX)
    w["w_pool"] = full["w_pool"].astype(MX)
    return w


def _local_step(x, p, positions, target, full, mesh_place=None, packed_rest=None):
    n_tok = x.shape[0]
    tm = tm_mlp = min(TOKEN_TILE, n_tok)
    fwd_tile = [min(t, n_tok) for t in ATTN_FWD_TILE]
    bwd_tile = [min(t, n_tok) for t in ATTN_BWD_TILE]
    w = _kernel_weights(full)
    cc, sa = _rope_tables(positions)

    if mesh_place is None:
        a, ps, u, gl, qn, kvn, q, k, v, kt, vt = _fwd_inproj(x, cc, sa, w, tm)
    else:
        my_chip, core = mesh_place
        a, ps, u, gl, qn, kvn, q, k, v, kt, vt, gathered = _fwd_inproj(x, cc, sa, w, tm, gather=packed_rest)
        w.update(_unpack_full(gathered, packed_rest, my_chip, REST))
    attn, lse = _attn_fwd(q, k, vt, *fwd_tile)
    d, pooled, a_br, p_br, merged, y, h1 = _fwd_mix(x, u, gl, attn, w, tm)
    m, zr, f, h2 = _fwd_mlp(h1, w, tm_mlp)
    dh2, de, dzg, loss_cols, dg_ple = _ple_fwd_bwd(h2, p, target, w, tm)
    df, dz, dh1, dg_post_mlp, dg_pre_mlp = _bwd_mlp(dh2, f, h1, zr, w, tm_mlp)
    dy, da_br, dp_br, dgl, do, delta, dyp, dd, dg_post_mix, db_gate, dpool_scale = _bwd_mix(dh1, y, a_br, p_br, gl, attn, d, w, tm)
    grads = {"w_branch_attn": _xtdy("dw_ba", attn, da_br), "w_branch_pool": _xtdy("dw_bp", pooled, dp_br),
             "w_out": _xtdy("dw_out", merged, dy), "w_ff1": _xtdy("dw_ff1", m, dz), "w_ff2": _xtdy("dw_ff2", zr, df, square_x=True),
             "w_ple_proj": _xtdy("dw_pe", p, de), "w_ple_gate": _xtdy("dw_pg", h2, dzg)}
    delta_t = delta[:, :N_HEADS].T
    if mesh_place is None:
        travelling = None
        dq_t, dk, dv = _attn_bwd(q, k, kt, v, do, lse, delta_t, *bwd_tile)
    else:
        pieces = _pack_pieces(grads, REST, WIRE)
        dq_t, dk, dv, received = _attn_bwd(q, k, kt, v, do, lse, delta_t, *bwd_tile, scatter=pieces)
        travelling = (pieces, received)
        grads = {}
    dqu, dproj, dx, dg_q, dg_kv, dg_pre_mix = _bwd_inproj(dq_t, dk, dv, dd, dgl, ps, x, dh1, cc, sa, w, tm)

    g_uq = _xtdy("dw_uq", qn, dqu)
    g_k = _xtdy("dw_k", kvn, dk)
    g_v = _xtdy("dw_v", kvn, dv)
    g_pool = _xtdy("dw_pool", d, dyp)
    small = {"g_pre_mix": dg_pre_mix, "b_gate": db_gate, "g_q": dg_q, "g_kv": dg_kv, "pool_scale": dpool_scale,
             "g_post_mix": dg_post_mix, "g_pre_mlp": dg_pre_mlp, "g_post_mlp": dg_post_mlp, "g_ple": dg_ple,
             "w_pool": jnp.stack([g_pool[g * POOL_GROUP:(g + 1) * POOL_GROUP, g * POOL_GROUP:(g + 1) * POOL_GROUP]
                                  for g in range(len(POOL_WINDOWS))])}
    if mesh_place is None:
        g_in = _xtdy("dw_in", a, dproj)
    else:
        g_in, small_sum = _xtdy("dw_in", a, dproj, allreduce=_pack_small(small, loss_cols))
        small, loss_cols = _unpack_small(small_sum), small_sum[-1:]

    c0 = Q_LORA + KV_LORA
    grads.update(small)
    grads.update({
        "w_in": jnp.concatenate([g_in[:, :c0], g_in[:, c0 + ROPE_LANE:c0 + ROPE_LANE + QK_ROPE], g_in[:, SMALL_COLS:]], 1),
        "w_uq": g_uq.reshape(Q_LORA, N_HEADS, HEAD_SLOT)[:, :, :QK_NOPE + QK_ROPE].reshape(Q_LORA, N_HEADS * (QK_NOPE + QK_ROPE)),
        "w_ukv": jnp.concatenate([g_k.reshape(KV_LORA, N_HEADS, HEAD_SLOT)[:, :, :QK_NOPE],
                                  g_v.reshape(KV_LORA, N_HEADS, V_HEAD)], 2).reshape(KV_LORA, N_HEADS * (QK_NOPE + V_HEAD)),
    })
    return loss_cols, dx, grads, travelling


def _place():
    return lax.axis_index("x"), lax.axis_index("y"), lax.axis_index("c")


CHIP_FLIPS = ((1, 0), (0, 1), (1, 1))


def _flip(x, y, fx, fy):
    return (1 - x if fx else x), (1 - y if fy else y)


_HBM = pl.BlockSpec(memory_space=pl.ANY)


def _gather_copies(w_ref, out_ref, send_sems, recv_sems):
    half = w_ref.shape[0] // 2
    x, y, c = _place()
    my_chip = 2 * x + y
    sibling = (x, y, 1 - c)

    def half_of(chip, hc):
        return out_ref.at[chip, pl.ds(pl.multiple_of(hc * half, 16), half), :]

    src = w_ref.at[pl.ds(pl.multiple_of(c * half, 16), half), :]
    sends, landed, forwards, from_sibling = [], [], [], []
    for j, (fx, fy) in enumerate(CHIP_FLIPS):
        px, py = _flip(x, y, fx, fy)
        mine_there, theirs_here, theirs_other = half_of(my_chip, c), half_of(2 * px + py, c), half_of(2 * px + py, 1 - c)
        sends.append(pltpu.make_async_remote_copy(src, mine_there, send_sems.at[j], recv_sems.at[j],
                                                  device_id=(px, py, c), device_id_type=MESH))
        landed.append(pltpu.make_async_remote_copy(src, theirs_here, send_sems.at[j], recv_sems.at[j],
                                                   device_id=(px, py, c), device_id_type=MESH))
        forwards.append(pltpu.make_async_remote_copy(theirs_here, theirs_here, send_sems.at[3 + j], recv_sems.at[3 + j],
                                                     device_id=sibling, device_id_type=MESH))
        from_sibling.append(pltpu.make_async_remote_copy(theirs_other, theirs_other, send_sems.at[3 + j],
                                                         recv_sems.at[3 + j], device_id=sibling, device_id_type=MESH))
    return sends, landed, forwards, from_sibling


def _gather_steps(i, n_steps, w_ref, out_ref, sems):
    sends, landed, forwards, from_sibling = _gather_copies(w_ref, out_ref, *sems)

    @pl.when(i == 0)
    def _():
        for cp in sends:
            cp.start()

    @pl.when(i == (3 * n_steps) // 4)
    def _():
        for arrived, fwd in zip(landed, forwards):
            arrived.wait_recv()
            fwd.start()

    @pl.when(i == n_steps - 1)
    def _():
        for cp in from_sibling:
            cp.wait_recv()
        for cp in sends + forwards:
            cp.wait_send()


def _allgather_shards(wp):
    def body(w_ref, out_ref, send_sems, recv_sems):
        sends, landed, forwards, from_sibling = _gather_copies(w_ref, out_ref, send_sems, recv_sems)
        for cp in sends:
            cp.start()
        for arrived, fwd in zip(landed, forwards):
            arrived.wait_recv()
            fwd.start()
        for cp in from_sibling:
            cp.wait_recv()
        for cp in sends + forwards:
            cp.wait_send()

    return pl.pallas_call(
        body, name="allgather_shards", out_shape=jax.ShapeDtypeStruct((N_CHIPS,) + wp.shape, wp.dtype),
        in_specs=[_HBM], out_specs=_HBM,
        scratch_shapes=[pltpu.SemaphoreType.DMA((6,)), pltpu.SemaphoreType.DMA((6,))],
    )(wp)


def _exchange_halves(g):
    rows = g.shape[1]
    half = rows // 2

    def body(g_ref, r_ref, send_sem, recv_sem):
        x, y, c = _place()
        src = g_ref.at[:, pl.ds(pl.multiple_of((1 - c) * half, 8), half), :]
        cp = pltpu.make_async_remote_copy(src, r_ref, send_sem, recv_sem, device_id=(x, y, 1 - c), device_id_type=MESH)
        cp.start()
        cp.wait()

    return pl.pallas_call(
        body, name="exchange_halves", out_shape=jax.ShapeDtypeStruct((N_CHIPS, half, PACK_COLS), g.dtype),
        in_specs=[_HBM], out_specs=_HBM, scratch_shapes=[pltpu.SemaphoreType.DMA, pltpu.SemaphoreType.DMA],
    )(g)


def _add_halves(g, r, c):
    rows = g.shape[1]
    half = rows // 2
    br = REDUCE_ROWS
    nb = half // br

    def kern(c_ref, g_ref, r_ref, o_ref):
        o_ref[...] = (g_ref[...] + r_ref[...]).astype(o_ref.dtype)

    gs = pltpu.PrefetchScalarGridSpec(
        num_scalar_prefetch=1, grid=(N_CHIPS, nb),
        in_specs=[pl.BlockSpec((1, br, PACK_COLS), lambda k, t, c: (k, c[0] * nb + t, 0)),
                  pl.BlockSpec((1, br, PACK_COLS), lambda k, t, c: (k, t, 0))],
        out_specs=pl.BlockSpec((1, br, PACK_COLS), lambda k, t, c: (k, t, 0)))
    return pl.pallas_call(kern, name="add_halves", grid_spec=gs,
                          out_shape=jax.ShapeDtypeStruct((N_CHIPS, half, PACK_COLS), WIRE),
                          compiler_params=_params(("arbitrary", "arbitrary")))(c.reshape(1), g, r)


def _scatter_copies(s_ref, r_ref, send_sems, recv_sems):
    x, y, c = _place()
    my_chip = 2 * x + y
    sends, arrivals = [], []
    for j, (fx, fy) in enumerate(CHIP_FLIPS):
        px, py = _flip(x, y, fx, fy)
        slot = r_ref.at[2 * px + py]
        sends.append(pltpu.make_async_remote_copy(s_ref.at[2 * px + py], r_ref.at[my_chip], send_sems.at[j], recv_sems.at[j],
                                                  device_id=(px, py, c), device_id_type=MESH))
        arrivals.append(pltpu.make_async_remote_copy(slot, slot, send_sems.at[j], recv_sems.at[j],
                                                     device_id=(px, py, c), device_id_type=MESH))
    return sends, arrivals


N_DEVICES = 8


def _peer(x, y, c, f):
    px, py = _flip(x, y, f & 4, f & 2)
    return px, py, (1 - c if f & 1 else c)


def _scatter_all_copies(p_ref, r_ref, send_sems, recv_sems):
    half = p_ref.shape[1] // 2
    x, y, c = _place()
    me = 4 * x + 2 * y + c
    sends, arrivals = [], []
    for f in range(1, N_DEVICES):
        px, py, pc = _peer(x, y, c, f)
        theirs = p_ref.at[2 * px + py, pl.ds(pl.multiple_of(pc * half, 16), half), :]
        slot = r_ref.at[4 * px + 2 * py + pc]
        sends.append(pltpu.make_async_remote_copy(theirs, r_ref.at[me], send_sems.at[f - 1], recv_sems.at[f - 1],
                                                  device_id=(px, py, pc), device_id_type=MESH))
        arrivals.append(pltpu.make_async_remote_copy(slot, slot, send_sems.at[f - 1], recv_sems.at[f - 1],
                                                     device_id=(px, py, pc), device_id_type=MESH))
    return sends, arrivals


def _scatter_steps(first, last, p_ref, r_ref, sems):
    sends, arrivals = _scatter_all_copies(p_ref, r_ref, *sems)

    @pl.when(first)
    def _():
        for cp in sends:
            cp.start()

    @pl.when(last)
    def _():
        for cp in arrivals:
            cp.wait_recv()
        for cp in sends:
            cp.wait_send()


def _scatter_pieces(s):
    def body(s_ref, r_ref, send_sems, recv_sems):
        sends, arrivals = _scatter_copies(s_ref, r_ref, send_sems, recv_sems)
        for cp in sends:
            cp.start()
        for cp in arrivals:
            cp.wait_recv()
        for cp in sends:
            cp.wait_send()

    return pl.pallas_call(
        body, name="scatter_pieces", out_shape=jax.ShapeDtypeStruct(s.shape, s.dtype), in_specs=[_HBM], out_specs=_HBM,
        scratch_shapes=[pltpu.SemaphoreType.DMA((3,)), pltpu.SemaphoreType.DMA((3,))],
    )(s)


def _sum_pieces(r, mine, slot):
    slots, half = r.shape[:2]
    br = REDUCE_ROWS

    def kern(slot_ref, r_ref, m_ref, o_ref):
        total = None
        for k in range(slots):
            term = jnp.where(slot_ref[0] == k, m_ref[0], r_ref[k]).astype(F32)
            total = term if total is None else total + term
        o_ref[...] = total

    gs = pltpu.PrefetchScalarGridSpec(
        num_scalar_prefetch=1, grid=(half // br,),
        in_specs=[pl.BlockSpec((slots, br, PACK_COLS), lambda t, s: (0, t, 0)),
                  pl.BlockSpec((1, br, PACK_COLS), lambda t, s: (0, t, 0))],
        out_specs=pl.BlockSpec((br, PACK_COLS), lambda t, s: (t, 0)))
    return pl.pallas_call(kern, name="sum_pieces", grid_spec=gs, out_shape=_sds(half, PACK_COLS, F32),
                          compiler_params=_params(("arbitrary",)))(slot.reshape(1), r, mine)


def _join_halves(f):
    def body(f_ref, o_ref, send_sem, recv_sem):
        x, y, c = _place()
        cp = pltpu.make_async_remote_copy(f_ref, o_ref, send_sem, recv_sem, device_id=(x, y, 1 - c), device_id_type=MESH)
        cp.start()
        cp.wait()

    return pl.pallas_call(
        body, name="join_halves", out_shape=jax.ShapeDtypeStruct(f.shape, f.dtype), in_specs=[_HBM], out_specs=_HBM,
        scratch_shapes=[pltpu.SemaphoreType.DMA, pltpu.SemaphoreType.DMA],
    )(f)


def _allreduce_steps(first, last, g_ref, o_ref, buf, send_sems, recv_sems):
    x, y, c = _place()
    me = 4 * x + 2 * y + c
    sends, arrivals = [], []
    for f in range(1, N_DEVICES):
        px, py, pc = _peer(x, y, c, f)
        slot = buf.at[4 * px + 2 * py + pc]
        sends.append(pltpu.make_async_remote_copy(g_ref, buf.at[me], send_sems.at[f - 1], recv_sems.at[f - 1],
                                                  device_id=(px, py, pc), device_id_type=MESH))
        arrivals.append(pltpu.make_async_remote_copy(slot, slot, send_sems.at[f - 1], recv_sems.at[f - 1],
                                                     device_id=(px, py, pc), device_id_type=MESH))

    @pl.when(first)
    def _():
        buf[me] = g_ref[...]
        for cp in sends:
            cp.start()

    @pl.when(last)
    def _():
        for cp in arrivals:
            cp.wait_recv()
        for cp in sends:
            cp.wait_send()
        total = buf[0]
        for k in range(1, N_DEVICES):
            total = total + buf[k]
        o_ref[...] = total


def _adamw_update(g_ref, w_ref, m_ref, v_ref, d_o, m_o, v_o):
    c1 = 1.0 - ADAM_B1 ** ADAM_STEP
    c2 = 1.0 - ADAM_B2 ** ADAM_STEP
    g_ = g_ref[...]
    m_new = ADAM_B1 * m_ref[...] + (1.0 - ADAM_B1) * g_
    v_new = ADAM_B2 * v_ref[...] + (1.0 - ADAM_B2) * (g_ * g_)
    m_o[...] = m_new
    v_o[...] = v_new
    d_o[...] = -ADAM_LR * ((m_new / c1) / (jnp.sqrt(v_new / c2) + ADAM_EPS) + ADAM_WD * w_ref[...])


ADAMW_ROWS = 256


def _adamw(name, g, w, m, v):
    _, rows, cols = w.shape
    br = int(np.gcd(ADAMW_ROWS, rows))

    def kern(*refs):
        _adamw_update(*refs)

    spec = pl.BlockSpec((1, br, cols), lambda t: (0, t, 0))
    out = jax.ShapeDtypeStruct(w.shape, F32)
    return pl.pallas_call(kern, name="adamw_" + name, grid=(rows // br,), in_specs=[spec] * 4, out_specs=[spec] * 3,
                          out_shape=[out, out, out], compiler_params=_params(("arbitrary",)))(g, w, m, v)


def _adamw_small(gs, ws, ms, vs):
    n = len(gs)

    def kern(*refs):
        ins, outs = refs[:4 * n], refs[4 * n:]
        for k in range(n):
            _adamw_update(ins[k], ins[n + k], ins[2 * n + k], ins[3 * n + k], outs[k], outs[n + k], outs[2 * n + k])

    vmem = pl.BlockSpec(memory_space=pltpu.VMEM)
    out = [jax.ShapeDtypeStruct(w.shape, F32) for w in ws]
    res = pl.pallas_call(kern, name="adamw_small", in_specs=[vmem] * (4 * n), out_specs=[vmem] * (3 * n),
                         out_shape=out * 3, compiler_params=pltpu.CompilerParams(vmem_limit_bytes=VMEM_LIMIT))(*gs, *ws, *ms, *vs)
    return [(res[k], res[n + k], res[2 * n + k]) for k in range(n)]


def _shard_rows(shape, axis):
    k, n = shape
    return (k * n // N_CHIPS) // PACK_COLS


def _group(names):
    entries = [e for e in SHARDED if e[0] in names]
    used = sum(_shard_rows(shape, axis) for _, shape, axis in entries)
    return entries, -(-used // (2 * REDUCE_ROWS)) * 2 * REDUCE_ROWS


def _pack_shards(shards, names, dtype):
    entries, rows = _group(names)
    parts = [shards[name].astype(dtype).reshape(-1, PACK_COLS) for name, _, _ in entries]
    used = sum(p.shape[0] for p in parts)
    if rows > used:
        parts.append(jnp.zeros((rows - used, PACK_COLS), dtype))
    return jnp.concatenate(parts, 0)


def _unpack_shards(packed, names):
    out, r0 = {}, 0
    for name, (k, n), axis in _group(names)[0]:
        nr = _shard_rows((k, n), axis)
        shape = (k // N_CHIPS, n) if axis == 0 else (k, n // N_CHIPS)
        out[name] = packed[r0:r0 + nr].reshape(shape)
        r0 += nr
    return out


def _unpack_full(gathered, own, my_chip, names):
    chip = lax.broadcasted_iota(jnp.int32, (N_CHIPS, 1, 1), 0)
    gathered = jnp.where(chip == my_chip, own[None], gathered)
    out, r0 = {}, 0
    for name, (k, n), axis in _group(names)[0]:
        nr = _shard_rows((k, n), axis)
        part = gathered[:, r0:r0 + nr]
        if axis == 0:
            out[name] = part.reshape(k, n)
        else:
            out[name] = part.reshape(N_CHIPS, k, n // N_CHIPS).transpose(1, 0, 2).reshape(k, n)
        r0 += nr
    return out


def _pack_pieces(grads, names, dtype=F32):
    entries, rows = _group(names)
    parts = []
    for name, (k, n), axis in entries:
        g = grads[name].astype(dtype)
        if axis == 0:
            parts.append(g.reshape(N_CHIPS, -1, PACK_COLS))
        else:
            parts.append(g.reshape(k, N_CHIPS, n // N_CHIPS).transpose(1, 0, 2).reshape(N_CHIPS, -1, PACK_COLS))
    used = sum(p.shape[1] for p in parts)
    if rows > used:
        parts.append(jnp.zeros((N_CHIPS, rows - used, PACK_COLS), dtype))
    return jnp.concatenate(parts, 1)


def _pack_small(vals, last_row):
    flat = jnp.concatenate([vals[name].astype(F32).reshape(-1) for name, _ in SMALL])
    spare = jnp.zeros(((SMALL_ROWS - 1) * PACK_COLS - flat.shape[0],), F32)
    return jnp.concatenate([flat, spare, last_row.reshape(-1)]).reshape(SMALL_ROWS, PACK_COLS)


def _unpack_small(packed):
    flat, out, o = packed.reshape(-1), {}, 0
    for name, shape in SMALL:
        n = int(np.prod(shape))
        out[name] = flat[o:o + n].reshape(shape)
        o += n
    return out


def kernel(x, p, positions, g_pre_mix, w_in, b_gate, g_q, w_uq, g_kv, w_ukv, w_pool, pool_scale, w_branch_attn, w_branch_pool, w_out, g_post_mix, g_pre_mlp, w_ff1, w_ff2, g_post_mlp, w_ple_proj, w_ple_gate, g_ple, loss_target, m_g_pre_mix, m_w_in, m_b_gate, m_g_q, m_w_uq, m_g_kv, m_w_ukv, m_w_pool, m_pool_scale, m_w_branch_attn, m_w_branch_pool, m_w_out, m_g_post_mix, m_g_pre_mlp, m_w_ff1, m_w_ff2, m_g_post_mlp, m_w_ple_proj, m_w_ple_gate, m_g_ple, v_g_pre_mix, v_w_in, v_b_gate, v_g_q, v_w_uq, v_g_kv, v_w_ukv, v_w_pool, v_pool_scale, v_w_branch_attn, v_w_branch_pool, v_w_out, v_g_post_mix, v_g_pre_mlp, v_w_ff1, v_w_ff2, v_g_post_mlp, v_w_ple_proj, v_w_ple_gate, v_g_ple):
    given = dict(locals())
    weights = {n: given[n] for n in WEIGHT_ORDER}
    moments_m = {n: given["m_" + n] for n in WEIGHT_ORDER}
    moments_v = {n: given["v_" + n] for n in WEIGHT_ORDER}
    c = lax.axis_index("c")

    big_w = {name: weights[name][0] for name, _, _ in SHARDED}
    my_chip = 2 * lax.axis_index("x") + lax.axis_index("y")
    packed_first = _pack_shards(big_w, FIRST, MX)
    full = _unpack_full(_allgather_shards(packed_first), packed_first, my_chip, FIRST)
    for name, _ in SMALL:
        full[name] = weights[name][0] if name == "w_pool" else weights[name]

    loss_cols, dx, grads, (pieces_rest, received_rest) = _local_step(
        x[0], p[0, 0], positions[0], loss_target[0], full, (my_chip, c), _pack_shards(big_w, REST, MX))
    loss = 0.5 * jnp.sum(loss_cols) / D_MODEL

    def finish(received, mine, slot):
        reduced = _sum_pieces(received, mine, slot)
        theirs = _join_halves(reduced)
        return jnp.where(c == 0, jnp.concatenate([reduced, theirs]), jnp.concatenate([theirs, reduced]))

    pieces = _pack_pieces(grads, FIRST)
    sent = _add_halves(pieces, _exchange_halves(pieces), c)
    mine = lax.dynamic_slice(sent, (my_chip, 0, 0), (1,) + sent.shape[1:])
    shards = _unpack_shards(finish(_scatter_pieces(sent), mine, my_chip), FIRST)
    half = received_rest.shape[1]
    mine = lax.dynamic_slice(pieces_rest, (my_chip, c * half, 0), (1, half, PACK_COLS))
    shards.update(_unpack_shards(finish(received_rest, mine, 2 * my_chip + c), REST))

    out = {}
    for name, g in shards.items():
        out[name] = (g[None], *_adamw(name, g[None], weights[name], moments_m[name], moments_v[name]))
    small_g = {n: grads[n] for n, _ in SMALL}
    names = [n for n, _ in SMALL]
    updates = _adamw_small([small_g[n] for n in names], [weights[n] for n in names], [moments_m[n] for n in names],
                           [moments_v[n] for n in names])
    for n, upd in zip(names, updates):
        out[n] = (small_g[n], *upd)
    return (loss, dx[None], *[out[n][k] for k in range(4) for n in WEIGHT_ORDER])
```

```python
import numpy as np
import jax
import jax.numpy as jnp
from jax import lax
from jax.experimental import pallas as pl
from jax.experimental.pallas import tpu as pltpu

F32 = jnp.float32
MX = jnp.bfloat16
WIRE = jnp.bfloat16

D_MODEL = 1024
N_HEADS = 8
QK_NOPE = 64
QK_ROPE = 32
V_HEAD = 64
Q_LORA = 384
KV_LORA = 256
POOL_WINDOWS = (2, 4, 8, 16)
POOL_GROUP = 128
POOL_WIDTH = 512
D_FF = 4096
ROPE_THETA = 10000.0
EPS = 1e-6
HEAD_SLOT = 128
QK_WIDTH = N_HEADS * HEAD_SLOT
ROPE_LANE = 64
SMALL_COLS = Q_LORA + KV_LORA + HEAD_SLOT
IN_PAD = SMALL_COLS + POOL_WIDTH + 2 * D_MODEL
SCALE = (QK_NOPE + QK_ROPE) ** -0.5
LOG2E = 1.4426950408889634
NEG = -1e30
HALO = 16

ADAM_LR = 0.001
ADAM_B1 = 0.9
ADAM_B2 = 0.999
ADAM_EPS = 1e-08
ADAM_WD = 0.01
ADAM_STEP = 10

VMEM_LIMIT = 56 * 2**20
TOKEN_TILE = 512
MESH = pl.DeviceIdType.MESH

SHARDED = (
    ("w_in", (1024, 3232), 1),
    ("w_uq", (384, 768), 1),
    ("w_ukv", (256, 1024), 1),
    ("w_branch_attn", (512, 1024), 1),
    ("w_branch_pool", (512, 1024), 1),
    ("w_out", (1024, 1024), 0),
    ("w_ff1", (1024, 4096), 1),
    ("w_ff2", (4096, 1024), 0),
    ("w_ple_proj", (256, 1024), 1),
    ("w_ple_gate", (1024, 1024), 0),
)
SMALL = (
    ("g_pre_mix", (1, 1024)),
    ("b_gate", (1, 2048)),
    ("g_q", (1, 384)),
    ("g_kv", (1, 256)),
    ("w_pool", (1, 4, 128, 128)),
    ("pool_scale", (1, 512)),
    ("g_post_mix", (1, 1024)),
    ("g_pre_mlp", (1, 1024)),
    ("g_post_mlp", (1, 1024)),
    ("g_ple", (1, 1024)),
)
WEIGHT_ORDER = ("g_pre_mix", "w_in", "b_gate", "g_q", "w_uq", "g_kv", "w_ukv", "w_pool", "pool_scale", "w_branch_attn",
                "w_branch_pool", "w_out", "g_post_mix", "g_pre_mlp", "w_ff1", "w_ff2", "g_post_mlp", "w_ple_proj",
                "w_ple_gate", "g_ple")
N_CHIPS = 4
PACK_COLS = 1024
REDUCE_ROWS = 160
SMALL_ROWS = 80
FIRST = ("w_in", "w_uq", "w_ukv")
REST = tuple(name for name, _, _ in SHARDED if name not in FIRST)


def _dot(a, b):
    return jnp.dot(a.astype(MX), b.astype(MX), preferred_element_type=F32)


def _dot_nt(a, b):
    return lax.dot_general(a.astype(MX), b.astype(MX), (((1,), (1,)), ((), ())), preferred_element_type=F32)


def _dot_tn(a, b):
    return lax.dot_general(a.astype(MX), b.astype(MX), (((0,), (0,)), ((), ())), preferred_element_type=F32)


def _sig(x):
    return 1.0 / (1.0 + jnp.exp(-x))


def _rms(x, g):
    r = lax.rsqrt(jnp.mean(x * x, axis=1, keepdims=True) + EPS)
    xh = x * r
    return xh * g, xh, r


def _rms_bwd(xh, r, g, dy):
    dxn = dy * g
    dx = r * (dxn - xh * jnp.mean(dxn * xh, axis=1, keepdims=True))
    return dx, jnp.sum(dy * xh, axis=0, keepdims=True)


def _rot_half(v):
    lane = lax.broadcasted_iota(jnp.int32, v.shape, 1)
    return jnp.where(lane < ROPE_LANE + QK_ROPE // 2, pltpu.roll(v, HEAD_SLOT - QK_ROPE // 2, 1), pltpu.roll(v, QK_ROPE // 2, 1))


def _rope(v, cc, sa):
    return v * cc + _rot_half(v) * sa


def _unrope(v, cc, sa):
    return v * cc - _rot_half(v) * sa


def _params(sem):
    return pltpu.CompilerParams(dimension_semantics=sem, vmem_limit_bytes=VMEM_LIMIT)


def _tok_call(name, body, n_tok, tm, tiled, resident, outs, accs=(), scratch=(), exchange=None):
    def as_pair(t):
        if isinstance(t, tuple):
            return t
        return t, pl.BlockSpec((tm, t.shape[1]), lambda i: (i, 0))
    tiled = [as_pair(t) for t in tiled]
    outs = [as_pair(o) for o in outs]
    res_specs = [pl.BlockSpec(r.shape, lambda i, nd=r.ndim: (0,) * nd, pipeline_mode=pl.Buffered(1)) for r in resident]
    out_specs = [s for _, s in outs] + [pl.BlockSpec(a.shape, lambda i: (0, 0)) for a in accs]
    n_t, n_r, n_o, n_a, n_s = len(tiled), len(resident), len(outs), len(accs), len(scratch)
    n_steps = n_tok // tm
    operands = [a for a, _ in tiled] + list(resident)
    in_specs = [s for _, s in tiled] + res_specs
    out_shape = [o for o, _ in outs] + list(accs)
    scratch = list(scratch)
    if exchange is not None:
        ex_in, ex_out, ex_sems, ex_steps = exchange
        operands.append(ex_in)
        in_specs.append(_HBM)
        out_shape.append(ex_out)
        out_specs.append(_HBM)
        scratch += list(ex_sems)

    def kern(*refs):
        refs = list(refs)
        n_in = n_t + n_r + (exchange is not None)
        n_out = n_o + n_a + (exchange is not None)
        tin, res = refs[:n_t], refs[n_t:n_t + n_r]
        tout = refs[n_in:n_in + n_o]
        acc = refs[n_in + n_o:n_in + n_o + n_a]
        scr = refs[n_in + n_out:n_in + n_out + n_s]
        i = pl.program_id(0)
        if exchange is not None:
            ex_steps(i, n_steps, refs[n_in - 1], refs[n_in + n_out - 1], refs[n_in + n_out + n_s:])

        @pl.when(i == 0)
        def _():
            for a in acc:
                a[...] = jnp.zeros(a.shape, a.dtype)
        body(i, tin, res, tout, acc, scr)

    return pl.pallas_call(
        kern, name=name, grid=(n_steps,), in_specs=in_specs, out_specs=out_specs,
        out_shape=out_shape, scratch_shapes=scratch, compiler_params=_params(("arbitrary",)),
    )(*operands)


def _sds(rows, cols, dtype):
    return jax.ShapeDtypeStruct((rows, cols), dtype)


def _fwd_inproj(x, cc, sa, w, tm, gather=None):
    n_tok = x.shape[0]

    def body(i, tin, res, tout, acc, scr):
        x_ref, c_ref, s_ref = tin
        g_pre, w_in, g_q, w_uq, g_kv, w_k, w_v, e_mat, w_kt, e_t, w_vt, v_ones = res
        a_o, ps_o, u_o, gl_o, qn_o, kvn_o, q_o, k_o, v_o, kt_o, vt_o = tout
        a = _rms(x_ref[...], g_pre[...])[0].astype(MX)
        a_o[...] = a
        ps = _dot(a, w_in[:, :SMALL_COLS])
        ps_o[...] = ps.astype(ps_o.dtype)
        u_o[...] = _dot(a, w_in[:, SMALL_COLS:SMALL_COLS + POOL_WIDTH]).astype(u_o.dtype)
        gl_o[...] = _dot(a, w_in[:, SMALL_COLS + POOL_WIDTH:]).astype(gl_o.dtype)
        cc_, sa_ = c_ref[...], s_ref[...]
        qn = _rms(ps[:, :Q_LORA], g_q[...])[0].astype(MX)
        qn_o[...] = qn
        q = _dot(qn, w_uq[...])
        for h in range(N_HEADS):
            hs = slice(h * HEAD_SLOT, (h + 1) * HEAD_SLOT)
            q_o[:, hs] = (_rope(q[:, hs], cc_, sa_) * (SCALE * LOG2E)).astype(q_o.dtype)
        kvn = _rms(ps[:, Q_LORA:Q_LORA + KV_LORA], g_kv[...])[0].astype(MX)
        kvn_o[...] = kvn
        kr = _rope(ps[:, Q_LORA + KV_LORA:], cc_, sa_)
        k_o[...] = (_dot(kvn, w_k[...]) + _dot(kr, e_mat[...])).astype(k_o.dtype)
        v_o[...] = _dot(kvn, w_v[...]).astype(v_o.dtype)
        kt_o[...] = (_dot_nt(w_kt[...], kvn) + _dot_nt(e_t[...], kr)).astype(kt_o.dtype)
        vt_o[...] = (_dot_nt(w_vt[...], kvn) + v_ones[...]).astype(vt_o.dtype)

    outs = [_sds(n_tok, D_MODEL, MX), _sds(n_tok, SMALL_COLS, MX), _sds(n_tok, POOL_WIDTH, MX), _sds(n_tok, 2 * D_MODEL, MX),
            _sds(n_tok, Q_LORA, MX), _sds(n_tok, KV_LORA, MX), _sds(n_tok, QK_WIDTH, MX), _sds(n_tok, QK_WIDTH, MX),
            _sds(n_tok, N_HEADS * V_HEAD, MX),
            (_sds(QK_WIDTH, n_tok, MX), pl.BlockSpec((QK_WIDTH, tm), lambda i: (0, i))),
            (_sds(N_HEADS * V_ROWS, n_tok, MX), pl.BlockSpec((N_HEADS * V_ROWS, tm), lambda i: (0, i)))]
    res = [w["g_pre_mix"], w["w_in"], w["g_q"], w["w_uq"], w["g_kv"], w["w_k"], w["w_v"], w["e_mat"], w["w_kt"], w["e_t"],
           w["w_vt"], w["v_ones"]]
    exchange = None
    if gather is not None:
        gathered = jax.ShapeDtypeStruct((N_CHIPS,) + gather.shape, gather.dtype)
        exchange = (gather, gathered, [pltpu.SemaphoreType.DMA((6,)), pltpu.SemaphoreType.DMA((6,))], _gather_steps)
    return _tok_call("fwd_inproj", body, n_tok, tm, [x, cc, sa], res, outs, exchange=exchange)


def _causal_pairs(nq, ratio, by_kv):
    if by_kv:
        pairs = [(i, j) for j in range(nq * ratio) for i in range(j // ratio, nq)]
    else:
        pairs = [(i, j) for i in range(nq) for j in range((i + 1) * ratio)]
    return (jnp.asarray(np.array([p[0] for p in pairs], np.int32)), jnp.asarray(np.array([p[1] for p in pairs], np.int32)))


def _keep_t(tk, tq, off):
    return lax.broadcasted_iota(jnp.int32, (tk, tq), 0) + off <= lax.broadcasted_iota(jnp.int32, (tk, tq), 1)


ATTN_DIAG_KEYS = 512
ATTN_FWD_TILE = (2048, 1024)
ATTN_DIAG_KEYS_BWD = 256
ATTN_BWD_TILE = (1024, 512)
V_ROWS = 80


def _attn_fwd(q, k, vt, tq, tk):
    n_tok = q.shape[0]
    nq, ratio = n_tok // tq, tq // tk
    qi, kj = _causal_pairs(nq, ratio, by_kv=False)

    def kern(qi_ref, kj_ref, q_ref, k_ref, vt_ref, o_ref, lse_ref, m_s, acc_s, st_s):
        s_id = pl.program_id(0)
        i, j = qi_ref[s_id], kj_ref[s_id]

        @pl.when(j == 0)
        def _():
            m_s[...] = jnp.full(m_s.shape, NEG, F32)
            acc_s[...] = jnp.zeros(acc_s.shape, F32)

        def heads(ks, qs, masked):
            n_keys = ks.stop - ks.start
            keep = _keep_t(n_keys, qs.stop - qs.start, 0) if masked else None

            def scores(h):
                hs = slice(h * HEAD_SLOT, (h + 1) * HEAD_SLOT)
                return _dot_nt(k_ref[ks, hs], q_ref[qs, hs])

            st_s[0, :n_keys, qs] = scores(0)
            for h in range(N_HEADS):
                if h + 1 < N_HEADS:
                    st_s[(h + 1) % 2, :n_keys, qs] = scores(h + 1)
                st = st_s[h % 2, :n_keys, qs]
                if masked:
                    st = jnp.where(keep, st, NEG)
                m_old = m_s[h, :, qs]
                m_new = jnp.maximum(m_old, jnp.max(st, axis=0, keepdims=True))
                pt = jnp.exp2(st - m_new)
                acc_s[h, :, qs] = (jnp.exp2(m_old - m_new) * acc_s[h, :, qs]
                                   + _dot(vt_ref[h * V_ROWS:(h + 1) * V_ROWS, ks], pt))
                m_s[h, :, qs] = m_new

        @pl.when(j < i * ratio)
        def _():
            heads(slice(0, tk), slice(0, tq), False)

        sub = min(tk, ATTN_DIAG_KEYS)
        for part in range(ratio):
            @pl.when(j == i * ratio + part)
            def _():
                for first in range(0, tk, sub):
                    heads(slice(first, first + sub), slice(part * tk + first, tq), True)

        @pl.when(j == (i + 1) * ratio - 1)
        def _():
            heads_out = []
            for h in range(N_HEADS):
                total = acc_s[h, V_HEAD:V_HEAD + 1, :]
                heads_out.append(acc_s[h, :V_HEAD, :] / total)
                lse_ref[h:h + 1, :] = m_s[h] + jnp.log2(total)
            o_ref[...] = jnp.concatenate(heads_out, 0).T.astype(o_ref.dtype)

    gs = pltpu.PrefetchScalarGridSpec(
        num_scalar_prefetch=2, grid=(qi.shape[0],),
        in_specs=[pl.BlockSpec((tq, QK_WIDTH), lambda s, qi, kj: (qi[s], 0)),
                  pl.BlockSpec((tk, QK_WIDTH), lambda s, qi, kj: (kj[s], 0)),
                  pl.BlockSpec((N_HEADS * V_ROWS, tk), lambda s, qi, kj: (0, kj[s]))],
        out_specs=[pl.BlockSpec((tq, N_HEADS * V_HEAD), lambda s, qi, kj: (qi[s], 0)),
                   pl.BlockSpec((N_HEADS, tq), lambda s, qi, kj: (0, qi[s]))],
        scratch_shapes=[pltpu.VMEM((N_HEADS, 1, tq), F32), pltpu.VMEM((N_HEADS, V_ROWS, tq), F32),
                        pltpu.VMEM((2, tk, tq), F32)])
    return pl.pallas_call(kern, name="attn_fwd", grid_spec=gs,
                          out_shape=[_sds(n_tok, N_HEADS * V_HEAD, MX), _sds(N_HEADS, n_tok, F32)],
                          compiler_params=_params(("arbitrary",)))(qi, kj, q, k, vt)


def _pool_windows(ext, i, tm, first_row):
    row = i * tm + lax.broadcasted_iota(jnp.int32, (tm, 1), 0)
    out = []
    for g, w in enumerate(POOL_WINDOWS):
        cs = slice(g * POOL_GROUP, (g + 1) * POOL_GROUP)
        s = ext[pl.ds(first_row, tm), cs]
        for k in range(1, w):
            s = s + ext[pl.ds(first_row - k, tm), cs]
        cnt = jnp.minimum(row + 1, w).astype(F32)
        out.append(s / cnt)
    return out


def _fwd_mix(x, u, gl, attn, w, tm):
    n_tok = x.shape[0]
    halo_spec = pl.BlockSpec((HALO, POOL_WIDTH), lambda i: (jnp.maximum(i * (tm // HALO) - 1, 0), 0))

    def body(i, tin, res, tout, acc, scr):
        x_ref, u_ref, uh_ref, gl_ref, at_ref = tin
        w_pool, pool_scale, w_ba, w_bp, b_gate, w_out, g_post = res
        d_o, pooled_o, a_o, pp_o, merged_o, y_o, h1_o = tout
        ext, = scr
        ext[pl.ds(0, HALO), :] = jnp.where(i > 0, uh_ref[...].astype(F32), 0.0)
        ext[pl.ds(HALO, tm), :] = u_ref[...].astype(F32)
        means = _pool_windows(ext, i, tm, HALO)
        for g in range(len(POOL_WINDOWS)):
            cs = slice(g * POOL_GROUP, (g + 1) * POOL_GROUP)
            d = (means[g] - ext[pl.ds(HALO, tm), cs]).astype(MX)
            d_o[:, cs] = d
            pooled_o[:, cs] = (_dot(d, w_pool[g]) * pool_scale[:, cs]).astype(pooled_o.dtype)
        a_br = _dot(at_ref[...], w_ba[...])
        p_br = _dot(pooled_o[...], w_bp[...])
        a_o[...] = a_br.astype(a_o.dtype)
        pp_o[...] = p_br.astype(pp_o.dtype)
        gates = _sig(gl_ref[...].astype(F32) + b_gate[...])
        merged = (gates[:, :D_MODEL] * a_br + gates[:, D_MODEL:] * p_br).astype(MX)
        merged_o[...] = merged
        y = _dot(merged, w_out[...])
        y_o[...] = y.astype(y_o.dtype)
        h1_o[...] = x_ref[...] + _rms(y, g_post[...])[0]

    outs = [_sds(n_tok, POOL_WIDTH, MX), _sds(n_tok, POOL_WIDTH, MX), _sds(n_tok, D_MODEL, MX), _sds(n_tok, D_MODEL, MX),
            _sds(n_tok, D_MODEL, MX), _sds(n_tok, D_MODEL, MX), _sds(n_tok, D_MODEL, F32)]
    res = [w["w_pool"], w["pool_scale"], w["w_branch_attn"], w["w_branch_pool"], w["b_gate"], w["w_out"], w["g_post_mix"]]
    return _tok_call("fwd_mix", body, n_tok, tm, [x, u, (u, halo_spec), gl, attn], res, outs,
                     scratch=[pltpu.VMEM((tm + HALO, POOL_WIDTH), F32)])


def _fwd_mlp(h1, w, tm):
    n_tok = h1.shape[0]

    def body(i, tin, res, tout, acc, scr):
        h1_ref, = tin
        g_pre, w1, w2, g_post = res
        m_o, zr_o, f_o, h2_o = tout
        h1_ = h1_ref[...]
        m = _rms(h1_, g_pre[...])[0].astype(MX)
        m_o[...] = m
        zr = jnp.maximum(_dot(m, w1[...]), 0.0)
        zr_o[...] = zr.astype(zr_o.dtype)
        a2 = (zr * zr).astype(MX)
        f = _dot(a2, w2[...])
        f_o[...] = f.astype(f_o.dtype)
        h2_o[...] = h1_ + _rms(f, g_post[...])[0]

    outs = [_sds(n_tok, D_MODEL, MX), _sds(n_tok, D_FF, MX), _sds(n_tok, D_MODEL, MX),
            _sds(n_tok, D_MODEL, F32)]
    res = [w["g_pre_mlp"], w["w_ff1"], w["w_ff2"], w["g_post_mlp"]]
    return _tok_call("fwd_mlp", body, n_tok, tm, [h1], res, outs)


def _ple_fwd_bwd(h2, p, target, w, tm):
    n_tok = h2.shape[0]

    def body(i, tin, res, tout, acc, scr):
        h2_ref, p_ref, t_ref = tin
        w_pe, w_pg, g_ple = res
        dh2_o, de_o, dzg_o = tout
        loss_a, dg_a = acc
        h2_ = h2_ref[...]
        e = _dot(p_ref[...], w_pe[...])
        pg = _sig(_dot(h2_, w_pg[...]))
        t = pg * e
        g = g_ple[...]
        tn, th, r = _rms(t, g)
        diff = h2_ + tn - t_ref[...]
        loss_a[...] += jnp.sum(diff * diff, axis=0, keepdims=True)
        dh3 = diff * (1.0 / D_MODEL)
        dt, dg = _rms_bwd(th, r, g, dh3)
        dg_a[...] += dg
        de_o[...] = (dt * pg).astype(de_o.dtype)
        dzg = (dt * e * pg * (1.0 - pg)).astype(MX)
        dzg_o[...] = dzg
        dh2_o[...] = dh3 + _dot_nt(dzg, w_pg[...])

    outs = [_sds(n_tok, D_MODEL, F32), _sds(n_tok, D_MODEL, MX), _sds(n_tok, D_MODEL, MX)]
    accs = [_sds(1, D_MODEL, F32), _sds(1, D_MODEL, F32)]
    return _tok_call("ple_fwd_bwd", body, n_tok, tm, [h2, p, target], [w["w_ple_proj"], w["w_ple_gate"], w["g_ple"]], outs, accs)


def _bwd_mlp(dh2, f, h1, zr, w, tm):
    n_tok = dh2.shape[0]

    def body(i, tin, res, tout, acc, scr):
        dh2_ref, f_ref, h1_ref, zr_ref = tin
        g_pre, w1, w2, g_post = res
        df_o, dz_o, dh1_o = tout
        dg_post_a, dg_pre_a = acc
        dh2_ = dh2_ref[...]
        gp = g_post[...]
        _, fh, rf = _rms(f_ref[...].astype(F32), gp)
        df, dg = _rms_bwd(fh, rf, gp, dh2_)
        dg_post_a[...] += dg
        df = df.astype(MX)
        df_o[...] = df
        dz = (_dot_nt(df, w2[...]) * (2.0 * zr_ref[...].astype(F32))).astype(MX)
        dz_o[...] = dz
        dm = _dot_nt(dz, w1[...])
        gq = g_pre[...]
        _, hh, rh = _rms(h1_ref[...], gq)
        dh1, dg = _rms_bwd(hh, rh, gq, dm)
        dg_pre_a[...] += dg
        dh1_o[...] = dh2_ + dh1

    outs = [_sds(n_tok, D_MODEL, MX), _sds(n_tok, D_FF, MX), _sds(n_tok, D_MODEL, F32)]
    accs = [_sds(1, D_MODEL, F32), _sds(1, D_MODEL, F32)]
    res = [w["g_pre_mlp"], w["w_ff1"], w["w_ff2"], w["g_post_mlp"]]
    return _tok_call("bwd_mlp", body, n_tok, tm, [dh2, f, h1, zr], res, outs, accs)


def _bwd_mix(dh1, y, a_br, p_br, gl, attn, d, w, tm):
    n_tok = dh1.shape[0]

    def body(i, tin, res, tout, acc, scr):
        dh1_ref, y_ref, a_ref, pp_ref, gl_ref, at_ref, d_ref = tin
        g_post, w_out, b_gate, w_ba, w_bp, w_pool, pool_scale, sel = res
        dy_o, da_o, dpp_o, dgl_o, do_o, delta_o, dyp_o, dd_o = tout
        dg_post_a, db_a, dps_a = acc
        g = g_post[...]
        _, yh, r = _rms(y_ref[...].astype(F32), g)
        dy, dg = _rms_bwd(yh, r, g, dh1_ref[...])
        dg_post_a[...] += dg
        dy = dy.astype(MX)
        dy_o[...] = dy
        dmerged = _dot_nt(dy, w_out[...])
        gates = _sig(gl_ref[...].astype(F32) + b_gate[...])
        ga, gp = gates[:, :D_MODEL], gates[:, D_MODEL:]
        da = (dmerged * ga).astype(MX)
        dpp = (dmerged * gp).astype(MX)
        da_o[...] = da
        dpp_o[...] = dpp
        dgl_a = dmerged * a_ref[...].astype(F32) * ga * (1.0 - ga)
        dgl_p = dmerged * pp_ref[...].astype(F32) * gp * (1.0 - gp)
        dgl_o[:, :D_MODEL] = dgl_a.astype(dgl_o.dtype)
        dgl_o[:, D_MODEL:] = dgl_p.astype(dgl_o.dtype)
        db_a[:, :D_MODEL] += jnp.sum(dgl_a, axis=0, keepdims=True)
        db_a[:, D_MODEL:] += jnp.sum(dgl_p, axis=0, keepdims=True)
        do = _dot_nt(da, w_ba[...]).astype(MX)
        do_o[...] = do
        prod = do.astype(F32) * at_ref[...].astype(F32)
        hi = prod.astype(MX)
        lo = (prod - hi.astype(F32)).astype(MX)
        delta_o[...] = _dot(hi, sel[...]) + _dot(lo, sel[...])
        dpooled = _dot_nt(dpp, w_bp[...])
        for gi in range(len(POOL_WINDOWS)):
            cs = slice(gi * POOL_GROUP, (gi + 1) * POOL_GROUP)
            ypre = _dot(d_ref[:, cs], w_pool[gi])
            dps_a[:, cs] += jnp.sum(dpooled[:, cs] * ypre, axis=0, keepdims=True)
            dyp = (dpooled[:, cs] * pool_scale[:, cs]).astype(MX)
            dyp_o[:, cs] = dyp
            dd_o[:, cs] = _dot_nt(dyp, w_pool[gi])

    outs = [_sds(n_tok, D_MODEL, MX), _sds(n_tok, D_MODEL, MX), _sds(n_tok, D_MODEL, MX), _sds(n_tok, 2 * D_MODEL, MX),
            _sds(n_tok, N_HEADS * V_HEAD, MX), _sds(n_tok, HEAD_SLOT, F32), _sds(n_tok, POOL_WIDTH, MX),
            _sds(n_tok, POOL_WIDTH, F32)]
    accs = [_sds(1, D_MODEL, F32), _sds(1, 2 * D_MODEL, F32), _sds(1, POOL_WIDTH, F32)]
    res = [w["g_post_mix"], w["w_out"], w["b_gate"], w["w_branch_attn"], w["w_branch_pool"], w["w_pool"], w["pool_scale"],
           w["head_sel"]]
    return _tok_call("bwd_mix", body, n_tok, tm, [dh1, y, a_br, p_br, gl, attn, d], res, outs, accs)


def _bwd_heads(q_ref, k_ref, v_ref, do_ref, lse_ref, dl_ref, st_s, dpt_s, keep, use, n_heads, qs, ks):
    def products(h):
        hs = slice(h * HEAD_SLOT, (h + 1) * HEAD_SLOT)
        vs = slice(h * V_HEAD, (h + 1) * V_HEAD)
        st_s[h % 2, ks, qs] = _dot_nt(k_ref[ks, hs], q_ref[qs, hs])
        dpt_s[h % 2, ks, qs] = _dot_nt(v_ref[ks, vs], do_ref[qs, vs])

    products(0)
    for h in range(n_heads):
        if h + 1 < n_heads:
            products(h + 1)
        st = st_s[h % 2, ks, qs]
        if keep is not None:
            st = jnp.where(keep, st, NEG)
        pt = jnp.exp2(st - lse_ref[h:h + 1, qs])
        use(h, pt, pt * (dpt_s[h % 2, ks, qs] - dl_ref[h:h + 1, qs]))


HEAD_GROUP = 4


def _attn_bwd(q, k, kt, v, do, lse, delta, tq, tk, scatter=None):
    n_tok = q.shape[0]
    nq, ratio = n_tok // tq, tq // tk
    n_groups = N_HEADS // HEAD_GROUP
    gq, gv = HEAD_GROUP * HEAD_SLOT, HEAD_GROUP * V_HEAD
    qi, kj = _causal_pairs(nq, ratio, by_kv=True)

    n_pairs = qi.shape[0]

    def kern(qi_ref, kj_ref, q_ref, k_ref, kt_ref, v_ref, do_ref, lse_ref, dl_ref, *rest):
        if scatter is not None:
            s_hbm, dq_ref, dk_ref, dv_ref, r_hbm, dk_s, dv_s, st_s, dpt_s, send_sems, recv_sems = rest
        else:
            dq_ref, dk_ref, dv_ref, dk_s, dv_s, st_s, dpt_s = rest
        s_id = pl.program_id(1)
        i, j = qi_ref[s_id], kj_ref[s_id]
        if scatter is not None:
            group = pl.program_id(0)
            _scatter_steps(jnp.logical_and(group == 0, s_id == 0),
                           jnp.logical_and(group == n_groups - 1, s_id == n_pairs - 1), s_hbm, r_hbm, (send_sems, recv_sems))

        @pl.when(s_id == 0)
        def _():
            dq_ref[...] = jnp.zeros(dq_ref.shape, F32)

        def heads(ks, first_query, masked):
            qs = slice(first_query, tq)
            cols = pl.ds(pl.multiple_of(i * tq + first_query, ATTN_DIAG_KEYS_BWD), tq - first_query)

            def use(h, pt, dst):
                hs = slice(h * HEAD_SLOT, (h + 1) * HEAD_SLOT)
                dv_s[h, ks, :] += _dot(pt, do_ref[qs, h * V_HEAD:(h + 1) * V_HEAD])
                dk_s[ks, hs] += _dot(dst, q_ref[qs, hs])
                dq_ref[hs, cols] += _dot(kt_ref[hs, ks], dst)

            keep = _keep_t(ks.stop - ks.start, tq - first_query, 0) if masked else None
            _bwd_heads(q_ref, k_ref, v_ref, do_ref, lse_ref.at[0], dl_ref.at[0], st_s, dpt_s, keep, use, HEAD_GROUP, qs, ks)

        @pl.when(j >= i * ratio)
        def _():
            dk_s[...] = jnp.zeros(dk_s.shape, F32)
            dv_s[...] = jnp.zeros(dv_s.shape, F32)

        sub = min(tk, ATTN_DIAG_KEYS_BWD)
        for part in range(ratio):
            @pl.when(j == i * ratio + part)
            def _():
                for first in range(0, tk, sub):
                    heads(slice(first, first + sub), part * tk + first, True)

        @pl.when(j < i * ratio)
        def _():
            heads(slice(0, tk), 0, False)

        @pl.when(i == nq - 1)
        def _():
            dk_ref[...] = (dk_s[...] * (1.0 / LOG2E)).astype(dk_ref.dtype)
            for h in range(HEAD_GROUP):
                dv_ref[:, h * V_HEAD:(h + 1) * V_HEAD] = dv_s[h].astype(dv_ref.dtype)

    at_q = lambda g, s, qi, kj: (qi[s], g)
    at_k = lambda g, s, qi, kj: (kj[s], g)
    at_kt = lambda g, s, qi, kj: (g, kj[s])
    at_stat = lambda g, s, qi, kj: (g, 0, qi[s])
    in_specs = [pl.BlockSpec((tq, gq), at_q), pl.BlockSpec((tk, gq), at_k), pl.BlockSpec((gq, tk), at_kt),
                pl.BlockSpec((tk, gv), at_k), pl.BlockSpec((tq, gv), at_q),
                pl.BlockSpec((1, HEAD_GROUP, tq), at_stat), pl.BlockSpec((1, HEAD_GROUP, tq), at_stat)]
    out_specs = [pl.BlockSpec((gq, n_tok), lambda g, s, qi, kj: (g, 0), pipeline_mode=pl.Buffered(1)),
                 pl.BlockSpec((tk, gq), at_k), pl.BlockSpec((tk, gv), at_k)]
    out_shape = [_sds(QK_WIDTH, n_tok, F32), _sds(n_tok, QK_WIDTH, MX), _sds(n_tok, N_HEADS * V_HEAD, MX)]
    scratch = [pltpu.VMEM((tk, gq), F32), pltpu.VMEM((HEAD_GROUP, tk, V_HEAD), F32),
               pltpu.VMEM((2, tk, tq), F32), pltpu.VMEM((2, tk, tq), F32)]
    stat3 = lambda a: a.reshape(n_groups, HEAD_GROUP, n_tok)
    operands = [qi, kj, q, k, kt, v, do, stat3(lse), stat3(delta)]
    if scatter is not None:
        operands.append(scatter)
        in_specs.append(_HBM)
        out_specs.append(_HBM)
        out_shape.append(jax.ShapeDtypeStruct((N_DEVICES, scatter.shape[1] // 2, PACK_COLS), scatter.dtype))
        scratch += [pltpu.SemaphoreType.DMA((N_DEVICES - 1,)), pltpu.SemaphoreType.DMA((N_DEVICES - 1,))]
    gs = pltpu.PrefetchScalarGridSpec(num_scalar_prefetch=2, grid=(n_groups, n_pairs), in_specs=in_specs,
                                      out_specs=out_specs, scratch_shapes=scratch)
    return pl.pallas_call(kern, name="attn_bwd", grid_spec=gs, out_shape=out_shape,
                          compiler_params=_params(("arbitrary", "arbitrary")))(*operands)


def _bwd_inproj(dq_t, dk, dv, dd, dgl, ps, x, dh1, cc, sa, w, tm):
    n_tok = x.shape[0]
    n_tiles = n_tok // tm
    last_halo = n_tok // HALO - 1
    halo_spec = pl.BlockSpec((HALO, POOL_WIDTH), lambda i: (jnp.minimum((i + 1) * (tm // HALO), last_halo), 0))

    def body(i, tin, res, tout, acc, scr):
        dq_ref, dk_ref, dv_ref, dd_ref, ddh_ref, dgl_ref, ps_ref, x_ref, dh1_ref, c_ref, s_ref = tin
        w_uq, g_q, w_k, w_v, e_mat, g_kv, w_in, g_pre = res
        dqu_o, dproj_o, dx_o = tout
        dgq_a, dgkv_a, dgpre_a = acc
        ext, = scr
        cc_, sa_ = c_ref[...], s_ref[...]
        for h in range(N_HEADS):
            hs = slice(h * HEAD_SLOT, (h + 1) * HEAD_SLOT)
            dqu_o[:, hs] = (_unrope(dq_ref[hs, :].T, cc_, sa_) * SCALE).astype(dqu_o.dtype)
        gq = g_q[...]
        _, qh, rq = _rms(ps_ref[:, :Q_LORA].astype(F32), gq)
        dqd, dg = _rms_bwd(qh, rq, gq, _dot_nt(dqu_o[...], w_uq[...]))
        dgq_a[...] += dg
        dproj_o[:, :Q_LORA] = dqd.astype(dproj_o.dtype)
        gkv = g_kv[...]
        _, kh, rk = _rms(ps_ref[:, Q_LORA:Q_LORA + KV_LORA].astype(F32), gkv)
        dkvd, dg = _rms_bwd(kh, rk, gkv, _dot_nt(dk_ref[...], w_k[...]) + _dot_nt(dv_ref[...], w_v[...]))
        dgkv_a[...] += dg
        dproj_o[:, Q_LORA:Q_LORA + KV_LORA] = dkvd.astype(dproj_o.dtype)
        dproj_o[:, Q_LORA + KV_LORA:SMALL_COLS] = _unrope(_dot_nt(dk_ref[...], e_mat[...]), cc_, sa_).astype(dproj_o.dtype)
        row = i * tm + lax.broadcasted_iota(jnp.int32, (tm + HALO, 1), 0)
        for gi, wdw in enumerate(POOL_WINDOWS):
            cs = slice(gi * POOL_GROUP, (gi + 1) * POOL_GROUP)
            inv = 1.0 / jnp.minimum(row + 1, wdw).astype(F32)
            ext[pl.ds(0, tm), cs] = dd_ref[:, cs] * inv[:tm]
            ext[pl.ds(tm, HALO), cs] = jnp.where(i < n_tiles - 1, ddh_ref[:, cs] * inv[tm:], 0.0)
            s = ext[pl.ds(0, tm), cs]
            for k_ in range(1, wdw):
                s = s + ext[pl.ds(k_, tm), cs]
            dproj_o[:, SMALL_COLS + gi * POOL_GROUP:SMALL_COLS + (gi + 1) * POOL_GROUP] = (s - dd_ref[:, cs]).astype(dproj_o.dtype)
        dproj_o[:, SMALL_COLS + POOL_WIDTH:] = dgl_ref[...]
        da = _dot_nt(dproj_o[...], w_in[...])
        gp = g_pre[...]
        _, xh, rx = _rms(x_ref[...], gp)
        dx, dg = _rms_bwd(xh, rx, gp, da)
        dgpre_a[...] += dg
        dx_o[...] = dh1_ref[...] + dx

    outs = [_sds(n_tok, QK_WIDTH, MX), _sds(n_tok, IN_PAD, MX), _sds(n_tok, D_MODEL, F32)]
    accs = [_sds(1, Q_LORA, F32), _sds(1, KV_LORA, F32), _sds(1, D_MODEL, F32)]
    res = [w["w_uq"], w["g_q"], w["w_k"], w["w_v"], w["e_mat"], w["g_kv"], w["w_in"], w["g_pre_mix"]]
    dq_spec = pl.BlockSpec((QK_WIDTH, tm), lambda i: (0, i))
    return _tok_call("bwd_inproj", body, n_tok, tm, [(dq_t, dq_spec), dk, dv, dd, (dd, halo_spec), dgl, ps, x, dh1, cc, sa], res, outs, accs,
                     scratch=[pltpu.VMEM((tm + HALO, POOL_WIDTH), F32)])


XTDY_TOKENS = 1024
XTDY_OUT_BYTES = 8 * 2**20
XTDY_IN_BYTES = 8 * 2**20


def _xtdy(name, x, dy, allreduce=None, square_x=False):
    n_tok, kk = x.shape
    nn = dy.shape[1]
    bk = kk
    while bk * nn * 4 > XTDY_OUT_BYTES and bk % 256 == 0:
        bk //= 2
    bt = min(XTDY_TOKENS, n_tok)
    while (2 * bt <= n_tok and n_tok % (2 * bt) == 0 and 2 * bt * nn * dy.dtype.itemsize <= XTDY_IN_BYTES
           and 2 * bt * bk * x.dtype.itemsize <= XTDY_IN_BYTES):
        bt *= 2

    grid = (kk // bk, n_tok // bt)

    def kern(x_ref, dy_ref, *rest):
        o_ref = rest[1] if allreduce is not None else rest[0]
        if allreduce is not None:
            g_ref, _, sum_ref, buf, send_sems, recv_sems = rest
            step = pl.program_id(0) * grid[1] + pl.program_id(1)
            _allreduce_steps(step == 0, step == grid[0] * grid[1] - 1, g_ref, sum_ref, buf, send_sems, recv_sems)

        @pl.when(pl.program_id(1) == 0)
        def _():
            o_ref[...] = jnp.zeros(o_ref.shape, F32)
        xv = x_ref[...]
        if square_x:
            xv = xv.astype(F32)
            xv = xv * xv
        o_ref[...] += _dot_tn(xv, dy_ref[...])

    operands = [x, dy]
    in_specs = [pl.BlockSpec((bt, bk), lambda a, t: (t, a)), pl.BlockSpec((bt, nn), lambda a, t: (t, 0))]
    out_specs = [pl.BlockSpec((bk, nn), lambda a, t: (a, 0))]
    out_shape = [_sds(kk, nn, F32)]
    scratch = []
    if allreduce is not None:
        vmem = pl.BlockSpec(memory_space=pltpu.VMEM)
        operands.append(allreduce)
        in_specs.append(vmem)
        out_specs.append(vmem)
        out_shape.append(jax.ShapeDtypeStruct(allreduce.shape, allreduce.dtype))
        scratch = [pltpu.VMEM((N_DEVICES,) + allreduce.shape, allreduce.dtype), pltpu.SemaphoreType.DMA((N_DEVICES - 1,)),
                   pltpu.SemaphoreType.DMA((N_DEVICES - 1,))]
    res = pl.pallas_call(kern, name=name, grid=grid, in_specs=in_specs, out_specs=out_specs, out_shape=out_shape,
                         scratch_shapes=scratch, compiler_params=_params(("arbitrary", "arbitrary")))(*operands)
    return res if allreduce is not None else res[0]


def _rope_tables(positions):
    inv_freq = ROPE_THETA ** (-jnp.arange(0, QK_ROPE, 2, dtype=F32) / QK_ROPE)
    ang_t = inv_freq[:, None] * positions.astype(F32)[None, :]
    cos_t, sin_t = lax.optimization_barrier((jnp.cos(ang_t), jnp.sin(ang_t)))
    cos, sin = cos_t.T, sin_t.T
    n_tok = positions.shape[0]
    ones, z64 = jnp.ones((n_tok, ROPE_LANE), F32), jnp.zeros((n_tok, ROPE_LANE), F32)
    z32 = jnp.zeros((n_tok, HEAD_SLOT - ROPE_LANE - QK_ROPE), F32)
    return jnp.concatenate([ones, cos, cos, z32], 1), jnp.concatenate([z64, -sin, sin, z32], 1)


def _kernel_weights(full):
    w_in, w_uq, w_ukv = full["w_in"], full["w_uq"], full["w_ukv"]
    c0 = Q_LORA + KV_LORA
    z = lambda n: jnp.zeros((D_MODEL, n), w_in.dtype)
    w = dict(full)
    w["w_in"] = jnp.concatenate([w_in[:, :c0], z(ROPE_LANE), w_in[:, c0:c0 + QK_ROPE], z(HEAD_SLOT - ROPE_LANE - QK_ROPE),
                                 w_in[:, c0 + QK_ROPE:]], 1)
    w["w_uq"] = jnp.pad(w_uq.reshape(Q_LORA, N_HEADS, QK_NOPE + QK_ROPE),
                        ((0, 0), (0, 0), (0, HEAD_SLOT - QK_NOPE - QK_ROPE))).reshape(Q_LORA, QK_WIDTH)
    kv = w_ukv.reshape(KV_LORA, N_HEADS, QK_NOPE + V_HEAD)
    w["w_k"] = jnp.pad(kv[:, :, :QK_NOPE], ((0, 0), (0, 0), (0, HEAD_SLOT - QK_NOPE))).reshape(KV_LORA, QK_WIDTH)
    w["w_v"] = kv[:, :, QK_NOPE:].reshape(KV_LORA, N_HEADS * V_HEAD)
    e = np.zeros((HEAD_SLOT, QK_WIDTH), np.float32)
    sel = np.zeros((N_HEADS * V_HEAD, HEAD_SLOT), np.float32)
    for h in range(N_HEADS):
        for r in range(QK_ROPE):
            e[ROPE_LANE + r, h * HEAD_SLOT + ROPE_LANE + r] = 1.0
        sel[h * V_HEAD:(h + 1) * V_HEAD, h] = 1.0
    w["e_mat"] = jnp.asarray(e, MX)
    w["w_kt"], w["e_t"] = w["w_k"].T, jnp.asarray(e.T, MX)
    pad = ((0, 0), (0, V_ROWS - V_HEAD), (0, 0))
    w["w_vt"] = jnp.pad(w["w_v"].T.reshape(N_HEADS, V_HEAD, KV_LORA), pad).reshape(N_HEADS * V_ROWS, KV_LORA)
    ones = np.zeros((N_HEADS, V_ROWS, 1), np.float32)
    ones[:, V_HEAD] = 1.0
    w["v_ones"] = jnp.asarray(ones.reshape(N_HEADS * V_ROWS, 1))
    w["head_sel"] = jnp.asarray(sel, MX)
    w["w_pool"] = full["w_pool"].astype(MX)
    return w


def _local_step(x, p, positions, target, full, mesh_place=None, packed_rest=None):
    n_tok = x.shape[0]
    tm = tm_mlp = min(TOKEN_TILE, n_tok)
    fwd_tile = [min(t, n_tok) for t in ATTN_FWD_TILE]
    bwd_tile = [min(t, n_tok) for t in ATTN_BWD_TILE]
    w = _kernel_weights(full)
    cc, sa = _rope_tables(positions)

    if mesh_place is None:
        a, ps, u, gl, qn, kvn, q, k, v, kt, vt = _fwd_inproj(x, cc, sa, w, tm)
    else:
        my_chip, core = mesh_place
        a, ps, u, gl, qn, kvn, q, k, v, kt, vt, gathered = _fwd_inproj(x, cc, sa, w, tm, gather=packed_rest)
        w.update(_unpack_full(gathered, packed_rest, my_chip, REST))
    attn, lse = _attn_fwd(q, k, vt, *fwd_tile)
    d, pooled, a_br, p_br, merged, y, h1 = _fwd_mix(x, u, gl, attn, w, tm)
    m, zr, f, h2 = _fwd_mlp(h1, w, tm_mlp)
    dh2, de, dzg, loss_cols, dg_ple = _ple_fwd_bwd(h2, p, target, w, tm)
    df, dz, dh1, dg_post_mlp, dg_pre_mlp = _bwd_mlp(dh2, f, h1, zr, w, tm_mlp)
    dy, da_br, dp_br, dgl, do, delta, dyp, dd, dg_post_mix, db_gate, dpool_scale = _bwd_mix(dh1, y, a_br, p_br, gl, attn, d, w, tm)
    grads = {"w_branch_attn": _xtdy("dw_ba", attn, da_br), "w_branch_pool": _xtdy("dw_bp", pooled, dp_br),
             "w_out": _xtdy("dw_out", merged, dy), "w_ff1": _xtdy("dw_ff1", m, dz), "w_ff2": _xtdy("dw_ff2", zr, df, square_x=True),
             "w_ple_proj": _xtdy("dw_pe", p, de), "w_ple_gate": _xtdy("dw_pg", h2, dzg)}
    delta_t = delta[:, :N_HEADS].T
    if mesh_place is None:
        travelling = None
        dq_t, dk, dv = _attn_bwd(q, k, kt, v, do, lse, delta_t, *bwd_tile)
    else:
        pieces = _pack_pieces(grads, REST, WIRE)
        dq_t, dk, dv, received = _attn_bwd(q, k, kt, v, do, lse, delta_t, *bwd_tile, scatter=pieces)
        travelling = (pieces, received)
        grads = {}
    dqu, dproj, dx, dg_q, dg_kv, dg_pre_mix = _bwd_inproj(dq_t, dk, dv, dd, dgl, ps, x, dh1, cc, sa, w, tm)

    g_uq = _xtdy("dw_uq", qn, dqu)
    g_k = _xtdy("dw_k", kvn, dk)
    g_v = _xtdy("dw_v", kvn, dv)
    g_pool = _xtdy("dw_pool", d, dyp)
    small = {"g_pre_mix": dg_pre_mix, "b_gate": db_gate, "g_q": dg_q, "g_kv": dg_kv, "pool_scale": dpool_scale,
             "g_post_mix": dg_post_mix, "g_pre_mlp": dg_pre_mlp, "g_post_mlp": dg_post_mlp, "g_ple": dg_ple,
             "w_pool": jnp.stack([g_pool[g * POOL_GROUP:(g + 1) * POOL_GROUP, g * POOL_GROUP:(g + 1) * POOL_GROUP]
                                  for g in range(len(POOL_WINDOWS))])}
    if mesh_place is None:
        g_in = _xtdy("dw_in", a, dproj)
    else:
        g_in, small_sum = _xtdy("dw_in", a, dproj, allreduce=_pack_small(small, loss_cols))
        small, loss_cols = _unpack_small(small_sum), small_sum[-1:]

    c0 = Q_LORA + KV_LORA
    grads.update(small)
    grads.update({
        "w_in": jnp.concatenate([g_in[:, :c0], g_in[:, c0 + ROPE_LANE:c0 + ROPE_LANE + QK_ROPE], g_in[:, SMALL_COLS:]], 1),
        "w_uq": g_uq.reshape(Q_LORA, N_HEADS, HEAD_SLOT)[:, :, :QK_NOPE + QK_ROPE].reshape(Q_LORA, N_HEADS * (QK_NOPE + QK_ROPE)),
        "w_ukv": jnp.concatenate([g_k.reshape(KV_LORA, N_HEADS, HEAD_SLOT)[:, :, :QK_NOPE],
                                  g_v.reshape(KV_LORA, N_HEADS, V_HEAD)], 2).reshape(KV_LORA, N_HEADS * (QK_NOPE + V_HEAD)),
    })
    return loss_cols, dx, grads, travelling


def _place():
    return lax.axis_index("x"), lax.axis_index("y"), lax.axis_index("c")


CHIP_FLIPS = ((1, 0), (0, 1), (1, 1))


def _flip(x, y, fx, fy):
    return (1 - x if fx else x), (1 - y if fy else y)


_HBM = pl.BlockSpec(memory_space=pl.ANY)


def _gather_copies(w_ref, out_ref, send_sems, recv_sems):
    half = w_ref.shape[0] // 2
    x, y, c = _place()
    my_chip = 2 * x + y
    sibling = (x, y, 1 - c)

    def half_of(chip, hc):
        return out_ref.at[chip, pl.ds(pl.multiple_of(hc * half, 16), half), :]

    src = w_ref.at[pl.ds(pl.multiple_of(c * half, 16), half), :]
    sends, landed, forwards, from_sibling = [], [], [], []
    for j, (fx, fy) in enumerate(CHIP_FLIPS):
        px, py = _flip(x, y, fx, fy)
        mine_there, theirs_here, theirs_other = half_of(my_chip, c), half_of(2 * px + py, c), half_of(2 * px + py, 1 - c)
        sends.append(pltpu.make_async_remote_copy(src, mine_there, send_sems.at[j], recv_sems.at[j],
                                                  device_id=(px, py, c), device_id_type=MESH))
        landed.append(pltpu.make_async_remote_copy(src, theirs_here, send_sems.at[j], recv_sems.at[j],
                                                   device_id=(px, py, c), device_id_type=MESH))
        forwards.append(pltpu.make_async_remote_copy(theirs_here, theirs_here, send_sems.at[3 + j], recv_sems.at[3 + j],
                                                     device_id=sibling, device_id_type=MESH))
        from_sibling.append(pltpu.make_async_remote_copy(theirs_other, theirs_other, send_sems.at[3 + j],
                                                         recv_sems.at[3 + j], device_id=sibling, device_id_type=MESH))
    return sends, landed, forwards, from_sibling


def _gather_steps(i, n_steps, w_ref, out_ref, sems):
    sends, landed, forwards, from_sibling = _gather_copies(w_ref, out_ref, *sems)

    @pl.when(i == 0)
    def _():
        for cp in sends:
            cp.start()

    @pl.when(i == (3 * n_steps) // 4)
    def _():
        for arrived, fwd in zip(landed, forwards):
            arrived.wait_recv()
            fwd.start()

    @pl.when(i == n_steps - 1)
    def _():
        for cp in from_sibling:
            cp.wait_recv()
        for cp in sends + forwards:
            cp.wait_send()


def _allgather_shards(wp):
    def body(w_ref, out_ref, send_sems, recv_sems):
        sends, landed, forwards, from_sibling = _gather_copies(w_ref, out_ref, send_sems, recv_sems)
        for cp in sends:
            cp.start()
        for arrived, fwd in zip(landed, forwards):
            arrived.wait_recv()
            fwd.start()
        for cp in from_sibling:
            cp.wait_recv()
        for cp in sends + forwards:
            cp.wait_send()

    return pl.pallas_call(
        body, name="allgather_shards", out_shape=jax.ShapeDtypeStruct((N_CHIPS,) + wp.shape, wp.dtype),
        in_specs=[_HBM], out_specs=_HBM,
        scratch_shapes=[pltpu.SemaphoreType.DMA((6,)), pltpu.SemaphoreType.DMA((6,))],
    )(wp)


def _exchange_halves(g):
    rows = g.shape[1]
    half = rows // 2

    def body(g_ref, r_ref, send_sem, recv_sem):
        x, y, c = _place()
        src = g_ref.at[:, pl.ds(pl.multiple_of((1 - c) * half, 8), half), :]
        cp = pltpu.make_async_remote_copy(src, r_ref, send_sem, recv_sem, device_id=(x, y, 1 - c), device_id_type=MESH)
        cp.start()
        cp.wait()

    return pl.pallas_call(
        body, name="exchange_halves", out_shape=jax.ShapeDtypeStruct((N_CHIPS, half, PACK_COLS), g.dtype),
        in_specs=[_HBM], out_specs=_HBM, scratch_shapes=[pltpu.SemaphoreType.DMA, pltpu.SemaphoreType.DMA],
    )(g)


def _add_halves(g, r, c):
    rows = g.shape[1]
    half = rows // 2
    br = REDUCE_ROWS
    nb = half // br

    def kern(c_ref, g_ref, r_ref, o_ref):
        o_ref[...] = (g_ref[...] + r_ref[...]).astype(o_ref.dtype)

    gs = pltpu.PrefetchScalarGridSpec(
        num_scalar_prefetch=1, grid=(N_CHIPS, nb),
        in_specs=[pl.BlockSpec((1, br, PACK_COLS), lambda k, t, c: (k, c[0] * nb + t, 0)),
                  pl.BlockSpec((1, br, PACK_COLS), lambda k, t, c: (k, t, 0))],
        out_specs=pl.BlockSpec((1, br, PACK_COLS), lambda k, t, c: (k, t, 0)))
    return pl.pallas_call(kern, name="add_halves", grid_spec=gs,
                          out_shape=jax.ShapeDtypeStruct((N_CHIPS, half, PACK_COLS), WIRE),
                          compiler_params=_params(("arbitrary", "arbitrary")))(c.reshape(1), g, r)


def _scatter_copies(s_ref, r_ref, send_sems, recv_sems):
    x, y, c = _place()
    my_chip = 2 * x + y
    sends, arrivals = [], []
    for j, (fx, fy) in enumerate(CHIP_FLIPS):
        px, py = _flip(x, y, fx, fy)
        slot = r_ref.at[2 * px + py]
        sends.append(pltpu.make_async_remote_copy(s_ref.at[2 * px + py], r_ref.at[my_chip], send_sems.at[j], recv_sems.at[j],
                                                  device_id=(px, py, c), device_id_type=MESH))
        arrivals.append(pltpu.make_async_remote_copy(slot, slot, send_sems.at[j], recv_sems.at[j],
                                                     device_id=(px, py, c), device_id_type=MESH))
    return sends, arrivals


N_DEVICES = 8


def _peer(x, y, c, f):
    px, py = _flip(x, y, f & 4, f & 2)
    return px, py, (1 - c if f & 1 else c)


def _scatter_all_copies(p_ref, r_ref, send_sems, recv_sems):
    half = p_ref.shape[1] // 2
    x, y, c = _place()
    me = 4 * x + 2 * y + c
    sends, arrivals = [], []
    for f in range(1, N_DEVICES):
        px, py, pc = _peer(x, y, c, f)
        theirs = p_ref.at[2 * px + py, pl.ds(pl.multiple_of(pc * half, 16), half), :]
        slot = r_ref.at[4 * px + 2 * py + pc]
        sends.append(pltpu.make_async_remote_copy(theirs, r_ref.at[me], send_sems.at[f - 1], recv_sems.at[f - 1],
                                                  device_id=(px, py, pc), device_id_type=MESH))
        arrivals.append(pltpu.make_async_remote_copy(slot, slot, send_sems.at[f - 1], recv_sems.at[f - 1],
                                                     device_id=(px, py, pc), device_id_type=MESH))
    return sends, arrivals


def _scatter_steps(first, last, p_ref, r_ref, sems):
    sends, arrivals = _scatter_all_copies(p_ref, r_ref, *sems)

    @pl.when(first)
    def _():
        for cp in sends:
            cp.start()

    @pl.when(last)
    def _():
        for cp in arrivals:
            cp.wait_recv()
        for cp in sends:
            cp.wait_send()


def _scatter_pieces(s):
    def body(s_ref, r_ref, send_sems, recv_sems):
        sends, arrivals = _scatter_copies(s_ref, r_ref, send_sems, recv_sems)
        for cp in sends:
            cp.start()
        for cp in arrivals:
            cp.wait_recv()
        for cp in sends:
            cp.wait_send()

    return pl.pallas_call(
        body, name="scatter_pieces", out_shape=jax.ShapeDtypeStruct(s.shape, s.dtype), in_specs=[_HBM], out_specs=_HBM,
        scratch_shapes=[pltpu.SemaphoreType.DMA((3,)), pltpu.SemaphoreType.DMA((3,))],
    )(s)


def _sum_pieces(r, mine, slot):
    slots, half = r.shape[:2]
    br = REDUCE_ROWS

    def kern(slot_ref, r_ref, m_ref, o_ref):
        total = None
        for k in range(slots):
            term = jnp.where(slot_ref[0] == k, m_ref[0], r_ref[k]).astype(F32)
            total = term if total is None else total + term
        o_ref[...] = total

    gs = pltpu.PrefetchScalarGridSpec(
        num_scalar_prefetch=1, grid=(half // br,),
        in_specs=[pl.BlockSpec((slots, br, PACK_COLS), lambda t, s: (0, t, 0)),
                  pl.BlockSpec((1, br, PACK_COLS), lambda t, s: (0, t, 0))],
        out_specs=pl.BlockSpec((br, PACK_COLS), lambda t, s: (t, 0)))
    return pl.pallas_call(kern, name="sum_pieces", grid_spec=gs, out_shape=_sds(half, PACK_COLS, F32),
                          compiler_params=_params(("arbitrary",)))(slot.reshape(1), r, mine)


def _join_halves(f):
    def body(f_ref, o_ref, send_sem, recv_sem):
        x, y, c = _place()
        cp = pltpu.make_async_remote_copy(f_ref, o_ref, send_sem, recv_sem, device_id=(x, y, 1 - c), device_id_type=MESH)
        cp.start()
        cp.wait()

    return pl.pallas_call(
        body, name="join_halves", out_shape=jax.ShapeDtypeStruct(f.shape, f.dtype), in_specs=[_HBM], out_specs=_HBM,
        scratch_shapes=[pltpu.SemaphoreType.DMA, pltpu.SemaphoreType.DMA],
    )(f)


def _allreduce_steps(first, last, g_ref, o_ref, buf, send_sems, recv_sems):
    x, y, c = _place()
    me = 4 * x + 2 * y + c
    sends, arrivals = [], []
    for f in range(1, N_DEVICES):
        px, py, pc = _peer(x, y, c, f)
        slot = buf.at[4 * px + 2 * py + pc]
        sends.append(pltpu.make_async_remote_copy(g_ref, buf.at[me], send_sems.at[f - 1], recv_sems.at[f - 1],
                                                  device_id=(px, py, pc), device_id_type=MESH))
        arrivals.append(pltpu.make_async_remote_copy(slot, slot, send_sems.at[f - 1], recv_sems.at[f - 1],
                                                     device_id=(px, py, pc), device_id_type=MESH))

    @pl.when(first)
    def _():
        buf[me] = g_ref[...]
        for cp in sends:
            cp.start()

    @pl.when(last)
    def _():
        for cp in arrivals:
            cp.wait_recv()
        for cp in sends:
            cp.wait_send()
        total = buf[0]
        for k in range(1, N_DEVICES):
            total = total + buf[k]
        o_ref[...] = total


def _adamw_update(g_ref, w_ref, m_ref, v_ref, d_o, m_o, v_o):
    c1 = 1.0 - ADAM_B1 ** ADAM_STEP
    c2 = 1.0 - ADAM_B2 ** ADAM_STEP
    g_ = g_ref[...]
    m_new = ADAM_B1 * m_ref[...] + (1.0 - ADAM_B1) * g_
    v_new = ADAM_B2 * v_ref[...] + (1.0 - ADAM_B2) * (g_ * g_)
    m_o[...] = m_new
    v_o[...] = v_new
    d_o[...] = -ADAM_LR * ((m_new / c1) / (jnp.sqrt(v_new / c2) + ADAM_EPS) + ADAM_WD * w_ref[...])


ADAMW_ROWS = 256


def _adamw(name, g, w, m, v):
    _, rows, cols = w.shape
    br = int(np.gcd(ADAMW_ROWS, rows))

    def kern(*refs):
        _adamw_update(*refs)

    spec = pl.BlockSpec((1, br, cols), lambda t: (0, t, 0))
    out = jax.ShapeDtypeStruct(w.shape, F32)
    return pl.pallas_call(kern, name="adamw_" + name, grid=(rows // br,), in_specs=[spec] * 4, out_specs=[spec] * 3,
                          out_shape=[out, out, out], compiler_params=_params(("arbitrary",)))(g, w, m, v)


def _adamw_small(gs, ws, ms, vs):
    n = len(gs)

    def kern(*refs):
        ins, outs = refs[:4 * n], refs[4 * n:]
        for k in range(n):
            _adamw_update(ins[k], ins[n + k], ins[2 * n + k], ins[3 * n + k], outs[k], outs[n + k], outs[2 * n + k])

    vmem = pl.BlockSpec(memory_space=pltpu.VMEM)
    out = [jax.ShapeDtypeStruct(w.shape, F32) for w in ws]
    res = pl.pallas_call(kern, name="adamw_small", in_specs=[vmem] * (4 * n), out_specs=[vmem] * (3 * n),
                         out_shape=out * 3, compiler_params=pltpu.CompilerParams(vmem_limit_bytes=VMEM_LIMIT))(*gs, *ws, *ms, *vs)
    return [(res[k], res[n + k], res[2 * n + k]) for k in range(n)]


def _shard_rows(shape, axis):
    k, n = shape
    return (k * n // N_CHIPS) // PACK_COLS


def _group(names):
    entries = [e for e in SHARDED if e[0] in names]
    used = sum(_shard_rows(shape, axis) for _, shape, axis in entries)
    return entries, -(-used // (2 * REDUCE_ROWS)) * 2 * REDUCE_ROWS


def _pack_shards(shards, names, dtype):
    entries, rows = _group(names)
    parts = [shards[name].astype(dtype).reshape(-1, PACK_COLS) for name, _, _ in entries]
    used = sum(p.shape[0] for p in parts)
    if rows > used:
        parts.append(jnp.zeros((rows - used, PACK_COLS), dtype))
    return jnp.concatenate(parts, 0)


def _unpack_shards(packed, names):
    out, r0 = {}, 0
    for name, (k, n), axis in _group(names)[0]:
        nr = _shard_rows((k, n), axis)
        shape = (k // N_CHIPS, n) if axis == 0 else (k, n // N_CHIPS)
        out[name] = packed[r0:r0 + nr].reshape(shape)
        r0 += nr
    return out


def _unpack_full(gathered, own, my_chip, names):
    chip = lax.broadcasted_iota(jnp.int32, (N_CHIPS, 1, 1), 0)
    gathered = jnp.where(chip == my_chip, own[None], gathered)
    out, r0 = {}, 0
    for name, (k, n), axis in _group(names)[0]:
        nr = _shard_rows((k, n), axis)
        part = gathered[:, r0:r0 + nr]
        if axis == 0:
            out[name] = part.reshape(k, n)
        else:
            out[name] = part.reshape(N_CHIPS, k, n // N_CHIPS).transpose(1, 0, 2).reshape(k, n)
        r0 += nr
    return out


def _pack_pieces(grads, names, dtype=F32):
    entries, rows = _group(names)
    parts = []
    for name, (k, n), axis in entries:
        g = grads[name].astype(dtype)
        if axis == 0:
            parts.append(g.reshape(N_CHIPS, -1, PACK_COLS))
        else:
            parts.append(g.reshape(k, N_CHIPS, n // N_CHIPS).transpose(1, 0, 2).reshape(N_CHIPS, -1, PACK_COLS))
    used = sum(p.shape[1] for p in parts)
    if rows > used:
        parts.append(jnp.zeros((N_CHIPS, rows - used, PACK_COLS), dtype))
    return jnp.concatenate(parts, 1)


def _pack_small(vals, last_row):
    flat = jnp.concatenate([vals[name].astype(F32).reshape(-1) for name, _ in SMALL])
    spare = jnp.zeros(((SMALL_ROWS - 1) * PACK_COLS - flat.shape[0],), F32)
    return jnp.concatenate([flat, spare, last_row.reshape(-1)]).reshape(SMALL_ROWS, PACK_COLS)


def _unpack_small(packed):
    flat, out, o = packed.reshape(-1), {}, 0
    for name, shape in SMALL:
        n = int(np.prod(shape))
        out[name] = flat[o:o + n].reshape(shape)
        o += n
    return out


def kernel(x, p, positions, g_pre_mix, w_in, b_gate, g_q, w_uq, g_kv, w_ukv, w_pool, pool_scale, w_branch_attn, w_branch_pool, w_out, g_post_mix, g_pre_mlp, w_ff1, w_ff2, g_post_mlp, w_ple_proj, w_ple_gate, g_ple, loss_target, m_g_pre_mix, m_w_in, m_b_gate, m_g_q, m_w_uq, m_g_kv, m_w_ukv, m_w_pool, m_pool_scale, m_w_branch_attn, m_w_branch_pool, m_w_out, m_g_post_mix, m_g_pre_mlp, m_w_ff1, m_w_ff2, m_g_post_mlp, m_w_ple_proj, m_w_ple_gate, m_g_ple, v_g_pre_mix, v_w_in, v_b_gate, v_g_q, v_w_uq, v_g_kv, v_w_ukv, v_w_pool, v_pool_scale, v_w_branch_attn, v_w_branch_pool, v_w_out, v_g_post_mix, v_g_pre_mlp, v_w_ff1, v_w_ff2, v_g_post_mlp, v_w_ple_proj, v_w_ple_gate, v_g_ple):
    given = dict(locals())
    weights = {n: given[n] for n in WEIGHT_ORDER}
    moments_m = {n: given["m_" + n] for n in WEIGHT_ORDER}
    moments_v = {n: given["v_" + n] for n in WEIGHT_ORDER}
    c = lax.axis_index("c")

    big_w = {name: weights[name][0] for name, _, _ in SHARDED}
    my_chip = 2 * lax.axis_index("x") + lax.axis_index("y")
    packed_first = _pack_shards(big_w, FIRST, MX)
    full = _unpack_full(_allgather_shards(packed_first), packed_first, my_chip, FIRST)
    for name, _ in SMALL:
        full[name] = weights[name][0] if name == "w_pool" else weights[name]

    loss_cols, dx, grads, (pieces_rest, received_rest) = _local_step(
        x[0], p[0, 0], positions[0], loss_target[0], full, (my_chip, c), _pack_shards(big_w, REST, MX))
    loss = 0.5 * jnp.sum(loss_cols) / D_MODEL

    def finish(received, mine, slot):
        reduced = _sum_pieces(received, mine, slot)
        theirs = _join_halves(reduced)
        return jnp.where(c == 0, jnp.concatenate([reduced, theirs]), jnp.concatenate([theirs, reduced]))

    pieces = _pack_pieces(grads, FIRST)
    sent = _add_halves(pieces, _exchange_halves(pieces), c)
    mine = lax.dynamic_slice(sent, (my_chip, 0, 0), (1,) + sent.shape[1:])
    shards = _unpack_shards(finish(_scatter_pieces(sent), mine, my_chip), FIRST)
    half = received_rest.shape[1]
    mine = lax.dynamic_slice(pieces_rest, (my_chip, c * half, 0), (1, half, PACK_COLS))
    shards.update(_unpack_shards(finish(received_rest, mine, 2 * my_chip + c), REST))

    out = {}
    for name, g in shards.items():
        out[name] = (g[None], *_adamw(name, g[None], weights[name], moments_m[name], moments_v[name]))
    small_g = {n: grads[n] for n, _ in SMALL}
    names = [n for n, _ in SMALL]
    updates = _adamw_small([small_g[n] for n in names], [weights[n] for n in names], [moments_m[n] for n in names],
                           [moments_v[n] for n in names])
    for n, upd in zip(names, updates):
        out[n] = (small_g[n], *upd)
    return (loss, dx[None], *[out[n][k] for k in range(4) for n in WEIGHT_ORDER])
```

```python
import numpy as np
import jax
import jax.numpy as jnp
from jax import lax
from jax.experimental import pallas as pl
from jax.experimental.pallas import tpu as pltpu

F32 = jnp.float32
MX = jnp.bfloat16
WIRE = jnp.bfloat16

D_MODEL = 1024
N_HEADS = 8
QK_NOPE = 64
QK_ROPE = 32
V_HEAD = 64
Q_LORA = 384
KV_LORA = 256
POOL_WINDOWS = (2, 4, 8, 16)
POOL_GROUP = 128
POOL_WIDTH = 512
D_FF = 4096
ROPE_THETA = 10000.0
EPS = 1e-6
HEAD_SLOT = 128
QK_WIDTH = N_HEADS * HEAD_SLOT
ROPE_LANE = 64
SMALL_COLS = Q_LORA + KV_LORA + HEAD_SLOT
IN_PAD = SMALL_COLS + POOL_WIDTH + 2 * D_MODEL
SCALE = (QK_NOPE + QK_ROPE) ** -0.5
LOG2E = 1.4426950408889634
NEG = -1e30
HALO = 16

ADAM_LR = 0.001
ADAM_B1 = 0.9
ADAM_B2 = 0.999
ADAM_EPS = 1e-08
ADAM_WD = 0.01
ADAM_STEP = 10

VMEM_LIMIT = 56 * 2**20
TOKEN_TILE = 512
MESH = pl.DeviceIdType.MESH

SHARDED = (
    ("w_in", (1024, 3232), 1),
    ("w_uq", (384, 768), 1),
    ("w_ukv", (256, 1024), 1),
    ("w_branch_attn", (512, 1024), 1),
    ("w_branch_pool", (512, 1024), 1),
    ("w_out", (1024, 1024), 0),
    ("w_ff1", (1024, 4096), 1),
    ("w_ff2", (4096, 1024), 0),
    ("w_ple_proj", (256, 1024), 1),
    ("w_ple_gate", (1024, 1024), 0),
)
SMALL = (
    ("g_pre_mix", (1, 1024)),
    ("b_gate", (1, 2048)),
    ("g_q", (1, 384)),
    ("g_kv", (1, 256)),
    ("w_pool", (1, 4, 128, 128)),
    ("pool_scale", (1, 512)),
    ("g_post_mix", (1, 1024)),
    ("g_pre_mlp", (1, 1024)),
    ("g_post_mlp", (1, 1024)),
    ("g_ple", (1, 1024)),
)
WEIGHT_ORDER = ("g_pre_mix", "w_in", "b_gate", "g_q", "w_uq", "g_kv", "w_ukv", "w_pool", "pool_scale", "w_branch_attn",
                "w_branch_pool", "w_out", "g_post_mix", "g_pre_mlp", "w_ff1", "w_ff2", "g_post_mlp", "w_ple_proj",
                "w_ple_gate", "g_ple")
N_CHIPS = 4
PACK_COLS = 1024
REDUCE_ROWS = 160
SMALL_ROWS = 80
FIRST = ("w_in", "w_uq", "w_ukv")
REST = tuple(name for name, _, _ in SHARDED if name not in FIRST)


def _dot(a, b):
    return jnp.dot(a.astype(MX), b.astype(MX), preferred_element_type=F32)


def _dot_nt(a, b):
    return lax.dot_general(a.astype(MX), b.astype(MX), (((1,), (1,)), ((), ())), preferred_element_type=F32)


def _dot_tn(a, b):
    return lax.dot_general(a.astype(MX), b.astype(MX), (((0,), (0,)), ((), ())), preferred_element_type=F32)


def _sig(x):
    return 1.0 / (1.0 + jnp.exp(-x))


def _rms(x, g):
    r = lax.rsqrt(jnp.mean(x * x, axis=1, keepdims=True) + EPS)
    xh = x * r
    return xh * g, xh, r


def _rms_bwd(xh, r, g, dy):
    dxn = dy * g
    dx = r * (dxn - xh * jnp.mean(dxn * xh, axis=1, keepdims=True))
    return dx, jnp.sum(dy * xh, axis=0, keepdims=True)


def _rot_half(v):
    lane = lax.broadcasted_iota(jnp.int32, v.shape, 1)
    return jnp.where(lane < ROPE_LANE + QK_ROPE // 2, pltpu.roll(v, HEAD_SLOT - QK_ROPE // 2, 1), pltpu.roll(v, QK_ROPE // 2, 1))


def _rope(v, cc, sa):
    return v * cc + _rot_half(v) * sa


def _unrope(v, cc, sa):
    return v * cc - _rot_half(v) * sa


def _params(sem):
    return pltpu.CompilerParams(dimension_semantics=sem, vmem_limit_bytes=VMEM_LIMIT)


def _tok_call(name, body, n_tok, tm, tiled, resident, outs, accs=(), scratch=(), exchange=None):
    def as_pair(t):
        if isinstance(t, tuple):
            return t
        return t, pl.BlockSpec((tm, t.shape[1]), lambda i: (i, 0))
    tiled = [as_pair(t) for t in tiled]
    outs = [as_pair(o) for o in outs]
    res_specs = [pl.BlockSpec(r.shape, lambda i, nd=r.ndim: (0,) * nd, pipeline_mode=pl.Buffered(1)) for r in resident]
    out_specs = [s for _, s in outs] + [pl.BlockSpec(a.shape, lambda i: (0, 0)) for a in accs]
    n_t, n_r, n_o, n_a, n_s = len(tiled), len(resident), len(outs), len(accs), len(scratch)
    n_steps = n_tok // tm
    operands = [a for a, _ in tiled] + list(resident)
    in_specs = [s for _, s in tiled] + res_specs
    out_shape = [o for o, _ in outs] + list(accs)
    scratch = list(scratch)
    if exchange is not None:
        ex_in, ex_out, ex_sems, ex_steps = exchange
        operands.append(ex_in)
        in_specs.append(_HBM)
        out_shape.append(ex_out)
        out_specs.append(_HBM)
        scratch += list(ex_sems)

    def kern(*refs):
        refs = list(refs)
        n_in = n_t + n_r + (exchange is not None)
        n_out = n_o + n_a + (exchange is not None)
        tin, res = refs[:n_t], refs[n_t:n_t + n_r]
        tout = refs[n_in:n_in + n_o]
        acc = refs[n_in + n_o:n_in + n_o + n_a]
        scr = refs[n_in + n_out:n_in + n_out + n_s]
        i = pl.program_id(0)
        if exchange is not None:
            ex_steps(i, n_steps, refs[n_in - 1], refs[n_in + n_out - 1], refs[n_in + n_out + n_s:])

        @pl.when(i == 0)
        def _():
            for a in acc:
                a[...] = jnp.zeros(a.shape, a.dtype)
        body(i, tin, res, tout, acc, scr)

    return pl.pallas_call(
        kern, name=name, grid=(n_steps,), in_specs=in_specs, out_specs=out_specs,
        out_shape=out_shape, scratch_shapes=scratch, compiler_params=_params(("arbitrary",)),
    )(*operands)


def _sds(rows, cols, dtype):
    return jax.ShapeDtypeStruct((rows, cols), dtype)


def _fwd_inproj(x, cc, sa, w, tm, gather=None):
    n_tok = x.shape[0]

    def body(i, tin, res, tout, acc, scr):
        x_ref, c_ref, s_ref = tin
        g_pre, w_in, g_q, w_uq, g_kv, w_k, w_v, e_mat, w_kt, e_t, w_vt, v_ones = res
        a_o, ps_o, u_o, gl_o, qn_o, kvn_o, q_o, k_o, v_o, kt_o, vt_o = tout
        a = _rms(x_ref[...], g_pre[...])[0].astype(MX)
        a_o[...] = a
        ps = _dot(a, w_in[:, :SMALL_COLS])
        ps_o[...] = ps.astype(ps_o.dtype)
        u_o[...] = _dot(a, w_in[:, SMALL_COLS:SMALL_COLS + POOL_WIDTH]).astype(u_o.dtype)
        gl_o[...] = _dot(a, w_in[:, SMALL_COLS + POOL_WIDTH:]).astype(gl_o.dtype)
        cc_, sa_ = c_ref[...], s_ref[...]
        qn = _rms(ps[:, :Q_LORA], g_q[...])[0].astype(MX)
        qn_o[...] = qn
        q = _dot(qn, w_uq[...])
        for h in range(N_HEADS):
            hs = slice(h * HEAD_SLOT, (h + 1) * HEAD_SLOT)
            q_o[:, hs] = (_rope(q[:, hs], cc_, sa_) * (SCALE * LOG2E)).astype(q_o.dtype)
        kvn = _rms(ps[:, Q_LORA:Q_LORA + KV_LORA], g_kv[...])[0].astype(MX)
        kvn_o[...] = kvn
        kr = _rope(ps[:, Q_LORA + KV_LORA:], cc_, sa_)
        k_o[...] = (_dot(kvn, w_k[...]) + _dot(kr, e_mat[...])).astype(k_o.dtype)
        v_o[...] = _dot(kvn, w_v[...]).astype(v_o.dtype)
        kt_o[...] = (_dot_nt(w_kt[...], kvn) + _dot_nt(e_t[...], kr)).astype(kt_o.dtype)
        vt_o[...] = (_dot_nt(w_vt[...], kvn) + v_ones[...]).astype(vt_o.dtype)

    outs = [_sds(n_tok, D_MODEL, MX), _sds(n_tok, SMALL_COLS, MX), _sds(n_tok, POOL_WIDTH, MX), _sds(n_tok, 2 * D_MODEL, MX),
            _sds(n_tok, Q_LORA, MX), _sds(n_tok, KV_LORA, MX), _sds(n_tok, QK_WIDTH, MX), _sds(n_tok, QK_WIDTH, MX),
            _sds(n_tok, N_HEADS * V_HEAD, MX),
            (_sds(QK_WIDTH, n_tok, MX), pl.BlockSpec((QK_WIDTH, tm), lambda i: (0, i))),
            (_sds(N_HEADS * V_ROWS, n_tok, MX), pl.BlockSpec((N_HEADS * V_ROWS, tm), lambda i: (0, i)))]
    res = [w["g_pre_mix"], w["w_in"], w["g_q"], w["w_uq"], w["g_kv"], w["w_k"], w["w_v"], w["e_mat"], w["w_kt"], w["e_t"],
           w["w_vt"], w["v_ones"]]
    exchange = None
    if gather is not None:
        gathered = jax.ShapeDtypeStruct((N_CHIPS,) + gather.shape, gather.dtype)
        exchange = (gather, gathered, [pltpu.SemaphoreType.DMA((6,)), pltpu.SemaphoreType.DMA((6,))], _gather_steps)
    return _tok_call("fwd_inproj", body, n_tok, tm, [x, cc, sa], res, outs, exchange=exchange)


def _causal_pairs(nq, ratio, by_kv):
    if by_kv:
        pairs = [(i, j) for j in range(nq * ratio) for i in range(j // ratio, nq)]
    else:
        pairs = [(i, j) for i in range(nq) for j in range((i + 1) * ratio)]
    return (jnp.asarray(np.array([p[0] for p in pairs], np.int32)), jnp.asarray(np.array([p[1] for p in pairs], np.int32)))


def _keep_t(tk, tq, off):
    return lax.broadcasted_iota(jnp.int32, (tk, tq), 0) + off <= lax.broadcasted_iota(jnp.int32, (tk, tq), 1)


ATTN_DIAG_KEYS = 512
ATTN_FWD_TILE = (1024, 1024)
ATTN_DIAG_KEYS_BWD = 256
ATTN_BWD_TILE = (1024, 512)
V_ROWS = 80


def _attn_fwd(q, k, vt, tq, tk):
    n_tok = q.shape[0]
    nq, ratio = n_tok // tq, tq // tk
    qi, kj = _causal_pairs(nq, ratio, by_kv=False)

    def kern(qi_ref, kj_ref, q_ref, k_ref, vt_ref, o_ref, lse_ref, m_s, acc_s, st_s):
        s_id = pl.program_id(0)
        i, j = qi_ref[s_id], kj_ref[s_id]

        @pl.when(j == 0)
        def _():
            m_s[...] = jnp.full(m_s.shape, NEG, F32)
            acc_s[...] = jnp.zeros(acc_s.shape, F32)

        def heads(ks, qs, masked):
            n_keys = ks.stop - ks.start
            keep = _keep_t(n_keys, qs.stop - qs.start, 0) if masked else None

            def scores(h):
                hs = slice(h * HEAD_SLOT, (h + 1) * HEAD_SLOT)
                return _dot_nt(k_ref[ks, hs], q_ref[qs, hs])

            st_s[0, :n_keys, qs] = scores(0)
            st_s[1, :n_keys, qs] = scores(1)
            for h in range(N_HEADS):
                if h + 2 < N_HEADS:
                    st_s[(h + 2) % 3, :n_keys, qs] = scores(h + 2)
                st = st_s[h % 3, :n_keys, qs]
                if masked:
                    st = jnp.where(keep, st, NEG)
                m_old = m_s[h, :, qs]
                m_new = jnp.maximum(m_old, jnp.max(st, axis=0, keepdims=True))
                pt = jnp.exp2(st - m_new)
                acc_s[h, :, qs] = (jnp.exp2(m_old - m_new) * acc_s[h, :, qs]
                                   + _dot(vt_ref[h * V_ROWS:(h + 1) * V_ROWS, ks], pt))
                m_s[h, :, qs] = m_new

        @pl.when(j < i * ratio)
        def _():
            heads(slice(0, tk), slice(0, tq), False)

        sub = min(tk, ATTN_DIAG_KEYS)
        for part in range(ratio):
            @pl.when(j == i * ratio + part)
            def _():
                for first in range(0, tk, sub):
                    heads(slice(first, first + sub), slice(part * tk + first, tq), True)

        @pl.when(j == (i + 1) * ratio - 1)
        def _():
            heads_out = []
            for h in range(N_HEADS):
                total = acc_s[h, V_HEAD:V_HEAD + 1, :]
                heads_out.append(acc_s[h, :V_HEAD, :] / total)
                lse_ref[h:h + 1, :] = m_s[h] + jnp.log2(total)
            o_ref[...] = jnp.concatenate(heads_out, 0).T.astype(o_ref.dtype)

    gs = pltpu.PrefetchScalarGridSpec(
        num_scalar_prefetch=2, grid=(qi.shape[0],),
        in_specs=[pl.BlockSpec((tq, QK_WIDTH), lambda s, qi, kj: (qi[s], 0)),
                  pl.BlockSpec((tk, QK_WIDTH), lambda s, qi, kj: (kj[s], 0)),
                  pl.BlockSpec((N_HEADS * V_ROWS, tk), lambda s, qi, kj: (0, kj[s]))],
        out_specs=[pl.BlockSpec((tq, N_HEADS * V_HEAD), lambda s, qi, kj: (qi[s], 0)),
                   pl.BlockSpec((N_HEADS, tq), lambda s, qi, kj: (0, qi[s]))],
        scratch_shapes=[pltpu.VMEM((N_HEADS, 1, tq), F32), pltpu.VMEM((N_HEADS, V_ROWS, tq), F32),
                        pltpu.VMEM((3, tk, tq), F32)])
    return pl.pallas_call(kern, name="attn_fwd", grid_spec=gs,
                          out_shape=[_sds(n_tok, N_HEADS * V_HEAD, MX), _sds(N_HEADS, n_tok, F32)],
                          compiler_params=_params(("arbitrary",)))(qi, kj, q, k, vt)


def _pool_windows(ext, i, tm, first_row):
    row = i * tm + lax.broadcasted_iota(jnp.int32, (tm, 1), 0)
    out = []
    for g, w in enumerate(POOL_WINDOWS):
        cs = slice(g * POOL_GROUP, (g + 1) * POOL_GROUP)
        s = ext[pl.ds(first_row, tm), cs]
        for k in range(1, w):
            s = s + ext[pl.ds(first_row - k, tm), cs]
        cnt = jnp.minimum(row + 1, w).astype(F32)
        out.append(s / cnt)
    return out


def _fwd_mix(x, u, gl, attn, w, tm):
    n_tok = x.shape[0]
    halo_spec = pl.BlockSpec((HALO, POOL_WIDTH), lambda i: (jnp.maximum(i * (tm // HALO) - 1, 0), 0))

    def body(i, tin, res, tout, acc, scr):
        x_ref, u_ref, uh_ref, gl_ref, at_ref = tin
        w_pool, pool_scale, w_ba, w_bp, b_gate, w_out, g_post = res
        d_o, pooled_o, a_o, pp_o, merged_o, y_o, h1_o = tout
        ext, = scr
        ext[pl.ds(0, HALO), :] = jnp.where(i > 0, uh_ref[...].astype(F32), 0.0)
        ext[pl.ds(HALO, tm), :] = u_ref[...].astype(F32)
        means = _pool_windows(ext, i, tm, HALO)
        for g in range(len(POOL_WINDOWS)):
            cs = slice(g * POOL_GROUP, (g + 1) * POOL_GROUP)
            d = (means[g] - ext[pl.ds(HALO, tm), cs]).astype(MX)
            d_o[:, cs] = d
            pooled_o[:, cs] = (_dot(d, w_pool[g]) * pool_scale[:, cs]).astype(pooled_o.dtype)
        a_br = _dot(at_ref[...], w_ba[...])
        p_br = _dot(pooled_o[...], w_bp[...])
        a_o[...] = a_br.astype(a_o.dtype)
        pp_o[...] = p_br.astype(pp_o.dtype)
        gates = _sig(gl_ref[...].astype(F32) + b_gate[...])
        merged = (gates[:, :D_MODEL] * a_br + gates[:, D_MODEL:] * p_br).astype(MX)
        merged_o[...] = merged
        y = _dot(merged, w_out[...])
        y_o[...] = y.astype(y_o.dtype)
        h1_o[...] = x_ref[...] + _rms(y, g_post[...])[0]

    outs = [_sds(n_tok, POOL_WIDTH, MX), _sds(n_tok, POOL_WIDTH, MX), _sds(n_tok, D_MODEL, MX), _sds(n_tok, D_MODEL, MX),
            _sds(n_tok, D_MODEL, MX), _sds(n_tok, D_MODEL, MX), _sds(n_tok, D_MODEL, F32)]
    res = [w["w_pool"], w["pool_scale"], w["w_branch_attn"], w["w_branch_pool"], w["b_gate"], w["w_out"], w["g_post_mix"]]
    return _tok_call("fwd_mix", body, n_tok, tm, [x, u, (u, halo_spec), gl, attn], res, outs,
                     scratch=[pltpu.VMEM((tm + HALO, POOL_WIDTH), F32)])


def _fwd_mlp(h1, w, tm):
    n_tok = h1.shape[0]

    def body(i, tin, res, tout, acc, scr):
        h1_ref, = tin
        g_pre, w1, w2, g_post = res
        m_o, zr_o, f_o, h2_o = tout
        h1_ = h1_ref[...]
        m = _rms(h1_, g_pre[...])[0].astype(MX)
        m_o[...] = m
        zr = jnp.maximum(_dot(m, w1[...]), 0.0)
        zr_o[...] = zr.astype(zr_o.dtype)
        a2 = (zr * zr).astype(MX)
        f = _dot(a2, w2[...])
        f_o[...] = f.astype(f_o.dtype)
        h2_o[...] = h1_ + _rms(f, g_post[...])[0]

    outs = [_sds(n_tok, D_MODEL, MX), _sds(n_tok, D_FF, MX), _sds(n_tok, D_MODEL, MX),
            _sds(n_tok, D_MODEL, F32)]
    res = [w["g_pre_mlp"], w["w_ff1"], w["w_ff2"], w["g_post_mlp"]]
    return _tok_call("fwd_mlp", body, n_tok, tm, [h1], res, outs)


def _ple_fwd_bwd(h2, p, target, w, tm):
    n_tok = h2.shape[0]

    def body(i, tin, res, tout, acc, scr):
        h2_ref, p_ref, t_ref = tin
        w_pe, w_pg, g_ple = res
        dh2_o, de_o, dzg_o = tout
        loss_a, dg_a = acc
        h2_ = h2_ref[...]
        e = _dot(p_ref[...], w_pe[...])
        pg = _sig(_dot(h2_, w_pg[...]))
        t = pg * e
        g = g_ple[...]
        tn, th, r = _rms(t, g)
        diff = h2_ + tn - t_ref[...]
        loss_a[...] += jnp.sum(diff * diff, axis=0, keepdims=True)
        dh3 = diff * (1.0 / D_MODEL)
        dt, dg = _rms_bwd(th, r, g, dh3)
        dg_a[...] += dg
        de_o[...] = (dt * pg).astype(de_o.dtype)
        dzg = (dt * e * pg * (1.0 - pg)).astype(MX)
        dzg_o[...] = dzg
        dh2_o[...] = dh3 + _dot_nt(dzg, w_pg[...])

    outs = [_sds(n_tok, D_MODEL, F32), _sds(n_tok, D_MODEL, MX), _sds(n_tok, D_MODEL, MX)]
    accs = [_sds(1, D_MODEL, F32), _sds(1, D_MODEL, F32)]
    return _tok_call("ple_fwd_bwd", body, n_tok, tm, [h2, p, target], [w["w_ple_proj"], w["w_ple_gate"], w["g_ple"]], outs, accs)


def _bwd_mlp(dh2, f, h1, zr, w, tm):
    n_tok = dh2.shape[0]

    def body(i, tin, res, tout, acc, scr):
        dh2_ref, f_ref, h1_ref, zr_ref = tin
        g_pre, w1, w2, g_post = res
        df_o, dz_o, dh1_o = tout
        dg_post_a, dg_pre_a = acc
        dh2_ = dh2_ref[...]
        gp = g_post[...]
        _, fh, rf = _rms(f_ref[...].astype(F32), gp)
        df, dg = _rms_bwd(fh, rf, gp, dh2_)
        dg_post_a[...] += dg
        df = df.astype(MX)
        df_o[...] = df
        dz = (_dot_nt(df, w2[...]) * (2.0 * zr_ref[...].astype(F32))).astype(MX)
        dz_o[...] = dz
        dm = _dot_nt(dz, w1[...])
        gq = g_pre[...]
        _, hh, rh = _rms(h1_ref[...], gq)
        dh1, dg = _rms_bwd(hh, rh, gq, dm)
        dg_pre_a[...] += dg
        dh1_o[...] = dh2_ + dh1

    outs = [_sds(n_tok, D_MODEL, MX), _sds(n_tok, D_FF, MX), _sds(n_tok, D_MODEL, F32)]
    accs = [_sds(1, D_MODEL, F32), _sds(1, D_MODEL, F32)]
    res = [w["g_pre_mlp"], w["w_ff1"], w["w_ff2"], w["g_post_mlp"]]
    return _tok_call("bwd_mlp", body, n_tok, tm, [dh2, f, h1, zr], res, outs, accs)


def _bwd_mix(dh1, y, a_br, p_br, gl, attn, d, w, tm):
    n_tok = dh1.shape[0]

    def body(i, tin, res, tout, acc, scr):
        dh1_ref, y_ref, a_ref, pp_ref, gl_ref, at_ref, d_ref = tin
        g_post, w_out, b_gate, w_ba, w_bp, w_pool, pool_scale, sel = res
        dy_o, da_o, dpp_o, dgl_o, do_o, delta_o, dyp_o, dd_o = tout
        dg_post_a, db_a, dps_a = acc
        g = g_post[...]
        _, yh, r = _rms(y_ref[...].astype(F32), g)
        dy, dg = _rms_bwd(yh, r, g, dh1_ref[...])
        dg_post_a[...] += dg
        dy = dy.astype(MX)
        dy_o[...] = dy
        dmerged = _dot_nt(dy, w_out[...])
        gates = _sig(gl_ref[...].astype(F32) + b_gate[...])
        ga, gp = gates[:, :D_MODEL], gates[:, D_MODEL:]
        da = (dmerged * ga).astype(MX)
        dpp = (dmerged * gp).astype(MX)
        da_o[...] = da
        dpp_o[...] = dpp
        dgl_a = dmerged * a_ref[...].astype(F32) * ga * (1.0 - ga)
        dgl_p = dmerged * pp_ref[...].astype(F32) * gp * (1.0 - gp)
        dgl_o[:, :D_MODEL] = dgl_a.astype(dgl_o.dtype)
        dgl_o[:, D_MODEL:] = dgl_p.astype(dgl_o.dtype)
        db_a[:, :D_MODEL] += jnp.sum(dgl_a, axis=0, keepdims=True)
        db_a[:, D_MODEL:] += jnp.sum(dgl_p, axis=0, keepdims=True)
        do = _dot_nt(da, w_ba[...]).astype(MX)
        do_o[...] = do
        prod = do.astype(F32) * at_ref[...].astype(F32)
        hi = prod.astype(MX)
        lo = (prod - hi.astype(F32)).astype(MX)
        delta_o[...] = _dot(hi, sel[...]) + _dot(lo, sel[...])
        dpooled = _dot_nt(dpp, w_bp[...])
        for gi in range(len(POOL_WINDOWS)):
            cs = slice(gi * POOL_GROUP, (gi + 1) * POOL_GROUP)
            ypre = _dot(d_ref[:, cs], w_pool[gi])
            dps_a[:, cs] += jnp.sum(dpooled[:, cs] * ypre, axis=0, keepdims=True)
            dyp = (dpooled[:, cs] * pool_scale[:, cs]).astype(MX)
            dyp_o[:, cs] = dyp
            dd_o[:, cs] = _dot_nt(dyp, w_pool[gi])

    outs = [_sds(n_tok, D_MODEL, MX), _sds(n_tok, D_MODEL, MX), _sds(n_tok, D_MODEL, MX), _sds(n_tok, 2 * D_MODEL, MX),
            _sds(n_tok, N_HEADS * V_HEAD, MX), _sds(n_tok, HEAD_SLOT, F32), _sds(n_tok, POOL_WIDTH, MX),
            _sds(n_tok, POOL_WIDTH, F32)]
    accs = [_sds(1, D_MODEL, F32), _sds(1, 2 * D_MODEL, F32), _sds(1, POOL_WIDTH, F32)]
    res = [w["g_post_mix"], w["w_out"], w["b_gate"], w["w_branch_attn"], w["w_branch_pool"], w["w_pool"], w["pool_scale"],
           w["head_sel"]]
    return _tok_call("bwd_mix", body, n_tok, tm, [dh1, y, a_br, p_br, gl, attn, d], res, outs, accs)


def _bwd_heads(q_ref, k_ref, v_ref, do_ref, lse_ref, dl_ref, st_s, dpt_s, keep, use, n_heads, qs, ks):
    def products(h):
        hs = slice(h * HEAD_SLOT, (h + 1) * HEAD_SLOT)
        vs = slice(h * V_HEAD, (h + 1) * V_HEAD)
        st_s[h % 2, ks, qs] = _dot_nt(k_ref[ks, hs], q_ref[qs, hs])
        dpt_s[h % 2, ks, qs] = _dot_nt(v_ref[ks, vs], do_ref[qs, vs])

    products(0)
    for h in range(n_heads):
        if h + 1 < n_heads:
            products(h + 1)
        st = st_s[h % 2, ks, qs]
        if keep is not None:
            st = jnp.where(keep, st, NEG)
        pt = jnp.exp2(st - lse_ref[h:h + 1, qs])
        use(h, pt, pt * (dpt_s[h % 2, ks, qs] - dl_ref[h:h + 1, qs]))


HEAD_GROUP = 4


def _attn_bwd(q, k, kt, v, do, lse, delta, tq, tk, scatter=None):
    n_tok = q.shape[0]
    nq, ratio = n_tok // tq, tq // tk
    n_groups = N_HEADS // HEAD_GROUP
    gq, gv = HEAD_GROUP * HEAD_SLOT, HEAD_GROUP * V_HEAD
    qi, kj = _causal_pairs(nq, ratio, by_kv=True)

    n_pairs = qi.shape[0]

    def kern(qi_ref, kj_ref, q_ref, k_ref, kt_ref, v_ref, do_ref, lse_ref, dl_ref, *rest):
        if scatter is not None:
            s_hbm, dq_ref, dk_ref, dv_ref, r_hbm, dk_s, dv_s, st_s, dpt_s, send_sems, recv_sems = rest
        else:
            dq_ref, dk_ref, dv_ref, dk_s, dv_s, st_s, dpt_s = rest
        s_id = pl.program_id(1)
        i, j = qi_ref[s_id], kj_ref[s_id]
        if scatter is not None:
            group = pl.program_id(0)
            _scatter_steps(jnp.logical_and(group == 0, s_id == 0),
                           jnp.logical_and(group == n_groups - 1, s_id == n_pairs - 1), s_hbm, r_hbm, (send_sems, recv_sems))

        @pl.when(s_id == 0)
        def _():
            dq_ref[...] = jnp.zeros(dq_ref.shape, F32)

        def heads(ks, first_query, masked):
            qs = slice(first_query, tq)
            cols = pl.ds(pl.multiple_of(i * tq + first_query, ATTN_DIAG_KEYS_BWD), tq - first_query)

            def use(h, pt, dst):
                hs = slice(h * HEAD_SLOT, (h + 1) * HEAD_SLOT)
                dv_s[h, ks, :] += _dot(pt, do_ref[qs, h * V_HEAD:(h + 1) * V_HEAD])
                dk_s[ks, hs] += _dot(dst, q_ref[qs, hs])
                dq_ref[hs, cols] += _dot(kt_ref[hs, ks], dst)

            keep = _keep_t(ks.stop - ks.start, tq - first_query, 0) if masked else None
            _bwd_heads(q_ref, k_ref, v_ref, do_ref, lse_ref.at[0], dl_ref.at[0], st_s, dpt_s, keep, use, HEAD_GROUP, qs, ks)

        @pl.when(j >= i * ratio)
        def _():
            dk_s[...] = jnp.zeros(dk_s.shape, F32)
            dv_s[...] = jnp.zeros(dv_s.shape, F32)

        sub = min(tk, ATTN_DIAG_KEYS_BWD)
        for part in range(ratio):
            @pl.when(j == i * ratio + part)
            def _():
                for first in range(0, tk, sub):
                    heads(slice(first, first + sub), part * tk + first, True)

        @pl.when(j < i * ratio)
        def _():
            heads(slice(0, tk), 0, False)

        @pl.when(i == nq - 1)
        def _():
            dk_ref[...] = (dk_s[...] * (1.0 / LOG2E)).astype(dk_ref.dtype)
            for h in range(HEAD_GROUP):
                dv_ref[:, h * V_HEAD:(h + 1) * V_HEAD] = dv_s[h].astype(dv_ref.dtype)

    at_q = lambda g, s, qi, kj: (qi[s], g)
    at_k = lambda g, s, qi, kj: (kj[s], g)
    at_kt = lambda g, s, qi, kj: (g, kj[s])
    at_stat = lambda g, s, qi, kj: (g, 0, qi[s])
    in_specs = [pl.BlockSpec((tq, gq), at_q), pl.BlockSpec((tk, gq), at_k), pl.BlockSpec((gq, tk), at_kt),
                pl.BlockSpec((tk, gv), at_k), pl.BlockSpec((tq, gv), at_q),
                pl.BlockSpec((1, HEAD_GROUP, tq), at_stat), pl.BlockSpec((1, HEAD_GROUP, tq), at_stat)]
    out_specs = [pl.BlockSpec((gq, n_tok), lambda g, s, qi, kj: (g, 0), pipeline_mode=pl.Buffered(1)),
                 pl.BlockSpec((tk, gq), at_k), pl.BlockSpec((tk, gv), at_k)]
    out_shape = [_sds(QK_WIDTH, n_tok, F32), _sds(n_tok, QK_WIDTH, MX), _sds(n_tok, N_HEADS * V_HEAD, MX)]
    scratch = [pltpu.VMEM((tk, gq), F32), pltpu.VMEM((HEAD_GROUP, tk, V_HEAD), F32),
               pltpu.VMEM((2, tk, tq), F32), pltpu.VMEM((2, tk, tq), F32)]
    stat3 = lambda a: a.reshape(n_groups, HEAD_GROUP, n_tok)
    operands = [qi, kj, q, k, kt, v, do, stat3(lse), stat3(delta)]
    if scatter is not None:
        operands.append(scatter)
        in_specs.append(_HBM)
        out_specs.append(_HBM)
        out_shape.append(jax.ShapeDtypeStruct((N_DEVICES, scatter.shape[1] // 2, PACK_COLS), scatter.dtype))
        scratch += [pltpu.SemaphoreType.DMA((N_DEVICES - 1,)), pltpu.SemaphoreType.DMA((N_DEVICES - 1,))]
    gs = pltpu.PrefetchScalarGridSpec(num_scalar_prefetch=2, grid=(n_groups, n_pairs), in_specs=in_specs,
                                      out_specs=out_specs, scratch_shapes=scratch)
    return pl.pallas_call(kern, name="attn_bwd", grid_spec=gs, out_shape=out_shape,
                          compiler_params=_params(("arbitrary", "arbitrary")))(*operands)


def _bwd_inproj(dq_t, dk, dv, dd, dgl, ps, x, dh1, cc, sa, w, tm):
    n_tok = x.shape[0]
    n_tiles = n_tok // tm
    last_halo = n_tok // HALO - 1
    halo_spec = pl.BlockSpec((HALO, POOL_WIDTH), lambda i: (jnp.minimum((i + 1) * (tm // HALO), last_halo), 0))

    def body(i, tin, res, tout, acc, scr):
        dq_ref, dk_ref, dv_ref, dd_ref, ddh_ref, dgl_ref, ps_ref, x_ref, dh1_ref, c_ref, s_ref = tin
        w_uq, g_q, w_k, w_v, e_mat, g_kv, w_in, g_pre = res
        dqu_o, dproj_o, dx_o = tout
        dgq_a, dgkv_a, dgpre_a = acc
        ext, = scr
        cc_, sa_ = c_ref[...], s_ref[...]
        for h in range(N_HEADS):
            hs = slice(h * HEAD_SLOT, (h + 1) * HEAD_SLOT)
            dqu_o[:, hs] = (_unrope(dq_ref[hs, :].T, cc_, sa_) * SCALE).astype(dqu_o.dtype)
        gq = g_q[...]
        _, qh, rq = _rms(ps_ref[:, :Q_LORA].astype(F32), gq)
        dqd, dg = _rms_bwd(qh, rq, gq, _dot_nt(dqu_o[...], w_uq[...]))
        dgq_a[...] += dg
        dproj_o[:, :Q_LORA] = dqd.astype(dproj_o.dtype)
        gkv = g_kv[...]
        _, kh, rk = _rms(ps_ref[:, Q_LORA:Q_LORA + KV_LORA].astype(F32), gkv)
        dkvd, dg = _rms_bwd(kh, rk, gkv, _dot_nt(dk_ref[...], w_k[...]) + _dot_nt(dv_ref[...], w_v[...]))
        dgkv_a[...] += dg
        dproj_o[:, Q_LORA:Q_LORA + KV_LORA] = dkvd.astype(dproj_o.dtype)
        dproj_o[:, Q_LORA + KV_LORA:SMALL_COLS] = _unrope(_dot_nt(dk_ref[...], e_mat[...]), cc_, sa_).astype(dproj_o.dtype)
        row = i * tm + lax.broadcasted_iota(jnp.int32, (tm + HALO, 1), 0)
        for gi, wdw in enumerate(POOL_WINDOWS):
            cs = slice(gi * POOL_GROUP, (gi + 1) * POOL_GROUP)
            inv = 1.0 / jnp.minimum(row + 1, wdw).astype(F32)
            ext[pl.ds(0, tm), cs] = dd_ref[:, cs] * inv[:tm]
            ext[pl.ds(tm, HALO), cs] = jnp.where(i < n_tiles - 1, ddh_ref[:, cs] * inv[tm:], 0.0)
            s = ext[pl.ds(0, tm), cs]
            for k_ in range(1, wdw):
                s = s + ext[pl.ds(k_, tm), cs]
            dproj_o[:, SMALL_COLS + gi * POOL_GROUP:SMALL_COLS + (gi + 1) * POOL_GROUP] = (s - dd_ref[:, cs]).astype(dproj_o.dtype)
        dproj_o[:, SMALL_COLS + POOL_WIDTH:] = dgl_ref[...]
        da = _dot_nt(dproj_o[...], w_in[...])
        gp = g_pre[...]
        _, xh, rx = _rms(x_ref[...], gp)
        dx, dg = _rms_bwd(xh, rx, gp, da)
        dgpre_a[...] += dg
        dx_o[...] = dh1_ref[...] + dx

    outs = [_sds(n_tok, QK_WIDTH, MX), _sds(n_tok, IN_PAD, MX), _sds(n_tok, D_MODEL, F32)]
    accs = [_sds(1, Q_LORA, F32), _sds(1, KV_LORA, F32), _sds(1, D_MODEL, F32)]
    res = [w["w_uq"], w["g_q"], w["w_k"], w["w_v"], w["e_mat"], w["g_kv"], w["w_in"], w["g_pre_mix"]]
    dq_spec = pl.BlockSpec((QK_WIDTH, tm), lambda i: (0, i))
    return _tok_call("bwd_inproj", body, n_tok, tm, [(dq_t, dq_spec), dk, dv, dd, (dd, halo_spec), dgl, ps, x, dh1, cc, sa], res, outs, accs,
                     scratch=[pltpu.VMEM((tm + HALO, POOL_WIDTH), F32)])


XTDY_TOKENS = 1024
XTDY_OUT_BYTES = 8 * 2**20
XTDY_IN_BYTES = 8 * 2**20


def _xtdy(name, x, dy, allreduce=None, square_x=False):
    n_tok, kk = x.shape
    nn = dy.shape[1]
    bk = kk
    while bk * nn * 4 > XTDY_OUT_BYTES and bk % 256 == 0:
        bk //= 2
    bt = min(XTDY_TOKENS, n_tok)
    while (2 * bt <= n_tok and n_tok % (2 * bt) == 0 and 2 * bt * nn * dy.dtype.itemsize <= XTDY_IN_BYTES
           and 2 * bt * bk * x.dtype.itemsize <= XTDY_IN_BYTES):
        bt *= 2

    grid = (kk // bk, n_tok // bt)

    def kern(x_ref, dy_ref, *rest):
        o_ref = rest[1] if allreduce is not None else rest[0]
        if allreduce is not None:
            g_ref, _, sum_ref, buf, send_sems, recv_sems = rest
            step = pl.program_id(0) * grid[1] + pl.program_id(1)
            _allreduce_steps(step == 0, step == grid[0] * grid[1] - 1, g_ref, sum_ref, buf, send_sems, recv_sems)

        @pl.when(pl.program_id(1) == 0)
        def _():
            o_ref[...] = jnp.zeros(o_ref.shape, F32)
        xv = x_ref[...]
        if square_x:
            xv = xv.astype(F32)
            xv = xv * xv
        o_ref[...] += _dot_tn(xv, dy_ref[...])

    operands = [x, dy]
    in_specs = [pl.BlockSpec((bt, bk), lambda a, t: (t, a)), pl.BlockSpec((bt, nn), lambda a, t: (t, 0))]
    out_specs = [pl.BlockSpec((bk, nn), lambda a, t: (a, 0))]
    out_shape = [_sds(kk, nn, F32)]
    scratch = []
    if allreduce is not None:
        vmem = pl.BlockSpec(memory_space=pltpu.VMEM)
        operands.append(allreduce)
        in_specs.append(vmem)
        out_specs.append(vmem)
        out_shape.append(jax.ShapeDtypeStruct(allreduce.shape, allreduce.dtype))
        scratch = [pltpu.VMEM((N_DEVICES,) + allreduce.shape, allreduce.dtype), pltpu.SemaphoreType.DMA((N_DEVICES - 1,)),
                   pltpu.SemaphoreType.DMA((N_DEVICES - 1,))]
    res = pl.pallas_call(kern, name=name, grid=grid, in_specs=in_specs, out_specs=out_specs, out_shape=out_shape,
                         scratch_shapes=scratch, compiler_params=_params(("arbitrary", "arbitrary")))(*operands)
    return res if allreduce is not None else res[0]


def _rope_tables(positions):
    inv_freq = ROPE_THETA ** (-jnp.arange(0, QK_ROPE, 2, dtype=F32) / QK_ROPE)
    ang_t = inv_freq[:, None] * positions.astype(F32)[None, :]
    cos_t, sin_t = lax.optimization_barrier((jnp.cos(ang_t), jnp.sin(ang_t)))
    cos, sin = cos_t.T, sin_t.T
    n_tok = positions.shape[0]
    ones, z64 = jnp.ones((n_tok, ROPE_LANE), F32), jnp.zeros((n_tok, ROPE_LANE), F32)
    z32 = jnp.zeros((n_tok, HEAD_SLOT - ROPE_LANE - QK_ROPE), F32)
    return jnp.concatenate([ones, cos, cos, z32], 1), jnp.concatenate([z64, -sin, sin, z32], 1)


def _kernel_weights(full):
    w_in, w_uq, w_ukv = full["w_in"], full["w_uq"], full["w_ukv"]
    c0 = Q_LORA + KV_LORA
    z = lambda n: jnp.zeros((D_MODEL, n), w_in.dtype)
    w = dict(full)
    w["w_in"] = jnp.concatenate([w_in[:, :c0], z(ROPE_LANE), w_in[:, c0:c0 + QK_ROPE], z(HEAD_SLOT - ROPE_LANE - QK_ROPE),
                                 w_in[:, c0 + QK_ROPE:]], 1)
    w["w_uq"] = jnp.pad(w_uq.reshape(Q_LORA, N_HEADS, QK_NOPE + QK_ROPE),
                        ((0, 0), (0, 0), (0, HEAD_SLOT - QK_NOPE - QK_ROPE))).reshape(Q_LORA, QK_WIDTH)
    kv = w_ukv.reshape(KV_LORA, N_HEADS, QK_NOPE + V_HEAD)
    w["w_k"] = jnp.pad(kv[:, :, :QK_NOPE], ((0, 0), (0, 0), (0, HEAD_SLOT - QK_NOPE))).reshape(KV_LORA, QK_WIDTH)
    w["w_v"] = kv[:, :, QK_NOPE:].reshape(KV_LORA, N_HEADS * V_HEAD)
    e = np.zeros((HEAD_SLOT, QK_WIDTH), np.float32)
    sel = np.zeros((N_HEADS * V_HEAD, HEAD_SLOT), np.float32)
    for h in range(N_HEADS):
        for r in range(QK_ROPE):
            e[ROPE_LANE + r, h * HEAD_SLOT + ROPE_LANE + r] = 1.0
        sel[h * V_HEAD:(h + 1) * V_HEAD, h] = 1.0
    w["e_mat"] = jnp.asarray(e, MX)
    w["w_kt"], w["e_t"] = w["w_k"].T, jnp.asarray(e.T, MX)
    pad = ((0, 0), (0, V_ROWS - V_HEAD), (0, 0))
    w["w_vt"] = jnp.pad(w["w_v"].T.reshape(N_HEADS, V_HEAD, KV_LORA), pad).reshape(N_HEADS * V_ROWS, KV_LORA)
    ones = np.zeros((N_HEADS, V_ROWS, 1), np.float32)
    ones[:, V_HEAD] = 1.0
    w["v_ones"] = jnp.asarray(ones.reshape(N_HEADS * V_ROWS, 1))
    w["head_sel"] = jnp.asarray(sel, MX)
    w["w_pool"] = full["w_pool"].astype(MX)
    return w


def _local_step(x, p, positions, target, full, mesh_place=None, packed_rest=None):
    n_tok = x.shape[0]
    tm = tm_mlp = min(TOKEN_TILE, n_tok)
    fwd_tile = [min(t, n_tok) for t in ATTN_FWD_TILE]
    bwd_tile = [min(t, n_tok) for t in ATTN_BWD_TILE]
    w = _kernel_weights(full)
    cc, sa = _rope_tables(positions)

    if mesh_place is None:
        a, ps, u, gl, qn, kvn, q, k, v, kt, vt = _fwd_inproj(x, cc, sa, w, tm)
    else:
        my_chip, core = mesh_place
        a, ps, u, gl, qn, kvn, q, k, v, kt, vt, gathered = _fwd_inproj(x, cc, sa, w, tm, gather=packed_rest)
        w.update(_unpack_full(gathered, packed_rest, my_chip, REST))
    attn, lse = _attn_fwd(q, k, vt, *fwd_tile)
    d, pooled, a_br, p_br, merged, y, h1 = _fwd_mix(x, u, gl, attn, w, tm)
    m, zr, f, h2 = _fwd_mlp(h1, w, tm_mlp)
    dh2, de, dzg, loss_cols, dg_ple = _ple_fwd_bwd(h2, p, target, w, tm)
    df, dz, dh1, dg_post_mlp, dg_pre_mlp = _bwd_mlp(dh2, f, h1, zr, w, tm_mlp)
    dy, da_br, dp_br, dgl, do, delta, dyp, dd, dg_post_mix, db_gate, dpool_scale = _bwd_mix(dh1, y, a_br, p_br, gl, attn, d, w, tm)
    grads = {"w_branch_attn": _xtdy("dw_ba", attn, da_br), "w_branch_pool": _xtdy("dw_bp", pooled, dp_br),
             "w_out": _xtdy("dw_out", merged, dy), "w_ff1": _xtdy("dw_ff1", m, dz), "w_ff2": _xtdy("dw_ff2", zr, df, square_x=True),
             "w_ple_proj": _xtdy("dw_pe", p, de), "w_ple_gate": _xtdy("dw_pg", h2, dzg)}
    delta_t = delta[:, :N_HEADS].T
    if mesh_place is None:
        travelling = None
        dq_t, dk, dv = _attn_bwd(q, k, kt, v, do, lse, delta_t, *bwd_tile)
    else:
        pieces = _pack_pieces(grads, REST, WIRE)
        dq_t, dk, dv, received = _attn_bwd(q, k, kt, v, do, lse, delta_t, *bwd_tile, scatter=pieces)
        travelling = (pieces, received)
        grads = {}
    dqu, dproj, dx, dg_q, dg_kv, dg_pre_mix = _bwd_inproj(dq_t, dk, dv, dd, dgl, ps, x, dh1, cc, sa, w, tm)

    g_uq = _xtdy("dw_uq", qn, dqu)
    g_k = _xtdy("dw_k", kvn, dk)
    g_v = _xtdy("dw_v", kvn, dv)
    g_pool = _xtdy("dw_pool", d, dyp)
    small = {"g_pre_mix": dg_pre_mix, "b_gate": db_gate, "g_q": dg_q, "g_kv": dg_kv, "pool_scale": dpool_scale,
             "g_post_mix": dg_post_mix, "g_pre_mlp": dg_pre_mlp, "g_post_mlp": dg_post_mlp, "g_ple": dg_ple,
             "w_pool": jnp.stack([g_pool[g * POOL_GROUP:(g + 1) * POOL_GROUP, g * POOL_GROUP:(g + 1) * POOL_GROUP]
                                  for g in range(len(POOL_WINDOWS))])}
    if mesh_place is None:
        g_in = _xtdy("dw_in", a, dproj)
    else:
        g_in, small_sum = _xtdy("dw_in", a, dproj, allreduce=_pack_small(small, loss_cols))
        small, loss_cols = _unpack_small(small_sum), small_sum[-1:]

    c0 = Q_LORA + KV_LORA
    grads.update(small)
    grads.update({
        "w_in": jnp.concatenate([g_in[:, :c0], g_in[:, c0 + ROPE_LANE:c0 + ROPE_LANE + QK_ROPE], g_in[:, SMALL_COLS:]], 1),
        "w_uq": g_uq.reshape(Q_LORA, N_HEADS, HEAD_SLOT)[:, :, :QK_NOPE + QK_ROPE].reshape(Q_LORA, N_HEADS * (QK_NOPE + QK_ROPE)),
        "w_ukv": jnp.concatenate([g_k.reshape(KV_LORA, N_HEADS, HEAD_SLOT)[:, :, :QK_NOPE],
                                  g_v.reshape(KV_LORA, N_HEADS, V_HEAD)], 2).reshape(KV_LORA, N_HEADS * (QK_NOPE + V_HEAD)),
    })
    return loss_cols, dx, grads, travelling


def _place():
    return lax.axis_index("x"), lax.axis_index("y"), lax.axis_index("c")


CHIP_FLIPS = ((1, 0), (0, 1), (1, 1))


def _flip(x, y, fx, fy):
    return (1 - x if fx else x), (1 - y if fy else y)


_HBM = pl.BlockSpec(memory_space=pl.ANY)


def _gather_copies(w_ref, out_ref, send_sems, recv_sems):
    half = w_ref.shape[0] // 2
    x, y, c = _place()
    my_chip = 2 * x + y
    sibling = (x, y, 1 - c)

    def half_of(chip, hc):
        return out_ref.at[chip, pl.ds(pl.multiple_of(hc * half, 16), half), :]

    src = w_ref.at[pl.ds(pl.multiple_of(c * half, 16), half), :]
    sends, landed, forwards, from_sibling = [], [], [], []
    for j, (fx, fy) in enumerate(CHIP_FLIPS):
        px, py = _flip(x, y, fx, fy)
        mine_there, theirs_here, theirs_other = half_of(my_chip, c), half_of(2 * px + py, c), half_of(2 * px + py, 1 - c)
        sends.append(pltpu.make_async_remote_copy(src, mine_there, send_sems.at[j], recv_sems.at[j],
                                                  device_id=(px, py, c), device_id_type=MESH))
        landed.append(pltpu.make_async_remote_copy(src, theirs_here, send_sems.at[j], recv_sems.at[j],
                                                   device_id=(px, py, c), device_id_type=MESH))
        forwards.append(pltpu.make_async_remote_copy(theirs_here, theirs_here, send_sems.at[3 + j], recv_sems.at[3 + j],
                                                     device_id=sibling, device_id_type=MESH))
        from_sibling.append(pltpu.make_async_remote_copy(theirs_other, theirs_other, send_sems.at[3 + j],
                                                         recv_sems.at[3 + j], device_id=sibling, device_id_type=MESH))
    return sends, landed, forwards, from_sibling


def _gather_steps(i, n_steps, w_ref, out_ref, sems):
    sends, landed, forwards, from_sibling = _gather_copies(w_ref, out_ref, *sems)

    @pl.when(i == 0)
    def _():
        for cp in sends:
            cp.start()

    @pl.when(i == (3 * n_steps) // 4)
    def _():
        for arrived, fwd in zip(landed, forwards):
            arrived.wait_recv()
            fwd.start()

    @pl.when(i == n_steps - 1)
    def _():
        for cp in from_sibling:
            cp.wait_recv()
        for cp in sends + forwards:
            cp.wait_send()


def _allgather_shards(wp):
    def body(w_ref, out_ref, send_sems, recv_sems):
        sends, landed, forwards, from_sibling = _gather_copies(w_ref, out_ref, send_sems, recv_sems)
        for cp in sends:
            cp.start()
        for arrived, fwd in zip(landed, forwards):
            arrived.wait_recv()
            fwd.start()
        for cp in from_sibling:
            cp.wait_recv()
        for cp in sends + forwards:
            cp.wait_send()

    return pl.pallas_call(
        body, name="allgather_shards", out_shape=jax.ShapeDtypeStruct((N_CHIPS,) + wp.shape, wp.dtype),
        in_specs=[_HBM], out_specs=_HBM,
        scratch_shapes=[pltpu.SemaphoreType.DMA((6,)), pltpu.SemaphoreType.DMA((6,))],
    )(wp)


def _exchange_halves(g):
    rows = g.shape[1]
    half = rows // 2

    def body(g_ref, r_ref, send_sem, recv_sem):
        x, y, c = _place()
        src = g_ref.at[:, pl.ds(pl.multiple_of((1 - c) * half, 8), half), :]
        cp = pltpu.make_async_remote_copy(src, r_ref, send_sem, recv_sem, device_id=(x, y, 1 - c), device_id_type=MESH)
        cp.start()
        cp.wait()

    return pl.pallas_call(
        body, name="exchange_halves", out_shape=jax.ShapeDtypeStruct((N_CHIPS, half, PACK_COLS), g.dtype),
        in_specs=[_HBM], out_specs=_HBM, scratch_shapes=[pltpu.SemaphoreType.DMA, pltpu.SemaphoreType.DMA],
    )(g)


def _add_halves(g, r, c):
    rows = g.shape[1]
    half = rows // 2
    br = REDUCE_ROWS
    nb = half // br

    def kern(c_ref, g_ref, r_ref, o_ref):
        o_ref[...] = (g_ref[...] + r_ref[...]).astype(o_ref.dtype)

    gs = pltpu.PrefetchScalarGridSpec(
        num_scalar_prefetch=1, grid=(N_CHIPS, nb),
        in_specs=[pl.BlockSpec((1, br, PACK_COLS), lambda k, t, c: (k, c[0] * nb + t, 0)),
                  pl.BlockSpec((1, br, PACK_COLS), lambda k, t, c: (k, t, 0))],
        out_specs=pl.BlockSpec((1, br, PACK_COLS), lambda k, t, c: (k, t, 0)))
    return pl.pallas_call(kern, name="add_halves", grid_spec=gs,
                          out_shape=jax.ShapeDtypeStruct((N_CHIPS, half, PACK_COLS), WIRE),
                          compiler_params=_params(("arbitrary", "arbitrary")))(c.reshape(1), g, r)


def _scatter_copies(s_ref, r_ref, send_sems, recv_sems):
    x, y, c = _place()
    my_chip = 2 * x + y
    sends, arrivals = [], []
    for j, (fx, fy) in enumerate(CHIP_FLIPS):
        px, py = _flip(x, y, fx, fy)
        slot = r_ref.at[2 * px + py]
        sends.append(pltpu.make_async_remote_copy(s_ref.at[2 * px + py], r_ref.at[my_chip], send_sems.at[j], recv_sems.at[j],
                                                  device_id=(px, py, c), device_id_type=MESH))
        arrivals.append(pltpu.make_async_remote_copy(slot, slot, send_sems.at[j], recv_sems.at[j],
                                                     device_id=(px, py, c), device_id_type=MESH))
    return sends, arrivals


N_DEVICES = 8


def _peer(x, y, c, f):
    px, py = _flip(x, y, f & 4, f & 2)
    return px, py, (1 - c if f & 1 else c)


def _scatter_all_copies(p_ref, r_ref, send_sems, recv_sems):
    half = p_ref.shape[1] // 2
    x, y, c = _place()
    me = 4 * x + 2 * y + c
    sends, arrivals = [], []
    for f in range(1, N_DEVICES):
        px, py, pc = _peer(x, y, c, f)
        theirs = p_ref.at[2 * px + py, pl.ds(pl.multiple_of(pc * half, 16), half), :]
        slot = r_ref.at[4 * px + 2 * py + pc]
        sends.append(pltpu.make_async_remote_copy(theirs, r_ref.at[me], send_sems.at[f - 1], recv_sems.at[f - 1],
                                                  device_id=(px, py, pc), device_id_type=MESH))
        arrivals.append(pltpu.make_async_remote_copy(slot, slot, send_sems.at[f - 1], recv_sems.at[f - 1],
                                                     device_id=(px, py, pc), device_id_type=MESH))
    return sends, arrivals


def _scatter_steps(first, last, p_ref, r_ref, sems):
    sends, arrivals = _scatter_all_copies(p_ref, r_ref, *sems)

    @pl.when(first)
    def _():
        for cp in sends:
            cp.start()

    @pl.when(last)
    def _():
        for cp in arrivals:
            cp.wait_recv()
        for cp in sends:
            cp.wait_send()


def _scatter_pieces(s):
    def body(s_ref, r_ref, send_sems, recv_sems):
        sends, arrivals = _scatter_copies(s_ref, r_ref, send_sems, recv_sems)
        for cp in sends:
            cp.start()
        for cp in arrivals:
            cp.wait_recv()
        for cp in sends:
            cp.wait_send()

    return pl.pallas_call(
        body, name="scatter_pieces", out_shape=jax.ShapeDtypeStruct(s.shape, s.dtype), in_specs=[_HBM], out_specs=_HBM,
        scratch_shapes=[pltpu.SemaphoreType.DMA((3,)), pltpu.SemaphoreType.DMA((3,))],
    )(s)


def _sum_pieces(r, mine, slot):
    slots, half = r.shape[:2]
    br = REDUCE_ROWS

    def kern(slot_ref, r_ref, m_ref, o_ref):
        total = None
        for k in range(slots):
            term = jnp.where(slot_ref[0] == k, m_ref[0], r_ref[k]).astype(F32)
            total = term if total is None else total + term
        o_ref[...] = total

    gs = pltpu.PrefetchScalarGridSpec(
        num_scalar_prefetch=1, grid=(half // br,),
        in_specs=[pl.BlockSpec((slots, br, PACK_COLS), lambda t, s: (0, t, 0)),
                  pl.BlockSpec((1, br, PACK_COLS), lambda t, s: (0, t, 0))],
        out_specs=pl.BlockSpec((br, PACK_COLS), lambda t, s: (t, 0)))
    return pl.pallas_call(kern, name="sum_pieces", grid_spec=gs, out_shape=_sds(half, PACK_COLS, F32),
                          compiler_params=_params(("arbitrary",)))(slot.reshape(1), r, mine)


def _join_halves(f):
    def body(f_ref, o_ref, send_sem, recv_sem):
        x, y, c = _place()
        cp = pltpu.make_async_remote_copy(f_ref, o_ref, send_sem, recv_sem, device_id=(x, y, 1 - c), device_id_type=MESH)
        cp.start()
        cp.wait()

    return pl.pallas_call(
        body, name="join_halves", out_shape=jax.ShapeDtypeStruct(f.shape, f.dtype), in_specs=[_HBM], out_specs=_HBM,
        scratch_shapes=[pltpu.SemaphoreType.DMA, pltpu.SemaphoreType.DMA],
    )(f)


def _allreduce_steps(first, last, g_ref, o_ref, buf, send_sems, recv_sems):
    x, y, c = _place()
    me = 4 * x + 2 * y + c
    sends, arrivals = [], []
    for f in range(1, N_DEVICES):
        px, py, pc = _peer(x, y, c, f)
        slot = buf.at[4 * px + 2 * py + pc]
        sends.append(pltpu.make_async_remote_copy(g_ref, buf.at[me], send_sems.at[f - 1], recv_sems.at[f - 1],
                                                  device_id=(px, py, pc), device_id_type=MESH))
        arrivals.append(pltpu.make_async_remote_copy(slot, slot, send_sems.at[f - 1], recv_sems.at[f - 1],
                                                     device_id=(px, py, pc), device_id_type=MESH))

    @pl.when(first)
    def _():
        buf[me] = g_ref[...]
        for cp in sends:
            cp.start()

    @pl.when(last)
    def _():
        for cp in arrivals:
            cp.wait_recv()
        for cp in sends:
            cp.wait_send()
        total = buf[0]
        for k in range(1, N_DEVICES):
            total = total + buf[k]
        o_ref[...] = total


def _adamw_update(g_ref, w_ref, m_ref, v_ref, d_o, m_o, v_o):
    c1 = 1.0 - ADAM_B1 ** ADAM_STEP
    c2 = 1.0 - ADAM_B2 ** ADAM_STEP
    g_ = g_ref[...]
    m_new = ADAM_B1 * m_ref[...] + (1.0 - ADAM_B1) * g_
    v_new = ADAM_B2 * v_ref[...] + (1.0 - ADAM_B2) * (g_ * g_)
    m_o[...] = m_new
    v_o[...] = v_new
    d_o[...] = -ADAM_LR * ((m_new / c1) / (jnp.sqrt(v_new / c2) + ADAM_EPS) + ADAM_WD * w_ref[...])


ADAMW_ROWS = 256


def _adamw(name, g, w, m, v):
    _, rows, cols = w.shape
    br = int(np.gcd(ADAMW_ROWS, rows))

    def kern(*refs):
        _adamw_update(*refs)

    spec = pl.BlockSpec((1, br, cols), lambda t: (0, t, 0))
    out = jax.ShapeDtypeStruct(w.shape, F32)
    return pl.pallas_call(kern, name="adamw_" + name, grid=(rows // br,), in_specs=[spec] * 4, out_specs=[spec] * 3,
                          out_shape=[out, out, out], compiler_params=_params(("arbitrary",)))(g, w, m, v)


def _adamw_small(gs, ws, ms, vs):
    n = len(gs)

    def kern(*refs):
        ins, outs = refs[:4 * n], refs[4 * n:]
        for k in range(n):
            _adamw_update(ins[k], ins[n + k], ins[2 * n + k], ins[3 * n + k], outs[k], outs[n + k], outs[2 * n + k])

    vmem = pl.BlockSpec(memory_space=pltpu.VMEM)
    out = [jax.ShapeDtypeStruct(w.shape, F32) for w in ws]
    res = pl.pallas_call(kern, name="adamw_small", in_specs=[vmem] * (4 * n), out_specs=[vmem] * (3 * n),
                         out_shape=out * 3, compiler_params=pltpu.CompilerParams(vmem_limit_bytes=VMEM_LIMIT))(*gs, *ws, *ms, *vs)
    return [(res[k], res[n + k], res[2 * n + k]) for k in range(n)]


def _shard_rows(shape, axis):
    k, n = shape
    return (k * n // N_CHIPS) // PACK_COLS


def _group(names):
    entries = [e for e in SHARDED if e[0] in names]
    used = sum(_shard_rows(shape, axis) for _, shape, axis in entries)
    return entries, -(-used // (2 * REDUCE_ROWS)) * 2 * REDUCE_ROWS


def _pack_shards(shards, names, dtype):
    entries, rows = _group(names)
    parts = [shards[name].astype(dtype).reshape(-1, PACK_COLS) for name, _, _ in entries]
    used = sum(p.shape[0] for p in parts)
    if rows > used:
        parts.append(jnp.zeros((rows - used, PACK_COLS), dtype))
    return jnp.concatenate(parts, 0)


def _unpack_shards(packed, names):
    out, r0 = {}, 0
    for name, (k, n), axis in _group(names)[0]:
        nr = _shard_rows((k, n), axis)
        shape = (k // N_CHIPS, n) if axis == 0 else (k, n // N_CHIPS)
        out[name] = packed[r0:r0 + nr].reshape(shape)
        r0 += nr
    return out


def _unpack_full(gathered, own, my_chip, names):
    chip = lax.broadcasted_iota(jnp.int32, (N_CHIPS, 1, 1), 0)
    gathered = jnp.where(chip == my_chip, own[None], gathered)
    out, r0 = {}, 0
    for name, (k, n), axis in _group(names)[0]:
        nr = _shard_rows((k, n), axis)
        part = gathered[:, r0:r0 + nr]
        if axis == 0:
            out[name] = part.reshape(k, n)
        else:
            out[name] = part.reshape(N_CHIPS, k, n // N_CHIPS).transpose(1, 0, 2).reshape(k, n)
        r0 += nr
    return out


def _pack_pieces(grads, names, dtype=F32):
    entries, rows = _group(names)
    parts = []
    for name, (k, n), axis in entries:
        g = grads[name].astype(dtype)
        if axis == 0:
            parts.append(g.reshape(N_CHIPS, -1, PACK_COLS))
        else:
            parts.append(g.reshape(k, N_CHIPS, n // N_CHIPS).transpose(1, 0, 2).reshape(N_CHIPS, -1, PACK_COLS))
    used = sum(p.shape[1] for p in parts)
    if rows > used:
        parts.append(jnp.zeros((N_CHIPS, rows - used, PACK_COLS), dtype))
    return jnp.concatenate(parts, 1)


def _pack_small(vals, last_row):
    flat = jnp.concatenate([vals[name].astype(F32).reshape(-1) for name, _ in SMALL])
    spare = jnp.zeros(((SMALL_ROWS - 1) * PACK_COLS - flat.shape[0],), F32)
    return jnp.concatenate([flat, spare, last_row.reshape(-1)]).reshape(SMALL_ROWS, PACK_COLS)


def _unpack_small(packed):
    flat, out, o = packed.reshape(-1), {}, 0
    for name, shape in SMALL:
        n = int(np.prod(shape))
        out[name] = flat[o:o + n].reshape(shape)
        o += n
    return out


def kernel(x, p, positions, g_pre_mix, w_in, b_gate, g_q, w_uq, g_kv, w_ukv, w_pool, pool_scale, w_branch_attn, w_branch_pool, w_out, g_post_mix, g_pre_mlp, w_ff1, w_ff2, g_post_mlp, w_ple_proj, w_ple_gate, g_ple, loss_target, m_g_pre_mix, m_w_in, m_b_gate, m_g_q, m_w_uq, m_g_kv, m_w_ukv, m_w_pool, m_pool_scale, m_w_branch_attn, m_w_branch_pool, m_w_out, m_g_post_mix, m_g_pre_mlp, m_w_ff1, m_w_ff2, m_g_post_mlp, m_w_ple_proj, m_w_ple_gate, m_g_ple, v_g_pre_mix, v_w_in, v_b_gate, v_g_q, v_w_uq, v_g_kv, v_w_ukv, v_w_pool, v_pool_scale, v_w_branch_attn, v_w_branch_pool, v_w_out, v_g_post_mix, v_g_pre_mlp, v_w_ff1, v_w_ff2, v_g_post_mlp, v_w_ple_proj, v_w_ple_gate, v_g_ple):
    given = dict(locals())
    weights = {n: given[n] for n in WEIGHT_ORDER}
    moments_m = {n: given["m_" + n] for n in WEIGHT_ORDER}
    moments_v = {n: given["v_" + n] for n in WEIGHT_ORDER}
    c = lax.axis_index("c")

    big_w = {name: weights[name][0] for name, _, _ in SHARDED}
    my_chip = 2 * lax.axis_index("x") + lax.axis_index("y")
    packed_first = _pack_shards(big_w, FIRST, MX)
    full = _unpack_full(_allgather_shards(packed_first), packed_first, my_chip, FIRST)
    for name, _ in SMALL:
        full[name] = weights[name][0] if name == "w_pool" else weights[name]

    loss_cols, dx, grads, (pieces_rest, received_rest) = _local_step(
        x[0], p[0, 0], positions[0], loss_target[0], full, (my_chip, c), _pack_shards(big_w, REST, MX))
    loss = 0.5 * jnp.sum(loss_cols) / D_MODEL

    def finish(received, mine, slot):
        reduced = _sum_pieces(received, mine, slot)
        theirs = _join_halves(reduced)
        return jnp.where(c == 0, jnp.concatenate([reduced, theirs]), jnp.concatenate([theirs, reduced]))

    pieces = _pack_pieces(grads, FIRST)
    sent = _add_halves(pieces, _exchange_halves(pieces), c)
    mine = lax.dynamic_slice(sent, (my_chip, 0, 0), (1,) + sent.shape[1:])
    shards = _unpack_shards(finish(_scatter_pieces(sent), mine, my_chip), FIRST)
    half = received_rest.shape[1]
    mine = lax.dynamic_slice(pieces_rest, (my_chip, c * half, 0), (1, half, PACK_COLS))
    shards.update(_unpack_shards(finish(received_rest, mine, 2 * my_chip + c), REST))

    out = {}
    for name, g in shards.items():
        out[name] = (g[None], *_adamw(name, g[None], weights[name], moments_m[name], moments_v[name]))
    small_g = {n: grads[n] for n, _ in SMALL}
    names = [n for n, _ in SMALL]
    updates = _adamw_small([small_g[n] for n in names], [weights[n] for n in names], [moments_m[n] for n in names],
                           [moments_v[n] for n in names])
    for n, upd in zip(names, updates):
        out[n] = (small_g[n], *upd)
    return (loss, dx[None], *[out[n][k] for k in range(4) for n in WEIGHT_ORDER])
```
